```python
import jax, jax.numpy as jnp
from jax import lax
import numpy as np

D_MODEL = 1024
BATCH = 8
SEQ = 2048
DEPTH = 2

GRID_W = 64
CTX_LEN = 256
HEAD_DIM = 64
EPS = 1e-6
A_WIDTH = D_MODEL // 4
A_HEADS = A_WIDTH // HEAD_DIM
A_CHUNK = 128
B_WIDTH = D_MODEL // 2
B_HEADS = B_WIDTH // HEAD_DIM
B_KV_HEADS = 2
B_GROUP = B_HEADS // B_KV_HEADS
B_KV_WIDTH = B_KV_HEADS * HEAD_DIM
Q_BLOCK = 128
ROPE_AXIS_DIM = HEAD_DIM // 2
ROPE_BASE = 10000.0
C_WIDTH = D_MODEL // 4
C_HEADS = C_WIDTH // HEAD_DIM
C_CHUNK = 64
MIX_WIDTH = A_WIDTH + B_WIDTH + C_WIDTH
OFF_B = 2 * A_WIDTH
OFF_KV = OFF_B + B_WIDTH
OFF_C = OFF_KV + 2 * B_KV_WIDTH
OFF_G = OFF_C + 4 * C_WIDTH
IN_WIDTH = OFF_G + C_WIDTH
N_GROUPS = 4
EXPERTS_PER_GROUP = 8
N_EXPERTS = N_GROUPS * EXPERTS_PER_GROUP
TOP_K = 2
D_FF_EXPERT = D_MODEL // 2
MOE_BLOCK = 128

kernel_name = 'hybrid_dit_gmlp_gqa_hgrn2_hmoe'


def _rms(x):
    xf = x.astype(jnp.float32)
    return (xf * lax.rsqrt(jnp.mean(xf * xf, axis=-1, keepdims=True) + EPS)).astype(x.dtype)


def axial_rope_tables(rows):
    row = jnp.repeat(jnp.arange(rows), GRID_W).astype(jnp.float32)
    col = jnp.tile(jnp.arange(GRID_W), rows).astype(jnp.float32)
    inv_freq = 1.0 / (ROPE_BASE ** (jnp.arange(0, ROPE_AXIS_DIM, 2, dtype=jnp.float32) / ROPE_AXIS_DIM))
    ang = jnp.stack([row[:, None] * inv_freq, col[:, None] * inv_freq], axis=1)
    return jnp.cos(ang)[:, None, :, None, :], jnp.sin(ang)[:, None, :, None, :]


def apply_rope(x, cos, sin):
    b, t, h, _ = x.shape
    xr = x.astype(jnp.float32).reshape(b, t, h, 2, 2, ROPE_AXIS_DIM // 2)
    rot = jnp.stack([-xr[..., 1, :], xr[..., 0, :]], axis=-2)
    return (xr * cos + rot * sin).reshape(b, t, h, HEAD_DIM).astype(x.dtype)


def chunk_mlp(z, w_s, b_s):
    b, t, _ = z.shape
    u, v = jnp.split(jax.nn.gelu(z), 2, axis=-1)
    v = _rms(v.reshape(b, t // A_CHUNK, A_CHUNK, A_HEADS, HEAD_DIM))
    mixed = jnp.einsum('hts,bnshd->bnthd', w_s, v) + b_s.T[:, :, None]
    return u * mixed.reshape(b, t, A_WIDTH)


def _gqa(q, k, v):
    s = jnp.einsum('bqkgd,bskd->bkgqs', q, k).astype(jnp.float32) * (HEAD_DIM ** -0.5)
    p = jax.nn.softmax(s, axis=-1).astype(v.dtype)
    return jnp.einsum('bkgqs,bskd->bqkgd', p, v)


def latent_attention(q, k, v, kc, vc):
    b, t, _, _ = q.shape
    keys = jnp.concatenate([k, kc], axis=1)
    vals = jnp.concatenate([v, vc], axis=1)
    qb = q.reshape(b, t // Q_BLOCK, Q_BLOCK, B_KV_HEADS, B_GROUP, HEAD_DIM).transpose(1, 0, 2, 3, 4, 5)
    out = lax.map(lambda qblk: _gqa(qblk, keys, vals), qb)
    return out.transpose(1, 0, 2, 3, 4, 5).reshape(b, t, B_WIDTH)


def forget_gate(zf, lb):
    zf32 = zf.astype(jnp.float32)
    lbh = lb.reshape(C_HEADS, HEAD_DIM)
    pos = lbh > 0.0
    log_lb = jnp.log(jnp.where(pos, lbh, 1.0))
    log_rest = jnp.log1p(-lbh) + jax.nn.log_sigmoid(zf32)
    log_f = jnp.where(pos, jnp.logaddexp(log_lb, log_rest), log_rest)
    k = (1.0 - lbh) * jax.nn.sigmoid(-zf32)
    return k, log_f


def hgrn_inputs(z, lb):
    b, t, _ = z.shape
    q, zf, zb, v = [a.reshape(b, t, C_HEADS, HEAD_DIM) for a in jnp.split(z, 4, axis=-1)]
    kf, logf_f = forget_gate(zf, lb[0])
    kb, logf_b = forget_gate(zb, lb[1])
    return jax.nn.silu(q), v, kf, logf_f, kb, logf_b


def hgrn_scan(q, k, v, log_f, s0, with_output):
    b, t, h, _ = q.shape
    nc = t // C_CHUNK

    def chunks(a):
        return a.astype(jnp.float32).reshape(b, nc, C_CHUNK, h, a.shape[-1]).transpose(1, 0, 3, 2, 4)

    qc, kc, vc = chunks(q), chunks(k), chunks(v)
    bc = jnp.cumsum(chunks(log_f), axis=3)
    tri = jnp.tril(jnp.ones((C_CHUNK, C_CHUNK), bool))[:, :, None]

    def step(s, inp):
        qi, ki, vi, bi = inp
        b_last = bi[:, :, -1:, :]
        s_new = jnp.exp(b_last[:, :, 0, :, None]) * s + jnp.einsum('bhsk,bhsv->bhkv', ki * jnp.exp(b_last - bi), vi)
        if not with_output:
            return s_new, None
        diff = bi[:, :, :, None, :] - bi[:, :, None, :, :]
        decay = jnp.where(tri, jnp.exp(jnp.where(tri, diff, 0.0)), 0.0)
        scores = jnp.einsum('bhtk,bhsk,bhtsk->bhts', qi, ki, decay)
        o = jnp.einsum('bhts,bhsv->bhtv', scores, vi) + jnp.einsum('bhtk,bhkv->bhtv', qi * jnp.exp(bi), s)
        return s_new, o

    s_fin, o = lax.scan(step, s0, (qc, kc, vc, bc))
    if with_output:
        o = o.transpose(1, 0, 3, 2, 4).reshape(b, t, h, v.shape[-1])
    return s_fin, o


def hgrn_bidir(z, zc, lb, ctx_out):
    q, v, kf, lf, kb, lbw = hgrn_inputs(z, lb)
    qc, vc, kfc, lfc, kbc, lbc = hgrn_inputs(zc, lb)
    s0 = jnp.zeros((z.shape[0], C_HEADS, HEAD_DIM, HEAD_DIM), jnp.float32)
    rev = lambda a: a[:, ::-1]
    sf, of_c = hgrn_scan(qc, kfc, vc, lfc, s0, ctx_out)
    _, of = hgrn_scan(q, kf, v, lf, sf, True)
    sb, ob_c = hgrn_scan(rev(qc), rev(kbc), rev(vc), rev(lbc), s0, ctx_out)
    _, ob = hgrn_scan(rev(q), rev(kb), rev(v), rev(lbw), sb, True)
    o = of + rev(ob)
    o_c = of_c + rev(ob_c) if ctx_out else None
    return o, o_c


def hgrn_out(o, g, w):
    b, t = g.shape[:2]
    return (_rms(o) * w).reshape(b, t, C_WIDTH).astype(g.dtype) * jax.nn.silu(g)


def hier_moe(h, w_grp, b_grp, w_exp, b_exp, w1, w3, w2):
    n_tok, d = h.shape
    g_logits = (h @ w_grp + b_grp).astype(jnp.float32)
    grp = jnp.argmax(g_logits, axis=-1)
    p_grp = jnp.take_along_axis(jax.nn.softmax(g_logits, axis=-1), grp[:, None], axis=-1)
    e_logits = (h @ w_exp + b_exp).astype(jnp.float32).reshape(n_tok, N_GROUPS, EXPERTS_PER_GROUP)
    e_in = jnp.take_along_axis(e_logits, grp[:, None, None], axis=1)[:, 0]
    top_val, top_idx = lax.top_k(e_in, TOP_K)
    gate = (jax.nn.softmax(top_val, axis=-1) * p_grp).reshape(-1)
    expert = (grp[:, None] * EXPERTS_PER_GROUP + top_idx).reshape(-1)
    tok = jnp.repeat(jnp.arange(n_tok), TOP_K)
    n_asg = n_tok * TOP_K
    order = jnp.argsort(expert)
    e_sorted, tok_sorted, gate_sorted = expert[order], tok[order], gate[order]
    counts = jax.ops.segment_sum(jnp.ones_like(expert), expert, num_segments=N_EXPERTS)
    padded = (counts + MOE_BLOCK - 1) // MOE_BLOCK * MOE_BLOCK
    starts = jnp.cumsum(counts) - counts
    pad_ends = jnp.cumsum(padded)
    pad_starts = pad_ends - padded
    dest = pad_starts[e_sorted] + jnp.arange(n_asg) - starts[e_sorted]
    n_blocks = -(-n_asg // MOE_BLOCK) + N_EXPERTS
    buf = jnp.zeros((n_blocks * MOE_BLOCK, d), h.dtype).at[dest].set(h[tok_sorted])
    blk_expert = jnp.minimum(jnp.searchsorted(pad_ends, jnp.arange(n_blocks) * MOE_BLOCK, side='right'), N_EXPERTS - 1)

    def expert_block(args):
        xb, e = args
        return (jax.nn.silu(xb @ w1[e]) * (xb @ w3[e])) @ w2[e]

    out = lax.map(expert_block, (buf.reshape(n_blocks, MOE_BLOCK, d), blk_expert)).reshape(-1, d)
    y = jax.ops.segment_sum(out[dest] * gate_sorted[:, None], tok_sorted, num_segments=n_tok)
    return y.astype(h.dtype)


def hybrid_layer(x, xc, c, c_ctx, cos, sin, lb, w_mod, b_mod, norm1_w, w_in, w_s, b_s, q_norm_w, k_norm_w,
                 hgrn_norm_w, w_out, norm2_w, w_grp, b_grp, w_exp, b_exp, w1, w3, w2, ctx_out):
    b, t, d = x.shape
    lc = xc.shape[1]
    sh1, sc1, g1, sh2, sc2, g2 = jnp.split((jax.nn.silu(c) @ w_mod + b_mod)[:, None, :], 6, axis=-1)
    n_mod = 6 if ctx_out else 2
    mod_c = jnp.split(jax.nn.silu(c_ctx) @ w_mod[:, :n_mod * d] + b_mod[:n_mod * d], n_mod)
    h = _rms(x) * norm1_w * (1.0 + sc1) + sh1
    hc = _rms(xc) * norm1_w * (1.0 + mod_c[1]) + mod_c[0]
    p = h @ w_in
    pc = hc @ w_in[:, OFF_KV:OFF_G]
    ya = chunk_mlp(p[..., :OFF_B], w_s, b_s)
    q = apply_rope(_rms(p[..., OFF_B:OFF_KV].reshape(b, t, B_HEADS, HEAD_DIM)) * q_norm_w, cos, sin)
    k, v = jnp.split(p[..., OFF_KV:OFF_C].reshape(b, t, 2 * B_KV_HEADS, HEAD_DIM), 2, axis=2)
    k = apply_rope(_rms(k) * k_norm_w, cos, sin)
    kc, vc = jnp.split(pc[..., :2 * B_KV_WIDTH].reshape(b, lc, 2 * B_KV_HEADS, HEAD_DIM), 2, axis=2)
    kc = _rms(kc) * k_norm_w
    yb = latent_attention(q, k, v, kc, vc)
    o, o_c = hgrn_bidir(p[..., OFF_C:OFF_G], pc[..., 2 * B_KV_WIDTH:], lb, ctx_out)
    yc = hgrn_out(o, p[..., OFF_G:], hgrn_norm_w)
    x = x + g1 * (jnp.concatenate([ya, yb, yc], axis=-1) @ w_out)
    h2 = _rms(x) * norm2_w * (1.0 + sc2) + sh2
    if ctx_out:
        pc_rest = hc @ w_in[:, :OFF_KV]
        pc_g = hc @ w_in[:, OFF_G:]
        ya_c = chunk_mlp(pc_rest[..., :OFF_B], w_s, b_s)
        qc = _rms(pc_rest[..., OFF_B:].reshape(b, lc, B_HEADS, HEAD_DIM)) * q_norm_w
        yb_c = _gqa(qc.reshape(b, lc, B_KV_HEADS, B_GROUP, HEAD_DIM), kc, vc).reshape(b, lc, B_WIDTH)
        yc_c = hgrn_out(o_c, pc_g, hgrn_norm_w)
        xc = xc + mod_c[2] * (jnp.concatenate([ya_c, yb_c, yc_c], axis=-1) @ w_out)
        h2c = _rms(xc) * norm2_w * (1.0 + mod_c[4]) + mod_c[3]
        tokens = jnp.concatenate([h2.reshape(-1, d), h2c.reshape(-1, d)], axis=0)
        m = hier_moe(tokens, w_grp, b_grp, w_exp, b_exp, w1, w3, w2)
        x = x + g2 * m[:b * t].reshape(b, t, d)
        xc = xc + mod_c[5] * m[b * t:].reshape(b, lc, d)
    else:
        x = x + g2 * hier_moe(h2.reshape(-1, d), w_grp, b_grp, w_exp, b_exp, w1, w3, w2).reshape(b, t, d)
    return x, xc


def setup_inputs(seed: int = 0) -> dict:
    key = jax.random.key(seed)
    ks = jax.random.split(key, 24)
    nrm = lambda k, shape, scale: jax.random.normal(k, shape, jnp.float32) * scale
    return {
        'x': nrm(ks[0], (BATCH, SEQ, D_MODEL), 1.0),
        'c': nrm(ks[1], (BATCH, D_MODEL), 1.0),
        'ctx': nrm(ks[2], (BATCH, CTX_LEN, D_MODEL), 1.0),
        'c_ctx': nrm(ks[3], (D_MODEL,), 1.0),
        'w_mod': nrm(ks[4], (DEPTH, D_MODEL, 6 * D_MODEL), 0.5 * D_MODEL ** -0.5),
        'b_mod': nrm(ks[5], (DEPTH, 6 * D_MODEL), 0.02),
        'norm1_w': 1.0 + nrm(ks[6], (DEPTH, D_MODEL), 0.02),
        'w_in': nrm(ks[7], (DEPTH, D_MODEL, IN_WIDTH), D_MODEL ** -0.5),
        'w_s': nrm(ks[8], (DEPTH, A_HEADS, A_CHUNK, A_CHUNK), A_CHUNK ** -0.5),
        'b_s': 1.0 + nrm(ks[9], (DEPTH, A_HEADS, A_CHUNK), 0.02),
        'q_norm_w': 1.0 + nrm(ks[10], (DEPTH, HEAD_DIM), 0.02),
        'k_norm_w': 1.0 + nrm(ks[11], (DEPTH, HEAD_DIM), 0.02),
        'hgrn_lb_logits': nrm(ks[12], (DEPTH, 2, C_WIDTH), 1.0),
        'hgrn_norm_w': 1.0 + nrm(ks[13], (DEPTH, HEAD_DIM), 0.02),
        'w_out': nrm(ks[14], (DEPTH, MIX_WIDTH, D_MODEL), MIX_WIDTH ** -0.5),
        'norm2_w': 1.0 + nrm(ks[15], (DEPTH, D_MODEL), 0.02),
        'w_grp': nrm(ks[16], (DEPTH, D_MODEL, N_GROUPS), D_MODEL ** -0.5),
        'b_grp': nrm(ks[17], (DEPTH, N_GROUPS), 0.01),
        'w_exp': nrm(ks[18], (DEPTH, D_MODEL, N_EXPERTS), D_MODEL ** -0.5),
        'b_exp': nrm(ks[19], (DEPTH, N_EXPERTS), 0.01),
        'w1': nrm(ks[20], (DEPTH, N_EXPERTS, D_MODEL, D_FF_EXPERT), D_MODEL ** -0.5),
        'w3': nrm(ks[21], (DEPTH, N_EXPERTS, D_MODEL, D_FF_EXPERT), D_MODEL ** -0.5),
        'w2': nrm(ks[22], (DEPTH, N_EXPERTS, D_FF_EXPERT, D_MODEL), D_FF_EXPERT ** -0.5),
    }


def reference(x, c, ctx, c_ctx, w_mod, b_mod, norm1_w, w_in, w_s, b_s, q_norm_w, k_norm_w, hgrn_lb_logits,
              hgrn_norm_w, w_out, norm2_w, w_grp, b_grp, w_exp, b_exp, w1, w3, w2):
    ROWS = x.shape[1] // GRID_W
    cos, sin = axial_rope_tables(ROWS)
    lb_sm = jax.nn.softmax(hgrn_lb_logits.astype(jnp.float32), axis=0)
    lb = jnp.cumsum(lb_sm, axis=0) - lb_sm[0]
    xc = ctx
    for l in range(DEPTH):
        x, xc = hybrid_layer(x, xc, c, c_ctx, cos, sin, lb[l], w_mod[l], b_mod[l], norm1_w[l], w_in[l], w_s[l],
                             b_s[l], q_norm_w[l], k_norm_w[l], hgrn_norm_w[l], w_out[l], norm2_w[l], w_grp[l],
                             b_grp[l], w_exp[l], b_exp[l], w1[l], w3[l], w2[l], ctx_out=(l < DEPTH - 1))
    return x
```

```python
import functools

import jax
import jax.numpy as jnp
from jax import lax
from jax.experimental import pallas as pl
from jax.experimental.pallas import tpu as pltpu

F32 = jnp.float32
BF16 = jnp.bfloat16

D_MODEL = 1024
HEAD_DIM = 64
GRID_W = 64
EPS = 1e-6
ROPE_BASE = 10000.0
A_WIDTH = D_MODEL // 4
A_HEADS = A_WIDTH // HEAD_DIM
A_CHUNK = 128
B_WIDTH = D_MODEL // 2
B_HEADS = B_WIDTH // HEAD_DIM
B_KV_HEADS = 2
B_GROUP = B_HEADS // B_KV_HEADS
B_KV_WIDTH = B_KV_HEADS * HEAD_DIM
C_WIDTH = D_MODEL // 4
C_HEADS = C_WIDTH // HEAD_DIM
OFF_B = 2 * A_WIDTH
OFF_KV = OFF_B + B_WIDTH
OFF_V = OFF_KV + B_KV_WIDTH
OFF_C = OFF_KV + 2 * B_KV_WIDTH
OFF_G = OFF_C + 4 * C_WIDTH
IN_WIDTH = OFF_G + C_WIDTH
N_GROUPS = 4
EXPERTS_PER_GROUP = 8
N_EXPERTS = N_GROUPS * EXPERTS_PER_GROUP
TOP_K = 2
D_FF_EXPERT = D_MODEL // 2

MOD_ROWS = 16
ROUTER_LANES = 128
HGRN_BLOCK = 16
MOE_ROWS = 256
VMEM_LIMIT = 48 * 1024 * 1024


def _cparams(*sem):
    return pltpu.CompilerParams(dimension_semantics=sem, vmem_limit_bytes=VMEM_LIMIT)


def _head_ones(n, dtype):
    r = lax.broadcasted_iota(jnp.int32, (n, n), 0) >> 6
    c = lax.broadcasted_iota(jnp.int32, (n, n), 1) >> 6
    return (r == c).astype(dtype)


def _head_sum(x, ones_bd):
    hi = x.astype(BF16)
    lo = (x - hi.astype(F32)).astype(BF16)
    return (jnp.dot(hi, ones_bd, preferred_element_type=F32)
            + jnp.dot(lo, ones_bd, preferred_element_type=F32))


def _head_rms(x, ones_bd):
    return x * lax.rsqrt(_head_sum(x * x, ones_bd) * (1.0 / HEAD_DIM) + EPS)


def _mod_kernel(c_ref, w_ref, b_ref, o_ref):
    a = jax.nn.silu(c_ref[...])
    o_ref[...] = jnp.dot(a, w_ref[...], preferred_element_type=F32,
                         precision=lax.Precision.HIGHEST) + b_ref[...]


def _mod(cc, w_mod, b_mod):
    n = w_mod.shape[1]
    tn = 1536
    return pl.pallas_call(
        _mod_kernel,
        out_shape=jax.ShapeDtypeStruct((MOD_ROWS, n), F32),
        grid=(n // tn,),
        in_specs=[pl.BlockSpec((MOD_ROWS, D_MODEL), lambda j: (0, 0)),
                  pl.BlockSpec((D_MODEL, tn), lambda j: (0, j)),
                  pl.BlockSpec((1, tn), lambda j: (0, j))],
        out_specs=pl.BlockSpec((MOD_ROWS, tn), lambda j: (0, j)),
        compiler_params=_cparams("arbitrary"),
        name="mod",
    )(cc, w_mod, b_mod.reshape(1, n))


def _rope(xn, c_ref, sp_ref, sm_ref):
    w = xn.shape[-1]
    return (xn * c_ref[...] + pltpu.roll(xn, 16, 1) * sp_ref[...]
            + pltpu.roll(xn, w - 16, 1) * sm_ref[...])


def _inproj_kernel(x_ref, mul_ref, add_ref, w_ref, qc_ref, qsp_ref, qsm_ref, kc_ref, ksp_ref, ksm_ref,
                   za_ref, q_ref, k_ref, v_ref, zc_ref, g_ref):
    x = x_ref[0]
    ms = jnp.mean(x * x, axis=-1, keepdims=True)
    h = x * lax.rsqrt(ms + EPS) * mul_ref[0] + add_ref[0]
    y = jnp.dot(h.astype(BF16), w_ref[...], preferred_element_type=F32)
    za_ref[0] = y[:, :OFF_B].astype(BF16)
    qn = _head_rms(y[:, OFF_B:OFF_KV], _head_ones(B_WIDTH, BF16))
    q_ref[0] = _rope(qn, qc_ref, qsp_ref, qsm_ref).astype(BF16)
    kn = _head_rms(y[:, OFF_KV:OFF_V], _head_ones(B_KV_WIDTH, BF16))
    k_ref[0] = _rope(kn, kc_ref, ksp_ref, ksm_ref).astype(BF16)
    v_ref[0] = y[:, OFF_V:OFF_C].astype(BF16)
    zc_ref[0] = y[:, OFF_C:OFF_G]
    g_ref[0] = y[:, OFF_G:]


def _inproj(x, mul, add, w_bf, q_tabs, k_tabs):
    b, t, d = x.shape
    tm = min(256, t)
    row = lambda i, bb: (bb, i, 0)
    vec = lambda i, bb: (bb, 0, 0)
    tab = lambda i, bb: (i, 0)
    widths = (OFF_B, B_WIDTH, B_KV_WIDTH, B_KV_WIDTH, 4 * C_WIDTH, C_WIDTH)
    dtypes = (BF16, BF16, BF16, BF16, F32, F32)
    return pl.pallas_call(
        _inproj_kernel,
        out_shape=[jax.ShapeDtypeStruct((b, t, w), dt) for w, dt in zip(widths, dtypes)],
        grid=(t // tm, b),
        in_specs=[pl.BlockSpec((1, tm, d), row),
                  pl.BlockSpec((1, 1, d), vec),
                  pl.BlockSpec((1, 1, d), vec),
                  pl.BlockSpec((d, IN_WIDTH), lambda i, bb: (0, 0))]
                 + [pl.BlockSpec((tm, B_WIDTH), tab)] * 3
                 + [pl.BlockSpec((tm, B_KV_WIDTH), tab)] * 3,
        out_specs=[pl.BlockSpec((1, tm, w), row) for w in widths],
        compiler_params=_cparams("arbitrary", "arbitrary"),
        name="inproj",
    )(x, mul, add, w_bf, *q_tabs, *k_tabs)


def _rope_tables(t, w, scale, width, rotate):
    ws = w.astype(F32) * scale
    if not rotate:
        c = jnp.broadcast_to(jnp.tile(ws, width // HEAD_DIM)[None, :], (t, width))
        z = jnp.zeros((t, width), F32)
        return c, z, z
    pos = jnp.arange(t)
    row = (pos // GRID_W).astype(F32)
    col = (pos % GRID_W).astype(F32)
    inv_freq = 1.0 / (ROPE_BASE ** (jnp.arange(0, HEAD_DIM // 2, 2, dtype=F32) / (HEAD_DIM // 2)))
    dd = jnp.arange(HEAD_DIM)
    axis = dd // 32
    half = (dd % 32) // 16
    ang = jnp.where(axis[None, :] == 0, row[:, None], col[:, None]) * inv_freq[dd % 16][None, :]
    cos, sin = jnp.cos(ang), jnp.sin(ang)
    c = cos * ws[None, :]
    sm = jnp.where(half[None, :] == 0, -sin * jnp.roll(ws, -16)[None, :], 0.0)
    sp = jnp.where(half[None, :] == 1, sin * jnp.roll(ws, 16)[None, :], 0.0)
    rep = width // HEAD_DIM
    return jnp.tile(c, (1, rep)), jnp.tile(sp, (1, rep)), jnp.tile(sm, (1, rep))


def _gmlp_kernel(z_ref, ws_ref, bias_ref, o_ref):
    gz = jax.nn.gelu(z_ref[0].astype(F32))
    u = gz[:, :A_WIDTH]
    vn = _head_rms(gz[:, A_WIDTH:], _head_ones(A_WIDTH, BF16))
    lane_head = lax.broadcasted_iota(jnp.int32, vn.shape, 1) >> 6
    acc = bias_ref[...]
    for hh in range(A_HEADS):
        vh = jnp.where(lane_head == hh, vn, 0.0).astype(BF16)
        acc = acc + jnp.dot(ws_ref[hh], vh, preferred_element_type=F32)
    o_ref[0] = (u * acc).astype(BF16)


def _gmlp(za, ws_bf, bias2d):
    b, t, _ = za.shape
    return pl.pallas_call(
        _gmlp_kernel,
        out_shape=jax.ShapeDtypeStruct((b, t, A_WIDTH), BF16),
        grid=(b, t // A_CHUNK),
        in_specs=[pl.BlockSpec((1, A_CHUNK, OFF_B), lambda bb, i: (bb, i, 0)),
                  pl.BlockSpec((A_HEADS, A_CHUNK, A_CHUNK), lambda bb, i: (0, 0, 0)),
                  pl.BlockSpec((A_CHUNK, A_WIDTH), lambda bb, i: (0, 0))],
        out_specs=pl.BlockSpec((1, A_CHUNK, A_WIDTH), lambda bb, i: (bb, i, 0)),
        compiler_params=_cparams("arbitrary", "arbitrary"),
        name="gmlp",
    )(za, ws_bf, bias2d)


def _attn_kernel(q_ref, k_ref, v_ref, o_ref):
    g, tq, dh = q_ref.shape[1:]
    q = q_ref[0].reshape(g * tq, dh)
    s = lax.dot_general(q, k_ref[0, 0], (((1,), (1,)), ((), ())), preferred_element_type=F32)
    m = jnp.max(s, axis=-1, keepdims=True)
    p = jnp.exp(s - m)
    l = jnp.sum(p, axis=-1, keepdims=True)
    o = jnp.dot(p.astype(BF16), v_ref[0, 0], preferred_element_type=F32) / l
    o_ref[0] = o.reshape(g, tq, dh).astype(BF16)


def _attn(q, k, v):
    b, h, t, dh = q.shape
    kvh, s = k.shape[1], k.shape[2]
    g = h // kvh
    tq = min(128, t)
    return pl.pallas_call(
        _attn_kernel,
        out_shape=jax.ShapeDtypeStruct((b, h, t, dh), BF16),
        grid=(b, kvh, t // tq),
        in_specs=[pl.BlockSpec((1, g, tq, dh), lambda bb, j, i: (bb, j, i, 0)),
                  pl.BlockSpec((1, 1, s, dh), lambda bb, j, i: (bb, j, 0, 0)),
                  pl.BlockSpec((1, 1, s, dh), lambda bb, j, i: (bb, j, 0, 0))],
        out_specs=pl.BlockSpec((1, g, tq, dh), lambda bb, j, i: (bb, j, i, 0)),
        compiler_params=_cparams("arbitrary", "arbitrary", "arbitrary"),
        name="attn",
    )(q, k, v)


def _split_heads(a, n):
    b, t, _ = a.shape
    return a.reshape(b, t, n, HEAD_DIM).transpose(0, 2, 1, 3)


def _merge_heads(a):
    b, h, t, dh = a.shape
    return a.transpose(0, 2, 1, 3).reshape(b, t, h * dh)


def _cumsum_rows(x):
    rows = lax.broadcasted_iota(jnp.int32, x.shape, 0)
    sh = 1
    while sh < x.shape[0]:
        x = x + jnp.where(rows >= sh, pltpu.roll(x, sh, 0), 0.0)
        sh *= 2
    return x


def _hgrn_kernel(q_ref, z_ref, v_ref, lbc_ref, o_ref, st_ref):
    n = C_WIDTH
    nb = HGRN_BLOCK
    one_m_lb = lbc_ref[0, 0:1, :]
    log1m_lb = lbc_ref[0, 1:2, :]
    log_lb = lbc_ref[0, 2:3, :]
    lb_pos = lbc_ref[0, 3:4, :] > 0.5
    ones_bd = _head_ones(n, BF16)
    mask_bd = _head_ones(n, F32)
    rows = lax.broadcasted_iota(jnp.int32, (nb, n), 0)
    st_ref[...] = jnp.zeros_like(st_ref)

    def body(i, carry):
        r0 = pl.multiple_of(i * nb, nb)
        q = jax.nn.silu(q_ref[0, 0, pl.ds(r0, nb), :])
        z = z_ref[0, 0, pl.ds(r0, nb), :]
        v = v_ref[0, 0, pl.ds(r0, nb), :]
        soft = jnp.log1p(jnp.exp(-jnp.abs(z)))
        log_rest = log1m_lb + (jnp.minimum(z, 0.0) - soft)
        hi = jnp.maximum(log_lb, log_rest)
        lse = hi + jnp.log1p(jnp.exp(-jnp.abs(log_lb - log_rest)))
        log_f = jnp.where(lb_pos, lse, log_rest)
        k = one_m_lb * jnp.exp(jnp.minimum(-z, 0.0) - soft)
        bc = _cumsum_rows(log_f)
        st = st_ref[...]
        o = lax.dot_general((q * jnp.exp(bc)).astype(BF16), st.astype(BF16),
                            (((1,), (1,)), ((), ())), preferred_element_type=F32)
        ps = []
        for s in range(nb):
            keep = rows >= s
            e = jnp.exp(jnp.where(keep, bc - bc[s:s + 1, :], 0.0))
            ps.append(jnp.where(keep, q * e * k[s:s + 1, :], 0.0))
        sc = jnp.dot(jnp.concatenate(ps, axis=0).astype(BF16), ones_bd, preferred_element_type=F32)
        for s in range(nb):
            o = o + sc[s * nb:(s + 1) * nb, :] * v[s:s + 1, :]
        o_ref[0, 0, pl.ds(r0, nb), :] = o
        b_last = bc[nb - 1:nb, :]
        kd = (k * jnp.exp(b_last - bc)).astype(BF16)
        upd = lax.dot_general(v.astype(BF16), kd, (((0,), (0,)), ((), ())), preferred_element_type=F32)
        st_ref[...] = st * jnp.exp(b_last) + mask_bd * upd
        return carry

    lax.fori_loop(0, q_ref.shape[2] // nb, body, 0)


def _hgrn(seq, lbc):
    nd, b, l, _ = seq.shape
    n = C_WIDTH
    blk = lambda col: pl.BlockSpec((1, 1, l, n), col)
    return pl.pallas_call(
        _hgrn_kernel,
        out_shape=jax.ShapeDtypeStruct((nd, b, l, n), F32),
        grid=(nd, b),
        in_specs=[blk(lambda d, bb: (d, bb, 0, 0)),
                  blk(lambda d, bb: (d, bb, 0, 1 + d)),
                  blk(lambda d, bb: (d, bb, 0, 3)),
                  pl.BlockSpec((1, 8, n), lambda d, bb: (d, 0, 0))],
        out_specs=blk(lambda d, bb: (d, bb, 0, 0)),
        scratch_shapes=[pltpu.VMEM((n, n), F32)],
        compiler_params=_cparams("arbitrary", "arbitrary"),
        name="hgrn",
    )(seq, seq, seq, lbc)


def _outproj_kernel(x_ref, ya_ref, yb_ref, o_ref, g_ref, w_ref, gate_ref, mul_ref, add_ref, hw_ref,
                    wr_ref, br_ref, xo_ref, h2_ref, lg_ref):
    yc = _head_rms(o_ref[0], _head_ones(C_WIDTH, BF16)) * hw_ref[...] * jax.nn.silu(g_ref[0])
    y = jnp.dot(ya_ref[0], w_ref[0:A_WIDTH, :], preferred_element_type=F32)
    y = y + jnp.dot(yb_ref[0], w_ref[A_WIDTH:A_WIDTH + B_WIDTH, :], preferred_element_type=F32)
    y = y + jnp.dot(yc.astype(BF16), w_ref[A_WIDTH + B_WIDTH:, :], preferred_element_type=F32)
    xn = x_ref[0] + gate_ref[0] * y
    xo_ref[0] = xn
    ms = jnp.mean(xn * xn, axis=-1, keepdims=True)
    h2 = xn * lax.rsqrt(ms + EPS) * mul_ref[0] + add_ref[0]
    h2_ref[0] = h2.astype(BF16)
    lg_ref[0] = jnp.dot(h2, wr_ref[...], preferred_element_type=F32,
                        precision=lax.Precision.HIGHEST) + br_ref[...]


def _outproj(x, ya, yb, o, g, w_bf, gate, mul, add, hw, wr, br):
    b, t, d = x.shape
    tm = min(256, t)
    row = lambda bb, i: (bb, i, 0)
    vec = lambda bb, i: (bb, 0, 0)
    const = lambda bb, i: (0, 0)
    return pl.pallas_call(
        _outproj_kernel,
        out_shape=[jax.ShapeDtypeStruct((b, t, d), F32),
                   jax.ShapeDtypeStruct((b, t, d), BF16),
                   jax.ShapeDtypeStruct((b, t, ROUTER_LANES), F32)],
        grid=(b, t // tm),
        in_specs=[pl.BlockSpec((1, tm, d), row),
                  pl.BlockSpec((1, tm, A_WIDTH), row),
                  pl.BlockSpec((1, tm, B_WIDTH), row),
                  pl.BlockSpec((1, tm, C_WIDTH), row),
                  pl.BlockSpec((1, tm, C_WIDTH), row),
                  pl.BlockSpec((d, d), const),
                  pl.BlockSpec((1, 1, d), vec),
                  pl.BlockSpec((1, 1, d), vec),
                  pl.BlockSpec((1, 1, d), vec),
                  pl.BlockSpec((1, C_WIDTH), const),
                  pl.BlockSpec((d, ROUTER_LANES), const),
                  pl.BlockSpec((1, ROUTER_LANES), const)],
        out_specs=[pl.BlockSpec((1, tm, d), row),
                   pl.BlockSpec((1, tm, d), row),
                   pl.BlockSpec((1, tm, ROUTER_LANES), row)],
        compiler_params=_cparams("arbitrary", "arbitrary"),
        name="outproj",
    )(x, ya, yb, o, g, w_bf, gate, mul, add, hw, wr, br)


def _moe_kernel(be_ref, nu_ref, x_ref, w1_ref, w3_ref, w2_ref, o_ref, w1b, w3b, w2b):
    i = pl.program_id(0)
    e = be_ref[i]
    prev = be_ref[jnp.maximum(i - 1, 0)]

    @pl.when((i == 0) | (e != prev))
    def _():
        w1b[...] = w1_ref[0].astype(BF16)
        w3b[...] = w3_ref[0].astype(BF16)
        w2b[...] = w2_ref[0].astype(BF16)

    @pl.when(i < nu_ref[0])
    def _():
        x = x_ref[...]
        a = jnp.dot(x, w1b[...], preferred_element_type=F32)
        b = jnp.dot(x, w3b[...], preferred_element_type=F32)
        hmid = (jax.nn.silu(a) * b).astype(BF16)
        o_ref[...] = jnp.dot(hmid, w2b[...], preferred_element_type=F32).astype(BF16)

    @pl.when(i >= nu_ref[0])
    def _():
        o_ref[...] = jnp.zeros_like(o_ref)


def _moe_mlp(blk_expert, n_used, xs, w1, w3, w2):
    n_rows, d = xs.shape
    f = w1.shape[-1]
    nblk = n_rows // MOE_ROWS
    return pl.pallas_call(
        _moe_kernel,
        out_shape=jax.ShapeDtypeStruct((n_rows, d), BF16),
        grid_spec=pltpu.PrefetchScalarGridSpec(
            num_scalar_prefetch=2,
            grid=(nblk,),
            in_specs=[pl.BlockSpec((MOE_ROWS, d), lambda i, be, nu: (i, 0)),
                      pl.BlockSpec((1, d, f), lambda i, be, nu: (be[i], 0, 0)),
                      pl.BlockSpec((1, d, f), lambda i, be, nu: (be[i], 0, 0)),
                      pl.BlockSpec((1, f, d), lambda i, be, nu: (be[i], 0, 0))],
            out_specs=pl.BlockSpec((MOE_ROWS, d), lambda i, be, nu: (i, 0)),
            scratch_shapes=[pltpu.VMEM((d, f), BF16), pltpu.VMEM((d, f), BF16), pltpu.VMEM((f, d), BF16)]),
        compiler_params=_cparams("arbitrary"),
        name="moe",
    )(blk_expert, n_used, xs, w1, w3, w2)


def _hier_moe(h2, logits, w1, w3, w2):
    n_tok = h2.shape[0]
    g_logits = logits[:, :N_GROUPS]
    grp = jnp.argmax(g_logits, axis=-1)
    p_grp = jnp.take_along_axis(jax.nn.softmax(g_logits, axis=-1), grp[:, None], axis=-1)
    e_logits = logits[:, N_GROUPS:N_GROUPS + N_EXPERTS].reshape(n_tok, N_GROUPS, EXPERTS_PER_GROUP)
    e_in = jnp.take_along_axis(e_logits, grp[:, None, None], axis=1)[:, 0]
    top_val, top_idx = lax.top_k(e_in, TOP_K)
    gate = jax.nn.softmax(top_val, axis=-1) * p_grp
    expert = (grp[:, None] * EXPERTS_PER_GROUP + top_idx).reshape(-1).astype(jnp.int32)
    n_asg = n_tok * TOP_K
    onehot = (expert[:, None] == jnp.arange(N_EXPERTS, dtype=jnp.int32)[None, :]).astype(jnp.int32)
    csum = jnp.cumsum(onehot, axis=0)
    rank = jnp.take_along_axis(csum, expert[:, None], axis=1)[:, 0] - 1
    counts = csum[-1]
    padded = (counts + MOE_ROWS - 1) // MOE_ROWS * MOE_ROWS
    pad_ends = jnp.cumsum(padded)
    dest = (pad_ends - padded)[expert] + rank
    nblk = -(-n_asg // MOE_ROWS) + N_EXPERTS
    blk_expert = jnp.minimum(
        jnp.searchsorted(pad_ends, jnp.arange(nblk, dtype=jnp.int32) * MOE_ROWS, side='right'),
        N_EXPERTS - 1).astype(jnp.int32)
    n_used = (pad_ends[-1:] // MOE_ROWS).astype(jnp.int32)
    tok_of_row = jnp.zeros((nblk * MOE_ROWS,), jnp.int32).at[dest].set(
        jnp.arange(n_asg, dtype=jnp.int32) // TOP_K)
    out = _moe_mlp(blk_expert, n_used, h2[tok_of_row], w1, w3, w2)
    picked = out[dest.reshape(n_tok, TOP_K)].astype(F32)
    return jnp.sum(picked * gate[:, :, None], axis=1)


def _layer(x, xc, c, c_ctx, lb, w_mod, b_mod, norm1_w, w_in, w_s, b_s, q_norm_w, k_norm_w, hgrn_norm_w, w_out,
           norm2_w, w_grp, b_grp, w_exp, b_exp, w1, w3, w2, ctx_out):
    b, t, d = x.shape
    lc = xc.shape[1]
    cc = jnp.zeros((MOD_ROWS, d), F32).at[:b].set(c).at[b].set(c_ctx)
    mod = _mod(cc, w_mod, b_mod)
    sh1, sc1, g1, sh2, sc2, g2 = [m[:, None, :] for m in jnp.split(mod[:b], 6, axis=-1)]
    mod_c = [jnp.broadcast_to(m[None, None, :], (b, 1, d)) for m in jnp.split(mod[b], 6)]

    w_in_bf = w_in.astype(BF16)
    scale = HEAD_DIM ** -0.5
    q_tabs = _rope_tables(t, q_norm_w, scale, B_WIDTH, True)
    k_tabs = _rope_tables(t, k_norm_w, 1.0, B_KV_WIDTH, True)
    qc_tabs = _rope_tables(lc, q_norm_w, scale, B_WIDTH, False)
    kc_tabs = _rope_tables(lc, k_norm_w, 1.0, B_KV_WIDTH, False)
    za, q, k, v, zc, g = _inproj(x, norm1_w * (1.0 + sc1), sh1, w_in_bf, q_tabs, k_tabs)
    za_c, q_c, k_c, v_c, zc_c, g_c = _inproj(xc, norm1_w * (1.0 + mod_c[1]), mod_c[0], w_in_bf,
                                             qc_tabs, kc_tabs)

    ws_bf = w_s.astype(BF16)
    bias2d = jnp.repeat(b_s.T, HEAD_DIM, axis=1)
    ya = _gmlp(za, ws_bf, bias2d)

    kh_c, vh_c = _split_heads(k_c, B_KV_HEADS), _split_heads(v_c, B_KV_HEADS)
    keys = jnp.concatenate([_split_heads(k, B_KV_HEADS), kh_c], axis=2)
    vals = jnp.concatenate([_split_heads(v, B_KV_HEADS), vh_c], axis=2)
    yb = _merge_heads(_attn(_split_heads(q, B_HEADS), keys, vals))

    seq = jnp.stack([jnp.concatenate([zc_c, zc], axis=1),
                     jnp.concatenate([zc_c[:, ::-1], zc[:, ::-1]], axis=1)])
    pos = lb > 0.0
    lbc = jnp.stack([1.0 - lb, jnp.log1p(-lb), jnp.log(jnp.where(pos, lb, 1.0)), pos.astype(F32)], axis=1)
    lbc = jnp.concatenate([lbc, jnp.zeros((2, 4, C_WIDTH), F32)], axis=1)
    o_seq = _hgrn(seq, lbc)
    o = o_seq[0, :, lc:] + o_seq[1, :, lc:][:, ::-1]

    w_out_bf = w_out.astype(BF16)
    hw = jnp.tile(hgrn_norm_w, C_HEADS)[None, :]
    wr = jnp.zeros((d, ROUTER_LANES), F32).at[:, :N_GROUPS].set(w_grp).at[
        :, N_GROUPS:N_GROUPS + N_EXPERTS].set(w_exp)
    br = jnp.zeros((1, ROUTER_LANES), F32).at[0, :N_GROUPS].set(b_grp).at[
        0, N_GROUPS:N_GROUPS + N_EXPERTS].set(b_exp)
    x, h2, lg = _outproj(x, ya, yb, o, g, w_out_bf, g1, norm2_w * (1.0 + sc2), sh2, hw, wr, br)
    if ctx_out:
        ya_c = _gmlp(za_c, ws_bf, bias2d)
        yb_c = _merge_heads(_attn(_split_heads(q_c, B_HEADS), kh_c, vh_c))
        o_c = o_seq[0, :, :lc] + o_seq[1, :, :lc][:, ::-1]
        xc, h2c, lgc = _outproj(xc, ya_c, yb_c, o_c, g_c, w_out_bf, mod_c[2],
                                norm2_w * (1.0 + mod_c[4]), mod_c[3], hw, wr, br)
        tokens = jnp.concatenate([h2.reshape(-1, d), h2c.reshape(-1, d)], axis=0)
        logits = jnp.concatenate([lg.reshape(-1, ROUTER_LANES), lgc.reshape(-1, ROUTER_LANES)], axis=0)
        m = _hier_moe(tokens, logits, w1, w3, w2)
        x = x + g2 * m[:b * t].reshape(b, t, d)
        xc = xc + mod_c[5] * m[b * t:].reshape(b, lc, d)
    else:
        m = _hier_moe(h2.reshape(-1, d), lg.reshape(-1, ROUTER_LANES), w1, w3, w2)
        x = x + g2 * m.reshape(b, t, d)
    return x, xc


def kernel(x, c, ctx, c_ctx, w_mod, b_mod, norm1_w, w_in, w_s, b_s, q_norm_w, k_norm_w, hgrn_lb_logits,
           hgrn_norm_w, w_out, norm2_w, w_grp, b_grp, w_exp, b_exp, w1, w3, w2):
    depth = w_mod.shape[0]
    lb_sm = jax.nn.softmax(hgrn_lb_logits.astype(F32), axis=0)
    lb = jnp.cumsum(lb_sm, axis=0) - lb_sm[0]
    xc = ctx
    for l in range(depth):
        x, xc = _layer(x, xc, c, c_ctx, lb[l], w_mod[l], b_mod[l], norm1_w[l], w_in[l], w_s[l], b_s[l],
                       q_norm_w[l], k_norm_w[l], hgrn_norm_w[l], w_out[l], norm2_w[l], w_grp[l], b_grp[l],
                       w_exp[l], b_exp[l], w1[l], w3[l], w2[l], ctx_out=(l < depth - 1))
    return x
```

```python
import functools

import jax
import jax.numpy as jnp
from jax import lax
from jax.experimental import pallas as pl
from jax.experimental.pallas import tpu as pltpu

F32 = jnp.float32
BF16 = jnp.bfloat16

D_MODEL = 1024
HEAD_DIM = 64
GRID_W = 64
EPS = 1e-6
ROPE_BASE = 10000.0
A_WIDTH = D_MODEL // 4
A_HEADS = A_WIDTH // HEAD_DIM
A_CHUNK = 128
B_WIDTH = D_MODEL // 2
B_HEADS = B_WIDTH // HEAD_DIM
B_KV_HEADS = 2
B_GROUP = B_HEADS // B_KV_HEADS
B_KV_WIDTH = B_KV_HEADS * HEAD_DIM
C_WIDTH = D_MODEL // 4
C_HEADS = C_WIDTH // HEAD_DIM
OFF_B = 2 * A_WIDTH
OFF_KV = OFF_B + B_WIDTH
OFF_V = OFF_KV + B_KV_WIDTH
OFF_C = OFF_KV + 2 * B_KV_WIDTH
OFF_G = OFF_C + 4 * C_WIDTH
IN_WIDTH = OFF_G + C_WIDTH
N_GROUPS = 4
EXPERTS_PER_GROUP = 8
N_EXPERTS = N_GROUPS * EXPERTS_PER_GROUP
TOP_K = 2
D_FF_EXPERT = D_MODEL // 2

MOD_ROWS = 16
ROUTER_LANES = 128
HGRN_BLOCK = 32
LOG2E = 1.4426950408889634
SAFE_SHIFT = 60
SAFE_DECAY = 80.0
MOE_ROWS = 256
VMEM_LIMIT = 48 * 1024 * 1024


def _cparams(*sem):
    return pltpu.CompilerParams(dimension_semantics=sem, vmem_limit_bytes=VMEM_LIMIT)


def _head_ones(n, dtype):
    r = lax.broadcasted_iota(jnp.int32, (n, n), 0) >> 6
    c = lax.broadcasted_iota(jnp.int32, (n, n), 1) >> 6
    return (r == c).astype(dtype)


def _head_sum(x, ones_bd):
    hi = x.astype(BF16)
    lo = (x - hi.astype(F32)).astype(BF16)
    return (jnp.dot(hi, ones_bd, preferred_element_type=F32)
            + jnp.dot(lo, ones_bd, preferred_element_type=F32))


def _head_rms(x, ones_bd):
    return x * lax.rsqrt(_head_sum(x * x, ones_bd) * (1.0 / HEAD_DIM) + EPS)


def _mod_kernel(c_ref, w_ref, b_ref, o_ref):
    a = jax.nn.silu(c_ref[...])
    o_ref[...] = jnp.dot(a, w_ref[...], preferred_element_type=F32,
                         precision=lax.Precision.HIGHEST) + b_ref[...]


def _mod(cc, w_mod, b_mod):
    n = w_mod.shape[1]
    tn = 1536
    return pl.pallas_call(
        _mod_kernel,
        out_shape=jax.ShapeDtypeStruct((MOD_ROWS, n), F32),
        grid=(n // tn,),
        in_specs=[pl.BlockSpec((MOD_ROWS, D_MODEL), lambda j: (0, 0)),
                  pl.BlockSpec((D_MODEL, tn), lambda j: (0, j)),
                  pl.BlockSpec((1, tn), lambda j: (0, j))],
        out_specs=pl.BlockSpec((MOD_ROWS, tn), lambda j: (0, j)),
        compiler_params=_cparams("arbitrary"),
        name="mod",
    )(cc, w_mod, b_mod.reshape(1, n))


def _rope(xn, c_ref, sp_ref, sm_ref):
    w = xn.shape[-1]
    return (xn * c_ref[...] + pltpu.roll(xn, 16, 1) * sp_ref[...]
            + pltpu.roll(xn, w - 16, 1) * sm_ref[...])


def _inproj_kernel(x_ref, mul_ref, add_ref, w_ref, qc_ref, qsp_ref, qsm_ref, kc_ref, ksp_ref, ksm_ref,
                   za_ref, q_ref, k_ref, v_ref, zc_ref, g_ref):
    x = x_ref[0]
    ms = jnp.mean(x * x, axis=-1, keepdims=True)
    h = x * lax.rsqrt(ms + EPS) * mul_ref[0] + add_ref[0]
    y = jnp.dot(h.astype(BF16), w_ref[...], preferred_element_type=F32)
    za_ref[0] = y[:, :OFF_B].astype(BF16)
    qn = _head_rms(y[:, OFF_B:OFF_KV], _head_ones(B_WIDTH, BF16))
    q_ref[0] = _rope(qn, qc_ref, qsp_ref, qsm_ref).astype(BF16)
    kn = _head_rms(y[:, OFF_KV:OFF_V], _head_ones(B_KV_WIDTH, BF16))
    k_ref[0] = _rope(kn, kc_ref, ksp_ref, ksm_ref).astype(BF16)
    v_ref[0] = y[:, OFF_V:OFF_C].astype(BF16)
    zc_ref[0] = y[:, OFF_C:OFF_G]
    g_ref[0] = y[:, OFF_G:]


def _inproj(x, mul, add, w_bf, q_tabs, k_tabs):
    b, t, d = x.shape
    tm = min(256, t)
    row = lambda i, bb: (bb, i, 0)
    vec = lambda i, bb: (bb, 0, 0)
    tab = lambda i, bb: (i, 0)
    widths = (OFF_B, B_WIDTH, B_KV_WIDTH, B_KV_WIDTH, 4 * C_WIDTH, C_WIDTH)
    dtypes = (BF16, BF16, BF16, BF16, F32, F32)
    return pl.pallas_call(
        _inproj_kernel,
        out_shape=[jax.ShapeDtypeStruct((b, t, w), dt) for w, dt in zip(widths, dtypes)],
        grid=(t // tm, b),
        in_specs=[pl.BlockSpec((1, tm, d), row),
                  pl.BlockSpec((1, 1, d), vec),
                  pl.BlockSpec((1, 1, d), vec),
                  pl.BlockSpec((d, IN_WIDTH), lambda i, bb: (0, 0))]
                 + [pl.BlockSpec((tm, B_WIDTH), tab)] * 3
                 + [pl.BlockSpec((tm, B_KV_WIDTH), tab)] * 3,
        out_specs=[pl.BlockSpec((1, tm, w), row) for w in widths],
        compiler_params=_cparams("arbitrary", "arbitrary"),
        name="inproj",
    )(x, mul, add, w_bf, *q_tabs, *k_tabs)


def _rope_tables(t, w, scale, width, rotate):
    ws = w.astype(F32) * scale
    if not rotate:
        c = jnp.broadcast_to(jnp.tile(ws, width // HEAD_DIM)[None, :], (t, width))
        z = jnp.zeros((t, width), F32)
        return c, z, z
    pos = jnp.arange(t)
    row = (pos // GRID_W).astype(F32)
    col = (pos % GRID_W).astype(F32)
    inv_freq = 1.0 / (ROPE_BASE ** (jnp.arange(0, HEAD_DIM // 2, 2, dtype=F32) / (HEAD_DIM // 2)))
    dd = jnp.arange(HEAD_DIM)
    axis = dd // 32
    half = (dd % 32) // 16
    ang = jnp.where(axis[None, :] == 0, row[:, None], col[:, None]) * inv_freq[dd % 16][None, :]
    cos, sin = jnp.cos(ang), jnp.sin(ang)
    c = cos * ws[None, :]
    sm = jnp.where(half[None, :] == 0, -sin * jnp.roll(ws, -16)[None, :], 0.0)
    sp = jnp.where(half[None, :] == 1, sin * jnp.roll(ws, 16)[None, :], 0.0)
    rep = width // HEAD_DIM
    return jnp.tile(c, (1, rep)), jnp.tile(sp, (1, rep)), jnp.tile(sm, (1, rep))


def _gmlp_kernel(z_ref, ws_ref, bias_ref, o_ref):
    gz = jax.nn.gelu(z_ref[0].astype(F32))
    u = gz[:, :A_WIDTH]
    vn = _head_rms(gz[:, A_WIDTH:], _head_ones(A_WIDTH, BF16))
    lane_head = lax.broadcasted_iota(jnp.int32, vn.shape, 1) >> 6
    acc = bias_ref[...]
    for hh in range(A_HEADS):
        vh = jnp.where(lane_head == hh, vn, 0.0).astype(BF16)
        acc = acc + jnp.dot(ws_ref[hh], vh, preferred_element_type=F32)
    o_ref[0] = (u * acc).astype(BF16)


def _gmlp(za, ws_bf, bias2d):
    b, t, _ = za.shape
    return pl.pallas_call(
        _gmlp_kernel,
        out_shape=jax.ShapeDtypeStruct((b, t, A_WIDTH), BF16),
        grid=(b, t // A_CHUNK),
        in_specs=[pl.BlockSpec((1, A_CHUNK, OFF_B), lambda bb, i: (bb, i, 0)),
                  pl.BlockSpec((A_HEADS, A_CHUNK, A_CHUNK), lambda bb, i: (0, 0, 0)),
                  pl.BlockSpec((A_CHUNK, A_WIDTH), lambda bb, i: (0, 0))],
        out_specs=pl.BlockSpec((1, A_CHUNK, A_WIDTH), lambda bb, i: (bb, i, 0)),
        compiler_params=_cparams("arbitrary", "arbitrary"),
        name="gmlp",
    )(za, ws_bf, bias2d)


def _attn_kernel(flag_ref, q_ref, *refs, n_seg):
    kv_refs, o_ref = refs[:2 * n_seg], refs[2 * n_seg]
    tq = q_ref.shape[1]
    dh = HEAD_DIM

    def heads(j):
        q4 = jnp.concatenate([q_ref[0, :, (B_GROUP * j + gg) * dh:(B_GROUP * j + gg + 1) * dh]
                              for gg in range(B_GROUP)], axis=0)
        ks = [kv_refs[2 * sg][0, :, j * dh:(j + 1) * dh] for sg in range(n_seg)]
        vs = [kv_refs[2 * sg + 1][0, :, j * dh:(j + 1) * dh] for sg in range(n_seg)]
        return q4, ks, vs

    def scores(q4, ks):
        return [lax.dot_general(q4, kk, (((1,), (1,)), ((), ())), preferred_element_type=F32) for kk in ks]

    def finish(j, ps, vs):
        l = sum(jnp.sum(p, axis=-1, keepdims=True) for p in ps)
        acc = sum(jnp.dot(p.astype(BF16), vv, preferred_element_type=F32) for p, vv in zip(ps, vs))
        o = acc / l
        for gg in range(B_GROUP):
            hh = B_GROUP * j + gg
            o_ref[0, :, hh * dh:(hh + 1) * dh] = o[gg * tq:(gg + 1) * tq].astype(BF16)

    @pl.when(flag_ref[0] > 0)
    def _():
        shift = flag_ref[1].astype(F32)
        for j in range(B_KV_HEADS):
            q4, ks, vs = heads(j)
            finish(j, [jnp.exp2(s - shift) for s in scores(q4, ks)], vs)

    @pl.when(flag_ref[0] <= 0)
    def _():
        for j in range(B_KV_HEADS):
            q4, ks, vs = heads(j)
            ss = scores(q4, ks)
            m = functools.reduce(jnp.maximum, [jnp.max(s, axis=-1, keepdims=True) for s in ss])
            finish(j, [jnp.exp2(s - m) for s in ss], vs)


def _attn(flag, q, kv_segs):
    b, t, w = q.shape
    tq = min(128, t)
    n_seg = len(kv_segs)
    kv_flat, kv_specs = [], []
    for kk, vv in kv_segs:
        s_len, kw = kk.shape[1:]
        kv_flat += [kk, vv]
        kv_specs += [pl.BlockSpec((1, s_len, kw), lambda bb, i, fl: (bb, 0, 0))] * 2
    return pl.pallas_call(
        functools.partial(_attn_kernel, n_seg=n_seg),
        out_shape=jax.ShapeDtypeStruct((b, t, w), BF16),
        grid_spec=pltpu.PrefetchScalarGridSpec(
            num_scalar_prefetch=1,
            grid=(b, t // tq),
            in_specs=[pl.BlockSpec((1, tq, w), lambda bb, i, fl: (bb, i, 0))] + kv_specs,
            out_specs=pl.BlockSpec((1, tq, w), lambda bb, i, fl: (bb, i, 0))),
        compiler_params=_cparams("arbitrary", "arbitrary"),
        name="attn",
    )(flag, q, *kv_flat)


def _scan_rows(x, reverse):
    n = x.shape[0]
    rows = lax.broadcasted_iota(jnp.int32, x.shape, 0)
    sh = 1
    while sh < n:
        if reverse:
            x = x + jnp.where(rows < n - sh, pltpu.roll(x, n - sh, 0), 0.0)
        else:
            x = x + jnp.where(rows >= sh, pltpu.roll(x, sh, 0), 0.0)
        sh *= 2
    return x


def _stack_heads(x, lane_head):
    return jnp.concatenate([jnp.where(lane_head == hh, x, 0.0) for hh in range(C_HEADS)], axis=0)


def _hgrn_kernel(z_ref, zc_ref, lbc_ref, *refs, ctx_out):
    if ctx_out:
        o_ref, oc_ref, st_ref, kx_ref, bx_ref, vx_ref, flag_ref = refs
    else:
        o_ref, st_ref, kx_ref, bx_ref, vx_ref, flag_ref = refs
        oc_ref = None
    n = C_WIDTH
    nb = HGRN_BLOCK
    nblk_c = zc_ref.shape[1] // nb
    nblk_l = z_ref.shape[1] // nb
    ones_bd = _head_ones(n, BF16)
    rows = lax.broadcasted_iota(jnp.int32, (nb, n), 0)
    lane_head = lax.broadcasted_iota(jnp.int32, (nb, n), 1) >> 6
    lane_head64 = lax.broadcasted_iota(jnp.int32, (HEAD_DIM, n), 1) >> 6
    diag64 = (lax.broadcasted_iota(jnp.int32, (HEAD_DIM, n), 0)
              == (lax.broadcasted_iota(jnp.int32, (HEAD_DIM, n), 1) & (HEAD_DIM - 1)))
    low_half = (lax.broadcasted_iota(jnp.int32, (HEAD_DIM, 2 * HEAD_DIM), 1) < HEAD_DIM)
    sc_t = lax.broadcasted_iota(jnp.int32, (nb, C_HEADS * nb), 0)
    sc_s = lax.broadcasted_iota(jnp.int32, (nb, C_HEADS * nb), 1) & (nb - 1)

    def flag_blocks(src_ref, nblk, base):
        def body(i, carry):
            r0 = pl.multiple_of(i * nb, nb)
            for d in range(2):
                zz = src_ref[0, pl.ds(r0, nb), (1 + d) * n:(2 + d) * n]
                bound = (jnp.sum(jnp.maximum(-zz, 0.0), axis=0, keepdims=True)
                         + nb * (jnp.log(2.0) - lbc_ref[d, 1:2, :]))
                flag_ref[d, base + i] = (jnp.max(bound) <= SAFE_DECAY).astype(jnp.int32)
            return carry
        lax.fori_loop(0, nblk, body, 0)

    flag_blocks(zc_ref, nblk_c, 0)
    flag_blocks(z_ref, nblk_l, nblk_c)
    st_ref[...] = jnp.zeros_like(st_ref)
    o_ref[...] = jnp.zeros_like(o_ref)
    if ctx_out:
        oc_ref[...] = jnp.zeros_like(oc_ref)

    def step(src_ref, dst_ref, blk, flag_idx, d):
        reverse = d == 1
        r0 = pl.multiple_of(blk * nb, nb)
        z = src_ref[0, pl.ds(r0, nb), (1 + d) * n:(2 + d) * n]
        v = src_ref[0, pl.ds(r0, nb), 3 * n:4 * n]
        one_m_lb = lbc_ref[d, 0:1, :]
        log1m_lb = lbc_ref[d, 1:2, :]
        log_lb = lbc_ref[d, 2:3, :]
        lb_pos = lbc_ref[d, 3:4, :] > 0.5
        soft = jnp.log1p(jnp.exp(-jnp.abs(z)))
        log_rest = log1m_lb + (jnp.minimum(z, 0.0) - soft)
        lse = jnp.maximum(log_lb, log_rest) + jnp.log1p(jnp.exp(-jnp.abs(log_lb - log_rest)))
        log_f = jnp.where(lb_pos, lse, log_rest)
        k = one_m_lb * jnp.exp(jnp.minimum(-z, 0.0) - soft)
        bc = _scan_rows(log_f, reverse)
        edge = 0 if reverse else nb - 1
        b_edge = bc[edge:edge + 1, :]
        st = st_ref[d]
        v_bf = v.astype(BF16)

        if dst_ref is not None:
            q = jax.nn.silu(src_ref[0, pl.ds(r0, nb), 0:n])
            qt = (q * jnp.exp(bc)).astype(BF16)
            o = jnp.dot(qt, _stack_heads(st, lane_head64).astype(BF16), preferred_element_type=F32)

            def intra_fast():
                kt = _stack_heads(k * jnp.exp(-bc), lane_head).astype(BF16)
                sc = lax.dot_general(qt, kt, (((1,), (1,)), ((), ())), preferred_element_type=F32)
                keep = (sc_s >= sc_t) if reverse else (sc_s <= sc_t)
                sc = jnp.where(keep, sc, 0.0).astype(BF16)
                return jnp.dot(sc, _stack_heads(v, lane_head).astype(BF16), preferred_element_type=F32)

            def intra_exact():
                kx_ref[d] = k
                bx_ref[d] = bc
                vx_ref[d] = v

                def sbody(s, acc):
                    keep = (rows <= s) if reverse else (rows >= s)
                    e = jnp.exp(jnp.where(keep, bc - bx_ref[d, pl.ds(s, 1), :], 0.0))
                    p = jnp.where(keep, q * e * kx_ref[d, pl.ds(s, 1), :], 0.0)
                    sc = jnp.dot(p.astype(BF16), ones_bd, preferred_element_type=F32)
                    return acc + sc * vx_ref[d, pl.ds(s, 1), :]

                return lax.fori_loop(0, nb, sbody, jnp.zeros((nb, n), F32))

            o = o + lax.cond(flag_ref[d, flag_idx] > 0, intra_fast, intra_exact)
            dst_ref[0, pl.ds(r0, nb), :] += o

        decay = _head_sum(jnp.where(diag64, jnp.exp(b_edge), 0.0), ones_bd)
        kd = (k * jnp.exp(b_edge - bc)).astype(BF16)
        full = lax.dot_general(kd, v_bf, (((0,), (0,)), ((), ())), preferred_element_type=F32)
        upd = jnp.concatenate(
            [jnp.where(low_half,
                       full[(2 * c) * HEAD_DIM:(2 * c + 1) * HEAD_DIM, 2 * c * HEAD_DIM:(2 * c + 2) * HEAD_DIM],
                       full[(2 * c + 1) * HEAD_DIM:(2 * c + 2) * HEAD_DIM, 2 * c * HEAD_DIM:(2 * c + 2) * HEAD_DIM])
             for c in range(C_HEADS // 2)], axis=1)
        st_ref[d] = st * decay + upd

    def run(src_ref, dst_ref, nblk, base):
        def body(i, carry):
            step(src_ref, dst_ref, i, base + i, 0)
            step(src_ref, dst_ref, nblk - 1 - i, base + nblk - 1 - i, 1)
            return carry
        lax.fori_loop(0, nblk, body, 0)

    run(zc_ref, oc_ref, nblk_c, 0)
    run(z_ref, o_ref, nblk_l, nblk_c)


def _hgrn(zc, zc_c, lbc, ctx_out):
    b, t, w = zc.shape
    lc = zc_c.shape[1]
    n = C_WIDTH
    row = lambda bb: (bb, 0, 0)
    out_shape = [jax.ShapeDtypeStruct((b, t, n), F32)]
    out_specs = [pl.BlockSpec((1, t, n), row)]
    if ctx_out:
        out_shape.append(jax.ShapeDtypeStruct((b, lc, n), F32))
        out_specs.append(pl.BlockSpec((1, lc, n), row))
    res = pl.pallas_call(
        functools.partial(_hgrn_kernel, ctx_out=ctx_out),
        out_shape=out_shape,
        grid=(b,),
        in_specs=[pl.BlockSpec((1, t, w), row),
                  pl.BlockSpec((1, lc, w), row),
                  pl.BlockSpec((2, 8, n), lambda bb: (0, 0, 0))],
        out_specs=out_specs,
        scratch_shapes=[pltpu.VMEM((2, HEAD_DIM, n), F32)]
                       + [pltpu.VMEM((2, HGRN_BLOCK, n), F32)] * 3
                       + [pltpu.SMEM((2, (t + lc) // HGRN_BLOCK), jnp.int32)],
        compiler_params=_cparams("arbitrary"),
        name="hgrn",
    )(zc, zc_c, lbc)
    return (res[0], res[1]) if ctx_out else (res[0], None)


def _outproj_kernel(x_ref, ya_ref, yb_ref, o_ref, g_ref, w_ref, gate_ref, mul_ref, add_ref, hw_ref,
                    wr_ref, br_ref, xo_ref, h2_ref, lg_ref):
    yc = _head_rms(o_ref[0], _head_ones(C_WIDTH, BF16)) * hw_ref[...] * jax.nn.silu(g_ref[0])
    y = jnp.dot(ya_ref[0], w_ref[0:A_WIDTH, :], preferred_element_type=F32)
    y = y + jnp.dot(yb_ref[0], w_ref[A_WIDTH:A_WIDTH + B_WIDTH, :], preferred_element_type=F32)
    y = y + jnp.dot(yc.astype(BF16), w_ref[A_WIDTH + B_WIDTH:, :], preferred_element_type=F32)
    xn = x_ref[0] + gate_ref[0] * y
    xo_ref[0] = xn
    ms = jnp.mean(xn * xn, axis=-1, keepdims=True)
    h2 = xn * lax.rsqrt(ms + EPS) * mul_ref[0] + add_ref[0]
    h2_ref[0] = h2.astype(BF16)
    lg_ref[0] = jnp.dot(h2, wr_ref[...], preferred_element_type=F32,
                        precision=lax.Precision.HIGHEST) + br_ref[...]


def _outproj(x, ya, yb, o, g, w_bf, gate, mul, add, hw, wr, br):
    b, t, d = x.shape
    tm = min(256, t)
    row = lambda bb, i: (bb, i, 0)
    vec = lambda bb, i: (bb, 0, 0)
    const = lambda bb, i: (0, 0)
    return pl.pallas_call(
        _outproj_kernel,
        out_shape=[jax.ShapeDtypeStruct((b, t, d), F32),
                   jax.ShapeDtypeStruct((b, t, d), BF16),
                   jax.ShapeDtypeStruct((b, t, ROUTER_LANES), F32)],
        grid=(b, t // tm),
        in_specs=[pl.BlockSpec((1, tm, d), row),
                  pl.BlockSpec((1, tm, A_WIDTH), row),
                  pl.BlockSpec((1, tm, B_WIDTH), row),
                  pl.BlockSpec((1, tm, C_WIDTH), row),
                  pl.BlockSpec((1, tm, C_WIDTH), row),
                  pl.BlockSpec((d, d), const),
                  pl.BlockSpec((1, 1, d), vec),
                  pl.BlockSpec((1, 1, d), vec),
                  pl.BlockSpec((1, 1, d), vec),
                  pl.BlockSpec((1, C_WIDTH), const),
                  pl.BlockSpec((d, ROUTER_LANES), const),
                  pl.BlockSpec((1, ROUTER_LANES), const)],
        out_specs=[pl.BlockSpec((1, tm, d), row),
                   pl.BlockSpec((1, tm, d), row),
                   pl.BlockSpec((1, tm, ROUTER_LANES), row)],
        compiler_params=_cparams("arbitrary", "arbitrary"),
        name="outproj",
    )(x, ya, yb, o, g, w_bf, gate, mul, add, hw, wr, br)


def _moe_kernel(be_ref, nu_ref, x_ref, w1_ref, w3_ref, w2_ref, o_ref, w1b, w3b, w2b):
    i = pl.program_id(0)
    e = be_ref[i]
    prev = be_ref[jnp.maximum(i - 1, 0)]

    @pl.when((i == 0) | (e != prev))
    def _():
        w1b[...] = w1_ref[0].astype(BF16)
        w3b[...] = w3_ref[0].astype(BF16)
        w2b[...] = w2_ref[0].astype(BF16)

    @pl.when(i < nu_ref[0])
    def _():
        x = x_ref[...]
        a = jnp.dot(x, w1b[...], preferred_element_type=F32)
        b = jnp.dot(x, w3b[...], preferred_element_type=F32)
        hmid = (jax.nn.silu(a) * b).astype(BF16)
        o_ref[...] = jnp.dot(hmid, w2b[...], preferred_element_type=F32).astype(BF16)

    @pl.when(i >= nu_ref[0])
    def _():
        o_ref[...] = jnp.zeros_like(o_ref)


def _moe_mlp(blk_expert, n_used, xs, w1, w3, w2):
    n_rows, d = xs.shape
    f = w1.shape[-1]
    nblk = n_rows // MOE_ROWS
    return pl.pallas_call(
        _moe_kernel,
        out_shape=jax.ShapeDtypeStruct((n_rows, d), BF16),
        grid_spec=pltpu.PrefetchScalarGridSpec(
            num_scalar_prefetch=2,
            grid=(nblk,),
            in_specs=[pl.BlockSpec((MOE_ROWS, d), lambda i, be, nu: (i, 0)),
                      pl.BlockSpec((1, d, f), lambda i, be, nu: (be[i], 0, 0)),
                      pl.BlockSpec((1, d, f), lambda i, be, nu: (be[i], 0, 0)),
                      pl.BlockSpec((1, f, d), lambda i, be, nu: (be[i], 0, 0))],
            out_specs=pl.BlockSpec((MOE_ROWS, d), lambda i, be, nu: (i, 0)),
            scratch_shapes=[pltpu.VMEM((d, f), BF16), pltpu.VMEM((d, f), BF16), pltpu.VMEM((f, d), BF16)]),
        compiler_params=_cparams("arbitrary"),
        name="moe",
    )(blk_expert, n_used, xs, w1, w3, w2)


def _hier_moe(h2, logits, w1, w3, w2):
    n_tok = h2.shape[0]
    g_logits = logits[:, :N_GROUPS]
    grp = jnp.argmax(g_logits, axis=-1)
    p_grp = jnp.take_along_axis(jax.nn.softmax(g_logits, axis=-1), grp[:, None], axis=-1)
    e_logits = logits[:, N_GROUPS:N_GROUPS + N_EXPERTS].reshape(n_tok, N_GROUPS, EXPERTS_PER_GROUP)
    e_in = jnp.take_along_axis(e_logits, grp[:, None, None], axis=1)[:, 0]
    top_val, top_idx = lax.top_k(e_in, TOP_K)
    gate = jax.nn.softmax(top_val, axis=-1) * p_grp
    expert = (grp[:, None] * EXPERTS_PER_GROUP + top_idx).reshape(-1).astype(jnp.int32)
    n_asg = n_tok * TOP_K
    onehot = (expert[:, None] == jnp.arange(N_EXPERTS, dtype=jnp.int32)[None, :]).astype(jnp.int32)
    csum = jnp.cumsum(onehot, axis=0)
    rank = jnp.take_along_axis(csum, expert[:, None], axis=1)[:, 0] - 1
    counts = csum[-1]
    padded = (counts + MOE_ROWS - 1) // MOE_ROWS * MOE_ROWS
    pad_ends = jnp.cumsum(padded)
    dest = (pad_ends - padded)[expert] + rank
    nblk = -(-n_asg // MOE_ROWS) + N_EXPERTS
    blk_expert = jnp.minimum(
        jnp.searchsorted(pad_ends, jnp.arange(nblk, dtype=jnp.int32) * MOE_ROWS, side='right'),
        N_EXPERTS - 1).astype(jnp.int32)
    n_used = (pad_ends[-1:] // MOE_ROWS).astype(jnp.int32)
    tok_of_row = jnp.zeros((nblk * MOE_ROWS,), jnp.int32).at[dest].set(
        jnp.arange(n_asg, dtype=jnp.int32) // TOP_K)
    out = _moe_mlp(blk_expert, n_used, h2[tok_of_row], w1, w3, w2)
    picked = out[dest.reshape(n_tok, TOP_K)].astype(F32)
    return jnp.sum(picked * gate[:, :, None], axis=1)


def _layer(x, xc, c, c_ctx, lb, w_mod, b_mod, norm1_w, w_in, w_s, b_s, q_norm_w, k_norm_w, hgrn_norm_w, w_out,
           norm2_w, w_grp, b_grp, w_exp, b_exp, w1, w3, w2, ctx_out):
    b, t, d = x.shape
    lc = xc.shape[1]
    cc = jnp.zeros((MOD_ROWS, d), F32).at[:b].set(c).at[b].set(c_ctx)
    mod = _mod(cc, w_mod, b_mod)
    sh1, sc1, g1, sh2, sc2, g2 = [m[:, None, :] for m in jnp.split(mod[:b], 6, axis=-1)]
    mod_c = [jnp.broadcast_to(m[None, None, :], (b, 1, d)) for m in jnp.split(mod[b], 6)]

    w_in_bf = w_in.astype(BF16)
    scale = LOG2E * HEAD_DIM ** -0.5
    q_tabs = _rope_tables(t, q_norm_w, scale, B_WIDTH, True)
    k_tabs = _rope_tables(t, k_norm_w, 1.0, B_KV_WIDTH, True)
    qc_tabs = _rope_tables(lc, q_norm_w, scale, B_WIDTH, False)
    kc_tabs = _rope_tables(lc, k_norm_w, 1.0, B_KV_WIDTH, False)
    za, q, k, v, zc, g = _inproj(x, norm1_w * (1.0 + sc1), sh1, w_in_bf, q_tabs, k_tabs)
    za_c, q_c, k_c, v_c, zc_c, g_c = _inproj(xc, norm1_w * (1.0 + mod_c[1]), mod_c[0], w_in_bf,
                                             qc_tabs, kc_tabs)

    ws_bf = w_s.astype(BF16)
    bias2d = jnp.repeat(b_s.T, HEAD_DIM, axis=1)
    ya = _gmlp(za, ws_bf, bias2d)

    bound = LOG2E * HEAD_DIM ** 0.5 * jnp.max(jnp.abs(q_norm_w)) * jnp.max(jnp.abs(k_norm_w)) * 1.02
    shift = jnp.ceil(bound)
    attn_flag = jnp.stack([(shift <= SAFE_SHIFT).astype(jnp.int32), shift.astype(jnp.int32)])
    yb = _attn(attn_flag, q, [(k, v), (k_c, v_c)])

    pos = lb > 0.0
    lbc = jnp.stack([1.0 - lb, jnp.log1p(-lb), jnp.log(jnp.where(pos, lb, 1.0)), pos.astype(F32)], axis=1)
    lbc = jnp.concatenate([lbc, jnp.zeros((2, 4, C_WIDTH), F32)], axis=1)
    o, o_c = _hgrn(zc, zc_c, lbc, ctx_out)

    w_out_bf = w_out.astype(BF16)
    hw = jnp.tile(hgrn_norm_w, C_HEADS)[None, :]
    wr = jnp.zeros((d, ROUTER_LANES), F32).at[:, :N_GROUPS].set(w_grp).at[
        :, N_GROUPS:N_GROUPS + N_EXPERTS].set(w_exp)
    br = jnp.zeros((1, ROUTER_LANES), F32).at[0, :N_GROUPS].set(b_grp).at[
        0, N_GROUPS:N_GROUPS + N_EXPERTS].set(b_exp)
    x, h2, lg = _outproj(x, ya, yb, o, g, w_out_bf, g1, norm2_w * (1.0 + sc2), sh2, hw, wr, br)
    if ctx_out:
        ya_c = _gmlp(za_c, ws_bf, bias2d)
        yb_c = _attn(attn_flag, q_c, [(k_c, v_c)])
        xc, h2c, lgc = _outproj(xc, ya_c, yb_c, o_c, g_c, w_out_bf, mod_c[2],
                                norm2_w * (1.0 + mod_c[4]), mod_c[3], hw, wr, br)
        tokens = jnp.concatenate([h2.reshape(-1, d), h2c.reshape(-1, d)], axis=0)
        logits = jnp.concatenate([lg.reshape(-1, ROUTER_LANES), lgc.reshape(-1, ROUTER_LANES)], axis=0)
        m = _hier_moe(tokens, logits, w1, w3, w2)
        x = x + g2 * m[:b * t].reshape(b, t, d)
        xc = xc + mod_c[5] * m[b * t:].reshape(b, lc, d)
    else:
        m = _hier_moe(h2.reshape(-1, d), lg.reshape(-1, ROUTER_LANES), w1, w3, w2)
        x = x + g2 * m.reshape(b, t, d)
    return x, xc


def kernel(x, c, ctx, c_ctx, w_mod, b_mod, norm1_w, w_in, w_s, b_s, q_norm_w, k_norm_w, hgrn_lb_logits,
           hgrn_norm_w, w_out, norm2_w, w_grp, b_grp, w_exp, b_exp, w1, w3, w2):
    depth = w_mod.shape[0]
    lb_sm = jax.nn.softmax(hgrn_lb_logits.astype(F32), axis=0)
    lb = jnp.cumsum(lb_sm, axis=0) - lb_sm[0]
    xc = ctx
    for l in range(depth):
        x, xc = _layer(x, xc, c, c_ctx, lb[l], w_mod[l], b_mod[l], norm1_w[l], w_in[l], w_s[l], b_s[l],
                       q_norm_w[l], k_norm_w[l], hgrn_norm_w[l], w_out[l], norm2_w[l], w_grp[l], b_grp[l],
                       w_exp[l], b_exp[l], w1[l], w3[l], w2[l], ctx_out=(l < depth - 1))
    return x
```

```python
import functools

import jax
import jax.numpy as jnp
from jax import lax
from jax.experimental import pallas as pl
from jax.experimental.pallas import tpu as pltpu

F32 = jnp.float32
BF16 = jnp.bfloat16

D_MODEL = 1024
HEAD_DIM = 64
GRID_W = 64
EPS = 1e-6
ROPE_BASE = 10000.0
A_WIDTH = D_MODEL // 4
A_HEADS = A_WIDTH // HEAD_DIM
A_CHUNK = 128
B_WIDTH = D_MODEL // 2
B_HEADS = B_WIDTH // HEAD_DIM
B_KV_HEADS = 2
B_GROUP = B_HEADS // B_KV_HEADS
B_KV_WIDTH = B_KV_HEADS * HEAD_DIM
C_WIDTH = D_MODEL // 4
C_HEADS = C_WIDTH // HEAD_DIM
OFF_B = 2 * A_WIDTH
OFF_KV = OFF_B + B_WIDTH
OFF_V = OFF_KV + B_KV_WIDTH
OFF_C = OFF_KV + 2 * B_KV_WIDTH
OFF_G = OFF_C + 4 * C_WIDTH
IN_WIDTH = OFF_G + C_WIDTH
N_GROUPS = 4
EXPERTS_PER_GROUP = 8
N_EXPERTS = N_GROUPS * EXPERTS_PER_GROUP
TOP_K = 2
D_FF_EXPERT = D_MODEL // 2

MOD_ROWS = 16
ROUTER_LANES = 128
HGRN_BLOCK = 32
HGRN_GROUP = 4
LOG2E = 1.4426950408889634
SAFE_SHIFT = 60
SAFE_DECAY = 80.0
MOE_ROWS = 256
VMEM_LIMIT = 48 * 1024 * 1024


def _cparams(*sem):
    return pltpu.CompilerParams(dimension_semantics=sem, vmem_limit_bytes=VMEM_LIMIT)


def _head_ones(n, dtype):
    r = lax.broadcasted_iota(jnp.int32, (n, n), 0) >> 6
    c = lax.broadcasted_iota(jnp.int32, (n, n), 1) >> 6
    return (r == c).astype(dtype)


def _head_sum(x, ones_bd):
    hi = x.astype(BF16)
    lo = (x - hi.astype(F32)).astype(BF16)
    return (jnp.dot(hi, ones_bd, preferred_element_type=F32)
            + jnp.dot(lo, ones_bd, preferred_element_type=F32))


def _head_rms(x, ones_bd):
    return x * lax.rsqrt(_head_sum(x * x, ones_bd) * (1.0 / HEAD_DIM) + EPS)


def _mod_kernel(c_ref, w_ref, b_ref, o_ref):
    a = jax.nn.silu(c_ref[...])
    o_ref[...] = jnp.dot(a, w_ref[...], preferred_element_type=F32,
                         precision=lax.Precision.HIGHEST) + b_ref[...]


def _mod(cc, w_mod, b_mod):
    n = w_mod.shape[1]
    tn = 1536
    return pl.pallas_call(
        _mod_kernel,
        out_shape=jax.ShapeDtypeStruct((MOD_ROWS, n), F32),
        grid=(n // tn,),
        in_specs=[pl.BlockSpec((MOD_ROWS, D_MODEL), lambda j: (0, 0)),
                  pl.BlockSpec((D_MODEL, tn), lambda j: (0, j)),
                  pl.BlockSpec((1, tn), lambda j: (0, j))],
        out_specs=pl.BlockSpec((MOD_ROWS, tn), lambda j: (0, j)),
        compiler_params=_cparams("arbitrary"),
        name="mod",
    )(cc, w_mod, b_mod.reshape(1, n))


def _rope(xn, c_ref, sp_ref, sm_ref):
    w = xn.shape[-1]
    return (xn * c_ref[...] + pltpu.roll(xn, 16, 1) * sp_ref[...]
            + pltpu.roll(xn, w - 16, 1) * sm_ref[...])


def _inproj_kernel(x_ref, mul_ref, add_ref, w_ref, qc_ref, qsp_ref, qsm_ref, kc_ref, ksp_ref, ksm_ref,
                   za_ref, q_ref, k_ref, v_ref, zc_ref, g_ref):
    x = x_ref[0]
    ms = jnp.mean(x * x, axis=-1, keepdims=True)
    h = x * lax.rsqrt(ms + EPS) * mul_ref[0] + add_ref[0]
    y = jnp.dot(h.astype(BF16), w_ref[...], preferred_element_type=F32)
    za_ref[0] = y[:, :OFF_B].astype(BF16)
    qn = _head_rms(y[:, OFF_B:OFF_KV], _head_ones(B_WIDTH, BF16))
    q_ref[0] = _rope(qn, qc_ref, qsp_ref, qsm_ref).astype(BF16)
    kn = _head_rms(y[:, OFF_KV:OFF_V], _head_ones(B_KV_WIDTH, BF16))
    k_ref[0] = _rope(kn, kc_ref, ksp_ref, ksm_ref).astype(BF16)
    v_ref[0] = y[:, OFF_V:OFF_C].astype(BF16)
    zc_ref[0] = y[:, OFF_C:OFF_G]
    g_ref[0] = y[:, OFF_G:]


def _inproj(x, mul, add, w_bf, q_tabs, k_tabs):
    b, t, d = x.shape
    tm = min(256, t)
    row = lambda i, bb: (bb, i, 0)
    vec = lambda i, bb: (bb, 0, 0)
    tab = lambda i, bb: (i, 0)
    widths = (OFF_B, B_WIDTH, B_KV_WIDTH, B_KV_WIDTH, 4 * C_WIDTH, C_WIDTH)
    dtypes = (BF16, BF16, BF16, BF16, F32, F32)
    return pl.pallas_call(
        _inproj_kernel,
        out_shape=[jax.ShapeDtypeStruct((b, t, w), dt) for w, dt in zip(widths, dtypes)],
        grid=(t // tm, b),
        in_specs=[pl.BlockSpec((1, tm, d), row),
                  pl.BlockSpec((1, 1, d), vec),
                  pl.BlockSpec((1, 1, d), vec),
                  pl.BlockSpec((d, IN_WIDTH), lambda i, bb: (0, 0))]
                 + [pl.BlockSpec((tm, B_WIDTH), tab)] * 3
                 + [pl.BlockSpec((tm, B_KV_WIDTH), tab)] * 3,
        out_specs=[pl.BlockSpec((1, tm, w), row) for w in widths],
        compiler_params=_cparams("arbitrary", "arbitrary"),
        name="inproj",
    )(x, mul, add, w_bf, *q_tabs, *k_tabs)


def _rope_tables(t, w, scale, width, rotate):
    ws = w.astype(F32) * scale
    if not rotate:
        c = jnp.broadcast_to(jnp.tile(ws, width // HEAD_DIM)[None, :], (t, width))
        z = jnp.zeros((t, width), F32)
        return c, z, z
    pos = jnp.arange(t)
    row = (pos // GRID_W).astype(F32)
    col = (pos % GRID_W).astype(F32)
    inv_freq = 1.0 / (ROPE_BASE ** (jnp.arange(0, HEAD_DIM // 2, 2, dtype=F32) / (HEAD_DIM // 2)))
    dd = jnp.arange(HEAD_DIM)
    axis = dd // 32
    half = (dd % 32) // 16
    ang = jnp.where(axis[None, :] == 0, row[:, None], col[:, None]) * inv_freq[dd % 16][None, :]
    cos, sin = jnp.cos(ang), jnp.sin(ang)
    c = cos * ws[None, :]
    sm = jnp.where(half[None, :] == 0, -sin * jnp.roll(ws, -16)[None, :], 0.0)
    sp = jnp.where(half[None, :] == 1, sin * jnp.roll(ws, 16)[None, :], 0.0)
    rep = width // HEAD_DIM
    return jnp.tile(c, (1, rep)), jnp.tile(sp, (1, rep)), jnp.tile(sm, (1, rep))


def _gmlp_kernel(z_ref, ws_ref, bias_ref, o_ref):
    gz = jax.nn.gelu(z_ref[0].astype(F32))
    u = gz[:, :A_WIDTH]
    vn = _head_rms(gz[:, A_WIDTH:], _head_ones(A_WIDTH, BF16))
    lane_head = lax.broadcasted_iota(jnp.int32, vn.shape, 1) >> 6
    acc = bias_ref[...]
    for hh in range(A_HEADS):
        vh = jnp.where(lane_head == hh, vn, 0.0).astype(BF16)
        acc = acc + jnp.dot(ws_ref[hh], vh, preferred_element_type=F32)
    o_ref[0] = (u * acc).astype(BF16)


def _gmlp(za, ws_bf, bias2d):
    b, t, _ = za.shape
    return pl.pallas_call(
        _gmlp_kernel,
        out_shape=jax.ShapeDtypeStruct((b, t, A_WIDTH), BF16),
        grid=(b, t // A_CHUNK),
        in_specs=[pl.BlockSpec((1, A_CHUNK, OFF_B), lambda bb, i: (bb, i, 0)),
                  pl.BlockSpec((A_HEADS, A_CHUNK, A_CHUNK), lambda bb, i: (0, 0, 0)),
                  pl.BlockSpec((A_CHUNK, A_WIDTH), lambda bb, i: (0, 0))],
        out_specs=pl.BlockSpec((1, A_CHUNK, A_WIDTH), lambda bb, i: (bb, i, 0)),
        compiler_params=_cparams("arbitrary", "arbitrary"),
        name="gmlp",
    )(za, ws_bf, bias2d)


def _attn_kernel(flag_ref, q_ref, *refs, n_seg):
    kv_refs, o_ref = refs[:2 * n_seg], refs[2 * n_seg]
    tq = q_ref.shape[1]
    dh = HEAD_DIM

    def heads(j):
        q4 = jnp.concatenate([q_ref[0, :, (B_GROUP * j + gg) * dh:(B_GROUP * j + gg + 1) * dh]
                              for gg in range(B_GROUP)], axis=0)
        ks = [kv_refs[2 * sg][0, :, j * dh:(j + 1) * dh] for sg in range(n_seg)]
        vs = [kv_refs[2 * sg + 1][0, :, j * dh:(j + 1) * dh] for sg in range(n_seg)]
        return q4, ks, vs

    def scores(q4, ks):
        return [lax.dot_general(q4, kk, (((1,), (1,)), ((), ())), preferred_element_type=F32) for kk in ks]

    def finish(j, ps, vs):
        l = sum(jnp.sum(p, axis=-1, keepdims=True) for p in ps)
        acc = sum(jnp.dot(p.astype(BF16), vv, preferred_element_type=F32) for p, vv in zip(ps, vs))
        o = acc / l
        for gg in range(B_GROUP):
            hh = B_GROUP * j + gg
            o_ref[0, :, hh * dh:(hh + 1) * dh] = o[gg * tq:(gg + 1) * tq].astype(BF16)

    @pl.when(flag_ref[0] > 0)
    def _():
        shift = flag_ref[1].astype(F32)
        for j in range(B_KV_HEADS):
            q4, ks, vs = heads(j)
            finish(j, [jnp.exp2(s - shift) for s in scores(q4, ks)], vs)

    @pl.when(flag_ref[0] <= 0)
    def _():
        for j in range(B_KV_HEADS):
            q4, ks, vs = heads(j)
            ss = scores(q4, ks)
            m = functools.reduce(jnp.maximum, [jnp.max(s, axis=-1, keepdims=True) for s in ss])
            finish(j, [jnp.exp2(s - m) for s in ss], vs)


def _attn(flag, q, kv_segs):
    b, t, w = q.shape
    tq = min(128, t)
    n_seg = len(kv_segs)
    kv_flat, kv_specs = [], []
    for kk, vv in kv_segs:
        s_len, kw = kk.shape[1:]
        kv_flat += [kk, vv]
        kv_specs += [pl.BlockSpec((1, s_len, kw), lambda bb, i, fl: (bb, 0, 0))] * 2
    return pl.pallas_call(
        functools.partial(_attn_kernel, n_seg=n_seg),
        out_shape=jax.ShapeDtypeStruct((b, t, w), BF16),
        grid_spec=pltpu.PrefetchScalarGridSpec(
            num_scalar_prefetch=1,
            grid=(b, t // tq),
            in_specs=[pl.BlockSpec((1, tq, w), lambda bb, i, fl: (bb, i, 0))] + kv_specs,
            out_specs=pl.BlockSpec((1, tq, w), lambda bb, i, fl: (bb, i, 0))),
        compiler_params=_cparams("arbitrary", "arbitrary"),
        name="attn",
    )(flag, q, *kv_flat)


def _scan_rows(x, reverse):
    n = x.shape[0]
    rows = lax.broadcasted_iota(jnp.int32, x.shape, 0)
    sh = 1
    while sh < n:
        if reverse:
            x = x + jnp.where(rows < n - sh, pltpu.roll(x, n - sh, 0), 0.0)
        else:
            x = x + jnp.where(rows >= sh, pltpu.roll(x, sh, 0), 0.0)
        sh *= 2
    return x


def _stack_heads(x, lane_head):
    return jnp.concatenate([jnp.where(lane_head == hh, x, 0.0) for hh in range(C_HEADS)], axis=0)


def _hgrn_kernel(z_ref, zc_ref, lbc_ref, *refs, ctx_out):
    if ctx_out:
        o_ref, oc_ref, st_ref, kx_ref, bx_ref, vx_ref, flag_ref = refs
    else:
        o_ref, st_ref, kx_ref, bx_ref, vx_ref, flag_ref = refs
        oc_ref = None
    n = C_WIDTH
    nb = HGRN_BLOCK
    nblk_c = zc_ref.shape[1] // nb
    nblk_l = z_ref.shape[1] // nb
    ones_bd = _head_ones(n, BF16)
    rows = lax.broadcasted_iota(jnp.int32, (nb, n), 0)
    lane_head = lax.broadcasted_iota(jnp.int32, (nb, n), 1) >> 6
    lane_head64 = lax.broadcasted_iota(jnp.int32, (HEAD_DIM, n), 1) >> 6
    diag64 = (lax.broadcasted_iota(jnp.int32, (HEAD_DIM, n), 0)
              == (lax.broadcasted_iota(jnp.int32, (HEAD_DIM, n), 1) & (HEAD_DIM - 1)))
    low_half = (lax.broadcasted_iota(jnp.int32, (HEAD_DIM, 2 * HEAD_DIM), 1) < HEAD_DIM)
    sc_t = lax.broadcasted_iota(jnp.int32, (nb, C_HEADS * nb), 0)
    sc_s = lax.broadcasted_iota(jnp.int32, (nb, C_HEADS * nb), 1) & (nb - 1)

    def flag_blocks(src_ref, nblk, base):
        def body(i, carry):
            r0 = pl.multiple_of(i * nb, nb)
            for d in range(2):
                zz = src_ref[0, pl.ds(r0, nb), (1 + d) * n:(2 + d) * n]
                step_bound = jnp.minimum(lbc_ref[d, 4:5, :],
                                         jnp.maximum(-zz, 0.0) + (jnp.log(2.0) - lbc_ref[d, 1:2, :]))
                bound = jnp.sum(step_bound, axis=0, keepdims=True)
                flag_ref[d, base + i] = (jnp.max(bound) <= SAFE_DECAY).astype(jnp.int32)
            return carry
        lax.fori_loop(0, nblk, body, 0)

    flag_blocks(zc_ref, nblk_c, 0)
    flag_blocks(z_ref, nblk_l, nblk_c)
    st_ref[...] = jnp.zeros_like(st_ref)
    o_ref[...] = jnp.zeros_like(o_ref)
    if ctx_out:
        oc_ref[...] = jnp.zeros_like(oc_ref)

    def step(src_ref, dst_ref, blk, d, fast):
        reverse = d == 1
        r0 = pl.multiple_of(blk * nb, nb)
        z = src_ref[0, pl.ds(r0, nb), (1 + d) * n:(2 + d) * n]
        v = src_ref[0, pl.ds(r0, nb), 3 * n:4 * n]
        one_m_lb = lbc_ref[d, 0:1, :]
        log1m_lb = lbc_ref[d, 1:2, :]
        log_lb = lbc_ref[d, 2:3, :]
        lb_pos = lbc_ref[d, 3:4, :] > 0.5
        soft = jnp.log1p(jnp.exp(-jnp.abs(z)))
        log_rest = log1m_lb + (jnp.minimum(z, 0.0) - soft)
        lse = jnp.maximum(log_lb, log_rest) + jnp.log1p(jnp.exp(-jnp.abs(log_lb - log_rest)))
        log_f = jnp.where(lb_pos, lse, log_rest)
        k = one_m_lb * jnp.exp(jnp.minimum(-z, 0.0) - soft)
        bc = _scan_rows(log_f, reverse)
        edge = 0 if reverse else nb - 1
        b_edge = bc[edge:edge + 1, :]
        st = st_ref[d]
        v_bf = v.astype(BF16)

        if dst_ref is not None:
            q = jax.nn.silu(src_ref[0, pl.ds(r0, nb), 0:n])
            qt = (q * jnp.exp(bc)).astype(BF16)
            o = jnp.dot(qt, _stack_heads(st, lane_head64).astype(BF16), preferred_element_type=F32)

            def intra_fast():
                kt = _stack_heads(k * jnp.exp(-bc), lane_head).astype(BF16)
                sc = lax.dot_general(qt, kt, (((1,), (1,)), ((), ())), preferred_element_type=F32)
                keep = (sc_s >= sc_t) if reverse else (sc_s <= sc_t)
                sc = jnp.where(keep, sc, 0.0).astype(BF16)
                return jnp.dot(sc, _stack_heads(v, lane_head).astype(BF16), preferred_element_type=F32)

            def intra_exact():
                kx_ref[d] = k
                bx_ref[d] = bc
                vx_ref[d] = v

                def sbody(s, acc):
                    keep = (rows <= s) if reverse else (rows >= s)
                    e = jnp.exp(jnp.where(keep, bc - bx_ref[d, pl.ds(s, 1), :], 0.0))
                    p = jnp.where(keep, q * e * kx_ref[d, pl.ds(s, 1), :], 0.0)
                    sc = jnp.dot(p.astype(BF16), ones_bd, preferred_element_type=F32)
                    return acc + sc * vx_ref[d, pl.ds(s, 1), :]

                return lax.fori_loop(0, nb, sbody, jnp.zeros((nb, n), F32))

            o = o + (intra_fast() if fast else intra_exact())
            dst_ref[0, pl.ds(r0, nb), :] += o

        decay = _head_sum(jnp.where(diag64, jnp.exp(b_edge), 0.0), ones_bd)
        kd = (k * jnp.exp(b_edge - bc)).astype(BF16)
        full = lax.dot_general(kd, v_bf, (((0,), (0,)), ((), ())), preferred_element_type=F32)
        upd = jnp.concatenate(
            [jnp.where(low_half,
                       full[(2 * c) * HEAD_DIM:(2 * c + 1) * HEAD_DIM, 2 * c * HEAD_DIM:(2 * c + 2) * HEAD_DIM],
                       full[(2 * c + 1) * HEAD_DIM:(2 * c + 2) * HEAD_DIM, 2 * c * HEAD_DIM:(2 * c + 2) * HEAD_DIM])
             for c in range(C_HEADS // 2)], axis=1)
        st_ref[d] = st * decay + upd

    def run(src_ref, dst_ref, nblk, base):
        grp = HGRN_GROUP if nblk % HGRN_GROUP == 0 else 1

        def body(i, carry):
            fwd = [i * grp + gg for gg in range(grp)]
            bwd = [nblk - 1 - blk for blk in fwd]
            safe = functools.reduce(jnp.minimum, [flag_ref[0, base + blk] for blk in fwd]
                                    + [flag_ref[1, base + blk] for blk in bwd])

            def group(fast):
                for bf, bb in zip(fwd, bwd):
                    step(src_ref, dst_ref, bf, 0, fast)
                    step(src_ref, dst_ref, bb, 1, fast)

            pl.when(safe > 0)(functools.partial(group, True))
            pl.when(safe <= 0)(functools.partial(group, False))
            return carry
        lax.fori_loop(0, nblk // grp, body, 0)

    run(zc_ref, oc_ref, nblk_c, 0)
    run(z_ref, o_ref, nblk_l, nblk_c)


def _hgrn(zc, zc_c, lbc, ctx_out):
    b, t, w = zc.shape
    lc = zc_c.shape[1]
    n = C_WIDTH
    row = lambda bb: (bb, 0, 0)
    out_shape = [jax.ShapeDtypeStruct((b, t, n), F32)]
    out_specs = [pl.BlockSpec((1, t, n), row)]
    if ctx_out:
        out_shape.append(jax.ShapeDtypeStruct((b, lc, n), F32))
        out_specs.append(pl.BlockSpec((1, lc, n), row))
    res = pl.pallas_call(
        functools.partial(_hgrn_kernel, ctx_out=ctx_out),
        out_shape=out_shape,
        grid=(b,),
        in_specs=[pl.BlockSpec((1, t, w), row),
                  pl.BlockSpec((1, lc, w), row),
                  pl.BlockSpec((2, 8, n), lambda bb: (0, 0, 0))],
        out_specs=out_specs,
        scratch_shapes=[pltpu.VMEM((2, HEAD_DIM, n), F32)]
                       + [pltpu.VMEM((2, HGRN_BLOCK, n), F32)] * 3
                       + [pltpu.SMEM((2, (t + lc) // HGRN_BLOCK), jnp.int32)],
        compiler_params=_cparams("arbitrary"),
        name="hgrn",
    )(zc, zc_c, lbc)
    return (res[0], res[1]) if ctx_out else (res[0], None)


def _outproj_kernel(x_ref, ya_ref, yb_ref, o_ref, g_ref, w_ref, gate_ref, mul_ref, add_ref, hw_ref,
                    wr_ref, br_ref, xo_ref, h2_ref, lg_ref):
    yc = _head_rms(o_ref[0], _head_ones(C_WIDTH, BF16)) * hw_ref[...] * jax.nn.silu(g_ref[0])
    y = jnp.dot(ya_ref[0], w_ref[0:A_WIDTH, :], preferred_element_type=F32)
    y = y + jnp.dot(yb_ref[0], w_ref[A_WIDTH:A_WIDTH + B_WIDTH, :], preferred_element_type=F32)
    y = y + jnp.dot(yc.astype(BF16), w_ref[A_WIDTH + B_WIDTH:, :], preferred_element_type=F32)
    xn = x_ref[0] + gate_ref[0] * y
    xo_ref[0] = xn
    ms = jnp.mean(xn * xn, axis=-1, keepdims=True)
    h2 = xn * lax.rsqrt(ms + EPS) * mul_ref[0] + add_ref[0]
    h2_ref[0] = h2.astype(BF16)
    lg_ref[0] = jnp.dot(h2, wr_ref[...], preferred_element_type=F32,
                        precision=lax.Precision.HIGHEST) + br_ref[...]


def _outproj(x, ya, yb, o, g, w_bf, gate, mul, add, hw, wr, br):
    b, t, d = x.shape
    tm = min(256, t)
    row = lambda bb, i: (bb, i, 0)
    vec = lambda bb, i: (bb, 0, 0)
    const = lambda bb, i: (0, 0)
    return pl.pallas_call(
        _outproj_kernel,
        out_shape=[jax.ShapeDtypeStruct((b, t, d), F32),
                   jax.ShapeDtypeStruct((b, t, d), BF16),
                   jax.ShapeDtypeStruct((b, t, ROUTER_LANES), F32)],
        grid=(b, t // tm),
        in_specs=[pl.BlockSpec((1, tm, d), row),
                  pl.BlockSpec((1, tm, A_WIDTH), row),
                  pl.BlockSpec((1, tm, B_WIDTH), row),
                  pl.BlockSpec((1, tm, C_WIDTH), row),
                  pl.BlockSpec((1, tm, C_WIDTH), row),
                  pl.BlockSpec((d, d), const),
                  pl.BlockSpec((1, 1, d), vec),
                  pl.BlockSpec((1, 1, d), vec),
                  pl.BlockSpec((1, 1, d), vec),
                  pl.BlockSpec((1, C_WIDTH), const),
                  pl.BlockSpec((d, ROUTER_LANES), const),
                  pl.BlockSpec((1, ROUTER_LANES), const)],
        out_specs=[pl.BlockSpec((1, tm, d), row),
                   pl.BlockSpec((1, tm, d), row),
                   pl.BlockSpec((1, tm, ROUTER_LANES), row)],
        compiler_params=_cparams("arbitrary", "arbitrary"),
        name="outproj",
    )(x, ya, yb, o, g, w_bf, gate, mul, add, hw, wr, br)


def _moe_kernel(be_ref, nu_ref, x_ref, w1_ref, w3_ref, w2_ref, o_ref, w1b, w3b, w2b):
    i = pl.program_id(0)
    e = be_ref[i]
    prev = be_ref[jnp.maximum(i - 1, 0)]

    @pl.when((i == 0) | (e != prev))
    def _():
        w1b[...] = w1_ref[0].astype(BF16)
        w3b[...] = w3_ref[0].astype(BF16)
        w2b[...] = w2_ref[0].astype(BF16)

    @pl.when(i < nu_ref[0])
    def _():
        x = x_ref[...]
        a = jnp.dot(x, w1b[...], preferred_element_type=F32)
        b = jnp.dot(x, w3b[...], preferred_element_type=F32)
        hmid = (jax.nn.silu(a) * b).astype(BF16)
        o_ref[...] = jnp.dot(hmid, w2b[...], preferred_element_type=F32).astype(BF16)

    @pl.when(i >= nu_ref[0])
    def _():
        o_ref[...] = jnp.zeros_like(o_ref)


def _moe_mlp(blk_expert, n_used, xs, w1, w3, w2):
    n_rows, d = xs.shape
    f = w1.shape[-1]
    nblk = n_rows // MOE_ROWS
    return pl.pallas_call(
        _moe_kernel,
        out_shape=jax.ShapeDtypeStruct((n_rows, d), BF16),
        grid_spec=pltpu.PrefetchScalarGridSpec(
            num_scalar_prefetch=2,
            grid=(nblk,),
            in_specs=[pl.BlockSpec((MOE_ROWS, d), lambda i, be, nu: (i, 0)),
                      pl.BlockSpec((1, d, f), lambda i, be, nu: (be[i], 0, 0)),
                      pl.BlockSpec((1, d, f), lambda i, be, nu: (be[i], 0, 0)),
                      pl.BlockSpec((1, f, d), lambda i, be, nu: (be[i], 0, 0))],
            out_specs=pl.BlockSpec((MOE_ROWS, d), lambda i, be, nu: (i, 0)),
            scratch_shapes=[pltpu.VMEM((d, f), BF16), pltpu.VMEM((d, f), BF16), pltpu.VMEM((f, d), BF16)]),
        compiler_params=_cparams("arbitrary"),
        name="moe",
    )(blk_expert, n_used, xs, w1, w3, w2)


def _hier_moe(h2, logits, w1, w3, w2):
    n_tok = h2.shape[0]
    g_logits = logits[:, :N_GROUPS]
    grp = jnp.argmax(g_logits, axis=-1)
    p_grp = jnp.take_along_axis(jax.nn.softmax(g_logits, axis=-1), grp[:, None], axis=-1)
    e_logits = logits[:, N_GROUPS:N_GROUPS + N_EXPERTS].reshape(n_tok, N_GROUPS, EXPERTS_PER_GROUP)
    e_in = jnp.take_along_axis(e_logits, grp[:, None, None], axis=1)[:, 0]
    top_val, top_idx = lax.top_k(e_in, TOP_K)
    gate = jax.nn.softmax(top_val, axis=-1) * p_grp
    expert = (grp[:, None] * EXPERTS_PER_GROUP + top_idx).reshape(-1).astype(jnp.int32)
    n_asg = n_tok * TOP_K
    onehot = (expert[:, None] == jnp.arange(N_EXPERTS, dtype=jnp.int32)[None, :]).astype(jnp.int32)
    csum = jnp.cumsum(onehot, axis=0)
    rank = jnp.take_along_axis(csum, expert[:, None], axis=1)[:, 0] - 1
    counts = csum[-1]
    padded = (counts + MOE_ROWS - 1) // MOE_ROWS * MOE_ROWS
    pad_ends = jnp.cumsum(padded)
    dest = (pad_ends - padded)[expert] + rank
    nblk = -(-n_asg // MOE_ROWS) + N_EXPERTS
    blk_expert = jnp.minimum(
        jnp.searchsorted(pad_ends, jnp.arange(nblk, dtype=jnp.int32) * MOE_ROWS, side='right'),
        N_EXPERTS - 1).astype(jnp.int32)
    n_used = (pad_ends[-1:] // MOE_ROWS).astype(jnp.int32)
    tok_of_row = jnp.zeros((nblk * MOE_ROWS,), jnp.int32).at[dest].set(
        jnp.arange(n_asg, dtype=jnp.int32) // TOP_K)
    out = _moe_mlp(blk_expert, n_used, h2[tok_of_row], w1, w3, w2)
    picked = out[dest.reshape(n_tok, TOP_K)].astype(F32)
    return jnp.sum(picked * gate[:, :, None], axis=1)


def _layer(x, xc, c, c_ctx, lb, w_mod, b_mod, norm1_w, w_in, w_s, b_s, q_norm_w, k_norm_w, hgrn_norm_w, w_out,
           norm2_w, w_grp, b_grp, w_exp, b_exp, w1, w3, w2, ctx_out):
    b, t, d = x.shape
    lc = xc.shape[1]
    cc = jnp.zeros((MOD_ROWS, d), F32).at[:b].set(c).at[b].set(c_ctx)
    mod = _mod(cc, w_mod, b_mod)
    sh1, sc1, g1, sh2, sc2, g2 = [m[:, None, :] for m in jnp.split(mod[:b], 6, axis=-1)]
    mod_c = [jnp.broadcast_to(m[None, None, :], (b, 1, d)) for m in jnp.split(mod[b], 6)]

    w_in_bf = w_in.astype(BF16)
    scale = LOG2E * HEAD_DIM ** -0.5
    q_tabs = _rope_tables(t, q_norm_w, scale, B_WIDTH, True)
    k_tabs = _rope_tables(t, k_norm_w, 1.0, B_KV_WIDTH, True)
    qc_tabs = _rope_tables(lc, q_norm_w, scale, B_WIDTH, False)
    kc_tabs = _rope_tables(lc, k_norm_w, 1.0, B_KV_WIDTH, False)
    za, q, k, v, zc, g = _inproj(x, norm1_w * (1.0 + sc1), sh1, w_in_bf, q_tabs, k_tabs)
    za_c, q_c, k_c, v_c, zc_c, g_c = _inproj(xc, norm1_w * (1.0 + mod_c[1]), mod_c[0], w_in_bf,
                                             qc_tabs, kc_tabs)

    ws_bf = w_s.astype(BF16)
    bias2d = jnp.repeat(b_s.T, HEAD_DIM, axis=1)
    ya = _gmlp(za, ws_bf, bias2d)

    bound = LOG2E * HEAD_DIM ** 0.5 * jnp.max(jnp.abs(q_norm_w)) * jnp.max(jnp.abs(k_norm_w)) * 1.02
    shift = jnp.ceil(bound)
    attn_flag = jnp.stack([(shift <= SAFE_SHIFT).astype(jnp.int32), shift.astype(jnp.int32)])
    yb = _attn(attn_flag, q, [(k, v), (k_c, v_c)])

    pos = lb > 0.0
    log_lb = jnp.log(jnp.where(pos, lb, 1.0))
    lbc = jnp.stack([1.0 - lb, jnp.log1p(-lb), log_lb, pos.astype(F32), jnp.where(pos, -log_lb, 1e30)], axis=1)
    lbc = jnp.concatenate([lbc, jnp.zeros((2, 3, C_WIDTH), F32)], axis=1)
    o, o_c = _hgrn(zc, zc_c, lbc, ctx_out)

    w_out_bf = w_out.astype(BF16)
    hw = jnp.tile(hgrn_norm_w, C_HEADS)[None, :]
    wr = jnp.zeros((d, ROUTER_LANES), F32).at[:, :N_GROUPS].set(w_grp).at[
        :, N_GROUPS:N_GROUPS + N_EXPERTS].set(w_exp)
    br = jnp.zeros((1, ROUTER_LANES), F32).at[0, :N_GROUPS].set(b_grp).at[
        0, N_GROUPS:N_GROUPS + N_EXPERTS].set(b_exp)
    x, h2, lg = _outproj(x, ya, yb, o, g, w_out_bf, g1, norm2_w * (1.0 + sc2), sh2, hw, wr, br)
    if ctx_out:
        ya_c = _gmlp(za_c, ws_bf, bias2d)
        yb_c = _attn(attn_flag, q_c, [(k_c, v_c)])
        xc, h2c, lgc = _outproj(xc, ya_c, yb_c, o_c, g_c, w_out_bf, mod_c[2],
                                norm2_w * (1.0 + mod_c[4]), mod_c[3], hw, wr, br)
        tokens = jnp.concatenate([h2.reshape(-1, d), h2c.reshape(-1, d)], axis=0)
        logits = jnp.concatenate([lg.reshape(-1, ROUTER_LANES), lgc.reshape(-1, ROUTER_LANES)], axis=0)
        m = _hier_moe(tokens, logits, w1, w3, w2)
        x = x + g2 * m[:b * t].reshape(b, t, d)
        xc = xc + mod_c[5] * m[b * t:].reshape(b, lc, d)
    else:
        m = _hier_moe(h2.reshape(-1, d), lg.reshape(-1, ROUTER_LANES), w1, w3, w2)
        x = x + g2 * m.reshape(b, t, d)
    return x, xc


def kernel(x, c, ctx, c_ctx, w_mod, b_mod, norm1_w, w_in, w_s, b_s, q_norm_w, k_norm_w, hgrn_lb_logits,
           hgrn_norm_w, w_out, norm2_w, w_grp, b_grp, w_exp, b_exp, w1, w3, w2):
    depth = w_mod.shape[0]
    lb_sm = jax.nn.softmax(hgrn_lb_logits.astype(F32), axis=0)
    lb = jnp.cumsum(lb_sm, axis=0) - lb_sm[0]
    xc = ctx
    for l in range(depth):
        x, xc = _layer(x, xc, c, c_ctx, lb[l], w_mod[l], b_mod[l], norm1_w[l], w_in[l], w_s[l], b_s[l],
                       q_norm_w[l], k_norm_w[l], hgrn_norm_w[l], w_out[l], norm2_w[l], w_grp[l], b_grp[l],
                       w_exp[l], b_exp[l], w1[l], w3[l], w2[l], ctx_out=(l < depth - 1))
    return x
```

```python
import functools

import jax
import jax.numpy as jnp
from jax import lax
from jax.experimental import pallas as pl
from jax.experimental.pallas import tpu as pltpu
from jax.experimental.pallas import tpu_sc as plsc

F32 = jnp.float32
BF16 = jnp.bfloat16

D_MODEL = 1024
HEAD_DIM = 64
GRID_W = 64
EPS = 1e-6
ROPE_BASE = 10000.0
A_WIDTH = D_MODEL // 4
A_HEADS = A_WIDTH // HEAD_DIM
A_CHUNK = 128
B_WIDTH = D_MODEL // 2
B_HEADS = B_WIDTH // HEAD_DIM
B_KV_HEADS = 2
B_GROUP = B_HEADS // B_KV_HEADS
B_KV_WIDTH = B_KV_HEADS * HEAD_DIM
C_WIDTH = D_MODEL // 4
C_HEADS = C_WIDTH // HEAD_DIM
OFF_B = 2 * A_WIDTH
OFF_KV = OFF_B + B_WIDTH
OFF_V = OFF_KV + B_KV_WIDTH
OFF_C = OFF_KV + 2 * B_KV_WIDTH
OFF_G = OFF_C + 4 * C_WIDTH
IN_WIDTH = OFF_G + C_WIDTH
N_GROUPS = 4
EXPERTS_PER_GROUP = 8
N_EXPERTS = N_GROUPS * EXPERTS_PER_GROUP
TOP_K = 2
D_FF_EXPERT = D_MODEL // 2

MOD_ROWS = 16
ROUTER_LANES = 128
HGRN_BLOCK = 32
HGRN_GROUP = 4
LOG2E = 1.4426950408889634
SAFE_SHIFT = 60
SAFE_DECAY = 80.0
MOE_ROWS = 256
ROUTE_ROWS = 512
SC_WINDOW = 128
SC_ROW = 256
PACKED = D_MODEL // 2
VMEM_LIMIT = 48 * 1024 * 1024


def _cparams(*sem):
    return pltpu.CompilerParams(dimension_semantics=sem, vmem_limit_bytes=VMEM_LIMIT)


def _head_ones(n, dtype):
    r = lax.broadcasted_iota(jnp.int32, (n, n), 0) >> 6
    c = lax.broadcasted_iota(jnp.int32, (n, n), 1) >> 6
    return (r == c).astype(dtype)


def _head_sum(x, ones_bd):
    hi = x.astype(BF16)
    lo = (x - hi.astype(F32)).astype(BF16)
    return (jnp.dot(hi, ones_bd, preferred_element_type=F32)
            + jnp.dot(lo, ones_bd, preferred_element_type=F32))


def _head_rms(x, ones_bd):
    return x * lax.rsqrt(_head_sum(x * x, ones_bd) * (1.0 / HEAD_DIM) + EPS)


def _pack_rows(y):
    bits = lax.bitcast_convert_type(y.astype(BF16).astype(F32), jnp.uint32)
    half = y.shape[1] // 2
    return lax.bitcast_convert_type(bits[:, :half] | (bits[:, half:] >> 16), F32)


def _unpack_rows(w):
    bits = lax.bitcast_convert_type(w, jnp.uint32)
    hi = lax.bitcast_convert_type(bits & jnp.uint32(0xFFFF0000), F32)
    lo = lax.bitcast_convert_type(bits << 16, F32)
    return hi, lo


def _mod_kernel(c_ref, w_ref, b_ref, o_ref):
    a = jax.nn.silu(c_ref[...])
    o_ref[...] = jnp.dot(a, w_ref[...], preferred_element_type=F32,
                         precision=lax.Precision.HIGHEST) + b_ref[...]


def _mod(cc, w_mod, b_mod):
    n = w_mod.shape[1]
    tn = 1536
    return pl.pallas_call(
        _mod_kernel,
        out_shape=jax.ShapeDtypeStruct((MOD_ROWS, n), F32),
        grid=(n // tn,),
        in_specs=[pl.BlockSpec((MOD_ROWS, D_MODEL), lambda j: (0, 0)),
                  pl.BlockSpec((D_MODEL, tn), lambda j: (0, j)),
                  pl.BlockSpec((1, tn), lambda j: (0, j))],
        out_specs=pl.BlockSpec((MOD_ROWS, tn), lambda j: (0, j)),
        compiler_params=_cparams("arbitrary"),
        name="mod",
    )(cc, w_mod, b_mod.reshape(1, n))


def _rope(xn, c_ref, sp_ref, sm_ref):
    w = xn.shape[-1]
    return (xn * c_ref[...] + pltpu.roll(xn, 16, 1) * sp_ref[...]
            + pltpu.roll(xn, w - 16, 1) * sm_ref[...])


def _inproj_kernel(x_ref, mul_ref, add_ref, w_ref, qc_ref, qsp_ref, qsm_ref, kc_ref, ksp_ref, ksm_ref,
                   za_ref, q_ref, k_ref, v_ref, zc_ref, g_ref):
    x = x_ref[0]
    ms = jnp.mean(x * x, axis=-1, keepdims=True)
    h = x * lax.rsqrt(ms + EPS) * mul_ref[0] + add_ref[0]
    y = jnp.dot(h.astype(BF16), w_ref[...], preferred_element_type=F32)
    za_ref[0] = y[:, :OFF_B].astype(BF16)
    qn = _head_rms(y[:, OFF_B:OFF_KV], _head_ones(B_WIDTH, BF16))
    q_ref[0] = _rope(qn, qc_ref, qsp_ref, qsm_ref).astype(BF16)
    kn = _head_rms(y[:, OFF_KV:OFF_V], _head_ones(B_KV_WIDTH, BF16))
    k_ref[0] = _rope(kn, kc_ref, ksp_ref, ksm_ref).astype(BF16)
    v_ref[0] = y[:, OFF_V:OFF_C].astype(BF16)
    zc_ref[0] = y[:, OFF_C:OFF_G]
    g_ref[0] = y[:, OFF_G:]


def _inproj(x, mul, add, w_bf, q_tabs, k_tabs):
    b, t, d = x.shape
    tm = min(256, t)
    row = lambda i, bb: (bb, i, 0)
    vec = lambda i, bb: (bb, 0, 0)
    tab = lambda i, bb: (i, 0)
    widths = (OFF_B, B_WIDTH, B_KV_WIDTH, B_KV_WIDTH, 4 * C_WIDTH, C_WIDTH)
    dtypes = (BF16, BF16, BF16, BF16, F32, F32)
    return pl.pallas_call(
        _inproj_kernel,
        out_shape=[jax.ShapeDtypeStruct((b, t, w), dt) for w, dt in zip(widths, dtypes)],
        grid=(t // tm, b),
        in_specs=[pl.BlockSpec((1, tm, d), row),
                  pl.BlockSpec((1, 1, d), vec),
                  pl.BlockSpec((1, 1, d), vec),
                  pl.BlockSpec((d, IN_WIDTH), lambda i, bb: (0, 0))]
                 + [pl.BlockSpec((tm, B_WIDTH), tab)] * 3
                 + [pl.BlockSpec((tm, B_KV_WIDTH), tab)] * 3,
        out_specs=[pl.BlockSpec((1, tm, w), row) for w in widths],
        compiler_params=_cparams("arbitrary", "arbitrary"),
        name="inproj",
    )(x, mul, add, w_bf, *q_tabs, *k_tabs)


def _rope_tables(t, w, scale, width, rotate):
    ws = w.astype(F32) * scale
    if not rotate:
        c = jnp.broadcast_to(jnp.tile(ws, width // HEAD_DIM)[None, :], (t, width))
        z = jnp.zeros((t, width), F32)
        return c, z, z
    pos = jnp.arange(t)
    row = (pos // GRID_W).astype(F32)
    col = (pos % GRID_W).astype(F32)
    inv_freq = 1.0 / (ROPE_BASE ** (jnp.arange(0, HEAD_DIM // 2, 2, dtype=F32) / (HEAD_DIM // 2)))
    dd = jnp.arange(HEAD_DIM)
    axis = dd // 32
    half = (dd % 32) // 16
    ang = jnp.where(axis[None, :] == 0, row[:, None], col[:, None]) * inv_freq[dd % 16][None, :]
    cos, sin = jnp.cos(ang), jnp.sin(ang)
    c = cos * ws[None, :]
    sm = jnp.where(half[None, :] == 0, -sin * jnp.roll(ws, -16)[None, :], 0.0)
    sp = jnp.where(half[None, :] == 1, sin * jnp.roll(ws, 16)[None, :], 0.0)
    rep = width // HEAD_DIM
    return jnp.tile(c, (1, rep)), jnp.tile(sp, (1, rep)), jnp.tile(sm, (1, rep))


def _gmlp_kernel(z_ref, ws_ref, bias_ref, o_ref):
    gz = jax.nn.gelu(z_ref[0].astype(F32))
    u = gz[:, :A_WIDTH]
    vn = _head_rms(gz[:, A_WIDTH:], _head_ones(A_WIDTH, BF16))
    lane_head = lax.broadcasted_iota(jnp.int32, vn.shape, 1) >> 6
    acc = bias_ref[...]
    for hh in range(A_HEADS):
        vh = jnp.where(lane_head == hh, vn, 0.0).astype(BF16)
        acc = acc + jnp.dot(ws_ref[hh], vh, preferred_element_type=F32)
    o_ref[0] = (u * acc).astype(BF16)


def _gmlp(za, ws_bf, bias2d):
    b, t, _ = za.shape
    return pl.pallas_call(
        _gmlp_kernel,
        out_shape=jax.ShapeDtypeStruct((b, t, A_WIDTH), BF16),
        grid=(b, t // A_CHUNK),
        in_specs=[pl.BlockSpec((1, A_CHUNK, OFF_B), lambda bb, i: (bb, i, 0)),
                  pl.BlockSpec((A_HEADS, A_CHUNK, A_CHUNK), lambda bb, i: (0, 0, 0)),
                  pl.BlockSpec((A_CHUNK, A_WIDTH), lambda bb, i: (0, 0))],
        out_specs=pl.BlockSpec((1, A_CHUNK, A_WIDTH), lambda bb, i: (bb, i, 0)),
        compiler_params=_cparams("arbitrary", "arbitrary"),
        name="gmlp",
    )(za, ws_bf, bias2d)


def _attn_kernel(flag_ref, q_ref, *refs, n_seg):
    kv_refs, o_ref = refs[:2 * n_seg], refs[2 * n_seg]
    tq = q_ref.shape[1]
    dh = HEAD_DIM

    def heads(j):
        q4 = jnp.concatenate([q_ref[0, :, (B_GROUP * j + gg) * dh:(B_GROUP * j + gg + 1) * dh]
                              for gg in range(B_GROUP)], axis=0)
        ks = [kv_refs[2 * sg][0, :, j * dh:(j + 1) * dh] for sg in range(n_seg)]
        vs = [kv_refs[2 * sg + 1][0, :, j * dh:(j + 1) * dh] for sg in range(n_seg)]
        return q4, ks, vs

    def scores(q4, ks):
        return [lax.dot_general(q4, kk, (((1,), (1,)), ((), ())), preferred_element_type=F32) for kk in ks]

    def finish(j, ps, vs):
        l = sum(jnp.sum(p, axis=-1, keepdims=True) for p in ps)
        acc = sum(jnp.dot(p.astype(BF16), vv, preferred_element_type=F32) for p, vv in zip(ps, vs))
        o = acc / l
        for gg in range(B_GROUP):
            hh = B_GROUP * j + gg
            o_ref[0, :, hh * dh:(hh + 1) * dh] = o[gg * tq:(gg + 1) * tq].astype(BF16)

    @pl.when(flag_ref[0] > 0)
    def _():
        shift = flag_ref[1].astype(F32)
        for j in range(B_KV_HEADS):
            q4, ks, vs = heads(j)
            finish(j, [jnp.exp2(s - shift) for s in scores(q4, ks)], vs)

    @pl.when(flag_ref[0] <= 0)
    def _():
        for j in range(B_KV_HEADS):
            q4, ks, vs = heads(j)
            ss = scores(q4, ks)
            m = functools.reduce(jnp.maximum, [jnp.max(s, axis=-1, keepdims=True) for s in ss])
            finish(j, [jnp.exp2(s - m) for s in ss], vs)


def _attn(flag, q, kv_segs):
    b, t, w = q.shape
    tq = min(128, t)
    n_seg = len(kv_segs)
    kv_flat, kv_specs = [], []
    for kk, vv in kv_segs:
        s_len, kw = kk.shape[1:]
        kv_flat += [kk, vv]
        kv_specs += [pl.BlockSpec((1, s_len, kw), lambda bb, i, fl: (bb, 0, 0))] * 2
    return pl.pallas_call(
        functools.partial(_attn_kernel, n_seg=n_seg),
        out_shape=jax.ShapeDtypeStruct((b, t, w), BF16),
        grid_spec=pltpu.PrefetchScalarGridSpec(
            num_scalar_prefetch=1,
            grid=(b, t // tq),
            in_specs=[pl.BlockSpec((1, tq, w), lambda bb, i, fl: (bb, i, 0))] + kv_specs,
            out_specs=pl.BlockSpec((1, tq, w), lambda bb, i, fl: (bb, i, 0))),
        compiler_params=_cparams("arbitrary", "arbitrary"),
        name="attn",
    )(flag, q, *kv_flat)


def _scan_rows(x, reverse):
    n = x.shape[0]
    rows = lax.broadcasted_iota(jnp.int32, x.shape, 0)
    sh = 1
    while sh < n:
        if reverse:
            x = x + jnp.where(rows < n - sh, pltpu.roll(x, n - sh, 0), 0.0)
        else:
            x = x + jnp.where(rows >= sh, pltpu.roll(x, sh, 0), 0.0)
        sh *= 2
    return x


def _stack_heads(x, lane_head):
    return jnp.concatenate([jnp.where(lane_head == hh, x, 0.0) for hh in range(C_HEADS)], axis=0)


def _hgrn_kernel(z_ref, zc_ref, lbc_ref, *refs, ctx_out):
    if ctx_out:
        o_ref, oc_ref, st_ref, kx_ref, bx_ref, vx_ref, flag_ref = refs
    else:
        o_ref, st_ref, kx_ref, bx_ref, vx_ref, flag_ref = refs
        oc_ref = None
    n = C_WIDTH
    nb = HGRN_BLOCK
    nblk_c = zc_ref.shape[1] // nb
    nblk_l = z_ref.shape[1] // nb
    ones_bd = _head_ones(n, BF16)
    rows = lax.broadcasted_iota(jnp.int32, (nb, n), 0)
    lane_head = lax.broadcasted_iota(jnp.int32, (nb, n), 1) >> 6
    lane_head64 = lax.broadcasted_iota(jnp.int32, (HEAD_DIM, n), 1) >> 6
    diag64 = (lax.broadcasted_iota(jnp.int32, (HEAD_DIM, n), 0)
              == (lax.broadcasted_iota(jnp.int32, (HEAD_DIM, n), 1) & (HEAD_DIM - 1)))
    low_half = (lax.broadcasted_iota(jnp.int32, (HEAD_DIM, 2 * HEAD_DIM), 1) < HEAD_DIM)
    sc_t = lax.broadcasted_iota(jnp.int32, (nb, C_HEADS * nb), 0)
    sc_s = lax.broadcasted_iota(jnp.int32, (nb, C_HEADS * nb), 1) & (nb - 1)

    def flag_blocks(src_ref, nblk, base):
        def body(i, carry):
            r0 = pl.multiple_of(i * nb, nb)
            for d in range(2):
                zz = src_ref[0, pl.ds(r0, nb), (1 + d) * n:(2 + d) * n]
                step_bound = jnp.minimum(lbc_ref[d, 4:5, :],
                                         jnp.maximum(-zz, 0.0) + (jnp.log(2.0) - lbc_ref[d, 1:2, :]))
                bound = jnp.sum(step_bound, axis=0, keepdims=True)
                flag_ref[d, base + i] = (jnp.max(bound) <= SAFE_DECAY).astype(jnp.int32)
            return carry
        lax.fori_loop(0, nblk, body, 0)

    flag_blocks(zc_ref, nblk_c, 0)
    flag_blocks(z_ref, nblk_l, nblk_c)
    st_ref[...] = jnp.zeros_like(st_ref)
    o_ref[...] = jnp.zeros_like(o_ref)
    if ctx_out:
        oc_ref[...] = jnp.zeros_like(oc_ref)

    def step(src_ref, dst_ref, blk, d, fast):
        reverse = d == 1
        r0 = pl.multiple_of(blk * nb, nb)
        z = src_ref[0, pl.ds(r0, nb), (1 + d) * n:(2 + d) * n]
        v = src_ref[0, pl.ds(r0, nb), 3 * n:4 * n]
        one_m_lb = lbc_ref[d, 0:1, :]
        log1m_lb = lbc_ref[d, 1:2, :]
        log_lb = lbc_ref[d, 2:3, :]
        lb_pos = lbc_ref[d, 3:4, :] > 0.5
        soft = jnp.log1p(jnp.exp(-jnp.abs(z)))
        log_rest = log1m_lb + (jnp.minimum(z, 0.0) - soft)
        lse = jnp.maximum(log_lb, log_rest) + jnp.log1p(jnp.exp(-jnp.abs(log_lb - log_rest)))
        log_f = jnp.where(lb_pos, lse, log_rest)
        k = one_m_lb * jnp.exp(jnp.minimum(-z, 0.0) - soft)
        bc = _scan_rows(log_f, reverse)
        edge = 0 if reverse else nb - 1
        b_edge = bc[edge:edge + 1, :]
        st = st_ref[d]
        v_bf = v.astype(BF16)

        if dst_ref is not None:
            q = jax.nn.silu(src_ref[0, pl.ds(r0, nb), 0:n])
            qt = (q * jnp.exp(bc)).astype(BF16)
            o = jnp.dot(qt, _stack_heads(st, lane_head64).astype(BF16), preferred_element_type=F32)

            def intra_fast():
                kt = _stack_heads(k * jnp.exp(-bc), lane_head).astype(BF16)
                sc = lax.dot_general(qt, kt, (((1,), (1,)), ((), ())), preferred_element_type=F32)
                keep = (sc_s >= sc_t) if reverse else (sc_s <= sc_t)
                sc = jnp.where(keep, sc, 0.0).astype(BF16)
                return jnp.dot(sc, _stack_heads(v, lane_head).astype(BF16), preferred_element_type=F32)

            def intra_exact():
                kx_ref[d] = k
                bx_ref[d] = bc
                vx_ref[d] = v

                def sbody(s, acc):
                    keep = (rows <= s) if reverse else (rows >= s)
                    e = jnp.exp(jnp.where(keep, bc - bx_ref[d, pl.ds(s, 1), :], 0.0))
                    p = jnp.where(keep, q * e * kx_ref[d, pl.ds(s, 1), :], 0.0)
                    sc = jnp.dot(p.astype(BF16), ones_bd, preferred_element_type=F32)
                    return acc + sc * vx_ref[d, pl.ds(s, 1), :]

                return lax.fori_loop(0, nb, sbody, jnp.zeros((nb, n), F32))

            o = o + (intra_fast() if fast else intra_exact())
            dst_ref[0, pl.ds(r0, nb), :] += o

        decay = _head_sum(jnp.where(diag64, jnp.exp(b_edge), 0.0), ones_bd)
        kd = (k * jnp.exp(b_edge - bc)).astype(BF16)
        full = lax.dot_general(kd, v_bf, (((0,), (0,)), ((), ())), preferred_element_type=F32)
        upd = jnp.concatenate(
            [jnp.where(low_half,
                       full[(2 * c) * HEAD_DIM:(2 * c + 1) * HEAD_DIM, 2 * c * HEAD_DIM:(2 * c + 2) * HEAD_DIM],
                       full[(2 * c + 1) * HEAD_DIM:(2 * c + 2) * HEAD_DIM, 2 * c * HEAD_DIM:(2 * c + 2) * HEAD_DIM])
             for c in range(C_HEADS // 2)], axis=1)
        st_ref[d] = st * decay + upd

    def run(src_ref, dst_ref, nblk, base):
        grp = HGRN_GROUP if nblk % HGRN_GROUP == 0 else 1

        def body(i, carry):
            fwd = [i * grp + gg for gg in range(grp)]
            bwd = [nblk - 1 - blk for blk in fwd]
            safe = functools.reduce(jnp.minimum, [flag_ref[0, base + blk] for blk in fwd]
                                    + [flag_ref[1, base + blk] for blk in bwd])

            def group(fast):
                for bf, bb in zip(fwd, bwd):
                    step(src_ref, dst_ref, bf, 0, fast)
                    step(src_ref, dst_ref, bb, 1, fast)

            pl.when(safe > 0)(functools.partial(group, True))
            pl.when(safe <= 0)(functools.partial(group, False))
            return carry
        lax.fori_loop(0, nblk // grp, body, 0)

    run(zc_ref, oc_ref, nblk_c, 0)
    run(z_ref, o_ref, nblk_l, nblk_c)


def _hgrn(zc, zc_c, lbc, ctx_out):
    b, t, w = zc.shape
    lc = zc_c.shape[1]
    n = C_WIDTH
    row = lambda bb: (bb, 0, 0)
    out_shape = [jax.ShapeDtypeStruct((b, t, n), F32)]
    out_specs = [pl.BlockSpec((1, t, n), row)]
    if ctx_out:
        out_shape.append(jax.ShapeDtypeStruct((b, lc, n), F32))
        out_specs.append(pl.BlockSpec((1, lc, n), row))
    res = pl.pallas_call(
        functools.partial(_hgrn_kernel, ctx_out=ctx_out),
        out_shape=out_shape,
        grid=(b,),
        in_specs=[pl.BlockSpec((1, t, w), row),
                  pl.BlockSpec((1, lc, w), row),
                  pl.BlockSpec((2, 8, n), lambda bb: (0, 0, 0))],
        out_specs=out_specs,
        scratch_shapes=[pltpu.VMEM((2, HEAD_DIM, n), F32)]
                       + [pltpu.VMEM((2, HGRN_BLOCK, n), F32)] * 3
                       + [pltpu.SMEM((2, (t + lc) // HGRN_BLOCK), jnp.int32)],
        compiler_params=_cparams("arbitrary"),
        name="hgrn",
    )(zc, zc_c, lbc)
    return (res[0], res[1]) if ctx_out else (res[0], None)


def _outproj_kernel(x_ref, ya_ref, yb_ref, o_ref, g_ref, w_ref, gate_ref, mul_ref, add_ref, hw_ref,
                    wr_ref, br_ref, xo_ref, h2_ref, lg_ref):
    yc = _head_rms(o_ref[0], _head_ones(C_WIDTH, BF16)) * hw_ref[...] * jax.nn.silu(g_ref[0])
    y = jnp.dot(ya_ref[0], w_ref[0:A_WIDTH, :], preferred_element_type=F32)
    y = y + jnp.dot(yb_ref[0], w_ref[A_WIDTH:A_WIDTH + B_WIDTH, :], preferred_element_type=F32)
    y = y + jnp.dot(yc.astype(BF16), w_ref[A_WIDTH + B_WIDTH:, :], preferred_element_type=F32)
    xn = x_ref[0] + gate_ref[0] * y
    xo_ref[0] = xn
    ms = jnp.mean(xn * xn, axis=-1, keepdims=True)
    h2 = xn * lax.rsqrt(ms + EPS) * mul_ref[0] + add_ref[0]
    h2_ref[0] = _pack_rows(h2)
    lg_ref[0] = jnp.dot(h2, wr_ref[...], preferred_element_type=F32,
                        precision=lax.Precision.HIGHEST) + br_ref[...]


def _outproj(x, ya, yb, o, g, w_bf, gate, mul, add, hw, wr, br):
    b, t, d = x.shape
    tm = min(256, t)
    row = lambda bb, i: (bb, i, 0)
    vec = lambda bb, i: (bb, 0, 0)
    const = lambda bb, i: (0, 0)
    return pl.pallas_call(
        _outproj_kernel,
        out_shape=[jax.ShapeDtypeStruct((b, t, d), F32),
                   jax.ShapeDtypeStruct((b, t, PACKED), F32),
                   jax.ShapeDtypeStruct((b, t, ROUTER_LANES), F32)],
        grid=(b, t // tm),
        in_specs=[pl.BlockSpec((1, tm, d), row),
                  pl.BlockSpec((1, tm, A_WIDTH), row),
                  pl.BlockSpec((1, tm, B_WIDTH), row),
                  pl.BlockSpec((1, tm, C_WIDTH), row),
                  pl.BlockSpec((1, tm, C_WIDTH), row),
                  pl.BlockSpec((d, d), const),
                  pl.BlockSpec((1, 1, d), vec),
                  pl.BlockSpec((1, 1, d), vec),
                  pl.BlockSpec((1, 1, d), vec),
                  pl.BlockSpec((1, C_WIDTH), const),
                  pl.BlockSpec((d, ROUTER_LANES), const),
                  pl.BlockSpec((1, ROUTER_LANES), const)],
        out_specs=[pl.BlockSpec((1, tm, d), row),
                   pl.BlockSpec((1, tm, PACKED), row),
                   pl.BlockSpec((1, tm, ROUTER_LANES), row)],
        compiler_params=_cparams("arbitrary", "arbitrary"),
        name="outproj",
    )(x, ya, yb, o, g, w_bf, gate, mul, add, hw, wr, br)


def _route_kernel(lg_ref, info_ref, meta_ref, cnt_ref, base_ref):
    phase = pl.program_id(0)
    i = pl.program_id(1)
    tm = lg_ref.shape[0]
    lane = lax.broadcasted_iota(jnp.int32, (tm, ROUTER_LANES), 1)
    lane_f = lane.astype(F32)
    lg = lg_ref[...]
    neg = -jnp.inf
    gl = jnp.where(lane < N_GROUPS, lg, neg)
    gmax = jnp.max(gl, axis=1, keepdims=True)
    grp = jnp.min(jnp.where(gl == gmax, lane_f, float(ROUTER_LANES)), axis=1, keepdims=True).astype(jnp.int32)
    p_grp = 1.0 / jnp.sum(jnp.exp(gl - gmax), axis=1, keepdims=True)
    in_grp = (lane >= N_GROUPS) & (lane < N_GROUPS + N_EXPERTS) & (((lane - N_GROUPS) >> 3) == grp)
    el = jnp.where(in_grp, lg, neg)
    v1 = jnp.max(el, axis=1, keepdims=True)
    i1 = jnp.min(jnp.where(el == v1, lane_f, float(ROUTER_LANES)), axis=1, keepdims=True)
    el2 = jnp.where(lane_f == i1, neg, el)
    v2 = jnp.max(el2, axis=1, keepdims=True)
    i2 = jnp.min(jnp.where(el2 == v2, lane_f, float(ROUTER_LANES)), axis=1, keepdims=True)
    hit1 = lane_f == i1
    hit2 = lane_f == i2
    onehot = (hit1 | hit2).astype(F32)
    tile_counts = jnp.sum(onehot, axis=0, keepdims=True)

    @pl.when(phase == 0)
    def _():
        @pl.when(i == 0)
        def _():
            cnt_ref[...] = jnp.zeros_like(cnt_ref)
        cnt_ref[...] += tile_counts
        info_ref[...] = jnp.zeros_like(info_ref)
        meta_ref[...] = jnp.zeros_like(meta_ref)

    @pl.when(phase == 1)
    def _():
        @pl.when(i == 0)
        def _():
            counts = jnp.broadcast_to(cnt_ref[...], (8, ROUTER_LANES))
            padded = jnp.floor((counts + (MOE_ROWS - 1.0)) * (1.0 / MOE_ROWS)) * MOE_ROWS
            r = lax.broadcasted_iota(jnp.int32, (ROUTER_LANES, ROUTER_LANES), 0)
            c = lax.broadcasted_iota(jnp.int32, (ROUTER_LANES, ROUTER_LANES), 1)
            ends = jnp.dot(padded, (r <= c).astype(F32), preferred_element_type=F32,
                           precision=lax.Precision.HIGHEST)
            base_ref[...] = (ends - padded)[0:1]
            row = lax.broadcasted_iota(jnp.int32, (8, ROUTER_LANES), 0)
            meta_ref[...] = jnp.where(row == 0, counts, jnp.where(row == 1, ends - padded, ends))

        tr = lax.broadcasted_iota(jnp.int32, (tm, tm), 0)
        tc = lax.broadcasted_iota(jnp.int32, (tm, tm), 1)
        before = jnp.dot((tc < tr).astype(BF16), onehot.astype(BF16), preferred_element_type=F32)
        pos = base_ref[...] + before
        d1 = jnp.sum(jnp.where(hit1, pos, 0.0), axis=1, keepdims=True)
        d2 = jnp.sum(jnp.where(hit2, pos, 0.0), axis=1, keepdims=True)
        base_ref[...] += tile_counts
        rr = jnp.exp(v2 - v1)
        g1 = p_grp / (1.0 + rr)
        g2 = p_grp * rr / (1.0 + rr)
        info_ref[...] = jnp.where(lane == 0, d1, jnp.where(lane == 1, d2, jnp.where(
            lane == 2, g1, jnp.where(lane == 3, g2, 0.0))))


def _route(logits):
    n = logits.shape[0]
    tm = ROUTE_ROWS if n % ROUTE_ROWS == 0 else MOE_ROWS
    return pl.pallas_call(
        _route_kernel,
        out_shape=[jax.ShapeDtypeStruct((n, ROUTER_LANES), F32),
                   jax.ShapeDtypeStruct((8, ROUTER_LANES), F32)],
        grid=(2, n // tm),
        in_specs=[pl.BlockSpec((tm, ROUTER_LANES), lambda p, i: (i, 0))],
        out_specs=[pl.BlockSpec((tm, ROUTER_LANES), lambda p, i: (p * i, 0)),
                   pl.BlockSpec((8, ROUTER_LANES), lambda p, i: (0, 0))],
        scratch_shapes=[pltpu.VMEM((1, ROUTER_LANES), F32), pltpu.VMEM((1, ROUTER_LANES), F32)],
        compiler_params=_cparams("arbitrary", "arbitrary"),
        name="route",
    )(logits)


def _sc_mesh():
    return plsc.VectorSubcoreMesh(core_axis_name="c", subcore_axis_name="s")


def _sc_gather(table, idx):
    n = idx.shape[0]
    d = table.shape[1]

    @functools.partial(pl.kernel, out_type=jax.ShapeDtypeStruct((n, d), table.dtype), mesh=_sc_mesh())
    def gather(x_hbm, i_hbm, o_hbm):
        def body(i_vmem, o_vmem):
            pltpu.sync_copy(x_hbm.at[i_vmem.at[0]], o_vmem)

        pltpu.emit_pipeline(
            body,
            grid=(n // SC_WINDOW,),
            in_specs=[pl.BlockSpec((1, SC_WINDOW), lambda i: (0, i))],
            out_specs=[pl.BlockSpec((SC_WINDOW, d), lambda i: (i, 0))],
            core_axis_name=("c", "s"),
            dimension_semantics=(pltpu.PARALLEL,),
        )(i_hbm, o_hbm)

    return gather(table, idx.reshape(1, n))


def _sc_scatter2(rows, idx0, idx1, n_out):
    m, d = rows.shape

    @functools.partial(pl.kernel, out_type=jax.ShapeDtypeStruct((n_out, d), rows.dtype), mesh=_sc_mesh())
    def scatter(x_hbm, i0_hbm, i1_hbm, o_hbm):
        def body(x_vmem, i0_vmem, i1_vmem):
            pltpu.sync_copy(x_vmem, o_hbm.at[i0_vmem.at[0]])
            pltpu.sync_copy(x_vmem, o_hbm.at[i1_vmem.at[0]])

        pltpu.emit_pipeline(
            body,
            grid=(m // SC_WINDOW,),
            in_specs=[pl.BlockSpec((SC_WINDOW, d), lambda i: (i, 0)),
                      pl.BlockSpec((1, SC_WINDOW), lambda i: (0, i)),
                      pl.BlockSpec((1, SC_WINDOW), lambda i: (0, i))],
            out_specs=[],
            core_axis_name=("c", "s"),
            dimension_semantics=(pltpu.PARALLEL,),
        )(x_hbm, i0_hbm, i1_hbm)

    return scatter(rows, idx0.reshape(1, m), idx1.reshape(1, m))


def _moe_kernel(be_ref, nu_ref, x_ref, w1_ref, w3_ref, w2_ref, o_ref, w1b, w3b, w2b):
    i = pl.program_id(0)
    e = be_ref[i]
    prev = be_ref[jnp.maximum(i - 1, 0)]

    @pl.when((i == 0) | (e != prev))
    def _():
        w1b[...] = w1_ref[0].astype(BF16)
        w3b[...] = w3_ref[0].astype(BF16)
        w2b[...] = w2_ref[0].astype(BF16)

    @pl.when(i < nu_ref[0])
    def _():
        hi, lo = _unpack_rows(x_ref[...])
        hi, lo = hi.astype(BF16), lo.astype(BF16)
        a = (jnp.dot(hi, w1b[0:PACKED, :], preferred_element_type=F32)
             + jnp.dot(lo, w1b[PACKED:, :], preferred_element_type=F32))
        b = (jnp.dot(hi, w3b[0:PACKED, :], preferred_element_type=F32)
             + jnp.dot(lo, w3b[PACKED:, :], preferred_element_type=F32))
        hmid = (jax.nn.silu(a) * b).astype(BF16)
        o_ref[...] = _pack_rows(jnp.dot(hmid, w2b[...], preferred_element_type=F32))

    @pl.when(i >= nu_ref[0])
    def _():
        o_ref[...] = jnp.zeros_like(o_ref)


def _moe_mlp(blk_expert, n_used, xs, w1, w3, w2):
    n_rows = xs.shape[0]
    d, f = w1.shape[1:]
    nblk = n_rows // MOE_ROWS
    return pl.pallas_call(
        _moe_kernel,
        out_shape=jax.ShapeDtypeStruct((n_rows, PACKED), F32),
        grid_spec=pltpu.PrefetchScalarGridSpec(
            num_scalar_prefetch=2,
            grid=(nblk,),
            in_specs=[pl.BlockSpec((MOE_ROWS, PACKED), lambda i, be, nu: (i, 0)),
                      pl.BlockSpec((1, d, f), lambda i, be, nu: (be[i], 0, 0)),
                      pl.BlockSpec((1, d, f), lambda i, be, nu: (be[i], 0, 0)),
                      pl.BlockSpec((1, f, d), lambda i, be, nu: (be[i], 0, 0))],
            out_specs=pl.BlockSpec((MOE_ROWS, PACKED), lambda i, be, nu: (i, 0)),
            scratch_shapes=[pltpu.VMEM((d, f), BF16), pltpu.VMEM((d, f), BF16), pltpu.VMEM((f, d), BF16)]),
        compiler_params=_cparams("arbitrary"),
        name="moe",
    )(blk_expert, n_used, xs, w1, w3, w2)


def _combine_kernel(x_ref, pk_ref, info_ref, g_ref, o_ref):
    info = info_ref[...]
    g1 = info[:, 2:3]
    g2 = info[:, 3:4]
    hi1, lo1 = _unpack_rows(pk_ref[:, 0:PACKED])
    hi2, lo2 = _unpack_rows(pk_ref[:, PACKED:])
    m = jnp.concatenate([g1 * hi1 + g2 * hi2, g1 * lo1 + g2 * lo2], axis=1)
    o_ref[0] = x_ref[0] + g_ref[0] * m


def _combine(x, picked, info, gate, row0):
    b, t, d = x.shape
    tm = min(256, t)
    off = row0 // tm
    tok = lambda bb, i: (off + bb * (t // tm) + i, 0)
    return pl.pallas_call(
        _combine_kernel,
        out_shape=jax.ShapeDtypeStruct((b, t, d), F32),
        grid=(b, t // tm),
        in_specs=[pl.BlockSpec((1, tm, d), lambda bb, i: (bb, i, 0)),
                  pl.BlockSpec((tm, 2 * PACKED), tok),
                  pl.BlockSpec((tm, ROUTER_LANES), tok),
                  pl.BlockSpec((1, 1, d), lambda bb, i: (bb, 0, 0))],
        out_specs=pl.BlockSpec((1, tm, d), lambda bb, i: (bb, i, 0)),
        compiler_params=_cparams("arbitrary", "arbitrary"),
        name="combine",
    )(x, picked, info, gate)


def _hier_moe(h2p, logits, w1, w3, w2):
    n_tok = h2p.shape[0]
    info, meta = _route(logits)
    dest = info[:, 0:TOP_K].astype(jnp.int32)
    pad_ends = meta[2, N_GROUPS:N_GROUPS + N_EXPERTS].astype(jnp.int32)
    nblk = -(-(n_tok * TOP_K) // MOE_ROWS) + N_EXPERTS
    blk_start = jnp.arange(nblk, dtype=jnp.int32) * MOE_ROWS
    blk_expert = jnp.minimum(jnp.sum((pad_ends[None, :] <= blk_start[:, None]).astype(jnp.int32), axis=1),
                             N_EXPERTS - 1)
    n_used = pad_ends[-1:] // MOE_ROWS
    halves = PACKED // SC_ROW
    two = jnp.arange(halves, dtype=jnp.int32)
    idx = [(halves * dest[:, s, None] + two).reshape(-1) for s in range(TOP_K)]
    xs = _sc_scatter2(h2p.reshape(n_tok * halves, SC_ROW), idx[0], idx[1], nblk * MOE_ROWS * halves)
    out = _moe_mlp(blk_expert, n_used, xs.reshape(nblk * MOE_ROWS, PACKED), w1, w3, w2)
    idx_all = (halves * dest[:, :, None] + two).reshape(-1)
    picked = _sc_gather(out.reshape(-1, SC_ROW), idx_all)
    return picked.reshape(n_tok, TOP_K * PACKED), info


def _layer(x, xc, c, c_ctx, lb, w_mod, b_mod, norm1_w, w_in, w_s, b_s, q_norm_w, k_norm_w, hgrn_norm_w, w_out,
           norm2_w, w_grp, b_grp, w_exp, b_exp, w1, w3, w2, ctx_out):
    b, t, d = x.shape
    lc = xc.shape[1]
    cc = jnp.zeros((MOD_ROWS, d), F32).at[:b].set(c).at[b].set(c_ctx)
    mod = _mod(cc, w_mod, b_mod)
    sh1, sc1, g1, sh2, sc2, g2 = [m[:, None, :] for m in jnp.split(mod[:b], 6, axis=-1)]
    mod_c = [jnp.broadcast_to(m[None, None, :], (b, 1, d)) for m in jnp.split(mod[b], 6)]

    w_in_bf = w_in.astype(BF16)
    scale = LOG2E * HEAD_DIM ** -0.5
    q_tabs = _rope_tables(t, q_norm_w, scale, B_WIDTH, True)
    k_tabs = _rope_tables(t, k_norm_w, 1.0, B_KV_WIDTH, True)
    qc_tabs = _rope_tables(lc, q_norm_w, scale, B_WIDTH, False)
    kc_tabs = _rope_tables(lc, k_norm_w, 1.0, B_KV_WIDTH, False)
    za, q, k, v, zc, g = _inproj(x, norm1_w * (1.0 + sc1), sh1, w_in_bf, q_tabs, k_tabs)
    za_c, q_c, k_c, v_c, zc_c, g_c = _inproj(xc, norm1_w * (1.0 + mod_c[1]), mod_c[0], w_in_bf,
                                             qc_tabs, kc_tabs)

    ws_bf = w_s.astype(BF16)
    bias2d = jnp.repeat(b_s.T, HEAD_DIM, axis=1)
    ya = _gmlp(za, ws_bf, bias2d)

    bound = LOG2E * HEAD_DIM ** 0.5 * jnp.max(jnp.abs(q_norm_w)) * jnp.max(jnp.abs(k_norm_w)) * 1.02
    shift = jnp.ceil(bound)
    attn_flag = jnp.stack([(shift <= SAFE_SHIFT).astype(jnp.int32), shift.astype(jnp.int32)])
    yb = _attn(attn_flag, q, [(k, v), (k_c, v_c)])

    pos = lb > 0.0
    log_lb = jnp.log(jnp.where(pos, lb, 1.0))
    lbc = jnp.stack([1.0 - lb, jnp.log1p(-lb), log_lb, pos.astype(F32), jnp.where(pos, -log_lb, 1e30)], axis=1)
    lbc = jnp.concatenate([lbc, jnp.zeros((2, 3, C_WIDTH), F32)], axis=1)
    o, o_c = _hgrn(zc, zc_c, lbc, ctx_out)

    w_out_bf = w_out.astype(BF16)
    hw = jnp.tile(hgrn_norm_w, C_HEADS)[None, :]
    wr = jnp.zeros((d, ROUTER_LANES), F32).at[:, :N_GROUPS].set(w_grp).at[
        :, N_GROUPS:N_GROUPS + N_EXPERTS].set(w_exp)
    br = jnp.zeros((1, ROUTER_LANES), F32).at[0, :N_GROUPS].set(b_grp).at[
        0, N_GROUPS:N_GROUPS + N_EXPERTS].set(b_exp)
    x, h2, lg = _outproj(x, ya, yb, o, g, w_out_bf, g1, norm2_w * (1.0 + sc2), sh2, hw, wr, br)
    if ctx_out:
        ya_c = _gmlp(za_c, ws_bf, bias2d)
        yb_c = _attn(attn_flag, q_c, [(k_c, v_c)])
        xc, h2c, lgc = _outproj(xc, ya_c, yb_c, o_c, g_c, w_out_bf, mod_c[2],
                                norm2_w * (1.0 + mod_c[4]), mod_c[3], hw, wr, br)
        tokens = jnp.concatenate([h2.reshape(-1, PACKED), h2c.reshape(-1, PACKED)], axis=0)
        logits = jnp.concatenate([lg.reshape(-1, ROUTER_LANES), lgc.reshape(-1, ROUTER_LANES)], axis=0)
        picked, info = _hier_moe(tokens, logits, w1, w3, w2)
        x = _combine(x, picked, info, g2, 0)
        xc = _combine(xc, picked, info, mod_c[5], b * t)
    else:
        picked, info = _hier_moe(h2.reshape(-1, PACKED), lg.reshape(-1, ROUTER_LANES), w1, w3, w2)
        x = _combine(x, picked, info, g2, 0)
    return x, xc


def kernel(x, c, ctx, c_ctx, w_mod, b_mod, norm1_w, w_in, w_s, b_s, q_norm_w, k_norm_w, hgrn_lb_logits,
           hgrn_norm_w, w_out, norm2_w, w_grp, b_grp, w_exp, b_exp, w1, w3, w2):
    depth = w_mod.shape[0]
    lb_sm = jax.nn.softmax(hgrn_lb_logits.astype(F32), axis=0)
    lb = jnp.cumsum(lb_sm, axis=0) - lb_sm[0]
    xc = ctx
    for l in range(depth):
        x, xc = _layer(x, xc, c, c_ctx, lb[l], w_mod[l], b_mod[l], norm1_w[l], w_in[l], w_s[l], b_s[l],
                       q_norm_w[l], k_norm_w[l], hgrn_norm_w[l], w_out[l], norm2_w[l], w_grp[l], b_grp[l],
                       w_exp[l], b_exp[l], w1[l], w3[l], w2[l], ctx_out=(l < depth - 1))
    return x
```

```python
import functools

import jax
import jax.numpy as jnp
from jax import lax
from jax.experimental import pallas as pl
from jax.experimental.pallas import tpu as pltpu
from jax.experimental.pallas import tpu_sc as plsc

F32 = jnp.float32
BF16 = jnp.bfloat16

D_MODEL = 1024
HEAD_DIM = 64
GRID_W = 64
EPS = 1e-6
ROPE_BASE = 10000.0
A_WIDTH = D_MODEL // 4
A_HEADS = A_WIDTH // HEAD_DIM
A_CHUNK = 128
B_WIDTH = D_MODEL // 2
B_HEADS = B_WIDTH // HEAD_DIM
B_KV_HEADS = 2
B_GROUP = B_HEADS // B_KV_HEADS
B_KV_WIDTH = B_KV_HEADS * HEAD_DIM
C_WIDTH = D_MODEL // 4
C_HEADS = C_WIDTH // HEAD_DIM
OFF_B = 2 * A_WIDTH
OFF_KV = OFF_B + B_WIDTH
OFF_V = OFF_KV + B_KV_WIDTH
OFF_C = OFF_KV + 2 * B_KV_WIDTH
OFF_G = OFF_C + 4 * C_WIDTH
IN_WIDTH = OFF_G + C_WIDTH
N_GROUPS = 4
EXPERTS_PER_GROUP = 8
N_EXPERTS = N_GROUPS * EXPERTS_PER_GROUP
TOP_K = 2
D_FF_EXPERT = D_MODEL // 2

MOD_ROWS = 16
ROUTER_LANES = 128
HGRN_BLOCK = 32
HGRN_GROUP = 4
LOG2E = 1.4426950408889634
SAFE_SHIFT = 60
SAFE_DECAY = 80.0
MOE_ROWS = 256
ROUTE_ROWS = 512
SC_WINDOW = 128
SC_ROW = 256
PLANES = D_MODEL // (2 * SC_ROW)
VMEM_LIMIT = 48 * 1024 * 1024


def _cparams(*sem):
    return pltpu.CompilerParams(dimension_semantics=sem, vmem_limit_bytes=VMEM_LIMIT)


def _head_ones(n, dtype):
    r = lax.broadcasted_iota(jnp.int32, (n, n), 0) >> 6
    c = lax.broadcasted_iota(jnp.int32, (n, n), 1) >> 6
    return (r == c).astype(dtype)


def _head_sum(x, ones_bd):
    hi = x.astype(BF16)
    lo = (x - hi.astype(F32)).astype(BF16)
    return (jnp.dot(hi, ones_bd, preferred_element_type=F32)
            + jnp.dot(lo, ones_bd, preferred_element_type=F32))


def _head_rms(x, ones_bd):
    return x * lax.rsqrt(_head_sum(x * x, ones_bd) * (1.0 / HEAD_DIM) + EPS)


def _pack_rows(y):
    bits = lax.bitcast_convert_type(y.astype(BF16).astype(F32), jnp.uint32)
    half = y.shape[1] // 2
    return lax.bitcast_convert_type(bits[:, :half] | (bits[:, half:] >> 16), F32)


def _unpack_rows(w):
    bits = lax.bitcast_convert_type(w, jnp.uint32)
    hi = lax.bitcast_convert_type(bits & jnp.uint32(0xFFFF0000), F32)
    lo = lax.bitcast_convert_type(bits << 16, F32)
    return hi, lo


def _pack_planes(y, ref, lead=()):
    for p in range(PLANES):
        ref[(p,) + lead] = _pack_rows(y[:, 2 * p * SC_ROW:(2 * p + 2) * SC_ROW])


def _mod_kernel(c_ref, w_ref, b_ref, o_ref):
    a = jax.nn.silu(c_ref[...])
    o_ref[...] = jnp.dot(a, w_ref[0], preferred_element_type=F32,
                         precision=lax.Precision.HIGHEST) + b_ref[...]


def _mod(cc, w_mod, b_mod, layer):
    n = w_mod.shape[2]
    tn = 1536
    return pl.pallas_call(
        _mod_kernel,
        out_shape=jax.ShapeDtypeStruct((MOD_ROWS, n), F32),
        grid=(n // tn,),
        in_specs=[pl.BlockSpec((MOD_ROWS, D_MODEL), lambda j: (0, 0)),
                  pl.BlockSpec((1, D_MODEL, tn), lambda j: (layer, 0, j)),
                  pl.BlockSpec((1, tn), lambda j: (0, j))],
        out_specs=pl.BlockSpec((MOD_ROWS, tn), lambda j: (0, j)),
        compiler_params=_cparams("arbitrary"),
        name="mod",
    )(cc, w_mod, b_mod.reshape(1, n))


def _rope(xn, c_ref, sp_ref, sm_ref):
    w = xn.shape[-1]
    return (xn * c_ref[...] + pltpu.roll(xn, 16, 1) * sp_ref[...]
            + pltpu.roll(xn, w - 16, 1) * sm_ref[...])


def _inproj_kernel(x_ref, mul_ref, add_ref, w_ref, qc_ref, qsp_ref, qsm_ref, kc_ref, ksp_ref, ksm_ref,
                   za_ref, q_ref, k_ref, v_ref, zc_ref, g_ref):
    x = x_ref[0]
    ms = jnp.mean(x * x, axis=-1, keepdims=True)
    h = x * lax.rsqrt(ms + EPS) * mul_ref[0] + add_ref[0]
    y = jnp.dot(h.astype(BF16), w_ref[...], preferred_element_type=F32)
    za_ref[0] = y[:, :OFF_B].astype(BF16)
    qn = _head_rms(y[:, OFF_B:OFF_KV], _head_ones(B_WIDTH, BF16))
    q_ref[0] = _rope(qn, qc_ref, qsp_ref, qsm_ref).astype(BF16)
    kn = _head_rms(y[:, OFF_KV:OFF_V], _head_ones(B_KV_WIDTH, BF16))
    k_ref[0] = _rope(kn, kc_ref, ksp_ref, ksm_ref).astype(BF16)
    v_ref[0] = y[:, OFF_V:OFF_C].astype(BF16)
    zc_ref[0] = y[:, OFF_C:OFF_G]
    g_ref[0] = y[:, OFF_G:]


def _inproj(x, mul, add, w_bf, q_tabs, k_tabs):
    b, t, d = x.shape
    tm = min(256, t)
    row = lambda i, bb: (bb, i, 0)
    vec = lambda i, bb: (bb, 0, 0)
    tab = lambda i, bb: (i, 0)
    widths = (OFF_B, B_WIDTH, B_KV_WIDTH, B_KV_WIDTH, 4 * C_WIDTH, C_WIDTH)
    dtypes = (BF16, BF16, BF16, BF16, F32, F32)
    return pl.pallas_call(
        _inproj_kernel,
        out_shape=[jax.ShapeDtypeStruct((b, t, w), dt) for w, dt in zip(widths, dtypes)],
        grid=(t // tm, b),
        in_specs=[pl.BlockSpec((1, tm, d), row),
                  pl.BlockSpec((1, 1, d), vec),
                  pl.BlockSpec((1, 1, d), vec),
                  pl.BlockSpec((d, IN_WIDTH), lambda i, bb: (0, 0))]
                 + [pl.BlockSpec((tm, B_WIDTH), tab)] * 3
                 + [pl.BlockSpec((tm, B_KV_WIDTH), tab)] * 3,
        out_specs=[pl.BlockSpec((1, tm, w), row) for w in widths],
        compiler_params=_cparams("arbitrary", "arbitrary"),
        name="inproj",
    )(x, mul, add, w_bf, *q_tabs, *k_tabs)


def _rope_tables(t, w, scale, width, rotate):
    ws = w.astype(F32) * scale
    if not rotate:
        c = jnp.broadcast_to(jnp.tile(ws, width // HEAD_DIM)[None, :], (t, width))
        z = jnp.zeros((t, width), F32)
        return c, z, z
    pos = jnp.arange(t)
    row = (pos // GRID_W).astype(F32)
    col = (pos % GRID_W).astype(F32)
    inv_freq = 1.0 / (ROPE_BASE ** (jnp.arange(0, HEAD_DIM // 2, 2, dtype=F32) / (HEAD_DIM // 2)))
    dd = jnp.arange(HEAD_DIM)
    axis = dd // 32
    half = (dd % 32) // 16
    ang = jnp.where(axis[None, :] == 0, row[:, None], col[:, None]) * inv_freq[dd % 16][None, :]
    cos, sin = jnp.cos(ang), jnp.sin(ang)
    c = cos * ws[None, :]
    sm = jnp.where(half[None, :] == 0, -sin * jnp.roll(ws, -16)[None, :], 0.0)
    sp = jnp.where(half[None, :] == 1, sin * jnp.roll(ws, 16)[None, :], 0.0)
    rep = width // HEAD_DIM
    return jnp.tile(c, (1, rep)), jnp.tile(sp, (1, rep)), jnp.tile(sm, (1, rep))


def _gmlp_kernel(z_ref, ws_ref, bias_ref, o_ref):
    gz = jax.nn.gelu(z_ref[0].astype(F32))
    u = gz[:, :A_WIDTH]
    vn = _head_rms(gz[:, A_WIDTH:], _head_ones(A_WIDTH, BF16))
    lane_head = lax.broadcasted_iota(jnp.int32, vn.shape, 1) >> 6
    acc = bias_ref[...]
    for hh in range(A_HEADS):
        vh = jnp.where(lane_head == hh, vn, 0.0).astype(BF16)
        acc = acc + jnp.dot(ws_ref[hh], vh, preferred_element_type=F32)
    o_ref[0] = (u * acc).astype(BF16)


def _gmlp(za, ws_bf, bias2d):
    b, t, _ = za.shape
    return pl.pallas_call(
        _gmlp_kernel,
        out_shape=jax.ShapeDtypeStruct((b, t, A_WIDTH), BF16),
        grid=(b, t // A_CHUNK),
        in_specs=[pl.BlockSpec((1, A_CHUNK, OFF_B), lambda bb, i: (bb, i, 0)),
                  pl.BlockSpec((A_HEADS, A_CHUNK, A_CHUNK), lambda bb, i: (0, 0, 0)),
                  pl.BlockSpec((A_CHUNK, A_WIDTH), lambda bb, i: (0, 0))],
        out_specs=pl.BlockSpec((1, A_CHUNK, A_WIDTH), lambda bb, i: (bb, i, 0)),
        compiler_params=_cparams("arbitrary", "arbitrary"),
        name="gmlp",
    )(za, ws_bf, bias2d)


def _attn_kernel(flag_ref, q_ref, *refs, n_seg):
    kv_refs, o_ref = refs[:2 * n_seg], refs[2 * n_seg]
    tq = q_ref.shape[1]
    dh = HEAD_DIM

    def heads(j):
        q4 = jnp.concatenate([q_ref[0, :, (B_GROUP * j + gg) * dh:(B_GROUP * j + gg + 1) * dh]
                              for gg in range(B_GROUP)], axis=0)
        ks = [kv_refs[2 * sg][0, :, j * dh:(j + 1) * dh] for sg in range(n_seg)]
        vs = [kv_refs[2 * sg + 1][0, :, j * dh:(j + 1) * dh] for sg in range(n_seg)]
        return q4, ks, vs

    def scores(q4, ks):
        return [lax.dot_general(q4, kk, (((1,), (1,)), ((), ())), preferred_element_type=F32) for kk in ks]

    def finish(j, ps, vs):
        l = sum(jnp.sum(p, axis=-1, keepdims=True) for p in ps)
        acc = sum(jnp.dot(p.astype(BF16), vv, preferred_element_type=F32) for p, vv in zip(ps, vs))
        o = acc / l
        for gg in range(B_GROUP):
            hh = B_GROUP * j + gg
            o_ref[0, :, hh * dh:(hh + 1) * dh] = o[gg * tq:(gg + 1) * tq].astype(BF16)

    @pl.when(flag_ref[0] > 0)
    def _():
        shift = flag_ref[1].astype(F32)
        for j in range(B_KV_HEADS):
            q4, ks, vs = heads(j)
            finish(j, [jnp.exp2(s - shift) for s in scores(q4, ks)], vs)

    @pl.when(flag_ref[0] <= 0)
    def _():
        for j in range(B_KV_HEADS):
            q4, ks, vs = heads(j)
            ss = scores(q4, ks)
            m = functools.reduce(jnp.maximum, [jnp.max(s, axis=-1, keepdims=True) for s in ss])
            finish(j, [jnp.exp2(s - m) for s in ss], vs)


def _attn(flag, q, kv_segs):
    b, t, w = q.shape
    tq = min(128, t)
    n_seg = len(kv_segs)
    kv_flat, kv_specs = [], []
    for kk, vv in kv_segs:
        s_len, kw = kk.shape[1:]
        kv_flat += [kk, vv]
        kv_specs += [pl.BlockSpec((1, s_len, kw), lambda bb, i, fl: (bb, 0, 0))] * 2
    return pl.pallas_call(
        functools.partial(_attn_kernel, n_seg=n_seg),
        out_shape=jax.ShapeDtypeStruct((b, t, w), BF16),
        grid_spec=pltpu.PrefetchScalarGridSpec(
            num_scalar_prefetch=1,
            grid=(b, t // tq),
            in_specs=[pl.BlockSpec((1, tq, w), lambda bb, i, fl: (bb, i, 0))] + kv_specs,
            out_specs=pl.BlockSpec((1, tq, w), lambda bb, i, fl: (bb, i, 0))),
        compiler_params=_cparams("arbitrary", "arbitrary"),
        name="attn",
    )(flag, q, *kv_flat)


def _scan_rows(x, reverse):
    n = x.shape[0]
    rows = lax.broadcasted_iota(jnp.int32, x.shape, 0)
    sh = 1
    while sh < n:
        if reverse:
            x = x + jnp.where(rows < n - sh, pltpu.roll(x, n - sh, 0), 0.0)
        else:
            x = x + jnp.where(rows >= sh, pltpu.roll(x, sh, 0), 0.0)
        sh *= 2
    return x


def _stack_heads(x, lane_head):
    return jnp.concatenate([jnp.where(lane_head == hh, x, 0.0) for hh in range(C_HEADS)], axis=0)


def _hgrn_kernel(z_ref, zc_ref, lbc_ref, *refs, ctx_out):
    if ctx_out:
        o_ref, oc_ref, st_ref, kx_ref, bx_ref, vx_ref, flag_ref = refs
    else:
        o_ref, st_ref, kx_ref, bx_ref, vx_ref, flag_ref = refs
        oc_ref = None
    n = C_WIDTH
    nb = HGRN_BLOCK
    nblk_c = zc_ref.shape[1] // nb
    nblk_l = z_ref.shape[1] // nb
    ones_bd = _head_ones(n, BF16)
    rows = lax.broadcasted_iota(jnp.int32, (nb, n), 0)
    lane_head = lax.broadcasted_iota(jnp.int32, (nb, n), 1) >> 6
    lane_head64 = lax.broadcasted_iota(jnp.int32, (HEAD_DIM, n), 1) >> 6
    diag64 = (lax.broadcasted_iota(jnp.int32, (HEAD_DIM, n), 0)
              == (lax.broadcasted_iota(jnp.int32, (HEAD_DIM, n), 1) & (HEAD_DIM - 1)))
    low_half = (lax.broadcasted_iota(jnp.int32, (HEAD_DIM, 2 * HEAD_DIM), 1) < HEAD_DIM)
    sc_t = lax.broadcasted_iota(jnp.int32, (nb, C_HEADS * nb), 0)
    sc_s = lax.broadcasted_iota(jnp.int32, (nb, C_HEADS * nb), 1) & (nb - 1)

    def flag_blocks(src_ref, nblk, base):
        def body(i, carry):
            r0 = pl.multiple_of(i * nb, nb)
            for d in range(2):
                zz = src_ref[0, pl.ds(r0, nb), (1 + d) * n:(2 + d) * n]
                step_bound = jnp.minimum(lbc_ref[d, 4:5, :],
                                         jnp.maximum(-zz, 0.0) + (jnp.log(2.0) - lbc_ref[d, 1:2, :]))
                bound = jnp.sum(step_bound, axis=0, keepdims=True)
                flag_ref[d, base + i] = (jnp.max(bound) <= SAFE_DECAY).astype(jnp.int32)
            return carry
        lax.fori_loop(0, nblk, body, 0)

    flag_blocks(zc_ref, nblk_c, 0)
    flag_blocks(z_ref, nblk_l, nblk_c)
    st_ref[...] = jnp.zeros_like(st_ref)
    o_ref[...] = jnp.zeros_like(o_ref)
    if ctx_out:
        oc_ref[...] = jnp.zeros_like(oc_ref)

    def step(src_ref, dst_ref, blk, d, fast):
        reverse = d == 1
        r0 = pl.multiple_of(blk * nb, nb)
        z = src_ref[0, pl.ds(r0, nb), (1 + d) * n:(2 + d) * n]
        v = src_ref[0, pl.ds(r0, nb), 3 * n:4 * n]
        one_m_lb = lbc_ref[d, 0:1, :]
        log1m_lb = lbc_ref[d, 1:2, :]
        log_lb = lbc_ref[d, 2:3, :]
        lb_pos = lbc_ref[d, 3:4, :] > 0.5
        soft = jnp.log1p(jnp.exp(-jnp.abs(z)))
        log_rest = log1m_lb + (jnp.minimum(z, 0.0) - soft)
        lse = jnp.maximum(log_lb, log_rest) + jnp.log1p(jnp.exp(-jnp.abs(log_lb - log_rest)))
        log_f = jnp.where(lb_pos, lse, log_rest)
        k = one_m_lb * jnp.exp(jnp.minimum(-z, 0.0) - soft)
        bc = _scan_rows(log_f, reverse)
        edge = 0 if reverse else nb - 1
        b_edge = bc[edge:edge + 1, :]
        st = st_ref[d]
        v_bf = v.astype(BF16)

        if dst_ref is not None:
            q = jax.nn.silu(src_ref[0, pl.ds(r0, nb), 0:n])
            qt = (q * jnp.exp(bc)).astype(BF16)
            o = jnp.dot(qt, _stack_heads(st, lane_head64).astype(BF16), preferred_element_type=F32)

            def intra_fast():
                kt = _stack_heads(k * jnp.exp(-bc), lane_head).astype(BF16)
                sc = lax.dot_general(qt, kt, (((1,), (1,)), ((), ())), preferred_element_type=F32)
                keep = (sc_s >= sc_t) if reverse else (sc_s <= sc_t)
                sc = jnp.where(keep, sc, 0.0).astype(BF16)
                return jnp.dot(sc, _stack_heads(v, lane_head).astype(BF16), preferred_element_type=F32)

            def intra_exact():
                kx_ref[d] = k
                bx_ref[d] = bc
                vx_ref[d] = v

                def sbody(s, acc):
                    keep = (rows <= s) if reverse else (rows >= s)
                    e = jnp.exp(jnp.where(keep, bc - bx_ref[d, pl.ds(s, 1), :], 0.0))
                    p = jnp.where(keep, q * e * kx_ref[d, pl.ds(s, 1), :], 0.0)
                    sc = jnp.dot(p.astype(BF16), ones_bd, preferred_element_type=F32)
                    return acc + sc * vx_ref[d, pl.ds(s, 1), :]

                return lax.fori_loop(0, nb, sbody, jnp.zeros((nb, n), F32))

            o = o + (intra_fast() if fast else intra_exact())
            dst_ref[0, pl.ds(r0, nb), :] += o

        decay = _head_sum(jnp.where(diag64, jnp.exp(b_edge), 0.0), ones_bd)
        kd = (k * jnp.exp(b_edge - bc)).astype(BF16)
        full = lax.dot_general(kd, v_bf, (((0,), (0,)), ((), ())), preferred_element_type=F32)
        upd = jnp.concatenate(
            [jnp.where(low_half,
                       full[(2 * c) * HEAD_DIM:(2 * c + 1) * HEAD_DIM, 2 * c * HEAD_DIM:(2 * c + 2) * HEAD_DIM],
                       full[(2 * c + 1) * HEAD_DIM:(2 * c + 2) * HEAD_DIM, 2 * c * HEAD_DIM:(2 * c + 2) * HEAD_DIM])
             for c in range(C_HEADS // 2)], axis=1)
        st_ref[d] = st * decay + upd

    def run(src_ref, dst_ref, nblk, base):
        grp = HGRN_GROUP if nblk % HGRN_GROUP == 0 else 1

        def body(i, carry):
            fwd = [i * grp + gg for gg in range(grp)]
            bwd = [nblk - 1 - blk for blk in fwd]
            safe = functools.reduce(jnp.minimum, [flag_ref[0, base + blk] for blk in fwd]
                                    + [flag_ref[1, base + blk] for blk in bwd])

            def group(fast):
                for bf, bb in zip(fwd, bwd):
                    step(src_ref, dst_ref, bf, 0, fast)
                    step(src_ref, dst_ref, bb, 1, fast)

            pl.when(safe > 0)(functools.partial(group, True))
            pl.when(safe <= 0)(functools.partial(group, False))
            return carry
        lax.fori_loop(0, nblk // grp, body, 0)

    run(zc_ref, oc_ref, nblk_c, 0)
    run(z_ref, o_ref, nblk_l, nblk_c)


def _hgrn(zc, zc_c, lbc, ctx_out):
    b, t, w = zc.shape
    lc = zc_c.shape[1]
    n = C_WIDTH
    row = lambda bb: (bb, 0, 0)
    out_shape = [jax.ShapeDtypeStruct((b, t, n), F32)]
    out_specs = [pl.BlockSpec((1, t, n), row)]
    if ctx_out:
        out_shape.append(jax.ShapeDtypeStruct((b, lc, n), F32))
        out_specs.append(pl.BlockSpec((1, lc, n), row))
    res = pl.pallas_call(
        functools.partial(_hgrn_kernel, ctx_out=ctx_out),
        out_shape=out_shape,
        grid=(b,),
        in_specs=[pl.BlockSpec((1, t, w), row),
                  pl.BlockSpec((1, lc, w), row),
                  pl.BlockSpec((2, 8, n), lambda bb: (0, 0, 0))],
        out_specs=out_specs,
        scratch_shapes=[pltpu.VMEM((2, HEAD_DIM, n), F32)]
                       + [pltpu.VMEM((2, HGRN_BLOCK, n), F32)] * 3
                       + [pltpu.SMEM((2, (t + lc) // HGRN_BLOCK), jnp.int32)],
        compiler_params=_cparams("arbitrary"),
        name="hgrn",
    )(zc, zc_c, lbc)
    return (res[0], res[1]) if ctx_out else (res[0], None)


def _outproj_kernel(x_ref, ya_ref, yb_ref, o_ref, g_ref, w_ref, gate_ref, mul_ref, add_ref, hw_ref,
                    wr_ref, br_ref, xo_ref, h2_ref, lg_ref):
    yc = _head_rms(o_ref[0], _head_ones(C_WIDTH, BF16)) * hw_ref[...] * jax.nn.silu(g_ref[0])
    y = jnp.dot(ya_ref[0], w_ref[0:A_WIDTH, :], preferred_element_type=F32)
    y = y + jnp.dot(yb_ref[0], w_ref[A_WIDTH:A_WIDTH + B_WIDTH, :], preferred_element_type=F32)
    y = y + jnp.dot(yc.astype(BF16), w_ref[A_WIDTH + B_WIDTH:, :], preferred_element_type=F32)
    xn = x_ref[0] + gate_ref[0] * y
    xo_ref[0] = xn
    ms = jnp.mean(xn * xn, axis=-1, keepdims=True)
    h2 = xn * lax.rsqrt(ms + EPS) * mul_ref[0] + add_ref[0]
    _pack_planes(h2, h2_ref, (0,))
    lg_ref[0] = jnp.dot(h2, wr_ref[...], preferred_element_type=F32,
                        precision=lax.Precision.HIGHEST) + br_ref[...]


def _outproj(x, ya, yb, o, g, w_bf, gate, mul, add, hw, wr, br):
    b, t, d = x.shape
    tm = min(256, t)
    row = lambda bb, i: (bb, i, 0)
    vec = lambda bb, i: (bb, 0, 0)
    const = lambda bb, i: (0, 0)
    return pl.pallas_call(
        _outproj_kernel,
        out_shape=[jax.ShapeDtypeStruct((b, t, d), F32),
                   jax.ShapeDtypeStruct((PLANES, b, t, SC_ROW), F32),
                   jax.ShapeDtypeStruct((b, t, ROUTER_LANES), F32)],
        grid=(b, t // tm),
        in_specs=[pl.BlockSpec((1, tm, d), row),
                  pl.BlockSpec((1, tm, A_WIDTH), row),
                  pl.BlockSpec((1, tm, B_WIDTH), row),
                  pl.BlockSpec((1, tm, C_WIDTH), row),
                  pl.BlockSpec((1, tm, C_WIDTH), row),
                  pl.BlockSpec((d, d), const),
                  pl.BlockSpec((1, 1, d), vec),
                  pl.BlockSpec((1, 1, d), vec),
                  pl.BlockSpec((1, 1, d), vec),
                  pl.BlockSpec((1, C_WIDTH), const),
                  pl.BlockSpec((d, ROUTER_LANES), const),
                  pl.BlockSpec((1, ROUTER_LANES), const)],
        out_specs=[pl.BlockSpec((1, tm, d), row),
                   pl.BlockSpec((PLANES, 1, tm, SC_ROW), lambda bb, i: (0, bb, i, 0)),
                   pl.BlockSpec((1, tm, ROUTER_LANES), row)],
        compiler_params=_cparams("arbitrary", "arbitrary"),
        name="outproj",
    )(x, ya, yb, o, g, w_bf, gate, mul, add, hw, wr, br)


def _route_kernel(lg_ref, info_ref, meta_ref, cnt_ref, base_ref):
    phase = pl.program_id(0)
    i = pl.program_id(1)
    tm = lg_ref.shape[0]
    lane = lax.broadcasted_iota(jnp.int32, (tm, ROUTER_LANES), 1)
    lane_f = lane.astype(F32)
    lg = lg_ref[...]
    neg = -jnp.inf
    gl = jnp.where(lane < N_GROUPS, lg, neg)
    gmax = jnp.max(gl, axis=1, keepdims=True)
    grp = jnp.min(jnp.where(gl == gmax, lane_f, float(ROUTER_LANES)), axis=1, keepdims=True).astype(jnp.int32)
    p_grp = 1.0 / jnp.sum(jnp.exp(gl - gmax), axis=1, keepdims=True)
    in_grp = (lane >= N_GROUPS) & (lane < N_GROUPS + N_EXPERTS) & (((lane - N_GROUPS) >> 3) == grp)
    el = jnp.where(in_grp, lg, neg)
    v1 = jnp.max(el, axis=1, keepdims=True)
    i1 = jnp.min(jnp.where(el == v1, lane_f, float(ROUTER_LANES)), axis=1, keepdims=True)
    el2 = jnp.where(lane_f == i1, neg, el)
    v2 = jnp.max(el2, axis=1, keepdims=True)
    i2 = jnp.min(jnp.where(el2 == v2, lane_f, float(ROUTER_LANES)), axis=1, keepdims=True)
    hit1 = lane_f == i1
    hit2 = lane_f == i2
    onehot = (hit1 | hit2).astype(F32)
    tile_counts = jnp.sum(onehot, axis=0, keepdims=True)

    @pl.when(phase == 0)
    def _():
        @pl.when(i == 0)
        def _():
            cnt_ref[...] = jnp.zeros_like(cnt_ref)
        cnt_ref[...] += tile_counts
        info_ref[...] = jnp.zeros_like(info_ref)
        meta_ref[...] = jnp.zeros_like(meta_ref)

    @pl.when(phase == 1)
    def _():
        @pl.when(i == 0)
        def _():
            counts = jnp.broadcast_to(cnt_ref[...], (8, ROUTER_LANES))
            padded = jnp.floor((counts + (MOE_ROWS - 1.0)) * (1.0 / MOE_ROWS)) * MOE_ROWS
            r = lax.broadcasted_iota(jnp.int32, (ROUTER_LANES, ROUTER_LANES), 0)
            c = lax.broadcasted_iota(jnp.int32, (ROUTER_LANES, ROUTER_LANES), 1)
            ends = jnp.dot(padded, (r <= c).astype(F32), preferred_element_type=F32,
                           precision=lax.Precision.HIGHEST)
            base_ref[...] = (ends - padded)[0:1]
            row = lax.broadcasted_iota(jnp.int32, (8, ROUTER_LANES), 0)
            meta_ref[...] = jnp.where(row == 0, counts, jnp.where(row == 1, ends - padded, ends))

        tr = lax.broadcasted_iota(jnp.int32, (tm, tm), 0)
        tc = lax.broadcasted_iota(jnp.int32, (tm, tm), 1)
        before = jnp.dot((tc < tr).astype(BF16), onehot.astype(BF16), preferred_element_type=F32)
        pos = base_ref[...] + before
        d1 = jnp.sum(jnp.where(hit1, pos, 0.0), axis=1, keepdims=True)
        d2 = jnp.sum(jnp.where(hit2, pos, 0.0), axis=1, keepdims=True)
        base_ref[...] += tile_counts
        rr = jnp.exp(v2 - v1)
        g1 = p_grp / (1.0 + rr)
        g2 = p_grp * rr / (1.0 + rr)
        info_ref[...] = jnp.where(lane == 0, d1, jnp.where(lane == 1, d2, jnp.where(
            lane == 2, g1, jnp.where(lane == 3, g2, 0.0))))


def _route(logits):
    n = logits.shape[0]
    tm = ROUTE_ROWS if n % ROUTE_ROWS == 0 else MOE_ROWS
    return pl.pallas_call(
        _route_kernel,
        out_shape=[jax.ShapeDtypeStruct((n, ROUTER_LANES), F32),
                   jax.ShapeDtypeStruct((8, ROUTER_LANES), F32)],
        grid=(2, n // tm),
        in_specs=[pl.BlockSpec((tm, ROUTER_LANES), lambda p, i: (i, 0))],
        out_specs=[pl.BlockSpec((tm, ROUTER_LANES), lambda p, i: (p * i, 0)),
                   pl.BlockSpec((8, ROUTER_LANES), lambda p, i: (0, 0))],
        scratch_shapes=[pltpu.VMEM((1, ROUTER_LANES), F32), pltpu.VMEM((1, ROUTER_LANES), F32)],
        compiler_params=_cparams("arbitrary", "arbitrary"),
        name="route",
    )(logits)


def _sc_mesh():
    return plsc.VectorSubcoreMesh(core_axis_name="c", subcore_axis_name="s")


def _sc_gather(table, idx):
    n = idx.shape[0]
    d = table.shape[1]

    @functools.partial(pl.kernel, out_type=jax.ShapeDtypeStruct((n, d), table.dtype), mesh=_sc_mesh())
    def gather(x_hbm, i_hbm, o_hbm):
        def body(i_vmem, o_vmem):
            pltpu.sync_copy(x_hbm.at[i_vmem.at[0]], o_vmem)

        pltpu.emit_pipeline(
            body,
            grid=(n // SC_WINDOW,),
            in_specs=[pl.BlockSpec((1, SC_WINDOW), lambda i: (0, i))],
            out_specs=[pl.BlockSpec((SC_WINDOW, d), lambda i: (i, 0))],
            core_axis_name=("c", "s"),
            dimension_semantics=(pltpu.PARALLEL,),
        )(i_hbm, o_hbm)

    return gather(table, idx.reshape(1, n))


def _sc_scatter2(rows, idx0, idx1, n_out):
    m, d = rows.shape

    @functools.partial(pl.kernel, out_type=jax.ShapeDtypeStruct((n_out, d), rows.dtype), mesh=_sc_mesh())
    def scatter(x_hbm, i0_hbm, i1_hbm, o_hbm):
        def body(x_vmem, i0_vmem, i1_vmem):
            pltpu.sync_copy(x_vmem, o_hbm.at[i0_vmem.at[0]])
            pltpu.sync_copy(x_vmem, o_hbm.at[i1_vmem.at[0]])

        pltpu.emit_pipeline(
            body,
            grid=(m // SC_WINDOW,),
            in_specs=[pl.BlockSpec((SC_WINDOW, d), lambda i: (i, 0)),
                      pl.BlockSpec((1, SC_WINDOW), lambda i: (0, i)),
                      pl.BlockSpec((1, SC_WINDOW), lambda i: (0, i))],
            out_specs=[],
            core_axis_name=("c", "s"),
            dimension_semantics=(pltpu.PARALLEL,),
        )(x_hbm, i0_hbm, i1_hbm)

    return scatter(rows, idx0.reshape(1, m), idx1.reshape(1, m))


def _moe_kernel(be_ref, nu_ref, x_ref, w1_ref, w3_ref, w2_ref, o_ref, w1b, w3b, w2b):
    i = pl.program_id(0)
    e = be_ref[i]
    prev = be_ref[jnp.maximum(i - 1, 0)]

    @pl.when((i == 0) | (e != prev))
    def _():
        w1b[...] = w1_ref[0, 0].astype(BF16)
        w3b[...] = w3_ref[0, 0].astype(BF16)
        w2b[...] = w2_ref[0, 0].astype(BF16)

    @pl.when(i < nu_ref[0])
    def _():
        parts = [h.astype(BF16) for p in range(PLANES) for h in _unpack_rows(x_ref[p])]
        a = sum(jnp.dot(h, w1b[q * SC_ROW:(q + 1) * SC_ROW, :], preferred_element_type=F32)
                for q, h in enumerate(parts))
        b = sum(jnp.dot(h, w3b[q * SC_ROW:(q + 1) * SC_ROW, :], preferred_element_type=F32)
                for q, h in enumerate(parts))
        hmid = (jax.nn.silu(a) * b).astype(BF16)
        _pack_planes(jnp.dot(hmid, w2b[...], preferred_element_type=F32), o_ref)

    @pl.when(i >= nu_ref[0])
    def _():
        o_ref[...] = jnp.zeros_like(o_ref)


def _moe_mlp(blk_expert, n_used, xs, w1, w3, w2, layer):
    n_rows = xs.shape[1]
    d, f = w1.shape[2:]
    nblk = n_rows // MOE_ROWS
    rows = lambda i, be, nu: (0, i, 0)
    return pl.pallas_call(
        _moe_kernel,
        out_shape=jax.ShapeDtypeStruct((PLANES, n_rows, SC_ROW), F32),
        grid_spec=pltpu.PrefetchScalarGridSpec(
            num_scalar_prefetch=2,
            grid=(nblk,),
            in_specs=[pl.BlockSpec((PLANES, MOE_ROWS, SC_ROW), rows),
                      pl.BlockSpec((1, 1, d, f), lambda i, be, nu: (layer, be[i], 0, 0)),
                      pl.BlockSpec((1, 1, d, f), lambda i, be, nu: (layer, be[i], 0, 0)),
                      pl.BlockSpec((1, 1, f, d), lambda i, be, nu: (layer, be[i], 0, 0))],
            out_specs=pl.BlockSpec((PLANES, MOE_ROWS, SC_ROW), rows),
            scratch_shapes=[pltpu.VMEM((d, f), BF16), pltpu.VMEM((d, f), BF16), pltpu.VMEM((f, d), BF16)]),
        compiler_params=_cparams("arbitrary"),
        name="moe",
    )(blk_expert, n_used, xs, w1, w3, w2)


def _combine_kernel(x_ref, pk_ref, info_ref, g_ref, o_ref):
    info = info_ref[...]
    g1 = info[:, 2:3]
    g2 = info[:, 3:4]
    parts = []
    for p in range(PLANES):
        hi1, lo1 = _unpack_rows(pk_ref[TOP_K * p])
        hi2, lo2 = _unpack_rows(pk_ref[TOP_K * p + 1])
        parts += [g1 * hi1 + g2 * hi2, g1 * lo1 + g2 * lo2]
    o_ref[0] = x_ref[0] + g_ref[0] * jnp.concatenate(parts, axis=1)


def _combine(x, picked, info, gate, row0):
    b, t, d = x.shape
    tm = min(256, t)
    off = row0 // tm
    tok = lambda bb, i: (off + bb * (t // tm) + i, 0)
    tok3 = lambda bb, i: (0, off + bb * (t // tm) + i, 0)
    return pl.pallas_call(
        _combine_kernel,
        out_shape=jax.ShapeDtypeStruct((b, t, d), F32),
        grid=(b, t // tm),
        in_specs=[pl.BlockSpec((1, tm, d), lambda bb, i: (bb, i, 0)),
                  pl.BlockSpec((PLANES * TOP_K, tm, SC_ROW), tok3),
                  pl.BlockSpec((tm, ROUTER_LANES), tok),
                  pl.BlockSpec((1, 1, d), lambda bb, i: (bb, 0, 0))],
        out_specs=pl.BlockSpec((1, tm, d), lambda bb, i: (bb, i, 0)),
        compiler_params=_cparams("arbitrary", "arbitrary"),
        name="combine",
    )(x, picked, info, gate)


def _hier_moe(h2p, logits, w1, w3, w2, layer):
    n_tok = h2p.shape[1]
    info, meta = _route(logits)
    dest = info[:, 0:TOP_K].astype(jnp.int32)
    pad_ends = meta[2, N_GROUPS:N_GROUPS + N_EXPERTS].astype(jnp.int32)
    nblk = -(-(n_tok * TOP_K) // MOE_ROWS) + N_EXPERTS
    n_rows = nblk * MOE_ROWS
    blk_start = jnp.arange(nblk, dtype=jnp.int32) * MOE_ROWS
    blk_expert = jnp.minimum(jnp.sum((pad_ends[None, :] <= blk_start[:, None]).astype(jnp.int32), axis=1),
                             N_EXPERTS - 1)
    n_used = pad_ends[-1:] // MOE_ROWS
    slot = [jnp.concatenate([p * n_rows + dest[:, s] for p in range(PLANES)]) for s in range(TOP_K)]
    xs = _sc_scatter2(h2p.reshape(PLANES * n_tok, SC_ROW), slot[0], slot[1], PLANES * n_rows)
    out = _moe_mlp(blk_expert, n_used, xs.reshape(PLANES, n_rows, SC_ROW), w1, w3, w2, layer)
    idx_all = jnp.concatenate([p * n_rows + dest[:, s] for p in range(PLANES) for s in range(TOP_K)])
    picked = _sc_gather(out.reshape(PLANES * n_rows, SC_ROW), idx_all)
    return picked.reshape(PLANES * TOP_K, n_tok, SC_ROW), info


def _layer(layer, x, xc, c, c_ctx, lb, w_mod, b_mod, norm1_w, w_in, w_s, b_s, q_norm_w, k_norm_w, hgrn_norm_w, w_out,
           norm2_w, w_grp, b_grp, w_exp, b_exp, w1, w3, w2, ctx_out):
    b, t, d = x.shape
    lc = xc.shape[1]
    cc = jnp.zeros((MOD_ROWS, d), F32).at[:b].set(c).at[b].set(c_ctx)
    mod = _mod(cc, w_mod, b_mod, layer)
    sh1, sc1, g1, sh2, sc2, g2 = [m[:, None, :] for m in jnp.split(mod[:b], 6, axis=-1)]
    mod_c = [jnp.broadcast_to(m[None, None, :], (b, 1, d)) for m in jnp.split(mod[b], 6)]

    w_in_bf = w_in.astype(BF16)
    scale = LOG2E * HEAD_DIM ** -0.5
    q_tabs = _rope_tables(t, q_norm_w, scale, B_WIDTH, True)
    k_tabs = _rope_tables(t, k_norm_w, 1.0, B_KV_WIDTH, True)
    qc_tabs = _rope_tables(lc, q_norm_w, scale, B_WIDTH, False)
    kc_tabs = _rope_tables(lc, k_norm_w, 1.0, B_KV_WIDTH, False)
    za, q, k, v, zc, g = _inproj(x, norm1_w * (1.0 + sc1), sh1, w_in_bf, q_tabs, k_tabs)
    za_c, q_c, k_c, v_c, zc_c, g_c = _inproj(xc, norm1_w * (1.0 + mod_c[1]), mod_c[0], w_in_bf,
                                             qc_tabs, kc_tabs)

    ws_bf = w_s.astype(BF16)
    bias2d = jnp.repeat(b_s.T, HEAD_DIM, axis=1)
    ya = _gmlp(za, ws_bf, bias2d)

    bound = LOG2E * HEAD_DIM ** 0.5 * jnp.max(jnp.abs(q_norm_w)) * jnp.max(jnp.abs(k_norm_w)) * 1.02
    shift = jnp.ceil(bound)
    attn_flag = jnp.stack([(shift <= SAFE_SHIFT).astype(jnp.int32), shift.astype(jnp.int32)])
    yb = _attn(attn_flag, q, [(k, v), (k_c, v_c)])

    pos = lb > 0.0
    log_lb = jnp.log(jnp.where(pos, lb, 1.0))
    lbc = jnp.stack([1.0 - lb, jnp.log1p(-lb), log_lb, pos.astype(F32), jnp.where(pos, -log_lb, 1e30)], axis=1)
    lbc = jnp.concatenate([lbc, jnp.zeros((2, 3, C_WIDTH), F32)], axis=1)
    o, o_c = _hgrn(zc, zc_c, lbc, ctx_out)

    w_out_bf = w_out.astype(BF16)
    hw = jnp.tile(hgrn_norm_w, C_HEADS)[None, :]
    wr = jnp.zeros((d, ROUTER_LANES), F32).at[:, :N_GROUPS].set(w_grp).at[
        :, N_GROUPS:N_GROUPS + N_EXPERTS].set(w_exp)
    br = jnp.zeros((1, ROUTER_LANES), F32).at[0, :N_GROUPS].set(b_grp).at[
        0, N_GROUPS:N_GROUPS + N_EXPERTS].set(b_exp)
    x, h2, lg = _outproj(x, ya, yb, o, g, w_out_bf, g1, norm2_w * (1.0 + sc2), sh2, hw, wr, br)
    if ctx_out:
        ya_c = _gmlp(za_c, ws_bf, bias2d)
        yb_c = _attn(attn_flag, q_c, [(k_c, v_c)])
        xc, h2c, lgc = _outproj(xc, ya_c, yb_c, o_c, g_c, w_out_bf, mod_c[2],
                                norm2_w * (1.0 + mod_c[4]), mod_c[3], hw, wr, br)
        tokens = jnp.concatenate([h2.reshape(PLANES, -1, SC_ROW), h2c.reshape(PLANES, -1, SC_ROW)], axis=1)
        logits = jnp.concatenate([lg.reshape(-1, ROUTER_LANES), lgc.reshape(-1, ROUTER_LANES)], axis=0)
        picked, info = _hier_moe(tokens, logits, w1, w3, w2, layer)
        x = _combine(x, picked, info, g2, 0)
        xc = _combine(xc, picked, info, mod_c[5], b * t)
    else:
        picked, info = _hier_moe(h2.reshape(PLANES, -1, SC_ROW), lg.reshape(-1, ROUTER_LANES), w1, w3, w2, layer)
        x = _combine(x, picked, info, g2, 0)
    return x, xc


def kernel(x, c, ctx, c_ctx, w_mod, b_mod, norm1_w, w_in, w_s, b_s, q_norm_w, k_norm_w, hgrn_lb_logits,
           hgrn_norm_w, w_out, norm2_w, w_grp, b_grp, w_exp, b_exp, w1, w3, w2):
    depth = w_mod.shape[0]
    lb_sm = jax.nn.softmax(hgrn_lb_logits.astype(F32), axis=0)
    lb = jnp.cumsum(lb_sm, axis=0) - lb_sm[0]
    xc = ctx
    for l in range(depth):
        x, xc = _layer(l, x, xc, c, c_ctx, lb[l], w_mod, b_mod[l], norm1_w[l], w_in[l], w_s[l], b_s[l],
                       q_norm_w[l], k_norm_w[l], hgrn_norm_w[l], w_out[l], norm2_w[l], w_grp[l], b_grp[l],
                       w_exp[l], b_exp[l], w1, w3, w2, ctx_out=(l < depth - 1))
    return x
```

```python
import functools

import jax
import jax.numpy as jnp
from jax import lax
from jax.experimental import pallas as pl
from jax.experimental.pallas import tpu as pltpu
from jax.experimental.pallas import tpu_sc as plsc

F32 = jnp.float32
BF16 = jnp.bfloat16

D_MODEL = 1024
HEAD_DIM = 64
GRID_W = 64
EPS = 1e-6
ROPE_BASE = 10000.0
A_WIDTH = D_MODEL // 4
A_HEADS = A_WIDTH // HEAD_DIM
A_CHUNK = 128
B_WIDTH = D_MODEL // 2
B_HEADS = B_WIDTH // HEAD_DIM
B_KV_HEADS = 2
B_GROUP = B_HEADS // B_KV_HEADS
B_KV_WIDTH = B_KV_HEADS * HEAD_DIM
C_WIDTH = D_MODEL // 4
C_HEADS = C_WIDTH // HEAD_DIM
OFF_B = 2 * A_WIDTH
OFF_KV = OFF_B + B_WIDTH
OFF_V = OFF_KV + B_KV_WIDTH
OFF_C = OFF_KV + 2 * B_KV_WIDTH
OFF_G = OFF_C + 4 * C_WIDTH
IN_WIDTH = OFF_G + C_WIDTH
N_GROUPS = 4
EXPERTS_PER_GROUP = 8
N_EXPERTS = N_GROUPS * EXPERTS_PER_GROUP
TOP_K = 2
D_FF_EXPERT = D_MODEL // 2

MOD_ROWS = 16
ROUTER_LANES = 128
HGRN_BLOCK = 32
HGRN_GROUP = 4
LOG2E = 1.4426950408889634
ATTN_KEYS = 512
SAFE_SHIFT = 60
SAFE_DECAY = 80.0
MOE_ROWS = 256
ROUTE_ROWS = 512
SC_WINDOW = 128
SC_ROW = 256
PLANES = D_MODEL // (2 * SC_ROW)
VMEM_LIMIT = 48 * 1024 * 1024


def _cparams(*sem):
    return pltpu.CompilerParams(dimension_semantics=sem, vmem_limit_bytes=VMEM_LIMIT)


def _head_ones(n, dtype):
    r = lax.broadcasted_iota(jnp.int32, (n, n), 0) >> 6
    c = lax.broadcasted_iota(jnp.int32, (n, n), 1) >> 6
    return (r == c).astype(dtype)


def _head_sum(x, ones_bd):
    hi = x.astype(BF16)
    lo = (x - hi.astype(F32)).astype(BF16)
    return (jnp.dot(hi, ones_bd, preferred_element_type=F32)
            + jnp.dot(lo, ones_bd, preferred_element_type=F32))


def _head_rms(x, ones_bd):
    return x * lax.rsqrt(_head_sum(x * x, ones_bd) * (1.0 / HEAD_DIM) + EPS)


def _pack_rows(y):
    bits = lax.bitcast_convert_type(y.astype(BF16).astype(F32), jnp.uint32)
    half = y.shape[1] // 2
    return lax.bitcast_convert_type(bits[:, :half] | (bits[:, half:] >> 16), F32)


def _unpack_rows(w):
    bits = lax.bitcast_convert_type(w, jnp.uint32)
    hi = lax.bitcast_convert_type(bits & jnp.uint32(0xFFFF0000), F32)
    lo = lax.bitcast_convert_type(bits << 16, F32)
    return hi, lo


def _pack_planes(y, ref, lead=()):
    for p in range(PLANES):
        ref[(p,) + lead] = _pack_rows(y[:, 2 * p * SC_ROW:(2 * p + 2) * SC_ROW])


def _mod_kernel(c_ref, w_ref, b_ref, o_ref):
    a = jax.nn.silu(c_ref[...])
    o_ref[...] = jnp.dot(a, w_ref[0], preferred_element_type=F32,
                         precision=lax.Precision.HIGHEST) + b_ref[...]


def _mod(cc, w_mod, b_mod, layer):
    n = w_mod.shape[2]
    tn = 1536
    return pl.pallas_call(
        _mod_kernel,
        out_shape=jax.ShapeDtypeStruct((MOD_ROWS, n), F32),
        grid=(n // tn,),
        in_specs=[pl.BlockSpec((MOD_ROWS, D_MODEL), lambda j: (0, 0)),
                  pl.BlockSpec((1, D_MODEL, tn), lambda j: (layer, 0, j)),
                  pl.BlockSpec((1, tn), lambda j: (0, j))],
        out_specs=pl.BlockSpec((MOD_ROWS, tn), lambda j: (0, j)),
        compiler_params=_cparams("arbitrary"),
        name="mod",
    )(cc, w_mod, b_mod.reshape(1, n))


def _rope(xn, c_ref, sp_ref, sm_ref):
    w = xn.shape[-1]
    return (xn * c_ref[...] + pltpu.roll(xn, 16, 1) * sp_ref[...]
            + pltpu.roll(xn, w - 16, 1) * sm_ref[...])


def _inproj_kernel(x_ref, mul_ref, add_ref, w_ref, qc_ref, qsp_ref, qsm_ref, kc_ref, ksp_ref, ksm_ref,
                   za_ref, q_ref, k_ref, v_ref, zc_ref, g_ref):
    x = x_ref[0]
    ms = jnp.mean(x * x, axis=-1, keepdims=True)
    h = x * lax.rsqrt(ms + EPS) * mul_ref[0] + add_ref[0]
    y = jnp.dot(h.astype(BF16), w_ref[...], preferred_element_type=F32)
    za_ref[0] = y[:, :OFF_B].astype(BF16)
    qn = _head_rms(y[:, OFF_B:OFF_KV], _head_ones(B_WIDTH, BF16))
    q_ref[0] = _rope(qn, qc_ref, qsp_ref, qsm_ref).astype(BF16)
    kn = _head_rms(y[:, OFF_KV:OFF_V], _head_ones(B_KV_WIDTH, BF16))
    k_ref[0] = _rope(kn, kc_ref, ksp_ref, ksm_ref).astype(BF16)
    v_ref[0] = y[:, OFF_V:OFF_C].astype(BF16)
    zc_ref[0] = y[:, OFF_C:OFF_G]
    g_ref[0] = y[:, OFF_G:]


def _inproj(x, mul, add, w_bf, q_tabs, k_tabs):
    b, t, d = x.shape
    tm = min(256, t)
    row = lambda i, bb: (bb, i, 0)
    vec = lambda i, bb: (bb, 0, 0)
    tab = lambda i, bb: (i, 0)
    widths = (OFF_B, B_WIDTH, B_KV_WIDTH, B_KV_WIDTH, 4 * C_WIDTH, C_WIDTH)
    dtypes = (BF16, BF16, BF16, BF16, F32, F32)
    return pl.pallas_call(
        _inproj_kernel,
        out_shape=[jax.ShapeDtypeStruct((b, t, w), dt) for w, dt in zip(widths, dtypes)],
        grid=(t // tm, b),
        in_specs=[pl.BlockSpec((1, tm, d), row),
                  pl.BlockSpec((1, 1, d), vec),
                  pl.BlockSpec((1, 1, d), vec),
                  pl.BlockSpec((d, IN_WIDTH), lambda i, bb: (0, 0))]
                 + [pl.BlockSpec((tm, B_WIDTH), tab)] * 3
                 + [pl.BlockSpec((tm, B_KV_WIDTH), tab)] * 3,
        out_specs=[pl.BlockSpec((1, tm, w), row) for w in widths],
        compiler_params=_cparams("arbitrary", "arbitrary"),
        name="inproj",
    )(x, mul, add, w_bf, *q_tabs, *k_tabs)


def _rope_tables(t, w, scale, width, rotate):
    ws = w.astype(F32) * scale
    if not rotate:
        c = jnp.broadcast_to(jnp.tile(ws, width // HEAD_DIM)[None, :], (t, width))
        z = jnp.zeros((t, width), F32)
        return c, z, z
    pos = jnp.arange(t)
    row = (pos // GRID_W).astype(F32)
    col = (pos % GRID_W).astype(F32)
    inv_freq = 1.0 / (ROPE_BASE ** (jnp.arange(0, HEAD_DIM // 2, 2, dtype=F32) / (HEAD_DIM // 2)))
    dd = jnp.arange(HEAD_DIM)
    axis = dd // 32
    half = (dd % 32) // 16
    ang = jnp.where(axis[None, :] == 0, row[:, None], col[:, None]) * inv_freq[dd % 16][None, :]
    cos, sin = jnp.cos(ang), jnp.sin(ang)
    c = cos * ws[None, :]
    sm = jnp.where(half[None, :] == 0, -sin * jnp.roll(ws, -16)[None, :], 0.0)
    sp = jnp.where(half[None, :] == 1, sin * jnp.roll(ws, 16)[None, :], 0.0)
    rep = width // HEAD_DIM
    return jnp.tile(c, (1, rep)), jnp.tile(sp, (1, rep)), jnp.tile(sm, (1, rep))


def _gmlp_kernel(z_ref, ws_ref, bias_ref, o_ref):
    gz = jax.nn.gelu(z_ref[0].astype(F32))
    u = gz[:, :A_WIDTH]
    vn = _head_rms(gz[:, A_WIDTH:], _head_ones(A_WIDTH, BF16))
    lane_head = lax.broadcasted_iota(jnp.int32, vn.shape, 1) >> 6
    acc = bias_ref[...]
    for hh in range(A_HEADS):
        vh = jnp.where(lane_head == hh, vn, 0.0).astype(BF16)
        acc = acc + jnp.dot(ws_ref[hh], vh, preferred_element_type=F32)
    o_ref[0] = (u * acc).astype(BF16)


def _gmlp(za, ws_bf, bias2d):
    b, t, _ = za.shape
    return pl.pallas_call(
        _gmlp_kernel,
        out_shape=jax.ShapeDtypeStruct((b, t, A_WIDTH), BF16),
        grid=(b, t // A_CHUNK),
        in_specs=[pl.BlockSpec((1, A_CHUNK, OFF_B), lambda bb, i: (bb, i, 0)),
                  pl.BlockSpec((A_HEADS, A_CHUNK, A_CHUNK), lambda bb, i: (0, 0, 0)),
                  pl.BlockSpec((A_CHUNK, A_WIDTH), lambda bb, i: (0, 0))],
        out_specs=pl.BlockSpec((1, A_CHUNK, A_WIDTH), lambda bb, i: (bb, i, 0)),
        compiler_params=_cparams("arbitrary", "arbitrary"),
        name="gmlp",
    )(za, ws_bf, bias2d)


def _attn_kernel(flag_ref, q_ref, *refs, n_seg):
    kv_refs, o_ref = refs[:2 * n_seg], refs[2 * n_seg]
    tq = q_ref.shape[1]
    dh = HEAD_DIM

    def heads(j):
        q4 = jnp.concatenate([q_ref[0, :, (B_GROUP * j + gg) * dh:(B_GROUP * j + gg + 1) * dh]
                              for gg in range(B_GROUP)], axis=0)
        ks, vs = [], []
        for sg in range(n_seg):
            s_len = kv_refs[2 * sg].shape[1]
            for c0 in range(0, s_len, ATTN_KEYS):
                c1 = min(c0 + ATTN_KEYS, s_len)
                ks.append(kv_refs[2 * sg][0, c0:c1, j * dh:(j + 1) * dh])
                vs.append(kv_refs[2 * sg + 1][0, c0:c1, j * dh:(j + 1) * dh])
        return q4, ks, vs

    def scores(q4, ks):
        return [lax.dot_general(q4, kk, (((1,), (1,)), ((), ())), preferred_element_type=F32) for kk in ks]

    def finish(j, ps, vs):
        cols = [p[:, c:c + 128] for p in ps for c in range(0, p.shape[1], 128)]
        l = jnp.sum(functools.reduce(jnp.add, cols), axis=-1, keepdims=True)
        acc = sum(jnp.dot(p.astype(BF16), vv, preferred_element_type=F32) for p, vv in zip(ps, vs))
        o = acc / l
        for gg in range(B_GROUP):
            hh = B_GROUP * j + gg
            o_ref[0, :, hh * dh:(hh + 1) * dh] = o[gg * tq:(gg + 1) * tq].astype(BF16)

    @pl.when(flag_ref[0] > 0)
    def _():
        shift = flag_ref[1].astype(F32)
        for j in range(B_KV_HEADS):
            q4, ks, vs = heads(j)
            finish(j, [jnp.exp2(s - shift) for s in scores(q4, ks)], vs)

    @pl.when(flag_ref[0] <= 0)
    def _():
        for j in range(B_KV_HEADS):
            q4, ks, vs = heads(j)
            ss = scores(q4, ks)
            m = functools.reduce(jnp.maximum, [jnp.max(s, axis=-1, keepdims=True) for s in ss])
            finish(j, [jnp.exp2(s - m) for s in ss], vs)


def _attn(flag, q, kv_segs):
    b, t, w = q.shape
    tq = min(128, t)
    n_seg = len(kv_segs)
    kv_flat, kv_specs = [], []
    for kk, vv in kv_segs:
        s_len, kw = kk.shape[1:]
        kv_flat += [kk, vv]
        kv_specs += [pl.BlockSpec((1, s_len, kw), lambda bb, i, fl: (bb, 0, 0))] * 2
    return pl.pallas_call(
        functools.partial(_attn_kernel, n_seg=n_seg),
        out_shape=jax.ShapeDtypeStruct((b, t, w), BF16),
        grid_spec=pltpu.PrefetchScalarGridSpec(
            num_scalar_prefetch=1,
            grid=(b, t // tq),
            in_specs=[pl.BlockSpec((1, tq, w), lambda bb, i, fl: (bb, i, 0))] + kv_specs,
            out_specs=pl.BlockSpec((1, tq, w), lambda bb, i, fl: (bb, i, 0))),
        compiler_params=_cparams("arbitrary", "arbitrary"),
        name="attn",
    )(flag, q, *kv_flat)


def _scan_rows(x, reverse):
    n = x.shape[0]
    rows = lax.broadcasted_iota(jnp.int32, x.shape, 0)
    sh = 1
    while sh < n:
        if reverse:
            x = x + jnp.where(rows < n - sh, pltpu.roll(x, n - sh, 0), 0.0)
        else:
            x = x + jnp.where(rows >= sh, pltpu.roll(x, sh, 0), 0.0)
        sh *= 2
    return x


def _stack_heads(x, lane_head):
    return jnp.concatenate([jnp.where(lane_head == hh, x, 0.0) for hh in range(C_HEADS)], axis=0)


def _hgrn_kernel(z_ref, zc_ref, lbc_ref, *refs, ctx_out):
    if ctx_out:
        o_ref, oc_ref, st_ref, kx_ref, bx_ref, vx_ref, flag_ref = refs
    else:
        o_ref, st_ref, kx_ref, bx_ref, vx_ref, flag_ref = refs
        oc_ref = None
    n = C_WIDTH
    nb = HGRN_BLOCK
    nblk_c = zc_ref.shape[1] // nb
    nblk_l = z_ref.shape[1] // nb
    ones_bd = _head_ones(n, BF16)
    rows = lax.broadcasted_iota(jnp.int32, (nb, n), 0)
    lane_head = lax.broadcasted_iota(jnp.int32, (nb, n), 1) >> 6
    lane_head64 = lax.broadcasted_iota(jnp.int32, (HEAD_DIM, n), 1) >> 6
    low_half = (lax.broadcasted_iota(jnp.int32, (HEAD_DIM, 2 * HEAD_DIM), 1) < HEAD_DIM)
    sc_t = lax.broadcasted_iota(jnp.int32, (nb, C_HEADS * nb), 0)
    sc_s = lax.broadcasted_iota(jnp.int32, (nb, C_HEADS * nb), 1) & (nb - 1)

    def flag_blocks(src_ref, nblk, base):
        def body(i, carry):
            r0 = pl.multiple_of(i * nb, nb)
            for d in range(2):
                zz = src_ref[0, pl.ds(r0, nb), (1 + d) * n:(2 + d) * n]
                step_bound = jnp.minimum(lbc_ref[d, 4:5, :],
                                         jnp.maximum(-zz, 0.0) + (jnp.log(2.0) - lbc_ref[d, 1:2, :]))
                bound = jnp.sum(step_bound, axis=0, keepdims=True)
                flag_ref[d, base + i] = (jnp.max(bound) <= SAFE_DECAY).astype(jnp.int32)
            return carry
        lax.fori_loop(0, nblk, body, 0)

    flag_blocks(zc_ref, nblk_c, 0)
    flag_blocks(z_ref, nblk_l, nblk_c)
    st_ref[...] = jnp.zeros_like(st_ref)
    o_ref[...] = jnp.zeros_like(o_ref)
    if ctx_out:
        oc_ref[...] = jnp.zeros_like(oc_ref)

    def step(src_ref, dst_ref, blk, d, fast):
        reverse = d == 1
        r0 = pl.multiple_of(blk * nb, nb)
        z = src_ref[0, pl.ds(r0, nb), (1 + d) * n:(2 + d) * n]
        v = src_ref[0, pl.ds(r0, nb), 3 * n:4 * n]
        one_m_lb = lbc_ref[d, 0:1, :]
        log1m_lb = lbc_ref[d, 1:2, :]
        log_lb = lbc_ref[d, 2:3, :]
        lb_pos = lbc_ref[d, 3:4, :] > 0.5
        soft = jnp.log1p(jnp.exp(-jnp.abs(z)))
        log_rest = log1m_lb + (jnp.minimum(z, 0.0) - soft)
        lse = jnp.maximum(log_lb, log_rest) + jnp.log1p(jnp.exp(-jnp.abs(log_lb - log_rest)))
        log_f = jnp.where(lb_pos, lse, log_rest)
        k = one_m_lb * jnp.exp(jnp.minimum(-z, 0.0) - soft)
        bc = _scan_rows(log_f, reverse)
        edge = 0 if reverse else nb - 1
        b_edge = bc[edge:edge + 1, :]
        st = st_ref[d]
        v_bf = v.astype(BF16)

        if dst_ref is not None:
            q = jax.nn.silu(src_ref[0, pl.ds(r0, nb), 0:n])
            qt = (q * jnp.exp(bc)).astype(BF16)
            o = lax.dot_general(qt, _stack_heads(st, lane_head64).astype(BF16), (((1,), (1,)), ((), ())),
                                preferred_element_type=F32)

            def intra_fast():
                kt = _stack_heads(k * jnp.exp(-bc), lane_head).astype(BF16)
                sc = lax.dot_general(qt, kt, (((1,), (1,)), ((), ())), preferred_element_type=F32)
                keep = (sc_s >= sc_t) if reverse else (sc_s <= sc_t)
                sc = jnp.where(keep, sc, 0.0).astype(BF16)
                return jnp.dot(sc, _stack_heads(v, lane_head).astype(BF16), preferred_element_type=F32)

            def intra_exact():
                kx_ref[d] = k
                bx_ref[d] = bc
                vx_ref[d] = v

                def sbody(s, acc):
                    keep = (rows <= s) if reverse else (rows >= s)
                    e = jnp.exp(jnp.where(keep, bc - bx_ref[d, pl.ds(s, 1), :], 0.0))
                    p = jnp.where(keep, q * e * kx_ref[d, pl.ds(s, 1), :], 0.0)
                    sc = jnp.dot(p.astype(BF16), ones_bd, preferred_element_type=F32)
                    return acc + sc * vx_ref[d, pl.ds(s, 1), :]

                return lax.fori_loop(0, nb, sbody, jnp.zeros((nb, n), F32))

            o = o + (intra_fast() if fast else intra_exact())
            dst_ref[0, pl.ds(r0, nb), :] += o

        kd = (k * jnp.exp(b_edge - bc)).astype(BF16)
        full = lax.dot_general(v_bf, kd, (((0,), (0,)), ((), ())), preferred_element_type=F32)
        upd = jnp.concatenate(
            [jnp.where(low_half,
                       full[(2 * c) * HEAD_DIM:(2 * c + 1) * HEAD_DIM, 2 * c * HEAD_DIM:(2 * c + 2) * HEAD_DIM],
                       full[(2 * c + 1) * HEAD_DIM:(2 * c + 2) * HEAD_DIM, 2 * c * HEAD_DIM:(2 * c + 2) * HEAD_DIM])
             for c in range(C_HEADS // 2)], axis=1)
        st_ref[d] = st * jnp.exp(b_edge) + upd

    def run(src_ref, dst_ref, nblk, base):
        grp = HGRN_GROUP if nblk % HGRN_GROUP == 0 else 1

        def body(i, carry):
            fwd = [i * grp + gg for gg in range(grp)]
            bwd = [nblk - 1 - blk for blk in fwd]
            safe = functools.reduce(jnp.minimum, [flag_ref[0, base + blk] for blk in fwd]
                                    + [flag_ref[1, base + blk] for blk in bwd])

            def group(fast):
                for bf, bb in zip(fwd, bwd):
                    step(src_ref, dst_ref, bf, 0, fast)
                    step(src_ref, dst_ref, bb, 1, fast)

            pl.when(safe > 0)(functools.partial(group, True))
            pl.when(safe <= 0)(functools.partial(group, False))
            return carry
        lax.fori_loop(0, nblk // grp, body, 0)

    run(zc_ref, oc_ref, nblk_c, 0)
    run(z_ref, o_ref, nblk_l, nblk_c)


def _hgrn(zc, zc_c, lbc, ctx_out):
    b, t, w = zc.shape
    lc = zc_c.shape[1]
    n = C_WIDTH
    row = lambda bb: (bb, 0, 0)
    out_shape = [jax.ShapeDtypeStruct((b, t, n), F32)]
    out_specs = [pl.BlockSpec((1, t, n), row)]
    if ctx_out:
        out_shape.append(jax.ShapeDtypeStruct((b, lc, n), F32))
        out_specs.append(pl.BlockSpec((1, lc, n), row))
    res = pl.pallas_call(
        functools.partial(_hgrn_kernel, ctx_out=ctx_out),
        out_shape=out_shape,
        grid=(b,),
        in_specs=[pl.BlockSpec((1, t, w), row),
                  pl.BlockSpec((1, lc, w), row),
                  pl.BlockSpec((2, 8, n), lambda bb: (0, 0, 0))],
        out_specs=out_specs,
        scratch_shapes=[pltpu.VMEM((2, HEAD_DIM, n), F32)]
                       + [pltpu.VMEM((2, HGRN_BLOCK, n), F32)] * 3
                       + [pltpu.SMEM((2, (t + lc) // HGRN_BLOCK), jnp.int32)],
        compiler_params=_cparams("arbitrary"),
        name="hgrn",
    )(zc, zc_c, lbc)
    return (res[0], res[1]) if ctx_out else (res[0], None)


def _outproj_kernel(x_ref, ya_ref, yb_ref, o_ref, g_ref, w_ref, gate_ref, mul_ref, add_ref, hw_ref,
                    wr_ref, br_ref, xo_ref, h2_ref, lg_ref):
    yc = _head_rms(o_ref[0], _head_ones(C_WIDTH, BF16)) * hw_ref[...] * jax.nn.silu(g_ref[0])
    y = jnp.dot(ya_ref[0], w_ref[0:A_WIDTH, :], preferred_element_type=F32)
    y = y + jnp.dot(yb_ref[0], w_ref[A_WIDTH:A_WIDTH + B_WIDTH, :], preferred_element_type=F32)
    y = y + jnp.dot(yc.astype(BF16), w_ref[A_WIDTH + B_WIDTH:, :], preferred_element_type=F32)
    xn = x_ref[0] + gate_ref[0] * y
    xo_ref[0] = xn
    ms = jnp.mean(xn * xn, axis=-1, keepdims=True)
    h2 = xn * lax.rsqrt(ms + EPS) * mul_ref[0] + add_ref[0]
    _pack_planes(h2, h2_ref, (0,))
    h_hi = h2.astype(BF16)
    h_lo = (h2 - h_hi.astype(F32)).astype(BF16)
    both = jnp.dot(h_hi, wr_ref[...], preferred_element_type=F32)
    lg_ref[0] = (both[:, :ROUTER_LANES] + both[:, ROUTER_LANES:] + br_ref[...]
                 + jnp.dot(h_lo, wr_ref[:, 0:ROUTER_LANES], preferred_element_type=F32))


def _outproj(x, ya, yb, o, g, w_bf, gate, mul, add, hw, wr, br):
    b, t, d = x.shape
    tm = min(256, t)
    row = lambda bb, i: (bb, i, 0)
    vec = lambda bb, i: (bb, 0, 0)
    const = lambda bb, i: (0, 0)
    return pl.pallas_call(
        _outproj_kernel,
        out_shape=[jax.ShapeDtypeStruct((b, t, d), F32),
                   jax.ShapeDtypeStruct((PLANES, b, t, SC_ROW), F32),
                   jax.ShapeDtypeStruct((b, t, ROUTER_LANES), F32)],
        grid=(b, t // tm),
        in_specs=[pl.BlockSpec((1, tm, d), row),
                  pl.BlockSpec((1, tm, A_WIDTH), row),
                  pl.BlockSpec((1, tm, B_WIDTH), row),
                  pl.BlockSpec((1, tm, C_WIDTH), row),
                  pl.BlockSpec((1, tm, C_WIDTH), row),
                  pl.BlockSpec((d, d), const),
                  pl.BlockSpec((1, 1, d), vec),
                  pl.BlockSpec((1, 1, d), vec),
                  pl.BlockSpec((1, 1, d), vec),
                  pl.BlockSpec((1, C_WIDTH), const),
                  pl.BlockSpec((d, 2 * ROUTER_LANES), const),
                  pl.BlockSpec((1, ROUTER_LANES), const)],
        out_specs=[pl.BlockSpec((1, tm, d), row),
                   pl.BlockSpec((PLANES, 1, tm, SC_ROW), lambda bb, i: (0, bb, i, 0)),
                   pl.BlockSpec((1, tm, ROUTER_LANES), row)],
        compiler_params=_cparams("arbitrary", "arbitrary"),
        name="outproj",
    )(x, ya, yb, o, g, w_bf, gate, mul, add, hw, wr, br)


def _route_kernel(lg_ref, info_ref, meta_ref, cnt_ref, base_ref):
    phase = pl.program_id(0)
    i = pl.program_id(1)
    tm = lg_ref.shape[0]
    lane = lax.broadcasted_iota(jnp.int32, (tm, ROUTER_LANES), 1)
    lane_f = lane.astype(F32)
    lg = lg_ref[...]
    neg = -jnp.inf
    gl = jnp.where(lane < N_GROUPS, lg, neg)
    gmax = jnp.max(gl, axis=1, keepdims=True)
    grp = jnp.min(jnp.where(gl == gmax, lane_f, float(ROUTER_LANES)), axis=1, keepdims=True).astype(jnp.int32)
    p_grp = 1.0 / jnp.sum(jnp.exp(gl - gmax), axis=1, keepdims=True)
    in_grp = (lane >= N_GROUPS) & (lane < N_GROUPS + N_EXPERTS) & (((lane - N_GROUPS) >> 3) == grp)
    el = jnp.where(in_grp, lg, neg)
    v1 = jnp.max(el, axis=1, keepdims=True)
    i1 = jnp.min(jnp.where(el == v1, lane_f, float(ROUTER_LANES)), axis=1, keepdims=True)
    el2 = jnp.where(lane_f == i1, neg, el)
    v2 = jnp.max(el2, axis=1, keepdims=True)
    i2 = jnp.min(jnp.where(el2 == v2, lane_f, float(ROUTER_LANES)), axis=1, keepdims=True)
    hit1 = lane_f == i1
    hit2 = lane_f == i2
    onehot = (hit1 | hit2).astype(F32)
    tile_counts = jnp.sum(onehot, axis=0, keepdims=True)

    @pl.when(phase == 0)
    def _():
        @pl.when(i == 0)
        def _():
            cnt_ref[...] = jnp.zeros_like(cnt_ref)
        cnt_ref[...] += tile_counts
        info_ref[...] = jnp.zeros_like(info_ref)
        meta_ref[...] = jnp.zeros_like(meta_ref)

    @pl.when(phase == 1)
    def _():
        @pl.when(i == 0)
        def _():
            counts = jnp.broadcast_to(cnt_ref[...], (8, ROUTER_LANES))
            padded = jnp.floor((counts + (MOE_ROWS - 1.0)) * (1.0 / MOE_ROWS)) * MOE_ROWS
            r = lax.broadcasted_iota(jnp.int32, (ROUTER_LANES, ROUTER_LANES), 0)
            c = lax.broadcasted_iota(jnp.int32, (ROUTER_LANES, ROUTER_LANES), 1)
            ends = jnp.dot(padded, (r <= c).astype(F32), preferred_element_type=F32,
                           precision=lax.Precision.HIGHEST)
            base_ref[...] = (ends - padded)[0:1]
            row = lax.broadcasted_iota(jnp.int32, (8, ROUTER_LANES), 0)
            meta_ref[...] = jnp.where(row == 0, counts, jnp.where(row == 1, ends - padded, ends))

        tr = lax.broadcasted_iota(jnp.int32, (tm, tm), 0)
        tc = lax.broadcasted_iota(jnp.int32, (tm, tm), 1)
        before = jnp.dot((tc < tr).astype(BF16), onehot.astype(BF16), preferred_element_type=F32)
        pos = base_ref[...] + before
        d1 = jnp.sum(jnp.where(hit1, pos, 0.0), axis=1, keepdims=True)
        d2 = jnp.sum(jnp.where(hit2, pos, 0.0), axis=1, keepdims=True)
        base_ref[...] += tile_counts
        rr = jnp.exp(v2 - v1)
        g1 = p_grp / (1.0 + rr)
        g2 = p_grp * rr / (1.0 + rr)
        info_ref[...] = jnp.where(lane == 0, d1, jnp.where(lane == 1, d2, jnp.where(
            lane == 2, g1, jnp.where(lane == 3, g2, 0.0))))


def _route(logits):
    n = logits.shape[0]
    tm = ROUTE_ROWS if n % ROUTE_ROWS == 0 else MOE_ROWS
    return pl.pallas_call(
        _route_kernel,
        out_shape=[jax.ShapeDtypeStruct((n, ROUTER_LANES), F32),
                   jax.ShapeDtypeStruct((8, ROUTER_LANES), F32)],
        grid=(2, n // tm),
        in_specs=[pl.BlockSpec((tm, ROUTER_LANES), lambda p, i: (i, 0))],
        out_specs=[pl.BlockSpec((tm, ROUTER_LANES), lambda p, i: (p * i, 0)),
                   pl.BlockSpec((8, ROUTER_LANES), lambda p, i: (0, 0))],
        scratch_shapes=[pltpu.VMEM((1, ROUTER_LANES), F32), pltpu.VMEM((1, ROUTER_LANES), F32)],
        compiler_params=_cparams("arbitrary", "arbitrary"),
        name="route",
    )(logits)


def _sc_mesh():
    return plsc.VectorSubcoreMesh(core_axis_name="c", subcore_axis_name="s")


def _sc_gather(table, idx):
    n = idx.shape[0]
    d = table.shape[1]

    @functools.partial(pl.kernel, out_type=jax.ShapeDtypeStruct((n, d), table.dtype), mesh=_sc_mesh())
    def gather(x_hbm, i_hbm, o_hbm):
        def body(i_vmem, o_vmem):
            pltpu.sync_copy(x_hbm.at[i_vmem.at[0]], o_vmem)

        pltpu.emit_pipeline(
            body,
            grid=(n // SC_WINDOW,),
            in_specs=[pl.BlockSpec((1, SC_WINDOW), lambda i: (0, i))],
            out_specs=[pl.BlockSpec((SC_WINDOW, d), lambda i: (i, 0))],
            core_axis_name=("c", "s"),
            dimension_semantics=(pltpu.PARALLEL,),
        )(i_hbm, o_hbm)

    return gather(table, idx.reshape(1, n))


def _sc_scatter2(rows, idx0, idx1, n_out):
    m, d = rows.shape

    @functools.partial(pl.kernel, out_type=jax.ShapeDtypeStruct((n_out, d), rows.dtype), mesh=_sc_mesh())
    def scatter(x_hbm, i0_hbm, i1_hbm, o_hbm):
        def body(x_vmem, i0_vmem, i1_vmem):
            pltpu.sync_copy(x_vmem, o_hbm.at[i0_vmem.at[0]])
            pltpu.sync_copy(x_vmem, o_hbm.at[i1_vmem.at[0]])

        pltpu.emit_pipeline(
            body,
            grid=(m // SC_WINDOW,),
            in_specs=[pl.BlockSpec((SC_WINDOW, d), lambda i: (i, 0)),
                      pl.BlockSpec((1, SC_WINDOW), lambda i: (0, i)),
                      pl.BlockSpec((1, SC_WINDOW), lambda i: (0, i))],
            out_specs=[],
            core_axis_name=("c", "s"),
            dimension_semantics=(pltpu.PARALLEL,),
        )(x_hbm, i0_hbm, i1_hbm)

    return scatter(rows, idx0.reshape(1, m), idx1.reshape(1, m))


def _moe_kernel(be_ref, nu_ref, x_ref, w1_ref, w3_ref, w2_ref, o_ref, w1b, w3b, w2b):
    i = pl.program_id(0)
    e = be_ref[i]
    prev = be_ref[jnp.maximum(i - 1, 0)]

    @pl.when((i == 0) | (e != prev))
    def _():
        w1b[...] = w1_ref[0, 0].astype(BF16)
        w3b[...] = w3_ref[0, 0].astype(BF16)
        w2b[...] = w2_ref[0, 0].astype(BF16)

    @pl.when(i < nu_ref[0])
    def _():
        parts = [h.astype(BF16) for p in range(PLANES) for h in _unpack_rows(x_ref[p])]
        a = sum(jnp.dot(h, w1b[q * SC_ROW:(q + 1) * SC_ROW, :], preferred_element_type=F32)
                for q, h in enumerate(parts))
        b = sum(jnp.dot(h, w3b[q * SC_ROW:(q + 1) * SC_ROW, :], preferred_element_type=F32)
                for q, h in enumerate(parts))
        hmid = (jax.nn.silu(a) * b).astype(BF16)
        _pack_planes(jnp.dot(hmid, w2b[...], preferred_element_type=F32), o_ref)

    @pl.when(i >= nu_ref[0])
    def _():
        o_ref[...] = jnp.zeros_like(o_ref)


def _moe_mlp(blk_expert, n_used, xs, w1, w3, w2, layer):
    n_rows = xs.shape[1]
    d, f = w1.shape[2:]
    nblk = n_rows // MOE_ROWS
    rows = lambda i, be, nu: (0, i, 0)
    return pl.pallas_call(
        _moe_kernel,
        out_shape=jax.ShapeDtypeStruct((PLANES, n_rows, SC_ROW), F32),
        grid_spec=pltpu.PrefetchScalarGridSpec(
            num_scalar_prefetch=2,
            grid=(nblk,),
            in_specs=[pl.BlockSpec((PLANES, MOE_ROWS, SC_ROW), rows),
                      pl.BlockSpec((1, 1, d, f), lambda i, be, nu: (layer, be[i], 0, 0)),
                      pl.BlockSpec((1, 1, d, f), lambda i, be, nu: (layer, be[i], 0, 0)),
                      pl.BlockSpec((1, 1, f, d), lambda i, be, nu: (layer, be[i], 0, 0))],
            out_specs=pl.BlockSpec((PLANES, MOE_ROWS, SC_ROW), rows),
            scratch_shapes=[pltpu.VMEM((d, f), BF16), pltpu.VMEM((d, f), BF16), pltpu.VMEM((f, d), BF16)]),
        compiler_params=_cparams("arbitrary"),
        name="moe",
    )(blk_expert, n_used, xs, w1, w3, w2)


def _combine_kernel(x_ref, pk_ref, info_ref, g_ref, o_ref):
    info = info_ref[...]
    g1 = info[:, 2:3]
    g2 = info[:, 3:4]
    parts = []
    for p in range(PLANES):
        hi1, lo1 = _unpack_rows(pk_ref[TOP_K * p])
        hi2, lo2 = _unpack_rows(pk_ref[TOP_K * p + 1])
        parts += [g1 * hi1 + g2 * hi2, g1 * lo1 + g2 * lo2]
    o_ref[0] = x_ref[0] + g_ref[0] * jnp.concatenate(parts, axis=1)


def _combine(x, picked, info, gate, row0):
    b, t, d = x.shape
    tm = min(256, t)
    off = row0 // tm
    tok = lambda bb, i: (off + bb * (t // tm) + i, 0)
    tok3 = lambda bb, i: (0, off + bb * (t // tm) + i, 0)
    return pl.pallas_call(
        _combine_kernel,
        out_shape=jax.ShapeDtypeStruct((b, t, d), F32),
        grid=(b, t // tm),
        in_specs=[pl.BlockSpec((1, tm, d), lambda bb, i: (bb, i, 0)),
                  pl.BlockSpec((PLANES * TOP_K, tm, SC_ROW), tok3),
                  pl.BlockSpec((tm, ROUTER_LANES), tok),
                  pl.BlockSpec((1, 1, d), lambda bb, i: (bb, 0, 0))],
        out_specs=pl.BlockSpec((1, tm, d), lambda bb, i: (bb, i, 0)),
        compiler_params=_cparams("arbitrary", "arbitrary"),
        name="combine",
    )(x, picked, info, gate)


def _hier_moe(h2p, logits, w1, w3, w2, layer):
    n_tok = h2p.shape[1]
    info, meta = _route(logits)
    dest = info[:, 0:TOP_K].astype(jnp.int32)
    pad_ends = meta[2, N_GROUPS:N_GROUPS + N_EXPERTS].astype(jnp.int32)
    nblk = -(-(n_tok * TOP_K) // MOE_ROWS) + N_EXPERTS
    n_rows = nblk * MOE_ROWS
    blk_start = jnp.arange(nblk, dtype=jnp.int32) * MOE_ROWS
    blk_expert = jnp.minimum(jnp.sum((pad_ends[None, :] <= blk_start[:, None]).astype(jnp.int32), axis=1),
                             N_EXPERTS - 1)
    n_used = pad_ends[-1:] // MOE_ROWS
    slot = [jnp.concatenate([p * n_rows + dest[:, s] for p in range(PLANES)]) for s in range(TOP_K)]
    xs = _sc_scatter2(h2p.reshape(PLANES * n_tok, SC_ROW), slot[0], slot[1], PLANES * n_rows)
    out = _moe_mlp(blk_expert, n_used, xs.reshape(PLANES, n_rows, SC_ROW), w1, w3, w2, layer)
    idx_all = jnp.concatenate([p * n_rows + dest[:, s] for p in range(PLANES) for s in range(TOP_K)])
    picked = _sc_gather(out.reshape(PLANES * n_rows, SC_ROW), idx_all)
    return picked.reshape(PLANES * TOP_K, n_tok, SC_ROW), info


def _layer(layer, x, xc, c, c_ctx, lb, w_mod, b_mod, norm1_w, w_in, w_s, b_s, q_norm_w, k_norm_w, hgrn_norm_w, w_out,
           norm2_w, w_grp, b_grp, w_exp, b_exp, w1, w3, w2, ctx_out):
    b, t, d = x.shape
    lc = xc.shape[1]
    cc = jnp.zeros((MOD_ROWS, d), F32).at[:b].set(c).at[b].set(c_ctx)
    mod = _mod(cc, w_mod, b_mod, layer)
    sh1, sc1, g1, sh2, sc2, g2 = [m[:, None, :] for m in jnp.split(mod[:b], 6, axis=-1)]
    mod_c = [jnp.broadcast_to(m[None, None, :], (b, 1, d)) for m in jnp.split(mod[b], 6)]

    w_in_bf = w_in.astype(BF16)
    scale = LOG2E * HEAD_DIM ** -0.5
    q_tabs = _rope_tables(t, q_norm_w, scale, B_WIDTH, True)
    k_tabs = _rope_tables(t, k_norm_w, 1.0, B_KV_WIDTH, True)
    qc_tabs = _rope_tables(lc, q_norm_w, scale, B_WIDTH, False)
    kc_tabs = _rope_tables(lc, k_norm_w, 1.0, B_KV_WIDTH, False)
    za, q, k, v, zc, g = _inproj(x, norm1_w * (1.0 + sc1), sh1, w_in_bf, q_tabs, k_tabs)
    za_c, q_c, k_c, v_c, zc_c, g_c = _inproj(xc, norm1_w * (1.0 + mod_c[1]), mod_c[0], w_in_bf,
                                             qc_tabs, kc_tabs)

    ws_bf = w_s.astype(BF16)
    bias2d = jnp.repeat(b_s.T, HEAD_DIM, axis=1)
    ya = _gmlp(za, ws_bf, bias2d)

    bound = LOG2E * HEAD_DIM ** 0.5 * jnp.max(jnp.abs(q_norm_w)) * jnp.max(jnp.abs(k_norm_w)) * 1.02
    shift = jnp.ceil(bound)
    attn_flag = jnp.stack([(shift <= SAFE_SHIFT).astype(jnp.int32), shift.astype(jnp.int32)])
    yb = _attn(attn_flag, q, [(k, v), (k_c, v_c)])

    pos = lb > 0.0
    log_lb = jnp.log(jnp.where(pos, lb, 1.0))
    lbc = jnp.stack([1.0 - lb, jnp.log1p(-lb), log_lb, pos.astype(F32), jnp.where(pos, -log_lb, 1e30)], axis=1)
    lbc = jnp.concatenate([lbc, jnp.zeros((2, 3, C_WIDTH), F32)], axis=1)
    o, o_c = _hgrn(zc, zc_c, lbc, ctx_out)

    w_out_bf = w_out.astype(BF16)
    hw = jnp.tile(hgrn_norm_w, C_HEADS)[None, :]
    wr = jnp.zeros((d, ROUTER_LANES), F32).at[:, :N_GROUPS].set(w_grp).at[
        :, N_GROUPS:N_GROUPS + N_EXPERTS].set(w_exp)
    wr_hi = wr.astype(BF16)
    wr = jnp.concatenate([wr_hi, (wr - wr_hi.astype(F32)).astype(BF16)], axis=1)
    br = jnp.zeros((1, ROUTER_LANES), F32).at[0, :N_GROUPS].set(b_grp).at[
        0, N_GROUPS:N_GROUPS + N_EXPERTS].set(b_exp)
    x, h2, lg = _outproj(x, ya, yb, o, g, w_out_bf, g1, norm2_w * (1.0 + sc2), sh2, hw, wr, br)
    if ctx_out:
        ya_c = _gmlp(za_c, ws_bf, bias2d)
        yb_c = _attn(attn_flag, q_c, [(k_c, v_c)])
        xc, h2c, lgc = _outproj(xc, ya_c, yb_c, o_c, g_c, w_out_bf, mod_c[2],
                                norm2_w * (1.0 + mod_c[4]), mod_c[3], hw, wr, br)
        tokens = jnp.concatenate([h2.reshape(PLANES, -1, SC_ROW), h2c.reshape(PLANES, -1, SC_ROW)], axis=1)
        logits = jnp.concatenate([lg.reshape(-1, ROUTER_LANES), lgc.reshape(-1, ROUTER_LANES)], axis=0)
        picked, info = _hier_moe(tokens, logits, w1, w3, w2, layer)
        x = _combine(x, picked, info, g2, 0)
        xc = _combine(xc, picked, info, mod_c[5], b * t)
    else:
        picked, info = _hier_moe(h2.reshape(PLANES, -1, SC_ROW), lg.reshape(-1, ROUTER_LANES), w1, w3, w2, layer)
        x = _combine(x, picked, info, g2, 0)
    return x, xc


def kernel(x, c, ctx, c_ctx, w_mod, b_mod, norm1_w, w_in, w_s, b_s, q_norm_w, k_norm_w, hgrn_lb_logits,
           hgrn_norm_w, w_out, norm2_w, w_grp, b_grp, w_exp, b_exp, w1, w3, w2):
    depth = w_mod.shape[0]
    lb_sm = jax.nn.softmax(hgrn_lb_logits.astype(F32), axis=0)
    lb = jnp.cumsum(lb_sm, axis=0) - lb_sm[0]
    xc = ctx
    for l in range(depth):
        x, xc = _layer(l, x, xc, c, c_ctx, lb[l], w_mod, b_mod[l], norm1_w[l], w_in[l], w_s[l], b_s[l],
                       q_norm_w[l], k_norm_w[l], hgrn_norm_w[l], w_out[l], norm2_w[l], w_grp[l], b_grp[l],
                       w_exp[l], b_exp[l], w1, w3, w2, ctx_out=(l < depth - 1))
    return x
```

```python
import functools

import jax
import jax.numpy as jnp
from jax import lax
from jax.experimental import pallas as pl
from jax.experimental.pallas import tpu as pltpu
from jax.experimental.pallas import tpu_sc as plsc

F32 = jnp.float32
BF16 = jnp.bfloat16

D_MODEL = 1024
HEAD_DIM = 64
GRID_W = 64
EPS = 1e-6
ROPE_BASE = 10000.0
A_WIDTH = D_MODEL // 4
A_HEADS = A_WIDTH // HEAD_DIM
A_CHUNK = 128
B_WIDTH = D_MODEL // 2
B_HEADS = B_WIDTH // HEAD_DIM
B_KV_HEADS = 2
B_GROUP = B_HEADS // B_KV_HEADS
B_KV_WIDTH = B_KV_HEADS * HEAD_DIM
C_WIDTH = D_MODEL // 4
C_HEADS = C_WIDTH // HEAD_DIM
OFF_B = 2 * A_WIDTH
OFF_KV = OFF_B + B_WIDTH
OFF_V = OFF_KV + B_KV_WIDTH
OFF_C = OFF_KV + 2 * B_KV_WIDTH
OFF_G = OFF_C + 4 * C_WIDTH
IN_WIDTH = OFF_G + C_WIDTH
N_GROUPS = 4
EXPERTS_PER_GROUP = 8
N_EXPERTS = N_GROUPS * EXPERTS_PER_GROUP
TOP_K = 2
D_FF_EXPERT = D_MODEL // 2

MOD_ROWS = 16
ROUTER_LANES = 128
HGRN_BLOCK = 32
HGRN_GROUP = 4
LOG2E = 1.4426950408889634
ATTN_KEYS = 512
SAFE_SHIFT = 60
SAFE_DECAY = 80.0
MOE_ROWS = 512
ROUTE_ROWS = 512
SC_WINDOW = 128
SC_ROW = 256
PLANES = D_MODEL // (2 * SC_ROW)
VMEM_LIMIT = 48 * 1024 * 1024


def _cparams(*sem):
    return pltpu.CompilerParams(dimension_semantics=sem, vmem_limit_bytes=VMEM_LIMIT)


def _head_ones(n, dtype):
    r = lax.broadcasted_iota(jnp.int32, (n, n), 0) >> 6
    c = lax.broadcasted_iota(jnp.int32, (n, n), 1) >> 6
    return (r == c).astype(dtype)


def _head_sum(x, ones_bd):
    hi = x.astype(BF16)
    lo = (x - hi.astype(F32)).astype(BF16)
    return (jnp.dot(hi, ones_bd, preferred_element_type=F32)
            + jnp.dot(lo, ones_bd, preferred_element_type=F32))


def _head_rms(x, ones_bd):
    return x * lax.rsqrt(_head_sum(x * x, ones_bd) * (1.0 / HEAD_DIM) + EPS)


def _pack_rows(y):
    bits = lax.bitcast_convert_type(y.astype(BF16).astype(F32), jnp.uint32)
    half = y.shape[1] // 2
    return lax.bitcast_convert_type(bits[:, :half] | (bits[:, half:] >> 16), F32)


def _unpack_rows(w):
    bits = lax.bitcast_convert_type(w, jnp.uint32)
    hi = lax.bitcast_convert_type(bits & jnp.uint32(0xFFFF0000), F32)
    lo = lax.bitcast_convert_type(bits << 16, F32)
    return hi, lo


def _pack_planes(y, ref, lead=()):
    for p in range(PLANES):
        ref[(p,) + lead] = _pack_rows(y[:, 2 * p * SC_ROW:(2 * p + 2) * SC_ROW])


def _mod_kernel(c_ref, w_ref, b_ref, o_ref):
    a = jax.nn.silu(c_ref[...])
    o_ref[...] = jnp.dot(a, w_ref[0], preferred_element_type=F32,
                         precision=lax.Precision.HIGHEST) + b_ref[...]


def _mod(cc, w_mod, b_mod, layer):
    n = w_mod.shape[2]
    tn = 1536
    return pl.pallas_call(
        _mod_kernel,
        out_shape=jax.ShapeDtypeStruct((MOD_ROWS, n), F32),
        grid=(n // tn,),
        in_specs=[pl.BlockSpec((MOD_ROWS, D_MODEL), lambda j: (0, 0)),
                  pl.BlockSpec((1, D_MODEL, tn), lambda j: (layer, 0, j)),
                  pl.BlockSpec((1, tn), lambda j: (0, j))],
        out_specs=pl.BlockSpec((MOD_ROWS, tn), lambda j: (0, j)),
        compiler_params=_cparams("arbitrary"),
        name="mod",
    )(cc, w_mod, b_mod.reshape(1, n))


def _rope(xn, c_ref, sp_ref, sm_ref):
    w = xn.shape[-1]
    return (xn * c_ref[...] + pltpu.roll(xn, 16, 1) * sp_ref[...]
            + pltpu.roll(xn, w - 16, 1) * sm_ref[...])


def _gmlp(z, ws_ref, bias_ref):
    gz = jax.nn.gelu(z)
    u = gz[:, :A_WIDTH]
    vn = _head_rms(gz[:, A_WIDTH:], _head_ones(A_WIDTH, BF16))
    lane_head = lax.broadcasted_iota(jnp.int32, vn.shape, 1) >> 6
    acc = bias_ref[...]
    for hh in range(A_HEADS):
        vh = jnp.where(lane_head == hh, vn, 0.0).astype(BF16)
        acc = acc + jnp.dot(ws_ref[hh], vh, preferred_element_type=F32)
    return u * acc


def _inproj_kernel(x_ref, mul_ref, add_ref, w_ref, qc_ref, qsp_ref, qsm_ref, kc_ref, ksp_ref, ksm_ref,
                   ws_ref, bias_ref, ya_ref, q_ref, k_ref, v_ref, zc_ref, g_ref):
    x = x_ref[0]
    ms = jnp.mean(x * x, axis=-1, keepdims=True)
    h = x * lax.rsqrt(ms + EPS) * mul_ref[0] + add_ref[0]
    y = jnp.dot(h.astype(BF16), w_ref[...], preferred_element_type=F32)
    for c0 in range(0, x.shape[0], A_CHUNK):
        ya_ref[0, c0:c0 + A_CHUNK, :] = _gmlp(y[c0:c0 + A_CHUNK, :OFF_B], ws_ref, bias_ref).astype(BF16)
    qn = _head_rms(y[:, OFF_B:OFF_KV], _head_ones(B_WIDTH, BF16))
    q_ref[0] = _rope(qn, qc_ref, qsp_ref, qsm_ref).astype(BF16)
    kn = _head_rms(y[:, OFF_KV:OFF_V], _head_ones(B_KV_WIDTH, BF16))
    k_ref[0] = _rope(kn, kc_ref, ksp_ref, ksm_ref).astype(BF16)
    v_ref[0] = y[:, OFF_V:OFF_C].astype(BF16)
    zc_ref[0] = y[:, OFF_C:OFF_G]
    g_ref[0] = y[:, OFF_G:]


def _inproj(x, mul, add, w_bf, q_tabs, k_tabs, ws_bf, bias2d):
    b, t, d = x.shape
    tm = min(256, t)
    row = lambda i, bb: (bb, i, 0)
    vec = lambda i, bb: (bb, 0, 0)
    tab = lambda i, bb: (i, 0)
    widths = (A_WIDTH, B_WIDTH, B_KV_WIDTH, B_KV_WIDTH, 4 * C_WIDTH, C_WIDTH)
    dtypes = (BF16, BF16, BF16, BF16, F32, F32)
    return pl.pallas_call(
        _inproj_kernel,
        out_shape=[jax.ShapeDtypeStruct((b, t, w), dt) for w, dt in zip(widths, dtypes)],
        grid=(t // tm, b),
        in_specs=[pl.BlockSpec((1, tm, d), row),
                  pl.BlockSpec((1, 1, d), vec),
                  pl.BlockSpec((1, 1, d), vec),
                  pl.BlockSpec((d, IN_WIDTH), lambda i, bb: (0, 0))]
                 + [pl.BlockSpec((tm, B_WIDTH), tab)] * 3
                 + [pl.BlockSpec((tm, B_KV_WIDTH), tab)] * 3
                 + [pl.BlockSpec((A_HEADS, A_CHUNK, A_CHUNK), lambda i, bb: (0, 0, 0)),
                    pl.BlockSpec((A_CHUNK, A_WIDTH), lambda i, bb: (0, 0))],
        out_specs=[pl.BlockSpec((1, tm, w), row) for w in widths],
        compiler_params=_cparams("arbitrary", "arbitrary"),
        name="inproj",
    )(x, mul, add, w_bf, *q_tabs, *k_tabs, ws_bf, bias2d)


def _rope_tables(t, w, scale, width, rotate):
    ws = w.astype(F32) * scale
    if not rotate:
        c = jnp.broadcast_to(jnp.tile(ws, width // HEAD_DIM)[None, :], (t, width))
        z = jnp.zeros((t, width), F32)
        return c, z, z
    pos = jnp.arange(t)
    row = (pos // GRID_W).astype(F32)
    col = (pos % GRID_W).astype(F32)
    inv_freq = 1.0 / (ROPE_BASE ** (jnp.arange(0, HEAD_DIM // 2, 2, dtype=F32) / (HEAD_DIM // 2)))
    dd = jnp.arange(HEAD_DIM)
    axis = dd // 32
    half = (dd % 32) // 16
    ang = jnp.where(axis[None, :] == 0, row[:, None], col[:, None]) * inv_freq[dd % 16][None, :]
    cos, sin = jnp.cos(ang), jnp.sin(ang)
    c = cos * ws[None, :]
    sm = jnp.where(half[None, :] == 0, -sin * jnp.roll(ws, -16)[None, :], 0.0)
    sp = jnp.where(half[None, :] == 1, sin * jnp.roll(ws, 16)[None, :], 0.0)
    rep = width // HEAD_DIM
    return jnp.tile(c, (1, rep)), jnp.tile(sp, (1, rep)), jnp.tile(sm, (1, rep))


def _attn_kernel(flag_ref, q_ref, *refs, n_seg):
    kv_refs, o_ref = refs[:2 * n_seg], refs[2 * n_seg]
    tq = q_ref.shape[1]
    dh = HEAD_DIM

    def heads(j):
        q4 = jnp.concatenate([q_ref[0, :, (B_GROUP * j + gg) * dh:(B_GROUP * j + gg + 1) * dh]
                              for gg in range(B_GROUP)], axis=0)
        ks, vs = [], []
        for sg in range(n_seg):
            s_len = kv_refs[2 * sg].shape[1]
            for c0 in range(0, s_len, ATTN_KEYS):
                c1 = min(c0 + ATTN_KEYS, s_len)
                ks.append(kv_refs[2 * sg][0, c0:c1, j * dh:(j + 1) * dh])
                vs.append(kv_refs[2 * sg + 1][0, c0:c1, j * dh:(j + 1) * dh])
        return q4, ks, vs

    def scores(q4, ks):
        return [lax.dot_general(q4, kk, (((1,), (1,)), ((), ())), preferred_element_type=F32) for kk in ks]

    def finish(j, ps, vs):
        cols = [p[:, c:c + 128] for p in ps for c in range(0, p.shape[1], 128)]
        l = jnp.sum(functools.reduce(jnp.add, cols), axis=-1, keepdims=True)
        acc = sum(jnp.dot(p.astype(BF16), vv, preferred_element_type=F32) for p, vv in zip(ps, vs))
        o = acc / l
        for gg in range(B_GROUP):
            hh = B_GROUP * j + gg
            o_ref[0, :, hh * dh:(hh + 1) * dh] = o[gg * tq:(gg + 1) * tq].astype(BF16)

    @pl.when(flag_ref[0] > 0)
    def _():
        shift = flag_ref[1].astype(F32)
        for j in range(B_KV_HEADS):
            q4, ks, vs = heads(j)
            finish(j, [jnp.exp2(s - shift) for s in scores(q4, ks)], vs)

    @pl.when(flag_ref[0] <= 0)
    def _():
        for j in range(B_KV_HEADS):
            q4, ks, vs = heads(j)
            ss = scores(q4, ks)
            m = functools.reduce(jnp.maximum, [jnp.max(s, axis=-1, keepdims=True) for s in ss])
            finish(j, [jnp.exp2(s - m) for s in ss], vs)


def _attn(flag, q, kv_segs):
    b, t, w = q.shape
    tq = min(128, t)
    n_seg = len(kv_segs)
    kv_flat, kv_specs = [], []
    for kk, vv in kv_segs:
        s_len, kw = kk.shape[1:]
        kv_flat += [kk, vv]
        kv_specs += [pl.BlockSpec((1, s_len, kw), lambda bb, i, fl: (bb, 0, 0))] * 2
    return pl.pallas_call(
        functools.partial(_attn_kernel, n_seg=n_seg),
        out_shape=jax.ShapeDtypeStruct((b, t, w), BF16),
        grid_spec=pltpu.PrefetchScalarGridSpec(
            num_scalar_prefetch=1,
            grid=(b, t // tq),
            in_specs=[pl.BlockSpec((1, tq, w), lambda bb, i, fl: (bb, i, 0))] + kv_specs,
            out_specs=pl.BlockSpec((1, tq, w), lambda bb, i, fl: (bb, i, 0))),
        compiler_params=_cparams("arbitrary", "arbitrary"),
        name="attn",
    )(flag, q, *kv_flat)


def _scan_rows(x, reverse):
    n = x.shape[0]
    rows = lax.broadcasted_iota(jnp.int32, x.shape, 0)
    sh = 1
    while sh < n:
        if reverse:
            x = x + jnp.where(rows < n - sh, pltpu.roll(x, n - sh, 0), 0.0)
        else:
            x = x + jnp.where(rows >= sh, pltpu.roll(x, sh, 0), 0.0)
        sh *= 2
    return x


def _stack_heads(x, lane_head):
    return jnp.concatenate([jnp.where(lane_head == hh, x, 0.0) for hh in range(C_HEADS)], axis=0)


def _hgrn_kernel(z_ref, zc_ref, lbc_ref, *refs, ctx_out):
    if ctx_out:
        o_ref, oc_ref, st_ref, kx_ref, bx_ref, vx_ref, flag_ref = refs
    else:
        o_ref, st_ref, kx_ref, bx_ref, vx_ref, flag_ref = refs
        oc_ref = None
    n = C_WIDTH
    nb = HGRN_BLOCK
    nblk_c = zc_ref.shape[1] // nb
    nblk_l = z_ref.shape[1] // nb
    ones_bd = _head_ones(n, BF16)
    rows = lax.broadcasted_iota(jnp.int32, (nb, n), 0)
    lane_head = lax.broadcasted_iota(jnp.int32, (nb, n), 1) >> 6
    lane_head64 = lax.broadcasted_iota(jnp.int32, (HEAD_DIM, n), 1) >> 6
    low_half = (lax.broadcasted_iota(jnp.int32, (HEAD_DIM, 2 * HEAD_DIM), 1) < HEAD_DIM)
    sc_t = lax.broadcasted_iota(jnp.int32, (nb, C_HEADS * nb), 0)
    sc_s = lax.broadcasted_iota(jnp.int32, (nb, C_HEADS * nb), 1) & (nb - 1)

    def group_blocks(nblk, i):
        grp = HGRN_GROUP if nblk % HGRN_GROUP == 0 else 1
        fwd = [i * grp + gg for gg in range(grp)]
        return grp, fwd, [nblk - 1 - blk for blk in fwd]

    def flag_groups(src_ref, nblk, base):
        def body(i, carry):
            _, fwd, bwd = group_blocks(nblk, i)
            worst = None
            for d, blks in ((0, fwd), (1, bwd)):
                for blk in blks:
                    zz = src_ref[0, pl.ds(pl.multiple_of(blk * nb, nb), nb), (1 + d) * n:(2 + d) * n]
                    step_bound = jnp.minimum(lbc_ref[d, 4:5, :],
                                             jnp.maximum(-zz, 0.0) + (jnp.log(2.0) - lbc_ref[d, 1:2, :]))
                    bound = jnp.sum(step_bound, axis=0, keepdims=True)
                    worst = bound if worst is None else jnp.maximum(worst, bound)
            flag_ref[base + i] = (jnp.max(worst) <= SAFE_DECAY).astype(jnp.int32)
            return carry
        lax.fori_loop(0, nblk // group_blocks(nblk, 0)[0], body, 0)

    n_grp_c = nblk_c // group_blocks(nblk_c, 0)[0]
    flag_groups(zc_ref, nblk_c, 0)
    flag_groups(z_ref, nblk_l, n_grp_c)
    st_ref[...] = jnp.zeros_like(st_ref)
    o_ref[...] = jnp.zeros_like(o_ref)
    if ctx_out:
        oc_ref[...] = jnp.zeros_like(oc_ref)

    def step(src_ref, dst_ref, blk, d, fast):
        reverse = d == 1
        r0 = pl.multiple_of(blk * nb, nb)
        z = src_ref[0, pl.ds(r0, nb), (1 + d) * n:(2 + d) * n]
        v = src_ref[0, pl.ds(r0, nb), 3 * n:4 * n]
        one_m_lb = lbc_ref[d, 0:1, :]
        log1m_lb = lbc_ref[d, 1:2, :]
        log_lb = lbc_ref[d, 2:3, :]
        lb_pos = lbc_ref[d, 3:4, :] > 0.5
        soft = jnp.log1p(jnp.exp(-jnp.abs(z)))
        log_rest = log1m_lb + (jnp.minimum(z, 0.0) - soft)
        lse = jnp.maximum(log_lb, log_rest) + jnp.log1p(jnp.exp(-jnp.abs(log_lb - log_rest)))
        log_f = jnp.where(lb_pos, lse, log_rest)
        k = one_m_lb * jnp.exp(jnp.minimum(-z, 0.0) - soft)
        bc = _scan_rows(log_f, reverse)
        edge = 0 if reverse else nb - 1
        b_edge = bc[edge:edge + 1, :]
        st = st_ref[d]
        v_bf = v.astype(BF16)

        if dst_ref is not None:
            q = jax.nn.silu(src_ref[0, pl.ds(r0, nb), 0:n])
            qt = (q * jnp.exp(bc)).astype(BF16)
            o = lax.dot_general(qt, _stack_heads(st, lane_head64).astype(BF16), (((1,), (1,)), ((), ())),
                                preferred_element_type=F32)

            def intra_fast():
                kt = _stack_heads(k * jnp.exp(-bc), lane_head).astype(BF16)
                sc = lax.dot_general(qt, kt, (((1,), (1,)), ((), ())), preferred_element_type=F32)
                keep = (sc_s >= sc_t) if reverse else (sc_s <= sc_t)
                sc = jnp.where(keep, sc, 0.0).astype(BF16)
                return jnp.dot(sc, _stack_heads(v, lane_head).astype(BF16), preferred_element_type=F32)

            def intra_exact():
                kx_ref[d] = k
                bx_ref[d] = bc
                vx_ref[d] = v

                def sbody(s, acc):
                    keep = (rows <= s) if reverse else (rows >= s)
                    e = jnp.exp(jnp.where(keep, bc - bx_ref[d, pl.ds(s, 1), :], 0.0))
                    p = jnp.where(keep, q * e * kx_ref[d, pl.ds(s, 1), :], 0.0)
                    sc = jnp.dot(p.astype(BF16), ones_bd, preferred_element_type=F32)
                    return acc + sc * vx_ref[d, pl.ds(s, 1), :]

                return lax.fori_loop(0, nb, sbody, jnp.zeros((nb, n), F32))

            o = o + (intra_fast() if fast else intra_exact())
            dst_ref[0, pl.ds(r0, nb), :] += o

        kd = (k * jnp.exp(b_edge - bc)).astype(BF16)
        full = lax.dot_general(v_bf, kd, (((0,), (0,)), ((), ())), preferred_element_type=F32)
        upd = jnp.concatenate(
            [jnp.where(low_half,
                       full[(2 * c) * HEAD_DIM:(2 * c + 1) * HEAD_DIM, 2 * c * HEAD_DIM:(2 * c + 2) * HEAD_DIM],
                       full[(2 * c + 1) * HEAD_DIM:(2 * c + 2) * HEAD_DIM, 2 * c * HEAD_DIM:(2 * c + 2) * HEAD_DIM])
             for c in range(C_HEADS // 2)], axis=1)
        st_ref[d] = st * jnp.exp(b_edge) + upd

    def run(src_ref, dst_ref, nblk, base):
        def body(i, carry):
            _, fwd, bwd = group_blocks(nblk, i)
            safe = flag_ref[base + i]

            def group(fast):
                for bf, bb in zip(fwd, bwd):
                    step(src_ref, dst_ref, bf, 0, fast)
                    step(src_ref, dst_ref, bb, 1, fast)

            pl.when(safe > 0)(functools.partial(group, True))
            pl.when(safe <= 0)(functools.partial(group, False))
            return carry
        lax.fori_loop(0, nblk // group_blocks(nblk, 0)[0], body, 0)

    run(zc_ref, oc_ref, nblk_c, 0)
    run(z_ref, o_ref, nblk_l, n_grp_c)


def _hgrn(zc, zc_c, lbc, ctx_out):
    b, t, w = zc.shape
    lc = zc_c.shape[1]
    n = C_WIDTH
    row = lambda bb: (bb, 0, 0)
    out_shape = [jax.ShapeDtypeStruct((b, t, n), F32)]
    out_specs = [pl.BlockSpec((1, t, n), row)]
    if ctx_out:
        out_shape.append(jax.ShapeDtypeStruct((b, lc, n), F32))
        out_specs.append(pl.BlockSpec((1, lc, n), row))
    res = pl.pallas_call(
        functools.partial(_hgrn_kernel, ctx_out=ctx_out),
        out_shape=out_shape,
        grid=(b,),
        in_specs=[pl.BlockSpec((1, t, w), row),
                  pl.BlockSpec((1, lc, w), row),
                  pl.BlockSpec((2, 8, n), lambda bb: (0, 0, 0))],
        out_specs=out_specs,
        scratch_shapes=[pltpu.VMEM((2, HEAD_DIM, n), F32)]
                       + [pltpu.VMEM((2, HGRN_BLOCK, n), F32)] * 3
                       + [pltpu.SMEM(((t + lc) // HGRN_BLOCK,), jnp.int32)],
        compiler_params=_cparams("arbitrary"),
        name="hgrn",
    )(zc, zc_c, lbc)
    return (res[0], res[1]) if ctx_out else (res[0], None)


def _outproj_kernel(x_ref, ya_ref, yb_ref, o_ref, g_ref, w_ref, gate_ref, mul_ref, add_ref, hw_ref,
                    wr_ref, br_ref, xo_ref, h2_ref, lg_ref):
    yc = _head_rms(o_ref[0], _head_ones(C_WIDTH, BF16)) * hw_ref[...] * jax.nn.silu(g_ref[0])
    y = jnp.dot(ya_ref[0], w_ref[0:A_WIDTH, :], preferred_element_type=F32)
    y = y + jnp.dot(yb_ref[0], w_ref[A_WIDTH:A_WIDTH + B_WIDTH, :], preferred_element_type=F32)
    y = y + jnp.dot(yc.astype(BF16), w_ref[A_WIDTH + B_WIDTH:, :], preferred_element_type=F32)
    xn = x_ref[0] + gate_ref[0] * y
    xo_ref[0] = xn
    ms = jnp.mean(xn * xn, axis=-1, keepdims=True)
    h2 = xn * lax.rsqrt(ms + EPS) * mul_ref[0] + add_ref[0]
    _pack_planes(h2, h2_ref, (0,))
    h_hi = h2.astype(BF16)
    h_lo = (h2 - h_hi.astype(F32)).astype(BF16)
    both = jnp.dot(h_hi, wr_ref[...], preferred_element_type=F32)
    lg_ref[0] = (both[:, :ROUTER_LANES] + both[:, ROUTER_LANES:] + br_ref[...]
                 + jnp.dot(h_lo, wr_ref[:, 0:ROUTER_LANES], preferred_element_type=F32))


def _outproj(x, ya, yb, o, g, w_bf, gate, mul, add, hw, wr, br):
    b, t, d = x.shape
    tm = min(256, t)
    row = lambda bb, i: (bb, i, 0)
    vec = lambda bb, i: (bb, 0, 0)
    const = lambda bb, i: (0, 0)
    return pl.pallas_call(
        _outproj_kernel,
        out_shape=[jax.ShapeDtypeStruct((b, t, d), F32),
                   jax.ShapeDtypeStruct((PLANES, b, t, SC_ROW), F32),
                   jax.ShapeDtypeStruct((b, t, ROUTER_LANES), F32)],
        grid=(b, t // tm),
        in_specs=[pl.BlockSpec((1, tm, d), row),
                  pl.BlockSpec((1, tm, A_WIDTH), row),
                  pl.BlockSpec((1, tm, B_WIDTH), row),
                  pl.BlockSpec((1, tm, C_WIDTH), row),
                  pl.BlockSpec((1, tm, C_WIDTH), row),
                  pl.BlockSpec((d, d), const),
                  pl.BlockSpec((1, 1, d), vec),
                  pl.BlockSpec((1, 1, d), vec),
                  pl.BlockSpec((1, 1, d), vec),
                  pl.BlockSpec((1, C_WIDTH), const),
                  pl.BlockSpec((d, 2 * ROUTER_LANES), const),
                  pl.BlockSpec((1, ROUTER_LANES), const)],
        out_specs=[pl.BlockSpec((1, tm, d), row),
                   pl.BlockSpec((PLANES, 1, tm, SC_ROW), lambda bb, i: (0, bb, i, 0)),
                   pl.BlockSpec((1, tm, ROUTER_LANES), row)],
        compiler_params=_cparams("arbitrary", "arbitrary"),
        name="outproj",
    )(x, ya, yb, o, g, w_bf, gate, mul, add, hw, wr, br)


def _route_kernel(lg_ref, info_ref, meta_ref, cnt_ref, base_ref):
    phase = pl.program_id(0)
    i = pl.program_id(1)
    tm = lg_ref.shape[0]
    lane = lax.broadcasted_iota(jnp.int32, (tm, ROUTER_LANES), 1)
    lane_f = lane.astype(F32)
    lg = lg_ref[...]
    neg = -jnp.inf
    gl = jnp.where(lane < N_GROUPS, lg, neg)
    gmax = jnp.max(gl, axis=1, keepdims=True)
    grp = jnp.min(jnp.where(gl == gmax, lane_f, float(ROUTER_LANES)), axis=1, keepdims=True).astype(jnp.int32)
    p_grp = 1.0 / jnp.sum(jnp.exp(gl - gmax), axis=1, keepdims=True)
    in_grp = (lane >= N_GROUPS) & (lane < N_GROUPS + N_EXPERTS) & (((lane - N_GROUPS) >> 3) == grp)
    el = jnp.where(in_grp, lg, neg)
    v1 = jnp.max(el, axis=1, keepdims=True)
    i1 = jnp.min(jnp.where(el == v1, lane_f, float(ROUTER_LANES)), axis=1, keepdims=True)
    el2 = jnp.where(lane_f == i1, neg, el)
    v2 = jnp.max(el2, axis=1, keepdims=True)
    i2 = jnp.min(jnp.where(el2 == v2, lane_f, float(ROUTER_LANES)), axis=1, keepdims=True)
    hit1 = lane_f == i1
    hit2 = lane_f == i2
    onehot = (hit1 | hit2).astype(F32)
    tile_counts = jnp.sum(onehot, axis=0, keepdims=True)

    @pl.when(phase == 0)
    def _():
        @pl.when(i == 0)
        def _():
            cnt_ref[...] = jnp.zeros_like(cnt_ref)
        cnt_ref[...] += tile_counts
        info_ref[...] = jnp.zeros_like(info_ref)
        meta_ref[...] = jnp.zeros_like(meta_ref)

    @pl.when(phase == 1)
    def _():
        @pl.when(i == 0)
        def _():
            counts = jnp.broadcast_to(cnt_ref[...], (8, ROUTER_LANES))
            padded = jnp.floor((counts + (MOE_ROWS - 1.0)) * (1.0 / MOE_ROWS)) * MOE_ROWS
            r = lax.broadcasted_iota(jnp.int32, (ROUTER_LANES, ROUTER_LANES), 0)
            c = lax.broadcasted_iota(jnp.int32, (ROUTER_LANES, ROUTER_LANES), 1)
            ends = jnp.dot(padded, (r <= c).astype(F32), preferred_element_type=F32,
                           precision=lax.Precision.HIGHEST)
            base_ref[...] = (ends - padded)[0:1]
            row = lax.broadcasted_iota(jnp.int32, (8, ROUTER_LANES), 0)
            meta_ref[...] = jnp.where(row == 0, counts, jnp.where(row == 1, ends - padded, ends))

        tr = lax.broadcasted_iota(jnp.int32, (tm, tm), 0)
        tc = lax.broadcasted_iota(jnp.int32, (tm, tm), 1)
        before = jnp.dot((tc < tr).astype(BF16), onehot.astype(BF16), preferred_element_type=F32)
        pos = base_ref[...] + before
        d1 = jnp.sum(jnp.where(hit1, pos, 0.0), axis=1, keepdims=True)
        d2 = jnp.sum(jnp.where(hit2, pos, 0.0), axis=1, keepdims=True)
        base_ref[...] += tile_counts
        rr = jnp.exp(v2 - v1)
        g1 = p_grp / (1.0 + rr)
        g2 = p_grp * rr / (1.0 + rr)
        info_ref[...] = jnp.where(lane == 0, d1, jnp.where(lane == 1, d2, jnp.where(
            lane == 2, g1, jnp.where(lane == 3, g2, 0.0))))


def _route(logits):
    n = logits.shape[0]
    tm = ROUTE_ROWS if n % ROUTE_ROWS == 0 else ROUTE_ROWS // 2
    return pl.pallas_call(
        _route_kernel,
        out_shape=[jax.ShapeDtypeStruct((n, ROUTER_LANES), F32),
                   jax.ShapeDtypeStruct((8, ROUTER_LANES), F32)],
        grid=(2, n // tm),
        in_specs=[pl.BlockSpec((tm, ROUTER_LANES), lambda p, i: (i, 0))],
        out_specs=[pl.BlockSpec((tm, ROUTER_LANES), lambda p, i: (p * i, 0)),
                   pl.BlockSpec((8, ROUTER_LANES), lambda p, i: (0, 0))],
        scratch_shapes=[pltpu.VMEM((1, ROUTER_LANES), F32), pltpu.VMEM((1, ROUTER_LANES), F32)],
        compiler_params=_cparams("arbitrary", "arbitrary"),
        name="route",
    )(logits)


def _sc_mesh():
    return plsc.VectorSubcoreMesh(core_axis_name="c", subcore_axis_name="s")


def _sc_gather(table, idx):
    n = idx.shape[0]
    d = table.shape[1]

    @functools.partial(pl.kernel, out_type=jax.ShapeDtypeStruct((n, d), table.dtype), mesh=_sc_mesh())
    def gather(x_hbm, i_hbm, o_hbm):
        def body(i_vmem, o_vmem):
            pltpu.sync_copy(x_hbm.at[i_vmem.at[0]], o_vmem)

        pltpu.emit_pipeline(
            body,
            grid=(n // SC_WINDOW,),
            in_specs=[pl.BlockSpec((1, SC_WINDOW), lambda i: (0, i))],
            out_specs=[pl.BlockSpec((SC_WINDOW, d), lambda i: (i, 0))],
            core_axis_name=("c", "s"),
            dimension_semantics=(pltpu.PARALLEL,),
        )(i_hbm, o_hbm)

    return gather(table, idx.reshape(1, n))


def _sc_scatter2(rows, idx0, idx1, n_out):
    m, d = rows.shape

    @functools.partial(pl.kernel, out_type=jax.ShapeDtypeStruct((n_out, d), rows.dtype), mesh=_sc_mesh())
    def scatter(x_hbm, i0_hbm, i1_hbm, o_hbm):
        def body(x_vmem, i0_vmem, i1_vmem):
            pltpu.sync_copy(x_vmem, o_hbm.at[i0_vmem.at[0]])
            pltpu.sync_copy(x_vmem, o_hbm.at[i1_vmem.at[0]])

        pltpu.emit_pipeline(
            body,
            grid=(m // SC_WINDOW,),
            in_specs=[pl.BlockSpec((SC_WINDOW, d), lambda i: (i, 0)),
                      pl.BlockSpec((1, SC_WINDOW), lambda i: (0, i)),
                      pl.BlockSpec((1, SC_WINDOW), lambda i: (0, i))],
            out_specs=[],
            core_axis_name=("c", "s"),
            dimension_semantics=(pltpu.PARALLEL,),
        )(x_hbm, i0_hbm, i1_hbm)

    return scatter(rows, idx0.reshape(1, m), idx1.reshape(1, m))


def _moe_kernel(be_ref, nu_ref, x_ref, w1_ref, w3_ref, w2_ref, o_ref, w1b, w3b, w2b):
    i = pl.program_id(0)
    e = be_ref[i]
    prev = be_ref[jnp.maximum(i - 1, 0)]

    @pl.when((i == 0) | (e != prev))
    def _():
        w1b[...] = w1_ref[0, 0].astype(BF16)
        w3b[...] = w3_ref[0, 0].astype(BF16)
        w2b[...] = w2_ref[0, 0].astype(BF16)

    @pl.when(i < nu_ref[0])
    def _():
        parts = [h.astype(BF16) for p in range(PLANES) for h in _unpack_rows(x_ref[p])]
        a = sum(jnp.dot(h, w1b[q * SC_ROW:(q + 1) * SC_ROW, :], preferred_element_type=F32)
                for q, h in enumerate(parts))
        b = sum(jnp.dot(h, w3b[q * SC_ROW:(q + 1) * SC_ROW, :], preferred_element_type=F32)
                for q, h in enumerate(parts))
        hmid = (jax.nn.silu(a) * b).astype(BF16)
        _pack_planes(jnp.dot(hmid, w2b[...], preferred_element_type=F32), o_ref)

    @pl.when(i >= nu_ref[0])
    def _():
        o_ref[...] = jnp.zeros_like(o_ref)


def _moe_mlp(blk_expert, n_used, xs, w1, w3, w2, layer):
    n_rows = xs.shape[1]
    d, f = w1.shape[2:]
    nblk = n_rows // MOE_ROWS
    rows = lambda i, be, nu: (0, i, 0)
    return pl.pallas_call(
        _moe_kernel,
        out_shape=jax.ShapeDtypeStruct((PLANES, n_rows, SC_ROW), F32),
        grid_spec=pltpu.PrefetchScalarGridSpec(
            num_scalar_prefetch=2,
            grid=(nblk,),
            in_specs=[pl.BlockSpec((PLANES, MOE_ROWS, SC_ROW), rows),
                      pl.BlockSpec((1, 1, d, f), lambda i, be, nu: (layer, be[i], 0, 0)),
                      pl.BlockSpec((1, 1, d, f), lambda i, be, nu: (layer, be[i], 0, 0)),
                      pl.BlockSpec((1, 1, f, d), lambda i, be, nu: (layer, be[i], 0, 0))],
            out_specs=pl.BlockSpec((PLANES, MOE_ROWS, SC_ROW), rows),
            scratch_shapes=[pltpu.VMEM((d, f), BF16), pltpu.VMEM((d, f), BF16), pltpu.VMEM((f, d), BF16)]),
        compiler_params=_cparams("arbitrary"),
        name="moe",
    )(blk_expert, n_used, xs, w1, w3, w2)


def _combine_kernel(x_ref, pk_ref, info_ref, g_ref, o_ref):
    info = info_ref[...]
    g1 = info[:, 2:3]
    g2 = info[:, 3:4]
    parts = []
    for p in range(PLANES):
        hi1, lo1 = _unpack_rows(pk_ref[TOP_K * p])
        hi2, lo2 = _unpack_rows(pk_ref[TOP_K * p + 1])
        parts += [g1 * hi1 + g2 * hi2, g1 * lo1 + g2 * lo2]
    o_ref[0] = x_ref[0] + g_ref[0] * jnp.concatenate(parts, axis=1)


def _combine(x, picked, info, gate, row0):
    b, t, d = x.shape
    tm = min(256, t)
    off = row0 // tm
    tok = lambda bb, i: (off + bb * (t // tm) + i, 0)
    tok3 = lambda bb, i: (0, off + bb * (t // tm) + i, 0)
    return pl.pallas_call(
        _combine_kernel,
        out_shape=jax.ShapeDtypeStruct((b, t, d), F32),
        grid=(b, t // tm),
        in_specs=[pl.BlockSpec((1, tm, d), lambda bb, i: (bb, i, 0)),
                  pl.BlockSpec((PLANES * TOP_K, tm, SC_ROW), tok3),
                  pl.BlockSpec((tm, ROUTER_LANES), tok),
                  pl.BlockSpec((1, 1, d), lambda bb, i: (bb, 0, 0))],
        out_specs=pl.BlockSpec((1, tm, d), lambda bb, i: (bb, i, 0)),
        compiler_params=_cparams("arbitrary", "arbitrary"),
        name="combine",
    )(x, picked, info, gate)


def _hier_moe(h2p, logits, w1, w3, w2, layer):
    n_tok = h2p.shape[1]
    info, meta = _route(logits)
    dest = info[:, 0:TOP_K].astype(jnp.int32)
    pad_ends = meta[2, N_GROUPS:N_GROUPS + N_EXPERTS].astype(jnp.int32)
    nblk = -(-(n_tok * TOP_K) // MOE_ROWS) + N_EXPERTS
    n_rows = nblk * MOE_ROWS
    blk_start = jnp.arange(nblk, dtype=jnp.int32) * MOE_ROWS
    blk_expert = jnp.minimum(jnp.sum((pad_ends[None, :] <= blk_start[:, None]).astype(jnp.int32), axis=1),
                             N_EXPERTS - 1)
    n_used = pad_ends[-1:] // MOE_ROWS
    slot = [jnp.concatenate([p * n_rows + dest[:, s] for p in range(PLANES)]) for s in range(TOP_K)]
    xs = _sc_scatter2(h2p.reshape(PLANES * n_tok, SC_ROW), slot[0], slot[1], PLANES * n_rows)
    out = _moe_mlp(blk_expert, n_used, xs.reshape(PLANES, n_rows, SC_ROW), w1, w3, w2, layer)
    idx_all = jnp.concatenate([p * n_rows + dest[:, s] for p in range(PLANES) for s in range(TOP_K)])
    picked = _sc_gather(out.reshape(PLANES * n_rows, SC_ROW), idx_all)
    return picked.reshape(PLANES * TOP_K, n_tok, SC_ROW), info


def _layer(layer, x, xc, c, c_ctx, lb, w_mod, b_mod, norm1_w, w_in, w_s, b_s, q_norm_w, k_norm_w, hgrn_norm_w, w_out,
           norm2_w, w_grp, b_grp, w_exp, b_exp, w1, w3, w2, ctx_out):
    b, t, d = x.shape
    lc = xc.shape[1]
    cc = jnp.zeros((MOD_ROWS, d), F32).at[:b].set(c).at[b].set(c_ctx)
    mod = _mod(cc, w_mod, b_mod, layer)
    sh1, sc1, g1, sh2, sc2, g2 = [m[:, None, :] for m in jnp.split(mod[:b], 6, axis=-1)]
    mod_c = [jnp.broadcast_to(m[None, None, :], (b, 1, d)) for m in jnp.split(mod[b], 6)]

    w_in_bf = w_in.astype(BF16)
    scale = LOG2E * HEAD_DIM ** -0.5
    q_tabs = _rope_tables(t, q_norm_w, scale, B_WIDTH, True)
    k_tabs = _rope_tables(t, k_norm_w, 1.0, B_KV_WIDTH, True)
    qc_tabs = _rope_tables(lc, q_norm_w, scale, B_WIDTH, False)
    kc_tabs = _rope_tables(lc, k_norm_w, 1.0, B_KV_WIDTH, False)
    ws_bf = w_s.astype(BF16)
    bias2d = jnp.repeat(b_s.T, HEAD_DIM, axis=1)
    ya, q, k, v, zc, g = _inproj(x, norm1_w * (1.0 + sc1), sh1, w_in_bf, q_tabs, k_tabs, ws_bf, bias2d)
    ya_c, q_c, k_c, v_c, zc_c, g_c = _inproj(xc, norm1_w * (1.0 + mod_c[1]), mod_c[0], w_in_bf,
                                             qc_tabs, kc_tabs, ws_bf, bias2d)

    bound = LOG2E * HEAD_DIM ** 0.5 * jnp.max(jnp.abs(q_norm_w)) * jnp.max(jnp.abs(k_norm_w)) * 1.02
    shift = jnp.ceil(bound)
    attn_flag = jnp.stack([(shift <= SAFE_SHIFT).astype(jnp.int32), shift.astype(jnp.int32)])
    yb = _attn(attn_flag, q, [(k, v), (k_c, v_c)])

    pos = lb > 0.0
    log_lb = jnp.log(jnp.where(pos, lb, 1.0))
    lbc = jnp.stack([1.0 - lb, jnp.log1p(-lb), log_lb, pos.astype(F32), jnp.where(pos, -log_lb, 1e30)], axis=1)
    lbc = jnp.concatenate([lbc, jnp.zeros((2, 3, C_WIDTH), F32)], axis=1)
    o, o_c = _hgrn(zc, zc_c, lbc, ctx_out)

    w_out_bf = w_out.astype(BF16)
    hw = jnp.tile(hgrn_norm_w, C_HEADS)[None, :]
    wr = jnp.zeros((d, ROUTER_LANES), F32).at[:, :N_GROUPS].set(w_grp).at[
        :, N_GROUPS:N_GROUPS + N_EXPERTS].set(w_exp)
    wr_hi = wr.astype(BF16)
    wr = jnp.concatenate([wr_hi, (wr - wr_hi.astype(F32)).astype(BF16)], axis=1)
    br = jnp.zeros((1, ROUTER_LANES), F32).at[0, :N_GROUPS].set(b_grp).at[
        0, N_GROUPS:N_GROUPS + N_EXPERTS].set(b_exp)
    x, h2, lg = _outproj(x, ya, yb, o, g, w_out_bf, g1, norm2_w * (1.0 + sc2), sh2, hw, wr, br)
    if ctx_out:
        yb_c = _attn(attn_flag, q_c, [(k_c, v_c)])
        xc, h2c, lgc = _outproj(xc, ya_c, yb_c, o_c, g_c, w_out_bf, mod_c[2],
                                norm2_w * (1.0 + mod_c[4]), mod_c[3], hw, wr, br)
        tokens = jnp.concatenate([h2.reshape(PLANES, -1, SC_ROW), h2c.reshape(PLANES, -1, SC_ROW)], axis=1)
        logits = jnp.concatenate([lg.reshape(-1, ROUTER_LANES), lgc.reshape(-1, ROUTER_LANES)], axis=0)
        picked, info = _hier_moe(tokens, logits, w1, w3, w2, layer)
        x = _combine(x, picked, info, g2, 0)
        xc = _combine(xc, picked, info, mod_c[5], b * t)
    else:
        picked, info = _hier_moe(h2.reshape(PLANES, -1, SC_ROW), lg.reshape(-1, ROUTER_LANES), w1, w3, w2, layer)
        x = _combine(x, picked, info, g2, 0)
    return x, xc


def kernel(x, c, ctx, c_ctx, w_mod, b_mod, norm1_w, w_in, w_s, b_s, q_norm_w, k_norm_w, hgrn_lb_logits,
           hgrn_norm_w, w_out, norm2_w, w_grp, b_grp, w_exp, b_exp, w1, w3, w2):
    depth = w_mod.shape[0]
    lb_sm = jax.nn.softmax(hgrn_lb_logits.astype(F32), axis=0)
    lb = jnp.cumsum(lb_sm, axis=0) - lb_sm[0]
    xc = ctx
    for l in range(depth):
        x, xc = _layer(l, x, xc, c, c_ctx, lb[l], w_mod, b_mod[l], norm1_w[l], w_in[l], w_s[l], b_s[l],
                       q_norm_w[l], k_norm_w[l], hgrn_norm_w[l], w_out[l], norm2_w[l], w_grp[l], b_grp[l],
                       w_exp[l], b_exp[l], w1, w3, w2, ctx_out=(l < depth - 1))
    return x
```

```python
import functools

import jax
import jax.numpy as jnp
from jax import lax
from jax.experimental import pallas as pl
from jax.experimental.pallas import tpu as pltpu
from jax.experimental.pallas import tpu_sc as plsc

F32 = jnp.float32
BF16 = jnp.bfloat16

D_MODEL = 1024
HEAD_DIM = 64
GRID_W = 64
EPS = 1e-6
ROPE_BASE = 10000.0
A_WIDTH = D_MODEL // 4
A_HEADS = A_WIDTH // HEAD_DIM
A_CHUNK = 128
B_WIDTH = D_MODEL // 2
B_HEADS = B_WIDTH // HEAD_DIM
B_KV_HEADS = 2
B_GROUP = B_HEADS // B_KV_HEADS
B_KV_WIDTH = B_KV_HEADS * HEAD_DIM
C_WIDTH = D_MODEL // 4
C_HEADS = C_WIDTH // HEAD_DIM
OFF_B = 2 * A_WIDTH
OFF_KV = OFF_B + B_WIDTH
OFF_V = OFF_KV + B_KV_WIDTH
OFF_C = OFF_KV + 2 * B_KV_WIDTH
OFF_G = OFF_C + 4 * C_WIDTH
IN_WIDTH = OFF_G + C_WIDTH
N_GROUPS = 4
EXPERTS_PER_GROUP = 8
N_EXPERTS = N_GROUPS * EXPERTS_PER_GROUP
TOP_K = 2
D_FF_EXPERT = D_MODEL // 2

MOD_ROWS = 16
ROUTER_LANES = 128
HGRN_BLOCK = 32
HGRN_GROUP = 4
LOG2E = 1.4426950408889634
ATTN_ROWS = 256
ATTN_KEYS = 512
SAFE_SHIFT = 60
SAFE_DECAY = 80.0
MOE_ROWS = 512
ROUTE_ROWS = 512
SC_WINDOW = 128
SC_ROW = 256
PLANES = D_MODEL // (2 * SC_ROW)
VMEM_LIMIT = 48 * 1024 * 1024


def _cparams(*sem):
    return pltpu.CompilerParams(dimension_semantics=sem, vmem_limit_bytes=VMEM_LIMIT)


def _head_ones(n, dtype):
    r = lax.broadcasted_iota(jnp.int32, (n, n), 0) >> 6
    c = lax.broadcasted_iota(jnp.int32, (n, n), 1) >> 6
    return (r == c).astype(dtype)


def _head_sum(x, ones_bd):
    return jnp.dot(x.astype(BF16), ones_bd, preferred_element_type=F32)


def _head_rms(x, ones_bd):
    return x * lax.rsqrt(_head_sum(x * x, ones_bd) * (1.0 / HEAD_DIM) + EPS)


def _pack_rows(y):
    bits = lax.bitcast_convert_type(y.astype(BF16).astype(F32), jnp.uint32)
    half = y.shape[1] // 2
    return lax.bitcast_convert_type(bits[:, :half] | (bits[:, half:] >> 16), F32)


def _unpack_rows(w):
    bits = lax.bitcast_convert_type(w, jnp.uint32)
    hi = lax.bitcast_convert_type(bits & jnp.uint32(0xFFFF0000), F32)
    lo = lax.bitcast_convert_type(bits << 16, F32)
    return hi, lo


def _pack_planes(y, ref, lead=()):
    for p in range(PLANES):
        ref[(p,) + lead] = _pack_rows(y[:, 2 * p * SC_ROW:(2 * p + 2) * SC_ROW])


def _mod_kernel(c_ref, w_ref, b_ref, o_ref):
    a = jax.nn.silu(c_ref[...])
    o_ref[...] = jnp.dot(a, w_ref[0], preferred_element_type=F32,
                         precision=lax.Precision.HIGHEST) + b_ref[...]


def _mod(cc, w_mod, b_mod, layer):
    n = w_mod.shape[2]
    tn = 1536
    return pl.pallas_call(
        _mod_kernel,
        out_shape=jax.ShapeDtypeStruct((MOD_ROWS, n), F32),
        grid=(n // tn,),
        in_specs=[pl.BlockSpec((MOD_ROWS, D_MODEL), lambda j: (0, 0)),
                  pl.BlockSpec((1, D_MODEL, tn), lambda j: (layer, 0, j)),
                  pl.BlockSpec((1, tn), lambda j: (0, j))],
        out_specs=pl.BlockSpec((MOD_ROWS, tn), lambda j: (0, j)),
        compiler_params=_cparams("arbitrary"),
        name="mod",
    )(cc, w_mod, b_mod.reshape(1, n))


def _rope(xn, c_ref, sp_ref, sm_ref):
    w = xn.shape[-1]
    return (xn * c_ref[...] + pltpu.roll(xn, 16, 1) * sp_ref[...]
            + pltpu.roll(xn, w - 16, 1) * sm_ref[...])


def _gmlp(z, ws_ref, bias_ref):
    gz = jax.nn.gelu(z)
    u = gz[:, :A_WIDTH]
    vn = _head_rms(gz[:, A_WIDTH:], _head_ones(A_WIDTH, BF16))
    lane_head = lax.broadcasted_iota(jnp.int32, vn.shape, 1) >> 6
    acc = bias_ref[...]
    for hh in range(A_HEADS):
        vh = jnp.where(lane_head == hh, vn, 0.0).astype(BF16)
        acc = acc + jnp.dot(ws_ref[hh], vh, preferred_element_type=F32)
    return u * acc


def _inproj_kernel(x_ref, mul_ref, add_ref, w_ref, qc_ref, qsp_ref, qsm_ref, kc_ref, ksp_ref, ksm_ref,
                   ws_ref, bias_ref, ya_ref, q_ref, k_ref, v_ref, zc_ref, g_ref):
    x = x_ref[0]
    ms = jnp.mean(x * x, axis=-1, keepdims=True)
    h = x * lax.rsqrt(ms + EPS) * mul_ref[0] + add_ref[0]
    y = jnp.dot(h.astype(BF16), w_ref[...], preferred_element_type=F32)
    for c0 in range(0, x.shape[0], A_CHUNK):
        ya_ref[0, c0:c0 + A_CHUNK, :] = _gmlp(y[c0:c0 + A_CHUNK, :OFF_B], ws_ref, bias_ref).astype(BF16)
    qn = _head_rms(y[:, OFF_B:OFF_KV], _head_ones(B_WIDTH, BF16))
    q_ref[0] = _rope(qn, qc_ref, qsp_ref, qsm_ref).astype(BF16)
    kn = _head_rms(y[:, OFF_KV:OFF_V], _head_ones(B_KV_WIDTH, BF16))
    k_ref[0] = _rope(kn, kc_ref, ksp_ref, ksm_ref).astype(BF16)
    v_ref[0] = y[:, OFF_V:OFF_C].astype(BF16)
    zc_ref[0] = y[:, OFF_C:OFF_G]
    g_ref[0] = y[:, OFF_G:]


def _inproj(x, mul, add, w_bf, q_tabs, k_tabs, ws_bf, bias2d):
    b, t, d = x.shape
    tm = min(256, t)
    row = lambda i, bb: (bb, i, 0)
    vec = lambda i, bb: (bb, 0, 0)
    tab = lambda i, bb: (i, 0)
    widths = (A_WIDTH, B_WIDTH, B_KV_WIDTH, B_KV_WIDTH, 4 * C_WIDTH, C_WIDTH)
    dtypes = (BF16, BF16, BF16, BF16, F32, F32)
    return pl.pallas_call(
        _inproj_kernel,
        out_shape=[jax.ShapeDtypeStruct((b, t, w), dt) for w, dt in zip(widths, dtypes)],
        grid=(t // tm, b),
        in_specs=[pl.BlockSpec((1, tm, d), row),
                  pl.BlockSpec((1, 1, d), vec),
                  pl.BlockSpec((1, 1, d), vec),
                  pl.BlockSpec((d, IN_WIDTH), lambda i, bb: (0, 0))]
                 + [pl.BlockSpec((tm, B_WIDTH), tab)] * 3
                 + [pl.BlockSpec((tm, B_KV_WIDTH), tab)] * 3
                 + [pl.BlockSpec((A_HEADS, A_CHUNK, A_CHUNK), lambda i, bb: (0, 0, 0)),
                    pl.BlockSpec((A_CHUNK, A_WIDTH), lambda i, bb: (0, 0))],
        out_specs=[pl.BlockSpec((1, tm, w), row) for w in widths],
        compiler_params=_cparams("arbitrary", "arbitrary"),
        name="inproj",
    )(x, mul, add, w_bf, *q_tabs, *k_tabs, ws_bf, bias2d)


def _rope_tables(t, w, scale, width, rotate):
    ws = w.astype(F32) * scale
    if not rotate:
        c = jnp.broadcast_to(jnp.tile(ws, width // HEAD_DIM)[None, :], (t, width))
        z = jnp.zeros((t, width), F32)
        return c, z, z
    pos = jnp.arange(t)
    row = (pos // GRID_W).astype(F32)
    col = (pos % GRID_W).astype(F32)
    inv_freq = 1.0 / (ROPE_BASE ** (jnp.arange(0, HEAD_DIM // 2, 2, dtype=F32) / (HEAD_DIM // 2)))
    dd = jnp.arange(HEAD_DIM)
    axis = dd // 32
    half = (dd % 32) // 16
    ang = jnp.where(axis[None, :] == 0, row[:, None], col[:, None]) * inv_freq[dd % 16][None, :]
    cos, sin = jnp.cos(ang), jnp.sin(ang)
    c = cos * ws[None, :]
    sm = jnp.where(half[None, :] == 0, -sin * jnp.roll(ws, -16)[None, :], 0.0)
    sp = jnp.where(half[None, :] == 1, sin * jnp.roll(ws, 16)[None, :], 0.0)
    rep = width // HEAD_DIM
    return jnp.tile(c, (1, rep)), jnp.tile(sp, (1, rep)), jnp.tile(sm, (1, rep))


def _attn_kernel(flag_ref, q_ref, *refs, n_seg):
    kv_refs, o_ref = refs[:2 * n_seg], refs[2 * n_seg]
    tq = q_ref.shape[1]
    dh = HEAD_DIM

    def heads(j):
        q4 = jnp.concatenate([q_ref[0, :, (B_GROUP * j + gg) * dh:(B_GROUP * j + gg + 1) * dh]
                              for gg in range(B_GROUP)], axis=0)
        ks, vs = [], []
        for sg in range(n_seg):
            s_len = kv_refs[2 * sg].shape[1]
            for c0 in range(0, s_len, ATTN_KEYS):
                c1 = min(c0 + ATTN_KEYS, s_len)
                ks.append(kv_refs[2 * sg][0, c0:c1, j * dh:(j + 1) * dh])
                vs.append(kv_refs[2 * sg + 1][0, c0:c1, j * dh:(j + 1) * dh])
        return q4, ks, vs

    def scores(q4, ks):
        return [lax.dot_general(q4, kk, (((1,), (1,)), ((), ())), preferred_element_type=F32) for kk in ks]

    def finish(j, ps, vs):
        cols = [p[:, c:c + 128] for p in ps for c in range(0, p.shape[1], 128)]
        l = jnp.sum(functools.reduce(jnp.add, cols), axis=-1, keepdims=True)
        acc = sum(jnp.dot(p.astype(BF16), vv, preferred_element_type=F32) for p, vv in zip(ps, vs))
        o = acc / l
        for gg in range(B_GROUP):
            hh = B_GROUP * j + gg
            o_ref[0, :, hh * dh:(hh + 1) * dh] = o[gg * tq:(gg + 1) * tq].astype(BF16)

    @pl.when(flag_ref[0] > 0)
    def _():
        shift = flag_ref[1].astype(F32)
        for j in range(B_KV_HEADS):
            q4, ks, vs = heads(j)
            finish(j, [jnp.exp2(s - shift) for s in scores(q4, ks)], vs)

    @pl.when(flag_ref[0] <= 0)
    def _():
        for j in range(B_KV_HEADS):
            q4, ks, vs = heads(j)
            ss = scores(q4, ks)
            m = functools.reduce(jnp.maximum, [jnp.max(s, axis=-1, keepdims=True) for s in ss])
            finish(j, [jnp.exp2(s - m) for s in ss], vs)


def _attn(flag, q, kv_segs):
    b, t, w = q.shape
    tq = min(ATTN_ROWS, t)
    n_seg = len(kv_segs)
    kv_flat, kv_specs = [], []
    for kk, vv in kv_segs:
        s_len, kw = kk.shape[1:]
        kv_flat += [kk, vv]
        kv_specs += [pl.BlockSpec((1, s_len, kw), lambda bb, i, fl: (bb, 0, 0))] * 2
    return pl.pallas_call(
        functools.partial(_attn_kernel, n_seg=n_seg),
        out_shape=jax.ShapeDtypeStruct((b, t, w), BF16),
        grid_spec=pltpu.PrefetchScalarGridSpec(
            num_scalar_prefetch=1,
            grid=(b, t // tq),
            in_specs=[pl.BlockSpec((1, tq, w), lambda bb, i, fl: (bb, i, 0))] + kv_specs,
            out_specs=pl.BlockSpec((1, tq, w), lambda bb, i, fl: (bb, i, 0))),
        compiler_params=_cparams("arbitrary", "arbitrary"),
        name="attn",
    )(flag, q, *kv_flat)


def _scan_rows(x, reverse):
    n = x.shape[0]
    rows = lax.broadcasted_iota(jnp.int32, x.shape, 0)
    sh = 1
    while sh < n:
        if reverse:
            x = x + jnp.where(rows < n - sh, pltpu.roll(x, n - sh, 0), 0.0)
        else:
            x = x + jnp.where(rows >= sh, pltpu.roll(x, sh, 0), 0.0)
        sh *= 2
    return x


def _stack_heads(x, lane_head):
    return jnp.concatenate([jnp.where(lane_head == hh, x, 0.0) for hh in range(C_HEADS)], axis=0)


def _hgrn_kernel(z_ref, zc_ref, lbc_ref, *refs, ctx_out):
    if ctx_out:
        o_ref, oc_ref, st_ref, kx_ref, bx_ref, vx_ref, flag_ref = refs
    else:
        o_ref, st_ref, kx_ref, bx_ref, vx_ref, flag_ref = refs
        oc_ref = None
    n = C_WIDTH
    nb = HGRN_BLOCK
    nblk_c = zc_ref.shape[1] // nb
    nblk_l = z_ref.shape[1] // nb
    ones_bd = _head_ones(n, BF16)
    rows = lax.broadcasted_iota(jnp.int32, (nb, n), 0)
    lane_head = lax.broadcasted_iota(jnp.int32, (nb, n), 1) >> 6
    lane_head64 = lax.broadcasted_iota(jnp.int32, (HEAD_DIM, n), 1) >> 6
    low_half = (lax.broadcasted_iota(jnp.int32, (HEAD_DIM, 2 * HEAD_DIM), 1) < HEAD_DIM)
    sc_t = lax.broadcasted_iota(jnp.int32, (nb, C_HEADS * nb), 0)
    sc_s = lax.broadcasted_iota(jnp.int32, (nb, C_HEADS * nb), 1) & (nb - 1)

    def group_blocks(nblk, i):
        grp = HGRN_GROUP if nblk % HGRN_GROUP == 0 else 1
        fwd = [i * grp + gg for gg in range(grp)]
        return grp, fwd, [nblk - 1 - blk for blk in fwd]

    def flag_groups(src_ref, nblk, base):
        def body(i, carry):
            _, fwd, bwd = group_blocks(nblk, i)
            worst = None
            for d, blks in ((0, fwd), (1, bwd)):
                for blk in blks:
                    zz = src_ref[0, pl.ds(pl.multiple_of(blk * nb, nb), nb), (1 + d) * n:(2 + d) * n]
                    step_bound = jnp.minimum(lbc_ref[d, 4:5, :],
                                             jnp.maximum(-zz, 0.0) + (jnp.log(2.0) - lbc_ref[d, 1:2, :]))
                    bound = jnp.sum(step_bound, axis=0, keepdims=True)
                    worst = bound if worst is None else jnp.maximum(worst, bound)
            flag_ref[base + i] = (jnp.max(worst) <= SAFE_DECAY).astype(jnp.int32)
            return carry
        lax.fori_loop(0, nblk // group_blocks(nblk, 0)[0], body, 0)

    n_grp_c = nblk_c // group_blocks(nblk_c, 0)[0]
    flag_groups(zc_ref, nblk_c, 0)
    flag_groups(z_ref, nblk_l, n_grp_c)
    st_ref[...] = jnp.zeros_like(st_ref)
    o_ref[...] = jnp.zeros_like(o_ref)
    if ctx_out:
        oc_ref[...] = jnp.zeros_like(oc_ref)

    def step(src_ref, dst_ref, blk, d, fast):
        reverse = d == 1
        r0 = pl.multiple_of(blk * nb, nb)
        z = src_ref[0, pl.ds(r0, nb), (1 + d) * n:(2 + d) * n]
        v = src_ref[0, pl.ds(r0, nb), 3 * n:4 * n]
        one_m_lb = lbc_ref[d, 0:1, :]
        log1m_lb = lbc_ref[d, 1:2, :]
        log_lb = lbc_ref[d, 2:3, :]
        lb_pos = lbc_ref[d, 3:4, :] > 0.5
        soft = jnp.log1p(jnp.exp(-jnp.abs(z)))
        log_rest = log1m_lb + (jnp.minimum(z, 0.0) - soft)
        lse = jnp.maximum(log_lb, log_rest) + jnp.log1p(jnp.exp(-jnp.abs(log_lb - log_rest)))
        log_f = jnp.where(lb_pos, lse, log_rest)
        k = one_m_lb * jnp.exp(jnp.minimum(-z, 0.0) - soft)
        bc = _scan_rows(log_f, reverse)
        edge = 0 if reverse else nb - 1
        b_edge = bc[edge:edge + 1, :]
        st = st_ref[d]
        v_bf = v.astype(BF16)

        if dst_ref is not None:
            q = jax.nn.silu(src_ref[0, pl.ds(r0, nb), 0:n])
            qt = (q * jnp.exp(bc)).astype(BF16)
            o = lax.dot_general(qt, _stack_heads(st, lane_head64).astype(BF16), (((1,), (1,)), ((), ())),
                                preferred_element_type=F32)

            def intra_fast():
                kt = _stack_heads(k * jnp.exp(-bc), lane_head).astype(BF16)
                sc = lax.dot_general(qt, kt, (((1,), (1,)), ((), ())), preferred_element_type=F32)
                keep = (sc_s >= sc_t) if reverse else (sc_s <= sc_t)
                sc = jnp.where(keep, sc, 0.0).astype(BF16)
                return jnp.dot(sc, _stack_heads(v, lane_head).astype(BF16), preferred_element_type=F32)

            def intra_exact():
                kx_ref[d] = k
                bx_ref[d] = bc
                vx_ref[d] = v

                def sbody(s, acc):
                    keep = (rows <= s) if reverse else (rows >= s)
                    e = jnp.exp(jnp.where(keep, bc - bx_ref[d, pl.ds(s, 1), :], 0.0))
                    p = jnp.where(keep, q * e * kx_ref[d, pl.ds(s, 1), :], 0.0)
                    sc = jnp.dot(p.astype(BF16), ones_bd, preferred_element_type=F32)
                    return acc + sc * vx_ref[d, pl.ds(s, 1), :]

                return lax.fori_loop(0, nb, sbody, jnp.zeros((nb, n), F32))

            o = o + (intra_fast() if fast else intra_exact())
            dst_ref[0, pl.ds(r0, nb), :] += o

        kd = (k * jnp.exp(b_edge - bc)).astype(BF16)
        full = lax.dot_general(v_bf, kd, (((0,), (0,)), ((), ())), preferred_element_type=F32)
        upd = jnp.concatenate(
            [jnp.where(low_half,
                       full[(2 * c) * HEAD_DIM:(2 * c + 1) * HEAD_DIM, 2 * c * HEAD_DIM:(2 * c + 2) * HEAD_DIM],
                       full[(2 * c + 1) * HEAD_DIM:(2 * c + 2) * HEAD_DIM, 2 * c * HEAD_DIM:(2 * c + 2) * HEAD_DIM])
             for c in range(C_HEADS // 2)], axis=1)
        st_ref[d] = st * jnp.exp(b_edge) + upd

    def run(src_ref, dst_ref, nblk, base):
        def body(i, carry):
            _, fwd, bwd = group_blocks(nblk, i)
            safe = flag_ref[base + i]

            def group(fast):
                for bf, bb in zip(fwd, bwd):
                    step(src_ref, dst_ref, bf, 0, fast)
                    step(src_ref, dst_ref, bb, 1, fast)

            pl.when(safe > 0)(functools.partial(group, True))
            pl.when(safe <= 0)(functools.partial(group, False))
            return carry
        lax.fori_loop(0, nblk // group_blocks(nblk, 0)[0], body, 0)

    run(zc_ref, oc_ref, nblk_c, 0)
    run(z_ref, o_ref, nblk_l, n_grp_c)


def _hgrn(zc, zc_c, lbc, ctx_out):
    b, t, w = zc.shape
    lc = zc_c.shape[1]
    n = C_WIDTH
    row = lambda bb: (bb, 0, 0)
    out_shape = [jax.ShapeDtypeStruct((b, t, n), F32)]
    out_specs = [pl.BlockSpec((1, t, n), row)]
    if ctx_out:
        out_shape.append(jax.ShapeDtypeStruct((b, lc, n), F32))
        out_specs.append(pl.BlockSpec((1, lc, n), row))
    res = pl.pallas_call(
        functools.partial(_hgrn_kernel, ctx_out=ctx_out),
        out_shape=out_shape,
        grid=(b,),
        in_specs=[pl.BlockSpec((1, t, w), row),
                  pl.BlockSpec((1, lc, w), row),
                  pl.BlockSpec((2, 8, n), lambda bb: (0, 0, 0))],
        out_specs=out_specs,
        scratch_shapes=[pltpu.VMEM((2, HEAD_DIM, n), F32)]
                       + [pltpu.VMEM((2, HGRN_BLOCK, n), F32)] * 3
                       + [pltpu.SMEM(((t + lc) // HGRN_BLOCK,), jnp.int32)],
        compiler_params=_cparams("arbitrary"),
        name="hgrn",
    )(zc, zc_c, lbc)
    return (res[0], res[1]) if ctx_out else (res[0], None)


def _outproj_kernel(x_ref, ya_ref, yb_ref, o_ref, g_ref, w_ref, gate_ref, mul_ref, add_ref, hw_ref,
                    wr_ref, br_ref, xo_ref, h2_ref, lg_ref):
    yc = _head_rms(o_ref[0], _head_ones(C_WIDTH, BF16)) * hw_ref[...] * jax.nn.silu(g_ref[0])
    y = jnp.dot(ya_ref[0], w_ref[0:A_WIDTH, :], preferred_element_type=F32)
    y = y + jnp.dot(yb_ref[0], w_ref[A_WIDTH:A_WIDTH + B_WIDTH, :], preferred_element_type=F32)
    y = y + jnp.dot(yc.astype(BF16), w_ref[A_WIDTH + B_WIDTH:, :], preferred_element_type=F32)
    xn = x_ref[0] + gate_ref[0] * y
    xo_ref[0] = xn
    ms = jnp.mean(xn * xn, axis=-1, keepdims=True)
    h2 = xn * lax.rsqrt(ms + EPS) * mul_ref[0] + add_ref[0]
    _pack_planes(h2, h2_ref, (0,))
    h_hi = h2.astype(BF16)
    h_lo = (h2 - h_hi.astype(F32)).astype(BF16)
    both = jnp.dot(h_hi, wr_ref[...], preferred_element_type=F32)
    lg_ref[0] = (both[:, :ROUTER_LANES] + both[:, ROUTER_LANES:] + br_ref[...]
                 + jnp.dot(h_lo, wr_ref[:, 0:ROUTER_LANES], preferred_element_type=F32))


def _outproj(x, ya, yb, o, g, w_bf, gate, mul, add, hw, wr, br):
    b, t, d = x.shape
    tm = min(256, t)
    row = lambda bb, i: (bb, i, 0)
    vec = lambda bb, i: (bb, 0, 0)
    const = lambda bb, i: (0, 0)
    return pl.pallas_call(
        _outproj_kernel,
        out_shape=[jax.ShapeDtypeStruct((b, t, d), F32),
                   jax.ShapeDtypeStruct((PLANES, b, t, SC_ROW), F32),
                   jax.ShapeDtypeStruct((b, t, ROUTER_LANES), F32)],
        grid=(b, t // tm),
        in_specs=[pl.BlockSpec((1, tm, d), row),
                  pl.BlockSpec((1, tm, A_WIDTH), row),
                  pl.BlockSpec((1, tm, B_WIDTH), row),
                  pl.BlockSpec((1, tm, C_WIDTH), row),
                  pl.BlockSpec((1, tm, C_WIDTH), row),
                  pl.BlockSpec((d, d), const),
                  pl.BlockSpec((1, 1, d), vec),
                  pl.BlockSpec((1, 1, d), vec),
                  pl.BlockSpec((1, 1, d), vec),
                  pl.BlockSpec((1, C_WIDTH), const),
                  pl.BlockSpec((d, 2 * ROUTER_LANES), const),
                  pl.BlockSpec((1, ROUTER_LANES), const)],
        out_specs=[pl.BlockSpec((1, tm, d), row),
                   pl.BlockSpec((PLANES, 1, tm, SC_ROW), lambda bb, i: (0, bb, i, 0)),
                   pl.BlockSpec((1, tm, ROUTER_LANES), row)],
        compiler_params=_cparams("arbitrary", "arbitrary"),
        name="outproj",
    )(x, ya, yb, o, g, w_bf, gate, mul, add, hw, wr, br)


def _route_kernel(lg_ref, info_ref, meta_ref, cnt_ref, base_ref):
    phase = pl.program_id(0)
    i = pl.program_id(1)
    tm = lg_ref.shape[0]
    lane = lax.broadcasted_iota(jnp.int32, (tm, ROUTER_LANES), 1)
    lane_f = lane.astype(F32)
    lg = lg_ref[...]
    neg = -jnp.inf
    gl = jnp.where(lane < N_GROUPS, lg, neg)
    gmax = jnp.max(gl, axis=1, keepdims=True)
    grp = jnp.min(jnp.where(gl == gmax, lane_f, float(ROUTER_LANES)), axis=1, keepdims=True).astype(jnp.int32)
    p_grp = 1.0 / jnp.sum(jnp.exp(gl - gmax), axis=1, keepdims=True)
    in_grp = (lane >= N_GROUPS) & (lane < N_GROUPS + N_EXPERTS) & (((lane - N_GROUPS) >> 3) == grp)
    el = jnp.where(in_grp, lg, neg)
    v1 = jnp.max(el, axis=1, keepdims=True)
    i1 = jnp.min(jnp.where(el == v1, lane_f, float(ROUTER_LANES)), axis=1, keepdims=True)
    el2 = jnp.where(lane_f == i1, neg, el)
    v2 = jnp.max(el2, axis=1, keepdims=True)
    i2 = jnp.min(jnp.where(el2 == v2, lane_f, float(ROUTER_LANES)), axis=1, keepdims=True)
    hit1 = lane_f == i1
    hit2 = lane_f == i2
    onehot = (hit1 | hit2).astype(F32)
    tile_counts = jnp.sum(onehot, axis=0, keepdims=True)

    @pl.when(phase == 0)
    def _():
        @pl.when(i == 0)
        def _():
            cnt_ref[...] = jnp.zeros_like(cnt_ref)
        cnt_ref[...] += tile_counts
        info_ref[...] = jnp.zeros_like(info_ref)
        meta_ref[...] = jnp.zeros_like(meta_ref)

    @pl.when(phase == 1)
    def _():
        @pl.when(i == 0)
        def _():
            counts = jnp.broadcast_to(cnt_ref[...], (8, ROUTER_LANES))
            padded = jnp.floor((counts + (MOE_ROWS - 1.0)) * (1.0 / MOE_ROWS)) * MOE_ROWS
            r = lax.broadcasted_iota(jnp.int32, (ROUTER_LANES, ROUTER_LANES), 0)
            c = lax.broadcasted_iota(jnp.int32, (ROUTER_LANES, ROUTER_LANES), 1)
            ends = jnp.dot(padded, (r <= c).astype(F32), preferred_element_type=F32,
                           precision=lax.Precision.HIGHEST)
            base_ref[...] = (ends - padded)[0:1]
            row = lax.broadcasted_iota(jnp.int32, (8, ROUTER_LANES), 0)
            meta_ref[...] = jnp.where(row == 0, counts, jnp.where(row == 1, ends - padded, ends))

        tr = lax.broadcasted_iota(jnp.int32, (tm, tm), 0)
        tc = lax.broadcasted_iota(jnp.int32, (tm, tm), 1)
        before = jnp.dot((tc < tr).astype(BF16), onehot.astype(BF16), preferred_element_type=F32)
        pos = base_ref[...] + before
        d1 = jnp.sum(jnp.where(hit1, pos, 0.0), axis=1, keepdims=True)
        d2 = jnp.sum(jnp.where(hit2, pos, 0.0), axis=1, keepdims=True)
        base_ref[...] += tile_counts
        rr = jnp.exp(v2 - v1)
        g1 = p_grp / (1.0 + rr)
        g2 = p_grp * rr / (1.0 + rr)
        info_ref[...] = jnp.where(lane == 0, d1, jnp.where(lane == 1, d2, jnp.where(
            lane == 2, g1, jnp.where(lane == 3, g2, 0.0))))


def _route(logits):
    n = logits.shape[0]
    tm = ROUTE_ROWS if n % ROUTE_ROWS == 0 else ROUTE_ROWS // 2
    return pl.pallas_call(
        _route_kernel,
        out_shape=[jax.ShapeDtypeStruct((n, ROUTER_LANES), F32),
                   jax.ShapeDtypeStruct((8, ROUTER_LANES), F32)],
        grid=(2, n // tm),
        in_specs=[pl.BlockSpec((tm, ROUTER_LANES), lambda p, i: (i, 0))],
        out_specs=[pl.BlockSpec((tm, ROUTER_LANES), lambda p, i: (p * i, 0)),
                   pl.BlockSpec((8, ROUTER_LANES), lambda p, i: (0, 0))],
        scratch_shapes=[pltpu.VMEM((1, ROUTER_LANES), F32), pltpu.VMEM((1, ROUTER_LANES), F32)],
        compiler_params=_cparams("arbitrary", "arbitrary"),
        name="route",
    )(logits)


def _sc_mesh():
    return plsc.VectorSubcoreMesh(core_axis_name="c", subcore_axis_name="s")


def _sc_gather(table, idx):
    n = idx.shape[0]
    d = table.shape[1]

    @functools.partial(pl.kernel, out_type=jax.ShapeDtypeStruct((n, d), table.dtype), mesh=_sc_mesh())
    def gather(x_hbm, i_hbm, o_hbm):
        def body(i_vmem, o_vmem):
            pltpu.sync_copy(x_hbm.at[i_vmem.at[0]], o_vmem)

        pltpu.emit_pipeline(
            body,
            grid=(n // SC_WINDOW,),
            in_specs=[pl.BlockSpec((1, SC_WINDOW), lambda i: (0, i))],
            out_specs=[pl.BlockSpec((SC_WINDOW, d), lambda i: (i, 0))],
            core_axis_name=("c", "s"),
            dimension_semantics=(pltpu.PARALLEL,),
        )(i_hbm, o_hbm)

    return gather(table, idx.reshape(1, n))


def _sc_scatter2(rows, idx0, idx1, n_out):
    m, d = rows.shape

    @functools.partial(pl.kernel, out_type=jax.ShapeDtypeStruct((n_out, d), rows.dtype), mesh=_sc_mesh())
    def scatter(x_hbm, i0_hbm, i1_hbm, o_hbm):
        def body(x_vmem, i0_vmem, i1_vmem):
            pltpu.sync_copy(x_vmem, o_hbm.at[i0_vmem.at[0]])
            pltpu.sync_copy(x_vmem, o_hbm.at[i1_vmem.at[0]])

        pltpu.emit_pipeline(
            body,
            grid=(m // SC_WINDOW,),
            in_specs=[pl.BlockSpec((SC_WINDOW, d), lambda i: (i, 0)),
                      pl.BlockSpec((1, SC_WINDOW), lambda i: (0, i)),
                      pl.BlockSpec((1, SC_WINDOW), lambda i: (0, i))],
            out_specs=[],
            core_axis_name=("c", "s"),
            dimension_semantics=(pltpu.PARALLEL,),
        )(x_hbm, i0_hbm, i1_hbm)

    return scatter(rows, idx0.reshape(1, m), idx1.reshape(1, m))


def _moe_kernel(be_ref, nu_ref, x_ref, w1_ref, w3_ref, w2_ref, o_ref, w1b, w3b, w2b):
    i = pl.program_id(0)
    e = be_ref[i]
    prev = be_ref[jnp.maximum(i - 1, 0)]

    @pl.when((i == 0) | (e != prev))
    def _():
        w1b[...] = w1_ref[0, 0].astype(BF16)
        w3b[...] = w3_ref[0, 0].astype(BF16)
        w2b[...] = w2_ref[0, 0].astype(BF16)

    @pl.when(i < nu_ref[0])
    def _():
        parts = [h.astype(BF16) for p in range(PLANES) for h in _unpack_rows(x_ref[p])]
        a = sum(jnp.dot(h, w1b[q * SC_ROW:(q + 1) * SC_ROW, :], preferred_element_type=F32)
                for q, h in enumerate(parts))
        b = sum(jnp.dot(h, w3b[q * SC_ROW:(q + 1) * SC_ROW, :], preferred_element_type=F32)
                for q, h in enumerate(parts))
        hmid = (jax.nn.silu(a) * b).astype(BF16)
        _pack_planes(jnp.dot(hmid, w2b[...], preferred_element_type=F32), o_ref)

    @pl.when(i >= nu_ref[0])
    def _():
        o_ref[...] = jnp.zeros_like(o_ref)


def _moe_mlp(blk_expert, n_used, xs, w1, w3, w2, layer):
    n_rows = xs.shape[1]
    d, f = w1.shape[2:]
    nblk = n_rows // MOE_ROWS
    rows = lambda i, be, nu: (0, i, 0)
    return pl.pallas_call(
        _moe_kernel,
        out_shape=jax.ShapeDtypeStruct((PLANES, n_rows, SC_ROW), F32),
        grid_spec=pltpu.PrefetchScalarGridSpec(
            num_scalar_prefetch=2,
            grid=(nblk,),
            in_specs=[pl.BlockSpec((PLANES, MOE_ROWS, SC_ROW), rows),
                      pl.BlockSpec((1, 1, d, f), lambda i, be, nu: (layer, be[i], 0, 0)),
                      pl.BlockSpec((1, 1, d, f), lambda i, be, nu: (layer, be[i], 0, 0)),
                      pl.BlockSpec((1, 1, f, d), lambda i, be, nu: (layer, be[i], 0, 0))],
            out_specs=pl.BlockSpec((PLANES, MOE_ROWS, SC_ROW), rows),
            scratch_shapes=[pltpu.VMEM((d, f), BF16), pltpu.VMEM((d, f), BF16), pltpu.VMEM((f, d), BF16)]),
        compiler_params=_cparams("arbitrary"),
        name="moe",
    )(blk_expert, n_used, xs, w1, w3, w2)


def _combine_kernel(x_ref, pk_ref, info_ref, g_ref, o_ref):
    info = info_ref[...]
    g1 = info[:, 2:3]
    g2 = info[:, 3:4]
    parts = []
    for p in range(PLANES):
        hi1, lo1 = _unpack_rows(pk_ref[TOP_K * p])
        hi2, lo2 = _unpack_rows(pk_ref[TOP_K * p + 1])
        parts += [g1 * hi1 + g2 * hi2, g1 * lo1 + g2 * lo2]
    o_ref[0] = x_ref[0] + g_ref[0] * jnp.concatenate(parts, axis=1)


def _combine(x, picked, info, gate, row0):
    b, t, d = x.shape
    tm = min(256, t)
    off = row0 // tm
    tok = lambda bb, i: (off + bb * (t // tm) + i, 0)
    tok3 = lambda bb, i: (0, off + bb * (t // tm) + i, 0)
    return pl.pallas_call(
        _combine_kernel,
        out_shape=jax.ShapeDtypeStruct((b, t, d), F32),
        grid=(b, t // tm),
        in_specs=[pl.BlockSpec((1, tm, d), lambda bb, i: (bb, i, 0)),
                  pl.BlockSpec((PLANES * TOP_K, tm, SC_ROW), tok3),
                  pl.BlockSpec((tm, ROUTER_LANES), tok),
                  pl.BlockSpec((1, 1, d), lambda bb, i: (bb, 0, 0))],
        out_specs=pl.BlockSpec((1, tm, d), lambda bb, i: (bb, i, 0)),
        compiler_params=_cparams("arbitrary", "arbitrary"),
        name="combine",
    )(x, picked, info, gate)


def _hier_moe(h2p, logits, w1, w3, w2, layer):
    n_tok = h2p.shape[1]
    info, meta = _route(logits)
    dest = info[:, 0:TOP_K].astype(jnp.int32)
    pad_ends = meta[2, N_GROUPS:N_GROUPS + N_EXPERTS].astype(jnp.int32)
    nblk = -(-(n_tok * TOP_K) // MOE_ROWS) + N_EXPERTS
    n_rows = nblk * MOE_ROWS
    blk_start = jnp.arange(nblk, dtype=jnp.int32) * MOE_ROWS
    blk_expert = jnp.minimum(jnp.sum((pad_ends[None, :] <= blk_start[:, None]).astype(jnp.int32), axis=1),
                             N_EXPERTS - 1)
    n_used = pad_ends[-1:] // MOE_ROWS
    slot = [jnp.concatenate([p * n_rows + dest[:, s] for p in range(PLANES)]) for s in range(TOP_K)]
    xs = _sc_scatter2(h2p.reshape(PLANES * n_tok, SC_ROW), slot[0], slot[1], PLANES * n_rows)
    out = _moe_mlp(blk_expert, n_used, xs.reshape(PLANES, n_rows, SC_ROW), w1, w3, w2, layer)
    idx_all = jnp.concatenate([p * n_rows + dest[:, s] for p in range(PLANES) for s in range(TOP_K)])
    picked = _sc_gather(out.reshape(PLANES * n_rows, SC_ROW), idx_all)
    return picked.reshape(PLANES * TOP_K, n_tok, SC_ROW), info


def _layer(layer, x, xc, c, c_ctx, lb, w_mod, b_mod, norm1_w, w_in, w_s, b_s, q_norm_w, k_norm_w, hgrn_norm_w, w_out,
           norm2_w, w_grp, b_grp, w_exp, b_exp, w1, w3, w2, ctx_out):
    b, t, d = x.shape
    lc = xc.shape[1]
    cc = jnp.zeros((MOD_ROWS, d), F32).at[:b].set(c).at[b].set(c_ctx)
    mod = _mod(cc, w_mod, b_mod, layer)
    sh1, sc1, g1, sh2, sc2, g2 = [m[:, None, :] for m in jnp.split(mod[:b], 6, axis=-1)]
    mod_c = [jnp.broadcast_to(m[None, None, :], (b, 1, d)) for m in jnp.split(mod[b], 6)]

    w_in_bf = w_in.astype(BF16)
    scale = LOG2E * HEAD_DIM ** -0.5
    q_tabs = _rope_tables(t, q_norm_w, scale, B_WIDTH, True)
    k_tabs = _rope_tables(t, k_norm_w, 1.0, B_KV_WIDTH, True)
    qc_tabs = _rope_tables(lc, q_norm_w, scale, B_WIDTH, False)
    kc_tabs = _rope_tables(lc, k_norm_w, 1.0, B_KV_WIDTH, False)
    ws_bf = w_s.astype(BF16)
    bias2d = jnp.repeat(b_s.T, HEAD_DIM, axis=1)
    ya, q, k, v, zc, g = _inproj(x, norm1_w * (1.0 + sc1), sh1, w_in_bf, q_tabs, k_tabs, ws_bf, bias2d)
    ya_c, q_c, k_c, v_c, zc_c, g_c = _inproj(xc, norm1_w * (1.0 + mod_c[1]), mod_c[0], w_in_bf,
                                             qc_tabs, kc_tabs, ws_bf, bias2d)

    bound = LOG2E * HEAD_DIM ** 0.5 * jnp.max(jnp.abs(q_norm_w)) * jnp.max(jnp.abs(k_norm_w)) * 1.02
    shift = jnp.ceil(bound)
    attn_flag = jnp.stack([(shift <= SAFE_SHIFT).astype(jnp.int32), shift.astype(jnp.int32)])
    yb = _attn(attn_flag, q, [(k, v), (k_c, v_c)])

    pos = lb > 0.0
    log_lb = jnp.log(jnp.where(pos, lb, 1.0))
    lbc = jnp.stack([1.0 - lb, jnp.log1p(-lb), log_lb, pos.astype(F32), jnp.where(pos, -log_lb, 1e30)], axis=1)
    lbc = jnp.concatenate([lbc, jnp.zeros((2, 3, C_WIDTH), F32)], axis=1)
    o, o_c = _hgrn(zc, zc_c, lbc, ctx_out)

    w_out_bf = w_out.astype(BF16)
    hw = jnp.tile(hgrn_norm_w, C_HEADS)[None, :]
    wr = jnp.zeros((d, ROUTER_LANES), F32).at[:, :N_GROUPS].set(w_grp).at[
        :, N_GROUPS:N_GROUPS + N_EXPERTS].set(w_exp)
    wr_hi = wr.astype(BF16)
    wr = jnp.concatenate([wr_hi, (wr - wr_hi.astype(F32)).astype(BF16)], axis=1)
    br = jnp.zeros((1, ROUTER_LANES), F32).at[0, :N_GROUPS].set(b_grp).at[
        0, N_GROUPS:N_GROUPS + N_EXPERTS].set(b_exp)
    x, h2, lg = _outproj(x, ya, yb, o, g, w_out_bf, g1, norm2_w * (1.0 + sc2), sh2, hw, wr, br)
    if ctx_out:
        yb_c = _attn(attn_flag, q_c, [(k_c, v_c)])
        xc, h2c, lgc = _outproj(xc, ya_c, yb_c, o_c, g_c, w_out_bf, mod_c[2],
                                norm2_w * (1.0 + mod_c[4]), mod_c[3], hw, wr, br)
        tokens = jnp.concatenate([h2.reshape(PLANES, -1, SC_ROW), h2c.reshape(PLANES, -1, SC_ROW)], axis=1)
        logits = jnp.concatenate([lg.reshape(-1, ROUTER_LANES), lgc.reshape(-1, ROUTER_LANES)], axis=0)
        picked, info = _hier_moe(tokens, logits, w1, w3, w2, layer)
        x = _combine(x, picked, info, g2, 0)
        xc = _combine(xc, picked, info, mod_c[5], b * t)
    else:
        picked, info = _hier_moe(h2.reshape(PLANES, -1, SC_ROW), lg.reshape(-1, ROUTER_LANES), w1, w3, w2, layer)
        x = _combine(x, picked, info, g2, 0)
    return x, xc


def kernel(x, c, ctx, c_ctx, w_mod, b_mod, norm1_w, w_in, w_s, b_s, q_norm_w, k_norm_w, hgrn_lb_logits,
           hgrn_norm_w, w_out, norm2_w, w_grp, b_grp, w_exp, b_exp, w1, w3, w2):
    depth = w_mod.shape[0]
    lb_sm = jax.nn.softmax(hgrn_lb_logits.astype(F32), axis=0)
    lb = jnp.cumsum(lb_sm, axis=0) - lb_sm[0]
    xc = ctx
    for l in range(depth):
        x, xc = _layer(l, x, xc, c, c_ctx, lb[l], w_mod, b_mod[l], norm1_w[l], w_in[l], w_s[l], b_s[l],
                       q_norm_w[l], k_norm_w[l], hgrn_norm_w[l], w_out[l], norm2_w[l], w_grp[l], b_grp[l],
                       w_exp[l], b_exp[l], w1, w3, w2, ctx_out=(l < depth - 1))
    return x
```

```python
import functools

import jax
import jax.numpy as jnp
from jax import lax
from jax.experimental import pallas as pl
from jax.experimental.pallas import tpu as pltpu
from jax.experimental.pallas import tpu_sc as plsc

F32 = jnp.float32
BF16 = jnp.bfloat16

D_MODEL = 1024
HEAD_DIM = 64
GRID_W = 64
EPS = 1e-6
ROPE_BASE = 10000.0
A_WIDTH = D_MODEL // 4
A_HEADS = A_WIDTH // HEAD_DIM
A_CHUNK = 128
B_WIDTH = D_MODEL // 2
B_HEADS = B_WIDTH // HEAD_DIM
B_KV_HEADS = 2
B_GROUP = B_HEADS // B_KV_HEADS
B_KV_WIDTH = B_KV_HEADS * HEAD_DIM
C_WIDTH = D_MODEL // 4
C_HEADS = C_WIDTH // HEAD_DIM
OFF_B = 2 * A_WIDTH
OFF_KV = OFF_B + B_WIDTH
OFF_V = OFF_KV + B_KV_WIDTH
OFF_C = OFF_KV + 2 * B_KV_WIDTH
OFF_G = OFF_C + 4 * C_WIDTH
IN_WIDTH = OFF_G + C_WIDTH
N_GROUPS = 4
EXPERTS_PER_GROUP = 8
N_EXPERTS = N_GROUPS * EXPERTS_PER_GROUP
TOP_K = 2
D_FF_EXPERT = D_MODEL // 2

MOD_ROWS = 16
ROUTER_LANES = 128
PROJ_ROWS = 512
HGRN_BLOCK = 32
HGRN_GROUP = 4
LOG2E = 1.4426950408889634
ATTN_ROWS = 256
ATTN_KEYS = 512
SAFE_SHIFT = 60
SAFE_DECAY = 80.0
MOE_ROWS = 512
ROUTE_ROWS = 512
SC_WINDOW = 128
SC_ROW = 256
PLANES = D_MODEL // (2 * SC_ROW)
VMEM_LIMIT = 48 * 1024 * 1024


def _cparams(*sem):
    return pltpu.CompilerParams(dimension_semantics=sem, vmem_limit_bytes=VMEM_LIMIT)


def _head_ones(n, dtype):
    r = lax.broadcasted_iota(jnp.int32, (n, n), 0) >> 6
    c = lax.broadcasted_iota(jnp.int32, (n, n), 1) >> 6
    return (r == c).astype(dtype)


def _head_sum(x, ones_bd):
    return jnp.dot(x.astype(BF16), ones_bd, preferred_element_type=F32)


def _head_rms(x, ones_bd):
    return x * lax.rsqrt(_head_sum(x * x, ones_bd) * (1.0 / HEAD_DIM) + EPS)


def _pack_rows(y):
    bits = lax.bitcast_convert_type(y.astype(BF16).astype(F32), jnp.uint32)
    half = y.shape[1] // 2
    return lax.bitcast_convert_type(bits[:, :half] | (bits[:, half:] >> 16), F32)


def _unpack_rows(w):
    bits = lax.bitcast_convert_type(w, jnp.uint32)
    hi = lax.bitcast_convert_type(bits & jnp.uint32(0xFFFF0000), F32)
    lo = lax.bitcast_convert_type(bits << 16, F32)
    return hi, lo


def _pack_planes(y, ref, lead=()):
    for p in range(PLANES):
        ref[(p,) + lead] = _pack_rows(y[:, 2 * p * SC_ROW:(2 * p + 2) * SC_ROW])


def _mod_kernel(c_ref, w_ref, b_ref, o_ref):
    a = jax.nn.silu(c_ref[...])
    o_ref[...] = jnp.dot(a, w_ref[0], preferred_element_type=F32,
                         precision=lax.Precision.HIGHEST) + b_ref[...]


def _mod(cc, w_mod, b_mod, layer):
    n = w_mod.shape[2]
    tn = 1536
    return pl.pallas_call(
        _mod_kernel,
        out_shape=jax.ShapeDtypeStruct((MOD_ROWS, n), F32),
        grid=(n // tn,),
        in_specs=[pl.BlockSpec((MOD_ROWS, D_MODEL), lambda j: (0, 0)),
                  pl.BlockSpec((1, D_MODEL, tn), lambda j: (layer, 0, j)),
                  pl.BlockSpec((1, tn), lambda j: (0, j))],
        out_specs=pl.BlockSpec((MOD_ROWS, tn), lambda j: (0, j)),
        compiler_params=_cparams("arbitrary"),
        name="mod",
    )(cc, w_mod, b_mod.reshape(1, n))


def _rope(xn, c_ref, sp_ref, sm_ref):
    w = xn.shape[-1]
    return (xn * c_ref[...] + pltpu.roll(xn, 16, 1) * sp_ref[...]
            + pltpu.roll(xn, w - 16, 1) * sm_ref[...])


def _gmlp(z, ws_ref, bias_ref):
    gz = jax.nn.gelu(z)
    u = gz[:, :A_WIDTH]
    vn = _head_rms(gz[:, A_WIDTH:], _head_ones(A_WIDTH, BF16))
    lane_head = lax.broadcasted_iota(jnp.int32, vn.shape, 1) >> 6
    acc = bias_ref[...]
    for hh in range(A_HEADS):
        vh = jnp.where(lane_head == hh, vn, 0.0).astype(BF16)
        acc = acc + jnp.dot(ws_ref[hh], vh, preferred_element_type=F32)
    return u * acc


def _inproj_kernel(x_ref, mul_ref, add_ref, w_ref, qc_ref, qsp_ref, qsm_ref, kc_ref, ksp_ref, ksm_ref,
                   ws_ref, bias_ref, ya_ref, q_ref, k_ref, v_ref, zc_ref, g_ref):
    x = x_ref[0]
    ms = jnp.mean(x * x, axis=-1, keepdims=True)
    h = x * lax.rsqrt(ms + EPS) * mul_ref[0] + add_ref[0]
    y = jnp.dot(h.astype(BF16), w_ref[...], preferred_element_type=F32)
    for c0 in range(0, x.shape[0], A_CHUNK):
        ya_ref[0, c0:c0 + A_CHUNK, :] = _gmlp(y[c0:c0 + A_CHUNK, :OFF_B], ws_ref, bias_ref).astype(BF16)
    qn = _head_rms(y[:, OFF_B:OFF_KV], _head_ones(B_WIDTH, BF16))
    q_ref[0] = _rope(qn, qc_ref, qsp_ref, qsm_ref).astype(BF16)
    kn = _head_rms(y[:, OFF_KV:OFF_V], _head_ones(B_KV_WIDTH, BF16))
    k_ref[0] = _rope(kn, kc_ref, ksp_ref, ksm_ref).astype(BF16)
    v_ref[0] = y[:, OFF_V:OFF_C].astype(BF16)
    zc_ref[0] = y[:, OFF_C:OFF_G]
    g_ref[0] = y[:, OFF_G:]


def _inproj(x, mul, add, w_bf, q_tabs, k_tabs, ws_bf, bias2d):
    b, t, d = x.shape
    tm = min(PROJ_ROWS, t)
    row = lambda i, bb: (bb, i, 0)
    vec = lambda i, bb: (bb, 0, 0)
    tab = lambda i, bb: (i, 0)
    widths = (A_WIDTH, B_WIDTH, B_KV_WIDTH, B_KV_WIDTH, 4 * C_WIDTH, C_WIDTH)
    dtypes = (BF16, BF16, BF16, BF16, F32, F32)
    return pl.pallas_call(
        _inproj_kernel,
        out_shape=[jax.ShapeDtypeStruct((b, t, w), dt) for w, dt in zip(widths, dtypes)],
        grid=(t // tm, b),
        in_specs=[pl.BlockSpec((1, tm, d), row),
                  pl.BlockSpec((1, 1, d), vec),
                  pl.BlockSpec((1, 1, d), vec),
                  pl.BlockSpec((d, IN_WIDTH), lambda i, bb: (0, 0))]
                 + [pl.BlockSpec((tm, B_WIDTH), tab)] * 3
                 + [pl.BlockSpec((tm, B_KV_WIDTH), tab)] * 3
                 + [pl.BlockSpec((A_HEADS, A_CHUNK, A_CHUNK), lambda i, bb: (0, 0, 0)),
                    pl.BlockSpec((A_CHUNK, A_WIDTH), lambda i, bb: (0, 0))],
        out_specs=[pl.BlockSpec((1, tm, w), row) for w in widths],
        compiler_params=_cparams("arbitrary", "arbitrary"),
        name="inproj",
    )(x, mul, add, w_bf, *q_tabs, *k_tabs, ws_bf, bias2d)


def _rope_tables(t, w, scale, width, rotate):
    ws = w.astype(F32) * scale
    if not rotate:
        c = jnp.broadcast_to(jnp.tile(ws, width // HEAD_DIM)[None, :], (t, width))
        z = jnp.zeros((t, width), F32)
        return c, z, z
    pos = jnp.arange(t)
    row = (pos // GRID_W).astype(F32)
    col = (pos % GRID_W).astype(F32)
    inv_freq = 1.0 / (ROPE_BASE ** (jnp.arange(0, HEAD_DIM // 2, 2, dtype=F32) / (HEAD_DIM // 2)))
    dd = jnp.arange(HEAD_DIM)
    axis = dd // 32
    half = (dd % 32) // 16
    ang = jnp.where(axis[None, :] == 0, row[:, None], col[:, None]) * inv_freq[dd % 16][None, :]
    cos, sin = jnp.cos(ang), jnp.sin(ang)
    c = cos * ws[None, :]
    sm = jnp.where(half[None, :] == 0, -sin * jnp.roll(ws, -16)[None, :], 0.0)
    sp = jnp.where(half[None, :] == 1, sin * jnp.roll(ws, 16)[None, :], 0.0)
    rep = width // HEAD_DIM
    return jnp.tile(c, (1, rep)), jnp.tile(sp, (1, rep)), jnp.tile(sm, (1, rep))


def _attn_kernel(flag_ref, q_ref, *refs, n_seg):
    kv_refs, o_ref = refs[:2 * n_seg], refs[2 * n_seg]
    tq = q_ref.shape[1]
    dh = HEAD_DIM

    def heads(j):
        q4 = jnp.concatenate([q_ref[0, :, (B_GROUP * j + gg) * dh:(B_GROUP * j + gg + 1) * dh]
                              for gg in range(B_GROUP)], axis=0)
        ks, vs = [], []
        for sg in range(n_seg):
            s_len = kv_refs[2 * sg].shape[1]
            for c0 in range(0, s_len, ATTN_KEYS):
                c1 = min(c0 + ATTN_KEYS, s_len)
                ks.append(kv_refs[2 * sg][0, c0:c1, j * dh:(j + 1) * dh])
                vs.append(kv_refs[2 * sg + 1][0, c0:c1, j * dh:(j + 1) * dh])
        return q4, ks, vs

    def scores(q4, ks):
        return [lax.dot_general(q4, kk, (((1,), (1,)), ((), ())), preferred_element_type=F32) for kk in ks]

    def finish(j, ps, vs):
        cols = [p[:, c:c + 128] for p in ps for c in range(0, p.shape[1], 128)]
        l = jnp.sum(functools.reduce(jnp.add, cols), axis=-1, keepdims=True)
        acc = sum(jnp.dot(p.astype(BF16), vv, preferred_element_type=F32) for p, vv in zip(ps, vs))
        o = acc / l
        for gg in range(B_GROUP):
            hh = B_GROUP * j + gg
            o_ref[0, :, hh * dh:(hh + 1) * dh] = o[gg * tq:(gg + 1) * tq].astype(BF16)

    @pl.when(flag_ref[0] > 0)
    def _():
        shift = flag_ref[1].astype(F32)
        for j in range(B_KV_HEADS):
            q4, ks, vs = heads(j)
            finish(j, [jnp.exp2(s - shift) for s in scores(q4, ks)], vs)

    @pl.when(flag_ref[0] <= 0)
    def _():
        for j in range(B_KV_HEADS):
            q4, ks, vs = heads(j)
            ss = scores(q4, ks)
            m = functools.reduce(jnp.maximum, [jnp.max(s, axis=-1, keepdims=True) for s in ss])
            finish(j, [jnp.exp2(s - m) for s in ss], vs)


def _attn(flag, q, kv_segs):
    b, t, w = q.shape
    tq = min(ATTN_ROWS, t)
    n_seg = len(kv_segs)
    kv_flat, kv_specs = [], []
    for kk, vv in kv_segs:
        s_len, kw = kk.shape[1:]
        kv_flat += [kk, vv]
        kv_specs += [pl.BlockSpec((1, s_len, kw), lambda bb, i, fl: (bb, 0, 0))] * 2
    return pl.pallas_call(
        functools.partial(_attn_kernel, n_seg=n_seg),
        out_shape=jax.ShapeDtypeStruct((b, t, w), BF16),
        grid_spec=pltpu.PrefetchScalarGridSpec(
            num_scalar_prefetch=1,
            grid=(b, t // tq),
            in_specs=[pl.BlockSpec((1, tq, w), lambda bb, i, fl: (bb, i, 0))] + kv_specs,
            out_specs=pl.BlockSpec((1, tq, w), lambda bb, i, fl: (bb, i, 0))),
        compiler_params=_cparams("arbitrary", "arbitrary"),
        name="attn",
    )(flag, q, *kv_flat)


def _scan_rows(x, reverse):
    n = x.shape[0]
    rows = lax.broadcasted_iota(jnp.int32, x.shape, 0)
    sh = 1
    while sh < n:
        if reverse:
            x = x + jnp.where(rows < n - sh, pltpu.roll(x, n - sh, 0), 0.0)
        else:
            x = x + jnp.where(rows >= sh, pltpu.roll(x, sh, 0), 0.0)
        sh *= 2
    return x


def _stack_heads(x, lane_head):
    return jnp.concatenate([jnp.where(lane_head == hh, x, 0.0) for hh in range(C_HEADS)], axis=0)


def _hgrn_kernel(z_ref, zc_ref, lbc_ref, *refs, ctx_out):
    if ctx_out:
        o_ref, oc_ref, st_ref, kx_ref, bx_ref, vx_ref, flag_ref = refs
    else:
        o_ref, st_ref, kx_ref, bx_ref, vx_ref, flag_ref = refs
        oc_ref = None
    n = C_WIDTH
    nb = HGRN_BLOCK
    nblk_c = zc_ref.shape[1] // nb
    nblk_l = z_ref.shape[1] // nb
    ones_bd = _head_ones(n, BF16)
    rows = lax.broadcasted_iota(jnp.int32, (nb, n), 0)
    lane_head = lax.broadcasted_iota(jnp.int32, (nb, n), 1) >> 6
    lane_head64 = lax.broadcasted_iota(jnp.int32, (HEAD_DIM, n), 1) >> 6
    low_half = (lax.broadcasted_iota(jnp.int32, (HEAD_DIM, 2 * HEAD_DIM), 1) < HEAD_DIM)
    sc_t = lax.broadcasted_iota(jnp.int32, (nb, C_HEADS * nb), 0)
    sc_s = lax.broadcasted_iota(jnp.int32, (nb, C_HEADS * nb), 1) & (nb - 1)

    def group_blocks(nblk, i):
        grp = HGRN_GROUP if nblk % HGRN_GROUP == 0 else 1
        fwd = [i * grp + gg for gg in range(grp)]
        return grp, fwd, [nblk - 1 - blk for blk in fwd]

    def flag_groups(src_ref, nblk, base):
        def body(i, carry):
            _, fwd, bwd = group_blocks(nblk, i)
            worst = None
            for d, blks in ((0, fwd), (1, bwd)):
                for blk in blks:
                    zz = src_ref[0, pl.ds(pl.multiple_of(blk * nb, nb), nb), (1 + d) * n:(2 + d) * n]
                    step_bound = jnp.minimum(lbc_ref[d, 4:5, :],
                                             jnp.maximum(-zz, 0.0) + (jnp.log(2.0) - lbc_ref[d, 1:2, :]))
                    bound = jnp.sum(step_bound, axis=0, keepdims=True)
                    worst = bound if worst is None else jnp.maximum(worst, bound)
            flag_ref[base + i] = (jnp.max(worst) <= SAFE_DECAY).astype(jnp.int32)
            return carry
        lax.fori_loop(0, nblk // group_blocks(nblk, 0)[0], body, 0)

    n_grp_c = nblk_c // group_blocks(nblk_c, 0)[0]
    flag_groups(zc_ref, nblk_c, 0)
    flag_groups(z_ref, nblk_l, n_grp_c)
    st_ref[...] = jnp.zeros_like(st_ref)
    o_ref[...] = jnp.zeros_like(o_ref)
    if ctx_out:
        oc_ref[...] = jnp.zeros_like(oc_ref)

    def step(src_ref, dst_ref, blk, d, fast):
        reverse = d == 1
        r0 = pl.multiple_of(blk * nb, nb)
        z = src_ref[0, pl.ds(r0, nb), (1 + d) * n:(2 + d) * n]
        v = src_ref[0, pl.ds(r0, nb), 3 * n:4 * n]
        one_m_lb = lbc_ref[d, 0:1, :]
        log1m_lb = lbc_ref[d, 1:2, :]
        log_lb = lbc_ref[d, 2:3, :]
        lb_pos = lbc_ref[d, 3:4, :] > 0.5
        soft = jnp.log1p(jnp.exp(-jnp.abs(z)))
        log_rest = log1m_lb + (jnp.minimum(z, 0.0) - soft)
        lse = jnp.maximum(log_lb, log_rest) + jnp.log1p(jnp.exp(-jnp.abs(log_lb - log_rest)))
        log_f = jnp.where(lb_pos, lse, log_rest)
        k = one_m_lb * jnp.exp(jnp.minimum(-z, 0.0) - soft)
        bc = _scan_rows(log_f, reverse)
        edge = 0 if reverse else nb - 1
        b_edge = bc[edge:edge + 1, :]
        st = st_ref[d]
        v_bf = v.astype(BF16)

        if dst_ref is not None:
            q = jax.nn.silu(src_ref[0, pl.ds(r0, nb), 0:n])
            qt = (q * jnp.exp(bc)).astype(BF16)
            o = lax.dot_general(qt, _stack_heads(st, lane_head64).astype(BF16), (((1,), (1,)), ((), ())),
                                preferred_element_type=F32)

            def intra_fast():
                kt = _stack_heads(k * jnp.exp(-bc), lane_head).astype(BF16)
                sc = lax.dot_general(qt, kt, (((1,), (1,)), ((), ())), preferred_element_type=F32)
                keep = (sc_s >= sc_t) if reverse else (sc_s <= sc_t)
                sc = jnp.where(keep, sc, 0.0).astype(BF16)
                return jnp.dot(sc, _stack_heads(v, lane_head).astype(BF16), preferred_element_type=F32)

            def intra_exact():
                kx_ref[d] = k
                bx_ref[d] = bc
                vx_ref[d] = v

                def sbody(s, acc):
                    keep = (rows <= s) if reverse else (rows >= s)
                    e = jnp.exp(jnp.where(keep, bc - bx_ref[d, pl.ds(s, 1), :], 0.0))
                    p = jnp.where(keep, q * e * kx_ref[d, pl.ds(s, 1), :], 0.0)
                    sc = jnp.dot(p.astype(BF16), ones_bd, preferred_element_type=F32)
                    return acc + sc * vx_ref[d, pl.ds(s, 1), :]

                return lax.fori_loop(0, nb, sbody, jnp.zeros((nb, n), F32))

            o = o + (intra_fast() if fast else intra_exact())
            dst_ref[0, pl.ds(r0, nb), :] += o

        kd = (k * jnp.exp(b_edge - bc)).astype(BF16)
        full = lax.dot_general(v_bf, kd, (((0,), (0,)), ((), ())), preferred_element_type=F32)
        upd = jnp.concatenate(
            [jnp.where(low_half,
                       full[(2 * c) * HEAD_DIM:(2 * c + 1) * HEAD_DIM, 2 * c * HEAD_DIM:(2 * c + 2) * HEAD_DIM],
                       full[(2 * c + 1) * HEAD_DIM:(2 * c + 2) * HEAD_DIM, 2 * c * HEAD_DIM:(2 * c + 2) * HEAD_DIM])
             for c in range(C_HEADS // 2)], axis=1)
        st_ref[d] = st * jnp.exp(b_edge) + upd

    def run(src_ref, dst_ref, nblk, base):
        def body(i, carry):
            _, fwd, bwd = group_blocks(nblk, i)
            safe = flag_ref[base + i]

            def group(fast):
                for bf, bb in zip(fwd, bwd):
                    step(src_ref, dst_ref, bf, 0, fast)
                    step(src_ref, dst_ref, bb, 1, fast)

            pl.when(safe > 0)(functools.partial(group, True))
            pl.when(safe <= 0)(functools.partial(group, False))
            return carry
        lax.fori_loop(0, nblk // group_blocks(nblk, 0)[0], body, 0)

    run(zc_ref, oc_ref, nblk_c, 0)
    run(z_ref, o_ref, nblk_l, n_grp_c)


def _hgrn(zc, zc_c, lbc, ctx_out):
    b, t, w = zc.shape
    lc = zc_c.shape[1]
    n = C_WIDTH
    row = lambda bb: (bb, 0, 0)
    out_shape = [jax.ShapeDtypeStruct((b, t, n), F32)]
    out_specs = [pl.BlockSpec((1, t, n), row)]
    if ctx_out:
        out_shape.append(jax.ShapeDtypeStruct((b, lc, n), F32))
        out_specs.append(pl.BlockSpec((1, lc, n), row))
    res = pl.pallas_call(
        functools.partial(_hgrn_kernel, ctx_out=ctx_out),
        out_shape=out_shape,
        grid=(b,),
        in_specs=[pl.BlockSpec((1, t, w), row),
                  pl.BlockSpec((1, lc, w), row),
                  pl.BlockSpec((2, 8, n), lambda bb: (0, 0, 0))],
        out_specs=out_specs,
        scratch_shapes=[pltpu.VMEM((2, HEAD_DIM, n), F32)]
                       + [pltpu.VMEM((2, HGRN_BLOCK, n), F32)] * 3
                       + [pltpu.SMEM(((t + lc) // HGRN_BLOCK,), jnp.int32)],
        compiler_params=_cparams("arbitrary"),
        name="hgrn",
    )(zc, zc_c, lbc)
    return (res[0], res[1]) if ctx_out else (res[0], None)


def _select_experts(lg):
    lane = lax.broadcasted_iota(jnp.int32, lg.shape, 1)
    lane_f = lane.astype(F32)
    neg = -jnp.inf
    gl = jnp.where(lane < N_GROUPS, lg, neg)
    gmax = jnp.max(gl, axis=1, keepdims=True)
    grp = jnp.min(jnp.where(gl == gmax, lane_f, float(ROUTER_LANES)), axis=1, keepdims=True).astype(jnp.int32)
    p_grp = 1.0 / jnp.sum(jnp.exp(gl - gmax), axis=1, keepdims=True)
    in_grp = (lane >= N_GROUPS) & (lane < N_GROUPS + N_EXPERTS) & (((lane - N_GROUPS) >> 3) == grp)
    el = jnp.where(in_grp, lg, neg)
    v1 = jnp.max(el, axis=1, keepdims=True)
    i1 = jnp.min(jnp.where(el == v1, lane_f, float(ROUTER_LANES)), axis=1, keepdims=True)
    el2 = jnp.where(lane_f == i1, neg, el)
    v2 = jnp.max(el2, axis=1, keepdims=True)
    i2 = jnp.min(jnp.where(el2 == v2, lane_f, float(ROUTER_LANES)), axis=1, keepdims=True)
    rr = jnp.exp(v2 - v1)
    g1 = p_grp / (1.0 + rr)
    g2 = p_grp * rr / (1.0 + rr)
    sel = jnp.where(lane == 0, i1, jnp.where(lane == 1, i2, jnp.where(lane == 2, g1, jnp.where(lane == 3, g2, 0.0))))
    counts = jnp.sum(((lane_f == i1) | (lane_f == i2)).astype(F32), axis=0, keepdims=True)
    return sel, counts


def _outproj_kernel(x_ref, ya_ref, yb_ref, o_ref, g_ref, w_ref, gate_ref, mul_ref, add_ref, hw_ref,
                    wr_ref, br_ref, xo_ref, h2_ref, sel_ref, cnt_ref):
    yc = _head_rms(o_ref[0], _head_ones(C_WIDTH, BF16)) * hw_ref[...] * jax.nn.silu(g_ref[0])
    y = jnp.dot(ya_ref[0], w_ref[0:A_WIDTH, :], preferred_element_type=F32)
    y = y + jnp.dot(yb_ref[0], w_ref[A_WIDTH:A_WIDTH + B_WIDTH, :], preferred_element_type=F32)
    y = y + jnp.dot(yc.astype(BF16), w_ref[A_WIDTH + B_WIDTH:, :], preferred_element_type=F32)
    xn = x_ref[0] + gate_ref[0] * y
    xo_ref[0] = xn
    ms = jnp.mean(xn * xn, axis=-1, keepdims=True)
    h2 = xn * lax.rsqrt(ms + EPS) * mul_ref[0] + add_ref[0]
    _pack_planes(h2, h2_ref, (0,))
    h_hi = h2.astype(BF16)
    h_lo = (h2 - h_hi.astype(F32)).astype(BF16)
    both = jnp.dot(h_hi, wr_ref[...], preferred_element_type=F32)
    lg = (both[:, :ROUTER_LANES] + both[:, ROUTER_LANES:] + br_ref[...]
          + jnp.dot(h_lo, wr_ref[:, 0:ROUTER_LANES], preferred_element_type=F32))
    sel, counts = _select_experts(lg)
    sel_ref[0] = sel

    @pl.when((pl.program_id(0) == 0) & (pl.program_id(1) == 0))
    def _():
        cnt_ref[...] = jnp.zeros_like(cnt_ref)
    cnt_ref[...] += counts


def _outproj(x, ya, yb, o, g, w_bf, gate, mul, add, hw, wr, br):
    b, t, d = x.shape
    tm = min(PROJ_ROWS, t)
    row = lambda bb, i: (bb, i, 0)
    vec = lambda bb, i: (bb, 0, 0)
    const = lambda bb, i: (0, 0)
    return pl.pallas_call(
        _outproj_kernel,
        out_shape=[jax.ShapeDtypeStruct((b, t, d), F32),
                   jax.ShapeDtypeStruct((PLANES, b, t, SC_ROW), F32),
                   jax.ShapeDtypeStruct((b, t, ROUTER_LANES), F32),
                   jax.ShapeDtypeStruct((8, ROUTER_LANES), F32)],
        grid=(b, t // tm),
        in_specs=[pl.BlockSpec((1, tm, d), row),
                  pl.BlockSpec((1, tm, A_WIDTH), row),
                  pl.BlockSpec((1, tm, B_WIDTH), row),
                  pl.BlockSpec((1, tm, C_WIDTH), row),
                  pl.BlockSpec((1, tm, C_WIDTH), row),
                  pl.BlockSpec((d, d), const),
                  pl.BlockSpec((1, 1, d), vec),
                  pl.BlockSpec((1, 1, d), vec),
                  pl.BlockSpec((1, 1, d), vec),
                  pl.BlockSpec((1, C_WIDTH), const),
                  pl.BlockSpec((d, 2 * ROUTER_LANES), const),
                  pl.BlockSpec((1, ROUTER_LANES), const)],
        out_specs=[pl.BlockSpec((1, tm, d), row),
                   pl.BlockSpec((PLANES, 1, tm, SC_ROW), lambda bb, i: (0, bb, i, 0)),
                   pl.BlockSpec((1, tm, ROUTER_LANES), row),
                   pl.BlockSpec((8, ROUTER_LANES), const)],
        compiler_params=_cparams("arbitrary", "arbitrary"),
        name="outproj",
    )(x, ya, yb, o, g, w_bf, gate, mul, add, hw, wr, br)


def _route_kernel(sel_ref, cnt_ref, info_ref, meta_ref, base_ref):
    i = pl.program_id(0)
    tm = sel_ref.shape[0]
    lane = lax.broadcasted_iota(jnp.int32, (tm, ROUTER_LANES), 1)
    lane_f = lane.astype(F32)
    sel = sel_ref[...]
    hit1 = lane_f == sel[:, 0:1]
    hit2 = lane_f == sel[:, 1:2]
    onehot = (hit1 | hit2).astype(F32)

    @pl.when(i == 0)
    def _():
        counts = cnt_ref[...]
        padded = jnp.floor((counts + (MOE_ROWS - 1.0)) * (1.0 / MOE_ROWS)) * MOE_ROWS
        r = lax.broadcasted_iota(jnp.int32, (ROUTER_LANES, ROUTER_LANES), 0)
        c = lax.broadcasted_iota(jnp.int32, (ROUTER_LANES, ROUTER_LANES), 1)
        ends = jnp.dot(padded, (r <= c).astype(F32), preferred_element_type=F32,
                       precision=lax.Precision.HIGHEST)
        base_ref[...] = (ends - padded)[0:1]
        row = lax.broadcasted_iota(jnp.int32, (8, ROUTER_LANES), 0)
        meta_ref[...] = jnp.where(row == 0, counts, jnp.where(row == 1, ends - padded, ends))

    tr = lax.broadcasted_iota(jnp.int32, (tm, tm), 0)
    tc = lax.broadcasted_iota(jnp.int32, (tm, tm), 1)
    before = jnp.dot((tc < tr).astype(BF16), onehot.astype(BF16), preferred_element_type=F32)
    pos = base_ref[...] + before
    d1 = jnp.sum(jnp.where(hit1, pos, 0.0), axis=1, keepdims=True)
    d2 = jnp.sum(jnp.where(hit2, pos, 0.0), axis=1, keepdims=True)
    base_ref[...] += jnp.sum(onehot, axis=0, keepdims=True)
    info_ref[...] = jnp.where(lane == 0, d1, jnp.where(lane == 1, d2, sel))


def _route(sel, counts):
    n = sel.shape[0]
    tm = ROUTE_ROWS if n % ROUTE_ROWS == 0 else ROUTE_ROWS // 2
    return pl.pallas_call(
        _route_kernel,
        out_shape=[jax.ShapeDtypeStruct((n, ROUTER_LANES), F32),
                   jax.ShapeDtypeStruct((8, ROUTER_LANES), F32)],
        grid=(n // tm,),
        in_specs=[pl.BlockSpec((tm, ROUTER_LANES), lambda i: (i, 0)),
                  pl.BlockSpec((8, ROUTER_LANES), lambda i: (0, 0))],
        out_specs=[pl.BlockSpec((tm, ROUTER_LANES), lambda i: (i, 0)),
                   pl.BlockSpec((8, ROUTER_LANES), lambda i: (0, 0))],
        scratch_shapes=[pltpu.VMEM((1, ROUTER_LANES), F32)],
        compiler_params=_cparams("arbitrary"),
        name="route",
    )(sel, counts)


def _sc_mesh():
    return plsc.VectorSubcoreMesh(core_axis_name="c", subcore_axis_name="s")


def _sc_gather(table, idx):
    n = idx.shape[0]
    d = table.shape[1]

    @functools.partial(pl.kernel, out_type=jax.ShapeDtypeStruct((n, d), table.dtype), mesh=_sc_mesh())
    def gather(x_hbm, i_hbm, o_hbm):
        def body(i_vmem, o_vmem):
            pltpu.sync_copy(x_hbm.at[i_vmem.at[0]], o_vmem)

        pltpu.emit_pipeline(
            body,
            grid=(n // SC_WINDOW,),
            in_specs=[pl.BlockSpec((1, SC_WINDOW), lambda i: (0, i))],
            out_specs=[pl.BlockSpec((SC_WINDOW, d), lambda i: (i, 0))],
            core_axis_name=("c", "s"),
            dimension_semantics=(pltpu.PARALLEL,),
        )(i_hbm, o_hbm)

    return gather(table, idx.reshape(1, n))


def _sc_scatter2(rows, idx0, idx1, n_out):
    m, d = rows.shape

    @functools.partial(pl.kernel, out_type=jax.ShapeDtypeStruct((n_out, d), rows.dtype), mesh=_sc_mesh())
    def scatter(x_hbm, i0_hbm, i1_hbm, o_hbm):
        def body(x_vmem, i0_vmem, i1_vmem):
            pltpu.sync_copy(x_vmem, o_hbm.at[i0_vmem.at[0]])
            pltpu.sync_copy(x_vmem, o_hbm.at[i1_vmem.at[0]])

        pltpu.emit_pipeline(
            body,
            grid=(m // SC_WINDOW,),
            in_specs=[pl.BlockSpec((SC_WINDOW, d), lambda i: (i, 0)),
                      pl.BlockSpec((1, SC_WINDOW), lambda i: (0, i)),
                      pl.BlockSpec((1, SC_WINDOW), lambda i: (0, i))],
            out_specs=[],
            core_axis_name=("c", "s"),
            dimension_semantics=(pltpu.PARALLEL,),
        )(x_hbm, i0_hbm, i1_hbm)

    return scatter(rows, idx0.reshape(1, m), idx1.reshape(1, m))


def _moe_kernel(be_ref, nu_ref, x_ref, w1_ref, w3_ref, w2_ref, o_ref, w1b, w3b, w2b):
    i = pl.program_id(0)
    e = be_ref[i]
    prev = be_ref[jnp.maximum(i - 1, 0)]

    @pl.when((i == 0) | (e != prev))
    def _():
        w1b[...] = w1_ref[0, 0].astype(BF16)
        w3b[...] = w3_ref[0, 0].astype(BF16)
        w2b[...] = w2_ref[0, 0].astype(BF16)

    @pl.when(i < nu_ref[0])
    def _():
        parts = [h.astype(BF16) for p in range(PLANES) for h in _unpack_rows(x_ref[p])]
        a = sum(jnp.dot(h, w1b[q * SC_ROW:(q + 1) * SC_ROW, :], preferred_element_type=F32)
                for q, h in enumerate(parts))
        b = sum(jnp.dot(h, w3b[q * SC_ROW:(q + 1) * SC_ROW, :], preferred_element_type=F32)
                for q, h in enumerate(parts))
        hmid = (jax.nn.silu(a) * b).astype(BF16)
        _pack_planes(jnp.dot(hmid, w2b[...], preferred_element_type=F32), o_ref)

    @pl.when(i >= nu_ref[0])
    def _():
        o_ref[...] = jnp.zeros_like(o_ref)


def _moe_mlp(blk_expert, n_used, xs, w1, w3, w2, layer):
    n_rows = xs.shape[1]
    d, f = w1.shape[2:]
    nblk = n_rows // MOE_ROWS
    rows = lambda i, be, nu: (0, i, 0)
    return pl.pallas_call(
        _moe_kernel,
        out_shape=jax.ShapeDtypeStruct((PLANES, n_rows, SC_ROW), F32),
        grid_spec=pltpu.PrefetchScalarGridSpec(
            num_scalar_prefetch=2,
            grid=(nblk,),
            in_specs=[pl.BlockSpec((PLANES, MOE_ROWS, SC_ROW), rows),
                      pl.BlockSpec((1, 1, d, f), lambda i, be, nu: (layer, be[i], 0, 0)),
                      pl.BlockSpec((1, 1, d, f), lambda i, be, nu: (layer, be[i], 0, 0)),
                      pl.BlockSpec((1, 1, f, d), lambda i, be, nu: (layer, be[i], 0, 0))],
            out_specs=pl.BlockSpec((PLANES, MOE_ROWS, SC_ROW), rows),
            scratch_shapes=[pltpu.VMEM((d, f), BF16), pltpu.VMEM((d, f), BF16), pltpu.VMEM((f, d), BF16)]),
        compiler_params=_cparams("arbitrary"),
        name="moe",
    )(blk_expert, n_used, xs, w1, w3, w2)


def _combine_kernel(x_ref, pk_ref, info_ref, g_ref, o_ref):
    info = info_ref[...]
    g1 = info[:, 2:3]
    g2 = info[:, 3:4]
    parts = []
    for p in range(PLANES):
        hi1, lo1 = _unpack_rows(pk_ref[TOP_K * p])
        hi2, lo2 = _unpack_rows(pk_ref[TOP_K * p + 1])
        parts += [g1 * hi1 + g2 * hi2, g1 * lo1 + g2 * lo2]
    o_ref[0] = x_ref[0] + g_ref[0] * jnp.concatenate(parts, axis=1)


def _combine(x, picked, info, gate, row0):
    b, t, d = x.shape
    tm = min(256, t)
    off = row0 // tm
    tok = lambda bb, i: (off + bb * (t // tm) + i, 0)
    tok3 = lambda bb, i: (0, off + bb * (t // tm) + i, 0)
    return pl.pallas_call(
        _combine_kernel,
        out_shape=jax.ShapeDtypeStruct((b, t, d), F32),
        grid=(b, t // tm),
        in_specs=[pl.BlockSpec((1, tm, d), lambda bb, i: (bb, i, 0)),
                  pl.BlockSpec((PLANES * TOP_K, tm, SC_ROW), tok3),
                  pl.BlockSpec((tm, ROUTER_LANES), tok),
                  pl.BlockSpec((1, 1, d), lambda bb, i: (bb, 0, 0))],
        out_specs=pl.BlockSpec((1, tm, d), lambda bb, i: (bb, i, 0)),
        compiler_params=_cparams("arbitrary", "arbitrary"),
        name="combine",
    )(x, picked, info, gate)


def _hier_moe(h2p, sel, counts, w1, w3, w2, layer):
    n_tok = h2p.shape[1]
    info, meta = _route(sel, counts)
    dest = info[:, 0:TOP_K].astype(jnp.int32)
    pad_ends = meta[2, N_GROUPS:N_GROUPS + N_EXPERTS].astype(jnp.int32)
    nblk = -(-(n_tok * TOP_K) // MOE_ROWS) + N_EXPERTS
    n_rows = nblk * MOE_ROWS
    blk_start = jnp.arange(nblk, dtype=jnp.int32) * MOE_ROWS
    blk_expert = jnp.minimum(jnp.sum((pad_ends[None, :] <= blk_start[:, None]).astype(jnp.int32), axis=1),
                             N_EXPERTS - 1)
    n_used = pad_ends[-1:] // MOE_ROWS
    slot = [jnp.concatenate([p * n_rows + dest[:, s] for p in range(PLANES)]) for s in range(TOP_K)]
    xs = _sc_scatter2(h2p.reshape(PLANES * n_tok, SC_ROW), slot[0], slot[1], PLANES * n_rows)
    out = _moe_mlp(blk_expert, n_used, xs.reshape(PLANES, n_rows, SC_ROW), w1, w3, w2, layer)
    idx_all = jnp.concatenate([p * n_rows + dest[:, s] for p in range(PLANES) for s in range(TOP_K)])
    picked = _sc_gather(out.reshape(PLANES * n_rows, SC_ROW), idx_all)
    return picked.reshape(PLANES * TOP_K, n_tok, SC_ROW), info


def _layer(layer, x, xc, c, c_ctx, lb, w_mod, b_mod, norm1_w, w_in, w_s, b_s, q_norm_w, k_norm_w, hgrn_norm_w, w_out,
           norm2_w, w_grp, b_grp, w_exp, b_exp, w1, w3, w2, ctx_out):
    b, t, d = x.shape
    lc = xc.shape[1]
    cc = jnp.zeros((MOD_ROWS, d), F32).at[:b].set(c).at[b].set(c_ctx)
    mod = _mod(cc, w_mod, b_mod, layer)
    sh1, sc1, g1, sh2, sc2, g2 = [m[:, None, :] for m in jnp.split(mod[:b], 6, axis=-1)]
    mod_c = [jnp.broadcast_to(m[None, None, :], (b, 1, d)) for m in jnp.split(mod[b], 6)]

    w_in_bf = w_in.astype(BF16)
    scale = LOG2E * HEAD_DIM ** -0.5
    q_tabs = _rope_tables(t, q_norm_w, scale, B_WIDTH, True)
    k_tabs = _rope_tables(t, k_norm_w, 1.0, B_KV_WIDTH, True)
    qc_tabs = _rope_tables(lc, q_norm_w, scale, B_WIDTH, False)
    kc_tabs = _rope_tables(lc, k_norm_w, 1.0, B_KV_WIDTH, False)
    ws_bf = w_s.astype(BF16)
    bias2d = jnp.repeat(b_s.T, HEAD_DIM, axis=1)
    ya, q, k, v, zc, g = _inproj(x, norm1_w * (1.0 + sc1), sh1, w_in_bf, q_tabs, k_tabs, ws_bf, bias2d)
    ya_c, q_c, k_c, v_c, zc_c, g_c = _inproj(xc, norm1_w * (1.0 + mod_c[1]), mod_c[0], w_in_bf,
                                             qc_tabs, kc_tabs, ws_bf, bias2d)

    bound = LOG2E * HEAD_DIM ** 0.5 * jnp.max(jnp.abs(q_norm_w)) * jnp.max(jnp.abs(k_norm_w)) * 1.02
    shift = jnp.ceil(bound)
    attn_flag = jnp.stack([(shift <= SAFE_SHIFT).astype(jnp.int32), shift.astype(jnp.int32)])
    yb = _attn(attn_flag, q, [(k, v), (k_c, v_c)])

    pos = lb > 0.0
    log_lb = jnp.log(jnp.where(pos, lb, 1.0))
    lbc = jnp.stack([1.0 - lb, jnp.log1p(-lb), log_lb, pos.astype(F32), jnp.where(pos, -log_lb, 1e30)], axis=1)
    lbc = jnp.concatenate([lbc, jnp.zeros((2, 3, C_WIDTH), F32)], axis=1)
    o, o_c = _hgrn(zc, zc_c, lbc, ctx_out)

    w_out_bf = w_out.astype(BF16)
    hw = jnp.tile(hgrn_norm_w, C_HEADS)[None, :]
    wr = jnp.zeros((d, ROUTER_LANES), F32).at[:, :N_GROUPS].set(w_grp).at[
        :, N_GROUPS:N_GROUPS + N_EXPERTS].set(w_exp)
    wr_hi = wr.astype(BF16)
    wr = jnp.concatenate([wr_hi, (wr - wr_hi.astype(F32)).astype(BF16)], axis=1)
    br = jnp.zeros((1, ROUTER_LANES), F32).at[0, :N_GROUPS].set(b_grp).at[
        0, N_GROUPS:N_GROUPS + N_EXPERTS].set(b_exp)
    x, h2, sel, cnt = _outproj(x, ya, yb, o, g, w_out_bf, g1, norm2_w * (1.0 + sc2), sh2, hw, wr, br)
    if ctx_out:
        yb_c = _attn(attn_flag, q_c, [(k_c, v_c)])
        xc, h2c, sel_c, cnt_c = _outproj(xc, ya_c, yb_c, o_c, g_c, w_out_bf, mod_c[2],
                                norm2_w * (1.0 + mod_c[4]), mod_c[3], hw, wr, br)
        tokens = jnp.concatenate([h2.reshape(PLANES, -1, SC_ROW), h2c.reshape(PLANES, -1, SC_ROW)], axis=1)
        sel_all = jnp.concatenate([sel.reshape(-1, ROUTER_LANES), sel_c.reshape(-1, ROUTER_LANES)], axis=0)
        picked, info = _hier_moe(tokens, sel_all, cnt + cnt_c, w1, w3, w2, layer)
        x = _combine(x, picked, info, g2, 0)
        xc = _combine(xc, picked, info, mod_c[5], b * t)
    else:
        picked, info = _hier_moe(h2.reshape(PLANES, -1, SC_ROW), sel.reshape(-1, ROUTER_LANES), cnt, w1, w3, w2, layer)
        x = _combine(x, picked, info, g2, 0)
    return x, xc


def kernel(x, c, ctx, c_ctx, w_mod, b_mod, norm1_w, w_in, w_s, b_s, q_norm_w, k_norm_w, hgrn_lb_logits,
           hgrn_norm_w, w_out, norm2_w, w_grp, b_grp, w_exp, b_exp, w1, w3, w2):
    depth = w_mod.shape[0]
    lb_sm = jax.nn.softmax(hgrn_lb_logits.astype(F32), axis=0)
    lb = jnp.cumsum(lb_sm, axis=0) - lb_sm[0]
    xc = ctx
    for l in range(depth):
        x, xc = _layer(l, x, xc, c, c_ctx, lb[l], w_mod, b_mod[l], norm1_w[l], w_in[l], w_s[l], b_s[l],
                       q_norm_w[l], k_norm_w[l], hgrn_norm_w[l], w_out[l], norm2_w[l], w_grp[l], b_grp[l],
                       w_exp[l], b_exp[l], w1, w3, w2, ctx_out=(l < depth - 1))
    return x
```

```python
import functools

import jax
import jax.numpy as jnp
from jax import lax
from jax.experimental import pallas as pl
from jax.experimental.pallas import tpu as pltpu
from jax.experimental.pallas import tpu_sc as plsc

F32 = jnp.float32
BF16 = jnp.bfloat16

D_MODEL = 1024
HEAD_DIM = 64
GRID_W = 64
EPS = 1e-6
ROPE_BASE = 10000.0
A_WIDTH = D_MODEL // 4
A_HEADS = A_WIDTH // HEAD_DIM
A_CHUNK = 128
B_WIDTH = D_MODEL // 2
B_HEADS = B_WIDTH // HEAD_DIM
B_KV_HEADS = 2
B_GROUP = B_HEADS // B_KV_HEADS
B_KV_WIDTH = B_KV_HEADS * HEAD_DIM
C_WIDTH = D_MODEL // 4
C_HEADS = C_WIDTH // HEAD_DIM
OFF_B = 2 * A_WIDTH
OFF_KV = OFF_B + B_WIDTH
OFF_V = OFF_KV + B_KV_WIDTH
OFF_C = OFF_KV + 2 * B_KV_WIDTH
OFF_G = OFF_C + 4 * C_WIDTH
IN_WIDTH = OFF_G + C_WIDTH
N_GROUPS = 4
EXPERTS_PER_GROUP = 8
N_EXPERTS = N_GROUPS * EXPERTS_PER_GROUP
TOP_K = 2
D_FF_EXPERT = D_MODEL // 2

MOD_ROWS = 16
ROUTER_LANES = 128
PROJ_ROWS = 512
HGRN_BLOCK = 64
HGRN_GROUP = 4
LOG2E = 1.4426950408889634
ATTN_ROWS = 256
ATTN_KEYS = 512
SAFE_SHIFT = 60
SAFE_DECAY = 80.0
MOE_ROWS = 512
ROUTE_ROWS = 512
SC_WINDOW = 128
SC_ROW = 256
PLANES = D_MODEL // (2 * SC_ROW)
VMEM_LIMIT = 48 * 1024 * 1024


def _cparams(*sem):
    return pltpu.CompilerParams(dimension_semantics=sem, vmem_limit_bytes=VMEM_LIMIT)


def _head_ones(n, dtype):
    r = lax.broadcasted_iota(jnp.int32, (n, n), 0) >> 6
    c = lax.broadcasted_iota(jnp.int32, (n, n), 1) >> 6
    return (r == c).astype(dtype)


def _head_sum(x, ones_bd):
    return jnp.dot(x.astype(BF16), ones_bd, preferred_element_type=F32)


def _head_rms(x, ones_bd):
    return x * lax.rsqrt(_head_sum(x * x, ones_bd) * (1.0 / HEAD_DIM) + EPS)


def _pack_rows(y):
    bits = lax.bitcast_convert_type(y.astype(BF16).astype(F32), jnp.uint32)
    half = y.shape[1] // 2
    return lax.bitcast_convert_type(bits[:, :half] | (bits[:, half:] >> 16), F32)


def _unpack_rows(w):
    bits = lax.bitcast_convert_type(w, jnp.uint32)
    hi = lax.bitcast_convert_type(bits & jnp.uint32(0xFFFF0000), F32)
    lo = lax.bitcast_convert_type(bits << 16, F32)
    return hi, lo


def _pack_planes(y, ref, lead=()):
    for p in range(PLANES):
        ref[(p,) + lead] = _pack_rows(y[:, 2 * p * SC_ROW:(2 * p + 2) * SC_ROW])


def _mod_kernel(c_ref, w_ref, b_ref, o_ref):
    a = jax.nn.silu(c_ref[...])
    o_ref[...] = jnp.dot(a, w_ref[0], preferred_element_type=F32,
                         precision=lax.Precision.HIGHEST) + b_ref[...]


def _mod(cc, w_mod, b_mod, layer):
    n = w_mod.shape[2]
    tn = 1536
    return pl.pallas_call(
        _mod_kernel,
        out_shape=jax.ShapeDtypeStruct((MOD_ROWS, n), F32),
        grid=(n // tn,),
        in_specs=[pl.BlockSpec((MOD_ROWS, D_MODEL), lambda j: (0, 0)),
                  pl.BlockSpec((1, D_MODEL, tn), lambda j: (layer, 0, j)),
                  pl.BlockSpec((1, tn), lambda j: (0, j))],
        out_specs=pl.BlockSpec((MOD_ROWS, tn), lambda j: (0, j)),
        compiler_params=_cparams("arbitrary"),
        name="mod",
    )(cc, w_mod, b_mod.reshape(1, n))


def _rope(xn, c_ref, sp_ref, sm_ref):
    w = xn.shape[-1]
    return (xn * c_ref[...] + pltpu.roll(xn, 16, 1) * sp_ref[...]
            + pltpu.roll(xn, w - 16, 1) * sm_ref[...])


def _gmlp(z, ws_ref, bias_ref):
    gz = jax.nn.gelu(z)
    u = gz[:, :A_WIDTH]
    vn = _head_rms(gz[:, A_WIDTH:], _head_ones(A_WIDTH, BF16))
    lane_head = lax.broadcasted_iota(jnp.int32, vn.shape, 1) >> 6
    acc = bias_ref[...]
    for hh in range(A_HEADS):
        vh = jnp.where(lane_head == hh, vn, 0.0).astype(BF16)
        acc = acc + jnp.dot(ws_ref[hh], vh, preferred_element_type=F32)
    return u * acc


def _inproj_kernel(x_ref, mul_ref, add_ref, w_ref, qc_ref, qsp_ref, qsm_ref, kc_ref, ksp_ref, ksm_ref,
                   ws_ref, bias_ref, ya_ref, q_ref, k_ref, v_ref, zc_ref, g_ref):
    x = x_ref[0]
    ms = jnp.mean(x * x, axis=-1, keepdims=True)
    h = x * lax.rsqrt(ms + EPS) * mul_ref[0] + add_ref[0]
    y = jnp.dot(h.astype(BF16), w_ref[...], preferred_element_type=F32)
    for c0 in range(0, x.shape[0], A_CHUNK):
        ya_ref[0, c0:c0 + A_CHUNK, :] = _gmlp(y[c0:c0 + A_CHUNK, :OFF_B], ws_ref, bias_ref).astype(BF16)
    qn = _head_rms(y[:, OFF_B:OFF_KV], _head_ones(B_WIDTH, BF16))
    q_ref[0] = _rope(qn, qc_ref, qsp_ref, qsm_ref).astype(BF16)
    kn = _head_rms(y[:, OFF_KV:OFF_V], _head_ones(B_KV_WIDTH, BF16))
    k_ref[0] = _rope(kn, kc_ref, ksp_ref, ksm_ref).astype(BF16)
    v_ref[0] = y[:, OFF_V:OFF_C].astype(BF16)
    zc_ref[0] = y[:, OFF_C:OFF_G]
    g_ref[0] = y[:, OFF_G:]


def _inproj(x, mul, add, w_bf, q_tabs, k_tabs, ws_bf, bias2d):
    b, t, d = x.shape
    tm = min(PROJ_ROWS, t)
    row = lambda i, bb: (bb, i, 0)
    vec = lambda i, bb: (bb, 0, 0)
    tab = lambda i, bb: (i, 0)
    widths = (A_WIDTH, B_WIDTH, B_KV_WIDTH, B_KV_WIDTH, 4 * C_WIDTH, C_WIDTH)
    dtypes = (BF16, BF16, BF16, BF16, F32, F32)
    return pl.pallas_call(
        _inproj_kernel,
        out_shape=[jax.ShapeDtypeStruct((b, t, w), dt) for w, dt in zip(widths, dtypes)],
        grid=(t // tm, b),
        in_specs=[pl.BlockSpec((1, tm, d), row),
                  pl.BlockSpec((1, 1, d), vec),
                  pl.BlockSpec((1, 1, d), vec),
                  pl.BlockSpec((d, IN_WIDTH), lambda i, bb: (0, 0))]
                 + [pl.BlockSpec((tm, B_WIDTH), tab)] * 3
                 + [pl.BlockSpec((tm, B_KV_WIDTH), tab)] * 3
                 + [pl.BlockSpec((A_HEADS, A_CHUNK, A_CHUNK), lambda i, bb: (0, 0, 0)),
                    pl.BlockSpec((A_CHUNK, A_WIDTH), lambda i, bb: (0, 0))],
        out_specs=[pl.BlockSpec((1, tm, w), row) for w in widths],
        compiler_params=_cparams("arbitrary", "arbitrary"),
        name="inproj",
    )(x, mul, add, w_bf, *q_tabs, *k_tabs, ws_bf, bias2d)


def _rope_tables(t, w, scale, width, rotate):
    ws = w.astype(F32) * scale
    if not rotate:
        c = jnp.broadcast_to(jnp.tile(ws, width // HEAD_DIM)[None, :], (t, width))
        z = jnp.zeros((t, width), F32)
        return c, z, z
    pos = jnp.arange(t)
    row = (pos // GRID_W).astype(F32)
    col = (pos % GRID_W).astype(F32)
    inv_freq = 1.0 / (ROPE_BASE ** (jnp.arange(0, HEAD_DIM // 2, 2, dtype=F32) / (HEAD_DIM // 2)))
    dd = jnp.arange(HEAD_DIM)
    axis = dd // 32
    half = (dd % 32) // 16
    ang = jnp.where(axis[None, :] == 0, row[:, None], col[:, None]) * inv_freq[dd % 16][None, :]
    cos, sin = jnp.cos(ang), jnp.sin(ang)
    c = cos * ws[None, :]
    sm = jnp.where(half[None, :] == 0, -sin * jnp.roll(ws, -16)[None, :], 0.0)
    sp = jnp.where(half[None, :] == 1, sin * jnp.roll(ws, 16)[None, :], 0.0)
    rep = width // HEAD_DIM
    return jnp.tile(c, (1, rep)), jnp.tile(sp, (1, rep)), jnp.tile(sm, (1, rep))


def _attn_kernel(flag_ref, q_ref, *refs, n_seg):
    kv_refs, o_ref = refs[:2 * n_seg], refs[2 * n_seg]
    tq = q_ref.shape[1]
    dh = HEAD_DIM

    def heads(j):
        q4 = jnp.concatenate([q_ref[0, :, (B_GROUP * j + gg) * dh:(B_GROUP * j + gg + 1) * dh]
                              for gg in range(B_GROUP)], axis=0)
        ks, vs = [], []
        for sg in range(n_seg):
            s_len = kv_refs[2 * sg].shape[1]
            for c0 in range(0, s_len, ATTN_KEYS):
                c1 = min(c0 + ATTN_KEYS, s_len)
                ks.append(kv_refs[2 * sg][0, c0:c1, j * dh:(j + 1) * dh])
                vs.append(kv_refs[2 * sg + 1][0, c0:c1, j * dh:(j + 1) * dh])
        return q4, ks, vs

    def scores(q4, ks):
        return [lax.dot_general(q4, kk, (((1,), (1,)), ((), ())), preferred_element_type=F32) for kk in ks]

    def finish(j, ps, vs):
        cols = [p[:, c:c + 128] for p in ps for c in range(0, p.shape[1], 128)]
        l = jnp.sum(functools.reduce(jnp.add, cols), axis=-1, keepdims=True)
        acc = sum(jnp.dot(p.astype(BF16), vv, preferred_element_type=F32) for p, vv in zip(ps, vs))
        o = acc / l
        for gg in range(B_GROUP):
            hh = B_GROUP * j + gg
            o_ref[0, :, hh * dh:(hh + 1) * dh] = o[gg * tq:(gg + 1) * tq].astype(BF16)

    @pl.when(flag_ref[0] > 0)
    def _():
        shift = flag_ref[1].astype(F32)
        for j in range(B_KV_HEADS):
            q4, ks, vs = heads(j)
            finish(j, [jnp.exp2(s - shift) for s in scores(q4, ks)], vs)

    @pl.when(flag_ref[0] <= 0)
    def _():
        for j in range(B_KV_HEADS):
            q4, ks, vs = heads(j)
            ss = scores(q4, ks)
            m = functools.reduce(jnp.maximum, [jnp.max(s, axis=-1, keepdims=True) for s in ss])
            finish(j, [jnp.exp2(s - m) for s in ss], vs)


def _attn(flag, q, kv_segs):
    b, t, w = q.shape
    tq = min(ATTN_ROWS, t)
    n_seg = len(kv_segs)
    kv_flat, kv_specs = [], []
    for kk, vv in kv_segs:
        s_len, kw = kk.shape[1:]
        kv_flat += [kk, vv]
        kv_specs += [pl.BlockSpec((1, s_len, kw), lambda bb, i, fl: (bb, 0, 0))] * 2
    return pl.pallas_call(
        functools.partial(_attn_kernel, n_seg=n_seg),
        out_shape=jax.ShapeDtypeStruct((b, t, w), BF16),
        grid_spec=pltpu.PrefetchScalarGridSpec(
            num_scalar_prefetch=1,
            grid=(b, t // tq),
            in_specs=[pl.BlockSpec((1, tq, w), lambda bb, i, fl: (bb, i, 0))] + kv_specs,
            out_specs=pl.BlockSpec((1, tq, w), lambda bb, i, fl: (bb, i, 0))),
        compiler_params=_cparams("arbitrary", "arbitrary"),
        name="attn",
    )(flag, q, *kv_flat)


def _scan_rows(x, reverse):
    n = x.shape[0]
    rows = lax.broadcasted_iota(jnp.int32, x.shape, 0)
    sh = 1
    while sh < n:
        if reverse:
            x = x + jnp.where(rows < n - sh, pltpu.roll(x, n - sh, 0), 0.0)
        else:
            x = x + jnp.where(rows >= sh, pltpu.roll(x, sh, 0), 0.0)
        sh *= 2
    return x


def _stack_heads(x, lane_head):
    return jnp.concatenate([jnp.where(lane_head == hh, x, 0.0) for hh in range(C_HEADS)], axis=0)


def _hgrn_kernel(z_ref, zc_ref, lbc_ref, *refs, ctx_out):
    if ctx_out:
        o_ref, oc_ref, st_ref, kx_ref, bx_ref, vx_ref, lf_ref, kk_ref, flag_ref = refs
    else:
        o_ref, st_ref, kx_ref, bx_ref, vx_ref, lf_ref, kk_ref, flag_ref = refs
        oc_ref = None
    n = C_WIDTH
    nb = HGRN_BLOCK
    nblk_c = zc_ref.shape[1] // nb
    nblk_l = z_ref.shape[1] // nb
    ones_bd = _head_ones(n, BF16)
    rows = lax.broadcasted_iota(jnp.int32, (nb, n), 0)
    lane_head = lax.broadcasted_iota(jnp.int32, (nb, n), 1) >> 6
    lane_head64 = lax.broadcasted_iota(jnp.int32, (HEAD_DIM, n), 1) >> 6
    low_half = (lax.broadcasted_iota(jnp.int32, (HEAD_DIM, 2 * HEAD_DIM), 1) < HEAD_DIM)
    sc_t = lax.broadcasted_iota(jnp.int32, (nb, C_HEADS * nb), 0)
    sc_s = lax.broadcasted_iota(jnp.int32, (nb, C_HEADS * nb), 1) & (nb - 1)

    def gates(z, d):
        one_m_lb = lbc_ref[d, 0:1, :]
        log1m_lb = lbc_ref[d, 1:2, :]
        log_lb = lbc_ref[d, 2:3, :]
        lb_pos = lbc_ref[d, 3:4, :] > 0.5
        soft = jnp.log(1.0 + jnp.exp(-jnp.abs(z)))
        log_rest = log1m_lb + (jnp.minimum(z, 0.0) - soft)
        lse = jnp.maximum(log_lb, log_rest) + jnp.log(1.0 + jnp.exp(-jnp.abs(log_lb - log_rest)))
        return jnp.where(lb_pos, lse, log_rest), one_m_lb * jnp.exp(jnp.minimum(-z, 0.0) - soft)

    def group_blocks(nblk, i):
        grp = HGRN_GROUP if nblk % HGRN_GROUP == 0 else 1
        fwd = [i * grp + gg for gg in range(grp)]
        return grp, fwd, [nblk - 1 - blk for blk in fwd]

    def flag_groups(src_ref, nblk, base, row_base):
        def body(i, carry):
            _, fwd, bwd = group_blocks(nblk, i)
            worst = None
            for d, blks in ((0, fwd), (1, bwd)):
                for blk in blks:
                    r0 = pl.multiple_of(blk * nb, nb)
                    log_f, k = gates(src_ref[0, pl.ds(r0, nb), (1 + d) * n:(2 + d) * n], d)
                    lf_ref[d, pl.ds(row_base + r0, nb), :] = log_f
                    kk_ref[d, pl.ds(row_base + r0, nb), :] = k
                    decay = -jnp.sum(log_f, axis=0, keepdims=True)
                    worst = decay if worst is None else jnp.maximum(worst, decay)
            flag_ref[base + i] = (jnp.max(worst) <= SAFE_DECAY).astype(jnp.int32)
            return carry
        lax.fori_loop(0, nblk // group_blocks(nblk, 0)[0], body, 0)

    n_grp_c = nblk_c // group_blocks(nblk_c, 0)[0]
    flag_groups(zc_ref, nblk_c, 0, 0)
    flag_groups(z_ref, nblk_l, n_grp_c, zc_ref.shape[1])
    st_ref[...] = jnp.zeros_like(st_ref)
    o_ref[...] = jnp.zeros_like(o_ref)
    if ctx_out:
        oc_ref[...] = jnp.zeros_like(oc_ref)

    def step(src_ref, dst_ref, blk, d, fast, row_base):
        reverse = d == 1
        r0 = pl.multiple_of(blk * nb, nb)
        v = src_ref[0, pl.ds(r0, nb), 3 * n:4 * n]
        log_f = lf_ref[d, pl.ds(row_base + r0, nb), :]
        k = kk_ref[d, pl.ds(row_base + r0, nb), :]
        bc = _scan_rows(log_f, reverse)
        edge = 0 if reverse else nb - 1
        b_edge = bc[edge:edge + 1, :]
        st = st_ref[d]
        v_bf = v.astype(BF16)

        if dst_ref is not None:
            q = jax.nn.silu(src_ref[0, pl.ds(r0, nb), 0:n])
            qt = (q * jnp.exp(bc)).astype(BF16)
            o = lax.dot_general(qt, _stack_heads(st, lane_head64).astype(BF16), (((1,), (1,)), ((), ())),
                                preferred_element_type=F32)

            def intra_fast():
                kt = _stack_heads(k * jnp.exp(-bc), lane_head).astype(BF16)
                sc = lax.dot_general(qt, kt, (((1,), (1,)), ((), ())), preferred_element_type=F32)
                keep = (sc_s >= sc_t) if reverse else (sc_s <= sc_t)
                sc = jnp.where(keep, sc, 0.0).astype(BF16)
                return jnp.dot(sc, _stack_heads(v, lane_head).astype(BF16), preferred_element_type=F32)

            def intra_exact():
                kx_ref[d] = k
                bx_ref[d] = bc
                vx_ref[d] = v

                def sbody(s, acc):
                    keep = (rows <= s) if reverse else (rows >= s)
                    e = jnp.exp(jnp.where(keep, bc - bx_ref[d, pl.ds(s, 1), :], 0.0))
                    p = jnp.where(keep, q * e * kx_ref[d, pl.ds(s, 1), :], 0.0)
                    sc = jnp.dot(p.astype(BF16), ones_bd, preferred_element_type=F32)
                    return acc + sc * vx_ref[d, pl.ds(s, 1), :]

                return lax.fori_loop(0, nb, sbody, jnp.zeros((nb, n), F32))

            o = o + (intra_fast() if fast else intra_exact())
            dst_ref[0, pl.ds(r0, nb), :] += o

        kd = (k * jnp.exp(b_edge - bc)).astype(BF16)
        full = lax.dot_general(v_bf, kd, (((0,), (0,)), ((), ())), preferred_element_type=F32)
        upd = jnp.concatenate(
            [jnp.where(low_half,
                       full[(2 * c) * HEAD_DIM:(2 * c + 1) * HEAD_DIM, 2 * c * HEAD_DIM:(2 * c + 2) * HEAD_DIM],
                       full[(2 * c + 1) * HEAD_DIM:(2 * c + 2) * HEAD_DIM, 2 * c * HEAD_DIM:(2 * c + 2) * HEAD_DIM])
             for c in range(C_HEADS // 2)], axis=1)
        st_ref[d] = st * jnp.exp(b_edge) + upd

    def run(src_ref, dst_ref, nblk, base, row_base):
        def body(i, carry):
            _, fwd, bwd = group_blocks(nblk, i)
            safe = flag_ref[base + i]

            def group(fast):
                for bf, bb in zip(fwd, bwd):
                    step(src_ref, dst_ref, bf, 0, fast, row_base)
                    step(src_ref, dst_ref, bb, 1, fast, row_base)

            pl.when(safe > 0)(functools.partial(group, True))
            pl.when(safe <= 0)(functools.partial(group, False))
            return carry
        lax.fori_loop(0, nblk // group_blocks(nblk, 0)[0], body, 0)

    run(zc_ref, oc_ref, nblk_c, 0, 0)
    run(z_ref, o_ref, nblk_l, n_grp_c, zc_ref.shape[1])


def _hgrn(zc, zc_c, lbc, ctx_out):
    b, t, w = zc.shape
    lc = zc_c.shape[1]
    n = C_WIDTH
    row = lambda bb: (bb, 0, 0)
    out_shape = [jax.ShapeDtypeStruct((b, t, n), F32)]
    out_specs = [pl.BlockSpec((1, t, n), row)]
    if ctx_out:
        out_shape.append(jax.ShapeDtypeStruct((b, lc, n), F32))
        out_specs.append(pl.BlockSpec((1, lc, n), row))
    res = pl.pallas_call(
        functools.partial(_hgrn_kernel, ctx_out=ctx_out),
        out_shape=out_shape,
        grid=(b,),
        in_specs=[pl.BlockSpec((1, t, w), row),
                  pl.BlockSpec((1, lc, w), row),
                  pl.BlockSpec((2, 8, n), lambda bb: (0, 0, 0))],
        out_specs=out_specs,
        scratch_shapes=[pltpu.VMEM((2, HEAD_DIM, n), F32)]
                       + [pltpu.VMEM((2, HGRN_BLOCK, n), F32)] * 3
                       + [pltpu.VMEM((2, t + lc, n), F32)] * 2
                       + [pltpu.SMEM(((t + lc) // HGRN_BLOCK,), jnp.int32)],
        compiler_params=_cparams("arbitrary"),
        name="hgrn",
    )(zc, zc_c, lbc)
    return (res[0], res[1]) if ctx_out else (res[0], None)


def _select_experts(lg):
    lane = lax.broadcasted_iota(jnp.int32, lg.shape, 1)
    lane_f = lane.astype(F32)
    neg = -jnp.inf
    gl = jnp.where(lane < N_GROUPS, lg, neg)
    gmax = jnp.max(gl, axis=1, keepdims=True)
    grp = jnp.min(jnp.where(gl == gmax, lane_f, float(ROUTER_LANES)), axis=1, keepdims=True).astype(jnp.int32)
    p_grp = 1.0 / jnp.sum(jnp.exp(gl - gmax), axis=1, keepdims=True)
    in_grp = (lane >= N_GROUPS) & (lane < N_GROUPS + N_EXPERTS) & (((lane - N_GROUPS) >> 3) == grp)
    el = jnp.where(in_grp, lg, neg)
    v1 = jnp.max(el, axis=1, keepdims=True)
    i1 = jnp.min(jnp.where(el == v1, lane_f, float(ROUTER_LANES)), axis=1, keepdims=True)
    el2 = jnp.where(lane_f == i1, neg, el)
    v2 = jnp.max(el2, axis=1, keepdims=True)
    i2 = jnp.min(jnp.where(el2 == v2, lane_f, float(ROUTER_LANES)), axis=1, keepdims=True)
    rr = jnp.exp(v2 - v1)
    g1 = p_grp / (1.0 + rr)
    g2 = p_grp * rr / (1.0 + rr)
    sel = jnp.where(lane == 0, i1, jnp.where(lane == 1, i2, jnp.where(lane == 2, g1, jnp.where(lane == 3, g2, 0.0))))
    counts = jnp.sum(((lane_f == i1) | (lane_f == i2)).astype(F32), axis=0, keepdims=True)
    return sel, counts


def _outproj_kernel(x_ref, ya_ref, yb_ref, o_ref, g_ref, w_ref, gate_ref, mul_ref, add_ref, hw_ref,
                    wr_ref, br_ref, xo_ref, h2_ref, sel_ref, cnt_ref):
    yc = _head_rms(o_ref[0], _head_ones(C_WIDTH, BF16)) * hw_ref[...] * jax.nn.silu(g_ref[0])
    y = jnp.dot(ya_ref[0], w_ref[0:A_WIDTH, :], preferred_element_type=F32)
    y = y + jnp.dot(yb_ref[0], w_ref[A_WIDTH:A_WIDTH + B_WIDTH, :], preferred_element_type=F32)
    y = y + jnp.dot(yc.astype(BF16), w_ref[A_WIDTH + B_WIDTH:, :], preferred_element_type=F32)
    xn = x_ref[0] + gate_ref[0] * y
    xo_ref[0] = xn
    ms = jnp.mean(xn * xn, axis=-1, keepdims=True)
    h2 = xn * lax.rsqrt(ms + EPS) * mul_ref[0] + add_ref[0]
    _pack_planes(h2, h2_ref, (0,))
    h_hi = h2.astype(BF16)
    h_lo = (h2 - h_hi.astype(F32)).astype(BF16)
    both = jnp.dot(h_hi, wr_ref[...], preferred_element_type=F32)
    lg = (both[:, :ROUTER_LANES] + both[:, ROUTER_LANES:] + br_ref[...]
          + jnp.dot(h_lo, wr_ref[:, 0:ROUTER_LANES], preferred_element_type=F32))
    sel, counts = _select_experts(lg)
    sel_ref[0] = sel

    @pl.when((pl.program_id(0) == 0) & (pl.program_id(1) == 0))
    def _():
        cnt_ref[...] = jnp.zeros_like(cnt_ref)
    cnt_ref[...] += counts


def _outproj(x, ya, yb, o, g, w_bf, gate, mul, add, hw, wr, br):
    b, t, d = x.shape
    tm = min(PROJ_ROWS, t)
    row = lambda bb, i: (bb, i, 0)
    vec = lambda bb, i: (bb, 0, 0)
    const = lambda bb, i: (0, 0)
    return pl.pallas_call(
        _outproj_kernel,
        out_shape=[jax.ShapeDtypeStruct((b, t, d), F32),
                   jax.ShapeDtypeStruct((PLANES, b, t, SC_ROW), F32),
                   jax.ShapeDtypeStruct((b, t, ROUTER_LANES), F32),
                   jax.ShapeDtypeStruct((8, ROUTER_LANES), F32)],
        grid=(b, t // tm),
        in_specs=[pl.BlockSpec((1, tm, d), row),
                  pl.BlockSpec((1, tm, A_WIDTH), row),
                  pl.BlockSpec((1, tm, B_WIDTH), row),
                  pl.BlockSpec((1, tm, C_WIDTH), row),
                  pl.BlockSpec((1, tm, C_WIDTH), row),
                  pl.BlockSpec((d, d), const),
                  pl.BlockSpec((1, 1, d), vec),
                  pl.BlockSpec((1, 1, d), vec),
                  pl.BlockSpec((1, 1, d), vec),
                  pl.BlockSpec((1, C_WIDTH), const),
                  pl.BlockSpec((d, 2 * ROUTER_LANES), const),
                  pl.BlockSpec((1, ROUTER_LANES), const)],
        out_specs=[pl.BlockSpec((1, tm, d), row),
                   pl.BlockSpec((PLANES, 1, tm, SC_ROW), lambda bb, i: (0, bb, i, 0)),
                   pl.BlockSpec((1, tm, ROUTER_LANES), row),
                   pl.BlockSpec((8, ROUTER_LANES), const)],
        compiler_params=_cparams("arbitrary", "arbitrary"),
        name="outproj",
    )(x, ya, yb, o, g, w_bf, gate, mul, add, hw, wr, br)


def _route_kernel(sel_ref, cnt_ref, info_ref, meta_ref, base_ref):
    i = pl.program_id(0)
    tm = sel_ref.shape[0]
    lane = lax.broadcasted_iota(jnp.int32, (tm, ROUTER_LANES), 1)
    lane_f = lane.astype(F32)
    sel = sel_ref[...]
    hit1 = lane_f == sel[:, 0:1]
    hit2 = lane_f == sel[:, 1:2]
    onehot = (hit1 | hit2).astype(F32)

    @pl.when(i == 0)
    def _():
        counts = cnt_ref[...]
        padded = jnp.floor((counts + (MOE_ROWS - 1.0)) * (1.0 / MOE_ROWS)) * MOE_ROWS
        r = lax.broadcasted_iota(jnp.int32, (ROUTER_LANES, ROUTER_LANES), 0)
        c = lax.broadcasted_iota(jnp.int32, (ROUTER_LANES, ROUTER_LANES), 1)
        ends = jnp.dot(padded, (r <= c).astype(F32), preferred_element_type=F32,
                       precision=lax.Precision.HIGHEST)
        base_ref[...] = (ends - padded)[0:1]
        row = lax.broadcasted_iota(jnp.int32, (8, ROUTER_LANES), 0)
        meta_ref[...] = jnp.where(row == 0, counts, jnp.where(row == 1, ends - padded, ends))

    tr = lax.broadcasted_iota(jnp.int32, (tm, tm), 0)
    tc = lax.broadcasted_iota(jnp.int32, (tm, tm), 1)
    before = jnp.dot((tc < tr).astype(BF16), onehot.astype(BF16), preferred_element_type=F32)
    pos = base_ref[...] + before
    d1 = jnp.sum(jnp.where(hit1, pos, 0.0), axis=1, keepdims=True)
    d2 = jnp.sum(jnp.where(hit2, pos, 0.0), axis=1, keepdims=True)
    base_ref[...] += jnp.sum(onehot, axis=0, keepdims=True)
    info_ref[...] = jnp.where(lane == 0, d1, jnp.where(lane == 1, d2, sel))


def _route(sel, counts):
    n = sel.shape[0]
    tm = ROUTE_ROWS if n % ROUTE_ROWS == 0 else ROUTE_ROWS // 2
    return pl.pallas_call(
        _route_kernel,
        out_shape=[jax.ShapeDtypeStruct((n, ROUTER_LANES), F32),
                   jax.ShapeDtypeStruct((8, ROUTER_LANES), F32)],
        grid=(n // tm,),
        in_specs=[pl.BlockSpec((tm, ROUTER_LANES), lambda i: (i, 0)),
                  pl.BlockSpec((8, ROUTER_LANES), lambda i: (0, 0))],
        out_specs=[pl.BlockSpec((tm, ROUTER_LANES), lambda i: (i, 0)),
                   pl.BlockSpec((8, ROUTER_LANES), lambda i: (0, 0))],
        scratch_shapes=[pltpu.VMEM((1, ROUTER_LANES), F32)],
        compiler_params=_cparams("arbitrary"),
        name="route",
    )(sel, counts)


def _sc_mesh():
    return plsc.VectorSubcoreMesh(core_axis_name="c", subcore_axis_name="s")


def _sc_gather(table, idx):
    n = idx.shape[0]
    d = table.shape[1]

    @functools.partial(pl.kernel, out_type=jax.ShapeDtypeStruct((n, d), table.dtype), mesh=_sc_mesh())
    def gather(x_hbm, i_hbm, o_hbm):
        def body(i_vmem, o_vmem):
            pltpu.sync_copy(x_hbm.at[i_vmem.at[0]], o_vmem)

        pltpu.emit_pipeline(
            body,
            grid=(n // SC_WINDOW,),
            in_specs=[pl.BlockSpec((1, SC_WINDOW), lambda i: (0, i))],
            out_specs=[pl.BlockSpec((SC_WINDOW, d), lambda i: (i, 0))],
            core_axis_name=("c", "s"),
            dimension_semantics=(pltpu.PARALLEL,),
        )(i_hbm, o_hbm)

    return gather(table, idx.reshape(1, n))


def _sc_scatter2(rows, idx0, idx1, n_out):
    m, d = rows.shape

    @functools.partial(pl.kernel, out_type=jax.ShapeDtypeStruct((n_out, d), rows.dtype), mesh=_sc_mesh())
    def scatter(x_hbm, i0_hbm, i1_hbm, o_hbm):
        def body(x_vmem, i0_vmem, i1_vmem):
            pltpu.sync_copy(x_vmem, o_hbm.at[i0_vmem.at[0]])
            pltpu.sync_copy(x_vmem, o_hbm.at[i1_vmem.at[0]])

        pltpu.emit_pipeline(
            body,
            grid=(m // SC_WINDOW,),
            in_specs=[pl.BlockSpec((SC_WINDOW, d), lambda i: (i, 0)),
                      pl.BlockSpec((1, SC_WINDOW), lambda i: (0, i)),
                      pl.BlockSpec((1, SC_WINDOW), lambda i: (0, i))],
            out_specs=[],
            core_axis_name=("c", "s"),
            dimension_semantics=(pltpu.PARALLEL,),
        )(x_hbm, i0_hbm, i1_hbm)

    return scatter(rows, idx0.reshape(1, m), idx1.reshape(1, m))


def _moe_kernel(be_ref, nu_ref, x_ref, w1_ref, w3_ref, w2_ref, o_ref, w1b, w3b, w2b):
    i = pl.program_id(0)
    e = be_ref[i]
    prev = be_ref[jnp.maximum(i - 1, 0)]

    @pl.when((i == 0) | (e != prev))
    def _():
        w1b[...] = w1_ref[0, 0].astype(BF16)
        w3b[...] = w3_ref[0, 0].astype(BF16)
        w2b[...] = w2_ref[0, 0].astype(BF16)

    @pl.when(i < nu_ref[0])
    def _():
        parts = [h.astype(BF16) for p in range(PLANES) for h in _unpack_rows(x_ref[p])]
        a = sum(jnp.dot(h, w1b[q * SC_ROW:(q + 1) * SC_ROW, :], preferred_element_type=F32)
                for q, h in enumerate(parts))
        b = sum(jnp.dot(h, w3b[q * SC_ROW:(q + 1) * SC_ROW, :], preferred_element_type=F32)
                for q, h in enumerate(parts))
        hmid = (jax.nn.silu(a) * b).astype(BF16)
        _pack_planes(jnp.dot(hmid, w2b[...], preferred_element_type=F32), o_ref)

    @pl.when(i >= nu_ref[0])
    def _():
        o_ref[...] = jnp.zeros_like(o_ref)


def _moe_mlp(blk_expert, n_used, xs, w1, w3, w2, layer):
    n_rows = xs.shape[1]
    d, f = w1.shape[2:]
    nblk = n_rows // MOE_ROWS
    rows = lambda i, be, nu: (0, i, 0)
    return pl.pallas_call(
        _moe_kernel,
        out_shape=jax.ShapeDtypeStruct((PLANES, n_rows, SC_ROW), F32),
        grid_spec=pltpu.PrefetchScalarGridSpec(
            num_scalar_prefetch=2,
            grid=(nblk,),
            in_specs=[pl.BlockSpec((PLANES, MOE_ROWS, SC_ROW), rows),
                      pl.BlockSpec((1, 1, d, f), lambda i, be, nu: (layer, be[i], 0, 0)),
                      pl.BlockSpec((1, 1, d, f), lambda i, be, nu: (layer, be[i], 0, 0)),
                      pl.BlockSpec((1, 1, f, d), lambda i, be, nu: (layer, be[i], 0, 0))],
            out_specs=pl.BlockSpec((PLANES, MOE_ROWS, SC_ROW), rows),
            scratch_shapes=[pltpu.VMEM((d, f), BF16), pltpu.VMEM((d, f), BF16), pltpu.VMEM((f, d), BF16)]),
        compiler_params=_cparams("arbitrary"),
        name="moe",
    )(blk_expert, n_used, xs, w1, w3, w2)


def _combine_kernel(x_ref, pk_ref, info_ref, g_ref, o_ref):
    info = info_ref[...]
    g1 = info[:, 2:3]
    g2 = info[:, 3:4]
    parts = []
    for p in range(PLANES):
        hi1, lo1 = _unpack_rows(pk_ref[TOP_K * p])
        hi2, lo2 = _unpack_rows(pk_ref[TOP_K * p + 1])
        parts += [g1 * hi1 + g2 * hi2, g1 * lo1 + g2 * lo2]
    o_ref[0] = x_ref[0] + g_ref[0] * jnp.concatenate(parts, axis=1)


def _combine(x, picked, info, gate, row0):
    b, t, d = x.shape
    tm = min(256, t)
    off = row0 // tm
    tok = lambda bb, i: (off + bb * (t // tm) + i, 0)
    tok3 = lambda bb, i: (0, off + bb * (t // tm) + i, 0)
    return pl.pallas_call(
        _combine_kernel,
        out_shape=jax.ShapeDtypeStruct((b, t, d), F32),
        grid=(b, t // tm),
        in_specs=[pl.BlockSpec((1, tm, d), lambda bb, i: (bb, i, 0)),
                  pl.BlockSpec((PLANES * TOP_K, tm, SC_ROW), tok3),
                  pl.BlockSpec((tm, ROUTER_LANES), tok),
                  pl.BlockSpec((1, 1, d), lambda bb, i: (bb, 0, 0))],
        out_specs=pl.BlockSpec((1, tm, d), lambda bb, i: (bb, i, 0)),
        compiler_params=_cparams("arbitrary", "arbitrary"),
        name="combine",
    )(x, picked, info, gate)


def _hier_moe(h2p, sel, counts, w1, w3, w2, layer):
    n_tok = h2p.shape[1]
    info, meta = _route(sel, counts)
    dest = info[:, 0:TOP_K].astype(jnp.int32)
    pad_ends = meta[2, N_GROUPS:N_GROUPS + N_EXPERTS].astype(jnp.int32)
    nblk = -(-(n_tok * TOP_K) // MOE_ROWS) + N_EXPERTS
    n_rows = nblk * MOE_ROWS
    blk_start = jnp.arange(nblk, dtype=jnp.int32) * MOE_ROWS
    blk_expert = jnp.minimum(jnp.sum((pad_ends[None, :] <= blk_start[:, None]).astype(jnp.int32), axis=1),
                             N_EXPERTS - 1)
    n_used = pad_ends[-1:] // MOE_ROWS
    slot = [jnp.concatenate([p * n_rows + dest[:, s] for p in range(PLANES)]) for s in range(TOP_K)]
    xs = _sc_scatter2(h2p.reshape(PLANES * n_tok, SC_ROW), slot[0], slot[1], PLANES * n_rows)
    out = _moe_mlp(blk_expert, n_used, xs.reshape(PLANES, n_rows, SC_ROW), w1, w3, w2, layer)
    idx_all = jnp.concatenate([p * n_rows + dest[:, s] for p in range(PLANES) for s in range(TOP_K)])
    picked = _sc_gather(out.reshape(PLANES * n_rows, SC_ROW), idx_all)
    return picked.reshape(PLANES * TOP_K, n_tok, SC_ROW), info


def _layer(layer, x, xc, c, c_ctx, lb, w_mod, b_mod, norm1_w, w_in, w_s, b_s, q_norm_w, k_norm_w, hgrn_norm_w, w_out,
           norm2_w, w_grp, b_grp, w_exp, b_exp, w1, w3, w2, ctx_out):
    b, t, d = x.shape
    lc = xc.shape[1]
    cc = jnp.zeros((MOD_ROWS, d), F32).at[:b].set(c).at[b].set(c_ctx)
    mod = _mod(cc, w_mod, b_mod, layer)
    sh1, sc1, g1, sh2, sc2, g2 = [m[:, None, :] for m in jnp.split(mod[:b], 6, axis=-1)]
    mod_c = [jnp.broadcast_to(m[None, None, :], (b, 1, d)) for m in jnp.split(mod[b], 6)]

    w_in_bf = w_in.astype(BF16)
    scale = LOG2E * HEAD_DIM ** -0.5
    q_tabs = _rope_tables(t, q_norm_w, scale, B_WIDTH, True)
    k_tabs = _rope_tables(t, k_norm_w, 1.0, B_KV_WIDTH, True)
    qc_tabs = _rope_tables(lc, q_norm_w, scale, B_WIDTH, False)
    kc_tabs = _rope_tables(lc, k_norm_w, 1.0, B_KV_WIDTH, False)
    ws_bf = w_s.astype(BF16)
    bias2d = jnp.repeat(b_s.T, HEAD_DIM, axis=1)
    ya, q, k, v, zc, g = _inproj(x, norm1_w * (1.0 + sc1), sh1, w_in_bf, q_tabs, k_tabs, ws_bf, bias2d)
    ya_c, q_c, k_c, v_c, zc_c, g_c = _inproj(xc, norm1_w * (1.0 + mod_c[1]), mod_c[0], w_in_bf,
                                             qc_tabs, kc_tabs, ws_bf, bias2d)

    bound = LOG2E * HEAD_DIM ** 0.5 * jnp.max(jnp.abs(q_norm_w)) * jnp.max(jnp.abs(k_norm_w)) * 1.02
    shift = jnp.ceil(bound)
    attn_flag = jnp.stack([(shift <= SAFE_SHIFT).astype(jnp.int32), shift.astype(jnp.int32)])
    yb = _attn(attn_flag, q, [(k, v), (k_c, v_c)])

    pos = lb > 0.0
    lbc = jnp.stack([1.0 - lb, jnp.log1p(-lb), jnp.log(jnp.where(pos, lb, 1.0)), pos.astype(F32)], axis=1)
    lbc = jnp.concatenate([lbc, jnp.zeros((2, 4, C_WIDTH), F32)], axis=1)
    o, o_c = _hgrn(zc, zc_c, lbc, ctx_out)

    w_out_bf = w_out.astype(BF16)
    hw = jnp.tile(hgrn_norm_w, C_HEADS)[None, :]
    wr = jnp.zeros((d, ROUTER_LANES), F32).at[:, :N_GROUPS].set(w_grp).at[
        :, N_GROUPS:N_GROUPS + N_EXPERTS].set(w_exp)
    wr_hi = wr.astype(BF16)
    wr = jnp.concatenate([wr_hi, (wr - wr_hi.astype(F32)).astype(BF16)], axis=1)
    br = jnp.zeros((1, ROUTER_LANES), F32).at[0, :N_GROUPS].set(b_grp).at[
        0, N_GROUPS:N_GROUPS + N_EXPERTS].set(b_exp)
    x, h2, sel, cnt = _outproj(x, ya, yb, o, g, w_out_bf, g1, norm2_w * (1.0 + sc2), sh2, hw, wr, br)
    if ctx_out:
        yb_c = _attn(attn_flag, q_c, [(k_c, v_c)])
        xc, h2c, sel_c, cnt_c = _outproj(xc, ya_c, yb_c, o_c, g_c, w_out_bf, mod_c[2],
                                norm2_w * (1.0 + mod_c[4]), mod_c[3], hw, wr, br)
        tokens = jnp.concatenate([h2.reshape(PLANES, -1, SC_ROW), h2c.reshape(PLANES, -1, SC_ROW)], axis=1)
        sel_all = jnp.concatenate([sel.reshape(-1, ROUTER_LANES), sel_c.reshape(-1, ROUTER_LANES)], axis=0)
        picked, info = _hier_moe(tokens, sel_all, cnt + cnt_c, w1, w3, w2, layer)
        x = _combine(x, picked, info, g2, 0)
        xc = _combine(xc, picked, info, mod_c[5], b * t)
    else:
        picked, info = _hier_moe(h2.reshape(PLANES, -1, SC_ROW), sel.reshape(-1, ROUTER_LANES), cnt, w1, w3, w2, layer)
        x = _combine(x, picked, info, g2, 0)
    return x, xc


def kernel(x, c, ctx, c_ctx, w_mod, b_mod, norm1_w, w_in, w_s, b_s, q_norm_w, k_norm_w, hgrn_lb_logits,
           hgrn_norm_w, w_out, norm2_w, w_grp, b_grp, w_exp, b_exp, w1, w3, w2):
    depth = w_mod.shape[0]
    lb_sm = jax.nn.softmax(hgrn_lb_logits.astype(F32), axis=0)
    lb = jnp.cumsum(lb_sm, axis=0) - lb_sm[0]
    xc = ctx
    for l in range(depth):
        x, xc = _layer(l, x, xc, c, c_ctx, lb[l], w_mod, b_mod[l], norm1_w[l], w_in[l], w_s[l], b_s[l],
                       q_norm_w[l], k_norm_w[l], hgrn_norm_w[l], w_out[l], norm2_w[l], w_grp[l], b_grp[l],
                       w_exp[l], b_exp[l], w1, w3, w2, ctx_out=(l < depth - 1))
    return x
```

```python
import functools

import jax
import jax.numpy as jnp
from jax import lax
from jax.experimental import pallas as pl
from jax.experimental.pallas import tpu as pltpu
from jax.experimental.pallas import tpu_sc as plsc

F32 = jnp.float32
BF16 = jnp.bfloat16

D_MODEL = 1024
HEAD_DIM = 64
GRID_W = 64
EPS = 1e-6
ROPE_BASE = 10000.0
A_WIDTH = D_MODEL // 4
A_HEADS = A_WIDTH // HEAD_DIM
A_CHUNK = 128
B_WIDTH = D_MODEL // 2
B_HEADS = B_WIDTH // HEAD_DIM
B_KV_HEADS = 2
B_GROUP = B_HEADS // B_KV_HEADS
B_KV_WIDTH = B_KV_HEADS * HEAD_DIM
C_WIDTH = D_MODEL // 4
C_HEADS = C_WIDTH // HEAD_DIM
OFF_B = 2 * A_WIDTH
OFF_KV = OFF_B + B_WIDTH
OFF_V = OFF_KV + B_KV_WIDTH
OFF_C = OFF_KV + 2 * B_KV_WIDTH
OFF_G = OFF_C + 4 * C_WIDTH
IN_WIDTH = OFF_G + C_WIDTH
N_GROUPS = 4
EXPERTS_PER_GROUP = 8
N_EXPERTS = N_GROUPS * EXPERTS_PER_GROUP
TOP_K = 2
D_FF_EXPERT = D_MODEL // 2

MOD_ROWS = 16
ROUTER_LANES = 128
PROJ_ROWS = 512
HGRN_BLOCK = 32
HGRN_GROUP = 4
LOG2E = 1.4426950408889634
ATTN_ROWS = 256
ATTN_KEYS = 512
SAFE_SHIFT = 60
SAFE_DECAY = 80.0
MOE_ROWS = 512
ROUTE_ROWS = 512
SC_WINDOW = 128
SC_ROW = 256
PLANES = D_MODEL // (2 * SC_ROW)
VMEM_LIMIT = 48 * 1024 * 1024


def _cparams(*sem):
    return pltpu.CompilerParams(dimension_semantics=sem, vmem_limit_bytes=VMEM_LIMIT)


def _head_ones(n, dtype):
    r = lax.broadcasted_iota(jnp.int32, (n, n), 0) >> 6
    c = lax.broadcasted_iota(jnp.int32, (n, n), 1) >> 6
    return (r == c).astype(dtype)


def _head_sum(x, ones_bd):
    return jnp.dot(x.astype(BF16), ones_bd, preferred_element_type=F32)


def _head_rms(x, ones_bd):
    return x * lax.rsqrt(_head_sum(x * x, ones_bd) * (1.0 / HEAD_DIM) + EPS)


def _pack_rows(y):
    bits = lax.bitcast_convert_type(y.astype(BF16).astype(F32), jnp.uint32)
    half = y.shape[1] // 2
    return lax.bitcast_convert_type(bits[:, :half] | (bits[:, half:] >> 16), F32)


def _unpack_rows(w):
    bits = lax.bitcast_convert_type(w, jnp.uint32)
    hi = lax.bitcast_convert_type(bits & jnp.uint32(0xFFFF0000), F32)
    lo = lax.bitcast_convert_type(bits << 16, F32)
    return hi, lo


def _pack_planes(y, ref, lead=()):
    for p in range(PLANES):
        ref[(p,) + lead] = _pack_rows(y[:, 2 * p * SC_ROW:(2 * p + 2) * SC_ROW])


def _mod_kernel(c_ref, w_ref, b_ref, o_ref):
    a = jax.nn.silu(c_ref[...])
    o_ref[...] = jnp.dot(a, w_ref[0], preferred_element_type=F32,
                         precision=lax.Precision.HIGHEST) + b_ref[...]


def _mod(cc, w_mod, b_mod, layer):
    n = w_mod.shape[2]
    tn = 1536
    return pl.pallas_call(
        _mod_kernel,
        out_shape=jax.ShapeDtypeStruct((MOD_ROWS, n), F32),
        grid=(n // tn,),
        in_specs=[pl.BlockSpec((MOD_ROWS, D_MODEL), lambda j: (0, 0)),
                  pl.BlockSpec((1, D_MODEL, tn), lambda j: (layer, 0, j)),
                  pl.BlockSpec((1, tn), lambda j: (0, j))],
        out_specs=pl.BlockSpec((MOD_ROWS, tn), lambda j: (0, j)),
        compiler_params=_cparams("arbitrary"),
        name="mod",
    )(cc, w_mod, b_mod.reshape(1, n))


def _rope(xn, c_ref, sp_ref, sm_ref):
    w = xn.shape[-1]
    return (xn * c_ref[...] + pltpu.roll(xn, 16, 1) * sp_ref[...]
            + pltpu.roll(xn, w - 16, 1) * sm_ref[...])


def _gmlp(z, ws_ref, bias_ref):
    gz = jax.nn.gelu(z)
    u = gz[:, :A_WIDTH]
    vn = _head_rms(gz[:, A_WIDTH:], _head_ones(A_WIDTH, BF16))
    lane_head = lax.broadcasted_iota(jnp.int32, vn.shape, 1) >> 6
    acc = bias_ref[...]
    for hh in range(A_HEADS):
        vh = jnp.where(lane_head == hh, vn, 0.0).astype(BF16)
        acc = acc + jnp.dot(ws_ref[hh], vh, preferred_element_type=F32)
    return u * acc


def _inproj_kernel(x_ref, mul_ref, add_ref, w_ref, qc_ref, qsp_ref, qsm_ref, kc_ref, ksp_ref, ksm_ref,
                   ws_ref, bias_ref, ya_ref, q_ref, k_ref, v_ref, zc_ref, g_ref):
    x = x_ref[0]
    ms = jnp.mean(x * x, axis=-1, keepdims=True)
    h = x * lax.rsqrt(ms + EPS) * mul_ref[0] + add_ref[0]
    y = jnp.dot(h.astype(BF16), w_ref[...], preferred_element_type=F32)
    for c0 in range(0, x.shape[0], A_CHUNK):
        ya_ref[0, c0:c0 + A_CHUNK, :] = _gmlp(y[c0:c0 + A_CHUNK, :OFF_B], ws_ref, bias_ref).astype(BF16)
    qn = _head_rms(y[:, OFF_B:OFF_KV], _head_ones(B_WIDTH, BF16))
    q_ref[0] = _rope(qn, qc_ref, qsp_ref, qsm_ref).astype(BF16)
    kn = _head_rms(y[:, OFF_KV:OFF_V], _head_ones(B_KV_WIDTH, BF16))
    k_ref[0] = _rope(kn, kc_ref, ksp_ref, ksm_ref).astype(BF16)
    v_ref[0] = y[:, OFF_V:OFF_C].astype(BF16)
    zc_ref[0] = y[:, OFF_C:OFF_G]
    g_ref[0] = y[:, OFF_G:]


def _inproj(x, mul, add, w_bf, q_tabs, k_tabs, ws_bf, bias2d):
    b, t, d = x.shape
    tm = min(PROJ_ROWS, t)
    row = lambda i, bb: (bb, i, 0)
    vec = lambda i, bb: (bb, 0, 0)
    tab = lambda i, bb: (i, 0)
    widths = (A_WIDTH, B_WIDTH, B_KV_WIDTH, B_KV_WIDTH, 4 * C_WIDTH, C_WIDTH)
    dtypes = (BF16, BF16, BF16, BF16, F32, F32)
    return pl.pallas_call(
        _inproj_kernel,
        out_shape=[jax.ShapeDtypeStruct((b, t, w), dt) for w, dt in zip(widths, dtypes)],
        grid=(t // tm, b),
        in_specs=[pl.BlockSpec((1, tm, d), row),
                  pl.BlockSpec((1, 1, d), vec),
                  pl.BlockSpec((1, 1, d), vec),
                  pl.BlockSpec((d, IN_WIDTH), lambda i, bb: (0, 0))]
                 + [pl.BlockSpec((tm, B_WIDTH), tab)] * 3
                 + [pl.BlockSpec((tm, B_KV_WIDTH), tab)] * 3
                 + [pl.BlockSpec((A_HEADS, A_CHUNK, A_CHUNK), lambda i, bb: (0, 0, 0)),
                    pl.BlockSpec((A_CHUNK, A_WIDTH), lambda i, bb: (0, 0))],
        out_specs=[pl.BlockSpec((1, tm, w), row) for w in widths],
        compiler_params=_cparams("arbitrary", "arbitrary"),
        name="inproj",
    )(x, mul, add, w_bf, *q_tabs, *k_tabs, ws_bf, bias2d)


def _rope_tables(t, w, scale, width, rotate):
    ws = w.astype(F32) * scale
    if not rotate:
        c = jnp.broadcast_to(jnp.tile(ws, width // HEAD_DIM)[None, :], (t, width))
        z = jnp.zeros((t, width), F32)
        return c, z, z
    pos = jnp.arange(t)
    row = (pos // GRID_W).astype(F32)
    col = (pos % GRID_W).astype(F32)
    inv_freq = 1.0 / (ROPE_BASE ** (jnp.arange(0, HEAD_DIM // 2, 2, dtype=F32) / (HEAD_DIM // 2)))
    dd = jnp.arange(HEAD_DIM)
    axis = dd // 32
    half = (dd % 32) // 16
    ang = jnp.where(axis[None, :] == 0, row[:, None], col[:, None]) * inv_freq[dd % 16][None, :]
    cos, sin = jnp.cos(ang), jnp.sin(ang)
    c = cos * ws[None, :]
    sm = jnp.where(half[None, :] == 0, -sin * jnp.roll(ws, -16)[None, :], 0.0)
    sp = jnp.where(half[None, :] == 1, sin * jnp.roll(ws, 16)[None, :], 0.0)
    rep = width // HEAD_DIM
    return jnp.tile(c, (1, rep)), jnp.tile(sp, (1, rep)), jnp.tile(sm, (1, rep))


def _attn_kernel(flag_ref, q_ref, *refs, n_seg):
    kv_refs, o_ref = refs[:2 * n_seg], refs[2 * n_seg]
    tq = q_ref.shape[1]
    dh = HEAD_DIM

    def heads(j):
        q4 = jnp.concatenate([q_ref[0, :, (B_GROUP * j + gg) * dh:(B_GROUP * j + gg + 1) * dh]
                              for gg in range(B_GROUP)], axis=0)
        ks, vs = [], []
        for sg in range(n_seg):
            s_len = kv_refs[2 * sg].shape[1]
            for c0 in range(0, s_len, ATTN_KEYS):
                c1 = min(c0 + ATTN_KEYS, s_len)
                ks.append(kv_refs[2 * sg][0, c0:c1, j * dh:(j + 1) * dh])
                vs.append(kv_refs[2 * sg + 1][0, c0:c1, j * dh:(j + 1) * dh])
        return q4, ks, vs

    def scores(q4, ks):
        return [lax.dot_general(kk, q4, (((1,), (1,)), ((), ())), preferred_element_type=F32) for kk in ks]

    def finish(j, ps, vs):
        l = functools.reduce(jnp.add, [jnp.sum(p, axis=0, keepdims=True) for p in ps])
        acc = sum(lax.dot_general(vv, p.astype(BF16), (((0,), (0,)), ((), ())), preferred_element_type=F32)
                  for p, vv in zip(ps, vs))
        o = (acc / l).T
        for gg in range(B_GROUP):
            hh = B_GROUP * j + gg
            o_ref[0, :, hh * dh:(hh + 1) * dh] = o[gg * tq:(gg + 1) * tq].astype(BF16)

    @pl.when(flag_ref[0] > 0)
    def _():
        shift = flag_ref[1].astype(F32)
        for j in range(B_KV_HEADS):
            q4, ks, vs = heads(j)
            finish(j, [jnp.exp2(s - shift) for s in scores(q4, ks)], vs)

    @pl.when(flag_ref[0] <= 0)
    def _():
        for j in range(B_KV_HEADS):
            q4, ks, vs = heads(j)
            ss = scores(q4, ks)
            m = functools.reduce(jnp.maximum, [jnp.max(s, axis=0, keepdims=True) for s in ss])
            finish(j, [jnp.exp2(s - m) for s in ss], vs)


def _attn(flag, q, kv_segs):
    b, t, w = q.shape
    tq = min(ATTN_ROWS, t)
    n_seg = len(kv_segs)
    kv_flat, kv_specs = [], []
    for kk, vv in kv_segs:
        s_len, kw = kk.shape[1:]
        kv_flat += [kk, vv]
        kv_specs += [pl.BlockSpec((1, s_len, kw), lambda bb, i, fl: (bb, 0, 0))] * 2
    return pl.pallas_call(
        functools.partial(_attn_kernel, n_seg=n_seg),
        out_shape=jax.ShapeDtypeStruct((b, t, w), BF16),
        grid_spec=pltpu.PrefetchScalarGridSpec(
            num_scalar_prefetch=1,
            grid=(b, t // tq),
            in_specs=[pl.BlockSpec((1, tq, w), lambda bb, i, fl: (bb, i, 0))] + kv_specs,
            out_specs=pl.BlockSpec((1, tq, w), lambda bb, i, fl: (bb, i, 0))),
        compiler_params=_cparams("arbitrary", "arbitrary"),
        name="attn",
    )(flag, q, *kv_flat)


def _scan_rows(x, reverse):
    n = x.shape[0]
    rows = lax.broadcasted_iota(jnp.int32, x.shape, 0)
    sh = 1
    while sh < n:
        if reverse:
            x = x + jnp.where(rows < n - sh, pltpu.roll(x, n - sh, 0), 0.0)
        else:
            x = x + jnp.where(rows >= sh, pltpu.roll(x, sh, 0), 0.0)
        sh *= 2
    return x


def _stack_heads(x, lane_head):
    return jnp.concatenate([jnp.where(lane_head == hh, x, 0.0) for hh in range(C_HEADS)], axis=0)


def _hgrn_kernel(z_ref, zc_ref, lbc_ref, *refs, ctx_out):
    if ctx_out:
        o_ref, oc_ref, st_ref, kx_ref, bx_ref, vx_ref, flag_ref = refs
    else:
        o_ref, st_ref, kx_ref, bx_ref, vx_ref, flag_ref = refs
        oc_ref = None
    n = C_WIDTH
    nb = HGRN_BLOCK
    nblk_c = zc_ref.shape[1] // nb
    nblk_l = z_ref.shape[1] // nb
    ones_bd = _head_ones(n, BF16)
    rows = lax.broadcasted_iota(jnp.int32, (nb, n), 0)
    lane_head = lax.broadcasted_iota(jnp.int32, (nb, n), 1) >> 6
    lane_head64 = lax.broadcasted_iota(jnp.int32, (HEAD_DIM, n), 1) >> 6
    low_half = (lax.broadcasted_iota(jnp.int32, (HEAD_DIM, 2 * HEAD_DIM), 1) < HEAD_DIM)
    sc_t = lax.broadcasted_iota(jnp.int32, (nb, C_HEADS * nb), 0)
    sc_s = lax.broadcasted_iota(jnp.int32, (nb, C_HEADS * nb), 1) & (nb - 1)

    def gates(z, d):
        one_m_lb = lbc_ref[d, 0:1, :]
        log1m_lb = lbc_ref[d, 1:2, :]
        log_lb = lbc_ref[d, 2:3, :]
        lb_pos = lbc_ref[d, 3:4, :] > 0.5
        soft = jnp.log(1.0 + jnp.exp(-jnp.abs(z)))
        log_rest = log1m_lb + (jnp.minimum(z, 0.0) - soft)
        lse = jnp.maximum(log_lb, log_rest) + jnp.log(1.0 + jnp.exp(-jnp.abs(log_lb - log_rest)))
        return jnp.where(lb_pos, lse, log_rest), one_m_lb * jnp.exp(jnp.minimum(-z, 0.0) - soft)

    def group_blocks(nblk, i):
        grp = HGRN_GROUP if nblk % HGRN_GROUP == 0 else 1
        fwd = [i * grp + gg for gg in range(grp)]
        return grp, fwd, [nblk - 1 - blk for blk in fwd]

    def flag_groups(src_ref, nblk, base):
        def body(i, carry):
            _, fwd, bwd = group_blocks(nblk, i)
            worst = None
            for d, blks in ((0, fwd), (1, bwd)):
                for blk in blks:
                    zz = src_ref[0, pl.ds(pl.multiple_of(blk * nb, nb), nb), (1 + d) * n:(2 + d) * n]
                    step_bound = jnp.minimum(lbc_ref[d, 4:5, :],
                                             jnp.maximum(-zz, 0.0) + (jnp.log(2.0) - lbc_ref[d, 1:2, :]))
                    bound = jnp.sum(step_bound, axis=0, keepdims=True)
                    worst = bound if worst is None else jnp.maximum(worst, bound)
            flag_ref[base + i] = (jnp.max(worst) <= SAFE_DECAY).astype(jnp.int32)
            return carry
        lax.fori_loop(0, nblk // group_blocks(nblk, 0)[0], body, 0)

    n_grp_c = nblk_c // group_blocks(nblk_c, 0)[0]
    flag_groups(zc_ref, nblk_c, 0)
    flag_groups(z_ref, nblk_l, n_grp_c)
    st_ref[...] = jnp.zeros_like(st_ref)
    o_ref[...] = jnp.zeros_like(o_ref)
    if ctx_out:
        oc_ref[...] = jnp.zeros_like(oc_ref)

    def step(src_ref, dst_ref, blk, d, fast):
        reverse = d == 1
        r0 = pl.multiple_of(blk * nb, nb)
        v = src_ref[0, pl.ds(r0, nb), 3 * n:4 * n]
        log_f, k = gates(src_ref[0, pl.ds(r0, nb), (1 + d) * n:(2 + d) * n], d)
        bc = _scan_rows(log_f, reverse)
        edge = 0 if reverse else nb - 1
        b_edge = bc[edge:edge + 1, :]
        st = st_ref[d]
        v_bf = v.astype(BF16)

        if dst_ref is not None:
            q = jax.nn.silu(src_ref[0, pl.ds(r0, nb), 0:n])
            qt = (q * jnp.exp(bc)).astype(BF16)
            o = lax.dot_general(qt, _stack_heads(st, lane_head64).astype(BF16), (((1,), (1,)), ((), ())),
                                preferred_element_type=F32)

            def intra_fast():
                kt = _stack_heads(k * jnp.exp(-bc), lane_head).astype(BF16)
                sc = lax.dot_general(qt, kt, (((1,), (1,)), ((), ())), preferred_element_type=F32)
                keep = (sc_s >= sc_t) if reverse else (sc_s <= sc_t)
                sc = jnp.where(keep, sc, 0.0).astype(BF16)
                return jnp.dot(sc, _stack_heads(v, lane_head).astype(BF16), preferred_element_type=F32)

            def intra_exact():
                kx_ref[d] = k
                bx_ref[d] = bc
                vx_ref[d] = v

                def sbody(s, acc):
                    keep = (rows <= s) if reverse else (rows >= s)
                    e = jnp.exp(jnp.where(keep, bc - bx_ref[d, pl.ds(s, 1), :], 0.0))
                    p = jnp.where(keep, q * e * kx_ref[d, pl.ds(s, 1), :], 0.0)
                    sc = jnp.dot(p.astype(BF16), ones_bd, preferred_element_type=F32)
                    return acc + sc * vx_ref[d, pl.ds(s, 1), :]

                return lax.fori_loop(0, nb, sbody, jnp.zeros((nb, n), F32))

            o = o + (intra_fast() if fast else intra_exact())
            dst_ref[0, pl.ds(r0, nb), :] += o

        kd = (k * jnp.exp(b_edge - bc)).astype(BF16)
        full = lax.dot_general(v_bf, kd, (((0,), (0,)), ((), ())), preferred_element_type=F32)
        upd = jnp.concatenate(
            [jnp.where(low_half,
                       full[(2 * c) * HEAD_DIM:(2 * c + 1) * HEAD_DIM, 2 * c * HEAD_DIM:(2 * c + 2) * HEAD_DIM],
                       full[(2 * c + 1) * HEAD_DIM:(2 * c + 2) * HEAD_DIM, 2 * c * HEAD_DIM:(2 * c + 2) * HEAD_DIM])
             for c in range(C_HEADS // 2)], axis=1)
        st_ref[d] = st * jnp.exp(b_edge) + upd

    def run(src_ref, dst_ref, nblk, base):
        def body(i, carry):
            _, fwd, bwd = group_blocks(nblk, i)
            safe = flag_ref[base + i]

            def group(fast):
                for bf, bb in zip(fwd, bwd):
                    step(src_ref, dst_ref, bf, 0, fast)
                    step(src_ref, dst_ref, bb, 1, fast)

            pl.when(safe > 0)(functools.partial(group, True))
            pl.when(safe <= 0)(functools.partial(group, False))
            return carry
        lax.fori_loop(0, nblk // group_blocks(nblk, 0)[0], body, 0)

    run(zc_ref, oc_ref, nblk_c, 0)
    run(z_ref, o_ref, nblk_l, n_grp_c)


def _hgrn(zc, zc_c, lbc, ctx_out):
    b, t, w = zc.shape
    lc = zc_c.shape[1]
    n = C_WIDTH
    row = lambda bb: (bb, 0, 0)
    out_shape = [jax.ShapeDtypeStruct((b, t, n), F32)]
    out_specs = [pl.BlockSpec((1, t, n), row)]
    if ctx_out:
        out_shape.append(jax.ShapeDtypeStruct((b, lc, n), F32))
        out_specs.append(pl.BlockSpec((1, lc, n), row))
    res = pl.pallas_call(
        functools.partial(_hgrn_kernel, ctx_out=ctx_out),
        out_shape=out_shape,
        grid=(b,),
        in_specs=[pl.BlockSpec((1, t, w), row),
                  pl.BlockSpec((1, lc, w), row),
                  pl.BlockSpec((2, 8, n), lambda bb: (0, 0, 0))],
        out_specs=out_specs,
        scratch_shapes=[pltpu.VMEM((2, HEAD_DIM, n), F32)]
                       + [pltpu.VMEM((2, HGRN_BLOCK, n), F32)] * 3
                       + [pltpu.SMEM(((t + lc) // HGRN_BLOCK,), jnp.int32)],
        compiler_params=_cparams("arbitrary"),
        name="hgrn",
    )(zc, zc_c, lbc)
    return (res[0], res[1]) if ctx_out else (res[0], None)


def _select_experts(lg):
    lane = lax.broadcasted_iota(jnp.int32, lg.shape, 1)
    lane_f = lane.astype(F32)
    neg = -jnp.inf
    gl = jnp.where(lane < N_GROUPS, lg, neg)
    gmax = jnp.max(gl, axis=1, keepdims=True)
    grp = jnp.min(jnp.where(gl == gmax, lane_f, float(ROUTER_LANES)), axis=1, keepdims=True).astype(jnp.int32)
    p_grp = 1.0 / jnp.sum(jnp.exp(gl - gmax), axis=1, keepdims=True)
    in_grp = (lane >= N_GROUPS) & (lane < N_GROUPS + N_EXPERTS) & (((lane - N_GROUPS) >> 3) == grp)
    el = jnp.where(in_grp, lg, neg)
    v1 = jnp.max(el, axis=1, keepdims=True)
    i1 = jnp.min(jnp.where(el == v1, lane_f, float(ROUTER_LANES)), axis=1, keepdims=True)
    el2 = jnp.where(lane_f == i1, neg, el)
    v2 = jnp.max(el2, axis=1, keepdims=True)
    i2 = jnp.min(jnp.where(el2 == v2, lane_f, float(ROUTER_LANES)), axis=1, keepdims=True)
    rr = jnp.exp(v2 - v1)
    g1 = p_grp / (1.0 + rr)
    g2 = p_grp * rr / (1.0 + rr)
    sel = jnp.where(lane == 0, i1, jnp.where(lane == 1, i2, jnp.where(lane == 2, g1, jnp.where(lane == 3, g2, 0.0))))
    counts = jnp.sum(((lane_f == i1) | (lane_f == i2)).astype(F32), axis=0, keepdims=True)
    return sel, counts


def _outproj_kernel(x_ref, ya_ref, yb_ref, o_ref, g_ref, w_ref, gate_ref, mul_ref, add_ref, hw_ref,
                    wr_ref, br_ref, xo_ref, h2_ref, sel_ref, cnt_ref):
    yc = _head_rms(o_ref[0], _head_ones(C_WIDTH, BF16)) * hw_ref[...] * jax.nn.silu(g_ref[0])
    y = jnp.dot(ya_ref[0], w_ref[0:A_WIDTH, :], preferred_element_type=F32)
    y = y + jnp.dot(yb_ref[0], w_ref[A_WIDTH:A_WIDTH + B_WIDTH, :], preferred_element_type=F32)
    y = y + jnp.dot(yc.astype(BF16), w_ref[A_WIDTH + B_WIDTH:, :], preferred_element_type=F32)
    xn = x_ref[0] + gate_ref[0] * y
    xo_ref[0] = xn
    ms = jnp.mean(xn * xn, axis=-1, keepdims=True)
    h2 = xn * lax.rsqrt(ms + EPS) * mul_ref[0] + add_ref[0]
    _pack_planes(h2, h2_ref, (0,))
    h_hi = h2.astype(BF16)
    h_lo = (h2 - h_hi.astype(F32)).astype(BF16)
    both = jnp.dot(h_hi, wr_ref[...], preferred_element_type=F32)
    lg = (both[:, :ROUTER_LANES] + both[:, ROUTER_LANES:] + br_ref[...]
          + jnp.dot(h_lo, wr_ref[:, 0:ROUTER_LANES], preferred_element_type=F32))
    sel, counts = _select_experts(lg)
    sel_ref[0] = sel

    @pl.when((pl.program_id(0) == 0) & (pl.program_id(1) == 0))
    def _():
        cnt_ref[...] = jnp.zeros_like(cnt_ref)
    cnt_ref[...] += counts


def _outproj(x, ya, yb, o, g, w_bf, gate, mul, add, hw, wr, br):
    b, t, d = x.shape
    tm = min(PROJ_ROWS, t)
    row = lambda bb, i: (bb, i, 0)
    vec = lambda bb, i: (bb, 0, 0)
    const = lambda bb, i: (0, 0)
    return pl.pallas_call(
        _outproj_kernel,
        out_shape=[jax.ShapeDtypeStruct((b, t, d), F32),
                   jax.ShapeDtypeStruct((PLANES, b, t, SC_ROW), F32),
                   jax.ShapeDtypeStruct((b, t, ROUTER_LANES), F32),
                   jax.ShapeDtypeStruct((8, ROUTER_LANES), F32)],
        grid=(b, t // tm),
        in_specs=[pl.BlockSpec((1, tm, d), row),
                  pl.BlockSpec((1, tm, A_WIDTH), row),
                  pl.BlockSpec((1, tm, B_WIDTH), row),
                  pl.BlockSpec((1, tm, C_WIDTH), row),
                  pl.BlockSpec((1, tm, C_WIDTH), row),
                  pl.BlockSpec((d, d), const),
                  pl.BlockSpec((1, 1, d), vec),
                  pl.BlockSpec((1, 1, d), vec),
                  pl.BlockSpec((1, 1, d), vec),
                  pl.BlockSpec((1, C_WIDTH), const),
                  pl.BlockSpec((d, 2 * ROUTER_LANES), const),
                  pl.BlockSpec((1, ROUTER_LANES), const)],
        out_specs=[pl.BlockSpec((1, tm, d), row),
                   pl.BlockSpec((PLANES, 1, tm, SC_ROW), lambda bb, i: (0, bb, i, 0)),
                   pl.BlockSpec((1, tm, ROUTER_LANES), row),
                   pl.BlockSpec((8, ROUTER_LANES), const)],
        compiler_params=_cparams("arbitrary", "arbitrary"),
        name="outproj",
    )(x, ya, yb, o, g, w_bf, gate, mul, add, hw, wr, br)


def _route_kernel(sel_ref, cnt_ref, info_ref, meta_ref, base_ref):
    i = pl.program_id(0)
    tm = sel_ref.shape[0]
    lane = lax.broadcasted_iota(jnp.int32, (tm, ROUTER_LANES), 1)
    lane_f = lane.astype(F32)
    sel = sel_ref[...]
    hit1 = lane_f == sel[:, 0:1]
    hit2 = lane_f == sel[:, 1:2]
    onehot = (hit1 | hit2).astype(F32)

    @pl.when(i == 0)
    def _():
        counts = cnt_ref[...]
        padded = jnp.floor((counts + (MOE_ROWS - 1.0)) * (1.0 / MOE_ROWS)) * MOE_ROWS
        r = lax.broadcasted_iota(jnp.int32, (ROUTER_LANES, ROUTER_LANES), 0)
        c = lax.broadcasted_iota(jnp.int32, (ROUTER_LANES, ROUTER_LANES), 1)
        ends = jnp.dot(padded, (r <= c).astype(F32), preferred_element_type=F32,
                       precision=lax.Precision.HIGHEST)
        base_ref[...] = (ends - padded)[0:1]
        row = lax.broadcasted_iota(jnp.int32, (8, ROUTER_LANES), 0)
        meta_ref[...] = jnp.where(row == 0, counts, jnp.where(row == 1, ends - padded, ends))

    tr = lax.broadcasted_iota(jnp.int32, (tm, tm), 0)
    tc = lax.broadcasted_iota(jnp.int32, (tm, tm), 1)
    before = jnp.dot((tc < tr).astype(BF16), onehot.astype(BF16), preferred_element_type=F32)
    pos = base_ref[...] + before
    d1 = jnp.sum(jnp.where(hit1, pos, 0.0), axis=1, keepdims=True)
    d2 = jnp.sum(jnp.where(hit2, pos, 0.0), axis=1, keepdims=True)
    base_ref[...] += jnp.sum(onehot, axis=0, keepdims=True)
    info_ref[...] = jnp.where(lane == 0, d1, jnp.where(lane == 1, d2, sel))


def _route(sel, counts):
    n = sel.shape[0]
    tm = ROUTE_ROWS if n % ROUTE_ROWS == 0 else ROUTE_ROWS // 2
    return pl.pallas_call(
        _route_kernel,
        out_shape=[jax.ShapeDtypeStruct((n, ROUTER_LANES), F32),
                   jax.ShapeDtypeStruct((8, ROUTER_LANES), F32)],
        grid=(n // tm,),
        in_specs=[pl.BlockSpec((tm, ROUTER_LANES), lambda i: (i, 0)),
                  pl.BlockSpec((8, ROUTER_LANES), lambda i: (0, 0))],
        out_specs=[pl.BlockSpec((tm, ROUTER_LANES), lambda i: (i, 0)),
                   pl.BlockSpec((8, ROUTER_LANES), lambda i: (0, 0))],
        scratch_shapes=[pltpu.VMEM((1, ROUTER_LANES), F32)],
        compiler_params=_cparams("arbitrary"),
        name="route",
    )(sel, counts)


def _sc_mesh():
    return plsc.VectorSubcoreMesh(core_axis_name="c", subcore_axis_name="s")


def _sc_gather(table, idx):
    n = idx.shape[0]
    d = table.shape[1]

    @functools.partial(pl.kernel, out_type=jax.ShapeDtypeStruct((n, d), table.dtype), mesh=_sc_mesh())
    def gather(x_hbm, i_hbm, o_hbm):
        def body(i_vmem, o_vmem):
            pltpu.sync_copy(x_hbm.at[i_vmem.at[0]], o_vmem)

        pltpu.emit_pipeline(
            body,
            grid=(n // SC_WINDOW,),
            in_specs=[pl.BlockSpec((1, SC_WINDOW), lambda i: (0, i))],
            out_specs=[pl.BlockSpec((SC_WINDOW, d), lambda i: (i, 0))],
            core_axis_name=("c", "s"),
            dimension_semantics=(pltpu.PARALLEL,),
        )(i_hbm, o_hbm)

    return gather(table, idx.reshape(1, n))


def _sc_scatter2(rows, idx0, idx1, n_out):
    m, d = rows.shape

    @functools.partial(pl.kernel, out_type=jax.ShapeDtypeStruct((n_out, d), rows.dtype), mesh=_sc_mesh())
    def scatter(x_hbm, i0_hbm, i1_hbm, o_hbm):
        def body(x_vmem, i0_vmem, i1_vmem):
            pltpu.sync_copy(x_vmem, o_hbm.at[i0_vmem.at[0]])
            pltpu.sync_copy(x_vmem, o_hbm.at[i1_vmem.at[0]])

        pltpu.emit_pipeline(
            body,
            grid=(m // SC_WINDOW,),
            in_specs=[pl.BlockSpec((SC_WINDOW, d), lambda i: (i, 0)),
                      pl.BlockSpec((1, SC_WINDOW), lambda i: (0, i)),
                      pl.BlockSpec((1, SC_WINDOW), lambda i: (0, i))],
            out_specs=[],
            core_axis_name=("c", "s"),
            dimension_semantics=(pltpu.PARALLEL,),
        )(x_hbm, i0_hbm, i1_hbm)

    return scatter(rows, idx0.reshape(1, m), idx1.reshape(1, m))


def _moe_kernel(be_ref, nu_ref, x_ref, w1_ref, w3_ref, w2_ref, o_ref, w1b, w3b, w2b):
    i = pl.program_id(0)
    e = be_ref[i]
    prev = be_ref[jnp.maximum(i - 1, 0)]

    @pl.when((i == 0) | (e != prev))
    def _():
        w1b[...] = w1_ref[0, 0].astype(BF16)
        w3b[...] = w3_ref[0, 0].astype(BF16)
        w2b[...] = w2_ref[0, 0].astype(BF16)

    @pl.when(i < nu_ref[0])
    def _():
        parts = [h.astype(BF16) for p in range(PLANES) for h in _unpack_rows(x_ref[p])]
        a = sum(jnp.dot(h, w1b[q * SC_ROW:(q + 1) * SC_ROW, :], preferred_element_type=F32)
                for q, h in enumerate(parts))
        b = sum(jnp.dot(h, w3b[q * SC_ROW:(q + 1) * SC_ROW, :], preferred_element_type=F32)
                for q, h in enumerate(parts))
        hmid = (jax.nn.silu(a) * b).astype(BF16)
        _pack_planes(jnp.dot(hmid, w2b[...], preferred_element_type=F32), o_ref)

    @pl.when(i >= nu_ref[0])
    def _():
        o_ref[...] = jnp.zeros_like(o_ref)


def _moe_mlp(blk_expert, n_used, xs, w1, w3, w2, layer):
    n_rows = xs.shape[1]
    d, f = w1.shape[2:]
    nblk = n_rows // MOE_ROWS
    rows = lambda i, be, nu: (0, i, 0)
    return pl.pallas_call(
        _moe_kernel,
        out_shape=jax.ShapeDtypeStruct((PLANES, n_rows, SC_ROW), F32),
        grid_spec=pltpu.PrefetchScalarGridSpec(
            num_scalar_prefetch=2,
            grid=(nblk,),
            in_specs=[pl.BlockSpec((PLANES, MOE_ROWS, SC_ROW), rows),
                      pl.BlockSpec((1, 1, d, f), lambda i, be, nu: (layer, be[i], 0, 0)),
                      pl.BlockSpec((1, 1, d, f), lambda i, be, nu: (layer, be[i], 0, 0)),
                      pl.BlockSpec((1, 1, f, d), lambda i, be, nu: (layer, be[i], 0, 0))],
            out_specs=pl.BlockSpec((PLANES, MOE_ROWS, SC_ROW), rows),
            scratch_shapes=[pltpu.VMEM((d, f), BF16), pltpu.VMEM((d, f), BF16), pltpu.VMEM((f, d), BF16)]),
        compiler_params=_cparams("arbitrary"),
        name="moe",
    )(blk_expert, n_used, xs, w1, w3, w2)


def _combine_kernel(x_ref, pk_ref, info_ref, g_ref, o_ref):
    info = info_ref[...]
    g1 = info[:, 2:3]
    g2 = info[:, 3:4]
    parts = []
    for p in range(PLANES):
        hi1, lo1 = _unpack_rows(pk_ref[TOP_K * p])
        hi2, lo2 = _unpack_rows(pk_ref[TOP_K * p + 1])
        parts += [g1 * hi1 + g2 * hi2, g1 * lo1 + g2 * lo2]
    o_ref[0] = x_ref[0] + g_ref[0] * jnp.concatenate(parts, axis=1)


def _combine(x, picked, info, gate, row0):
    b, t, d = x.shape
    tm = min(256, t)
    off = row0 // tm
    tok = lambda bb, i: (off + bb * (t // tm) + i, 0)
    tok3 = lambda bb, i: (0, off + bb * (t // tm) + i, 0)
    return pl.pallas_call(
        _combine_kernel,
        out_shape=jax.ShapeDtypeStruct((b, t, d), F32),
        grid=(b, t // tm),
        in_specs=[pl.BlockSpec((1, tm, d), lambda bb, i: (bb, i, 0)),
                  pl.BlockSpec((PLANES * TOP_K, tm, SC_ROW), tok3),
                  pl.BlockSpec((tm, ROUTER_LANES), tok),
                  pl.BlockSpec((1, 1, d), lambda bb, i: (bb, 0, 0))],
        out_specs=pl.BlockSpec((1, tm, d), lambda bb, i: (bb, i, 0)),
        compiler_params=_cparams("arbitrary", "arbitrary"),
        name="combine",
    )(x, picked, info, gate)


def _hier_moe(h2p, sel, counts, w1, w3, w2, layer):
    n_tok = h2p.shape[1]
    info, meta = _route(sel, counts)
    dest = info[:, 0:TOP_K].astype(jnp.int32)
    pad_ends = meta[2, N_GROUPS:N_GROUPS + N_EXPERTS].astype(jnp.int32)
    nblk = -(-(n_tok * TOP_K) // MOE_ROWS) + N_EXPERTS
    n_rows = nblk * MOE_ROWS
    blk_start = jnp.arange(nblk, dtype=jnp.int32) * MOE_ROWS
    blk_expert = jnp.minimum(jnp.sum((pad_ends[None, :] <= blk_start[:, None]).astype(jnp.int32), axis=1),
                             N_EXPERTS - 1)
    n_used = pad_ends[-1:] // MOE_ROWS
    slot = [jnp.concatenate([p * n_rows + dest[:, s] for p in range(PLANES)]) for s in range(TOP_K)]
    xs = _sc_scatter2(h2p.reshape(PLANES * n_tok, SC_ROW), slot[0], slot[1], PLANES * n_rows)
    out = _moe_mlp(blk_expert, n_used, xs.reshape(PLANES, n_rows, SC_ROW), w1, w3, w2, layer)
    idx_all = jnp.concatenate([p * n_rows + dest[:, s] for p in range(PLANES) for s in range(TOP_K)])
    picked = _sc_gather(out.reshape(PLANES * n_rows, SC_ROW), idx_all)
    return picked.reshape(PLANES * TOP_K, n_tok, SC_ROW), info


def _layer(layer, x, xc, c, c_ctx, lb, w_mod, b_mod, norm1_w, w_in, w_s, b_s, q_norm_w, k_norm_w, hgrn_norm_w, w_out,
           norm2_w, w_grp, b_grp, w_exp, b_exp, w1, w3, w2, ctx_out):
    b, t, d = x.shape
    lc = xc.shape[1]
    cc = jnp.zeros((MOD_ROWS, d), F32).at[:b].set(c).at[b].set(c_ctx)
    mod = _mod(cc, w_mod, b_mod, layer)
    sh1, sc1, g1, sh2, sc2, g2 = [m[:, None, :] for m in jnp.split(mod[:b], 6, axis=-1)]
    mod_c = [jnp.broadcast_to(m[None, None, :], (b, 1, d)) for m in jnp.split(mod[b], 6)]

    w_in_bf = w_in.astype(BF16)
    scale = LOG2E * HEAD_DIM ** -0.5
    q_tabs = _rope_tables(t, q_norm_w, scale, B_WIDTH, True)
    k_tabs = _rope_tables(t, k_norm_w, 1.0, B_KV_WIDTH, True)
    qc_tabs = _rope_tables(lc, q_norm_w, scale, B_WIDTH, False)
    kc_tabs = _rope_tables(lc, k_norm_w, 1.0, B_KV_WIDTH, False)
    ws_bf = w_s.astype(BF16)
    bias2d = jnp.repeat(b_s.T, HEAD_DIM, axis=1)
    ya, q, k, v, zc, g = _inproj(x, norm1_w * (1.0 + sc1), sh1, w_in_bf, q_tabs, k_tabs, ws_bf, bias2d)
    ya_c, q_c, k_c, v_c, zc_c, g_c = _inproj(xc, norm1_w * (1.0 + mod_c[1]), mod_c[0], w_in_bf,
                                             qc_tabs, kc_tabs, ws_bf, bias2d)

    bound = LOG2E * HEAD_DIM ** 0.5 * jnp.max(jnp.abs(q_norm_w)) * jnp.max(jnp.abs(k_norm_w)) * 1.02
    shift = jnp.ceil(bound)
    attn_flag = jnp.stack([(shift <= SAFE_SHIFT).astype(jnp.int32), shift.astype(jnp.int32)])
    yb = _attn(attn_flag, q, [(k, v), (k_c, v_c)])

    pos = lb > 0.0
    log_lb = jnp.log(jnp.where(pos, lb, 1.0))
    lbc = jnp.stack([1.0 - lb, jnp.log1p(-lb), log_lb, pos.astype(F32), jnp.where(pos, -log_lb, 1e30)], axis=1)
    lbc = jnp.concatenate([lbc, jnp.zeros((2, 3, C_WIDTH), F32)], axis=1)
    o, o_c = _hgrn(zc, zc_c, lbc, ctx_out)

    w_out_bf = w_out.astype(BF16)
    hw = jnp.tile(hgrn_norm_w, C_HEADS)[None, :]
    wr = jnp.zeros((d, ROUTER_LANES), F32).at[:, :N_GROUPS].set(w_grp).at[
        :, N_GROUPS:N_GROUPS + N_EXPERTS].set(w_exp)
    wr_hi = wr.astype(BF16)
    wr = jnp.concatenate([wr_hi, (wr - wr_hi.astype(F32)).astype(BF16)], axis=1)
    br = jnp.zeros((1, ROUTER_LANES), F32).at[0, :N_GROUPS].set(b_grp).at[
        0, N_GROUPS:N_GROUPS + N_EXPERTS].set(b_exp)
    x, h2, sel, cnt = _outproj(x, ya, yb, o, g, w_out_bf, g1, norm2_w * (1.0 + sc2), sh2, hw, wr, br)
    if ctx_out:
        yb_c = _attn(attn_flag, q_c, [(k_c, v_c)])
        xc, h2c, sel_c, cnt_c = _outproj(xc, ya_c, yb_c, o_c, g_c, w_out_bf, mod_c[2],
                                norm2_w * (1.0 + mod_c[4]), mod_c[3], hw, wr, br)
        tokens = jnp.concatenate([h2.reshape(PLANES, -1, SC_ROW), h2c.reshape(PLANES, -1, SC_ROW)], axis=1)
        sel_all = jnp.concatenate([sel.reshape(-1, ROUTER_LANES), sel_c.reshape(-1, ROUTER_LANES)], axis=0)
        picked, info = _hier_moe(tokens, sel_all, cnt + cnt_c, w1, w3, w2, layer)
        x = _combine(x, picked, info, g2, 0)
        xc = _combine(xc, picked, info, mod_c[5], b * t)
    else:
        picked, info = _hier_moe(h2.reshape(PLANES, -1, SC_ROW), sel.reshape(-1, ROUTER_LANES), cnt, w1, w3, w2, layer)
        x = _combine(x, picked, info, g2, 0)
    return x, xc


def kernel(x, c, ctx, c_ctx, w_mod, b_mod, norm1_w, w_in, w_s, b_s, q_norm_w, k_norm_w, hgrn_lb_logits,
           hgrn_norm_w, w_out, norm2_w, w_grp, b_grp, w_exp, b_exp, w1, w3, w2):
    depth = w_mod.shape[0]
    lb_sm = jax.nn.softmax(hgrn_lb_logits.astype(F32), axis=0)
    lb = jnp.cumsum(lb_sm, axis=0) - lb_sm[0]
    xc = ctx
    for l in range(depth):
        x, xc = _layer(l, x, xc, c, c_ctx, lb[l], w_mod, b_mod[l], norm1_w[l], w_in[l], w_s[l], b_s[l],
                       q_norm_w[l], k_norm_w[l], hgrn_norm_w[l], w_out[l], norm2_w[l], w_grp[l], b_grp[l],
                       w_exp[l], b_exp[l], w1, w3, w2, ctx_out=(l < depth - 1))
    return x
```

```python
import functools

import jax
import jax.numpy as jnp
from jax import lax
from jax.experimental import pallas as pl
from jax.experimental.pallas import tpu as pltpu
from jax.experimental.pallas import tpu_sc as plsc

F32 = jnp.float32
BF16 = jnp.bfloat16

D_MODEL = 1024
HEAD_DIM = 64
GRID_W = 64
EPS = 1e-6
ROPE_BASE = 10000.0
A_WIDTH = D_MODEL // 4
A_HEADS = A_WIDTH // HEAD_DIM
A_CHUNK = 128
B_WIDTH = D_MODEL // 2
B_HEADS = B_WIDTH // HEAD_DIM
B_KV_HEADS = 2
B_GROUP = B_HEADS // B_KV_HEADS
B_KV_WIDTH = B_KV_HEADS * HEAD_DIM
C_WIDTH = D_MODEL // 4
C_HEADS = C_WIDTH // HEAD_DIM
OFF_B = 2 * A_WIDTH
OFF_KV = OFF_B + B_WIDTH
OFF_V = OFF_KV + B_KV_WIDTH
OFF_C = OFF_KV + 2 * B_KV_WIDTH
OFF_G = OFF_C + 4 * C_WIDTH
IN_WIDTH = OFF_G + C_WIDTH
N_GROUPS = 4
EXPERTS_PER_GROUP = 8
N_EXPERTS = N_GROUPS * EXPERTS_PER_GROUP
TOP_K = 2
D_FF_EXPERT = D_MODEL // 2

MOD_ROWS = 16
ROUTER_LANES = 128
PROJ_ROWS = 512
HGRN_BLOCK = 32
HGRN_GROUP = 4
LOG2E = 1.4426950408889634
ATTN_ROWS = 256
ATTN_KEYS = 512
SAFE_SHIFT = 60
SAFE_DECAY = 80.0
MOE_ROWS = 512
ROUTE_ROWS = 512
SC_WINDOW = 128
SC_ROW = 256
PLANES = D_MODEL // (2 * SC_ROW)
VMEM_LIMIT = 48 * 1024 * 1024


def _cparams(*sem):
    return pltpu.CompilerParams(dimension_semantics=sem, vmem_limit_bytes=VMEM_LIMIT)


def _head_ones(n, dtype):
    r = lax.broadcasted_iota(jnp.int32, (n, n), 0) >> 6
    c = lax.broadcasted_iota(jnp.int32, (n, n), 1) >> 6
    return (r == c).astype(dtype)


def _head_sum(x, ones_bd):
    return jnp.dot(x.astype(BF16), ones_bd, preferred_element_type=F32)


def _head_rms(x, ones_bd):
    return x * lax.rsqrt(_head_sum(x * x, ones_bd) * (1.0 / HEAD_DIM) + EPS)


def _pack_rows(y):
    bits = lax.bitcast_convert_type(y.astype(BF16).astype(F32), jnp.uint32)
    half = y.shape[1] // 2
    return lax.bitcast_convert_type(bits[:, :half] | (bits[:, half:] >> 16), F32)


def _unpack_rows(w):
    bits = lax.bitcast_convert_type(w, jnp.uint32)
    hi = lax.bitcast_convert_type(bits & jnp.uint32(0xFFFF0000), F32)
    lo = lax.bitcast_convert_type(bits << 16, F32)
    return hi, lo


def _pack_planes(y, ref, lead=()):
    for p in range(PLANES):
        ref[(p,) + lead] = _pack_rows(y[:, 2 * p * SC_ROW:(2 * p + 2) * SC_ROW])


def _mod_kernel(c_ref, w_ref, b_ref, o_ref):
    a = jax.nn.silu(c_ref[...])
    w = w_ref[0]
    a_hi, w_hi = a.astype(BF16), w.astype(BF16)
    a_lo, w_lo = (a - a_hi.astype(F32)).astype(BF16), (w - w_hi.astype(F32)).astype(BF16)
    o_ref[...] = (jnp.dot(a_hi, w_hi, preferred_element_type=F32) + jnp.dot(a_hi, w_lo, preferred_element_type=F32)
                  + jnp.dot(a_lo, w_hi, preferred_element_type=F32) + b_ref[...])


def _mod(cc, w_mod, b_mod, layer):
    n = w_mod.shape[2]
    tn = 1536
    return pl.pallas_call(
        _mod_kernel,
        out_shape=jax.ShapeDtypeStruct((MOD_ROWS, n), F32),
        grid=(n // tn,),
        in_specs=[pl.BlockSpec((MOD_ROWS, D_MODEL), lambda j: (0, 0)),
                  pl.BlockSpec((1, D_MODEL, tn), lambda j: (layer, 0, j)),
                  pl.BlockSpec((1, tn), lambda j: (0, j))],
        out_specs=pl.BlockSpec((MOD_ROWS, tn), lambda j: (0, j)),
        compiler_params=_cparams("arbitrary"),
        name="mod",
    )(cc, w_mod, b_mod.reshape(1, n))


def _rope(xn, c_ref, sp_ref, sm_ref):
    w = xn.shape[-1]
    return (xn * c_ref[...] + pltpu.roll(xn, 16, 1) * sp_ref[...]
            + pltpu.roll(xn, w - 16, 1) * sm_ref[...])


def _combined(x_ref, pk_ref, info_ref, g_ref):
    info = info_ref[...]
    g1 = info[:, 2:3]
    g2 = info[:, 3:4]
    parts = []
    for p in range(PLANES):
        hi1, lo1 = _unpack_rows(pk_ref[TOP_K * p])
        hi2, lo2 = _unpack_rows(pk_ref[TOP_K * p + 1])
        parts += [g1 * hi1 + g2 * hi2, g1 * lo1 + g2 * lo2]
    return x_ref[0] + g_ref[0] * jnp.concatenate(parts, axis=1)


def _gmlp(z, ws_ref, bias_ref):
    gz = jax.nn.gelu(z)
    u = gz[:, :A_WIDTH]
    vn = _head_rms(gz[:, A_WIDTH:], _head_ones(A_WIDTH, BF16))
    lane_head = lax.broadcasted_iota(jnp.int32, vn.shape, 1) >> 6
    acc = bias_ref[...]
    for hh in range(A_HEADS):
        vh = jnp.where(lane_head == hh, vn, 0.0).astype(BF16)
        acc = acc + jnp.dot(ws_ref[hh], vh, preferred_element_type=F32)
    return u * acc


def _inproj_kernel(*refs, pending):
    if pending:
        (x_ref, pk_ref, info_ref, g2_ref, mul_ref, add_ref, w_ref, qc_ref, qsp_ref, qsm_ref, kc_ref, ksp_ref, ksm_ref,
         ws_ref, bias_ref, xo_ref, ya_ref, q_ref, k_ref, v_ref, zc_ref, g_ref) = refs
        x = _combined(x_ref, pk_ref, info_ref, g2_ref)
        xo_ref[0] = x
    else:
        (x_ref, mul_ref, add_ref, w_ref, qc_ref, qsp_ref, qsm_ref, kc_ref, ksp_ref, ksm_ref,
         ws_ref, bias_ref, ya_ref, q_ref, k_ref, v_ref, zc_ref, g_ref) = refs
        x = x_ref[0]
    ms = jnp.mean(x * x, axis=-1, keepdims=True)
    h = x * lax.rsqrt(ms + EPS) * mul_ref[0] + add_ref[0]
    y = jnp.dot(h.astype(BF16), w_ref[...], preferred_element_type=F32)
    for c0 in range(0, x.shape[0], A_CHUNK):
        ya_ref[0, c0:c0 + A_CHUNK, :] = _gmlp(y[c0:c0 + A_CHUNK, :OFF_B], ws_ref, bias_ref).astype(BF16)
    qn = _head_rms(y[:, OFF_B:OFF_KV], _head_ones(B_WIDTH, BF16))
    q_ref[0] = _rope(qn, qc_ref, qsp_ref, qsm_ref).astype(BF16)
    kn = _head_rms(y[:, OFF_KV:OFF_V], _head_ones(B_KV_WIDTH, BF16))
    k_ref[0] = _rope(kn, kc_ref, ksp_ref, ksm_ref).astype(BF16)
    v_ref[0] = y[:, OFF_V:OFF_C].astype(BF16)
    zc_ref[0] = y[:, OFF_C:OFF_G]
    g_ref[0] = y[:, OFF_G:]


def _inproj(x, mul, add, w_bf, q_tabs, k_tabs, ws_bf, bias2d, pending=None):
    b, t, d = x.shape
    tm = min(PROJ_ROWS, t)
    row = lambda i, bb: (bb, i, 0)
    vec = lambda i, bb: (bb, 0, 0)
    tab = lambda i, bb: (i, 0)
    widths = (A_WIDTH, B_WIDTH, B_KV_WIDTH, B_KV_WIDTH, 4 * C_WIDTH, C_WIDTH)
    dtypes = (BF16, BF16, BF16, BF16, F32, F32)
    pre_specs, pre_args = [], []
    if pending is not None:
        picked, info, gate, row0 = pending
        off = row0 // tm
        widths, dtypes = (d,) + widths, (F32,) + dtypes
        pre_specs = [pl.BlockSpec((PLANES * TOP_K, tm, SC_ROW), lambda i, bb: (0, off + bb * (t // tm) + i, 0)),
                     pl.BlockSpec((tm, ROUTER_LANES), lambda i, bb: (off + bb * (t // tm) + i, 0)),
                     pl.BlockSpec((1, 1, d), vec)]
        pre_args = [picked, info, gate]
    return pl.pallas_call(
        functools.partial(_inproj_kernel, pending=pending is not None),
        out_shape=[jax.ShapeDtypeStruct((b, t, w), dt) for w, dt in zip(widths, dtypes)],
        grid=(t // tm, b),
        in_specs=[pl.BlockSpec((1, tm, d), row)] + pre_specs
                 + [pl.BlockSpec((1, 1, d), vec),
                    pl.BlockSpec((1, 1, d), vec),
                    pl.BlockSpec((d, IN_WIDTH), lambda i, bb: (0, 0))]
                 + [pl.BlockSpec((tm, B_WIDTH), tab)] * 3
                 + [pl.BlockSpec((tm, B_KV_WIDTH), tab)] * 3
                 + [pl.BlockSpec((A_HEADS, A_CHUNK, A_CHUNK), lambda i, bb: (0, 0, 0)),
                    pl.BlockSpec((A_CHUNK, A_WIDTH), lambda i, bb: (0, 0))],
        out_specs=[pl.BlockSpec((1, tm, w), row) for w in widths],
        compiler_params=_cparams("arbitrary", "arbitrary"),
        name="inproj",
    )(x, *pre_args, mul, add, w_bf, *q_tabs, *k_tabs, ws_bf, bias2d)


def _rope_tables(t, w, scale, width, rotate):
    ws = w.astype(F32) * scale
    if not rotate:
        c = jnp.broadcast_to(jnp.tile(ws, width // HEAD_DIM)[None, :], (t, width))
        z = jnp.zeros((t, width), F32)
        return c, z, z
    pos = jnp.arange(t)
    row = (pos // GRID_W).astype(F32)
    col = (pos % GRID_W).astype(F32)
    inv_freq = 1.0 / (ROPE_BASE ** (jnp.arange(0, HEAD_DIM // 2, 2, dtype=F32) / (HEAD_DIM // 2)))
    dd = jnp.arange(HEAD_DIM)
    axis = dd // 32
    half = (dd % 32) // 16
    ang = jnp.where(axis[None, :] == 0, row[:, None], col[:, None]) * inv_freq[dd % 16][None, :]
    cos, sin = jnp.cos(ang), jnp.sin(ang)
    c = cos * ws[None, :]
    sm = jnp.where(half[None, :] == 0, -sin * jnp.roll(ws, -16)[None, :], 0.0)
    sp = jnp.where(half[None, :] == 1, sin * jnp.roll(ws, 16)[None, :], 0.0)
    rep = width // HEAD_DIM
    return jnp.tile(c, (1, rep)), jnp.tile(sp, (1, rep)), jnp.tile(sm, (1, rep))


def _attn_kernel(flag_ref, q_ref, *refs, n_seg):
    kv_refs, o_ref = refs[:2 * n_seg], refs[2 * n_seg]
    tq = q_ref.shape[1]
    dh = HEAD_DIM

    def heads(j):
        q4 = jnp.concatenate([q_ref[0, :, (B_GROUP * j + gg) * dh:(B_GROUP * j + gg + 1) * dh]
                              for gg in range(B_GROUP)], axis=0)
        ks, vs = [], []
        for sg in range(n_seg):
            s_len = kv_refs[2 * sg].shape[1]
            for c0 in range(0, s_len, ATTN_KEYS):
                c1 = min(c0 + ATTN_KEYS, s_len)
                ks.append(kv_refs[2 * sg][0, c0:c1, j * dh:(j + 1) * dh])
                vs.append(kv_refs[2 * sg + 1][0, c0:c1, j * dh:(j + 1) * dh])
        return q4, ks, vs

    def scores(q4, ks):
        return [lax.dot_general(kk, q4, (((1,), (1,)), ((), ())), preferred_element_type=F32) for kk in ks]

    def finish(j, ps, vs):
        l = functools.reduce(jnp.add, [jnp.sum(p, axis=0, keepdims=True) for p in ps])
        acc = sum(lax.dot_general(vv, p.astype(BF16), (((0,), (0,)), ((), ())), preferred_element_type=F32)
                  for p, vv in zip(ps, vs))
        o = (acc / l).T
        for gg in range(B_GROUP):
            hh = B_GROUP * j + gg
            o_ref[0, :, hh * dh:(hh + 1) * dh] = o[gg * tq:(gg + 1) * tq].astype(BF16)

    @pl.when(flag_ref[0] > 0)
    def _():
        shift = flag_ref[1].astype(F32)
        for j in range(B_KV_HEADS):
            q4, ks, vs = heads(j)
            finish(j, [jnp.exp2(s - shift) for s in scores(q4, ks)], vs)

    @pl.when(flag_ref[0] <= 0)
    def _():
        for j in range(B_KV_HEADS):
            q4, ks, vs = heads(j)
            ss = scores(q4, ks)
            m = functools.reduce(jnp.maximum, [jnp.max(s, axis=0, keepdims=True) for s in ss])
            finish(j, [jnp.exp2(s - m) for s in ss], vs)


def _attn(flag, q, kv_segs):
    b, t, w = q.shape
    tq = min(ATTN_ROWS, t)
    n_seg = len(kv_segs)
    kv_flat, kv_specs = [], []
    for kk, vv in kv_segs:
        s_len, kw = kk.shape[1:]
        kv_flat += [kk, vv]
        kv_specs += [pl.BlockSpec((1, s_len, kw), lambda bb, i, fl: (bb, 0, 0))] * 2
    return pl.pallas_call(
        functools.partial(_attn_kernel, n_seg=n_seg),
        out_shape=jax.ShapeDtypeStruct((b, t, w), BF16),
        grid_spec=pltpu.PrefetchScalarGridSpec(
            num_scalar_prefetch=1,
            grid=(b, t // tq),
            in_specs=[pl.BlockSpec((1, tq, w), lambda bb, i, fl: (bb, i, 0))] + kv_specs,
            out_specs=pl.BlockSpec((1, tq, w), lambda bb, i, fl: (bb, i, 0))),
        compiler_params=_cparams("arbitrary", "arbitrary"),
        name="attn",
    )(flag, q, *kv_flat)


def _scan_rows(x, reverse):
    n = x.shape[0]
    rows = lax.broadcasted_iota(jnp.int32, x.shape, 0)
    sh = 1
    while sh < n:
        if reverse:
            x = x + jnp.where(rows < n - sh, pltpu.roll(x, n - sh, 0), 0.0)
        else:
            x = x + jnp.where(rows >= sh, pltpu.roll(x, sh, 0), 0.0)
        sh *= 2
    return x


def _stack_heads(x, lane_head):
    return jnp.concatenate([jnp.where(lane_head == hh, x, 0.0) for hh in range(C_HEADS)], axis=0)


def _hgrn_kernel(z_ref, zc_ref, lbc_ref, *refs, ctx_out):
    if ctx_out:
        o_ref, oc_ref, st_ref, kx_ref, bx_ref, vx_ref, flag_ref = refs
    else:
        o_ref, st_ref, kx_ref, bx_ref, vx_ref, flag_ref = refs
        oc_ref = None
    n = C_WIDTH
    nb = HGRN_BLOCK
    nblk_c = zc_ref.shape[1] // nb
    nblk_l = z_ref.shape[1] // nb
    ones_bd = _head_ones(n, BF16)
    rows = lax.broadcasted_iota(jnp.int32, (nb, n), 0)
    lane_head = lax.broadcasted_iota(jnp.int32, (nb, n), 1) >> 6
    lane_head64 = lax.broadcasted_iota(jnp.int32, (HEAD_DIM, n), 1) >> 6
    low_half = (lax.broadcasted_iota(jnp.int32, (HEAD_DIM, 2 * HEAD_DIM), 1) < HEAD_DIM)
    sc_t = lax.broadcasted_iota(jnp.int32, (nb, C_HEADS * nb), 0)
    sc_s = lax.broadcasted_iota(jnp.int32, (nb, C_HEADS * nb), 1) & (nb - 1)

    def gates(z, d):
        one_m_lb = lbc_ref[d, 0:1, :]
        log1m_lb = lbc_ref[d, 1:2, :]
        log_lb = lbc_ref[d, 2:3, :]
        lb_pos = lbc_ref[d, 3:4, :] > 0.5
        soft = jnp.log(1.0 + jnp.exp(-jnp.abs(z)))
        log_rest = log1m_lb + (jnp.minimum(z, 0.0) - soft)
        lse = jnp.maximum(log_lb, log_rest) + jnp.log(1.0 + jnp.exp(-jnp.abs(log_lb - log_rest)))
        return jnp.where(lb_pos, lse, log_rest), one_m_lb * jnp.exp(jnp.minimum(-z, 0.0) - soft)

    def group_blocks(nblk, i):
        grp = HGRN_GROUP if nblk % HGRN_GROUP == 0 else 1
        fwd = [i * grp + gg for gg in range(grp)]
        return grp, fwd, [nblk - 1 - blk for blk in fwd]

    def flag_groups(src_ref, nblk, base):
        def body(i, carry):
            _, fwd, bwd = group_blocks(nblk, i)
            worst = None
            for d, blks in ((0, fwd), (1, bwd)):
                for blk in blks:
                    zz = src_ref[0, pl.ds(pl.multiple_of(blk * nb, nb), nb), (1 + d) * n:(2 + d) * n]
                    step_bound = jnp.minimum(lbc_ref[d, 4:5, :],
                                             jnp.maximum(-zz, 0.0) + (jnp.log(2.0) - lbc_ref[d, 1:2, :]))
                    bound = jnp.sum(step_bound, axis=0, keepdims=True)
                    worst = bound if worst is None else jnp.maximum(worst, bound)
            flag_ref[base + i] = (jnp.max(worst) <= SAFE_DECAY).astype(jnp.int32)
            return carry
        lax.fori_loop(0, nblk // group_blocks(nblk, 0)[0], body, 0)

    n_grp_c = nblk_c // group_blocks(nblk_c, 0)[0]
    flag_groups(zc_ref, nblk_c, 0)
    flag_groups(z_ref, nblk_l, n_grp_c)
    st_ref[...] = jnp.zeros_like(st_ref)
    o_ref[...] = jnp.zeros_like(o_ref)
    if ctx_out:
        oc_ref[...] = jnp.zeros_like(oc_ref)

    def step(src_ref, dst_ref, blk, d, fast):
        reverse = d == 1
        r0 = pl.multiple_of(blk * nb, nb)
        v = src_ref[0, pl.ds(r0, nb), 3 * n:4 * n]
        log_f, k = gates(src_ref[0, pl.ds(r0, nb), (1 + d) * n:(2 + d) * n], d)
        bc = _scan_rows(log_f, reverse)
        edge = 0 if reverse else nb - 1
        b_edge = bc[edge:edge + 1, :]
        st = st_ref[d]
        v_bf = v.astype(BF16)

        if dst_ref is not None:
            q = jax.nn.silu(src_ref[0, pl.ds(r0, nb), 0:n])
            qt = (q * jnp.exp(bc)).astype(BF16)
            o = lax.dot_general(qt, _stack_heads(st, lane_head64).astype(BF16), (((1,), (1,)), ((), ())),
                                preferred_element_type=F32)

            def intra_fast():
                kt = _stack_heads(k * jnp.exp(-bc), lane_head).astype(BF16)
                sc = lax.dot_general(qt, kt, (((1,), (1,)), ((), ())), preferred_element_type=F32)
                keep = (sc_s >= sc_t) if reverse else (sc_s <= sc_t)
                sc = jnp.where(keep, sc, 0.0).astype(BF16)
                return jnp.dot(sc, _stack_heads(v, lane_head).astype(BF16), preferred_element_type=F32)

            def intra_exact():
                kx_ref[d] = k
                bx_ref[d] = bc
                vx_ref[d] = v

                def sbody(s, acc):
                    keep = (rows <= s) if reverse else (rows >= s)
                    e = jnp.exp(jnp.where(keep, bc - bx_ref[d, pl.ds(s, 1), :], 0.0))
                    p = jnp.where(keep, q * e * kx_ref[d, pl.ds(s, 1), :], 0.0)
                    sc = jnp.dot(p.astype(BF16), ones_bd, preferred_element_type=F32)
                    return acc + sc * vx_ref[d, pl.ds(s, 1), :]

                return lax.fori_loop(0, nb, sbody, jnp.zeros((nb, n), F32))

            o = o + (intra_fast() if fast else intra_exact())
            dst_ref[0, pl.ds(r0, nb), :] += o

        kd = (k * jnp.exp(b_edge - bc)).astype(BF16)
        full = lax.dot_general(v_bf, kd, (((0,), (0,)), ((), ())), preferred_element_type=F32)
        upd = jnp.concatenate(
            [jnp.where(low_half,
                       full[(2 * c) * HEAD_DIM:(2 * c + 1) * HEAD_DIM, 2 * c * HEAD_DIM:(2 * c + 2) * HEAD_DIM],
                       full[(2 * c + 1) * HEAD_DIM:(2 * c + 2) * HEAD_DIM, 2 * c * HEAD_DIM:(2 * c + 2) * HEAD_DIM])
             for c in range(C_HEADS // 2)], axis=1)
        st_ref[d] = st * jnp.exp(b_edge) + upd

    def run(src_ref, dst_ref, nblk, base):
        def body(i, carry):
            _, fwd, bwd = group_blocks(nblk, i)
            safe = flag_ref[base + i]

            def group(fast):
                for bf, bb in zip(fwd, bwd):
                    step(src_ref, dst_ref, bf, 0, fast)
                    step(src_ref, dst_ref, bb, 1, fast)

            pl.when(safe > 0)(functools.partial(group, True))
            pl.when(safe <= 0)(functools.partial(group, False))
            return carry
        lax.fori_loop(0, nblk // group_blocks(nblk, 0)[0], body, 0)

    run(zc_ref, oc_ref, nblk_c, 0)
    run(z_ref, o_ref, nblk_l, n_grp_c)


def _hgrn(zc, zc_c, lbc, ctx_out):
    b, t, w = zc.shape
    lc = zc_c.shape[1]
    n = C_WIDTH
    row = lambda bb: (bb, 0, 0)
    out_shape = [jax.ShapeDtypeStruct((b, t, n), F32)]
    out_specs = [pl.BlockSpec((1, t, n), row)]
    if ctx_out:
        out_shape.append(jax.ShapeDtypeStruct((b, lc, n), F32))
        out_specs.append(pl.BlockSpec((1, lc, n), row))
    res = pl.pallas_call(
        functools.partial(_hgrn_kernel, ctx_out=ctx_out),
        out_shape=out_shape,
        grid=(b,),
        in_specs=[pl.BlockSpec((1, t, w), row),
                  pl.BlockSpec((1, lc, w), row),
                  pl.BlockSpec((2, 8, n), lambda bb: (0, 0, 0))],
        out_specs=out_specs,
        scratch_shapes=[pltpu.VMEM((2, HEAD_DIM, n), F32)]
                       + [pltpu.VMEM((2, HGRN_BLOCK, n), F32)] * 3
                       + [pltpu.SMEM(((t + lc) // HGRN_BLOCK,), jnp.int32)],
        compiler_params=_cparams("arbitrary"),
        name="hgrn",
    )(zc, zc_c, lbc)
    return (res[0], res[1]) if ctx_out else (res[0], None)


def _select_experts(lg):
    lane = lax.broadcasted_iota(jnp.int32, lg.shape, 1)
    lane_f = lane.astype(F32)
    neg = -jnp.inf
    gl = jnp.where(lane < N_GROUPS, lg, neg)
    gmax = jnp.max(gl, axis=1, keepdims=True)
    grp = jnp.min(jnp.where(gl == gmax, lane_f, float(ROUTER_LANES)), axis=1, keepdims=True).astype(jnp.int32)
    p_grp = 1.0 / jnp.sum(jnp.exp(gl - gmax), axis=1, keepdims=True)
    in_grp = (lane >= N_GROUPS) & (lane < N_GROUPS + N_EXPERTS) & (((lane - N_GROUPS) >> 3) == grp)
    el = jnp.where(in_grp, lg, neg)
    v1 = jnp.max(el, axis=1, keepdims=True)
    i1 = jnp.min(jnp.where(el == v1, lane_f, float(ROUTER_LANES)), axis=1, keepdims=True)
    el2 = jnp.where(lane_f == i1, neg, el)
    v2 = jnp.max(el2, axis=1, keepdims=True)
    i2 = jnp.min(jnp.where(el2 == v2, lane_f, float(ROUTER_LANES)), axis=1, keepdims=True)
    rr = jnp.exp(v2 - v1)
    g1 = p_grp / (1.0 + rr)
    g2 = p_grp * rr / (1.0 + rr)
    sel = jnp.where(lane == 0, i1, jnp.where(lane == 1, i2, jnp.where(lane == 2, g1, jnp.where(lane == 3, g2, 0.0))))
    counts = jnp.sum(((lane_f == i1) | (lane_f == i2)).astype(F32), axis=0, keepdims=True)
    return sel, counts


def _outproj_kernel(x_ref, ya_ref, yb_ref, o_ref, g_ref, w_ref, gate_ref, mul_ref, add_ref, hw_ref,
                    wr_ref, br_ref, xo_ref, h2_ref, sel_ref, cnt_ref):
    yc = _head_rms(o_ref[0], _head_ones(C_WIDTH, BF16)) * hw_ref[...] * jax.nn.silu(g_ref[0])
    y = jnp.dot(ya_ref[0], w_ref[0:A_WIDTH, :], preferred_element_type=F32)
    y = y + jnp.dot(yb_ref[0], w_ref[A_WIDTH:A_WIDTH + B_WIDTH, :], preferred_element_type=F32)
    y = y + jnp.dot(yc.astype(BF16), w_ref[A_WIDTH + B_WIDTH:, :], preferred_element_type=F32)
    xn = x_ref[0] + gate_ref[0] * y
    xo_ref[0] = xn
    ms = jnp.mean(xn * xn, axis=-1, keepdims=True)
    h2 = xn * lax.rsqrt(ms + EPS) * mul_ref[0] + add_ref[0]
    _pack_planes(h2, h2_ref, (0,))
    h_hi = h2.astype(BF16)
    h_lo = (h2 - h_hi.astype(F32)).astype(BF16)
    both = jnp.dot(h_hi, wr_ref[...], preferred_element_type=F32)
    lg = (both[:, :ROUTER_LANES] + both[:, ROUTER_LANES:] + br_ref[...]
          + jnp.dot(h_lo, wr_ref[:, 0:ROUTER_LANES], preferred_element_type=F32))
    sel, counts = _select_experts(lg)
    sel_ref[0] = sel

    @pl.when((pl.program_id(0) == 0) & (pl.program_id(1) == 0))
    def _():
        cnt_ref[...] = jnp.zeros_like(cnt_ref)
    cnt_ref[...] += counts


def _outproj(x, ya, yb, o, g, w_bf, gate, mul, add, hw, wr, br):
    b, t, d = x.shape
    tm = min(PROJ_ROWS, t)
    row = lambda bb, i: (bb, i, 0)
    vec = lambda bb, i: (bb, 0, 0)
    const = lambda bb, i: (0, 0)
    return pl.pallas_call(
        _outproj_kernel,
        out_shape=[jax.ShapeDtypeStruct((b, t, d), F32),
                   jax.ShapeDtypeStruct((PLANES, b, t, SC_ROW), F32),
                   jax.ShapeDtypeStruct((b, t, ROUTER_LANES), F32),
                   jax.ShapeDtypeStruct((8, ROUTER_LANES), F32)],
        grid=(b, t // tm),
        in_specs=[pl.BlockSpec((1, tm, d), row),
                  pl.BlockSpec((1, tm, A_WIDTH), row),
                  pl.BlockSpec((1, tm, B_WIDTH), row),
                  pl.BlockSpec((1, tm, C_WIDTH), row),
                  pl.BlockSpec((1, tm, C_WIDTH), row),
                  pl.BlockSpec((d, d), const),
                  pl.BlockSpec((1, 1, d), vec),
                  pl.BlockSpec((1, 1, d), vec),
                  pl.BlockSpec((1, 1, d), vec),
                  pl.BlockSpec((1, C_WIDTH), const),
                  pl.BlockSpec((d, 2 * ROUTER_LANES), const),
                  pl.BlockSpec((1, ROUTER_LANES), const)],
        out_specs=[pl.BlockSpec((1, tm, d), row),
                   pl.BlockSpec((PLANES, 1, tm, SC_ROW), lambda bb, i: (0, bb, i, 0)),
                   pl.BlockSpec((1, tm, ROUTER_LANES), row),
                   pl.BlockSpec((8, ROUTER_LANES), const)],
        compiler_params=_cparams("arbitrary", "arbitrary"),
        name="outproj",
    )(x, ya, yb, o, g, w_bf, gate, mul, add, hw, wr, br)


def _route_kernel(sel_ref, cnt_ref, info_ref, meta_ref, base_ref):
    i = pl.program_id(0)
    tm = sel_ref.shape[0]
    lane = lax.broadcasted_iota(jnp.int32, (tm, ROUTER_LANES), 1)
    lane_f = lane.astype(F32)
    sel = sel_ref[...]
    hit1 = lane_f == sel[:, 0:1]
    hit2 = lane_f == sel[:, 1:2]
    onehot = (hit1 | hit2).astype(F32)

    @pl.when(i == 0)
    def _():
        counts = cnt_ref[...]
        padded = jnp.floor((counts + (MOE_ROWS - 1.0)) * (1.0 / MOE_ROWS)) * MOE_ROWS
        r = lax.broadcasted_iota(jnp.int32, (ROUTER_LANES, ROUTER_LANES), 0)
        c = lax.broadcasted_iota(jnp.int32, (ROUTER_LANES, ROUTER_LANES), 1)
        ends = jnp.dot(padded, (r <= c).astype(F32), preferred_element_type=F32,
                       precision=lax.Precision.HIGHEST)
        base_ref[...] = (ends - padded)[0:1]
        row = lax.broadcasted_iota(jnp.int32, (8, ROUTER_LANES), 0)
        meta_ref[...] = jnp.where(row == 0, counts, jnp.where(row == 1, ends - padded, ends))

    tr = lax.broadcasted_iota(jnp.int32, (tm, tm), 0)
    tc = lax.broadcasted_iota(jnp.int32, (tm, tm), 1)
    before = jnp.dot((tc < tr).astype(BF16), onehot.astype(BF16), preferred_element_type=F32)
    pos = base_ref[...] + before
    d1 = jnp.sum(jnp.where(hit1, pos, 0.0), axis=1, keepdims=True)
    d2 = jnp.sum(jnp.where(hit2, pos, 0.0), axis=1, keepdims=True)
    base_ref[...] += jnp.sum(onehot, axis=0, keepdims=True)
    info_ref[...] = jnp.where(lane == 0, d1, jnp.where(lane == 1, d2, sel))


def _route(sel, counts):
    n = sel.shape[0]
    tm = ROUTE_ROWS if n % ROUTE_ROWS == 0 else ROUTE_ROWS // 2
    return pl.pallas_call(
        _route_kernel,
        out_shape=[jax.ShapeDtypeStruct((n, ROUTER_LANES), F32),
                   jax.ShapeDtypeStruct((8, ROUTER_LANES), F32)],
        grid=(n // tm,),
        in_specs=[pl.BlockSpec((tm, ROUTER_LANES), lambda i: (i, 0)),
                  pl.BlockSpec((8, ROUTER_LANES), lambda i: (0, 0))],
        out_specs=[pl.BlockSpec((tm, ROUTER_LANES), lambda i: (i, 0)),
                   pl.BlockSpec((8, ROUTER_LANES), lambda i: (0, 0))],
        scratch_shapes=[pltpu.VMEM((1, ROUTER_LANES), F32)],
        compiler_params=_cparams("arbitrary"),
        name="route",
    )(sel, counts)


def _sc_mesh():
    return plsc.VectorSubcoreMesh(core_axis_name="c", subcore_axis_name="s")


def _sc_gather(table, idx):
    n = idx.shape[0]
    d = table.shape[1]

    @functools.partial(pl.kernel, out_type=jax.ShapeDtypeStruct((n, d), table.dtype), mesh=_sc_mesh())
    def gather(x_hbm, i_hbm, o_hbm):
        def body(i_vmem, o_vmem):
            pltpu.sync_copy(x_hbm.at[i_vmem.at[0]], o_vmem)

        pltpu.emit_pipeline(
            body,
            grid=(n // SC_WINDOW,),
            in_specs=[pl.BlockSpec((1, SC_WINDOW), lambda i: (0, i))],
            out_specs=[pl.BlockSpec((SC_WINDOW, d), lambda i: (i, 0))],
            core_axis_name=("c", "s"),
            dimension_semantics=(pltpu.PARALLEL,),
        )(i_hbm, o_hbm)

    return gather(table, idx.reshape(1, n))


def _sc_scatter2(rows, idx0, idx1, n_out):
    m, d = rows.shape

    @functools.partial(pl.kernel, out_type=jax.ShapeDtypeStruct((n_out, d), rows.dtype), mesh=_sc_mesh())
    def scatter(x_hbm, i0_hbm, i1_hbm, o_hbm):
        def body(x_vmem, i0_vmem, i1_vmem):
            pltpu.sync_copy(x_vmem, o_hbm.at[i0_vmem.at[0]])
            pltpu.sync_copy(x_vmem, o_hbm.at[i1_vmem.at[0]])

        pltpu.emit_pipeline(
            body,
            grid=(m // SC_WINDOW,),
            in_specs=[pl.BlockSpec((SC_WINDOW, d), lambda i: (i, 0)),
                      pl.BlockSpec((1, SC_WINDOW), lambda i: (0, i)),
                      pl.BlockSpec((1, SC_WINDOW), lambda i: (0, i))],
            out_specs=[],
            core_axis_name=("c", "s"),
            dimension_semantics=(pltpu.PARALLEL,),
        )(x_hbm, i0_hbm, i1_hbm)

    return scatter(rows, idx0.reshape(1, m), idx1.reshape(1, m))


def _moe_kernel(be_ref, nu_ref, x_ref, w1_ref, w3_ref, w2_ref, o_ref, w1b, w3b, w2b):
    i = pl.program_id(0)
    e = be_ref[i]
    prev = be_ref[jnp.maximum(i - 1, 0)]

    @pl.when((i == 0) | (e != prev))
    def _():
        w1b[...] = w1_ref[0, 0].astype(BF16)
        w3b[...] = w3_ref[0, 0].astype(BF16)
        w2b[...] = w2_ref[0, 0].astype(BF16)

    @pl.when(i < nu_ref[0])
    def _():
        parts = [h.astype(BF16) for p in range(PLANES) for h in _unpack_rows(x_ref[p])]
        a = sum(jnp.dot(h, w1b[q * SC_ROW:(q + 1) * SC_ROW, :], preferred_element_type=F32)
                for q, h in enumerate(parts))
        b = sum(jnp.dot(h, w3b[q * SC_ROW:(q + 1) * SC_ROW, :], preferred_element_type=F32)
                for q, h in enumerate(parts))
        hmid = (jax.nn.silu(a) * b).astype(BF16)
        _pack_planes(jnp.dot(hmid, w2b[...], preferred_element_type=F32), o_ref)

    @pl.when(i >= nu_ref[0])
    def _():
        o_ref[...] = jnp.zeros_like(o_ref)


def _moe_mlp(blk_expert, n_used, xs, w1, w3, w2, layer):
    n_rows = xs.shape[1]
    d, f = w1.shape[2:]
    nblk = n_rows // MOE_ROWS
    rows = lambda i, be, nu: (0, i, 0)
    return pl.pallas_call(
        _moe_kernel,
        out_shape=jax.ShapeDtypeStruct((PLANES, n_rows, SC_ROW), F32),
        grid_spec=pltpu.PrefetchScalarGridSpec(
            num_scalar_prefetch=2,
            grid=(nblk,),
            in_specs=[pl.BlockSpec((PLANES, MOE_ROWS, SC_ROW), rows),
                      pl.BlockSpec((1, 1, d, f), lambda i, be, nu: (layer, be[i], 0, 0)),
                      pl.BlockSpec((1, 1, d, f), lambda i, be, nu: (layer, be[i], 0, 0)),
                      pl.BlockSpec((1, 1, f, d), lambda i, be, nu: (layer, be[i], 0, 0))],
            out_specs=pl.BlockSpec((PLANES, MOE_ROWS, SC_ROW), rows),
            scratch_shapes=[pltpu.VMEM((d, f), BF16), pltpu.VMEM((d, f), BF16), pltpu.VMEM((f, d), BF16)]),
        compiler_params=_cparams("arbitrary"),
        name="moe",
    )(blk_expert, n_used, xs, w1, w3, w2)


def _combine_kernel(x_ref, pk_ref, info_ref, g_ref, o_ref):
    o_ref[0] = _combined(x_ref, pk_ref, info_ref, g_ref)


def _combine(x, picked, info, gate, row0):
    b, t, d = x.shape
    tm = min(256, t)
    off = row0 // tm
    tok = lambda bb, i: (off + bb * (t // tm) + i, 0)
    tok3 = lambda bb, i: (0, off + bb * (t // tm) + i, 0)
    return pl.pallas_call(
        _combine_kernel,
        out_shape=jax.ShapeDtypeStruct((b, t, d), F32),
        grid=(b, t // tm),
        in_specs=[pl.BlockSpec((1, tm, d), lambda bb, i: (bb, i, 0)),
                  pl.BlockSpec((PLANES * TOP_K, tm, SC_ROW), tok3),
                  pl.BlockSpec((tm, ROUTER_LANES), tok),
                  pl.BlockSpec((1, 1, d), lambda bb, i: (bb, 0, 0))],
        out_specs=pl.BlockSpec((1, tm, d), lambda bb, i: (bb, i, 0)),
        compiler_params=_cparams("arbitrary", "arbitrary"),
        name="combine",
    )(x, picked, info, gate)


def _hier_moe(h2p, sel, counts, w1, w3, w2, layer):
    n_tok = h2p.shape[1]
    info, meta = _route(sel, counts)
    dest = info[:, 0:TOP_K].astype(jnp.int32)
    pad_ends = meta[2, N_GROUPS:N_GROUPS + N_EXPERTS].astype(jnp.int32)
    nblk = -(-(n_tok * TOP_K) // MOE_ROWS) + N_EXPERTS
    n_rows = nblk * MOE_ROWS
    blk_start = jnp.arange(nblk, dtype=jnp.int32) * MOE_ROWS
    blk_expert = jnp.minimum(jnp.sum((pad_ends[None, :] <= blk_start[:, None]).astype(jnp.int32), axis=1),
                             N_EXPERTS - 1)
    n_used = pad_ends[-1:] // MOE_ROWS
    slot = [jnp.concatenate([p * n_rows + dest[:, s] for p in range(PLANES)]) for s in range(TOP_K)]
    xs = _sc_scatter2(h2p.reshape(PLANES * n_tok, SC_ROW), slot[0], slot[1], PLANES * n_rows)
    out = _moe_mlp(blk_expert, n_used, xs.reshape(PLANES, n_rows, SC_ROW), w1, w3, w2, layer)
    idx_all = jnp.concatenate([p * n_rows + dest[:, s] for p in range(PLANES) for s in range(TOP_K)])
    picked = _sc_gather(out.reshape(PLANES * n_rows, SC_ROW), idx_all)
    return picked.reshape(PLANES * TOP_K, n_tok, SC_ROW), info


def _layer(layer, x, xc, pend, c, c_ctx, lb, w_mod, b_mod, norm1_w, w_in, w_s, b_s, q_norm_w, k_norm_w, hgrn_norm_w, w_out,
           norm2_w, w_grp, b_grp, w_exp, b_exp, w1, w3, w2, ctx_out):
    b, t, d = x.shape
    lc = xc.shape[1]
    cc = jnp.zeros((MOD_ROWS, d), F32).at[:b].set(c).at[b].set(c_ctx)
    mod = _mod(cc, w_mod, b_mod, layer)
    sh1, sc1, g1, sh2, sc2, g2 = [m[:, None, :] for m in jnp.split(mod[:b], 6, axis=-1)]
    mod_c = [jnp.broadcast_to(m[None, None, :], (b, 1, d)) for m in jnp.split(mod[b], 6)]

    w_in_bf = w_in.astype(BF16)
    scale = LOG2E * HEAD_DIM ** -0.5
    q_tabs = _rope_tables(t, q_norm_w, scale, B_WIDTH, True)
    k_tabs = _rope_tables(t, k_norm_w, 1.0, B_KV_WIDTH, True)
    qc_tabs = _rope_tables(lc, q_norm_w, scale, B_WIDTH, False)
    kc_tabs = _rope_tables(lc, k_norm_w, 1.0, B_KV_WIDTH, False)
    ws_bf = w_s.astype(BF16)
    bias2d = jnp.repeat(b_s.T, HEAD_DIM, axis=1)
    res = _inproj(x, norm1_w * (1.0 + sc1), sh1, w_in_bf, q_tabs, k_tabs, ws_bf, bias2d,
                  None if pend is None else pend[:2] + (pend[2], 0))
    res_c = _inproj(xc, norm1_w * (1.0 + mod_c[1]), mod_c[0], w_in_bf, qc_tabs, kc_tabs, ws_bf, bias2d,
                    None if pend is None else pend[:2] + (pend[3], b * t))
    if pend is not None:
        x, xc, res, res_c = res[0], res_c[0], res[1:], res_c[1:]
    ya, q, k, v, zc, g = res
    ya_c, q_c, k_c, v_c, zc_c, g_c = res_c

    bound = LOG2E * HEAD_DIM ** 0.5 * jnp.max(jnp.abs(q_norm_w)) * jnp.max(jnp.abs(k_norm_w)) * 1.02
    shift = jnp.ceil(bound)
    attn_flag = jnp.stack([(shift <= SAFE_SHIFT).astype(jnp.int32), shift.astype(jnp.int32)])
    yb = _attn(attn_flag, q, [(k, v), (k_c, v_c)])

    pos = lb > 0.0
    log_lb = jnp.log(jnp.where(pos, lb, 1.0))
    lbc = jnp.stack([1.0 - lb, jnp.log1p(-lb), log_lb, pos.astype(F32), jnp.where(pos, -log_lb, 1e30)], axis=1)
    lbc = jnp.concatenate([lbc, jnp.zeros((2, 3, C_WIDTH), F32)], axis=1)
    o, o_c = _hgrn(zc, zc_c, lbc, ctx_out)

    w_out_bf = w_out.astype(BF16)
    hw = jnp.tile(hgrn_norm_w, C_HEADS)[None, :]
    wr = jnp.zeros((d, ROUTER_LANES), F32).at[:, :N_GROUPS].set(w_grp).at[
        :, N_GROUPS:N_GROUPS + N_EXPERTS].set(w_exp)
    wr_hi = wr.astype(BF16)
    wr = jnp.concatenate([wr_hi, (wr - wr_hi.astype(F32)).astype(BF16)], axis=1)
    br = jnp.zeros((1, ROUTER_LANES), F32).at[0, :N_GROUPS].set(b_grp).at[
        0, N_GROUPS:N_GROUPS + N_EXPERTS].set(b_exp)
    x, h2, sel, cnt = _outproj(x, ya, yb, o, g, w_out_bf, g1, norm2_w * (1.0 + sc2), sh2, hw, wr, br)
    if ctx_out:
        yb_c = _attn(attn_flag, q_c, [(k_c, v_c)])
        xc, h2c, sel_c, cnt_c = _outproj(xc, ya_c, yb_c, o_c, g_c, w_out_bf, mod_c[2],
                                norm2_w * (1.0 + mod_c[4]), mod_c[3], hw, wr, br)
        tokens = jnp.concatenate([h2.reshape(PLANES, -1, SC_ROW), h2c.reshape(PLANES, -1, SC_ROW)], axis=1)
        sel_all = jnp.concatenate([sel.reshape(-1, ROUTER_LANES), sel_c.reshape(-1, ROUTER_LANES)], axis=0)
        picked, info = _hier_moe(tokens, sel_all, cnt + cnt_c, w1, w3, w2, layer)
        return x, xc, (picked, info, g2, mod_c[5])
    picked, info = _hier_moe(h2.reshape(PLANES, -1, SC_ROW), sel.reshape(-1, ROUTER_LANES), cnt, w1, w3, w2, layer)
    return x, xc, (picked, info, g2, None)


def kernel(x, c, ctx, c_ctx, w_mod, b_mod, norm1_w, w_in, w_s, b_s, q_norm_w, k_norm_w, hgrn_lb_logits,
           hgrn_norm_w, w_out, norm2_w, w_grp, b_grp, w_exp, b_exp, w1, w3, w2):
    depth = w_mod.shape[0]
    lb_sm = jax.nn.softmax(hgrn_lb_logits.astype(F32), axis=0)
    lb = jnp.cumsum(lb_sm, axis=0) - lb_sm[0]
    xc = ctx
    pend = None
    for l in range(depth):
        x, xc, pend = _layer(l, x, xc, pend, c, c_ctx, lb[l], w_mod, b_mod[l], norm1_w[l], w_in[l], w_s[l], b_s[l],
                             q_norm_w[l], k_norm_w[l], hgrn_norm_w[l], w_out[l], norm2_w[l], w_grp[l], b_grp[l],
                             w_exp[l], b_exp[l], w1, w3, w2, ctx_out=(l < depth - 1))
    return _combine(x, pend[0], pend[1], pend[2], 0)
```

```python
import functools

import jax
import jax.numpy as jnp
from jax import lax
from jax.experimental import pallas as pl
from jax.experimental.pallas import tpu as pltpu
from jax.experimental.pallas import tpu_sc as plsc

F32 = jnp.float32
BF16 = jnp.bfloat16

D_MODEL = 1024
HEAD_DIM = 64
GRID_W = 64
EPS = 1e-6
ROPE_BASE = 10000.0
A_WIDTH = D_MODEL // 4
A_HEADS = A_WIDTH // HEAD_DIM
A_CHUNK = 128
B_WIDTH = D_MODEL // 2
B_HEADS = B_WIDTH // HEAD_DIM
B_KV_HEADS = 2
B_GROUP = B_HEADS // B_KV_HEADS
B_KV_WIDTH = B_KV_HEADS * HEAD_DIM
C_WIDTH = D_MODEL // 4
C_HEADS = C_WIDTH // HEAD_DIM
OFF_B = 2 * A_WIDTH
OFF_KV = OFF_B + B_WIDTH
OFF_V = OFF_KV + B_KV_WIDTH
OFF_C = OFF_KV + 2 * B_KV_WIDTH
OFF_G = OFF_C + 4 * C_WIDTH
IN_WIDTH = OFF_G + C_WIDTH
N_GROUPS = 4
EXPERTS_PER_GROUP = 8
N_EXPERTS = N_GROUPS * EXPERTS_PER_GROUP
TOP_K = 2
D_FF_EXPERT = D_MODEL // 2

MOD_ROWS = 16
ROUTER_LANES = 128
PROJ_ROWS = 512
HGRN_BLOCK = 32
HGRN_GROUP = 4
LOG2E = 1.4426950408889634
ATTN_ROWS = 256
ATTN_KEYS = 512
SAFE_SHIFT = 60
SAFE_DECAY = 80.0
MOE_ROWS = 512
ROUTE_ROWS = 512
SC_WINDOW = 128
SC_ROW = 256
PLANES = D_MODEL // (2 * SC_ROW)
VMEM_LIMIT = 48 * 1024 * 1024


def _cparams(*sem):
    return pltpu.CompilerParams(dimension_semantics=sem, vmem_limit_bytes=VMEM_LIMIT)


def _head_ones(n, dtype):
    r = lax.broadcasted_iota(jnp.int32, (n, n), 0) >> 6
    c = lax.broadcasted_iota(jnp.int32, (n, n), 1) >> 6
    return (r == c).astype(dtype)


def _head_sum(x, ones_bd):
    return jnp.dot(x.astype(BF16), ones_bd, preferred_element_type=F32)


def _head_rms(x, ones_bd):
    return x * lax.rsqrt(_head_sum(x * x, ones_bd) * (1.0 / HEAD_DIM) + EPS)


def _pack_rows(y):
    bits = lax.bitcast_convert_type(y.astype(BF16).astype(F32), jnp.uint32)
    half = y.shape[1] // 2
    return lax.bitcast_convert_type(bits[:, :half] | (bits[:, half:] >> 16), F32)


def _unpack_rows(w):
    bits = lax.bitcast_convert_type(w, jnp.uint32)
    hi = lax.bitcast_convert_type(bits & jnp.uint32(0xFFFF0000), F32)
    lo = lax.bitcast_convert_type(bits << 16, F32)
    return hi, lo


def _pack_planes(y, ref, lead=()):
    for p in range(PLANES):
        ref[(p,) + lead] = _pack_rows(y[:, 2 * p * SC_ROW:(2 * p + 2) * SC_ROW])


def _mod_kernel(c_ref, w_ref, b_ref, o_ref):
    a = jax.nn.silu(c_ref[...])
    w = w_ref[0]
    a_hi, w_hi = a.astype(BF16), w.astype(BF16)
    a_lo, w_lo = (a - a_hi.astype(F32)).astype(BF16), (w - w_hi.astype(F32)).astype(BF16)
    o_ref[...] = (jnp.dot(a_hi, w_hi, preferred_element_type=F32) + jnp.dot(a_hi, w_lo, preferred_element_type=F32)
                  + jnp.dot(a_lo, w_hi, preferred_element_type=F32) + b_ref[...])


def _mod(cc, w_mod, b_mod, layer):
    n = w_mod.shape[2]
    tn = 1536
    return pl.pallas_call(
        _mod_kernel,
        out_shape=jax.ShapeDtypeStruct((MOD_ROWS, n), F32),
        grid=(n // tn,),
        in_specs=[pl.BlockSpec((MOD_ROWS, D_MODEL), lambda j: (0, 0)),
                  pl.BlockSpec((1, D_MODEL, tn), lambda j: (layer, 0, j)),
                  pl.BlockSpec((1, tn), lambda j: (0, j))],
        out_specs=pl.BlockSpec((MOD_ROWS, tn), lambda j: (0, j)),
        compiler_params=_cparams("arbitrary"),
        name="mod",
    )(cc, w_mod, b_mod.reshape(1, n))


def _rope(xn, c_ref, sp_ref, sm_ref):
    w = xn.shape[-1]
    return (xn * c_ref[...] + pltpu.roll(xn, 16, 1) * sp_ref[...]
            + pltpu.roll(xn, w - 16, 1) * sm_ref[...])


def _combined(x_ref, pk_ref, info_ref, g_ref):
    info = info_ref[...]
    g1 = info[:, 2:3]
    g2 = info[:, 3:4]
    parts = []
    for p in range(PLANES):
        hi1, lo1 = _unpack_rows(pk_ref[TOP_K * p])
        hi2, lo2 = _unpack_rows(pk_ref[TOP_K * p + 1])
        parts += [g1 * hi1 + g2 * hi2, g1 * lo1 + g2 * lo2]
    return x_ref[0] + g_ref[0] * jnp.concatenate(parts, axis=1)


def _gmlp(z, ws_ref, bias_ref):
    gz = jax.nn.gelu(z)
    u = gz[:, :A_WIDTH]
    vn = _head_rms(gz[:, A_WIDTH:], _head_ones(A_WIDTH, BF16))
    lane_head = lax.broadcasted_iota(jnp.int32, vn.shape, 1) >> 6
    acc = bias_ref[...]
    for hh in range(A_HEADS):
        vh = jnp.where(lane_head == hh, vn, 0.0).astype(BF16)
        acc = acc + jnp.dot(ws_ref[hh], vh, preferred_element_type=F32)
    return u * acc


def _inproj_kernel(*refs, pending):
    if pending:
        (x_ref, pk_ref, info_ref, g2_ref, mul_ref, add_ref, w_ref, qc_ref, qsp_ref, qsm_ref, kc_ref, ksp_ref, ksm_ref,
         ws_ref, bias_ref, xo_ref, ya_ref, q_ref, k_ref, v_ref, zc_ref, g_ref) = refs
        x = _combined(x_ref, pk_ref, info_ref, g2_ref)
        xo_ref[0] = x
    else:
        (x_ref, mul_ref, add_ref, w_ref, qc_ref, qsp_ref, qsm_ref, kc_ref, ksp_ref, ksm_ref,
         ws_ref, bias_ref, ya_ref, q_ref, k_ref, v_ref, zc_ref, g_ref) = refs
        x = x_ref[0]
    ms = jnp.mean(x * x, axis=-1, keepdims=True)
    h = x * lax.rsqrt(ms + EPS) * mul_ref[0] + add_ref[0]
    y = jnp.dot(h.astype(BF16), w_ref[...], preferred_element_type=F32)
    for c0 in range(0, x.shape[0], A_CHUNK):
        ya_ref[0, c0:c0 + A_CHUNK, :] = _gmlp(y[c0:c0 + A_CHUNK, :OFF_B], ws_ref, bias_ref).astype(BF16)
    qn = _head_rms(y[:, OFF_B:OFF_KV], _head_ones(B_WIDTH, BF16))
    q_ref[0] = _rope(qn, qc_ref, qsp_ref, qsm_ref).astype(BF16)
    kn = _head_rms(y[:, OFF_KV:OFF_V], _head_ones(B_KV_WIDTH, BF16))
    k_ref[0] = _rope(kn, kc_ref, ksp_ref, ksm_ref).astype(BF16)
    v_ref[0] = y[:, OFF_V:OFF_C].astype(BF16)
    zc_ref[0] = y[:, OFF_C:OFF_G]
    g_ref[0] = y[:, OFF_G:]


def _inproj(x, mul, add, w_bf, q_tabs, k_tabs, ws_bf, bias2d, pending=None):
    b, t, d = x.shape
    tm = min(PROJ_ROWS, t)
    row = lambda i, bb: (bb, i, 0)
    vec = lambda i, bb: (bb, 0, 0)
    tab = lambda i, bb: (i, 0)
    widths = (A_WIDTH, B_WIDTH, B_KV_WIDTH, B_KV_WIDTH, 4 * C_WIDTH, C_WIDTH)
    dtypes = (BF16, BF16, BF16, BF16, F32, F32)
    pre_specs, pre_args = [], []
    if pending is not None:
        picked, info, gate, row0 = pending
        off = row0 // tm
        widths, dtypes = (d,) + widths, (F32,) + dtypes
        pre_specs = [pl.BlockSpec((PLANES * TOP_K, tm, SC_ROW), lambda i, bb: (0, off + bb * (t // tm) + i, 0)),
                     pl.BlockSpec((tm, ROUTER_LANES), lambda i, bb: (off + bb * (t // tm) + i, 0)),
                     pl.BlockSpec((1, 1, d), vec)]
        pre_args = [picked, info, gate]
    return pl.pallas_call(
        functools.partial(_inproj_kernel, pending=pending is not None),
        out_shape=[jax.ShapeDtypeStruct((b, t, w), dt) for w, dt in zip(widths, dtypes)],
        grid=(t // tm, b),
        in_specs=[pl.BlockSpec((1, tm, d), row)] + pre_specs
                 + [pl.BlockSpec((1, 1, d), vec),
                    pl.BlockSpec((1, 1, d), vec),
                    pl.BlockSpec((d, IN_WIDTH), lambda i, bb: (0, 0))]
                 + [pl.BlockSpec((tm, B_WIDTH), tab)] * 3
                 + [pl.BlockSpec((tm, B_KV_WIDTH), tab)] * 3
                 + [pl.BlockSpec((A_HEADS, A_CHUNK, A_CHUNK), lambda i, bb: (0, 0, 0)),
                    pl.BlockSpec((A_CHUNK, A_WIDTH), lambda i, bb: (0, 0))],
        out_specs=[pl.BlockSpec((1, tm, w), row) for w in widths],
        compiler_params=_cparams("arbitrary", "arbitrary"),
        name="inproj",
    )(x, *pre_args, mul, add, w_bf, *q_tabs, *k_tabs, ws_bf, bias2d)


def _rope_tables(t, w, scale, width, rotate):
    ws = w.astype(F32) * scale
    if not rotate:
        c = jnp.broadcast_to(jnp.tile(ws, width // HEAD_DIM)[None, :], (t, width))
        z = jnp.zeros((t, width), F32)
        return c, z, z
    pos = jnp.arange(t)
    row = (pos // GRID_W).astype(F32)
    col = (pos % GRID_W).astype(F32)
    inv_freq = 1.0 / (ROPE_BASE ** (jnp.arange(0, HEAD_DIM // 2, 2, dtype=F32) / (HEAD_DIM // 2)))
    dd = jnp.arange(HEAD_DIM)
    axis = dd // 32
    half = (dd % 32) // 16
    ang = jnp.where(axis[None, :] == 0, row[:, None], col[:, None]) * inv_freq[dd % 16][None, :]
    cos, sin = jnp.cos(ang), jnp.sin(ang)
    c = cos * ws[None, :]
    sm = jnp.where(half[None, :] == 0, -sin * jnp.roll(ws, -16)[None, :], 0.0)
    sp = jnp.where(half[None, :] == 1, sin * jnp.roll(ws, 16)[None, :], 0.0)
    rep = width // HEAD_DIM
    return jnp.tile(c, (1, rep)), jnp.tile(sp, (1, rep)), jnp.tile(sm, (1, rep))


def _attn_kernel(flag_ref, q_ref, *refs, n_seg):
    kv_refs, o_ref = refs[:2 * n_seg], refs[2 * n_seg]
    tq = q_ref.shape[1]
    dh = HEAD_DIM

    def heads(j):
        q4 = jnp.concatenate([q_ref[0, :, (B_GROUP * j + gg) * dh:(B_GROUP * j + gg + 1) * dh]
                              for gg in range(B_GROUP)], axis=0)
        ks, vs = [], []
        for sg in range(n_seg):
            s_len = kv_refs[2 * sg].shape[1]
            for c0 in range(0, s_len, ATTN_KEYS):
                c1 = min(c0 + ATTN_KEYS, s_len)
                ks.append(kv_refs[2 * sg][0, c0:c1, j * dh:(j + 1) * dh])
                vs.append(kv_refs[2 * sg + 1][0, c0:c1, j * dh:(j + 1) * dh])
        return q4, ks, vs

    def scores(q4, ks):
        return [lax.dot_general(kk, q4, (((1,), (1,)), ((), ())), preferred_element_type=F32) for kk in ks]

    def finish(j, ps, vs):
        l = functools.reduce(jnp.add, [jnp.sum(p, axis=0, keepdims=True) for p in ps])
        acc = sum(lax.dot_general(vv, p.astype(BF16), (((0,), (0,)), ((), ())), preferred_element_type=F32)
                  for p, vv in zip(ps, vs))
        o = (acc / l).T
        for gg in range(B_GROUP):
            hh = B_GROUP * j + gg
            o_ref[0, :, hh * dh:(hh + 1) * dh] = o[gg * tq:(gg + 1) * tq].astype(BF16)

    @pl.when(flag_ref[0] > 0)
    def _():
        shift = flag_ref[1].astype(F32)
        for j in range(B_KV_HEADS):
            q4, ks, vs = heads(j)
            finish(j, [jnp.exp2(s - shift) for s in scores(q4, ks)], vs)

    @pl.when(flag_ref[0] <= 0)
    def _():
        for j in range(B_KV_HEADS):
            q4, ks, vs = heads(j)
            ss = scores(q4, ks)
            m = functools.reduce(jnp.maximum, [jnp.max(s, axis=0, keepdims=True) for s in ss])
            finish(j, [jnp.exp2(s - m) for s in ss], vs)


def _attn(flag, q, kv_segs):
    b, t, w = q.shape
    tq = min(ATTN_ROWS, t)
    n_seg = len(kv_segs)
    kv_flat, kv_specs = [], []
    for kk, vv in kv_segs:
        s_len, kw = kk.shape[1:]
        kv_flat += [kk, vv]
        kv_specs += [pl.BlockSpec((1, s_len, kw), lambda bb, i, fl: (bb, 0, 0))] * 2
    return pl.pallas_call(
        functools.partial(_attn_kernel, n_seg=n_seg),
        out_shape=jax.ShapeDtypeStruct((b, t, w), BF16),
        grid_spec=pltpu.PrefetchScalarGridSpec(
            num_scalar_prefetch=1,
            grid=(b, t // tq),
            in_specs=[pl.BlockSpec((1, tq, w), lambda bb, i, fl: (bb, i, 0))] + kv_specs,
            out_specs=pl.BlockSpec((1, tq, w), lambda bb, i, fl: (bb, i, 0))),
        compiler_params=_cparams("arbitrary", "arbitrary"),
        name="attn",
    )(flag, q, *kv_flat)


def _scan_rows(x, reverse):
    n = x.shape[0]
    rows = lax.broadcasted_iota(jnp.int32, x.shape, 0)
    sh = 1
    while sh < n:
        if reverse:
            x = x + jnp.where(rows < n - sh, pltpu.roll(x, n - sh, 0), 0.0)
        else:
            x = x + jnp.where(rows >= sh, pltpu.roll(x, sh, 0), 0.0)
        sh *= 2
    return x


def _stack_heads(x, lane_head):
    return jnp.concatenate([jnp.where(lane_head == hh, x, 0.0) for hh in range(C_HEADS)], axis=0)


def _hgrn_kernel(z_ref, zc_ref, lbc_ref, *refs, ctx_out):
    if ctx_out:
        o_ref, oc_ref, st_ref, kx_ref, bx_ref, vx_ref, flag_ref = refs
    else:
        o_ref, st_ref, kx_ref, bx_ref, vx_ref, flag_ref = refs
        oc_ref = None
    n = C_WIDTH
    nb = HGRN_BLOCK
    nblk_c = zc_ref.shape[1] // nb
    nblk_l = z_ref.shape[1] // nb
    ones_bd = _head_ones(n, BF16)
    rows = lax.broadcasted_iota(jnp.int32, (nb, n), 0)
    lane_head = lax.broadcasted_iota(jnp.int32, (nb, n), 1) >> 6
    lane_head64 = lax.broadcasted_iota(jnp.int32, (HEAD_DIM, n), 1) >> 6
    low_half = (lax.broadcasted_iota(jnp.int32, (HEAD_DIM, 2 * HEAD_DIM), 1) < HEAD_DIM)
    sc_t = lax.broadcasted_iota(jnp.int32, (nb, C_HEADS * nb), 0)
    sc_s = lax.broadcasted_iota(jnp.int32, (nb, C_HEADS * nb), 1) & (nb - 1)

    def gates(z, d):
        one_m_lb = lbc_ref[d, 0:1, :]
        log1m_lb = lbc_ref[d, 1:2, :]
        log_lb = lbc_ref[d, 2:3, :]
        lb_pos = lbc_ref[d, 3:4, :] > 0.5
        soft = jnp.log(1.0 + jnp.exp(-jnp.abs(z)))
        log_rest = log1m_lb + (jnp.minimum(z, 0.0) - soft)
        lse = jnp.maximum(log_lb, log_rest) + jnp.log(1.0 + jnp.exp(-jnp.abs(log_lb - log_rest)))
        return jnp.where(lb_pos, lse, log_rest), one_m_lb * jnp.exp(jnp.minimum(-z, 0.0) - soft)

    def group_blocks(nblk, i):
        grp = HGRN_GROUP if nblk % HGRN_GROUP == 0 else 1
        fwd = [i * grp + gg for gg in range(grp)]
        return grp, fwd, [nblk - 1 - blk for blk in fwd]

    def flag_groups(src_ref, nblk, base):
        def body(i, carry):
            _, fwd, bwd = group_blocks(nblk, i)
            worst = None
            for d, blks in ((0, fwd), (1, bwd)):
                for blk in blks:
                    zz = src_ref[0, pl.ds(pl.multiple_of(blk * nb, nb), nb), (1 + d) * n:(2 + d) * n]
                    step_bound = jnp.minimum(lbc_ref[d, 4:5, :],
                                             jnp.maximum(-zz, 0.0) + (jnp.log(2.0) - lbc_ref[d, 1:2, :]))
                    bound = jnp.sum(step_bound, axis=0, keepdims=True)
                    worst = bound if worst is None else jnp.maximum(worst, bound)
            flag_ref[base + i] = (jnp.max(worst) <= SAFE_DECAY).astype(jnp.int32)
            return carry
        lax.fori_loop(0, nblk // group_blocks(nblk, 0)[0], body, 0)

    n_grp_c = nblk_c // group_blocks(nblk_c, 0)[0]
    flag_groups(zc_ref, nblk_c, 0)
    flag_groups(z_ref, nblk_l, n_grp_c)
    st_ref[...] = jnp.zeros_like(st_ref)
    o_ref[...] = jnp.zeros_like(o_ref)
    if ctx_out:
        oc_ref[...] = jnp.zeros_like(oc_ref)

    def step(src_ref, dst_ref, blk, d, fast):
        reverse = d == 1
        r0 = pl.multiple_of(blk * nb, nb)
        v = src_ref[0, pl.ds(r0, nb), 3 * n:4 * n]
        log_f, k = gates(src_ref[0, pl.ds(r0, nb), (1 + d) * n:(2 + d) * n], d)
        bc = _scan_rows(log_f, reverse)
        edge = 0 if reverse else nb - 1
        b_edge = bc[edge:edge + 1, :]
        st = st_ref[d]
        v_bf = v.astype(BF16)

        if dst_ref is not None:
            q = jax.nn.silu(src_ref[0, pl.ds(r0, nb), 0:n])
            qt = (q * jnp.exp(bc)).astype(BF16)
            o = lax.dot_general(qt, _stack_heads(st, lane_head64).astype(BF16), (((1,), (1,)), ((), ())),
                                preferred_element_type=F32)

            def intra_fast():
                kt = _stack_heads(k * jnp.exp(-bc), lane_head).astype(BF16)
                sc = lax.dot_general(qt, kt, (((1,), (1,)), ((), ())), preferred_element_type=F32)
                keep = (sc_s >= sc_t) if reverse else (sc_s <= sc_t)
                sc = jnp.where(keep, sc, 0.0).astype(BF16)
                return jnp.dot(sc, _stack_heads(v, lane_head).astype(BF16), preferred_element_type=F32)

            def intra_exact():
                kx_ref[d] = k
                bx_ref[d] = bc
                vx_ref[d] = v

                def sbody(s, acc):
                    keep = (rows <= s) if reverse else (rows >= s)
                    e = jnp.exp(jnp.where(keep, bc - bx_ref[d, pl.ds(s, 1), :], 0.0))
                    p = jnp.where(keep, q * e * kx_ref[d, pl.ds(s, 1), :], 0.0)
                    sc = jnp.dot(p.astype(BF16), ones_bd, preferred_element_type=F32)
                    return acc + sc * vx_ref[d, pl.ds(s, 1), :]

                return lax.fori_loop(0, nb, sbody, jnp.zeros((nb, n), F32))

            o = o + (intra_fast() if fast else intra_exact())
            dst_ref[0, pl.ds(r0, nb), :] += o

        kd = (k * jnp.exp(b_edge - bc)).astype(BF16)
        full = lax.dot_general(v_bf, kd, (((0,), (0,)), ((), ())), preferred_element_type=F32)
        upd = jnp.concatenate(
            [jnp.where(low_half,
                       full[(2 * c) * HEAD_DIM:(2 * c + 1) * HEAD_DIM, 2 * c * HEAD_DIM:(2 * c + 2) * HEAD_DIM],
                       full[(2 * c + 1) * HEAD_DIM:(2 * c + 2) * HEAD_DIM, 2 * c * HEAD_DIM:(2 * c + 2) * HEAD_DIM])
             for c in range(C_HEADS // 2)], axis=1)
        st_ref[d] = st * jnp.exp(b_edge) + upd

    def run(src_ref, dst_ref, nblk, base):
        def body(i, carry):
            _, fwd, bwd = group_blocks(nblk, i)
            safe = flag_ref[base + i]

            def group(fast):
                for bf, bb in zip(fwd, bwd):
                    step(src_ref, dst_ref, bf, 0, fast)
                    step(src_ref, dst_ref, bb, 1, fast)

            pl.when(safe > 0)(functools.partial(group, True))
            pl.when(safe <= 0)(functools.partial(group, False))
            return carry
        lax.fori_loop(0, nblk // group_blocks(nblk, 0)[0], body, 0)

    run(zc_ref, oc_ref, nblk_c, 0)
    run(z_ref, o_ref, nblk_l, n_grp_c)


def _hgrn(zc, zc_c, lbc, ctx_out):
    b, t, w = zc.shape
    lc = zc_c.shape[1]
    n = C_WIDTH
    row = lambda bb: (bb, 0, 0)
    out_shape = [jax.ShapeDtypeStruct((b, t, n), F32)]
    out_specs = [pl.BlockSpec((1, t, n), row)]
    if ctx_out:
        out_shape.append(jax.ShapeDtypeStruct((b, lc, n), F32))
        out_specs.append(pl.BlockSpec((1, lc, n), row))
    res = pl.pallas_call(
        functools.partial(_hgrn_kernel, ctx_out=ctx_out),
        out_shape=out_shape,
        grid=(b,),
        in_specs=[pl.BlockSpec((1, t, w), row),
                  pl.BlockSpec((1, lc, w), row),
                  pl.BlockSpec((2, 8, n), lambda bb: (0, 0, 0))],
        out_specs=out_specs,
        scratch_shapes=[pltpu.VMEM((2, HEAD_DIM, n), F32)]
                       + [pltpu.VMEM((2, HGRN_BLOCK, n), F32)] * 3
                       + [pltpu.SMEM(((t + lc) // HGRN_BLOCK,), jnp.int32)],
        compiler_params=_cparams("arbitrary"),
        name="hgrn",
    )(zc, zc_c, lbc)
    return (res[0], res[1]) if ctx_out else (res[0], None)


def _select_experts(lg):
    lane = lax.broadcasted_iota(jnp.int32, lg.shape, 1)
    lane_f = lane.astype(F32)
    neg = -jnp.inf
    gl = jnp.where(lane < N_GROUPS, lg, neg)
    gmax = jnp.max(gl, axis=1, keepdims=True)
    grp = jnp.min(jnp.where(gl == gmax, lane_f, float(ROUTER_LANES)), axis=1, keepdims=True).astype(jnp.int32)
    p_grp = 1.0 / jnp.sum(jnp.exp(gl - gmax), axis=1, keepdims=True)
    in_grp = (lane >= N_GROUPS) & (lane < N_GROUPS + N_EXPERTS) & (((lane - N_GROUPS) >> 3) == grp)
    el = jnp.where(in_grp, lg, neg)
    v1 = jnp.max(el, axis=1, keepdims=True)
    i1 = jnp.min(jnp.where(el == v1, lane_f, float(ROUTER_LANES)), axis=1, keepdims=True)
    el2 = jnp.where(lane_f == i1, neg, el)
    v2 = jnp.max(el2, axis=1, keepdims=True)
    i2 = jnp.min(jnp.where(el2 == v2, lane_f, float(ROUTER_LANES)), axis=1, keepdims=True)
    rr = jnp.exp(v2 - v1)
    g1 = p_grp / (1.0 + rr)
    g2 = p_grp * rr / (1.0 + rr)
    sel = jnp.where(lane == 0, i1, jnp.where(lane == 1, i2, jnp.where(lane == 2, g1, jnp.where(lane == 3, g2, 0.0))))
    counts = jnp.sum(((lane_f == i1) | (lane_f == i2)).astype(F32), axis=0, keepdims=True)
    return sel, counts


def _outproj_kernel(x_ref, ya_ref, yb_ref, o_ref, g_ref, w_ref, gate_ref, mul_ref, add_ref, hw_ref,
                    wr_ref, br_ref, xo_ref, h2_ref, sel_ref, cnt_ref):
    yc = _head_rms(o_ref[0], _head_ones(C_WIDTH, BF16)) * hw_ref[...] * jax.nn.silu(g_ref[0])
    y = jnp.dot(ya_ref[0], w_ref[0:A_WIDTH, :], preferred_element_type=F32)
    y = y + jnp.dot(yb_ref[0], w_ref[A_WIDTH:A_WIDTH + B_WIDTH, :], preferred_element_type=F32)
    y = y + jnp.dot(yc.astype(BF16), w_ref[A_WIDTH + B_WIDTH:, :], preferred_element_type=F32)
    xn = x_ref[0] + gate_ref[0] * y
    xo_ref[0] = xn
    ms = jnp.mean(xn * xn, axis=-1, keepdims=True)
    h2 = xn * lax.rsqrt(ms + EPS) * mul_ref[0] + add_ref[0]
    _pack_planes(h2, h2_ref, (0,))
    h_hi = h2.astype(BF16)
    h_lo = (h2 - h_hi.astype(F32)).astype(BF16)
    both = jnp.dot(h_hi, wr_ref[...], preferred_element_type=F32)
    lg = (both[:, :ROUTER_LANES] + both[:, ROUTER_LANES:] + br_ref[...]
          + jnp.dot(h_lo, wr_ref[:, 0:ROUTER_LANES], preferred_element_type=F32))
    sel, counts = _select_experts(lg)
    sel_ref[0] = sel

    @pl.when((pl.program_id(0) == 0) & (pl.program_id(1) == 0))
    def _():
        cnt_ref[...] = jnp.zeros_like(cnt_ref)
    cnt_ref[...] += counts


def _outproj(x, ya, yb, o, g, w_bf, gate, mul, add, hw, wr, br):
    b, t, d = x.shape
    tm = min(PROJ_ROWS, t)
    row = lambda bb, i: (bb, i, 0)
    vec = lambda bb, i: (bb, 0, 0)
    const = lambda bb, i: (0, 0)
    return pl.pallas_call(
        _outproj_kernel,
        out_shape=[jax.ShapeDtypeStruct((b, t, d), F32),
                   jax.ShapeDtypeStruct((PLANES, b, t, SC_ROW), F32),
                   jax.ShapeDtypeStruct((b, t, ROUTER_LANES), F32),
                   jax.ShapeDtypeStruct((8, ROUTER_LANES), F32)],
        grid=(b, t // tm),
        in_specs=[pl.BlockSpec((1, tm, d), row),
                  pl.BlockSpec((1, tm, A_WIDTH), row),
                  pl.BlockSpec((1, tm, B_WIDTH), row),
                  pl.BlockSpec((1, tm, C_WIDTH), row),
                  pl.BlockSpec((1, tm, C_WIDTH), row),
                  pl.BlockSpec((d, d), const),
                  pl.BlockSpec((1, 1, d), vec),
                  pl.BlockSpec((1, 1, d), vec),
                  pl.BlockSpec((1, 1, d), vec),
                  pl.BlockSpec((1, C_WIDTH), const),
                  pl.BlockSpec((d, 2 * ROUTER_LANES), const),
                  pl.BlockSpec((1, ROUTER_LANES), const)],
        out_specs=[pl.BlockSpec((1, tm, d), row),
                   pl.BlockSpec((PLANES, 1, tm, SC_ROW), lambda bb, i: (0, bb, i, 0)),
                   pl.BlockSpec((1, tm, ROUTER_LANES), row),
                   pl.BlockSpec((8, ROUTER_LANES), const)],
        compiler_params=_cparams("arbitrary", "arbitrary"),
        name="outproj",
    )(x, ya, yb, o, g, w_bf, gate, mul, add, hw, wr, br)


def _route_kernel(sel_ref, cnt_ref, info_ref, meta_ref, base_ref):
    i = pl.program_id(0)
    tm = sel_ref.shape[0]
    lane = lax.broadcasted_iota(jnp.int32, (tm, ROUTER_LANES), 1)
    lane_f = lane.astype(F32)
    sel = sel_ref[...]
    hit1 = lane_f == sel[:, 0:1]
    hit2 = lane_f == sel[:, 1:2]
    onehot = (hit1 | hit2).astype(F32)

    @pl.when(i == 0)
    def _():
        counts = cnt_ref[...]
        padded = jnp.floor((counts + (MOE_ROWS - 1.0)) * (1.0 / MOE_ROWS)) * MOE_ROWS
        r = lax.broadcasted_iota(jnp.int32, (ROUTER_LANES, ROUTER_LANES), 0)
        c = lax.broadcasted_iota(jnp.int32, (ROUTER_LANES, ROUTER_LANES), 1)
        ends = jnp.dot(padded, (r <= c).astype(F32), preferred_element_type=F32,
                       precision=lax.Precision.HIGHEST)
        base_ref[...] = (ends - padded)[0:1]
        row = lax.broadcasted_iota(jnp.int32, (8, ROUTER_LANES), 0)
        meta_ref[...] = jnp.where(row == 0, counts, jnp.where(row == 1, ends - padded, ends))

    tr = lax.broadcasted_iota(jnp.int32, (tm, tm), 0)
    tc = lax.broadcasted_iota(jnp.int32, (tm, tm), 1)
    before = jnp.dot((tc < tr).astype(BF16), onehot.astype(BF16), preferred_element_type=F32)
    pos = base_ref[...] + before
    d1 = jnp.sum(jnp.where(hit1, pos, 0.0), axis=1, keepdims=True)
    d2 = jnp.sum(jnp.where(hit2, pos, 0.0), axis=1, keepdims=True)
    base_ref[...] += jnp.sum(onehot, axis=0, keepdims=True)
    info_ref[...] = jnp.where(lane == 0, d1, jnp.where(lane == 1, d2, sel))


def _route(sel, counts):
    n = sel.shape[0]
    tm = ROUTE_ROWS if n % ROUTE_ROWS == 0 else ROUTE_ROWS // 2
    return pl.pallas_call(
        _route_kernel,
        out_shape=[jax.ShapeDtypeStruct((n, ROUTER_LANES), F32),
                   jax.ShapeDtypeStruct((8, ROUTER_LANES), F32)],
        grid=(n // tm,),
        in_specs=[pl.BlockSpec((tm, ROUTER_LANES), lambda i: (i, 0)),
                  pl.BlockSpec((8, ROUTER_LANES), lambda i: (0, 0))],
        out_specs=[pl.BlockSpec((tm, ROUTER_LANES), lambda i: (i, 0)),
                   pl.BlockSpec((8, ROUTER_LANES), lambda i: (0, 0))],
        scratch_shapes=[pltpu.VMEM((1, ROUTER_LANES), F32)],
        compiler_params=_cparams("arbitrary"),
        name="route",
    )(sel, counts)


def _sc_mesh():
    return plsc.VectorSubcoreMesh(core_axis_name="c", subcore_axis_name="s")


def _sc_gather(table, idx):
    n = idx.shape[0]
    d = table.shape[1]

    @functools.partial(pl.kernel, out_type=jax.ShapeDtypeStruct((n, d), table.dtype), mesh=_sc_mesh())
    def gather(x_hbm, i_hbm, o_hbm):
        def body(i_vmem, o_vmem):
            pltpu.sync_copy(x_hbm.at[i_vmem.at[0]], o_vmem)

        pltpu.emit_pipeline(
            body,
            grid=(n // SC_WINDOW,),
            in_specs=[pl.BlockSpec((1, SC_WINDOW), lambda i: (0, i))],
            out_specs=[pl.BlockSpec((SC_WINDOW, d), lambda i: (i, 0))],
            core_axis_name=("c", "s"),
            dimension_semantics=(pltpu.PARALLEL,),
        )(i_hbm, o_hbm)

    return gather(table, idx.reshape(1, n))


def _sc_scatter2(rows, idx0, idx1, n_out):
    m, d = rows.shape

    @functools.partial(pl.kernel, out_type=jax.ShapeDtypeStruct((n_out, d), rows.dtype), mesh=_sc_mesh())
    def scatter(x_hbm, i0_hbm, i1_hbm, o_hbm):
        def body(x_vmem, i0_vmem, i1_vmem):
            pltpu.sync_copy(x_vmem, o_hbm.at[i0_vmem.at[0]])
            pltpu.sync_copy(x_vmem, o_hbm.at[i1_vmem.at[0]])

        pltpu.emit_pipeline(
            body,
            grid=(m // SC_WINDOW,),
            in_specs=[pl.BlockSpec((SC_WINDOW, d), lambda i: (i, 0)),
                      pl.BlockSpec((1, SC_WINDOW), lambda i: (0, i)),
                      pl.BlockSpec((1, SC_WINDOW), lambda i: (0, i))],
            out_specs=[],
            core_axis_name=("c", "s"),
            dimension_semantics=(pltpu.PARALLEL,),
        )(x_hbm, i0_hbm, i1_hbm)

    return scatter(rows, idx0.reshape(1, m), idx1.reshape(1, m))


def _moe_kernel(be_ref, nu_ref, x_ref, w1_ref, w3_ref, w2_ref, o_ref):
    i = pl.program_id(0)

    @pl.when(i < nu_ref[0])
    def _():
        parts = [h.astype(BF16) for p in range(PLANES) for h in _unpack_rows(x_ref[p])]
        a = sum(jnp.dot(h, w1_ref[0, 0, q * SC_ROW:(q + 1) * SC_ROW, :], preferred_element_type=F32)
                for q, h in enumerate(parts))
        b = sum(jnp.dot(h, w3_ref[0, 0, q * SC_ROW:(q + 1) * SC_ROW, :], preferred_element_type=F32)
                for q, h in enumerate(parts))
        hmid = (jax.nn.silu(a) * b).astype(BF16)
        _pack_planes(jnp.dot(hmid, w2_ref[0, 0], preferred_element_type=F32), o_ref)

    @pl.when(i >= nu_ref[0])
    def _():
        o_ref[...] = jnp.zeros_like(o_ref)


def _moe_mlp(blk_expert, n_used, xs, w1, w3, w2, layer):
    n_rows = xs.shape[1]
    d, f = w1.shape[2:]
    nblk = n_rows // MOE_ROWS
    rows = lambda i, be, nu: (0, i, 0)
    return pl.pallas_call(
        _moe_kernel,
        out_shape=jax.ShapeDtypeStruct((PLANES, n_rows, SC_ROW), F32),
        grid_spec=pltpu.PrefetchScalarGridSpec(
            num_scalar_prefetch=2,
            grid=(nblk,),
            in_specs=[pl.BlockSpec((PLANES, MOE_ROWS, SC_ROW), rows),
                      pl.BlockSpec((1, 1, d, f), lambda i, be, nu: (layer, be[i], 0, 0)),
                      pl.BlockSpec((1, 1, d, f), lambda i, be, nu: (layer, be[i], 0, 0)),
                      pl.BlockSpec((1, 1, f, d), lambda i, be, nu: (layer, be[i], 0, 0))],
            out_specs=pl.BlockSpec((PLANES, MOE_ROWS, SC_ROW), rows)),
        compiler_params=_cparams("arbitrary"),
        name="moe",
    )(blk_expert, n_used, xs, w1, w3, w2)


def _combine_kernel(x_ref, pk_ref, info_ref, g_ref, o_ref):
    o_ref[0] = _combined(x_ref, pk_ref, info_ref, g_ref)


def _combine(x, picked, info, gate, row0):
    b, t, d = x.shape
    tm = min(256, t)
    off = row0 // tm
    tok = lambda bb, i: (off + bb * (t // tm) + i, 0)
    tok3 = lambda bb, i: (0, off + bb * (t // tm) + i, 0)
    return pl.pallas_call(
        _combine_kernel,
        out_shape=jax.ShapeDtypeStruct((b, t, d), F32),
        grid=(b, t // tm),
        in_specs=[pl.BlockSpec((1, tm, d), lambda bb, i: (bb, i, 0)),
                  pl.BlockSpec((PLANES * TOP_K, tm, SC_ROW), tok3),
                  pl.BlockSpec((tm, ROUTER_LANES), tok),
                  pl.BlockSpec((1, 1, d), lambda bb, i: (bb, 0, 0))],
        out_specs=pl.BlockSpec((1, tm, d), lambda bb, i: (bb, i, 0)),
        compiler_params=_cparams("arbitrary", "arbitrary"),
        name="combine",
    )(x, picked, info, gate)


def _hier_moe(h2p, sel, counts, w1, w3, w2, layer):
    n_tok = h2p.shape[1]
    info, meta = _route(sel, counts)
    dest = info[:, 0:TOP_K].astype(jnp.int32)
    pad_ends = meta[2, N_GROUPS:N_GROUPS + N_EXPERTS].astype(jnp.int32)
    nblk = -(-(n_tok * TOP_K) // MOE_ROWS) + N_EXPERTS
    n_rows = nblk * MOE_ROWS
    blk_start = jnp.arange(nblk, dtype=jnp.int32) * MOE_ROWS
    blk_expert = jnp.minimum(jnp.sum((pad_ends[None, :] <= blk_start[:, None]).astype(jnp.int32), axis=1),
                             N_EXPERTS - 1)
    n_used = pad_ends[-1:] // MOE_ROWS
    slot = [jnp.concatenate([p * n_rows + dest[:, s] for p in range(PLANES)]) for s in range(TOP_K)]
    xs = _sc_scatter2(h2p.reshape(PLANES * n_tok, SC_ROW), slot[0], slot[1], PLANES * n_rows)
    out = _moe_mlp(blk_expert, n_used, xs.reshape(PLANES, n_rows, SC_ROW), w1, w3, w2, layer)
    idx_all = jnp.concatenate([p * n_rows + dest[:, s] for p in range(PLANES) for s in range(TOP_K)])
    picked = _sc_gather(out.reshape(PLANES * n_rows, SC_ROW), idx_all)
    return picked.reshape(PLANES * TOP_K, n_tok, SC_ROW), info


def _layer(layer, x, xc, pend, c, c_ctx, lb, w_mod, b_mod, norm1_w, w_in, w_s, b_s, q_norm_w, k_norm_w, hgrn_norm_w, w_out,
           norm2_w, w_grp, b_grp, w_exp, b_exp, w1, w3, w2, ctx_out):
    b, t, d = x.shape
    lc = xc.shape[1]
    cc = jnp.zeros((MOD_ROWS, d), F32).at[:b].set(c).at[b].set(c_ctx)
    mod = _mod(cc, w_mod, b_mod, layer)
    sh1, sc1, g1, sh2, sc2, g2 = [m[:, None, :] for m in jnp.split(mod[:b], 6, axis=-1)]
    mod_c = [jnp.broadcast_to(m[None, None, :], (b, 1, d)) for m in jnp.split(mod[b], 6)]

    w_in_bf = w_in.astype(BF16)
    scale = LOG2E * HEAD_DIM ** -0.5
    q_tabs = _rope_tables(t, q_norm_w, scale, B_WIDTH, True)
    k_tabs = _rope_tables(t, k_norm_w, 1.0, B_KV_WIDTH, True)
    qc_tabs = _rope_tables(lc, q_norm_w, scale, B_WIDTH, False)
    kc_tabs = _rope_tables(lc, k_norm_w, 1.0, B_KV_WIDTH, False)
    ws_bf = w_s.astype(BF16)
    bias2d = jnp.repeat(b_s.T, HEAD_DIM, axis=1)
    res = _inproj(x, norm1_w * (1.0 + sc1), sh1, w_in_bf, q_tabs, k_tabs, ws_bf, bias2d,
                  None if pend is None else pend[:2] + (pend[2], 0))
    res_c = _inproj(xc, norm1_w * (1.0 + mod_c[1]), mod_c[0], w_in_bf, qc_tabs, kc_tabs, ws_bf, bias2d,
                    None if pend is None else pend[:2] + (pend[3], b * t))
    if pend is not None:
        x, xc, res, res_c = res[0], res_c[0], res[1:], res_c[1:]
    ya, q, k, v, zc, g = res
    ya_c, q_c, k_c, v_c, zc_c, g_c = res_c

    bound = LOG2E * HEAD_DIM ** 0.5 * jnp.max(jnp.abs(q_norm_w)) * jnp.max(jnp.abs(k_norm_w)) * 1.02
    shift = jnp.ceil(bound)
    attn_flag = jnp.stack([(shift <= SAFE_SHIFT).astype(jnp.int32), shift.astype(jnp.int32)])
    yb = _attn(attn_flag, q, [(k, v), (k_c, v_c)])

    pos = lb > 0.0
    log_lb = jnp.log(jnp.where(pos, lb, 1.0))
    lbc = jnp.stack([1.0 - lb, jnp.log1p(-lb), log_lb, pos.astype(F32), jnp.where(pos, -log_lb, 1e30)], axis=1)
    lbc = jnp.concatenate([lbc, jnp.zeros((2, 3, C_WIDTH), F32)], axis=1)
    o, o_c = _hgrn(zc, zc_c, lbc, ctx_out)

    w_out_bf = w_out.astype(BF16)
    hw = jnp.tile(hgrn_norm_w, C_HEADS)[None, :]
    wr = jnp.zeros((d, ROUTER_LANES), F32).at[:, :N_GROUPS].set(w_grp).at[
        :, N_GROUPS:N_GROUPS + N_EXPERTS].set(w_exp)
    wr_hi = wr.astype(BF16)
    wr = jnp.concatenate([wr_hi, (wr - wr_hi.astype(F32)).astype(BF16)], axis=1)
    br = jnp.zeros((1, ROUTER_LANES), F32).at[0, :N_GROUPS].set(b_grp).at[
        0, N_GROUPS:N_GROUPS + N_EXPERTS].set(b_exp)
    x, h2, sel, cnt = _outproj(x, ya, yb, o, g, w_out_bf, g1, norm2_w * (1.0 + sc2), sh2, hw, wr, br)
    if ctx_out:
        yb_c = _attn(attn_flag, q_c, [(k_c, v_c)])
        xc, h2c, sel_c, cnt_c = _outproj(xc, ya_c, yb_c, o_c, g_c, w_out_bf, mod_c[2],
                                norm2_w * (1.0 + mod_c[4]), mod_c[3], hw, wr, br)
        tokens = jnp.concatenate([h2.reshape(PLANES, -1, SC_ROW), h2c.reshape(PLANES, -1, SC_ROW)], axis=1)
        sel_all = jnp.concatenate([sel.reshape(-1, ROUTER_LANES), sel_c.reshape(-1, ROUTER_LANES)], axis=0)
        picked, info = _hier_moe(tokens, sel_all, cnt + cnt_c, w1, w3, w2, layer)
        return x, xc, (picked, info, g2, mod_c[5])
    picked, info = _hier_moe(h2.reshape(PLANES, -1, SC_ROW), sel.reshape(-1, ROUTER_LANES), cnt, w1, w3, w2, layer)
    return x, xc, (picked, info, g2, None)


def kernel(x, c, ctx, c_ctx, w_mod, b_mod, norm1_w, w_in, w_s, b_s, q_norm_w, k_norm_w, hgrn_lb_logits,
           hgrn_norm_w, w_out, norm2_w, w_grp, b_grp, w_exp, b_exp, w1, w3, w2):
    depth = w_mod.shape[0]
    lb_sm = jax.nn.softmax(hgrn_lb_logits.astype(F32), axis=0)
    lb = jnp.cumsum(lb_sm, axis=0) - lb_sm[0]
    xc = ctx
    pend = None
    w1, w3, w2 = w1.astype(BF16), w3.astype(BF16), w2.astype(BF16)
    for l in range(depth):
        x, xc, pend = _layer(l, x, xc, pend, c, c_ctx, lb[l], w_mod, b_mod[l], norm1_w[l], w_in[l], w_s[l], b_s[l],
                             q_norm_w[l], k_norm_w[l], hgrn_norm_w[l], w_out[l], norm2_w[l], w_grp[l], b_grp[l],
                             w_exp[l], b_exp[l], w1, w3, w2, ctx_out=(l < depth - 1))
    return _combine(x, pend[0], pend[1], pend[2], 0)
```

```python
import functools

import jax
import jax.numpy as jnp
from jax import lax
from jax.experimental import pallas as pl
from jax.experimental.pallas import tpu as pltpu
from jax.experimental.pallas import tpu_sc as plsc

F32 = jnp.float32
BF16 = jnp.bfloat16

D_MODEL = 1024
HEAD_DIM = 64
GRID_W = 64
EPS = 1e-6
ROPE_BASE = 10000.0
A_WIDTH = D_MODEL // 4
A_HEADS = A_WIDTH // HEAD_DIM
A_CHUNK = 128
B_WIDTH = D_MODEL // 2
B_HEADS = B_WIDTH // HEAD_DIM
B_KV_HEADS = 2
B_GROUP = B_HEADS // B_KV_HEADS
B_KV_WIDTH = B_KV_HEADS * HEAD_DIM
C_WIDTH = D_MODEL // 4
C_HEADS = C_WIDTH // HEAD_DIM
OFF_B = 2 * A_WIDTH
OFF_KV = OFF_B + B_WIDTH
OFF_V = OFF_KV + B_KV_WIDTH
OFF_C = OFF_KV + 2 * B_KV_WIDTH
OFF_G = OFF_C + 4 * C_WIDTH
IN_WIDTH = OFF_G + C_WIDTH
N_GROUPS = 4
EXPERTS_PER_GROUP = 8
N_EXPERTS = N_GROUPS * EXPERTS_PER_GROUP
TOP_K = 2
D_FF_EXPERT = D_MODEL // 2

MOD_ROWS = 16
ROUTER_LANES = 128
PROJ_ROWS = 512
HGRN_BLOCK = 32
HGRN_GROUP = 4
LOG2E = 1.4426950408889634
ATTN_ROWS = 256
ATTN_KEYS = 512
SAFE_SHIFT = 60
SAFE_DECAY = 80.0
MOE_ROWS = 512
ROUTE_ROWS = 512
SC_WINDOW = 128
SC_ROW = 256
PLANES = D_MODEL // (2 * SC_ROW)
VMEM_LIMIT = 48 * 1024 * 1024


def _cparams(*sem):
    return pltpu.CompilerParams(dimension_semantics=sem, vmem_limit_bytes=VMEM_LIMIT)


def _head_ones(n, dtype):
    r = lax.broadcasted_iota(jnp.int32, (n, n), 0) >> 6
    c = lax.broadcasted_iota(jnp.int32, (n, n), 1) >> 6
    return (r == c).astype(dtype)


def _head_sum(x, ones_bd):
    return jnp.dot(x.astype(BF16), ones_bd, preferred_element_type=F32)


def _head_rms(x, ones_bd):
    return x * lax.rsqrt(_head_sum(x * x, ones_bd) * (1.0 / HEAD_DIM) + EPS)


def _pack_rows(y):
    bits = lax.bitcast_convert_type(y.astype(BF16).astype(F32), jnp.uint32)
    half = y.shape[1] // 2
    return lax.bitcast_convert_type(bits[:, :half] | (bits[:, half:] >> 16), F32)


def _unpack_rows(w):
    bits = lax.bitcast_convert_type(w, jnp.uint32)
    hi = lax.bitcast_convert_type(bits & jnp.uint32(0xFFFF0000), F32)
    lo = lax.bitcast_convert_type(bits << 16, F32)
    return hi, lo


def _pack_planes(y, ref, lead=()):
    for p in range(PLANES):
        ref[(p,) + lead] = _pack_rows(y[:, 2 * p * SC_ROW:(2 * p + 2) * SC_ROW])


def _mod_kernel(c_ref, w_ref, b_ref, o_ref):
    a = jax.nn.silu(c_ref[...])
    w = w_ref[0]
    a_hi, w_hi = a.astype(BF16), w.astype(BF16)
    a_lo, w_lo = (a - a_hi.astype(F32)).astype(BF16), (w - w_hi.astype(F32)).astype(BF16)
    o_ref[...] = (jnp.dot(a_hi, w_hi, preferred_element_type=F32) + jnp.dot(a_hi, w_lo, preferred_element_type=F32)
                  + jnp.dot(a_lo, w_hi, preferred_element_type=F32) + b_ref[...])


def _mod(cc, w_mod, b_mod, layer):
    n = w_mod.shape[2]
    tn = 1536
    return pl.pallas_call(
        _mod_kernel,
        out_shape=jax.ShapeDtypeStruct((MOD_ROWS, n), F32),
        grid=(n // tn,),
        in_specs=[pl.BlockSpec((MOD_ROWS, D_MODEL), lambda j: (0, 0)),
                  pl.BlockSpec((1, D_MODEL, tn), lambda j: (layer, 0, j)),
                  pl.BlockSpec((1, tn), lambda j: (0, j))],
        out_specs=pl.BlockSpec((MOD_ROWS, tn), lambda j: (0, j)),
        compiler_params=_cparams("arbitrary"),
        name="mod",
    )(cc, w_mod, b_mod.reshape(1, n))


def _rope(xn, c_ref, sp_ref, sm_ref):
    w = xn.shape[-1]
    return (xn * c_ref[...] + pltpu.roll(xn, 16, 1) * sp_ref[...]
            + pltpu.roll(xn, w - 16, 1) * sm_ref[...])


def _combined(x_ref, pk_ref, info_ref, g_ref):
    info = info_ref[...]
    g1 = info[:, 2:3]
    g2 = info[:, 3:4]
    parts = []
    for p in range(PLANES):
        hi1, lo1 = _unpack_rows(pk_ref[TOP_K * p])
        hi2, lo2 = _unpack_rows(pk_ref[TOP_K * p + 1])
        parts += [g1 * hi1 + g2 * hi2, g1 * lo1 + g2 * lo2]
    return x_ref[0] + g_ref[0] * jnp.concatenate(parts, axis=1)


def _gmlp(z, ws_ref, bias_ref):
    gz = jax.nn.gelu(z)
    u = gz[:, :A_WIDTH]
    vn = _head_rms(gz[:, A_WIDTH:], _head_ones(A_WIDTH, BF16))
    lane_head = lax.broadcasted_iota(jnp.int32, vn.shape, 1) >> 6
    acc = bias_ref[...]
    for hh in range(A_HEADS):
        vh = jnp.where(lane_head == hh, vn, 0.0).astype(BF16)
        acc = acc + jnp.dot(ws_ref[hh], vh, preferred_element_type=F32)
    return u * acc


def _inproj_kernel(*refs, pending):
    if pending:
        (x_ref, pk_ref, info_ref, g2_ref, mul_ref, add_ref, w_ref, qc_ref, qsp_ref, qsm_ref, kc_ref, ksp_ref, ksm_ref,
         ws_ref, bias_ref, xo_ref, ya_ref, q_ref, k_ref, v_ref, zc_ref, g_ref) = refs
        x = _combined(x_ref, pk_ref, info_ref, g2_ref)
        xo_ref[0] = x
    else:
        (x_ref, mul_ref, add_ref, w_ref, qc_ref, qsp_ref, qsm_ref, kc_ref, ksp_ref, ksm_ref,
         ws_ref, bias_ref, ya_ref, q_ref, k_ref, v_ref, zc_ref, g_ref) = refs
        x = x_ref[0]
    ms = jnp.mean(x * x, axis=-1, keepdims=True)
    h = x * lax.rsqrt(ms + EPS) * mul_ref[0] + add_ref[0]
    y = jnp.dot(h.astype(BF16), w_ref[...], preferred_element_type=F32)
    for c0 in range(0, x.shape[0], A_CHUNK):
        ya_ref[0, c0:c0 + A_CHUNK, :] = _gmlp(y[c0:c0 + A_CHUNK, :OFF_B], ws_ref, bias_ref).astype(BF16)
    qn = _head_rms(y[:, OFF_B:OFF_KV], _head_ones(B_WIDTH, BF16))
    q_ref[0] = _rope(qn, qc_ref, qsp_ref, qsm_ref).astype(BF16)
    kn = _head_rms(y[:, OFF_KV:OFF_V], _head_ones(B_KV_WIDTH, BF16))
    k_ref[0] = _rope(kn, kc_ref, ksp_ref, ksm_ref).astype(BF16)
    v_ref[0] = y[:, OFF_V:OFF_C].astype(BF16)
    zc_ref[0] = y[:, OFF_C:OFF_G]
    g_ref[0] = y[:, OFF_G:]


def _inproj(x, mul, add, w_bf, q_tabs, k_tabs, ws_bf, bias2d, pending=None):
    b, t, d = x.shape
    tm = min(PROJ_ROWS, t)
    row = lambda i, bb: (bb, i, 0)
    vec = lambda i, bb: (bb, 0, 0)
    tab = lambda i, bb: (i, 0)
    widths = (A_WIDTH, B_WIDTH, B_KV_WIDTH, B_KV_WIDTH, 4 * C_WIDTH, C_WIDTH)
    dtypes = (BF16, BF16, BF16, BF16, F32, F32)
    pre_specs, pre_args = [], []
    if pending is not None:
        picked, info, gate, row0 = pending
        off = row0 // tm
        widths, dtypes = (d,) + widths, (F32,) + dtypes
        pre_specs = [pl.BlockSpec((PLANES * TOP_K, tm, SC_ROW), lambda i, bb: (0, off + bb * (t // tm) + i, 0)),
                     pl.BlockSpec((tm, ROUTER_LANES), lambda i, bb: (off + bb * (t // tm) + i, 0)),
                     pl.BlockSpec((1, 1, d), vec)]
        pre_args = [picked, info, gate]
    return pl.pallas_call(
        functools.partial(_inproj_kernel, pending=pending is not None),
        out_shape=[jax.ShapeDtypeStruct((b, t, w), dt) for w, dt in zip(widths, dtypes)],
        grid=(t // tm, b),
        in_specs=[pl.BlockSpec((1, tm, d), row)] + pre_specs
                 + [pl.BlockSpec((1, 1, d), vec),
                    pl.BlockSpec((1, 1, d), vec),
                    pl.BlockSpec((d, IN_WIDTH), lambda i, bb: (0, 0))]
                 + [pl.BlockSpec((tm, B_WIDTH), tab)] * 3
                 + [pl.BlockSpec((tm, B_KV_WIDTH), tab)] * 3
                 + [pl.BlockSpec((A_HEADS, A_CHUNK, A_CHUNK), lambda i, bb: (0, 0, 0)),
                    pl.BlockSpec((A_CHUNK, A_WIDTH), lambda i, bb: (0, 0))],
        out_specs=[pl.BlockSpec((1, tm, w), row) for w in widths],
        compiler_params=_cparams("arbitrary", "arbitrary"),
        name="inproj",
    )(x, *pre_args, mul, add, w_bf, *q_tabs, *k_tabs, ws_bf, bias2d)


def _rope_tables(t, w, scale, width, rotate):
    ws = w.astype(F32) * scale
    if not rotate:
        c = jnp.broadcast_to(jnp.tile(ws, width // HEAD_DIM)[None, :], (t, width))
        z = jnp.zeros((t, width), F32)
        return c, z, z
    pos = jnp.arange(t)
    row = (pos // GRID_W).astype(F32)
    col = (pos % GRID_W).astype(F32)
    inv_freq = 1.0 / (ROPE_BASE ** (jnp.arange(0, HEAD_DIM // 2, 2, dtype=F32) / (HEAD_DIM // 2)))
    dd = jnp.arange(HEAD_DIM)
    axis = dd // 32
    half = (dd % 32) // 16
    ang = jnp.where(axis[None, :] == 0, row[:, None], col[:, None]) * inv_freq[dd % 16][None, :]
    cos, sin = jnp.cos(ang), jnp.sin(ang)
    c = cos * ws[None, :]
    sm = jnp.where(half[None, :] == 0, -sin * jnp.roll(ws, -16)[None, :], 0.0)
    sp = jnp.where(half[None, :] == 1, sin * jnp.roll(ws, 16)[None, :], 0.0)
    rep = width // HEAD_DIM
    return jnp.tile(c, (1, rep)), jnp.tile(sp, (1, rep)), jnp.tile(sm, (1, rep))


def _attn_kernel(flag_ref, q_ref, *refs, n_seg):
    kv_refs, o_ref = refs[:2 * n_seg], refs[2 * n_seg]
    tq = q_ref.shape[1]
    dh = HEAD_DIM

    def heads(j):
        q4 = jnp.concatenate([q_ref[0, :, (B_GROUP * j + gg) * dh:(B_GROUP * j + gg + 1) * dh]
                              for gg in range(B_GROUP)], axis=0)
        ks, vs = [], []
        for sg in range(n_seg):
            s_len = kv_refs[2 * sg].shape[1]
            for c0 in range(0, s_len, ATTN_KEYS):
                c1 = min(c0 + ATTN_KEYS, s_len)
                ks.append(kv_refs[2 * sg][0, c0:c1, j * dh:(j + 1) * dh])
                vs.append(kv_refs[2 * sg + 1][0, c0:c1, j * dh:(j + 1) * dh])
        return q4, ks, vs

    def scores(q4, ks):
        return [lax.dot_general(kk, q4, (((1,), (1,)), ((), ())), preferred_element_type=F32) for kk in ks]

    def finish(j, ps, vs):
        l = functools.reduce(jnp.add, [jnp.sum(p, axis=0, keepdims=True) for p in ps])
        acc = sum(lax.dot_general(vv, p.astype(BF16), (((0,), (0,)), ((), ())), preferred_element_type=F32)
                  for p, vv in zip(ps, vs))
        o = (acc / l).T
        for gg in range(B_GROUP):
            hh = B_GROUP * j + gg
            o_ref[0, :, hh * dh:(hh + 1) * dh] = o[gg * tq:(gg + 1) * tq].astype(BF16)

    @pl.when(flag_ref[0] > 0)
    def _():
        shift = flag_ref[1].astype(F32)
        for j in range(B_KV_HEADS):
            q4, ks, vs = heads(j)
            finish(j, [jnp.exp2(s - shift) for s in scores(q4, ks)], vs)

    @pl.when(flag_ref[0] <= 0)
    def _():
        for j in range(B_KV_HEADS):
            q4, ks, vs = heads(j)
            ss = scores(q4, ks)
            m = functools.reduce(jnp.maximum, [jnp.max(s, axis=0, keepdims=True) for s in ss])
            finish(j, [jnp.exp2(s - m) for s in ss], vs)


def _attn(flag, q, kv_segs):
    b, t, w = q.shape
    tq = min(ATTN_ROWS, t)
    n_seg = len(kv_segs)
    kv_flat, kv_specs = [], []
    for kk, vv in kv_segs:
        s_len, kw = kk.shape[1:]
        kv_flat += [kk, vv]
        kv_specs += [pl.BlockSpec((1, s_len, kw), lambda bb, i, fl: (bb, 0, 0))] * 2
    return pl.pallas_call(
        functools.partial(_attn_kernel, n_seg=n_seg),
        out_shape=jax.ShapeDtypeStruct((b, t, w), BF16),
        grid_spec=pltpu.PrefetchScalarGridSpec(
            num_scalar_prefetch=1,
            grid=(b, t // tq),
            in_specs=[pl.BlockSpec((1, tq, w), lambda bb, i, fl: (bb, i, 0))] + kv_specs,
            out_specs=pl.BlockSpec((1, tq, w), lambda bb, i, fl: (bb, i, 0))),
        compiler_params=_cparams("arbitrary", "arbitrary"),
        name="attn",
    )(flag, q, *kv_flat)


def _scan_rows(x, reverse):
    n = x.shape[0]
    rows = lax.broadcasted_iota(jnp.int32, x.shape, 0)
    sh = 1
    while sh < n:
        if reverse:
            x = x + jnp.where(rows < n - sh, pltpu.roll(x, n - sh, 0), 0.0)
        else:
            x = x + jnp.where(rows >= sh, pltpu.roll(x, sh, 0), 0.0)
        sh *= 2
    return x


def _stack_heads(x, lane_head):
    return jnp.concatenate([jnp.where(lane_head == hh, x, 0.0) for hh in range(C_HEADS)], axis=0)


def _hgrn_kernel(z_ref, zc_ref, lbc_ref, *refs, ctx_out):
    if ctx_out:
        o_ref, oc_ref, st_ref, kx_ref, bx_ref, vx_ref, flag_ref = refs
    else:
        o_ref, st_ref, kx_ref, bx_ref, vx_ref, flag_ref = refs
        oc_ref = None
    n = C_WIDTH
    nb = HGRN_BLOCK
    nblk_c = zc_ref.shape[1] // nb
    nblk_l = z_ref.shape[1] // nb
    ones_bd = _head_ones(n, BF16)
    rows = lax.broadcasted_iota(jnp.int32, (nb, n), 0)
    lane_head = lax.broadcasted_iota(jnp.int32, (nb, n), 1) >> 6
    lane_head64 = lax.broadcasted_iota(jnp.int32, (HEAD_DIM, n), 1) >> 6
    low_half = (lax.broadcasted_iota(jnp.int32, (HEAD_DIM, 2 * HEAD_DIM), 1) < HEAD_DIM)
    sc_t = lax.broadcasted_iota(jnp.int32, (nb, C_HEADS * nb), 0)
    sc_s = lax.broadcasted_iota(jnp.int32, (nb, C_HEADS * nb), 1) & (nb - 1)

    def gates(z, d):
        one_m_lb = lbc_ref[d, 0:1, :]
        log1m_lb = lbc_ref[d, 1:2, :]
        log_lb = lbc_ref[d, 2:3, :]
        lb_pos = lbc_ref[d, 3:4, :] > 0.5
        soft = jnp.log(1.0 + jnp.exp(-jnp.abs(z)))
        log_rest = log1m_lb + (jnp.minimum(z, 0.0) - soft)
        lse = jnp.maximum(log_lb, log_rest) + jnp.log(1.0 + jnp.exp(-jnp.abs(log_lb - log_rest)))
        return jnp.where(lb_pos, lse, log_rest), one_m_lb * jnp.exp(jnp.minimum(-z, 0.0) - soft)

    def group_blocks(nblk, i):
        grp = HGRN_GROUP if nblk % HGRN_GROUP == 0 else 1
        fwd = [i * grp + gg for gg in range(grp)]
        return grp, fwd, [nblk - 1 - blk for blk in fwd]

    def flag_groups(src_ref, nblk, base):
        def body(i, carry):
            _, fwd, bwd = group_blocks(nblk, i)
            worst = None
            for d, blks in ((0, fwd), (1, bwd)):
                for blk in blks:
                    zz = src_ref[0, pl.ds(pl.multiple_of(blk * nb, nb), nb), (1 + d) * n:(2 + d) * n]
                    step_bound = jnp.minimum(lbc_ref[d, 4:5, :],
                                             jnp.maximum(-zz, 0.0) + (jnp.log(2.0) - lbc_ref[d, 1:2, :]))
                    bound = jnp.sum(step_bound, axis=0, keepdims=True)
                    worst = bound if worst is None else jnp.maximum(worst, bound)
            flag_ref[base + i] = (jnp.max(worst) <= SAFE_DECAY).astype(jnp.int32)
            return carry
        lax.fori_loop(0, nblk // group_blocks(nblk, 0)[0], body, 0)

    n_grp_c = nblk_c // group_blocks(nblk_c, 0)[0]
    flag_groups(zc_ref, nblk_c, 0)
    flag_groups(z_ref, nblk_l, n_grp_c)
    st_ref[...] = jnp.zeros_like(st_ref)
    o_ref[...] = jnp.zeros_like(o_ref)
    if ctx_out:
        oc_ref[...] = jnp.zeros_like(oc_ref)

    def step(src_ref, dst_ref, blk, d, fast):
        reverse = d == 1
        r0 = pl.multiple_of(blk * nb, nb)
        v = src_ref[0, pl.ds(r0, nb), 3 * n:4 * n]
        log_f, k = gates(src_ref[0, pl.ds(r0, nb), (1 + d) * n:(2 + d) * n], d)
        bc = _scan_rows(log_f, reverse)
        edge = 0 if reverse else nb - 1
        b_edge = bc[edge:edge + 1, :]
        st = st_ref[d]
        v_bf = v.astype(BF16)

        if dst_ref is not None:
            q = jax.nn.silu(src_ref[0, pl.ds(r0, nb), 0:n])
            qt = (q * jnp.exp(bc)).astype(BF16)
            o = lax.dot_general(qt, _stack_heads(st, lane_head64).astype(BF16), (((1,), (1,)), ((), ())),
                                preferred_element_type=F32)

            def intra_fast():
                kt = _stack_heads(k * jnp.exp(-bc), lane_head).astype(BF16)
                sc = lax.dot_general(qt, kt, (((1,), (1,)), ((), ())), preferred_element_type=F32)
                keep = (sc_s >= sc_t) if reverse else (sc_s <= sc_t)
                sc = jnp.where(keep, sc, 0.0).astype(BF16)
                return jnp.dot(sc, _stack_heads(v, lane_head).astype(BF16), preferred_element_type=F32)

            def intra_exact():
                kx_ref[d] = k
                bx_ref[d] = bc
                vx_ref[d] = v

                def sbody(s, acc):
                    keep = (rows <= s) if reverse else (rows >= s)
                    e = jnp.exp(jnp.where(keep, bc - bx_ref[d, pl.ds(s, 1), :], 0.0))
                    p = jnp.where(keep, q * e * kx_ref[d, pl.ds(s, 1), :], 0.0)
                    sc = jnp.dot(p.astype(BF16), ones_bd, preferred_element_type=F32)
                    return acc + sc * vx_ref[d, pl.ds(s, 1), :]

                return lax.fori_loop(0, nb, sbody, jnp.zeros((nb, n), F32))

            o = o + (intra_fast() if fast else intra_exact())
            dst_ref[0, pl.ds(r0, nb), :] += o

        kd = (k * jnp.exp(b_edge - bc)).astype(BF16)
        full = lax.dot_general(v_bf, kd, (((0,), (0,)), ((), ())), preferred_element_type=F32)
        upd = jnp.concatenate(
            [jnp.where(low_half,
                       full[(2 * c) * HEAD_DIM:(2 * c + 1) * HEAD_DIM, 2 * c * HEAD_DIM:(2 * c + 2) * HEAD_DIM],
                       full[(2 * c + 1) * HEAD_DIM:(2 * c + 2) * HEAD_DIM, 2 * c * HEAD_DIM:(2 * c + 2) * HEAD_DIM])
             for c in range(C_HEADS // 2)], axis=1)
        st_ref[d] = st * jnp.exp(b_edge) + upd

    def run(src_ref, dst_ref, nblk, base):
        def body(i, carry):
            _, fwd, bwd = group_blocks(nblk, i)
            safe = flag_ref[base + i]

            def group(fast):
                for bf, bb in zip(fwd, bwd):
                    step(src_ref, dst_ref, bf, 0, fast)
                    step(src_ref, dst_ref, bb, 1, fast)

            pl.when(safe > 0)(functools.partial(group, True))
            pl.when(safe <= 0)(functools.partial(group, False))
            return carry
        lax.fori_loop(0, nblk // group_blocks(nblk, 0)[0], body, 0)

    run(zc_ref, oc_ref, nblk_c, 0)
    run(z_ref, o_ref, nblk_l, n_grp_c)


def _hgrn(zc, zc_c, lbc, ctx_out):
    b, t, w = zc.shape
    lc = zc_c.shape[1]
    n = C_WIDTH
    row = lambda bb: (bb, 0, 0)
    out_shape = [jax.ShapeDtypeStruct((b, t, n), F32)]
    out_specs = [pl.BlockSpec((1, t, n), row)]
    if ctx_out:
        out_shape.append(jax.ShapeDtypeStruct((b, lc, n), F32))
        out_specs.append(pl.BlockSpec((1, lc, n), row))
    res = pl.pallas_call(
        functools.partial(_hgrn_kernel, ctx_out=ctx_out),
        out_shape=out_shape,
        grid=(b,),
        in_specs=[pl.BlockSpec((1, t, w), row),
                  pl.BlockSpec((1, lc, w), row),
                  pl.BlockSpec((2, 8, n), lambda bb: (0, 0, 0))],
        out_specs=out_specs,
        scratch_shapes=[pltpu.VMEM((2, HEAD_DIM, n), F32)]
                       + [pltpu.VMEM((2, HGRN_BLOCK, n), F32)] * 3
                       + [pltpu.SMEM(((t + lc) // HGRN_BLOCK,), jnp.int32)],
        compiler_params=_cparams("arbitrary"),
        name="hgrn",
    )(zc, zc_c, lbc)
    return (res[0], res[1]) if ctx_out else (res[0], None)


def _select_experts(lg):
    lane = lax.broadcasted_iota(jnp.int32, lg.shape, 1)
    lane_f = lane.astype(F32)
    neg = -jnp.inf
    gl = jnp.where(lane < N_GROUPS, lg, neg)
    gmax = jnp.max(gl, axis=1, keepdims=True)
    grp = jnp.min(jnp.where(gl == gmax, lane_f, float(ROUTER_LANES)), axis=1, keepdims=True).astype(jnp.int32)
    p_grp = 1.0 / jnp.sum(jnp.exp(gl - gmax), axis=1, keepdims=True)
    in_grp = (lane >= N_GROUPS) & (lane < N_GROUPS + N_EXPERTS) & (((lane - N_GROUPS) >> 3) == grp)
    el = jnp.where(in_grp, lg, neg)
    v1 = jnp.max(el, axis=1, keepdims=True)
    i1 = jnp.min(jnp.where(el == v1, lane_f, float(ROUTER_LANES)), axis=1, keepdims=True)
    el2 = jnp.where(lane_f == i1, neg, el)
    v2 = jnp.max(el2, axis=1, keepdims=True)
    i2 = jnp.min(jnp.where(el2 == v2, lane_f, float(ROUTER_LANES)), axis=1, keepdims=True)
    rr = jnp.exp(v2 - v1)
    g1 = p_grp / (1.0 + rr)
    g2 = p_grp * rr / (1.0 + rr)
    sel = jnp.where(lane == 0, i1, jnp.where(lane == 1, i2, jnp.where(lane == 2, g1, jnp.where(lane == 3, g2, 0.0))))
    counts = jnp.sum(((lane_f == i1) | (lane_f == i2)).astype(F32), axis=0, keepdims=True)
    return sel, counts


def _outproj_kernel(x_ref, ya_ref, yb_ref, o_ref, g_ref, w_ref, gate_ref, mul_ref, add_ref, hw_ref,
                    wr_ref, br_ref, xo_ref, h2_ref, sel_ref, cnt_ref):
    yc = _head_rms(o_ref[0], _head_ones(C_WIDTH, BF16)) * hw_ref[...] * jax.nn.silu(g_ref[0])
    y = jnp.dot(ya_ref[0], w_ref[0:A_WIDTH, :], preferred_element_type=F32)
    y = y + jnp.dot(yb_ref[0], w_ref[A_WIDTH:A_WIDTH + B_WIDTH, :], preferred_element_type=F32)
    y = y + jnp.dot(yc.astype(BF16), w_ref[A_WIDTH + B_WIDTH:, :], preferred_element_type=F32)
    xn = x_ref[0] + gate_ref[0] * y
    xo_ref[0] = xn
    ms = jnp.mean(xn * xn, axis=-1, keepdims=True)
    h2 = xn * lax.rsqrt(ms + EPS) * mul_ref[0] + add_ref[0]
    _pack_planes(h2, h2_ref, (0,))
    h_hi = h2.astype(BF16)
    h_lo = (h2 - h_hi.astype(F32)).astype(BF16)
    both = jnp.dot(h_hi, wr_ref[...], preferred_element_type=F32)
    lg = (both[:, :ROUTER_LANES] + both[:, ROUTER_LANES:] + br_ref[...]
          + jnp.dot(h_lo, wr_ref[:, 0:ROUTER_LANES], preferred_element_type=F32))
    sel, counts = _select_experts(lg)
    sel_ref[0] = sel

    @pl.when((pl.program_id(0) == 0) & (pl.program_id(1) == 0))
    def _():
        cnt_ref[...] = jnp.zeros_like(cnt_ref)
    cnt_ref[...] += counts


def _outproj(x, ya, yb, o, g, w_bf, gate, mul, add, hw, wr, br):
    b, t, d = x.shape
    tm = min(PROJ_ROWS, t)
    row = lambda bb, i: (bb, i, 0)
    vec = lambda bb, i: (bb, 0, 0)
    const = lambda bb, i: (0, 0)
    return pl.pallas_call(
        _outproj_kernel,
        out_shape=[jax.ShapeDtypeStruct((b, t, d), F32),
                   jax.ShapeDtypeStruct((PLANES, b, t, SC_ROW), F32),
                   jax.ShapeDtypeStruct((b, t, ROUTER_LANES), F32),
                   jax.ShapeDtypeStruct((8, ROUTER_LANES), F32)],
        grid=(b, t // tm),
        in_specs=[pl.BlockSpec((1, tm, d), row),
                  pl.BlockSpec((1, tm, A_WIDTH), row),
                  pl.BlockSpec((1, tm, B_WIDTH), row),
                  pl.BlockSpec((1, tm, C_WIDTH), row),
                  pl.BlockSpec((1, tm, C_WIDTH), row),
                  pl.BlockSpec((d, d), const),
                  pl.BlockSpec((1, 1, d), vec),
                  pl.BlockSpec((1, 1, d), vec),
                  pl.BlockSpec((1, 1, d), vec),
                  pl.BlockSpec((1, C_WIDTH), const),
                  pl.BlockSpec((d, 2 * ROUTER_LANES), const),
                  pl.BlockSpec((1, ROUTER_LANES), const)],
        out_specs=[pl.BlockSpec((1, tm, d), row),
                   pl.BlockSpec((PLANES, 1, tm, SC_ROW), lambda bb, i: (0, bb, i, 0)),
                   pl.BlockSpec((1, tm, ROUTER_LANES), row),
                   pl.BlockSpec((8, ROUTER_LANES), const)],
        compiler_params=_cparams("arbitrary", "arbitrary"),
        name="outproj",
    )(x, ya, yb, o, g, w_bf, gate, mul, add, hw, wr, br)


def _route_kernel(sel_ref, cnt_ref, info_ref, meta_ref, base_ref):
    i = pl.program_id(0)
    tm = sel_ref.shape[0]
    lane = lax.broadcasted_iota(jnp.int32, (tm, ROUTER_LANES), 1)
    lane_f = lane.astype(F32)
    sel = sel_ref[...]
    hit1 = lane_f == sel[:, 0:1]
    hit2 = lane_f == sel[:, 1:2]
    onehot = (hit1 | hit2).astype(F32)

    @pl.when(i == 0)
    def _():
        counts = cnt_ref[...]
        padded = jnp.floor((counts + (MOE_ROWS - 1.0)) * (1.0 / MOE_ROWS)) * MOE_ROWS
        r = lax.broadcasted_iota(jnp.int32, (ROUTER_LANES, ROUTER_LANES), 0)
        c = lax.broadcasted_iota(jnp.int32, (ROUTER_LANES, ROUTER_LANES), 1)
        ends = jnp.dot(padded, (r <= c).astype(F32), preferred_element_type=F32,
                       precision=lax.Precision.HIGHEST)
        base_ref[...] = (ends - padded)[0:1]
        row = lax.broadcasted_iota(jnp.int32, (8, ROUTER_LANES), 0)
        meta_ref[...] = jnp.where(row == 0, counts, jnp.where(row == 1, ends - padded, ends))

    tr = lax.broadcasted_iota(jnp.int32, (tm, tm), 0)
    tc = lax.broadcasted_iota(jnp.int32, (tm, tm), 1)
    before = jnp.dot((tc < tr).astype(BF16), onehot.astype(BF16), preferred_element_type=F32)
    pos = base_ref[...] + before
    d1 = jnp.sum(jnp.where(hit1, pos, 0.0), axis=1, keepdims=True)
    d2 = jnp.sum(jnp.where(hit2, pos, 0.0), axis=1, keepdims=True)
    base_ref[...] += jnp.sum(onehot, axis=0, keepdims=True)
    info_ref[...] = jnp.where(lane == 0, d1, jnp.where(lane == 1, d2, sel))


def _route(sel, counts):
    n = sel.shape[0]
    tm = ROUTE_ROWS if n % ROUTE_ROWS == 0 else ROUTE_ROWS // 2
    return pl.pallas_call(
        _route_kernel,
        out_shape=[jax.ShapeDtypeStruct((n, ROUTER_LANES), F32),
                   jax.ShapeDtypeStruct((8, ROUTER_LANES), F32)],
        grid=(n // tm,),
        in_specs=[pl.BlockSpec((tm, ROUTER_LANES), lambda i: (i, 0)),
                  pl.BlockSpec((8, ROUTER_LANES), lambda i: (0, 0))],
        out_specs=[pl.BlockSpec((tm, ROUTER_LANES), lambda i: (i, 0)),
                   pl.BlockSpec((8, ROUTER_LANES), lambda i: (0, 0))],
        scratch_shapes=[pltpu.VMEM((1, ROUTER_LANES), F32)],
        compiler_params=_cparams("arbitrary"),
        name="route",
    )(sel, counts)


def _sc_mesh():
    return plsc.VectorSubcoreMesh(core_axis_name="c", subcore_axis_name="s")


def _sc_gather(table, idx):
    n = idx.shape[0]
    d = table.shape[1]

    @functools.partial(pl.kernel, out_type=jax.ShapeDtypeStruct((n, d), table.dtype), mesh=_sc_mesh())
    def gather(x_hbm, i_hbm, o_hbm):
        def body(i_vmem, o_vmem):
            pltpu.sync_copy(x_hbm.at[i_vmem.at[0]], o_vmem)

        pltpu.emit_pipeline(
            body,
            grid=(n // SC_WINDOW,),
            in_specs=[pl.BlockSpec((1, SC_WINDOW), lambda i: (0, i))],
            out_specs=[pl.BlockSpec((SC_WINDOW, d), lambda i: (i, 0))],
            core_axis_name=("c", "s"),
            dimension_semantics=(pltpu.PARALLEL,),
        )(i_hbm, o_hbm)

    return gather(table, idx.reshape(1, n))


def _sc_scatter2(rows, idx0, idx1, n_out):
    m, d = rows.shape

    @functools.partial(pl.kernel, out_type=jax.ShapeDtypeStruct((n_out, d), rows.dtype), mesh=_sc_mesh())
    def scatter(x_hbm, i0_hbm, i1_hbm, o_hbm):
        def body(x_vmem, i0_vmem, i1_vmem):
            pltpu.sync_copy(x_vmem, o_hbm.at[i0_vmem.at[0]])
            pltpu.sync_copy(x_vmem, o_hbm.at[i1_vmem.at[0]])

        pltpu.emit_pipeline(
            body,
            grid=(m // SC_WINDOW,),
            in_specs=[pl.BlockSpec((SC_WINDOW, d), lambda i: (i, 0)),
                      pl.BlockSpec((1, SC_WINDOW), lambda i: (0, i)),
                      pl.BlockSpec((1, SC_WINDOW), lambda i: (0, i))],
            out_specs=[],
            core_axis_name=("c", "s"),
            dimension_semantics=(pltpu.PARALLEL,),
        )(x_hbm, i0_hbm, i1_hbm)

    return scatter(rows, idx0.reshape(1, m), idx1.reshape(1, m))


def _moe_kernel(be_ref, nu_ref, first_ref, slot_ref, nxt_ref, x_ref, w1_hbm, w3_hbm, w2_hbm, o_ref,
                w1f, w3f, w2f, w1b, w3b, w2b, sem, *, layer):
    i = pl.program_id(0)

    def fetch(e, slot):
        return [pltpu.make_async_copy(src.at[layer, e], dst.at[slot], sem.at[n, slot])
                for n, (src, dst) in enumerate(((w1_hbm, w1f), (w3_hbm, w3f), (w2_hbm, w2f)))]

    @pl.when((i == 0) & (nu_ref[0] > 0))
    def _():
        for cp in fetch(be_ref[0], 0):
            cp.start()

    @pl.when((first_ref[i] > 0) & (i < nu_ref[0]))
    def _():
        slot = slot_ref[i]
        for cp in fetch(be_ref[i], slot):
            cp.wait()
        w1b[...] = w1f[slot].astype(BF16)
        w3b[...] = w3f[slot].astype(BF16)
        w2b[...] = w2f[slot].astype(BF16)

        @pl.when(nxt_ref[i] >= 0)
        def _():
            for cp in fetch(nxt_ref[i], 1 - slot):
                cp.start()

    @pl.when(i < nu_ref[0])
    def _():
        parts = [h.astype(BF16) for p in range(PLANES) for h in _unpack_rows(x_ref[p])]
        a = sum(jnp.dot(h, w1b[q * SC_ROW:(q + 1) * SC_ROW, :], preferred_element_type=F32)
                for q, h in enumerate(parts))
        b = sum(jnp.dot(h, w3b[q * SC_ROW:(q + 1) * SC_ROW, :], preferred_element_type=F32)
                for q, h in enumerate(parts))
        hmid = (jax.nn.silu(a) * b).astype(BF16)
        _pack_planes(jnp.dot(hmid, w2b[...], preferred_element_type=F32), o_ref)

    @pl.when(i >= nu_ref[0])
    def _():
        o_ref[...] = jnp.zeros_like(o_ref)


def _moe_mlp(blk_expert, n_used, xs, w1, w3, w2, layer):
    n_rows = xs.shape[1]
    d, f = w1.shape[2:]
    nblk = n_rows // MOE_ROWS
    rows = lambda i, *_: (0, i, 0)
    idx = jnp.arange(nblk, dtype=jnp.int32)
    first = (idx < n_used[0]) & ((idx == 0) | (blk_expert != jnp.roll(blk_expert, 1)))
    slot = (jnp.cumsum(first.astype(jnp.int32)) - 1) % 2
    nxt_first = lax.cummin(jnp.where(first, idx, nblk)[::-1])[::-1]
    nxt_first = jnp.concatenate([nxt_first[1:], jnp.full((1,), nblk, jnp.int32)])
    nxt = jnp.where(nxt_first < nblk, blk_expert[jnp.minimum(nxt_first, nblk - 1)], -1)
    hbm = pl.BlockSpec(memory_space=pl.ANY)
    return pl.pallas_call(
        functools.partial(_moe_kernel, layer=layer),
        out_shape=jax.ShapeDtypeStruct((PLANES, n_rows, SC_ROW), F32),
        grid_spec=pltpu.PrefetchScalarGridSpec(
            num_scalar_prefetch=5,
            grid=(nblk,),
            in_specs=[pl.BlockSpec((PLANES, MOE_ROWS, SC_ROW), rows), hbm, hbm, hbm],
            out_specs=pl.BlockSpec((PLANES, MOE_ROWS, SC_ROW), rows),
            scratch_shapes=[pltpu.VMEM((2, d, f), F32), pltpu.VMEM((2, d, f), F32), pltpu.VMEM((2, f, d), F32),
                            pltpu.VMEM((d, f), BF16), pltpu.VMEM((d, f), BF16), pltpu.VMEM((f, d), BF16),
                            pltpu.SemaphoreType.DMA((3, 2))]),
        compiler_params=_cparams("arbitrary"),
        name="moe",
    )(blk_expert, n_used, first.astype(jnp.int32), slot.astype(jnp.int32), nxt.astype(jnp.int32), xs, w1, w3, w2)


def _combine_kernel(x_ref, pk_ref, info_ref, g_ref, o_ref):
    o_ref[0] = _combined(x_ref, pk_ref, info_ref, g_ref)


def _combine(x, picked, info, gate, row0):
    b, t, d = x.shape
    tm = min(256, t)
    off = row0 // tm
    tok = lambda bb, i: (off + bb * (t // tm) + i, 0)
    tok3 = lambda bb, i: (0, off + bb * (t // tm) + i, 0)
    return pl.pallas_call(
        _combine_kernel,
        out_shape=jax.ShapeDtypeStruct((b, t, d), F32),
        grid=(b, t // tm),
        in_specs=[pl.BlockSpec((1, tm, d), lambda bb, i: (bb, i, 0)),
                  pl.BlockSpec((PLANES * TOP_K, tm, SC_ROW), tok3),
                  pl.BlockSpec((tm, ROUTER_LANES), tok),
                  pl.BlockSpec((1, 1, d), lambda bb, i: (bb, 0, 0))],
        out_specs=pl.BlockSpec((1, tm, d), lambda bb, i: (bb, i, 0)),
        compiler_params=_cparams("arbitrary", "arbitrary"),
        name="combine",
    )(x, picked, info, gate)


def _hier_moe(h2p, sel, counts, w1, w3, w2, layer):
    n_tok = h2p.shape[1]
    info, meta = _route(sel, counts)
    dest = info[:, 0:TOP_K].astype(jnp.int32)
    pad_ends = meta[2, N_GROUPS:N_GROUPS + N_EXPERTS].astype(jnp.int32)
    nblk = -(-(n_tok * TOP_K) // MOE_ROWS) + N_EXPERTS
    n_rows = nblk * MOE_ROWS
    blk_start = jnp.arange(nblk, dtype=jnp.int32) * MOE_ROWS
    blk_expert = jnp.minimum(jnp.sum((pad_ends[None, :] <= blk_start[:, None]).astype(jnp.int32), axis=1),
                             N_EXPERTS - 1)
    n_used = pad_ends[-1:] // MOE_ROWS
    slot = [jnp.concatenate([p * n_rows + dest[:, s] for p in range(PLANES)]) for s in range(TOP_K)]
    xs = _sc_scatter2(h2p.reshape(PLANES * n_tok, SC_ROW), slot[0], slot[1], PLANES * n_rows)
    out = _moe_mlp(blk_expert, n_used, xs.reshape(PLANES, n_rows, SC_ROW), w1, w3, w2, layer)
    idx_all = jnp.concatenate([p * n_rows + dest[:, s] for p in range(PLANES) for s in range(TOP_K)])
    picked = _sc_gather(out.reshape(PLANES * n_rows, SC_ROW), idx_all)
    return picked.reshape(PLANES * TOP_K, n_tok, SC_ROW), info


def _layer(layer, x, xc, pend, c, c_ctx, lb, w_mod, b_mod, norm1_w, w_in, w_s, b_s, q_norm_w, k_norm_w, hgrn_norm_w, w_out,
           norm2_w, w_grp, b_grp, w_exp, b_exp, w1, w3, w2, ctx_out):
    b, t, d = x.shape
    lc = xc.shape[1]
    cc = jnp.zeros((MOD_ROWS, d), F32).at[:b].set(c).at[b].set(c_ctx)
    mod = _mod(cc, w_mod, b_mod, layer)
    sh1, sc1, g1, sh2, sc2, g2 = [m[:, None, :] for m in jnp.split(mod[:b], 6, axis=-1)]
    mod_c = [jnp.broadcast_to(m[None, None, :], (b, 1, d)) for m in jnp.split(mod[b], 6)]

    w_in_bf = w_in.astype(BF16)
    scale = LOG2E * HEAD_DIM ** -0.5
    q_tabs = _rope_tables(t, q_norm_w, scale, B_WIDTH, True)
    k_tabs = _rope_tables(t, k_norm_w, 1.0, B_KV_WIDTH, True)
    qc_tabs = _rope_tables(lc, q_norm_w, scale, B_WIDTH, False)
    kc_tabs = _rope_tables(lc, k_norm_w, 1.0, B_KV_WIDTH, False)
    ws_bf = w_s.astype(BF16)
    bias2d = jnp.repeat(b_s.T, HEAD_DIM, axis=1)
    res = _inproj(x, norm1_w * (1.0 + sc1), sh1, w_in_bf, q_tabs, k_tabs, ws_bf, bias2d,
                  None if pend is None else pend[:2] + (pend[2], 0))
    res_c = _inproj(xc, norm1_w * (1.0 + mod_c[1]), mod_c[0], w_in_bf, qc_tabs, kc_tabs, ws_bf, bias2d,
                    None if pend is None else pend[:2] + (pend[3], b * t))
    if pend is not None:
        x, xc, res, res_c = res[0], res_c[0], res[1:], res_c[1:]
    ya, q, k, v, zc, g = res
    ya_c, q_c, k_c, v_c, zc_c, g_c = res_c

    bound = LOG2E * HEAD_DIM ** 0.5 * jnp.max(jnp.abs(q_norm_w)) * jnp.max(jnp.abs(k_norm_w)) * 1.02
    shift = jnp.ceil(bound)
    attn_flag = jnp.stack([(shift <= SAFE_SHIFT).astype(jnp.int32), shift.astype(jnp.int32)])
    yb = _attn(attn_flag, q, [(k, v), (k_c, v_c)])

    pos = lb > 0.0
    log_lb = jnp.log(jnp.where(pos, lb, 1.0))
    lbc = jnp.stack([1.0 - lb, jnp.log1p(-lb), log_lb, pos.astype(F32), jnp.where(pos, -log_lb, 1e30)], axis=1)
    lbc = jnp.concatenate([lbc, jnp.zeros((2, 3, C_WIDTH), F32)], axis=1)
    o, o_c = _hgrn(zc, zc_c, lbc, ctx_out)

    w_out_bf = w_out.astype(BF16)
    hw = jnp.tile(hgrn_norm_w, C_HEADS)[None, :]
    wr = jnp.zeros((d, ROUTER_LANES), F32).at[:, :N_GROUPS].set(w_grp).at[
        :, N_GROUPS:N_GROUPS + N_EXPERTS].set(w_exp)
    wr_hi = wr.astype(BF16)
    wr = jnp.concatenate([wr_hi, (wr - wr_hi.astype(F32)).astype(BF16)], axis=1)
    br = jnp.zeros((1, ROUTER_LANES), F32).at[0, :N_GROUPS].set(b_grp).at[
        0, N_GROUPS:N_GROUPS + N_EXPERTS].set(b_exp)
    x, h2, sel, cnt = _outproj(x, ya, yb, o, g, w_out_bf, g1, norm2_w * (1.0 + sc2), sh2, hw, wr, br)
    if ctx_out:
        yb_c = _attn(attn_flag, q_c, [(k_c, v_c)])
        xc, h2c, sel_c, cnt_c = _outproj(xc, ya_c, yb_c, o_c, g_c, w_out_bf, mod_c[2],
                                norm2_w * (1.0 + mod_c[4]), mod_c[3], hw, wr, br)
        tokens = jnp.concatenate([h2.reshape(PLANES, -1, SC_ROW), h2c.reshape(PLANES, -1, SC_ROW)], axis=1)
        sel_all = jnp.concatenate([sel.reshape(-1, ROUTER_LANES), sel_c.reshape(-1, ROUTER_LANES)], axis=0)
        picked, info = _hier_moe(tokens, sel_all, cnt + cnt_c, w1, w3, w2, layer)
        return x, xc, (picked, info, g2, mod_c[5])
    picked, info = _hier_moe(h2.reshape(PLANES, -1, SC_ROW), sel.reshape(-1, ROUTER_LANES), cnt, w1, w3, w2, layer)
    return x, xc, (picked, info, g2, None)


def kernel(x, c, ctx, c_ctx, w_mod, b_mod, norm1_w, w_in, w_s, b_s, q_norm_w, k_norm_w, hgrn_lb_logits,
           hgrn_norm_w, w_out, norm2_w, w_grp, b_grp, w_exp, b_exp, w1, w3, w2):
    depth = w_mod.shape[0]
    lb_sm = jax.nn.softmax(hgrn_lb_logits.astype(F32), axis=0)
    lb = jnp.cumsum(lb_sm, axis=0) - lb_sm[0]
    xc = ctx
    pend = None
    for l in range(depth):
        x, xc, pend = _layer(l, x, xc, pend, c, c_ctx, lb[l], w_mod, b_mod[l], norm1_w[l], w_in[l], w_s[l], b_s[l],
                             q_norm_w[l], k_norm_w[l], hgrn_norm_w[l], w_out[l], norm2_w[l], w_grp[l], b_grp[l],
                             w_exp[l], b_exp[l], w1, w3, w2, ctx_out=(l < depth - 1))
    return _combine(x, pend[0], pend[1], pend[2], 0)
```

```python
import functools

import jax
import jax.numpy as jnp
from jax import lax
from jax.experimental import pallas as pl
from jax.experimental.pallas import tpu as pltpu
from jax.experimental.pallas import tpu_sc as plsc

F32 = jnp.float32
BF16 = jnp.bfloat16

D_MODEL = 1024
HEAD_DIM = 64
GRID_W = 64
EPS = 1e-6
ROPE_BASE = 10000.0
A_WIDTH = D_MODEL // 4
A_HEADS = A_WIDTH // HEAD_DIM
A_CHUNK = 128
B_WIDTH = D_MODEL // 2
B_HEADS = B_WIDTH // HEAD_DIM
B_KV_HEADS = 2
B_GROUP = B_HEADS // B_KV_HEADS
B_KV_WIDTH = B_KV_HEADS * HEAD_DIM
C_WIDTH = D_MODEL // 4
C_HEADS = C_WIDTH // HEAD_DIM
OFF_B = 2 * A_WIDTH
OFF_KV = OFF_B + B_WIDTH
OFF_V = OFF_KV + B_KV_WIDTH
OFF_C = OFF_KV + 2 * B_KV_WIDTH
OFF_G = OFF_C + 4 * C_WIDTH
IN_WIDTH = OFF_G + C_WIDTH
N_GROUPS = 4
EXPERTS_PER_GROUP = 8
N_EXPERTS = N_GROUPS * EXPERTS_PER_GROUP
TOP_K = 2
D_FF_EXPERT = D_MODEL // 2

MOD_ROWS = 16
ROUTER_LANES = 128
PROJ_ROWS = 512
HGRN_BLOCK = 32
HGRN_GROUP = 8
LOG2E = 1.4426950408889634
ATTN_ROWS = 256
ATTN_KEYS = 512
SAFE_SHIFT = 60
SAFE_DECAY = 80.0
MOE_ROWS = 512
ROUTE_ROWS = 512
SC_WINDOW = 128
SC_ROW = 256
PLANES = D_MODEL // (2 * SC_ROW)
VMEM_LIMIT = 48 * 1024 * 1024


def _cparams(*sem):
    return pltpu.CompilerParams(dimension_semantics=sem, vmem_limit_bytes=VMEM_LIMIT)


def _head_ones(n, dtype):
    r = lax.broadcasted_iota(jnp.int32, (n, n), 0) >> 6
    c = lax.broadcasted_iota(jnp.int32, (n, n), 1) >> 6
    return (r == c).astype(dtype)


def _head_sum(x, ones_bd):
    return jnp.dot(x.astype(BF16), ones_bd, preferred_element_type=F32)


def _head_rms(x, ones_bd):
    return x * lax.rsqrt(_head_sum(x * x, ones_bd) * (1.0 / HEAD_DIM) + EPS)


def _pack_rows(y):
    bits = lax.bitcast_convert_type(y.astype(BF16).astype(F32), jnp.uint32)
    half = y.shape[1] // 2
    return lax.bitcast_convert_type(bits[:, :half] | (bits[:, half:] >> 16), F32)


def _unpack_rows(w):
    bits = lax.bitcast_convert_type(w, jnp.uint32)
    hi = lax.bitcast_convert_type(bits & jnp.uint32(0xFFFF0000), F32)
    lo = lax.bitcast_convert_type(bits << 16, F32)
    return hi, lo


def _pack_planes(y, ref, lead=()):
    for p in range(PLANES):
        ref[(p,) + lead] = _pack_rows(y[:, 2 * p * SC_ROW:(2 * p + 2) * SC_ROW])


def _mod_kernel(c_ref, w_ref, b_ref, o_ref):
    a = jax.nn.silu(c_ref[...])
    w = w_ref[0]
    a_hi, w_hi = a.astype(BF16), w.astype(BF16)
    a_lo, w_lo = (a - a_hi.astype(F32)).astype(BF16), (w - w_hi.astype(F32)).astype(BF16)
    o_ref[...] = (jnp.dot(a_hi, w_hi, preferred_element_type=F32) + jnp.dot(a_hi, w_lo, preferred_element_type=F32)
                  + jnp.dot(a_lo, w_hi, preferred_element_type=F32) + b_ref[...])


def _mod(cc, w_mod, b_mod, layer):
    n = w_mod.shape[2]
    tn = 1536
    return pl.pallas_call(
        _mod_kernel,
        out_shape=jax.ShapeDtypeStruct((MOD_ROWS, n), F32),
        grid=(n // tn,),
        in_specs=[pl.BlockSpec((MOD_ROWS, D_MODEL), lambda j: (0, 0)),
                  pl.BlockSpec((1, D_MODEL, tn), lambda j: (layer, 0, j)),
                  pl.BlockSpec((1, tn), lambda j: (0, j))],
        out_specs=pl.BlockSpec((MOD_ROWS, tn), lambda j: (0, j)),
        compiler_params=_cparams("arbitrary"),
        name="mod",
    )(cc, w_mod, b_mod.reshape(1, n))


def _rope(xn, c_ref, sp_ref, sm_ref):
    w = xn.shape[-1]
    return (xn * c_ref[...] + pltpu.roll(xn, 16, 1) * sp_ref[...]
            + pltpu.roll(xn, w - 16, 1) * sm_ref[...])


def _combined(x_ref, pk_ref, info_ref, g_ref):
    info = info_ref[...]
    g1 = info[:, 2:3]
    g2 = info[:, 3:4]
    parts = []
    for p in range(PLANES):
        hi1, lo1 = _unpack_rows(pk_ref[TOP_K * p])
        hi2, lo2 = _unpack_rows(pk_ref[TOP_K * p + 1])
        parts += [g1 * hi1 + g2 * hi2, g1 * lo1 + g2 * lo2]
    return x_ref[0] + g_ref[0] * jnp.concatenate(parts, axis=1)


def _gmlp(z, ws_ref, bias_ref):
    gz = jax.nn.gelu(z)
    u = gz[:, :A_WIDTH]
    vn = _head_rms(gz[:, A_WIDTH:], _head_ones(A_WIDTH, BF16))
    lane_head = lax.broadcasted_iota(jnp.int32, vn.shape, 1) >> 6
    acc = bias_ref[...]
    for hh in range(A_HEADS):
        vh = jnp.where(lane_head == hh, vn, 0.0).astype(BF16)
        acc = acc + jnp.dot(ws_ref[hh], vh, preferred_element_type=F32)
    return u * acc


def _inproj_kernel(*refs, pending):
    if pending:
        (x_ref, pk_ref, info_ref, g2_ref, mul_ref, add_ref, w_ref, qc_ref, qsp_ref, qsm_ref, kc_ref, ksp_ref, ksm_ref,
         ws_ref, bias_ref, xo_ref, ya_ref, q_ref, k_ref, v_ref, zc_ref, g_ref) = refs
        x = _combined(x_ref, pk_ref, info_ref, g2_ref)
        xo_ref[0] = x
    else:
        (x_ref, mul_ref, add_ref, w_ref, qc_ref, qsp_ref, qsm_ref, kc_ref, ksp_ref, ksm_ref,
         ws_ref, bias_ref, ya_ref, q_ref, k_ref, v_ref, zc_ref, g_ref) = refs
        x = x_ref[0]
    ms = jnp.mean(x * x, axis=-1, keepdims=True)
    h = x * lax.rsqrt(ms + EPS) * mul_ref[0] + add_ref[0]
    y = jnp.dot(h.astype(BF16), w_ref[...], preferred_element_type=F32)
    for c0 in range(0, x.shape[0], A_CHUNK):
        ya_ref[0, c0:c0 + A_CHUNK, :] = _gmlp(y[c0:c0 + A_CHUNK, :OFF_B], ws_ref, bias_ref).astype(BF16)
    qn = _head_rms(y[:, OFF_B:OFF_KV], _head_ones(B_WIDTH, BF16))
    q_ref[0] = _rope(qn, qc_ref, qsp_ref, qsm_ref).astype(BF16)
    kn = _head_rms(y[:, OFF_KV:OFF_V], _head_ones(B_KV_WIDTH, BF16))
    k_ref[0] = _rope(kn, kc_ref, ksp_ref, ksm_ref).astype(BF16)
    v_ref[0] = y[:, OFF_V:OFF_C].astype(BF16)
    zc_ref[0] = y[:, OFF_C:OFF_G]
    g_ref[0] = y[:, OFF_G:]


def _inproj(x, mul, add, w_bf, q_tabs, k_tabs, ws_bf, bias2d, pending=None):
    b, t, d = x.shape
    tm = min(PROJ_ROWS, t)
    row = lambda i, bb: (bb, i, 0)
    vec = lambda i, bb: (bb, 0, 0)
    tab = lambda i, bb: (i, 0)
    widths = (A_WIDTH, B_WIDTH, B_KV_WIDTH, B_KV_WIDTH, 4 * C_WIDTH, C_WIDTH)
    dtypes = (BF16, BF16, BF16, BF16, F32, F32)
    pre_specs, pre_args = [], []
    if pending is not None:
        picked, info, gate, row0 = pending
        off = row0 // tm
        widths, dtypes = (d,) + widths, (F32,) + dtypes
        pre_specs = [pl.BlockSpec((PLANES * TOP_K, tm, SC_ROW), lambda i, bb: (0, off + bb * (t // tm) + i, 0)),
                     pl.BlockSpec((tm, ROUTER_LANES), lambda i, bb: (off + bb * (t // tm) + i, 0)),
                     pl.BlockSpec((1, 1, d), vec)]
        pre_args = [picked, info, gate]
    return pl.pallas_call(
        functools.partial(_inproj_kernel, pending=pending is not None),
        out_shape=[jax.ShapeDtypeStruct((b, t, w), dt) for w, dt in zip(widths, dtypes)],
        grid=(t // tm, b),
        in_specs=[pl.BlockSpec((1, tm, d), row)] + pre_specs
                 + [pl.BlockSpec((1, 1, d), vec),
                    pl.BlockSpec((1, 1, d), vec),
                    pl.BlockSpec((d, IN_WIDTH), lambda i, bb: (0, 0))]
                 + [pl.BlockSpec((tm, B_WIDTH), tab)] * 3
                 + [pl.BlockSpec((tm, B_KV_WIDTH), tab)] * 3
                 + [pl.BlockSpec((A_HEADS, A_CHUNK, A_CHUNK), lambda i, bb: (0, 0, 0)),
                    pl.BlockSpec((A_CHUNK, A_WIDTH), lambda i, bb: (0, 0))],
        out_specs=[pl.BlockSpec((1, tm, w), row) for w in widths],
        compiler_params=_cparams("arbitrary", "arbitrary"),
        name="inproj",
    )(x, *pre_args, mul, add, w_bf, *q_tabs, *k_tabs, ws_bf, bias2d)


def _rope_tables(t, w, scale, width, rotate):
    ws = w.astype(F32) * scale
    if not rotate:
        c = jnp.broadcast_to(jnp.tile(ws, width // HEAD_DIM)[None, :], (t, width))
        z = jnp.zeros((t, width), F32)
        return c, z, z
    pos = jnp.arange(t)
    row = (pos // GRID_W).astype(F32)
    col = (pos % GRID_W).astype(F32)
    inv_freq = 1.0 / (ROPE_BASE ** (jnp.arange(0, HEAD_DIM // 2, 2, dtype=F32) / (HEAD_DIM // 2)))
    dd = jnp.arange(HEAD_DIM)
    axis = dd // 32
    half = (dd % 32) // 16
    ang = jnp.where(axis[None, :] == 0, row[:, None], col[:, None]) * inv_freq[dd % 16][None, :]
    cos, sin = jnp.cos(ang), jnp.sin(ang)
    c = cos * ws[None, :]
    sm = jnp.where(half[None, :] == 0, -sin * jnp.roll(ws, -16)[None, :], 0.0)
    sp = jnp.where(half[None, :] == 1, sin * jnp.roll(ws, 16)[None, :], 0.0)
    rep = width // HEAD_DIM
    return jnp.tile(c, (1, rep)), jnp.tile(sp, (1, rep)), jnp.tile(sm, (1, rep))


def _attn_kernel(flag_ref, q_ref, *refs, n_seg):
    kv_refs, o_ref = refs[:2 * n_seg], refs[2 * n_seg]
    tq = q_ref.shape[1]
    dh = HEAD_DIM

    def heads(j):
        q4 = jnp.concatenate([q_ref[0, :, (B_GROUP * j + gg) * dh:(B_GROUP * j + gg + 1) * dh]
                              for gg in range(B_GROUP)], axis=0)
        ks, vs = [], []
        for sg in range(n_seg):
            s_len = kv_refs[2 * sg].shape[1]
            for c0 in range(0, s_len, ATTN_KEYS):
                c1 = min(c0 + ATTN_KEYS, s_len)
                ks.append(kv_refs[2 * sg][0, c0:c1, j * dh:(j + 1) * dh])
                vs.append(kv_refs[2 * sg + 1][0, c0:c1, j * dh:(j + 1) * dh])
        return q4, ks, vs

    def scores(q4, ks):
        return [lax.dot_general(kk, q4, (((1,), (1,)), ((), ())), preferred_element_type=F32) for kk in ks]

    def finish(j, ps, vs):
        l = functools.reduce(jnp.add, [jnp.sum(p, axis=0, keepdims=True) for p in ps])
        acc = sum(lax.dot_general(vv, p.astype(BF16), (((0,), (0,)), ((), ())), preferred_element_type=F32)
                  for p, vv in zip(ps, vs))
        o = (acc / l).T
        for gg in range(B_GROUP):
            hh = B_GROUP * j + gg
            o_ref[0, :, hh * dh:(hh + 1) * dh] = o[gg * tq:(gg + 1) * tq].astype(BF16)

    @pl.when(flag_ref[0] > 0)
    def _():
        shift = flag_ref[1].astype(F32)
        for j in range(B_KV_HEADS):
            q4, ks, vs = heads(j)
            finish(j, [jnp.exp2(s - shift) for s in scores(q4, ks)], vs)

    @pl.when(flag_ref[0] <= 0)
    def _():
        for j in range(B_KV_HEADS):
            q4, ks, vs = heads(j)
            ss = scores(q4, ks)
            m = functools.reduce(jnp.maximum, [jnp.max(s, axis=0, keepdims=True) for s in ss])
            finish(j, [jnp.exp2(s - m) for s in ss], vs)


def _attn(flag, q, kv_segs):
    b, t, w = q.shape
    tq = min(ATTN_ROWS, t)
    n_seg = len(kv_segs)
    kv_flat, kv_specs = [], []
    for kk, vv in kv_segs:
        s_len, kw = kk.shape[1:]
        kv_flat += [kk, vv]
        kv_specs += [pl.BlockSpec((1, s_len, kw), lambda bb, i, fl: (bb, 0, 0))] * 2
    return pl.pallas_call(
        functools.partial(_attn_kernel, n_seg=n_seg),
        out_shape=jax.ShapeDtypeStruct((b, t, w), BF16),
        grid_spec=pltpu.PrefetchScalarGridSpec(
            num_scalar_prefetch=1,
            grid=(b, t // tq),
            in_specs=[pl.BlockSpec((1, tq, w), lambda bb, i, fl: (bb, i, 0))] + kv_specs,
            out_specs=pl.BlockSpec((1, tq, w), lambda bb, i, fl: (bb, i, 0))),
        compiler_params=_cparams("arbitrary", "arbitrary"),
        name="attn",
    )(flag, q, *kv_flat)


def _scan_rows(x, reverse):
    n = x.shape[0]
    rows = lax.broadcasted_iota(jnp.int32, x.shape, 0)
    sh = 1
    while sh < n:
        if reverse:
            x = x + jnp.where(rows < n - sh, pltpu.roll(x, n - sh, 0), 0.0)
        else:
            x = x + jnp.where(rows >= sh, pltpu.roll(x, sh, 0), 0.0)
        sh *= 2
    return x


def _stack_heads(x, lane_head):
    return jnp.concatenate([jnp.where(lane_head == hh, x, 0.0) for hh in range(C_HEADS)], axis=0)


def _hgrn_kernel(z_ref, zc_ref, lbc_ref, *refs, ctx_out):
    if ctx_out:
        o_ref, oc_ref, st_ref, kx_ref, bx_ref, vx_ref, flag_ref = refs
    else:
        o_ref, st_ref, kx_ref, bx_ref, vx_ref, flag_ref = refs
        oc_ref = None
    n = C_WIDTH
    nb = HGRN_BLOCK
    nblk_c = zc_ref.shape[1] // nb
    nblk_l = z_ref.shape[1] // nb
    ones_bd = _head_ones(n, BF16)
    rows = lax.broadcasted_iota(jnp.int32, (nb, n), 0)
    lane_head = lax.broadcasted_iota(jnp.int32, (nb, n), 1) >> 6
    lane_head64 = lax.broadcasted_iota(jnp.int32, (HEAD_DIM, n), 1) >> 6
    low_half = (lax.broadcasted_iota(jnp.int32, (HEAD_DIM, 2 * HEAD_DIM), 1) < HEAD_DIM)
    sc_t = lax.broadcasted_iota(jnp.int32, (nb, C_HEADS * nb), 0)
    sc_s = lax.broadcasted_iota(jnp.int32, (nb, C_HEADS * nb), 1) & (nb - 1)

    def gates(z, d):
        one_m_lb = lbc_ref[d, 0:1, :]
        log1m_lb = lbc_ref[d, 1:2, :]
        log_lb = lbc_ref[d, 2:3, :]
        lb_pos = lbc_ref[d, 3:4, :] > 0.5
        soft = jnp.log(1.0 + jnp.exp(-jnp.abs(z)))
        log_rest = log1m_lb + (jnp.minimum(z, 0.0) - soft)
        lse = jnp.maximum(log_lb, log_rest) + jnp.log(1.0 + jnp.exp(-jnp.abs(log_lb - log_rest)))
        return jnp.where(lb_pos, lse, log_rest), one_m_lb * jnp.exp(jnp.minimum(-z, 0.0) - soft)

    def group_blocks(nblk, i):
        grp = HGRN_GROUP if nblk % HGRN_GROUP == 0 else 1
        fwd = [i * grp + gg for gg in range(grp)]
        return grp, fwd, [nblk - 1 - blk for blk in fwd]

    def block_bound(src_ref, blk, d):
        zz = src_ref[0, pl.ds(pl.multiple_of(blk * nb, nb), nb), (1 + d) * n:(2 + d) * n]
        step_bound = jnp.minimum(lbc_ref[d, 4:5, :],
                                 jnp.maximum(-zz, 0.0) + (jnp.log(2.0) - lbc_ref[d, 1:2, :]))
        return jnp.sum(step_bound, axis=0, keepdims=True)

    def flag_groups(src_ref, nblk, base):
        def body(i, carry):
            _, fwd, bwd = group_blocks(nblk, i)
            worst = functools.reduce(jnp.maximum, [block_bound(src_ref, blk, d)
                                                   for d, blks in ((0, fwd), (1, bwd)) for blk in blks])
            flag_ref[base + i] = (jnp.max(worst) <= SAFE_DECAY).astype(jnp.int32)
            return carry
        lax.fori_loop(0, nblk // group_blocks(nblk, 0)[0], body, 0)

    def worst_bound(src_ref, nblk):
        def body(blk, worst):
            return jnp.maximum(worst, jnp.maximum(block_bound(src_ref, blk, 0), block_bound(src_ref, blk, 1)))
        return lax.fori_loop(0, nblk, body, jnp.zeros((1, n), F32))

    n_grp_c = nblk_c // group_blocks(nblk_c, 0)[0]
    all_safe = jnp.max(jnp.maximum(worst_bound(zc_ref, nblk_c), worst_bound(z_ref, nblk_l))) <= SAFE_DECAY
    st_ref[...] = jnp.zeros_like(st_ref)
    o_ref[...] = jnp.zeros_like(o_ref)
    if ctx_out:
        oc_ref[...] = jnp.zeros_like(oc_ref)

    def step(src_ref, dst_ref, blk, d, fast):
        reverse = d == 1
        r0 = pl.multiple_of(blk * nb, nb)
        v = src_ref[0, pl.ds(r0, nb), 3 * n:4 * n]
        log_f, k = gates(src_ref[0, pl.ds(r0, nb), (1 + d) * n:(2 + d) * n], d)
        bc = _scan_rows(log_f, reverse)
        edge = 0 if reverse else nb - 1
        b_edge = bc[edge:edge + 1, :]
        st = st_ref[d]
        v_bf = v.astype(BF16)

        if dst_ref is not None:
            q = jax.nn.silu(src_ref[0, pl.ds(r0, nb), 0:n])
            qt = (q * jnp.exp(bc)).astype(BF16)
            o = lax.dot_general(qt, _stack_heads(st, lane_head64).astype(BF16), (((1,), (1,)), ((), ())),
                                preferred_element_type=F32)

            def intra_fast():
                kt = _stack_heads(k * jnp.exp(-bc), lane_head).astype(BF16)
                sc = lax.dot_general(qt, kt, (((1,), (1,)), ((), ())), preferred_element_type=F32)
                keep = (sc_s >= sc_t) if reverse else (sc_s <= sc_t)
                sc = jnp.where(keep, sc, 0.0).astype(BF16)
                return jnp.dot(sc, _stack_heads(v, lane_head).astype(BF16), preferred_element_type=F32)

            def intra_exact():
                kx_ref[d] = k
                bx_ref[d] = bc
                vx_ref[d] = v

                def sbody(s, acc):
                    keep = (rows <= s) if reverse else (rows >= s)
                    e = jnp.exp(jnp.where(keep, bc - bx_ref[d, pl.ds(s, 1), :], 0.0))
                    p = jnp.where(keep, q * e * kx_ref[d, pl.ds(s, 1), :], 0.0)
                    sc = jnp.dot(p.astype(BF16), ones_bd, preferred_element_type=F32)
                    return acc + sc * vx_ref[d, pl.ds(s, 1), :]

                return lax.fori_loop(0, nb, sbody, jnp.zeros((nb, n), F32))

            o = o + (intra_fast() if fast else intra_exact())
            dst_ref[0, pl.ds(r0, nb), :] += o

        kd = (k * jnp.exp(b_edge - bc)).astype(BF16)
        full = lax.dot_general(v_bf, kd, (((0,), (0,)), ((), ())), preferred_element_type=F32)
        upd = jnp.concatenate(
            [jnp.where(low_half,
                       full[(2 * c) * HEAD_DIM:(2 * c + 1) * HEAD_DIM, 2 * c * HEAD_DIM:(2 * c + 2) * HEAD_DIM],
                       full[(2 * c + 1) * HEAD_DIM:(2 * c + 2) * HEAD_DIM, 2 * c * HEAD_DIM:(2 * c + 2) * HEAD_DIM])
             for c in range(C_HEADS // 2)], axis=1)
        st_ref[d] = st * jnp.exp(b_edge) + upd

    def run(src_ref, dst_ref, nblk, base, per_group):
        def body(i, carry):
            _, fwd, bwd = group_blocks(nblk, i)

            def group(fast):
                for bf, bb in zip(fwd, bwd):
                    step(src_ref, dst_ref, bf, 0, fast)
                    step(src_ref, dst_ref, bb, 1, fast)

            if per_group:
                safe = flag_ref[base + i]
                pl.when(safe > 0)(functools.partial(group, True))
                pl.when(safe <= 0)(functools.partial(group, False))
            else:
                group(True)
            return carry
        lax.fori_loop(0, nblk // group_blocks(nblk, 0)[0], body, 0)

    @pl.when(all_safe)
    def _():
        run(zc_ref, oc_ref, nblk_c, 0, False)
        run(z_ref, o_ref, nblk_l, n_grp_c, False)

    @pl.when(jnp.logical_not(all_safe))
    def _():
        flag_groups(zc_ref, nblk_c, 0)
        flag_groups(z_ref, nblk_l, n_grp_c)
        run(zc_ref, oc_ref, nblk_c, 0, True)
        run(z_ref, o_ref, nblk_l, n_grp_c, True)


def _hgrn(zc, zc_c, lbc, ctx_out):
    b, t, w = zc.shape
    lc = zc_c.shape[1]
    n = C_WIDTH
    row = lambda bb: (bb, 0, 0)
    out_shape = [jax.ShapeDtypeStruct((b, t, n), F32)]
    out_specs = [pl.BlockSpec((1, t, n), row)]
    if ctx_out:
        out_shape.append(jax.ShapeDtypeStruct((b, lc, n), F32))
        out_specs.append(pl.BlockSpec((1, lc, n), row))
    res = pl.pallas_call(
        functools.partial(_hgrn_kernel, ctx_out=ctx_out),
        out_shape=out_shape,
        grid=(b,),
        in_specs=[pl.BlockSpec((1, t, w), row),
                  pl.BlockSpec((1, lc, w), row),
                  pl.BlockSpec((2, 8, n), lambda bb: (0, 0, 0))],
        out_specs=out_specs,
        scratch_shapes=[pltpu.VMEM((2, HEAD_DIM, n), F32)]
                       + [pltpu.VMEM((2, HGRN_BLOCK, n), F32)] * 3
                       + [pltpu.SMEM(((t + lc) // HGRN_BLOCK,), jnp.int32)],
        compiler_params=_cparams("arbitrary"),
        name="hgrn",
    )(zc, zc_c, lbc)
    return (res[0], res[1]) if ctx_out else (res[0], None)


def _select_experts(lg):
    lane = lax.broadcasted_iota(jnp.int32, lg.shape, 1)
    lane_f = lane.astype(F32)
    neg = -jnp.inf
    gl = jnp.where(lane < N_GROUPS, lg, neg)
    gmax = jnp.max(gl, axis=1, keepdims=True)
    grp = jnp.min(jnp.where(gl == gmax, lane_f, float(ROUTER_LANES)), axis=1, keepdims=True).astype(jnp.int32)
    p_grp = 1.0 / jnp.sum(jnp.exp(gl - gmax), axis=1, keepdims=True)
    in_grp = (lane >= N_GROUPS) & (lane < N_GROUPS + N_EXPERTS) & (((lane - N_GROUPS) >> 3) == grp)
    el = jnp.where(in_grp, lg, neg)
    v1 = jnp.max(el, axis=1, keepdims=True)
    i1 = jnp.min(jnp.where(el == v1, lane_f, float(ROUTER_LANES)), axis=1, keepdims=True)
    el2 = jnp.where(lane_f == i1, neg, el)
    v2 = jnp.max(el2, axis=1, keepdims=True)
    i2 = jnp.min(jnp.where(el2 == v2, lane_f, float(ROUTER_LANES)), axis=1, keepdims=True)
    rr = jnp.exp(v2 - v1)
    g1 = p_grp / (1.0 + rr)
    g2 = p_grp * rr / (1.0 + rr)
    sel = jnp.where(lane == 0, i1, jnp.where(lane == 1, i2, jnp.where(lane == 2, g1, jnp.where(lane == 3, g2, 0.0))))
    counts = jnp.sum(((lane_f == i1) | (lane_f == i2)).astype(F32), axis=0, keepdims=True)
    return sel, counts


def _outproj_kernel(x_ref, ya_ref, yb_ref, o_ref, g_ref, w_ref, gate_ref, mul_ref, add_ref, hw_ref,
                    wr_ref, br_ref, xo_ref, h2_ref, sel_ref, cnt_ref):
    yc = _head_rms(o_ref[0], _head_ones(C_WIDTH, BF16)) * hw_ref[...] * jax.nn.silu(g_ref[0])
    y = jnp.dot(ya_ref[0], w_ref[0:A_WIDTH, :], preferred_element_type=F32)
    y = y + jnp.dot(yb_ref[0], w_ref[A_WIDTH:A_WIDTH + B_WIDTH, :], preferred_element_type=F32)
    y = y + jnp.dot(yc.astype(BF16), w_ref[A_WIDTH + B_WIDTH:, :], preferred_element_type=F32)
    xn = x_ref[0] + gate_ref[0] * y
    xo_ref[0] = xn
    ms = jnp.mean(xn * xn, axis=-1, keepdims=True)
    h2 = xn * lax.rsqrt(ms + EPS) * mul_ref[0] + add_ref[0]
    _pack_planes(h2, h2_ref, (0,))
    h_hi = h2.astype(BF16)
    h_lo = (h2 - h_hi.astype(F32)).astype(BF16)
    both = jnp.dot(h_hi, wr_ref[...], preferred_element_type=F32)
    lg = (both[:, :ROUTER_LANES] + both[:, ROUTER_LANES:] + br_ref[...]
          + jnp.dot(h_lo, wr_ref[:, 0:ROUTER_LANES], preferred_element_type=F32))
    sel, counts = _select_experts(lg)
    sel_ref[0] = sel

    @pl.when((pl.program_id(0) == 0) & (pl.program_id(1) == 0))
    def _():
        cnt_ref[...] = jnp.zeros_like(cnt_ref)
    cnt_ref[...] += counts


def _outproj(x, ya, yb, o, g, w_bf, gate, mul, add, hw, wr, br):
    b, t, d = x.shape
    tm = min(PROJ_ROWS, t)
    row = lambda bb, i: (bb, i, 0)
    vec = lambda bb, i: (bb, 0, 0)
    const = lambda bb, i: (0, 0)
    return pl.pallas_call(
        _outproj_kernel,
        out_shape=[jax.ShapeDtypeStruct((b, t, d), F32),
                   jax.ShapeDtypeStruct((PLANES, b, t, SC_ROW), F32),
                   jax.ShapeDtypeStruct((b, t, ROUTER_LANES), F32),
                   jax.ShapeDtypeStruct((8, ROUTER_LANES), F32)],
        grid=(b, t // tm),
        in_specs=[pl.BlockSpec((1, tm, d), row),
                  pl.BlockSpec((1, tm, A_WIDTH), row),
                  pl.BlockSpec((1, tm, B_WIDTH), row),
                  pl.BlockSpec((1, tm, C_WIDTH), row),
                  pl.BlockSpec((1, tm, C_WIDTH), row),
                  pl.BlockSpec((d, d), const),
                  pl.BlockSpec((1, 1, d), vec),
                  pl.BlockSpec((1, 1, d), vec),
                  pl.BlockSpec((1, 1, d), vec),
                  pl.BlockSpec((1, C_WIDTH), const),
                  pl.BlockSpec((d, 2 * ROUTER_LANES), const),
                  pl.BlockSpec((1, ROUTER_LANES), const)],
        out_specs=[pl.BlockSpec((1, tm, d), row),
                   pl.BlockSpec((PLANES, 1, tm, SC_ROW), lambda bb, i: (0, bb, i, 0)),
                   pl.BlockSpec((1, tm, ROUTER_LANES), row),
                   pl.BlockSpec((8, ROUTER_LANES), const)],
        compiler_params=_cparams("arbitrary", "arbitrary"),
        name="outproj",
    )(x, ya, yb, o, g, w_bf, gate, mul, add, hw, wr, br)


def _route_kernel(sel_ref, cnt_ref, info_ref, meta_ref, base_ref):
    i = pl.program_id(0)
    tm = sel_ref.shape[0]
    lane = lax.broadcasted_iota(jnp.int32, (tm, ROUTER_LANES), 1)
    lane_f = lane.astype(F32)
    sel = sel_ref[...]
    hit1 = lane_f == sel[:, 0:1]
    hit2 = lane_f == sel[:, 1:2]
    onehot = (hit1 | hit2).astype(F32)

    @pl.when(i == 0)
    def _():
        counts = cnt_ref[...]
        padded = jnp.floor((counts + (MOE_ROWS - 1.0)) * (1.0 / MOE_ROWS)) * MOE_ROWS
        r = lax.broadcasted_iota(jnp.int32, (ROUTER_LANES, ROUTER_LANES), 0)
        c = lax.broadcasted_iota(jnp.int32, (ROUTER_LANES, ROUTER_LANES), 1)
        ends = jnp.dot(padded, (r <= c).astype(F32), preferred_element_type=F32,
                       precision=lax.Precision.HIGHEST)
        base_ref[...] = (ends - padded)[0:1]
        row = lax.broadcasted_iota(jnp.int32, (8, ROUTER_LANES), 0)
        meta_ref[...] = jnp.where(row == 0, counts, jnp.where(row == 1, ends - padded, ends))

    tr = lax.broadcasted_iota(jnp.int32, (tm, tm), 0)
    tc = lax.broadcasted_iota(jnp.int32, (tm, tm), 1)
    before = jnp.dot((tc < tr).astype(BF16), onehot.astype(BF16), preferred_element_type=F32)
    pos = base_ref[...] + before
    d1 = jnp.sum(jnp.where(hit1, pos, 0.0), axis=1, keepdims=True)
    d2 = jnp.sum(jnp.where(hit2, pos, 0.0), axis=1, keepdims=True)
    base_ref[...] += jnp.sum(onehot, axis=0, keepdims=True)
    info_ref[...] = jnp.where(lane == 0, d1, jnp.where(lane == 1, d2, sel))


def _route(sel, counts):
    n = sel.shape[0]
    tm = ROUTE_ROWS if n % ROUTE_ROWS == 0 else ROUTE_ROWS // 2
    return pl.pallas_call(
        _route_kernel,
        out_shape=[jax.ShapeDtypeStruct((n, ROUTER_LANES), F32),
                   jax.ShapeDtypeStruct((8, ROUTER_LANES), F32)],
        grid=(n // tm,),
        in_specs=[pl.BlockSpec((tm, ROUTER_LANES), lambda i: (i, 0)),
                  pl.BlockSpec((8, ROUTER_LANES), lambda i: (0, 0))],
        out_specs=[pl.BlockSpec((tm, ROUTER_LANES), lambda i: (i, 0)),
                   pl.BlockSpec((8, ROUTER_LANES), lambda i: (0, 0))],
        scratch_shapes=[pltpu.VMEM((1, ROUTER_LANES), F32)],
        compiler_params=_cparams("arbitrary"),
        name="route",
    )(sel, counts)


def _sc_mesh():
    return plsc.VectorSubcoreMesh(core_axis_name="c", subcore_axis_name="s")


def _sc_gather(table, idx):
    n = idx.shape[0]
    d = table.shape[1]

    @functools.partial(pl.kernel, out_type=jax.ShapeDtypeStruct((n, d), table.dtype), mesh=_sc_mesh())
    def gather(x_hbm, i_hbm, o_hbm):
        def body(i_vmem, o_vmem):
            pltpu.sync_copy(x_hbm.at[i_vmem.at[0]], o_vmem)

        pltpu.emit_pipeline(
            body,
            grid=(n // SC_WINDOW,),
            in_specs=[pl.BlockSpec((1, SC_WINDOW), lambda i: (0, i))],
            out_specs=[pl.BlockSpec((SC_WINDOW, d), lambda i: (i, 0))],
            core_axis_name=("c", "s"),
            dimension_semantics=(pltpu.PARALLEL,),
        )(i_hbm, o_hbm)

    return gather(table, idx.reshape(1, n))


def _sc_scatter2(rows, idx0, idx1, n_out):
    m, d = rows.shape

    @functools.partial(pl.kernel, out_type=jax.ShapeDtypeStruct((n_out, d), rows.dtype), mesh=_sc_mesh())
    def scatter(x_hbm, i0_hbm, i1_hbm, o_hbm):
        def body(x_vmem, i0_vmem, i1_vmem):
            pltpu.sync_copy(x_vmem, o_hbm.at[i0_vmem.at[0]])
            pltpu.sync_copy(x_vmem, o_hbm.at[i1_vmem.at[0]])

        pltpu.emit_pipeline(
            body,
            grid=(m // SC_WINDOW,),
            in_specs=[pl.BlockSpec((SC_WINDOW, d), lambda i: (i, 0)),
                      pl.BlockSpec((1, SC_WINDOW), lambda i: (0, i)),
                      pl.BlockSpec((1, SC_WINDOW), lambda i: (0, i))],
            out_specs=[],
            core_axis_name=("c", "s"),
            dimension_semantics=(pltpu.PARALLEL,),
        )(x_hbm, i0_hbm, i1_hbm)

    return scatter(rows, idx0.reshape(1, m), idx1.reshape(1, m))


def _moe_kernel(be_ref, nu_ref, first_ref, slot_ref, nxt_ref, x_ref, w1_hbm, w3_hbm, w2_hbm, o_ref,
                w1f, w3f, w2f, w1b, w3b, w2b, sem, *, layer):
    i = pl.program_id(0)

    def fetch(e, slot):
        return [pltpu.make_async_copy(src.at[layer, e], dst.at[slot], sem.at[n, slot])
                for n, (src, dst) in enumerate(((w1_hbm, w1f), (w3_hbm, w3f), (w2_hbm, w2f)))]

    @pl.when((i == 0) & (nu_ref[0] > 0))
    def _():
        for cp in fetch(be_ref[0], 0):
            cp.start()

    @pl.when((first_ref[i] > 0) & (i < nu_ref[0]))
    def _():
        slot = slot_ref[i]
        for cp in fetch(be_ref[i], slot):
            cp.wait()
        w1b[...] = w1f[slot].astype(BF16)
        w3b[...] = w3f[slot].astype(BF16)
        w2b[...] = w2f[slot].astype(BF16)

        @pl.when(nxt_ref[i] >= 0)
        def _():
            for cp in fetch(nxt_ref[i], 1 - slot):
                cp.start()

    @pl.when(i < nu_ref[0])
    def _():
        parts = [h.astype(BF16) for p in range(PLANES) for h in _unpack_rows(x_ref[p])]
        a = sum(jnp.dot(h, w1b[q * SC_ROW:(q + 1) * SC_ROW, :], preferred_element_type=F32)
                for q, h in enumerate(parts))
        b = sum(jnp.dot(h, w3b[q * SC_ROW:(q + 1) * SC_ROW, :], preferred_element_type=F32)
                for q, h in enumerate(parts))
        hmid = (jax.nn.silu(a) * b).astype(BF16)
        _pack_planes(jnp.dot(hmid, w2b[...], preferred_element_type=F32), o_ref)

    @pl.when(i >= nu_ref[0])
    def _():
        o_ref[...] = jnp.zeros_like(o_ref)


def _moe_mlp(blk_expert, n_used, xs, w1, w3, w2, layer):
    n_rows = xs.shape[1]
    d, f = w1.shape[2:]
    nblk = n_rows // MOE_ROWS
    rows = lambda i, *_: (0, i, 0)
    idx = jnp.arange(nblk, dtype=jnp.int32)
    first = (idx < n_used[0]) & ((idx == 0) | (blk_expert != jnp.roll(blk_expert, 1)))
    slot = (jnp.cumsum(first.astype(jnp.int32)) - 1) % 2
    nxt_first = lax.cummin(jnp.where(first, idx, nblk)[::-1])[::-1]
    nxt_first = jnp.concatenate([nxt_first[1:], jnp.full((1,), nblk, jnp.int32)])
    nxt = jnp.where(nxt_first < nblk, blk_expert[jnp.minimum(nxt_first, nblk - 1)], -1)
    hbm = pl.BlockSpec(memory_space=pl.ANY)
    return pl.pallas_call(
        functools.partial(_moe_kernel, layer=layer),
        out_shape=jax.ShapeDtypeStruct((PLANES, n_rows, SC_ROW), F32),
        grid_spec=pltpu.PrefetchScalarGridSpec(
            num_scalar_prefetch=5,
            grid=(nblk,),
            in_specs=[pl.BlockSpec((PLANES, MOE_ROWS, SC_ROW), rows), hbm, hbm, hbm],
            out_specs=pl.BlockSpec((PLANES, MOE_ROWS, SC_ROW), rows),
            scratch_shapes=[pltpu.VMEM((2, d, f), F32), pltpu.VMEM((2, d, f), F32), pltpu.VMEM((2, f, d), F32),
                            pltpu.VMEM((d, f), BF16), pltpu.VMEM((d, f), BF16), pltpu.VMEM((f, d), BF16),
                            pltpu.SemaphoreType.DMA((3, 2))]),
        compiler_params=_cparams("arbitrary"),
        name="moe",
    )(blk_expert, n_used, first.astype(jnp.int32), slot.astype(jnp.int32), nxt.astype(jnp.int32), xs, w1, w3, w2)


def _combine_kernel(x_ref, pk_ref, info_ref, g_ref, o_ref):
    o_ref[0] = _combined(x_ref, pk_ref, info_ref, g_ref)


def _combine(x, picked, info, gate, row0):
    b, t, d = x.shape
    tm = min(256, t)
    off = row0 // tm
    tok = lambda bb, i: (off + bb * (t // tm) + i, 0)
    tok3 = lambda bb, i: (0, off + bb * (t // tm) + i, 0)
    return pl.pallas_call(
        _combine_kernel,
        out_shape=jax.ShapeDtypeStruct((b, t, d), F32),
        grid=(b, t // tm),
        in_specs=[pl.BlockSpec((1, tm, d), lambda bb, i: (bb, i, 0)),
                  pl.BlockSpec((PLANES * TOP_K, tm, SC_ROW), tok3),
                  pl.BlockSpec((tm, ROUTER_LANES), tok),
                  pl.BlockSpec((1, 1, d), lambda bb, i: (bb, 0, 0))],
        out_specs=pl.BlockSpec((1, tm, d), lambda bb, i: (bb, i, 0)),
        compiler_params=_cparams("arbitrary", "arbitrary"),
        name="combine",
    )(x, picked, info, gate)


def _hier_moe(h2p, sel, counts, w1, w3, w2, layer):
    n_tok = h2p.shape[1]
    info, meta = _route(sel, counts)
    dest = info[:, 0:TOP_K].astype(jnp.int32)
    pad_ends = meta[2, N_GROUPS:N_GROUPS + N_EXPERTS].astype(jnp.int32)
    nblk = -(-(n_tok * TOP_K) // MOE_ROWS) + N_EXPERTS
    n_rows = nblk * MOE_ROWS
    blk_start = jnp.arange(nblk, dtype=jnp.int32) * MOE_ROWS
    blk_expert = jnp.minimum(jnp.sum((pad_ends[None, :] <= blk_start[:, None]).astype(jnp.int32), axis=1),
                             N_EXPERTS - 1)
    n_used = pad_ends[-1:] // MOE_ROWS
    slot = [jnp.concatenate([p * n_rows + dest[:, s] for p in range(PLANES)]) for s in range(TOP_K)]
    xs = _sc_scatter2(h2p.reshape(PLANES * n_tok, SC_ROW), slot[0], slot[1], PLANES * n_rows)
    out = _moe_mlp(blk_expert, n_used, xs.reshape(PLANES, n_rows, SC_ROW), w1, w3, w2, layer)
    idx_all = jnp.concatenate([p * n_rows + dest[:, s] for p in range(PLANES) for s in range(TOP_K)])
    picked = _sc_gather(out.reshape(PLANES * n_rows, SC_ROW), idx_all)
    return picked.reshape(PLANES * TOP_K, n_tok, SC_ROW), info


def _layer(layer, x, xc, pend, c, c_ctx, lb, w_mod, b_mod, norm1_w, w_in, w_s, b_s, q_norm_w, k_norm_w, hgrn_norm_w, w_out,
           norm2_w, w_grp, b_grp, w_exp, b_exp, w1, w3, w2, ctx_out):
    b, t, d = x.shape
    lc = xc.shape[1]
    cc = jnp.zeros((MOD_ROWS, d), F32).at[:b].set(c).at[b].set(c_ctx)
    mod = _mod(cc, w_mod, b_mod, layer)
    sh1, sc1, g1, sh2, sc2, g2 = [m[:, None, :] for m in jnp.split(mod[:b], 6, axis=-1)]
    mod_c = [jnp.broadcast_to(m[None, None, :], (b, 1, d)) for m in jnp.split(mod[b], 6)]

    w_in_bf = w_in.astype(BF16)
    scale = LOG2E * HEAD_DIM ** -0.5
    q_tabs = _rope_tables(t, q_norm_w, scale, B_WIDTH, True)
    k_tabs = _rope_tables(t, k_norm_w, 1.0, B_KV_WIDTH, True)
    qc_tabs = _rope_tables(lc, q_norm_w, scale, B_WIDTH, False)
    kc_tabs = _rope_tables(lc, k_norm_w, 1.0, B_KV_WIDTH, False)
    ws_bf = w_s.astype(BF16)
    bias2d = jnp.repeat(b_s.T, HEAD_DIM, axis=1)
    res = _inproj(x, norm1_w * (1.0 + sc1), sh1, w_in_bf, q_tabs, k_tabs, ws_bf, bias2d,
                  None if pend is None else pend[:2] + (pend[2], 0))
    res_c = _inproj(xc, norm1_w * (1.0 + mod_c[1]), mod_c[0], w_in_bf, qc_tabs, kc_tabs, ws_bf, bias2d,
                    None if pend is None else pend[:2] + (pend[3], b * t))
    if pend is not None:
        x, xc, res, res_c = res[0], res_c[0], res[1:], res_c[1:]
    ya, q, k, v, zc, g = res
    ya_c, q_c, k_c, v_c, zc_c, g_c = res_c

    bound = LOG2E * HEAD_DIM ** 0.5 * jnp.max(jnp.abs(q_norm_w)) * jnp.max(jnp.abs(k_norm_w)) * 1.02
    shift = jnp.ceil(bound)
    attn_flag = jnp.stack([(shift <= SAFE_SHIFT).astype(jnp.int32), shift.astype(jnp.int32)])
    yb = _attn(attn_flag, q, [(k, v), (k_c, v_c)])

    pos = lb > 0.0
    log_lb = jnp.log(jnp.where(pos, lb, 1.0))
    lbc = jnp.stack([1.0 - lb, jnp.log1p(-lb), log_lb, pos.astype(F32), jnp.where(pos, -log_lb, 1e30)], axis=1)
    lbc = jnp.concatenate([lbc, jnp.zeros((2, 3, C_WIDTH), F32)], axis=1)
    o, o_c = _hgrn(zc, zc_c, lbc, ctx_out)

    w_out_bf = w_out.astype(BF16)
    hw = jnp.tile(hgrn_norm_w, C_HEADS)[None, :]
    wr = jnp.zeros((d, ROUTER_LANES), F32).at[:, :N_GROUPS].set(w_grp).at[
        :, N_GROUPS:N_GROUPS + N_EXPERTS].set(w_exp)
    wr_hi = wr.astype(BF16)
    wr = jnp.concatenate([wr_hi, (wr - wr_hi.astype(F32)).astype(BF16)], axis=1)
    br = jnp.zeros((1, ROUTER_LANES), F32).at[0, :N_GROUPS].set(b_grp).at[
        0, N_GROUPS:N_GROUPS + N_EXPERTS].set(b_exp)
    x, h2, sel, cnt = _outproj(x, ya, yb, o, g, w_out_bf, g1, norm2_w * (1.0 + sc2), sh2, hw, wr, br)
    if ctx_out:
        yb_c = _attn(attn_flag, q_c, [(k_c, v_c)])
        xc, h2c, sel_c, cnt_c = _outproj(xc, ya_c, yb_c, o_c, g_c, w_out_bf, mod_c[2],
                                norm2_w * (1.0 + mod_c[4]), mod_c[3], hw, wr, br)
        tokens = jnp.concatenate([h2.reshape(PLANES, -1, SC_ROW), h2c.reshape(PLANES, -1, SC_ROW)], axis=1)
        sel_all = jnp.concatenate([sel.reshape(-1, ROUTER_LANES), sel_c.reshape(-1, ROUTER_LANES)], axis=0)
        picked, info = _hier_moe(tokens, sel_all, cnt + cnt_c, w1, w3, w2, layer)
        return x, xc, (picked, info, g2, mod_c[5])
    picked, info = _hier_moe(h2.reshape(PLANES, -1, SC_ROW), sel.reshape(-1, ROUTER_LANES), cnt, w1, w3, w2, layer)
    return x, xc, (picked, info, g2, None)


def kernel(x, c, ctx, c_ctx, w_mod, b_mod, norm1_w, w_in, w_s, b_s, q_norm_w, k_norm_w, hgrn_lb_logits,
           hgrn_norm_w, w_out, norm2_w, w_grp, b_grp, w_exp, b_exp, w1, w3, w2):
    depth = w_mod.shape[0]
    lb_sm = jax.nn.softmax(hgrn_lb_logits.astype(F32), axis=0)
    lb = jnp.cumsum(lb_sm, axis=0) - lb_sm[0]
    xc = ctx
    pend = None
    for l in range(depth):
        x, xc, pend = _layer(l, x, xc, pend, c, c_ctx, lb[l], w_mod, b_mod[l], norm1_w[l], w_in[l], w_s[l], b_s[l],
                             q_norm_w[l], k_norm_w[l], hgrn_norm_w[l], w_out[l], norm2_w[l], w_grp[l], b_grp[l],
                             w_exp[l], b_exp[l], w1, w3, w2, ctx_out=(l < depth - 1))
    return _combine(x, pend[0], pend[1], pend[2], 0)
```

```python
import functools

import jax
import jax.numpy as jnp
from jax import lax
from jax.experimental import pallas as pl
from jax.experimental.pallas import tpu as pltpu
from jax.experimental.pallas import tpu_sc as plsc

F32 = jnp.float32
BF16 = jnp.bfloat16

D_MODEL = 1024
HEAD_DIM = 64
GRID_W = 64
EPS = 1e-6
ROPE_BASE = 10000.0
A_WIDTH = D_MODEL // 4
A_HEADS = A_WIDTH // HEAD_DIM
A_CHUNK = 128
B_WIDTH = D_MODEL // 2
B_HEADS = B_WIDTH // HEAD_DIM
B_KV_HEADS = 2
B_GROUP = B_HEADS // B_KV_HEADS
B_KV_WIDTH = B_KV_HEADS * HEAD_DIM
C_WIDTH = D_MODEL // 4
C_HEADS = C_WIDTH // HEAD_DIM
OFF_B = 2 * A_WIDTH
OFF_KV = OFF_B + B_WIDTH
OFF_V = OFF_KV + B_KV_WIDTH
OFF_C = OFF_KV + 2 * B_KV_WIDTH
OFF_G = OFF_C + 4 * C_WIDTH
IN_WIDTH = OFF_G + C_WIDTH
N_GROUPS = 4
EXPERTS_PER_GROUP = 8
N_EXPERTS = N_GROUPS * EXPERTS_PER_GROUP
TOP_K = 2
D_FF_EXPERT = D_MODEL // 2

MOD_ROWS = 16
ROUTER_LANES = 128
PROJ_ROWS = 512
HGRN_BLOCK = 32
HGRN_GROUP = 8
LOG2E = 1.4426950408889634
ATTN_ROWS = 256
ATTN_KEYS = 512
SAFE_SHIFT = 60
SAFE_DECAY = 80.0
MOE_ROWS = 512
ROUTE_ROWS = 512
SC_WINDOW = 128
SC_ROW = 256
PLANES = D_MODEL // (2 * SC_ROW)
VMEM_LIMIT = 48 * 1024 * 1024


def _cparams(*sem):
    return pltpu.CompilerParams(dimension_semantics=sem, vmem_limit_bytes=VMEM_LIMIT)


def _head_ones(n, dtype):
    r = lax.broadcasted_iota(jnp.int32, (n, n), 0) >> 6
    c = lax.broadcasted_iota(jnp.int32, (n, n), 1) >> 6
    return (r == c).astype(dtype)


def _head_sum(x, ones_bd):
    return jnp.dot(x.astype(BF16), ones_bd, preferred_element_type=F32)


def _head_rms(x, ones_bd):
    return x * lax.rsqrt(_head_sum(x * x, ones_bd) * (1.0 / HEAD_DIM) + EPS)


def _pack_rows(y):
    bits = lax.bitcast_convert_type(y.astype(BF16).astype(F32), jnp.uint32)
    half = y.shape[1] // 2
    return lax.bitcast_convert_type(bits[:, :half] | (bits[:, half:] >> 16), F32)


def _unpack_rows(w):
    bits = lax.bitcast_convert_type(w, jnp.uint32)
    hi = lax.bitcast_convert_type(bits & jnp.uint32(0xFFFF0000), F32)
    lo = lax.bitcast_convert_type(bits << 16, F32)
    return hi, lo


def _pack_planes(y, ref, lead=()):
    for p in range(PLANES):
        ref[(p,) + lead] = _pack_rows(y[:, 2 * p * SC_ROW:(2 * p + 2) * SC_ROW])


def _mod_kernel(c_ref, w_ref, b_ref, o_ref):
    a = jax.nn.silu(c_ref[...])
    w = w_ref[0]
    a_hi, w_hi = a.astype(BF16), w.astype(BF16)
    a_lo, w_lo = (a - a_hi.astype(F32)).astype(BF16), (w - w_hi.astype(F32)).astype(BF16)
    o_ref[...] = (jnp.dot(a_hi, w_hi, preferred_element_type=F32) + jnp.dot(a_hi, w_lo, preferred_element_type=F32)
                  + jnp.dot(a_lo, w_hi, preferred_element_type=F32) + b_ref[...])


def _mod(cc, w_mod, b_mod, layer):
    n = w_mod.shape[2]
    tn = 1536
    return pl.pallas_call(
        _mod_kernel,
        out_shape=jax.ShapeDtypeStruct((MOD_ROWS, n), F32),
        grid=(n // tn,),
        in_specs=[pl.BlockSpec((MOD_ROWS, D_MODEL), lambda j: (0, 0)),
                  pl.BlockSpec((1, D_MODEL, tn), lambda j: (layer, 0, j)),
                  pl.BlockSpec((1, tn), lambda j: (0, j))],
        out_specs=pl.BlockSpec((MOD_ROWS, tn), lambda j: (0, j)),
        compiler_params=_cparams("arbitrary"),
        name="mod",
    )(cc, w_mod, b_mod.reshape(1, n))


def _rope(xn, c_ref, sp_ref, sm_ref):
    w = xn.shape[-1]
    rep = lambda ref: jnp.tile(ref[...], (1, w // ref.shape[-1]))
    return xn * rep(c_ref) + pltpu.roll(xn, 16, 1) * rep(sp_ref) + pltpu.roll(xn, w - 16, 1) * rep(sm_ref)


def _combined(x_ref, pk_ref, info_ref, g_ref):
    info = info_ref[...]
    g1 = info[:, 2:3]
    g2 = info[:, 3:4]
    parts = []
    for p in range(PLANES):
        hi1, lo1 = _unpack_rows(pk_ref[TOP_K * p])
        hi2, lo2 = _unpack_rows(pk_ref[TOP_K * p + 1])
        parts += [g1 * hi1 + g2 * hi2, g1 * lo1 + g2 * lo2]
    return x_ref[0] + g_ref[0] * jnp.concatenate(parts, axis=1)


def _gmlp(z, ws_ref, bias_ref):
    gz = jax.nn.gelu(z)
    u = gz[:, :A_WIDTH]
    vn = _head_rms(gz[:, A_WIDTH:], _head_ones(A_WIDTH, BF16))
    lane_head = lax.broadcasted_iota(jnp.int32, vn.shape, 1) >> 6
    acc = bias_ref[...]
    for hh in range(A_HEADS):
        vh = jnp.where(lane_head == hh, vn, 0.0).astype(BF16)
        acc = acc + jnp.dot(ws_ref[hh], vh, preferred_element_type=F32)
    return u * acc


def _inproj_kernel(*refs, pending):
    if pending:
        (x_ref, pk_ref, info_ref, g2_ref, mul_ref, add_ref, w_ref, qc_ref, qsp_ref, qsm_ref, kc_ref, ksp_ref, ksm_ref,
         ws_ref, bias_ref, xo_ref, ya_ref, q_ref, k_ref, v_ref, zc_ref, g_ref) = refs
        x = _combined(x_ref, pk_ref, info_ref, g2_ref)
        xo_ref[0] = x
    else:
        (x_ref, mul_ref, add_ref, w_ref, qc_ref, qsp_ref, qsm_ref, kc_ref, ksp_ref, ksm_ref,
         ws_ref, bias_ref, ya_ref, q_ref, k_ref, v_ref, zc_ref, g_ref) = refs
        x = x_ref[0]
    ms = jnp.mean(x * x, axis=-1, keepdims=True)
    h = x * lax.rsqrt(ms + EPS) * mul_ref[0] + add_ref[0]
    y = jnp.dot(h.astype(BF16), w_ref[...], preferred_element_type=F32)
    for c0 in range(0, x.shape[0], A_CHUNK):
        ya_ref[0, c0:c0 + A_CHUNK, :] = _gmlp(y[c0:c0 + A_CHUNK, :OFF_B], ws_ref, bias_ref).astype(BF16)
    qn = _head_rms(y[:, OFF_B:OFF_KV], _head_ones(B_WIDTH, BF16))
    q_ref[0] = _rope(qn, qc_ref, qsp_ref, qsm_ref).astype(BF16)
    kn = _head_rms(y[:, OFF_KV:OFF_V], _head_ones(B_KV_WIDTH, BF16))
    k_ref[0] = _rope(kn, kc_ref, ksp_ref, ksm_ref).astype(BF16)
    v_ref[0] = y[:, OFF_V:OFF_C].astype(BF16)
    zc_ref[0] = y[:, OFF_C:OFF_G]
    g_ref[0] = y[:, OFF_G:]


def _inproj(x, mul, add, w_bf, q_tabs, k_tabs, ws_bf, bias2d, pending=None):
    b, t, d = x.shape
    tm = min(PROJ_ROWS, t)
    row = lambda i, bb: (bb, i, 0)
    vec = lambda i, bb: (bb, 0, 0)
    tab = lambda i, bb: (i, 0)
    widths = (A_WIDTH, B_WIDTH, B_KV_WIDTH, B_KV_WIDTH, 4 * C_WIDTH, C_WIDTH)
    dtypes = (BF16, BF16, BF16, BF16, F32, F32)
    pre_specs, pre_args = [], []
    if pending is not None:
        picked, info, gate, row0 = pending
        off = row0 // tm
        widths, dtypes = (d,) + widths, (F32,) + dtypes
        pre_specs = [pl.BlockSpec((PLANES * TOP_K, tm, SC_ROW), lambda i, bb: (0, off + bb * (t // tm) + i, 0)),
                     pl.BlockSpec((tm, ROUTER_LANES), lambda i, bb: (off + bb * (t // tm) + i, 0)),
                     pl.BlockSpec((1, 1, d), vec)]
        pre_args = [picked, info, gate]
    return pl.pallas_call(
        functools.partial(_inproj_kernel, pending=pending is not None),
        out_shape=[jax.ShapeDtypeStruct((b, t, w), dt) for w, dt in zip(widths, dtypes)],
        grid=(t // tm, b),
        in_specs=[pl.BlockSpec((1, tm, d), row)] + pre_specs
                 + [pl.BlockSpec((1, 1, d), vec),
                    pl.BlockSpec((1, 1, d), vec),
                    pl.BlockSpec((d, IN_WIDTH), lambda i, bb: (0, 0))]
                 + [pl.BlockSpec((tm, B_KV_WIDTH), tab)] * 6
                 + [pl.BlockSpec((A_HEADS, A_CHUNK, A_CHUNK), lambda i, bb: (0, 0, 0)),
                    pl.BlockSpec((A_CHUNK, A_WIDTH), lambda i, bb: (0, 0))],
        out_specs=[pl.BlockSpec((1, tm, w), row) for w in widths],
        compiler_params=_cparams("arbitrary", "arbitrary"),
        name="inproj",
    )(x, *pre_args, mul, add, w_bf, *q_tabs, *k_tabs, ws_bf, bias2d)


def _rope_tables(t, w, scale, width, rotate):
    ws = w.astype(F32) * scale
    if not rotate:
        c = jnp.broadcast_to(jnp.tile(ws, width // HEAD_DIM)[None, :], (t, width))
        z = jnp.zeros((t, width), F32)
        return c, z, z
    pos = jnp.arange(t)
    row = (pos // GRID_W).astype(F32)
    col = (pos % GRID_W).astype(F32)
    inv_freq = 1.0 / (ROPE_BASE ** (jnp.arange(0, HEAD_DIM // 2, 2, dtype=F32) / (HEAD_DIM // 2)))
    dd = jnp.arange(HEAD_DIM)
    axis = dd // 32
    half = (dd % 32) // 16
    ang = jnp.where(axis[None, :] == 0, row[:, None], col[:, None]) * inv_freq[dd % 16][None, :]
    cos, sin = jnp.cos(ang), jnp.sin(ang)
    c = cos * ws[None, :]
    sm = jnp.where(half[None, :] == 0, -sin * jnp.roll(ws, -16)[None, :], 0.0)
    sp = jnp.where(half[None, :] == 1, sin * jnp.roll(ws, 16)[None, :], 0.0)
    rep = width // HEAD_DIM
    return jnp.tile(c, (1, rep)), jnp.tile(sp, (1, rep)), jnp.tile(sm, (1, rep))


def _attn_kernel(flag_ref, q_ref, *refs, n_seg):
    kv_refs, o_ref = refs[:2 * n_seg], refs[2 * n_seg]
    tq = q_ref.shape[1]
    dh = HEAD_DIM

    def heads(j):
        q4 = jnp.concatenate([q_ref[0, :, (B_GROUP * j + gg) * dh:(B_GROUP * j + gg + 1) * dh]
                              for gg in range(B_GROUP)], axis=0)
        ks, vs = [], []
        for sg in range(n_seg):
            s_len = kv_refs[2 * sg].shape[1]
            for c0 in range(0, s_len, ATTN_KEYS):
                c1 = min(c0 + ATTN_KEYS, s_len)
                ks.append(kv_refs[2 * sg][0, c0:c1, j * dh:(j + 1) * dh])
                vs.append(kv_refs[2 * sg + 1][0, c0:c1, j * dh:(j + 1) * dh])
        return q4, ks, vs

    def scores(q4, ks):
        return [lax.dot_general(kk, q4, (((1,), (1,)), ((), ())), preferred_element_type=F32) for kk in ks]

    def finish(j, ps, vs):
        l = functools.reduce(jnp.add, [jnp.sum(p, axis=0, keepdims=True) for p in ps])
        acc = sum(lax.dot_general(vv, p.astype(BF16), (((0,), (0,)), ((), ())), preferred_element_type=F32)
                  for p, vv in zip(ps, vs))
        o = (acc / l).T
        for gg in range(B_GROUP):
            hh = B_GROUP * j + gg
            o_ref[0, :, hh * dh:(hh + 1) * dh] = o[gg * tq:(gg + 1) * tq].astype(BF16)

    @pl.when(flag_ref[0] > 0)
    def _():
        shift = flag_ref[1].astype(F32)
        for j in range(B_KV_HEADS):
            q4, ks, vs = heads(j)
            finish(j, [jnp.exp2(s - shift) for s in scores(q4, ks)], vs)

    @pl.when(flag_ref[0] <= 0)
    def _():
        for j in range(B_KV_HEADS):
            q4, ks, vs = heads(j)
            ss = scores(q4, ks)
            m = functools.reduce(jnp.maximum, [jnp.max(s, axis=0, keepdims=True) for s in ss])
            finish(j, [jnp.exp2(s - m) for s in ss], vs)


def _attn(flag, q, kv_segs):
    b, t, w = q.shape
    tq = min(ATTN_ROWS, t)
    n_seg = len(kv_segs)
    kv_flat, kv_specs = [], []
    for kk, vv in kv_segs:
        s_len, kw = kk.shape[1:]
        kv_flat += [kk, vv]
        kv_specs += [pl.BlockSpec((1, s_len, kw), lambda bb, i, fl: (bb, 0, 0))] * 2
    return pl.pallas_call(
        functools.partial(_attn_kernel, n_seg=n_seg),
        out_shape=jax.ShapeDtypeStruct((b, t, w), BF16),
        grid_spec=pltpu.PrefetchScalarGridSpec(
            num_scalar_prefetch=1,
            grid=(b, t // tq),
            in_specs=[pl.BlockSpec((1, tq, w), lambda bb, i, fl: (bb, i, 0))] + kv_specs,
            out_specs=pl.BlockSpec((1, tq, w), lambda bb, i, fl: (bb, i, 0))),
        compiler_params=_cparams("arbitrary", "arbitrary"),
        name="attn",
    )(flag, q, *kv_flat)


def _scan_rows(x, reverse):
    n = x.shape[0]
    rows = lax.broadcasted_iota(jnp.int32, x.shape, 0)
    sh = 1
    while sh < n:
        if reverse:
            x = x + jnp.where(rows < n - sh, pltpu.roll(x, n - sh, 0), 0.0)
        else:
            x = x + jnp.where(rows >= sh, pltpu.roll(x, sh, 0), 0.0)
        sh *= 2
    return x


def _stack_heads(x, lane_head):
    return jnp.concatenate([jnp.where(lane_head == hh, x, 0.0) for hh in range(C_HEADS)], axis=0)


def _hgrn_kernel(z_ref, zc_ref, lbc_ref, *refs, ctx_out):
    if ctx_out:
        o_ref, oc_ref, st_ref, kx_ref, bx_ref, vx_ref, flag_ref = refs
    else:
        o_ref, st_ref, kx_ref, bx_ref, vx_ref, flag_ref = refs
        oc_ref = None
    n = C_WIDTH
    nb = HGRN_BLOCK
    nblk_c = zc_ref.shape[1] // nb
    nblk_l = z_ref.shape[1] // nb
    ones_bd = _head_ones(n, BF16)
    rows = lax.broadcasted_iota(jnp.int32, (nb, n), 0)
    lane_head = lax.broadcasted_iota(jnp.int32, (nb, n), 1) >> 6
    lane_head64 = lax.broadcasted_iota(jnp.int32, (HEAD_DIM, n), 1) >> 6
    low_half = (lax.broadcasted_iota(jnp.int32, (HEAD_DIM, 2 * HEAD_DIM), 1) < HEAD_DIM)
    sc_t = lax.broadcasted_iota(jnp.int32, (nb, C_HEADS * nb), 0)
    sc_s = lax.broadcasted_iota(jnp.int32, (nb, C_HEADS * nb), 1) & (nb - 1)

    def gates(z, d):
        one_m_lb = lbc_ref[d, 0:1, :]
        log1m_lb = lbc_ref[d, 1:2, :]
        log_lb = lbc_ref[d, 2:3, :]
        lb_pos = lbc_ref[d, 3:4, :] > 0.5
        soft = jnp.log(1.0 + jnp.exp(-jnp.abs(z)))
        log_rest = log1m_lb + (jnp.minimum(z, 0.0) - soft)
        lse = jnp.maximum(log_lb, log_rest) + jnp.log(1.0 + jnp.exp(-jnp.abs(log_lb - log_rest)))
        return jnp.where(lb_pos, lse, log_rest), one_m_lb * jnp.exp(jnp.minimum(-z, 0.0) - soft)

    def group_blocks(nblk, i):
        grp = HGRN_GROUP if nblk % HGRN_GROUP == 0 else 1
        fwd = [i * grp + gg for gg in range(grp)]
        return grp, fwd, [nblk - 1 - blk for blk in fwd]

    def block_bound(src_ref, blk, d):
        zz = src_ref[0, pl.ds(pl.multiple_of(blk * nb, nb), nb), (1 + d) * n:(2 + d) * n]
        step_bound = jnp.minimum(lbc_ref[d, 4:5, :],
                                 jnp.maximum(-zz, 0.0) + (jnp.log(2.0) - lbc_ref[d, 1:2, :]))
        return jnp.sum(step_bound, axis=0, keepdims=True)

    def flag_groups(src_ref, nblk, base):
        def body(i, carry):
            _, fwd, bwd = group_blocks(nblk, i)
            worst = functools.reduce(jnp.maximum, [block_bound(src_ref, blk, d)
                                                   for d, blks in ((0, fwd), (1, bwd)) for blk in blks])
            flag_ref[base + i] = (jnp.max(worst) <= SAFE_DECAY).astype(jnp.int32)
            return carry
        lax.fori_loop(0, nblk // group_blocks(nblk, 0)[0], body, 0)

    def worst_bound(src_ref, nblk):
        def body(blk, worst):
            return jnp.maximum(worst, jnp.maximum(block_bound(src_ref, blk, 0), block_bound(src_ref, blk, 1)))
        return lax.fori_loop(0, nblk, body, jnp.zeros((1, n), F32))

    n_grp_c = nblk_c // group_blocks(nblk_c, 0)[0]
    all_safe = jnp.max(jnp.maximum(worst_bound(zc_ref, nblk_c), worst_bound(z_ref, nblk_l))) <= SAFE_DECAY
    st_ref[...] = jnp.zeros_like(st_ref)
    o_ref[...] = jnp.zeros_like(o_ref)
    if ctx_out:
        oc_ref[...] = jnp.zeros_like(oc_ref)

    def step(src_ref, dst_ref, blk, d, fast):
        reverse = d == 1
        r0 = pl.multiple_of(blk * nb, nb)
        v = src_ref[0, pl.ds(r0, nb), 3 * n:4 * n]
        log_f, k = gates(src_ref[0, pl.ds(r0, nb), (1 + d) * n:(2 + d) * n], d)
        bc = _scan_rows(log_f, reverse)
        edge = 0 if reverse else nb - 1
        b_edge = bc[edge:edge + 1, :]
        st = st_ref[d]
        v_bf = v.astype(BF16)

        if dst_ref is not None:
            q = jax.nn.silu(src_ref[0, pl.ds(r0, nb), 0:n])
            qt = (q * jnp.exp(bc)).astype(BF16)
            o = lax.dot_general(qt, _stack_heads(st, lane_head64).astype(BF16), (((1,), (1,)), ((), ())),
                                preferred_element_type=F32)

            def intra_fast():
                kt = _stack_heads(k * jnp.exp(-bc), lane_head).astype(BF16)
                sc = lax.dot_general(qt, kt, (((1,), (1,)), ((), ())), preferred_element_type=F32)
                keep = (sc_s >= sc_t) if reverse else (sc_s <= sc_t)
                sc = jnp.where(keep, sc, 0.0).astype(BF16)
                return jnp.dot(sc, _stack_heads(v, lane_head).astype(BF16), preferred_element_type=F32)

            def intra_exact():
                kx_ref[d] = k
                bx_ref[d] = bc
                vx_ref[d] = v

                def sbody(s, acc):
                    keep = (rows <= s) if reverse else (rows >= s)
                    e = jnp.exp(jnp.where(keep, bc - bx_ref[d, pl.ds(s, 1), :], 0.0))
                    p = jnp.where(keep, q * e * kx_ref[d, pl.ds(s, 1), :], 0.0)
                    sc = jnp.dot(p.astype(BF16), ones_bd, preferred_element_type=F32)
                    return acc + sc * vx_ref[d, pl.ds(s, 1), :]

                return lax.fori_loop(0, nb, sbody, jnp.zeros((nb, n), F32))

            o = o + (intra_fast() if fast else intra_exact())
            dst_ref[0, pl.ds(r0, nb), :] += o

        kd = (k * jnp.exp(b_edge - bc)).astype(BF16)
        full = lax.dot_general(v_bf, kd, (((0,), (0,)), ((), ())), preferred_element_type=F32)
        upd = jnp.concatenate(
            [jnp.where(low_half,
                       full[(2 * c) * HEAD_DIM:(2 * c + 1) * HEAD_DIM, 2 * c * HEAD_DIM:(2 * c + 2) * HEAD_DIM],
                       full[(2 * c + 1) * HEAD_DIM:(2 * c + 2) * HEAD_DIM, 2 * c * HEAD_DIM:(2 * c + 2) * HEAD_DIM])
             for c in range(C_HEADS // 2)], axis=1)
        st_ref[d] = st * jnp.exp(b_edge) + upd

    def run(src_ref, dst_ref, nblk, base, per_group):
        def body(i, carry):
            _, fwd, bwd = group_blocks(nblk, i)

            def group(fast):
                for bf, bb in zip(fwd, bwd):
                    step(src_ref, dst_ref, bf, 0, fast)
                    step(src_ref, dst_ref, bb, 1, fast)

            if per_group:
                safe = flag_ref[base + i]
                pl.when(safe > 0)(functools.partial(group, True))
                pl.when(safe <= 0)(functools.partial(group, False))
            else:
                group(True)
            return carry
        lax.fori_loop(0, nblk // group_blocks(nblk, 0)[0], body, 0)

    @pl.when(all_safe)
    def _():
        run(zc_ref, oc_ref, nblk_c, 0, False)
        run(z_ref, o_ref, nblk_l, n_grp_c, False)

    @pl.when(jnp.logical_not(all_safe))
    def _():
        flag_groups(zc_ref, nblk_c, 0)
        flag_groups(z_ref, nblk_l, n_grp_c)
        run(zc_ref, oc_ref, nblk_c, 0, True)
        run(z_ref, o_ref, nblk_l, n_grp_c, True)


def _hgrn(zc, zc_c, lbc, ctx_out):
    b, t, w = zc.shape
    lc = zc_c.shape[1]
    n = C_WIDTH
    row = lambda bb: (bb, 0, 0)
    out_shape = [jax.ShapeDtypeStruct((b, t, n), F32)]
    out_specs = [pl.BlockSpec((1, t, n), row)]
    if ctx_out:
        out_shape.append(jax.ShapeDtypeStruct((b, lc, n), F32))
        out_specs.append(pl.BlockSpec((1, lc, n), row))
    res = pl.pallas_call(
        functools.partial(_hgrn_kernel, ctx_out=ctx_out),
        out_shape=out_shape,
        grid=(b,),
        in_specs=[pl.BlockSpec((1, t, w), row),
                  pl.BlockSpec((1, lc, w), row),
                  pl.BlockSpec((2, 8, n), lambda bb: (0, 0, 0))],
        out_specs=out_specs,
        scratch_shapes=[pltpu.VMEM((2, HEAD_DIM, n), F32)]
                       + [pltpu.VMEM((2, HGRN_BLOCK, n), F32)] * 3
                       + [pltpu.SMEM(((t + lc) // HGRN_BLOCK,), jnp.int32)],
        compiler_params=_cparams("arbitrary"),
        name="hgrn",
    )(zc, zc_c, lbc)
    return (res[0], res[1]) if ctx_out else (res[0], None)


def _select_experts(lg):
    lane = lax.broadcasted_iota(jnp.int32, lg.shape, 1)
    lane_f = lane.astype(F32)
    neg = -jnp.inf
    gl = jnp.where(lane < N_GROUPS, lg, neg)
    gmax = jnp.max(gl, axis=1, keepdims=True)
    grp = jnp.min(jnp.where(gl == gmax, lane_f, float(ROUTER_LANES)), axis=1, keepdims=True).astype(jnp.int32)
    p_grp = 1.0 / jnp.sum(jnp.exp(gl - gmax), axis=1, keepdims=True)
    in_grp = (lane >= N_GROUPS) & (lane < N_GROUPS + N_EXPERTS) & (((lane - N_GROUPS) >> 3) == grp)
    el = jnp.where(in_grp, lg, neg)
    v1 = jnp.max(el, axis=1, keepdims=True)
    i1 = jnp.min(jnp.where(el == v1, lane_f, float(ROUTER_LANES)), axis=1, keepdims=True)
    el2 = jnp.where(lane_f == i1, neg, el)
    v2 = jnp.max(el2, axis=1, keepdims=True)
    i2 = jnp.min(jnp.where(el2 == v2, lane_f, float(ROUTER_LANES)), axis=1, keepdims=True)
    rr = jnp.exp(v2 - v1)
    g1 = p_grp / (1.0 + rr)
    g2 = p_grp * rr / (1.0 + rr)
    sel = jnp.where(lane == 0, i1, jnp.where(lane == 1, i2, jnp.where(lane == 2, g1, jnp.where(lane == 3, g2, 0.0))))
    counts = jnp.sum(((lane_f == i1) | (lane_f == i2)).astype(F32), axis=0, keepdims=True)
    return sel, counts


def _outproj_kernel(x_ref, ya_ref, yb_ref, o_ref, g_ref, w_ref, gate_ref, mul_ref, add_ref, hw_ref,
                    wr_ref, br_ref, xo_ref, h2_ref, sel_ref, cnt_ref):
    yc = _head_rms(o_ref[0], _head_ones(C_WIDTH, BF16)) * hw_ref[...] * jax.nn.silu(g_ref[0])
    y = jnp.dot(ya_ref[0], w_ref[0:A_WIDTH, :], preferred_element_type=F32)
    y = y + jnp.dot(yb_ref[0], w_ref[A_WIDTH:A_WIDTH + B_WIDTH, :], preferred_element_type=F32)
    y = y + jnp.dot(yc.astype(BF16), w_ref[A_WIDTH + B_WIDTH:, :], preferred_element_type=F32)
    xn = x_ref[0] + gate_ref[0] * y
    xo_ref[0] = xn
    ms = jnp.mean(xn * xn, axis=-1, keepdims=True)
    h2 = xn * lax.rsqrt(ms + EPS) * mul_ref[0] + add_ref[0]
    _pack_planes(h2, h2_ref, (0,))
    h_hi = h2.astype(BF16)
    h_lo = (h2 - h_hi.astype(F32)).astype(BF16)
    both = jnp.dot(h_hi, wr_ref[...], preferred_element_type=F32)
    lg = (both[:, :ROUTER_LANES] + both[:, ROUTER_LANES:] + br_ref[...]
          + jnp.dot(h_lo, wr_ref[:, 0:ROUTER_LANES], preferred_element_type=F32))
    sel, counts = _select_experts(lg)
    sel_ref[0] = sel

    @pl.when((pl.program_id(0) == 0) & (pl.program_id(1) == 0))
    def _():
        cnt_ref[...] = jnp.zeros_like(cnt_ref)
    cnt_ref[...] += counts


def _outproj(x, ya, yb, o, g, w_bf, gate, mul, add, hw, wr, br):
    b, t, d = x.shape
    tm = min(PROJ_ROWS, t)
    row = lambda bb, i: (bb, i, 0)
    vec = lambda bb, i: (bb, 0, 0)
    const = lambda bb, i: (0, 0)
    return pl.pallas_call(
        _outproj_kernel,
        out_shape=[jax.ShapeDtypeStruct((b, t, d), F32),
                   jax.ShapeDtypeStruct((PLANES, b, t, SC_ROW), F32),
                   jax.ShapeDtypeStruct((b, t, ROUTER_LANES), F32),
                   jax.ShapeDtypeStruct((8, ROUTER_LANES), F32)],
        grid=(b, t // tm),
        in_specs=[pl.BlockSpec((1, tm, d), row),
                  pl.BlockSpec((1, tm, A_WIDTH), row),
                  pl.BlockSpec((1, tm, B_WIDTH), row),
                  pl.BlockSpec((1, tm, C_WIDTH), row),
                  pl.BlockSpec((1, tm, C_WIDTH), row),
                  pl.BlockSpec((d, d), const),
                  pl.BlockSpec((1, 1, d), vec),
                  pl.BlockSpec((1, 1, d), vec),
                  pl.BlockSpec((1, 1, d), vec),
                  pl.BlockSpec((1, C_WIDTH), const),
                  pl.BlockSpec((d, 2 * ROUTER_LANES), const),
                  pl.BlockSpec((1, ROUTER_LANES), const)],
        out_specs=[pl.BlockSpec((1, tm, d), row),
                   pl.BlockSpec((PLANES, 1, tm, SC_ROW), lambda bb, i: (0, bb, i, 0)),
                   pl.BlockSpec((1, tm, ROUTER_LANES), row),
                   pl.BlockSpec((8, ROUTER_LANES), const)],
        compiler_params=_cparams("arbitrary", "arbitrary"),
        name="outproj",
    )(x, ya, yb, o, g, w_bf, gate, mul, add, hw, wr, br)


def _route_kernel(sel_ref, cnt_ref, info_ref, meta_ref, base_ref):
    i = pl.program_id(0)
    tm = sel_ref.shape[0]
    lane = lax.broadcasted_iota(jnp.int32, (tm, ROUTER_LANES), 1)
    lane_f = lane.astype(F32)
    sel = sel_ref[...]
    hit1 = lane_f == sel[:, 0:1]
    hit2 = lane_f == sel[:, 1:2]
    onehot = (hit1 | hit2).astype(F32)

    @pl.when(i == 0)
    def _():
        counts = cnt_ref[...]
        padded = jnp.floor((counts + (MOE_ROWS - 1.0)) * (1.0 / MOE_ROWS)) * MOE_ROWS
        r = lax.broadcasted_iota(jnp.int32, (ROUTER_LANES, ROUTER_LANES), 0)
        c = lax.broadcasted_iota(jnp.int32, (ROUTER_LANES, ROUTER_LANES), 1)
        ends = jnp.dot(padded, (r <= c).astype(F32), preferred_element_type=F32,
                       precision=lax.Precision.HIGHEST)
        base_ref[...] = (ends - padded)[0:1]
        row = lax.broadcasted_iota(jnp.int32, (8, ROUTER_LANES), 0)
        meta_ref[...] = jnp.where(row == 0, counts, jnp.where(row == 1, ends - padded, ends))

    tr = lax.broadcasted_iota(jnp.int32, (tm, tm), 0)
    tc = lax.broadcasted_iota(jnp.int32, (tm, tm), 1)
    before = jnp.dot((tc < tr).astype(BF16), onehot.astype(BF16), preferred_element_type=F32)
    pos = base_ref[...] + before
    d1 = jnp.sum(jnp.where(hit1, pos, 0.0), axis=1, keepdims=True)
    d2 = jnp.sum(jnp.where(hit2, pos, 0.0), axis=1, keepdims=True)
    base_ref[...] += jnp.sum(onehot, axis=0, keepdims=True)
    info_ref[...] = jnp.where(lane == 0, d1, jnp.where(lane == 1, d2, sel))


def _route(sel, counts):
    n = sel.shape[0]
    tm = ROUTE_ROWS if n % ROUTE_ROWS == 0 else ROUTE_ROWS // 2
    return pl.pallas_call(
        _route_kernel,
        out_shape=[jax.ShapeDtypeStruct((n, ROUTER_LANES), F32),
                   jax.ShapeDtypeStruct((8, ROUTER_LANES), F32)],
        grid=(n // tm,),
        in_specs=[pl.BlockSpec((tm, ROUTER_LANES), lambda i: (i, 0)),
                  pl.BlockSpec((8, ROUTER_LANES), lambda i: (0, 0))],
        out_specs=[pl.BlockSpec((tm, ROUTER_LANES), lambda i: (i, 0)),
                   pl.BlockSpec((8, ROUTER_LANES), lambda i: (0, 0))],
        scratch_shapes=[pltpu.VMEM((1, ROUTER_LANES), F32)],
        compiler_params=_cparams("arbitrary"),
        name="route",
    )(sel, counts)


def _sc_mesh():
    return plsc.VectorSubcoreMesh(core_axis_name="c", subcore_axis_name="s")


def _sc_gather(table, idx):
    n = idx.shape[0]
    d = table.shape[1]

    @functools.partial(pl.kernel, out_type=jax.ShapeDtypeStruct((n, d), table.dtype), mesh=_sc_mesh())
    def gather(x_hbm, i_hbm, o_hbm):
        def body(i_vmem, o_vmem):
            pltpu.sync_copy(x_hbm.at[i_vmem.at[0]], o_vmem)

        pltpu.emit_pipeline(
            body,
            grid=(n // SC_WINDOW,),
            in_specs=[pl.BlockSpec((1, SC_WINDOW), lambda i: (0, i))],
            out_specs=[pl.BlockSpec((SC_WINDOW, d), lambda i: (i, 0))],
            core_axis_name=("c", "s"),
            dimension_semantics=(pltpu.PARALLEL,),
        )(i_hbm, o_hbm)

    return gather(table, idx.reshape(1, n))


def _sc_scatter2(rows, idx0, idx1, n_out):
    m, d = rows.shape

    @functools.partial(pl.kernel, out_type=jax.ShapeDtypeStruct((n_out, d), rows.dtype), mesh=_sc_mesh())
    def scatter(x_hbm, i0_hbm, i1_hbm, o_hbm):
        def body(x_vmem, i0_vmem, i1_vmem):
            pltpu.sync_copy(x_vmem, o_hbm.at[i0_vmem.at[0]])
            pltpu.sync_copy(x_vmem, o_hbm.at[i1_vmem.at[0]])

        pltpu.emit_pipeline(
            body,
            grid=(m // SC_WINDOW,),
            in_specs=[pl.BlockSpec((SC_WINDOW, d), lambda i: (i, 0)),
                      pl.BlockSpec((1, SC_WINDOW), lambda i: (0, i)),
                      pl.BlockSpec((1, SC_WINDOW), lambda i: (0, i))],
            out_specs=[],
            core_axis_name=("c", "s"),
            dimension_semantics=(pltpu.PARALLEL,),
        )(x_hbm, i0_hbm, i1_hbm)

    return scatter(rows, idx0.reshape(1, m), idx1.reshape(1, m))


def _moe_kernel(be_ref, nu_ref, first_ref, slot_ref, nxt_ref, x_ref, w1_hbm, w3_hbm, w2_hbm, o_ref,
                w1f, w3f, w2f, w1b, w3b, w2b, sem, *, layer):
    i = pl.program_id(0)

    def fetch(e, slot):
        return [pltpu.make_async_copy(src.at[layer, e], dst.at[slot], sem.at[n, slot])
                for n, (src, dst) in enumerate(((w1_hbm, w1f), (w3_hbm, w3f), (w2_hbm, w2f)))]

    @pl.when((i == 0) & (nu_ref[0] > 0))
    def _():
        for cp in fetch(be_ref[0], 0):
            cp.start()

    @pl.when((first_ref[i] > 0) & (i < nu_ref[0]))
    def _():
        slot = slot_ref[i]
        for cp in fetch(be_ref[i], slot):
            cp.wait()
        w1b[...] = w1f[slot].astype(BF16)
        w3b[...] = w3f[slot].astype(BF16)
        w2b[...] = w2f[slot].astype(BF16)

        @pl.when(nxt_ref[i] >= 0)
        def _():
            for cp in fetch(nxt_ref[i], 1 - slot):
                cp.start()

    @pl.when(i < nu_ref[0])
    def _():
        parts = [h.astype(BF16) for p in range(PLANES) for h in _unpack_rows(x_ref[p])]
        a = sum(jnp.dot(h, w1b[q * SC_ROW:(q + 1) * SC_ROW, :], preferred_element_type=F32)
                for q, h in enumerate(parts))
        b = sum(jnp.dot(h, w3b[q * SC_ROW:(q + 1) * SC_ROW, :], preferred_element_type=F32)
                for q, h in enumerate(parts))
        hmid = (jax.nn.silu(a) * b).astype(BF16)
        _pack_planes(jnp.dot(hmid, w2b[...], preferred_element_type=F32), o_ref)

    @pl.when(i >= nu_ref[0])
    def _():
        o_ref[...] = jnp.zeros_like(o_ref)


def _moe_mlp(blk_expert, n_used, xs, w1, w3, w2, layer):
    n_rows = xs.shape[1]
    d, f = w1.shape[2:]
    nblk = n_rows // MOE_ROWS
    rows = lambda i, *_: (0, i, 0)
    idx = jnp.arange(nblk, dtype=jnp.int32)
    first = (idx < n_used[0]) & ((idx == 0) | (blk_expert != jnp.roll(blk_expert, 1)))
    slot = (jnp.cumsum(first.astype(jnp.int32)) - 1) % 2
    nxt_first = lax.cummin(jnp.where(first, idx, nblk)[::-1])[::-1]
    nxt_first = jnp.concatenate([nxt_first[1:], jnp.full((1,), nblk, jnp.int32)])
    nxt = jnp.where(nxt_first < nblk, blk_expert[jnp.minimum(nxt_first, nblk - 1)], -1)
    hbm = pl.BlockSpec(memory_space=pl.ANY)
    return pl.pallas_call(
        functools.partial(_moe_kernel, layer=layer),
        out_shape=jax.ShapeDtypeStruct((PLANES, n_rows, SC_ROW), F32),
        grid_spec=pltpu.PrefetchScalarGridSpec(
            num_scalar_prefetch=5,
            grid=(nblk,),
            in_specs=[pl.BlockSpec((PLANES, MOE_ROWS, SC_ROW), rows), hbm, hbm, hbm],
            out_specs=pl.BlockSpec((PLANES, MOE_ROWS, SC_ROW), rows),
            scratch_shapes=[pltpu.VMEM((2, d, f), F32), pltpu.VMEM((2, d, f), F32), pltpu.VMEM((2, f, d), F32),
                            pltpu.VMEM((d, f), BF16), pltpu.VMEM((d, f), BF16), pltpu.VMEM((f, d), BF16),
                            pltpu.SemaphoreType.DMA((3, 2))]),
        compiler_params=_cparams("arbitrary"),
        name="moe",
    )(blk_expert, n_used, first.astype(jnp.int32), slot.astype(jnp.int32), nxt.astype(jnp.int32), xs, w1, w3, w2)


def _combine_kernel(x_ref, pk_ref, info_ref, g_ref, o_ref):
    o_ref[0] = _combined(x_ref, pk_ref, info_ref, g_ref)


def _combine(x, picked, info, gate, row0):
    b, t, d = x.shape
    tm = min(256, t)
    off = row0 // tm
    tok = lambda bb, i: (off + bb * (t // tm) + i, 0)
    tok3 = lambda bb, i: (0, off + bb * (t // tm) + i, 0)
    return pl.pallas_call(
        _combine_kernel,
        out_shape=jax.ShapeDtypeStruct((b, t, d), F32),
        grid=(b, t // tm),
        in_specs=[pl.BlockSpec((1, tm, d), lambda bb, i: (bb, i, 0)),
                  pl.BlockSpec((PLANES * TOP_K, tm, SC_ROW), tok3),
                  pl.BlockSpec((tm, ROUTER_LANES), tok),
                  pl.BlockSpec((1, 1, d), lambda bb, i: (bb, 0, 0))],
        out_specs=pl.BlockSpec((1, tm, d), lambda bb, i: (bb, i, 0)),
        compiler_params=_cparams("arbitrary", "arbitrary"),
        name="combine",
    )(x, picked, info, gate)


def _hier_moe(h2p, sel, counts, w1, w3, w2, layer):
    n_tok = h2p.shape[1]
    info, meta = _route(sel, counts)
    dest = info[:, 0:TOP_K].astype(jnp.int32)
    pad_ends = meta[2, N_GROUPS:N_GROUPS + N_EXPERTS].astype(jnp.int32)
    nblk = -(-(n_tok * TOP_K) // MOE_ROWS) + N_EXPERTS
    n_rows = nblk * MOE_ROWS
    blk_start = jnp.arange(nblk, dtype=jnp.int32) * MOE_ROWS
    blk_expert = jnp.minimum(jnp.sum((pad_ends[None, :] <= blk_start[:, None]).astype(jnp.int32), axis=1),
                             N_EXPERTS - 1)
    n_used = pad_ends[-1:] // MOE_ROWS
    slot = [jnp.concatenate([p * n_rows + dest[:, s] for p in range(PLANES)]) for s in range(TOP_K)]
    xs = _sc_scatter2(h2p.reshape(PLANES * n_tok, SC_ROW), slot[0], slot[1], PLANES * n_rows)
    out = _moe_mlp(blk_expert, n_used, xs.reshape(PLANES, n_rows, SC_ROW), w1, w3, w2, layer)
    idx_all = jnp.concatenate([p * n_rows + dest[:, s] for p in range(PLANES) for s in range(TOP_K)])
    picked = _sc_gather(out.reshape(PLANES * n_rows, SC_ROW), idx_all)
    return picked.reshape(PLANES * TOP_K, n_tok, SC_ROW), info


def _layer(layer, x, xc, pend, c, c_ctx, lb, w_mod, b_mod, norm1_w, w_in, w_s, b_s, q_norm_w, k_norm_w, hgrn_norm_w, w_out,
           norm2_w, w_grp, b_grp, w_exp, b_exp, w1, w3, w2, ctx_out):
    b, t, d = x.shape
    lc = xc.shape[1]
    cc = jnp.zeros((MOD_ROWS, d), F32).at[:b].set(c).at[b].set(c_ctx)
    mod = _mod(cc, w_mod, b_mod, layer)
    sh1, sc1, g1, sh2, sc2, g2 = [m[:, None, :] for m in jnp.split(mod[:b], 6, axis=-1)]
    mod_c = [jnp.broadcast_to(m[None, None, :], (b, 1, d)) for m in jnp.split(mod[b], 6)]

    w_in_bf = w_in.astype(BF16)
    scale = LOG2E * HEAD_DIM ** -0.5
    q_tabs = _rope_tables(t, q_norm_w, scale, B_KV_WIDTH, True)
    k_tabs = _rope_tables(t, k_norm_w, 1.0, B_KV_WIDTH, True)
    qc_tabs = _rope_tables(lc, q_norm_w, scale, B_KV_WIDTH, False)
    kc_tabs = _rope_tables(lc, k_norm_w, 1.0, B_KV_WIDTH, False)
    ws_bf = w_s.astype(BF16)
    bias2d = jnp.repeat(b_s.T, HEAD_DIM, axis=1)
    res = _inproj(x, norm1_w * (1.0 + sc1), sh1, w_in_bf, q_tabs, k_tabs, ws_bf, bias2d,
                  None if pend is None else pend[:2] + (pend[2], 0))
    res_c = _inproj(xc, norm1_w * (1.0 + mod_c[1]), mod_c[0], w_in_bf, qc_tabs, kc_tabs, ws_bf, bias2d,
                    None if pend is None else pend[:2] + (pend[3], b * t))
    if pend is not None:
        x, xc, res, res_c = res[0], res_c[0], res[1:], res_c[1:]
    ya, q, k, v, zc, g = res
    ya_c, q_c, k_c, v_c, zc_c, g_c = res_c

    bound = LOG2E * HEAD_DIM ** 0.5 * jnp.max(jnp.abs(q_norm_w)) * jnp.max(jnp.abs(k_norm_w)) * 1.02
    shift = jnp.ceil(bound)
    attn_flag = jnp.stack([(shift <= SAFE_SHIFT).astype(jnp.int32), shift.astype(jnp.int32)])
    yb = _attn(attn_flag, q, [(k, v), (k_c, v_c)])

    pos = lb > 0.0
    log_lb = jnp.log(jnp.where(pos, lb, 1.0))
    lbc = jnp.stack([1.0 - lb, jnp.log1p(-lb), log_lb, pos.astype(F32), jnp.where(pos, -log_lb, 1e30)], axis=1)
    lbc = jnp.concatenate([lbc, jnp.zeros((2, 3, C_WIDTH), F32)], axis=1)
    o, o_c = _hgrn(zc, zc_c, lbc, ctx_out)

    w_out_bf = w_out.astype(BF16)
    hw = jnp.tile(hgrn_norm_w, C_HEADS)[None, :]
    wr = jnp.zeros((d, ROUTER_LANES), F32).at[:, :N_GROUPS].set(w_grp).at[
        :, N_GROUPS:N_GROUPS + N_EXPERTS].set(w_exp)
    wr_hi = wr.astype(BF16)
    wr = jnp.concatenate([wr_hi, (wr - wr_hi.astype(F32)).astype(BF16)], axis=1)
    br = jnp.zeros((1, ROUTER_LANES), F32).at[0, :N_GROUPS].set(b_grp).at[
        0, N_GROUPS:N_GROUPS + N_EXPERTS].set(b_exp)
    x, h2, sel, cnt = _outproj(x, ya, yb, o, g, w_out_bf, g1, norm2_w * (1.0 + sc2), sh2, hw, wr, br)
    if ctx_out:
        yb_c = _attn(attn_flag, q_c, [(k_c, v_c)])
        xc, h2c, sel_c, cnt_c = _outproj(xc, ya_c, yb_c, o_c, g_c, w_out_bf, mod_c[2],
                                norm2_w * (1.0 + mod_c[4]), mod_c[3], hw, wr, br)
        tokens = jnp.concatenate([h2.reshape(PLANES, -1, SC_ROW), h2c.reshape(PLANES, -1, SC_ROW)], axis=1)
        sel_all = jnp.concatenate([sel.reshape(-1, ROUTER_LANES), sel_c.reshape(-1, ROUTER_LANES)], axis=0)
        picked, info = _hier_moe(tokens, sel_all, cnt + cnt_c, w1, w3, w2, layer)
        return x, xc, (picked, info, g2, mod_c[5])
    picked, info = _hier_moe(h2.reshape(PLANES, -1, SC_ROW), sel.reshape(-1, ROUTER_LANES), cnt, w1, w3, w2, layer)
    return x, xc, (picked, info, g2, None)


def kernel(x, c, ctx, c_ctx, w_mod, b_mod, norm1_w, w_in, w_s, b_s, q_norm_w, k_norm_w, hgrn_lb_logits,
           hgrn_norm_w, w_out, norm2_w, w_grp, b_grp, w_exp, b_exp, w1, w3, w2):
    depth = w_mod.shape[0]
    lb_sm = jax.nn.softmax(hgrn_lb_logits.astype(F32), axis=0)
    lb = jnp.cumsum(lb_sm, axis=0) - lb_sm[0]
    xc = ctx
    pend = None
    for l in range(depth):
        x, xc, pend = _layer(l, x, xc, pend, c, c_ctx, lb[l], w_mod, b_mod[l], norm1_w[l], w_in[l], w_s[l], b_s[l],
                             q_norm_w[l], k_norm_w[l], hgrn_norm_w[l], w_out[l], norm2_w[l], w_grp[l], b_grp[l],
                             w_exp[l], b_exp[l], w1, w3, w2, ctx_out=(l < depth - 1))
    return _combine(x, pend[0], pend[1], pend[2], 0)
```

```python
import functools

import jax
import jax.numpy as jnp
from jax import lax
from jax.experimental import pallas as pl
from jax.experimental.pallas import tpu as pltpu
from jax.experimental.pallas import tpu_sc as plsc

F32 = jnp.float32
BF16 = jnp.bfloat16

D_MODEL = 1024
HEAD_DIM = 64
GRID_W = 64
EPS = 1e-6
ROPE_BASE = 10000.0
A_WIDTH = D_MODEL // 4
A_HEADS = A_WIDTH // HEAD_DIM
A_CHUNK = 128
B_WIDTH = D_MODEL // 2
B_HEADS = B_WIDTH // HEAD_DIM
B_KV_HEADS = 2
B_GROUP = B_HEADS // B_KV_HEADS
B_KV_WIDTH = B_KV_HEADS * HEAD_DIM
C_WIDTH = D_MODEL // 4
C_HEADS = C_WIDTH // HEAD_DIM
OFF_B = 2 * A_WIDTH
OFF_KV = OFF_B + B_WIDTH
OFF_V = OFF_KV + B_KV_WIDTH
OFF_C = OFF_KV + 2 * B_KV_WIDTH
OFF_G = OFF_C + 4 * C_WIDTH
IN_WIDTH = OFF_G + C_WIDTH
N_GROUPS = 4
EXPERTS_PER_GROUP = 8
N_EXPERTS = N_GROUPS * EXPERTS_PER_GROUP
TOP_K = 2
D_FF_EXPERT = D_MODEL // 2

HEAD_SHIFT = HEAD_DIM.bit_length() - 1
GROUP_SHIFT = EXPERTS_PER_GROUP.bit_length() - 1
ROPE_AXIS = HEAD_DIM // 2
ROPE_PAIR = ROPE_AXIS // 2
MOD_ROWS = 16
MOD_COLS = 1536
ROUTER_LANES = 128
PROJ_ROWS = 512
HGRN_BLOCK = 32
HGRN_GROUP = 8
LOG2E = 1.4426950408889634
ATTN_ROWS = 256
ATTN_KEYS = 512
SAFE_SHIFT = 60
SAFE_DECAY = 80.0
MOE_ROWS = 512
ROUTE_ROWS = 512
SC_WINDOW = 128
SC_ROW = 256
PLANES = D_MODEL // (2 * SC_ROW)
VMEM_LIMIT = 48 * 1024 * 1024


def _cparams(*sem):
    return pltpu.CompilerParams(dimension_semantics=sem, vmem_limit_bytes=VMEM_LIMIT)


def _head_ones(n, dtype):
    r = lax.broadcasted_iota(jnp.int32, (n, n), 0) >> HEAD_SHIFT
    c = lax.broadcasted_iota(jnp.int32, (n, n), 1) >> HEAD_SHIFT
    return (r == c).astype(dtype)


def _head_sum(x, ones_bd):
    return jnp.dot(x.astype(BF16), ones_bd, preferred_element_type=F32)


def _head_rms(x, ones_bd):
    w = ones_bd.shape[0]
    if x.shape[1] > w:
        return jnp.concatenate([_head_rms(x[:, c:c + w], ones_bd) for c in range(0, x.shape[1], w)], axis=1)
    return x * lax.rsqrt(_head_sum(x * x, ones_bd) * (1.0 / HEAD_DIM) + EPS)


def _pack_rows(y):
    bits = lax.bitcast_convert_type(y.astype(BF16).astype(F32), jnp.uint32)
    half = y.shape[1] // 2
    return lax.bitcast_convert_type(bits[:, :half] | (bits[:, half:] >> 16), F32)


def _unpack_rows(w):
    bits = lax.bitcast_convert_type(w, jnp.uint32)
    hi = lax.bitcast_convert_type(bits & jnp.uint32(0xFFFF0000), F32)
    lo = lax.bitcast_convert_type(bits << 16, F32)
    return hi, lo


def _pack_planes(y, ref, lead=()):
    for p in range(PLANES):
        ref[(p,) + lead] = _pack_rows(y[:, 2 * p * SC_ROW:(2 * p + 2) * SC_ROW])


def _mod_kernel(c_ref, w_ref, b_ref, o_ref):
    a = jax.nn.silu(c_ref[...])
    w = w_ref[0]
    a_hi, w_hi = a.astype(BF16), w.astype(BF16)
    a_lo, w_lo = (a - a_hi.astype(F32)).astype(BF16), (w - w_hi.astype(F32)).astype(BF16)
    o_ref[...] = (jnp.dot(a_hi, w_hi, preferred_element_type=F32) + jnp.dot(a_hi, w_lo, preferred_element_type=F32)
                  + jnp.dot(a_lo, w_hi, preferred_element_type=F32) + b_ref[...])


def _mod(cc, w_mod, b_mod, layer):
    n = w_mod.shape[2]
    tn = MOD_COLS
    return pl.pallas_call(
        _mod_kernel,
        out_shape=jax.ShapeDtypeStruct((MOD_ROWS, n), F32),
        grid=(n // tn,),
        in_specs=[pl.BlockSpec((MOD_ROWS, D_MODEL), lambda j: (0, 0)),
                  pl.BlockSpec((1, D_MODEL, tn), lambda j: (layer, 0, j)),
                  pl.BlockSpec((1, tn), lambda j: (0, j))],
        out_specs=pl.BlockSpec((MOD_ROWS, tn), lambda j: (0, j)),
        compiler_params=_cparams("arbitrary"),
        name="mod",
    )(cc, w_mod, b_mod.reshape(1, n))


def _rope(xn, c_ref, sp_ref, sm_ref):
    w = xn.shape[-1]
    rep = lambda ref: jnp.tile(ref[...], (1, w // ref.shape[-1]))
    return xn * rep(c_ref) + pltpu.roll(xn, ROPE_PAIR, 1) * rep(sp_ref) + pltpu.roll(xn, w - ROPE_PAIR, 1) * rep(sm_ref)


def _combined(x_ref, pk_ref, info_ref, g_ref):
    info = info_ref[...]
    g1 = info[:, 2:3]
    g2 = info[:, 3:4]
    parts = []
    for p in range(PLANES):
        hi1, lo1 = _unpack_rows(pk_ref[TOP_K * p])
        hi2, lo2 = _unpack_rows(pk_ref[TOP_K * p + 1])
        parts += [g1 * hi1 + g2 * hi2, g1 * lo1 + g2 * lo2]
    return x_ref[0] + g_ref[0] * jnp.concatenate(parts, axis=1)


def _gmlp(z, ws_ref, bias_ref):
    gz = jax.nn.gelu(z)
    u = gz[:, :A_WIDTH]
    vn = _head_rms(gz[:, A_WIDTH:], _head_ones(A_WIDTH, BF16))
    lane_head = lax.broadcasted_iota(jnp.int32, vn.shape, 1) >> HEAD_SHIFT
    acc = bias_ref[...]
    for hh in range(A_HEADS):
        vh = jnp.where(lane_head == hh, vn, 0.0).astype(BF16)
        acc = acc + jnp.dot(ws_ref[hh], vh, preferred_element_type=F32)
    return u * acc


def _inproj_kernel(*refs, pending):
    if pending:
        (x_ref, pk_ref, info_ref, g2_ref, mul_ref, add_ref, w_ref, qc_ref, qsp_ref, qsm_ref, kc_ref, ksp_ref, ksm_ref,
         ws_ref, bias_ref, xo_ref, ya_ref, q_ref, k_ref, v_ref, zc_ref, g_ref) = refs
        x = _combined(x_ref, pk_ref, info_ref, g2_ref)
        xo_ref[0] = x
    else:
        (x_ref, mul_ref, add_ref, w_ref, qc_ref, qsp_ref, qsm_ref, kc_ref, ksp_ref, ksm_ref,
         ws_ref, bias_ref, ya_ref, q_ref, k_ref, v_ref, zc_ref, g_ref) = refs
        x = x_ref[0]
    ms = jnp.mean(x * x, axis=-1, keepdims=True)
    h = x * lax.rsqrt(ms + EPS) * mul_ref[0] + add_ref[0]
    y = jnp.dot(h.astype(BF16), w_ref[...], preferred_element_type=F32)
    for c0 in range(0, x.shape[0], A_CHUNK):
        ya_ref[0, c0:c0 + A_CHUNK, :] = _gmlp(y[c0:c0 + A_CHUNK, :OFF_B], ws_ref, bias_ref).astype(BF16)
    qn = _head_rms(y[:, OFF_B:OFF_KV], _head_ones(B_WIDTH // 2, BF16))
    q_ref[0] = _rope(qn, qc_ref, qsp_ref, qsm_ref).astype(BF16)
    kn = _head_rms(y[:, OFF_KV:OFF_V], _head_ones(B_KV_WIDTH, BF16))
    k_ref[0] = _rope(kn, kc_ref, ksp_ref, ksm_ref).astype(BF16)
    v_ref[0] = y[:, OFF_V:OFF_C].astype(BF16)
    zc_ref[0] = y[:, OFF_C:OFF_G]
    g_ref[0] = y[:, OFF_G:]


def _inproj(x, mul, add, w_bf, q_tabs, k_tabs, ws_bf, bias2d, pending=None):
    b, t, d = x.shape
    tm = min(PROJ_ROWS, t)
    row = lambda i, bb: (bb, i, 0)
    vec = lambda i, bb: (bb, 0, 0)
    tab = lambda i, bb: (i, 0)
    widths = (A_WIDTH, B_WIDTH, B_KV_WIDTH, B_KV_WIDTH, 4 * C_WIDTH, C_WIDTH)
    dtypes = (BF16, BF16, BF16, BF16, F32, F32)
    pre_specs, pre_args = [], []
    if pending is not None:
        picked, info, gate, row0 = pending
        off = row0 // tm
        widths, dtypes = (d,) + widths, (F32,) + dtypes
        pre_specs = [pl.BlockSpec((PLANES * TOP_K, tm, SC_ROW), lambda i, bb: (0, off + bb * (t // tm) + i, 0)),
                     pl.BlockSpec((tm, ROUTER_LANES), lambda i, bb: (off + bb * (t // tm) + i, 0)),
                     pl.BlockSpec((1, 1, d), vec)]
        pre_args = [picked, info, gate]
    return pl.pallas_call(
        functools.partial(_inproj_kernel, pending=pending is not None),
        out_shape=[jax.ShapeDtypeStruct((b, t, w), dt) for w, dt in zip(widths, dtypes)],
        grid=(t // tm, b),
        in_specs=[pl.BlockSpec((1, tm, d), row)] + pre_specs
                 + [pl.BlockSpec((1, 1, d), vec),
                    pl.BlockSpec((1, 1, d), vec),
                    pl.BlockSpec((d, IN_WIDTH), lambda i, bb: (0, 0))]
                 + [pl.BlockSpec((tm, B_KV_WIDTH), tab)] * 6
                 + [pl.BlockSpec((A_HEADS, A_CHUNK, A_CHUNK), lambda i, bb: (0, 0, 0)),
                    pl.BlockSpec((A_CHUNK, A_WIDTH), lambda i, bb: (0, 0))],
        out_specs=[pl.BlockSpec((1, tm, w), row) for w in widths],
        compiler_params=_cparams("arbitrary", "arbitrary"),
        name="inproj",
    )(x, *pre_args, mul, add, w_bf, *q_tabs, *k_tabs, ws_bf, bias2d)


def _rope_tables(t, w, scale, width, rotate):
    ws = w.astype(F32) * scale
    if not rotate:
        c = jnp.broadcast_to(jnp.tile(ws, width // HEAD_DIM)[None, :], (t, width))
        z = jnp.zeros((t, width), F32)
        return c, z, z
    pos = jnp.arange(t)
    row = (pos // GRID_W).astype(F32)
    col = (pos % GRID_W).astype(F32)
    inv_freq = 1.0 / (ROPE_BASE ** (jnp.arange(0, HEAD_DIM // 2, 2, dtype=F32) / (HEAD_DIM // 2)))
    dd = jnp.arange(HEAD_DIM)
    axis = dd // ROPE_AXIS
    half = (dd % ROPE_AXIS) // ROPE_PAIR
    ang = jnp.where(axis[None, :] == 0, row[:, None], col[:, None]) * inv_freq[dd % ROPE_PAIR][None, :]
    cos, sin = jnp.cos(ang), jnp.sin(ang)
    c = cos * ws[None, :]
    sm = jnp.where(half[None, :] == 0, -sin * jnp.roll(ws, -ROPE_PAIR)[None, :], 0.0)
    sp = jnp.where(half[None, :] == 1, sin * jnp.roll(ws, ROPE_PAIR)[None, :], 0.0)
    rep = width // HEAD_DIM
    return jnp.tile(c, (1, rep)), jnp.tile(sp, (1, rep)), jnp.tile(sm, (1, rep))


def _attn_kernel(flag_ref, q_ref, *refs, n_seg):
    kv_refs, o_ref = refs[:2 * n_seg], refs[2 * n_seg]
    tq = q_ref.shape[1]
    dh = HEAD_DIM

    def heads(j):
        q4 = jnp.concatenate([q_ref[0, :, (B_GROUP * j + gg) * dh:(B_GROUP * j + gg + 1) * dh]
                              for gg in range(B_GROUP)], axis=0)
        ks, vs = [], []
        for sg in range(n_seg):
            s_len = kv_refs[2 * sg].shape[1]
            for c0 in range(0, s_len, ATTN_KEYS):
                c1 = min(c0 + ATTN_KEYS, s_len)
                ks.append(kv_refs[2 * sg][0, c0:c1, j * dh:(j + 1) * dh])
                vs.append(kv_refs[2 * sg + 1][0, c0:c1, j * dh:(j + 1) * dh])
        return q4, ks, vs

    def scores(q4, ks):
        return [lax.dot_general(kk, q4, (((1,), (1,)), ((), ())), preferred_element_type=F32) for kk in ks]

    def finish(j, ps, vs):
        l = functools.reduce(jnp.add, [jnp.sum(p, axis=0, keepdims=True) for p in ps])
        acc = sum(lax.dot_general(vv, p.astype(BF16), (((0,), (0,)), ((), ())), preferred_element_type=F32)
                  for p, vv in zip(ps, vs))
        o = (acc / l).T
        for gg in range(B_GROUP):
            hh = B_GROUP * j + gg
            o_ref[0, :, hh * dh:(hh + 1) * dh] = o[gg * tq:(gg + 1) * tq].astype(BF16)

    @pl.when(flag_ref[0] > 0)
    def _():
        shift = flag_ref[1].astype(F32)
        for j in range(B_KV_HEADS):
            q4, ks, vs = heads(j)
            finish(j, [jnp.exp2(s - shift) for s in scores(q4, ks)], vs)

    @pl.when(flag_ref[0] <= 0)
    def _():
        for j in range(B_KV_HEADS):
            q4, ks, vs = heads(j)
            ss = scores(q4, ks)
            m = functools.reduce(jnp.maximum, [jnp.max(s, axis=0, keepdims=True) for s in ss])
            finish(j, [jnp.exp2(s - m) for s in ss], vs)


def _attn(flag, q, kv_segs):
    b, t, w = q.shape
    tq = min(ATTN_ROWS, t)
    n_seg = len(kv_segs)
    kv_flat, kv_specs = [], []
    for kk, vv in kv_segs:
        s_len, kw = kk.shape[1:]
        kv_flat += [kk, vv]
        kv_specs += [pl.BlockSpec((1, s_len, kw), lambda bb, i, fl: (bb, 0, 0))] * 2
    return pl.pallas_call(
        functools.partial(_attn_kernel, n_seg=n_seg),
        out_shape=jax.ShapeDtypeStruct((b, t, w), BF16),
        grid_spec=pltpu.PrefetchScalarGridSpec(
            num_scalar_prefetch=1,
            grid=(b, t // tq),
            in_specs=[pl.BlockSpec((1, tq, w), lambda bb, i, fl: (bb, i, 0))] + kv_specs,
            out_specs=pl.BlockSpec((1, tq, w), lambda bb, i, fl: (bb, i, 0))),
        compiler_params=_cparams("arbitrary", "arbitrary"),
        name="attn",
    )(flag, q, *kv_flat)


def _scan_rows(x, reverse):
    n = x.shape[0]
    rows = lax.broadcasted_iota(jnp.int32, x.shape, 0)
    sh = 1
    while sh < n:
        if reverse:
            x = x + jnp.where(rows < n - sh, pltpu.roll(x, n - sh, 0), 0.0)
        else:
            x = x + jnp.where(rows >= sh, pltpu.roll(x, sh, 0), 0.0)
        sh *= 2
    return x


def _stack_heads(x, lane_head):
    return jnp.concatenate([jnp.where(lane_head == hh, x, 0.0) for hh in range(C_HEADS)], axis=0)


def _hgrn_kernel(z_ref, zc_ref, lbc_ref, *refs, ctx_out):
    if ctx_out:
        o_ref, oc_ref, st_ref, kx_ref, bx_ref, vx_ref, flag_ref = refs
    else:
        o_ref, st_ref, kx_ref, bx_ref, vx_ref, flag_ref = refs
        oc_ref = None
    n = C_WIDTH
    nb = HGRN_BLOCK
    nblk_c = zc_ref.shape[1] // nb
    nblk_l = z_ref.shape[1] // nb
    ones_bd = _head_ones(n, BF16)
    rows = lax.broadcasted_iota(jnp.int32, (nb, n), 0)
    lane_head = lax.broadcasted_iota(jnp.int32, (nb, n), 1) >> HEAD_SHIFT
    lane_head64 = lax.broadcasted_iota(jnp.int32, (HEAD_DIM, n), 1) >> HEAD_SHIFT
    low_half = (lax.broadcasted_iota(jnp.int32, (HEAD_DIM, 2 * HEAD_DIM), 1) < HEAD_DIM)
    sc_t = lax.broadcasted_iota(jnp.int32, (nb, C_HEADS * nb), 0)
    sc_s = lax.broadcasted_iota(jnp.int32, (nb, C_HEADS * nb), 1) & (nb - 1)

    def gates(z, d):
        one_m_lb = lbc_ref[d, 0:1, :]
        log1m_lb = lbc_ref[d, 1:2, :]
        log_lb = lbc_ref[d, 2:3, :]
        lb_pos = lbc_ref[d, 3:4, :] > 0.5
        soft = jnp.log(1.0 + jnp.exp(-jnp.abs(z)))
        log_rest = log1m_lb + (jnp.minimum(z, 0.0) - soft)
        lse = jnp.maximum(log_lb, log_rest) + jnp.log(1.0 + jnp.exp(-jnp.abs(log_lb - log_rest)))
        return jnp.where(lb_pos, lse, log_rest), one_m_lb * jnp.exp(jnp.minimum(-z, 0.0) - soft)

    def group_blocks(nblk, i):
        grp = HGRN_GROUP if nblk % HGRN_GROUP == 0 else 1
        fwd = [i * grp + gg for gg in range(grp)]
        return grp, fwd, [nblk - 1 - blk for blk in fwd]

    def block_bound(src_ref, blk, d):
        zz = src_ref[0, pl.ds(pl.multiple_of(blk * nb, nb), nb), (1 + d) * n:(2 + d) * n]
        step_bound = jnp.minimum(lbc_ref[d, 4:5, :],
                                 jnp.maximum(-zz, 0.0) + (jnp.log(2.0) - lbc_ref[d, 1:2, :]))
        return jnp.sum(step_bound, axis=0, keepdims=True)

    def flag_groups(src_ref, nblk, base):
        def body(i, carry):
            _, fwd, bwd = group_blocks(nblk, i)
            worst = functools.reduce(jnp.maximum, [block_bound(src_ref, blk, d)
                                                   for d, blks in ((0, fwd), (1, bwd)) for blk in blks])
            flag_ref[base + i] = (jnp.max(worst) <= SAFE_DECAY).astype(jnp.int32)
            return carry
        lax.fori_loop(0, nblk // group_blocks(nblk, 0)[0], body, 0)

    def worst_bound(src_ref, nblk):
        def body(blk, worst):
            return jnp.maximum(worst, jnp.maximum(block_bound(src_ref, blk, 0), block_bound(src_ref, blk, 1)))
        return lax.fori_loop(0, nblk, body, jnp.zeros((1, n), F32))

    n_grp_c = nblk_c // group_blocks(nblk_c, 0)[0]
    all_safe = jnp.max(jnp.maximum(worst_bound(zc_ref, nblk_c), worst_bound(z_ref, nblk_l))) <= SAFE_DECAY
    st_ref[...] = jnp.zeros_like(st_ref)
    o_ref[...] = jnp.zeros_like(o_ref)
    if ctx_out:
        oc_ref[...] = jnp.zeros_like(oc_ref)

    def step(src_ref, dst_ref, blk, d, fast):
        reverse = d == 1
        r0 = pl.multiple_of(blk * nb, nb)
        v = src_ref[0, pl.ds(r0, nb), 3 * n:4 * n]
        log_f, k = gates(src_ref[0, pl.ds(r0, nb), (1 + d) * n:(2 + d) * n], d)
        bc = _scan_rows(log_f, reverse)
        edge = 0 if reverse else nb - 1
        b_edge = bc[edge:edge + 1, :]
        st = st_ref[d]
        v_bf = v.astype(BF16)

        if dst_ref is not None:
            q = jax.nn.silu(src_ref[0, pl.ds(r0, nb), 0:n])
            qt = (q * jnp.exp(bc)).astype(BF16)
            o = lax.dot_general(qt, _stack_heads(st, lane_head64).astype(BF16), (((1,), (1,)), ((), ())),
                                preferred_element_type=F32)

            def intra_fast():
                kt = _stack_heads(k * jnp.exp(-bc), lane_head).astype(BF16)
                sc = lax.dot_general(qt, kt, (((1,), (1,)), ((), ())), preferred_element_type=F32)
                keep = (sc_s >= sc_t) if reverse else (sc_s <= sc_t)
                sc = jnp.where(keep, sc, 0.0).astype(BF16)
                return jnp.dot(sc, _stack_heads(v, lane_head).astype(BF16), preferred_element_type=F32)

            def intra_exact():
                kx_ref[d] = k
                bx_ref[d] = bc
                vx_ref[d] = v

                def sbody(s, acc):
                    keep = (rows <= s) if reverse else (rows >= s)
                    e = jnp.exp(jnp.where(keep, bc - bx_ref[d, pl.ds(s, 1), :], 0.0))
                    p = jnp.where(keep, q * e * kx_ref[d, pl.ds(s, 1), :], 0.0)
                    sc = jnp.dot(p.astype(BF16), ones_bd, preferred_element_type=F32)
                    return acc + sc * vx_ref[d, pl.ds(s, 1), :]

                return lax.fori_loop(0, nb, sbody, jnp.zeros((nb, n), F32))

            o = o + (intra_fast() if fast else intra_exact())
            dst_ref[0, pl.ds(r0, nb), :] += o

        kd = (k * jnp.exp(b_edge - bc)).astype(BF16)
        full = lax.dot_general(v_bf, kd, (((0,), (0,)), ((), ())), preferred_element_type=F32)
        upd = jnp.concatenate(
            [jnp.where(low_half,
                       full[(2 * c) * HEAD_DIM:(2 * c + 1) * HEAD_DIM, 2 * c * HEAD_DIM:(2 * c + 2) * HEAD_DIM],
                       full[(2 * c + 1) * HEAD_DIM:(2 * c + 2) * HEAD_DIM, 2 * c * HEAD_DIM:(2 * c + 2) * HEAD_DIM])
             for c in range(C_HEADS // 2)], axis=1)
        st_ref[d] = st * jnp.exp(b_edge) + upd

    def run(src_ref, dst_ref, nblk, base, per_group):
        def body(i, carry):
            _, fwd, bwd = group_blocks(nblk, i)

            def group(fast):
                for bf, bb in zip(fwd, bwd):
                    step(src_ref, dst_ref, bf, 0, fast)
                    step(src_ref, dst_ref, bb, 1, fast)

            if per_group:
                safe = flag_ref[base + i]
                pl.when(safe > 0)(functools.partial(group, True))
                pl.when(safe <= 0)(functools.partial(group, False))
            else:
                group(True)
            return carry
        lax.fori_loop(0, nblk // group_blocks(nblk, 0)[0], body, 0)

    @pl.when(all_safe)
    def _():
        run(zc_ref, oc_ref, nblk_c, 0, False)
        run(z_ref, o_ref, nblk_l, n_grp_c, False)

    @pl.when(jnp.logical_not(all_safe))
    def _():
        flag_groups(zc_ref, nblk_c, 0)
        flag_groups(z_ref, nblk_l, n_grp_c)
        run(zc_ref, oc_ref, nblk_c, 0, True)
        run(z_ref, o_ref, nblk_l, n_grp_c, True)


def _hgrn(zc, zc_c, lbc, ctx_out):
    b, t, w = zc.shape
    lc = zc_c.shape[1]
    n = C_WIDTH
    row = lambda bb: (bb, 0, 0)
    out_shape = [jax.ShapeDtypeStruct((b, t, n), F32)]
    out_specs = [pl.BlockSpec((1, t, n), row)]
    if ctx_out:
        out_shape.append(jax.ShapeDtypeStruct((b, lc, n), F32))
        out_specs.append(pl.BlockSpec((1, lc, n), row))
    res = pl.pallas_call(
        functools.partial(_hgrn_kernel, ctx_out=ctx_out),
        out_shape=out_shape,
        grid=(b,),
        in_specs=[pl.BlockSpec((1, t, w), row),
                  pl.BlockSpec((1, lc, w), row),
                  pl.BlockSpec((2, 8, n), lambda bb: (0, 0, 0))],
        out_specs=out_specs,
        scratch_shapes=[pltpu.VMEM((2, HEAD_DIM, n), F32)]
                       + [pltpu.VMEM((2, HGRN_BLOCK, n), F32)] * 3
                       + [pltpu.SMEM(((t + lc) // HGRN_BLOCK,), jnp.int32)],
        compiler_params=_cparams("arbitrary"),
        name="hgrn",
    )(zc, zc_c, lbc)
    return (res[0], res[1]) if ctx_out else (res[0], None)


def _select_experts(lg):
    lane = lax.broadcasted_iota(jnp.int32, lg.shape, 1)
    lane_f = lane.astype(F32)
    neg = -jnp.inf
    gl = jnp.where(lane < N_GROUPS, lg, neg)
    gmax = jnp.max(gl, axis=1, keepdims=True)
    grp = jnp.min(jnp.where(gl == gmax, lane_f, float(ROUTER_LANES)), axis=1, keepdims=True).astype(jnp.int32)
    p_grp = 1.0 / jnp.sum(jnp.exp(gl - gmax), axis=1, keepdims=True)
    in_grp = (lane >= N_GROUPS) & (lane < N_GROUPS + N_EXPERTS) & (((lane - N_GROUPS) >> GROUP_SHIFT) == grp)
    el = jnp.where(in_grp, lg, neg)
    v1 = jnp.max(el, axis=1, keepdims=True)
    i1 = jnp.min(jnp.where(el == v1, lane_f, float(ROUTER_LANES)), axis=1, keepdims=True)
    el2 = jnp.where(lane_f == i1, neg, el)
    v2 = jnp.max(el2, axis=1, keepdims=True)
    i2 = jnp.min(jnp.where(el2 == v2, lane_f, float(ROUTER_LANES)), axis=1, keepdims=True)
    rr = jnp.exp(v2 - v1)
    g1 = p_grp / (1.0 + rr)
    g2 = p_grp * rr / (1.0 + rr)
    sel = jnp.where(lane == 0, i1, jnp.where(lane == 1, i2, jnp.where(lane == 2, g1, jnp.where(lane == 3, g2, 0.0))))
    counts = jnp.sum(((lane_f == i1) | (lane_f == i2)).astype(F32), axis=0, keepdims=True)
    return sel, counts


def _outproj_kernel(x_ref, ya_ref, yb_ref, o_ref, g_ref, w_ref, gate_ref, mul_ref, add_ref, hw_ref,
                    wr_ref, br_ref, xo_ref, h2_ref, sel_ref, cnt_ref):
    yc = _head_rms(o_ref[0], _head_ones(C_WIDTH, BF16)) * hw_ref[...] * jax.nn.silu(g_ref[0])
    y = jnp.dot(ya_ref[0], w_ref[0:A_WIDTH, :], preferred_element_type=F32)
    y = y + jnp.dot(yb_ref[0], w_ref[A_WIDTH:A_WIDTH + B_WIDTH, :], preferred_element_type=F32)
    y = y + jnp.dot(yc.astype(BF16), w_ref[A_WIDTH + B_WIDTH:, :], preferred_element_type=F32)
    xn = x_ref[0] + gate_ref[0] * y
    xo_ref[0] = xn
    ms = jnp.mean(xn * xn, axis=-1, keepdims=True)
    h2 = xn * lax.rsqrt(ms + EPS) * mul_ref[0] + add_ref[0]
    _pack_planes(h2, h2_ref, (0,))
    h_hi = h2.astype(BF16)
    h_lo = (h2 - h_hi.astype(F32)).astype(BF16)
    both = jnp.dot(h_hi, wr_ref[...], preferred_element_type=F32)
    lg = (both[:, :ROUTER_LANES] + both[:, ROUTER_LANES:] + br_ref[...]
          + jnp.dot(h_lo, wr_ref[:, 0:ROUTER_LANES], preferred_element_type=F32))
    sel, counts = _select_experts(lg)
    sel_ref[0] = sel

    @pl.when((pl.program_id(0) == 0) & (pl.program_id(1) == 0))
    def _():
        cnt_ref[...] = jnp.zeros_like(cnt_ref)
    cnt_ref[...] += counts


def _outproj(x, ya, yb, o, g, w_bf, gate, mul, add, hw, wr, br):
    b, t, d = x.shape
    tm = min(PROJ_ROWS, t)
    row = lambda bb, i: (bb, i, 0)
    vec = lambda bb, i: (bb, 0, 0)
    const = lambda bb, i: (0, 0)
    return pl.pallas_call(
        _outproj_kernel,
        out_shape=[jax.ShapeDtypeStruct((b, t, d), F32),
                   jax.ShapeDtypeStruct((PLANES, b, t, SC_ROW), F32),
                   jax.ShapeDtypeStruct((b, t, ROUTER_LANES), F32),
                   jax.ShapeDtypeStruct((8, ROUTER_LANES), F32)],
        grid=(b, t // tm),
        in_specs=[pl.BlockSpec((1, tm, d), row),
                  pl.BlockSpec((1, tm, A_WIDTH), row),
                  pl.BlockSpec((1, tm, B_WIDTH), row),
                  pl.BlockSpec((1, tm, C_WIDTH), row),
                  pl.BlockSpec((1, tm, C_WIDTH), row),
                  pl.BlockSpec((d, d), const),
                  pl.BlockSpec((1, 1, d), vec),
                  pl.BlockSpec((1, 1, d), vec),
                  pl.BlockSpec((1, 1, d), vec),
                  pl.BlockSpec((1, C_WIDTH), const),
                  pl.BlockSpec((d, 2 * ROUTER_LANES), const),
                  pl.BlockSpec((1, ROUTER_LANES), const)],
        out_specs=[pl.BlockSpec((1, tm, d), row),
                   pl.BlockSpec((PLANES, 1, tm, SC_ROW), lambda bb, i: (0, bb, i, 0)),
                   pl.BlockSpec((1, tm, ROUTER_LANES), row),
                   pl.BlockSpec((8, ROUTER_LANES), const)],
        compiler_params=_cparams("arbitrary", "arbitrary"),
        name="outproj",
    )(x, ya, yb, o, g, w_bf, gate, mul, add, hw, wr, br)


def _route_kernel(sel_ref, cnt_ref, info_ref, meta_ref, base_ref):
    i = pl.program_id(0)
    tm = sel_ref.shape[0]
    lane = lax.broadcasted_iota(jnp.int32, (tm, ROUTER_LANES), 1)
    lane_f = lane.astype(F32)
    sel = sel_ref[...]
    hit1 = lane_f == sel[:, 0:1]
    hit2 = lane_f == sel[:, 1:2]
    onehot = (hit1 | hit2).astype(F32)

    @pl.when(i == 0)
    def _():
        counts = cnt_ref[...]
        padded = jnp.floor((counts + (MOE_ROWS - 1.0)) * (1.0 / MOE_ROWS)) * MOE_ROWS
        r = lax.broadcasted_iota(jnp.int32, (ROUTER_LANES, ROUTER_LANES), 0)
        c = lax.broadcasted_iota(jnp.int32, (ROUTER_LANES, ROUTER_LANES), 1)
        ends = jnp.dot(padded, (r <= c).astype(F32), preferred_element_type=F32,
                       precision=lax.Precision.HIGHEST)
        base_ref[...] = (ends - padded)[0:1]
        row = lax.broadcasted_iota(jnp.int32, (8, ROUTER_LANES), 0)
        meta_ref[...] = jnp.where(row == 0, counts, jnp.where(row == 1, ends - padded, ends))

    tr = lax.broadcasted_iota(jnp.int32, (tm, tm), 0)
    tc = lax.broadcasted_iota(jnp.int32, (tm, tm), 1)
    before = jnp.dot((tc < tr).astype(BF16), onehot.astype(BF16), preferred_element_type=F32)
    pos = base_ref[...] + before
    d1 = jnp.sum(jnp.where(hit1, pos, 0.0), axis=1, keepdims=True)
    d2 = jnp.sum(jnp.where(hit2, pos, 0.0), axis=1, keepdims=True)
    base_ref[...] += jnp.sum(onehot, axis=0, keepdims=True)
    info_ref[...] = jnp.where(lane == 0, d1, jnp.where(lane == 1, d2, sel))


def _route(sel, counts):
    n = sel.shape[0]
    tm = ROUTE_ROWS if n % ROUTE_ROWS == 0 else ROUTE_ROWS // 2
    return pl.pallas_call(
        _route_kernel,
        out_shape=[jax.ShapeDtypeStruct((n, ROUTER_LANES), F32),
                   jax.ShapeDtypeStruct((8, ROUTER_LANES), F32)],
        grid=(n // tm,),
        in_specs=[pl.BlockSpec((tm, ROUTER_LANES), lambda i: (i, 0)),
                  pl.BlockSpec((8, ROUTER_LANES), lambda i: (0, 0))],
        out_specs=[pl.BlockSpec((tm, ROUTER_LANES), lambda i: (i, 0)),
                   pl.BlockSpec((8, ROUTER_LANES), lambda i: (0, 0))],
        scratch_shapes=[pltpu.VMEM((1, ROUTER_LANES), F32)],
        compiler_params=_cparams("arbitrary"),
        name="route",
    )(sel, counts)


def _sc_mesh():
    return plsc.VectorSubcoreMesh(core_axis_name="c", subcore_axis_name="s")


def _sc_gather(table, idx):
    n = idx.shape[0]
    d = table.shape[1]

    @functools.partial(pl.kernel, out_type=jax.ShapeDtypeStruct((n, d), table.dtype), mesh=_sc_mesh())
    def gather(x_hbm, i_hbm, o_hbm):
        def body(i_vmem, o_vmem):
            pltpu.sync_copy(x_hbm.at[i_vmem.at[0]], o_vmem)

        pltpu.emit_pipeline(
            body,
            grid=(n // SC_WINDOW,),
            in_specs=[pl.BlockSpec((1, SC_WINDOW), lambda i: (0, i))],
            out_specs=[pl.BlockSpec((SC_WINDOW, d), lambda i: (i, 0))],
            core_axis_name=("c", "s"),
            dimension_semantics=(pltpu.PARALLEL,),
        )(i_hbm, o_hbm)

    return gather(table, idx.reshape(1, n))


def _sc_scatter2(rows, idx0, idx1, n_out):
    m, d = rows.shape

    @functools.partial(pl.kernel, out_type=jax.ShapeDtypeStruct((n_out, d), rows.dtype), mesh=_sc_mesh())
    def scatter(x_hbm, i0_hbm, i1_hbm, o_hbm):
        def body(x_vmem, i0_vmem, i1_vmem):
            pltpu.sync_copy(x_vmem, o_hbm.at[i0_vmem.at[0]])
            pltpu.sync_copy(x_vmem, o_hbm.at[i1_vmem.at[0]])

        pltpu.emit_pipeline(
            body,
            grid=(m // SC_WINDOW,),
            in_specs=[pl.BlockSpec((SC_WINDOW, d), lambda i: (i, 0)),
                      pl.BlockSpec((1, SC_WINDOW), lambda i: (0, i)),
                      pl.BlockSpec((1, SC_WINDOW), lambda i: (0, i))],
            out_specs=[],
            core_axis_name=("c", "s"),
            dimension_semantics=(pltpu.PARALLEL,),
        )(x_hbm, i0_hbm, i1_hbm)

    return scatter(rows, idx0.reshape(1, m), idx1.reshape(1, m))


def _moe_kernel(be_ref, nu_ref, first_ref, slot_ref, nxt_ref, x_ref, w1_hbm, w3_hbm, w2_hbm, o_ref,
                w1f, w3f, w2f, w1b, w3b, w2b, sem, *, layer):
    i = pl.program_id(0)

    def fetch(e, slot):
        return [pltpu.make_async_copy(src.at[layer, e], dst.at[slot], sem.at[n, slot])
                for n, (src, dst) in enumerate(((w1_hbm, w1f), (w3_hbm, w3f), (w2_hbm, w2f)))]

    @pl.when((i == 0) & (nu_ref[0] > 0))
    def _():
        for cp in fetch(be_ref[0], 0):
            cp.start()

    @pl.when((first_ref[i] > 0) & (i < nu_ref[0]))
    def _():
        slot = slot_ref[i]
        for cp in fetch(be_ref[i], slot):
            cp.wait()
        w1b[...] = w1f[slot].astype(BF16)
        w3b[...] = w3f[slot].astype(BF16)
        w2b[...] = w2f[slot].astype(BF16)

        @pl.when(nxt_ref[i] >= 0)
        def _():
            for cp in fetch(nxt_ref[i], 1 - slot):
                cp.start()

    @pl.when(i < nu_ref[0])
    def _():
        parts = [h.astype(BF16) for p in range(PLANES) for h in _unpack_rows(x_ref[p])]
        a = sum(jnp.dot(h, w1b[q * SC_ROW:(q + 1) * SC_ROW, :], preferred_element_type=F32)
                for q, h in enumerate(parts))
        b = sum(jnp.dot(h, w3b[q * SC_ROW:(q + 1) * SC_ROW, :], preferred_element_type=F32)
                for q, h in enumerate(parts))
        hmid = (jax.nn.silu(a) * b).astype(BF16)
        _pack_planes(jnp.dot(hmid, w2b[...], preferred_element_type=F32), o_ref)

    @pl.when(i >= nu_ref[0])
    def _():
        o_ref[...] = jnp.zeros_like(o_ref)


def _moe_mlp(blk_expert, n_used, xs, w1, w3, w2, layer):
    n_rows = xs.shape[1]
    d, f = w1.shape[2:]
    nblk = n_rows // MOE_ROWS
    rows = lambda i, *_: (0, i, 0)
    idx = jnp.arange(nblk, dtype=jnp.int32)
    first = (idx < n_used[0]) & ((idx == 0) | (blk_expert != jnp.roll(blk_expert, 1)))
    slot = (jnp.cumsum(first.astype(jnp.int32)) - 1) % 2
    nxt_first = lax.cummin(jnp.where(first, idx, nblk)[::-1])[::-1]
    nxt_first = jnp.concatenate([nxt_first[1:], jnp.full((1,), nblk, jnp.int32)])
    nxt = jnp.where(nxt_first < nblk, blk_expert[jnp.minimum(nxt_first, nblk - 1)], -1)
    hbm = pl.BlockSpec(memory_space=pl.ANY)
    return pl.pallas_call(
        functools.partial(_moe_kernel, layer=layer),
        out_shape=jax.ShapeDtypeStruct((PLANES, n_rows, SC_ROW), F32),
        grid_spec=pltpu.PrefetchScalarGridSpec(
            num_scalar_prefetch=5,
            grid=(nblk,),
            in_specs=[pl.BlockSpec((PLANES, MOE_ROWS, SC_ROW), rows), hbm, hbm, hbm],
            out_specs=pl.BlockSpec((PLANES, MOE_ROWS, SC_ROW), rows),
            scratch_shapes=[pltpu.VMEM((2, d, f), F32), pltpu.VMEM((2, d, f), F32), pltpu.VMEM((2, f, d), F32),
                            pltpu.VMEM((d, f), BF16), pltpu.VMEM((d, f), BF16), pltpu.VMEM((f, d), BF16),
                            pltpu.SemaphoreType.DMA((3, 2))]),
        compiler_params=_cparams("arbitrary"),
        name="moe",
    )(blk_expert, n_used, first.astype(jnp.int32), slot.astype(jnp.int32), nxt.astype(jnp.int32), xs, w1, w3, w2)


def _combine_kernel(x_ref, pk_ref, info_ref, g_ref, o_ref):
    o_ref[0] = _combined(x_ref, pk_ref, info_ref, g_ref)


def _combine(x, picked, info, gate, row0):
    b, t, d = x.shape
    tm = min(256, t)
    off = row0 // tm
    tok = lambda bb, i: (off + bb * (t // tm) + i, 0)
    tok3 = lambda bb, i: (0, off + bb * (t // tm) + i, 0)
    return pl.pallas_call(
        _combine_kernel,
        out_shape=jax.ShapeDtypeStruct((b, t, d), F32),
        grid=(b, t // tm),
        in_specs=[pl.BlockSpec((1, tm, d), lambda bb, i: (bb, i, 0)),
                  pl.BlockSpec((PLANES * TOP_K, tm, SC_ROW), tok3),
                  pl.BlockSpec((tm, ROUTER_LANES), tok),
                  pl.BlockSpec((1, 1, d), lambda bb, i: (bb, 0, 0))],
        out_specs=pl.BlockSpec((1, tm, d), lambda bb, i: (bb, i, 0)),
        compiler_params=_cparams("arbitrary", "arbitrary"),
        name="combine",
    )(x, picked, info, gate)


def _hier_moe(h2p, sel, counts, w1, w3, w2, layer):
    n_tok = h2p.shape[1]
    info, meta = _route(sel, counts)
    dest = info[:, 0:TOP_K].astype(jnp.int32)
    pad_ends = meta[2, N_GROUPS:N_GROUPS + N_EXPERTS].astype(jnp.int32)
    nblk = -(-(n_tok * TOP_K) // MOE_ROWS) + N_EXPERTS
    n_rows = nblk * MOE_ROWS
    blk_start = jnp.arange(nblk, dtype=jnp.int32) * MOE_ROWS
    blk_expert = jnp.minimum(jnp.sum((pad_ends[None, :] <= blk_start[:, None]).astype(jnp.int32), axis=1),
                             N_EXPERTS - 1)
    n_used = pad_ends[-1:] // MOE_ROWS
    slot = [jnp.concatenate([p * n_rows + dest[:, s] for p in range(PLANES)]) for s in range(TOP_K)]
    xs = _sc_scatter2(h2p.reshape(PLANES * n_tok, SC_ROW), slot[0], slot[1], PLANES * n_rows)
    out = _moe_mlp(blk_expert, n_used, xs.reshape(PLANES, n_rows, SC_ROW), w1, w3, w2, layer)
    idx_all = jnp.concatenate([p * n_rows + dest[:, s] for p in range(PLANES) for s in range(TOP_K)])
    picked = _sc_gather(out.reshape(PLANES * n_rows, SC_ROW), idx_all)
    return picked.reshape(PLANES * TOP_K, n_tok, SC_ROW), info


def _layer(layer, x, xc, pend, c, c_ctx, lb, w_mod, b_mod, norm1_w, w_in, w_s, b_s, q_norm_w, k_norm_w, hgrn_norm_w, w_out,
           norm2_w, w_grp, b_grp, w_exp, b_exp, w1, w3, w2, ctx_out):
    b, t, d = x.shape
    lc = xc.shape[1]
    cc = jnp.zeros((MOD_ROWS, d), F32).at[:b].set(c).at[b].set(c_ctx)
    mod = _mod(cc, w_mod, b_mod, layer)
    sh1, sc1, g1, sh2, sc2, g2 = [m[:, None, :] for m in jnp.split(mod[:b], 6, axis=-1)]
    mod_c = [jnp.broadcast_to(m[None, None, :], (b, 1, d)) for m in jnp.split(mod[b], 6)]

    w_in_bf = w_in.astype(BF16)
    scale = LOG2E * HEAD_DIM ** -0.5
    q_tabs = _rope_tables(t, q_norm_w, scale, B_KV_WIDTH, True)
    k_tabs = _rope_tables(t, k_norm_w, 1.0, B_KV_WIDTH, True)
    qc_tabs = _rope_tables(lc, q_norm_w, scale, B_KV_WIDTH, False)
    kc_tabs = _rope_tables(lc, k_norm_w, 1.0, B_KV_WIDTH, False)
    ws_bf = w_s.astype(BF16)
    bias2d = jnp.repeat(b_s.T, HEAD_DIM, axis=1)
    res = _inproj(x, norm1_w * (1.0 + sc1), sh1, w_in_bf, q_tabs, k_tabs, ws_bf, bias2d,
                  None if pend is None else pend[:2] + (pend[2], 0))
    res_c = _inproj(xc, norm1_w * (1.0 + mod_c[1]), mod_c[0], w_in_bf, qc_tabs, kc_tabs, ws_bf, bias2d,
                    None if pend is None else pend[:2] + (pend[3], b * t))
    if pend is not None:
        x, xc, res, res_c = res[0], res_c[0], res[1:], res_c[1:]
    ya, q, k, v, zc, g = res
    ya_c, q_c, k_c, v_c, zc_c, g_c = res_c

    bound = LOG2E * HEAD_DIM ** 0.5 * jnp.max(jnp.abs(q_norm_w)) * jnp.max(jnp.abs(k_norm_w)) * 1.02
    shift = jnp.ceil(bound)
    attn_flag = jnp.stack([(shift <= SAFE_SHIFT).astype(jnp.int32), shift.astype(jnp.int32)])
    yb = _attn(attn_flag, q, [(k, v), (k_c, v_c)])

    pos = lb > 0.0
    log_lb = jnp.log(jnp.where(pos, lb, 1.0))
    lbc = jnp.stack([1.0 - lb, jnp.log1p(-lb), log_lb, pos.astype(F32), jnp.where(pos, -log_lb, 1e30)], axis=1)
    lbc = jnp.concatenate([lbc, jnp.zeros((2, 3, C_WIDTH), F32)], axis=1)
    o, o_c = _hgrn(zc, zc_c, lbc, ctx_out)

    w_out_bf = w_out.astype(BF16)
    hw = jnp.tile(hgrn_norm_w, C_HEADS)[None, :]
    wr = jnp.zeros((d, ROUTER_LANES), F32).at[:, :N_GROUPS].set(w_grp).at[
        :, N_GROUPS:N_GROUPS + N_EXPERTS].set(w_exp)
    wr_hi = wr.astype(BF16)
    wr = jnp.concatenate([wr_hi, (wr - wr_hi.astype(F32)).astype(BF16)], axis=1)
    br = jnp.zeros((1, ROUTER_LANES), F32).at[0, :N_GROUPS].set(b_grp).at[
        0, N_GROUPS:N_GROUPS + N_EXPERTS].set(b_exp)
    x, h2, sel, cnt = _outproj(x, ya, yb, o, g, w_out_bf, g1, norm2_w * (1.0 + sc2), sh2, hw, wr, br)
    if ctx_out:
        yb_c = _attn(attn_flag, q_c, [(k_c, v_c)])
        xc, h2c, sel_c, cnt_c = _outproj(xc, ya_c, yb_c, o_c, g_c, w_out_bf, mod_c[2],
                                norm2_w * (1.0 + mod_c[4]), mod_c[3], hw, wr, br)
        tokens = jnp.concatenate([h2.reshape(PLANES, -1, SC_ROW), h2c.reshape(PLANES, -1, SC_ROW)], axis=1)
        sel_all = jnp.concatenate([sel.reshape(-1, ROUTER_LANES), sel_c.reshape(-1, ROUTER_LANES)], axis=0)
        picked, info = _hier_moe(tokens, sel_all, cnt + cnt_c, w1, w3, w2, layer)
        return x, xc, (picked, info, g2, mod_c[5])
    picked, info = _hier_moe(h2.reshape(PLANES, -1, SC_ROW), sel.reshape(-1, ROUTER_LANES), cnt, w1, w3, w2, layer)
    return x, xc, (picked, info, g2, None)


def kernel(x, c, ctx, c_ctx, w_mod, b_mod, norm1_w, w_in, w_s, b_s, q_norm_w, k_norm_w, hgrn_lb_logits,
           hgrn_norm_w, w_out, norm2_w, w_grp, b_grp, w_exp, b_exp, w1, w3, w2):
    depth = w_mod.shape[0]
    lb_sm = jax.nn.softmax(hgrn_lb_logits.astype(F32), axis=0)
    lb = jnp.cumsum(lb_sm, axis=0) - lb_sm[0]
    xc = ctx
    pend = None
    for l in range(depth):
        x, xc, pend = _layer(l, x, xc, pend, c, c_ctx, lb[l], w_mod, b_mod[l], norm1_w[l], w_in[l], w_s[l], b_s[l],
                             q_norm_w[l], k_norm_w[l], hgrn_norm_w[l], w_out[l], norm2_w[l], w_grp[l], b_grp[l],
                             w_exp[l], b_exp[l], w1, w3, w2, ctx_out=(l < depth - 1))
    return _combine(x, pend[0], pend[1], pend[2], 0)
```

```python
import functools

import jax
import jax.numpy as jnp
from jax import lax
from jax.experimental import pallas as pl
from jax.experimental.pallas import tpu as pltpu
from jax.experimental.pallas import tpu_sc as plsc

F32 = jnp.float32
BF16 = jnp.bfloat16

D_MODEL = 1024
HEAD_DIM = 64
GRID_W = 64
EPS = 1e-6
ROPE_BASE = 10000.0
A_WIDTH = D_MODEL // 4
A_HEADS = A_WIDTH // HEAD_DIM
A_CHUNK = 128
B_WIDTH = D_MODEL // 2
B_HEADS = B_WIDTH // HEAD_DIM
B_KV_HEADS = 2
B_GROUP = B_HEADS // B_KV_HEADS
B_KV_WIDTH = B_KV_HEADS * HEAD_DIM
C_WIDTH = D_MODEL // 4
C_HEADS = C_WIDTH // HEAD_DIM
OFF_B = 2 * A_WIDTH
OFF_KV = OFF_B + B_WIDTH
OFF_V = OFF_KV + B_KV_WIDTH
OFF_C = OFF_KV + 2 * B_KV_WIDTH
OFF_G = OFF_C + 4 * C_WIDTH
IN_WIDTH = OFF_G + C_WIDTH
N_GROUPS = 4
EXPERTS_PER_GROUP = 8
N_EXPERTS = N_GROUPS * EXPERTS_PER_GROUP
TOP_K = 2
D_FF_EXPERT = D_MODEL // 2

HEAD_SHIFT = HEAD_DIM.bit_length() - 1
GROUP_SHIFT = EXPERTS_PER_GROUP.bit_length() - 1
ROPE_AXIS = HEAD_DIM // 2
ROPE_PAIR = ROPE_AXIS // 2
MOD_ROWS = 16
MOD_COLS = 1536
ROUTER_LANES = 128
PROJ_ROWS = 512
HGRN_BLOCK = 32
HGRN_GROUP = 16
LOG2E = 1.4426950408889634
ATTN_ROWS = 256
ATTN_KEYS = 512
SAFE_SHIFT = 60
SAFE_DECAY = 80.0
MOE_ROWS = 512
ROUTE_ROWS = 512
SC_WINDOW = 128
SC_ROW = 256
PLANES = D_MODEL // (2 * SC_ROW)
VMEM_LIMIT = 48 * 1024 * 1024


def _cparams(*sem):
    return pltpu.CompilerParams(dimension_semantics=sem, vmem_limit_bytes=VMEM_LIMIT)


def _head_ones(n, dtype):
    r = lax.broadcasted_iota(jnp.int32, (n, n), 0) >> HEAD_SHIFT
    c = lax.broadcasted_iota(jnp.int32, (n, n), 1) >> HEAD_SHIFT
    return (r == c).astype(dtype)


def _head_sum(x, ones_bd):
    return jnp.dot(x.astype(BF16), ones_bd, preferred_element_type=F32)


def _head_rms(x, ones_bd):
    w = ones_bd.shape[0]
    if x.shape[1] > w:
        return jnp.concatenate([_head_rms(x[:, c:c + w], ones_bd) for c in range(0, x.shape[1], w)], axis=1)
    return x * lax.rsqrt(_head_sum(x * x, ones_bd) * (1.0 / HEAD_DIM) + EPS)


def _pack_rows(y):
    bits = lax.bitcast_convert_type(y.astype(BF16).astype(F32), jnp.uint32)
    half = y.shape[1] // 2
    return lax.bitcast_convert_type(bits[:, :half] | (bits[:, half:] >> 16), F32)


def _unpack_rows(w):
    bits = lax.bitcast_convert_type(w, jnp.uint32)
    hi = lax.bitcast_convert_type(bits & jnp.uint32(0xFFFF0000), F32)
    lo = lax.bitcast_convert_type(bits << 16, F32)
    return hi, lo


def _pack_planes(y, ref, lead=()):
    for p in range(PLANES):
        ref[(p,) + lead] = _pack_rows(y[:, 2 * p * SC_ROW:(2 * p + 2) * SC_ROW])


def _mod_kernel(c_ref, w_ref, b_ref, o_ref):
    a = jax.nn.silu(c_ref[...])
    w = w_ref[0]
    a_hi, w_hi = a.astype(BF16), w.astype(BF16)
    a_lo, w_lo = (a - a_hi.astype(F32)).astype(BF16), (w - w_hi.astype(F32)).astype(BF16)
    o_ref[...] = (jnp.dot(a_hi, w_hi, preferred_element_type=F32) + jnp.dot(a_hi, w_lo, preferred_element_type=F32)
                  + jnp.dot(a_lo, w_hi, preferred_element_type=F32) + b_ref[...])


def _mod(cc, w_mod, b_mod, layer):
    n = w_mod.shape[2]
    tn = MOD_COLS
    return pl.pallas_call(
        _mod_kernel,
        out_shape=jax.ShapeDtypeStruct((MOD_ROWS, n), F32),
        grid=(n // tn,),
        in_specs=[pl.BlockSpec((MOD_ROWS, D_MODEL), lambda j: (0, 0)),
                  pl.BlockSpec((1, D_MODEL, tn), lambda j: (layer, 0, j)),
                  pl.BlockSpec((1, tn), lambda j: (0, j))],
        out_specs=pl.BlockSpec((MOD_ROWS, tn), lambda j: (0, j)),
        compiler_params=_cparams("arbitrary"),
        name="mod",
    )(cc, w_mod, b_mod.reshape(1, n))


def _rope(xn, c_ref, sp_ref, sm_ref):
    w = xn.shape[-1]
    rep = lambda ref: jnp.tile(ref[...], (1, w // ref.shape[-1]))
    return xn * rep(c_ref) + pltpu.roll(xn, ROPE_PAIR, 1) * rep(sp_ref) + pltpu.roll(xn, w - ROPE_PAIR, 1) * rep(sm_ref)


def _combined(x_ref, pk_ref, info_ref, g_ref):
    info = info_ref[...]
    g1 = info[:, 2:3]
    g2 = info[:, 3:4]
    parts = []
    for p in range(PLANES):
        hi1, lo1 = _unpack_rows(pk_ref[TOP_K * p])
        hi2, lo2 = _unpack_rows(pk_ref[TOP_K * p + 1])
        parts += [g1 * hi1 + g2 * hi2, g1 * lo1 + g2 * lo2]
    return x_ref[0] + g_ref[0] * jnp.concatenate(parts, axis=1)


def _gmlp(z, ws_ref, bias_ref):
    gz = jax.nn.gelu(z)
    u = gz[:, :A_WIDTH]
    vn = _head_rms(gz[:, A_WIDTH:], _head_ones(A_WIDTH, BF16))
    lane_head = lax.broadcasted_iota(jnp.int32, vn.shape, 1) >> HEAD_SHIFT
    acc = bias_ref[...]
    for hh in range(A_HEADS):
        vh = jnp.where(lane_head == hh, vn, 0.0).astype(BF16)
        acc = acc + jnp.dot(ws_ref[hh], vh, preferred_element_type=F32)
    return u * acc


def _inproj_kernel(*refs, pending):
    if pending:
        (x_ref, pk_ref, info_ref, g2_ref, mul_ref, add_ref, w_ref, qc_ref, qsp_ref, qsm_ref, kc_ref, ksp_ref, ksm_ref,
         ws_ref, bias_ref, xo_ref, ya_ref, q_ref, k_ref, v_ref, zc_ref, g_ref) = refs
        x = _combined(x_ref, pk_ref, info_ref, g2_ref)
        xo_ref[0] = x
    else:
        (x_ref, mul_ref, add_ref, w_ref, qc_ref, qsp_ref, qsm_ref, kc_ref, ksp_ref, ksm_ref,
         ws_ref, bias_ref, ya_ref, q_ref, k_ref, v_ref, zc_ref, g_ref) = refs
        x = x_ref[0]
    ms = jnp.mean(x * x, axis=-1, keepdims=True)
    h = x * lax.rsqrt(ms + EPS) * mul_ref[0] + add_ref[0]
    y = jnp.dot(h.astype(BF16), w_ref[...], preferred_element_type=F32)
    for c0 in range(0, x.shape[0], A_CHUNK):
        ya_ref[0, c0:c0 + A_CHUNK, :] = _gmlp(y[c0:c0 + A_CHUNK, :OFF_B], ws_ref, bias_ref).astype(BF16)
    qn = _head_rms(y[:, OFF_B:OFF_KV], _head_ones(B_WIDTH // 2, BF16))
    q_ref[0] = _rope(qn, qc_ref, qsp_ref, qsm_ref).astype(BF16)
    kn = _head_rms(y[:, OFF_KV:OFF_V], _head_ones(B_KV_WIDTH, BF16))
    k_ref[0] = _rope(kn, kc_ref, ksp_ref, ksm_ref).astype(BF16)
    v_ref[0] = y[:, OFF_V:OFF_C].astype(BF16)
    zc_ref[0] = y[:, OFF_C:OFF_G]
    g_ref[0] = y[:, OFF_G:]


def _inproj(x, mul, add, w_bf, q_tabs, k_tabs, ws_bf, bias2d, pending=None):
    b, t, d = x.shape
    tm = min(PROJ_ROWS, t)
    row = lambda i, bb: (bb, i, 0)
    vec = lambda i, bb: (bb, 0, 0)
    tab = lambda i, bb: (i, 0)
    widths = (A_WIDTH, B_WIDTH, B_KV_WIDTH, B_KV_WIDTH, 4 * C_WIDTH, C_WIDTH)
    dtypes = (BF16, BF16, BF16, BF16, F32, F32)
    pre_specs, pre_args = [], []
    if pending is not None:
        picked, info, gate, row0 = pending
        off = row0 // tm
        widths, dtypes = (d,) + widths, (F32,) + dtypes
        pre_specs = [pl.BlockSpec((PLANES * TOP_K, tm, SC_ROW), lambda i, bb: (0, off + bb * (t // tm) + i, 0)),
                     pl.BlockSpec((tm, ROUTER_LANES), lambda i, bb: (off + bb * (t // tm) + i, 0)),
                     pl.BlockSpec((1, 1, d), vec)]
        pre_args = [picked, info, gate]
    return pl.pallas_call(
        functools.partial(_inproj_kernel, pending=pending is not None),
        out_shape=[jax.ShapeDtypeStruct((b, t, w), dt) for w, dt in zip(widths, dtypes)],
        grid=(t // tm, b),
        in_specs=[pl.BlockSpec((1, tm, d), row)] + pre_specs
                 + [pl.BlockSpec((1, 1, d), vec),
                    pl.BlockSpec((1, 1, d), vec),
                    pl.BlockSpec((d, IN_WIDTH), lambda i, bb: (0, 0))]
                 + [pl.BlockSpec((tm, B_KV_WIDTH), tab)] * 6
                 + [pl.BlockSpec((A_HEADS, A_CHUNK, A_CHUNK), lambda i, bb: (0, 0, 0)),
                    pl.BlockSpec((A_CHUNK, A_WIDTH), lambda i, bb: (0, 0))],
        out_specs=[pl.BlockSpec((1, tm, w), row) for w in widths],
        compiler_params=_cparams("arbitrary", "arbitrary"),
        name="inproj",
    )(x, *pre_args, mul, add, w_bf, *q_tabs, *k_tabs, ws_bf, bias2d)


def _rope_tables(t, w, scale, width, rotate):
    ws = w.astype(F32) * scale
    if not rotate:
        c = jnp.broadcast_to(jnp.tile(ws, width // HEAD_DIM)[None, :], (t, width))
        z = jnp.zeros((t, width), F32)
        return c, z, z
    pos = jnp.arange(t)
    row = (pos // GRID_W).astype(F32)
    col = (pos % GRID_W).astype(F32)
    inv_freq = 1.0 / (ROPE_BASE ** (jnp.arange(0, HEAD_DIM // 2, 2, dtype=F32) / (HEAD_DIM // 2)))
    dd = jnp.arange(HEAD_DIM)
    axis = dd // ROPE_AXIS
    half = (dd % ROPE_AXIS) // ROPE_PAIR
    ang = jnp.where(axis[None, :] == 0, row[:, None], col[:, None]) * inv_freq[dd % ROPE_PAIR][None, :]
    cos, sin = jnp.cos(ang), jnp.sin(ang)
    c = cos * ws[None, :]
    sm = jnp.where(half[None, :] == 0, -sin * jnp.roll(ws, -ROPE_PAIR)[None, :], 0.0)
    sp = jnp.where(half[None, :] == 1, sin * jnp.roll(ws, ROPE_PAIR)[None, :], 0.0)
    rep = width // HEAD_DIM
    return jnp.tile(c, (1, rep)), jnp.tile(sp, (1, rep)), jnp.tile(sm, (1, rep))


def _attn_kernel(flag_ref, q_ref, *refs, n_seg):
    kv_refs, o_ref = refs[:2 * n_seg], refs[2 * n_seg]
    tq = q_ref.shape[1]
    dh = HEAD_DIM

    def heads(j):
        q4 = jnp.concatenate([q_ref[0, :, (B_GROUP * j + gg) * dh:(B_GROUP * j + gg + 1) * dh]
                              for gg in range(B_GROUP)], axis=0)
        ks, vs = [], []
        for sg in range(n_seg):
            s_len = kv_refs[2 * sg].shape[1]
            for c0 in range(0, s_len, ATTN_KEYS):
                c1 = min(c0 + ATTN_KEYS, s_len)
                ks.append(kv_refs[2 * sg][0, c0:c1, j * dh:(j + 1) * dh])
                vs.append(kv_refs[2 * sg + 1][0, c0:c1, j * dh:(j + 1) * dh])
        return q4, ks, vs

    def scores(q4, ks):
        return [lax.dot_general(kk, q4, (((1,), (1,)), ((), ())), preferred_element_type=F32) for kk in ks]

    def finish(j, ps, vs):
        l = functools.reduce(jnp.add, [jnp.sum(p, axis=0, keepdims=True) for p in ps])
        acc = sum(lax.dot_general(vv, p.astype(BF16), (((0,), (0,)), ((), ())), preferred_element_type=F32)
                  for p, vv in zip(ps, vs))
        o = (acc / l).T
        for gg in range(B_GROUP):
            hh = B_GROUP * j + gg
            o_ref[0, :, hh * dh:(hh + 1) * dh] = o[gg * tq:(gg + 1) * tq].astype(BF16)

    @pl.when(flag_ref[0] > 0)
    def _():
        shift = flag_ref[1].astype(F32)
        for j in range(B_KV_HEADS):
            q4, ks, vs = heads(j)
            finish(j, [jnp.exp2(s - shift) for s in scores(q4, ks)], vs)

    @pl.when(flag_ref[0] <= 0)
    def _():
        for j in range(B_KV_HEADS):
            q4, ks, vs = heads(j)
            ss = scores(q4, ks)
            m = functools.reduce(jnp.maximum, [jnp.max(s, axis=0, keepdims=True) for s in ss])
            finish(j, [jnp.exp2(s - m) for s in ss], vs)


def _attn(flag, q, kv_segs):
    b, t, w = q.shape
    tq = min(ATTN_ROWS, t)
    n_seg = len(kv_segs)
    kv_flat, kv_specs = [], []
    for kk, vv in kv_segs:
        s_len, kw = kk.shape[1:]
        kv_flat += [kk, vv]
        kv_specs += [pl.BlockSpec((1, s_len, kw), lambda bb, i, fl: (bb, 0, 0))] * 2
    return pl.pallas_call(
        functools.partial(_attn_kernel, n_seg=n_seg),
        out_shape=jax.ShapeDtypeStruct((b, t, w), BF16),
        grid_spec=pltpu.PrefetchScalarGridSpec(
            num_scalar_prefetch=1,
            grid=(b, t // tq),
            in_specs=[pl.BlockSpec((1, tq, w), lambda bb, i, fl: (bb, i, 0))] + kv_specs,
            out_specs=pl.BlockSpec((1, tq, w), lambda bb, i, fl: (bb, i, 0))),
        compiler_params=_cparams("arbitrary", "arbitrary"),
        name="attn",
    )(flag, q, *kv_flat)


def _scan_rows(x, reverse):
    n = x.shape[0]
    rows = lax.broadcasted_iota(jnp.int32, x.shape, 0)
    sh = 1
    while sh < n:
        if reverse:
            x = x + jnp.where(rows < n - sh, pltpu.roll(x, n - sh, 0), 0.0)
        else:
            x = x + jnp.where(rows >= sh, pltpu.roll(x, sh, 0), 0.0)
        sh *= 2
    return x


def _stack_heads(x, lane_head):
    return jnp.concatenate([jnp.where(lane_head == hh, x, 0.0) for hh in range(C_HEADS)], axis=0)


def _hgrn_kernel(z_ref, zc_ref, lbc_ref, *refs, ctx_out):
    if ctx_out:
        o_ref, oc_ref, st_ref, kx_ref, bx_ref, vx_ref, flag_ref = refs
    else:
        o_ref, st_ref, kx_ref, bx_ref, vx_ref, flag_ref = refs
        oc_ref = None
    n = C_WIDTH
    nb = HGRN_BLOCK
    nblk_c = zc_ref.shape[1] // nb
    nblk_l = z_ref.shape[1] // nb
    ones_bd = _head_ones(n, BF16)
    rows = lax.broadcasted_iota(jnp.int32, (nb, n), 0)
    lane_head = lax.broadcasted_iota(jnp.int32, (nb, n), 1) >> HEAD_SHIFT
    lane_head64 = lax.broadcasted_iota(jnp.int32, (HEAD_DIM, n), 1) >> HEAD_SHIFT
    low_half = (lax.broadcasted_iota(jnp.int32, (HEAD_DIM, 2 * HEAD_DIM), 1) < HEAD_DIM)
    sc_t = lax.broadcasted_iota(jnp.int32, (nb, C_HEADS * nb), 0)
    sc_s = lax.broadcasted_iota(jnp.int32, (nb, C_HEADS * nb), 1) & (nb - 1)

    def gates(z, d):
        one_m_lb = lbc_ref[d, 0:1, :]
        log1m_lb = lbc_ref[d, 1:2, :]
        log_lb = lbc_ref[d, 2:3, :]
        lb_pos = lbc_ref[d, 3:4, :] > 0.5
        soft = jnp.log(1.0 + jnp.exp(-jnp.abs(z)))
        log_rest = log1m_lb + (jnp.minimum(z, 0.0) - soft)
        lse = jnp.maximum(log_lb, log_rest) + jnp.log(1.0 + jnp.exp(-jnp.abs(log_lb - log_rest)))
        return jnp.where(lb_pos, lse, log_rest), one_m_lb * jnp.exp(jnp.minimum(-z, 0.0) - soft)

    def group_blocks(nblk, i):
        grp = HGRN_GROUP
        while nblk % grp:
            grp //= 2
        fwd = [i * grp + gg for gg in range(grp)]
        return grp, fwd, [nblk - 1 - blk for blk in fwd]

    def block_bound(src_ref, blk, d):
        zz = src_ref[0, pl.ds(pl.multiple_of(blk * nb, nb), nb), (1 + d) * n:(2 + d) * n]
        step_bound = jnp.minimum(lbc_ref[d, 4:5, :],
                                 jnp.maximum(-zz, 0.0) + (jnp.log(2.0) - lbc_ref[d, 1:2, :]))
        return jnp.sum(step_bound, axis=0, keepdims=True)

    def flag_groups(src_ref, nblk, base):
        def body(i, carry):
            _, fwd, bwd = group_blocks(nblk, i)
            worst = functools.reduce(jnp.maximum, [block_bound(src_ref, blk, d)
                                                   for d, blks in ((0, fwd), (1, bwd)) for blk in blks])
            flag_ref[base + i] = (jnp.max(worst) <= SAFE_DECAY).astype(jnp.int32)
            return carry
        lax.fori_loop(0, nblk // group_blocks(nblk, 0)[0], body, 0)

    def worst_bound(src_ref, nblk):
        def body(blk, worst):
            return jnp.maximum(worst, jnp.maximum(block_bound(src_ref, blk, 0), block_bound(src_ref, blk, 1)))
        return lax.fori_loop(0, nblk, body, jnp.zeros((1, n), F32))

    n_grp_c = nblk_c // group_blocks(nblk_c, 0)[0]
    all_safe = jnp.max(jnp.maximum(worst_bound(zc_ref, nblk_c), worst_bound(z_ref, nblk_l))) <= SAFE_DECAY
    st_ref[...] = jnp.zeros_like(st_ref)
    o_ref[...] = jnp.zeros_like(o_ref)
    if ctx_out:
        oc_ref[...] = jnp.zeros_like(oc_ref)

    def step(src_ref, dst_ref, blk, d, fast):
        reverse = d == 1
        r0 = pl.multiple_of(blk * nb, nb)
        v = src_ref[0, pl.ds(r0, nb), 3 * n:4 * n]
        log_f, k = gates(src_ref[0, pl.ds(r0, nb), (1 + d) * n:(2 + d) * n], d)
        bc = _scan_rows(log_f, reverse)
        edge = 0 if reverse else nb - 1
        b_edge = bc[edge:edge + 1, :]
        st = st_ref[d]
        v_bf = v.astype(BF16)

        if dst_ref is not None:
            q = jax.nn.silu(src_ref[0, pl.ds(r0, nb), 0:n])
            qt = (q * jnp.exp(bc)).astype(BF16)
            o = lax.dot_general(qt, _stack_heads(st, lane_head64).astype(BF16), (((1,), (1,)), ((), ())),
                                preferred_element_type=F32)

            def intra_fast():
                kt = _stack_heads(k * jnp.exp(-bc), lane_head).astype(BF16)
                sc = lax.dot_general(qt, kt, (((1,), (1,)), ((), ())), preferred_element_type=F32)
                keep = (sc_s >= sc_t) if reverse else (sc_s <= sc_t)
                sc = jnp.where(keep, sc, 0.0).astype(BF16)
                return jnp.dot(sc, _stack_heads(v, lane_head).astype(BF16), preferred_element_type=F32)

            def intra_exact():
                kx_ref[d] = k
                bx_ref[d] = bc
                vx_ref[d] = v

                def sbody(s, acc):
                    keep = (rows <= s) if reverse else (rows >= s)
                    e = jnp.exp(jnp.where(keep, bc - bx_ref[d, pl.ds(s, 1), :], 0.0))
                    p = jnp.where(keep, q * e * kx_ref[d, pl.ds(s, 1), :], 0.0)
                    sc = jnp.dot(p.astype(BF16), ones_bd, preferred_element_type=F32)
                    return acc + sc * vx_ref[d, pl.ds(s, 1), :]

                return lax.fori_loop(0, nb, sbody, jnp.zeros((nb, n), F32))

            o = o + (intra_fast() if fast else intra_exact())
            dst_ref[0, pl.ds(r0, nb), :] += o

        kd = (k * jnp.exp(b_edge - bc)).astype(BF16)
        full = lax.dot_general(v_bf, kd, (((0,), (0,)), ((), ())), preferred_element_type=F32)
        upd = jnp.concatenate(
            [jnp.where(low_half,
                       full[(2 * c) * HEAD_DIM:(2 * c + 1) * HEAD_DIM, 2 * c * HEAD_DIM:(2 * c + 2) * HEAD_DIM],
                       full[(2 * c + 1) * HEAD_DIM:(2 * c + 2) * HEAD_DIM, 2 * c * HEAD_DIM:(2 * c + 2) * HEAD_DIM])
             for c in range(C_HEADS // 2)], axis=1)
        st_ref[d] = st * jnp.exp(b_edge) + upd

    def run(src_ref, dst_ref, nblk, base, per_group):
        def body(i, carry):
            _, fwd, bwd = group_blocks(nblk, i)

            def group(fast):
                for bf, bb in zip(fwd, bwd):
                    step(src_ref, dst_ref, bf, 0, fast)
                    step(src_ref, dst_ref, bb, 1, fast)

            if per_group:
                safe = flag_ref[base + i]
                pl.when(safe > 0)(functools.partial(group, True))
                pl.when(safe <= 0)(functools.partial(group, False))
            else:
                group(True)
            return carry
        lax.fori_loop(0, nblk // group_blocks(nblk, 0)[0], body, 0)

    @pl.when(all_safe)
    def _():
        run(zc_ref, oc_ref, nblk_c, 0, False)
        run(z_ref, o_ref, nblk_l, n_grp_c, False)

    @pl.when(jnp.logical_not(all_safe))
    def _():
        flag_groups(zc_ref, nblk_c, 0)
        flag_groups(z_ref, nblk_l, n_grp_c)
        run(zc_ref, oc_ref, nblk_c, 0, True)
        run(z_ref, o_ref, nblk_l, n_grp_c, True)


def _hgrn(zc, zc_c, lbc, ctx_out):
    b, t, w = zc.shape
    lc = zc_c.shape[1]
    n = C_WIDTH
    row = lambda bb: (bb, 0, 0)
    out_shape = [jax.ShapeDtypeStruct((b, t, n), F32)]
    out_specs = [pl.BlockSpec((1, t, n), row)]
    if ctx_out:
        out_shape.append(jax.ShapeDtypeStruct((b, lc, n), F32))
        out_specs.append(pl.BlockSpec((1, lc, n), row))
    res = pl.pallas_call(
        functools.partial(_hgrn_kernel, ctx_out=ctx_out),
        out_shape=out_shape,
        grid=(b,),
        in_specs=[pl.BlockSpec((1, t, w), row),
                  pl.BlockSpec((1, lc, w), row),
                  pl.BlockSpec((2, 8, n), lambda bb: (0, 0, 0))],
        out_specs=out_specs,
        scratch_shapes=[pltpu.VMEM((2, HEAD_DIM, n), F32)]
                       + [pltpu.VMEM((2, HGRN_BLOCK, n), F32)] * 3
                       + [pltpu.SMEM(((t + lc) // HGRN_BLOCK,), jnp.int32)],
        compiler_params=_cparams("arbitrary"),
        name="hgrn",
    )(zc, zc_c, lbc)
    return (res[0], res[1]) if ctx_out else (res[0], None)


def _select_experts(lg):
    lane = lax.broadcasted_iota(jnp.int32, lg.shape, 1)
    lane_f = lane.astype(F32)
    neg = -jnp.inf
    gl = jnp.where(lane < N_GROUPS, lg, neg)
    gmax = jnp.max(gl, axis=1, keepdims=True)
    grp = jnp.min(jnp.where(gl == gmax, lane_f, float(ROUTER_LANES)), axis=1, keepdims=True).astype(jnp.int32)
    p_grp = 1.0 / jnp.sum(jnp.exp(gl - gmax), axis=1, keepdims=True)
    in_grp = (lane >= N_GROUPS) & (lane < N_GROUPS + N_EXPERTS) & (((lane - N_GROUPS) >> GROUP_SHIFT) == grp)
    el = jnp.where(in_grp, lg, neg)
    v1 = jnp.max(el, axis=1, keepdims=True)
    i1 = jnp.min(jnp.where(el == v1, lane_f, float(ROUTER_LANES)), axis=1, keepdims=True)
    el2 = jnp.where(lane_f == i1, neg, el)
    v2 = jnp.max(el2, axis=1, keepdims=True)
    i2 = jnp.min(jnp.where(el2 == v2, lane_f, float(ROUTER_LANES)), axis=1, keepdims=True)
    rr = jnp.exp(v2 - v1)
    g1 = p_grp / (1.0 + rr)
    g2 = p_grp * rr / (1.0 + rr)
    sel = jnp.where(lane == 0, i1, jnp.where(lane == 1, i2, jnp.where(lane == 2, g1, jnp.where(lane == 3, g2, 0.0))))
    counts = jnp.sum(((lane_f == i1) | (lane_f == i2)).astype(F32), axis=0, keepdims=True)
    return sel, counts


def _outproj_kernel(x_ref, ya_ref, yb_ref, o_ref, g_ref, w_ref, gate_ref, mul_ref, add_ref, hw_ref,
                    wr_ref, br_ref, xo_ref, h2_ref, sel_ref, cnt_ref):
    yc = _head_rms(o_ref[0], _head_ones(C_WIDTH, BF16)) * hw_ref[...] * jax.nn.silu(g_ref[0])
    y = jnp.dot(ya_ref[0], w_ref[0:A_WIDTH, :], preferred_element_type=F32)
    y = y + jnp.dot(yb_ref[0], w_ref[A_WIDTH:A_WIDTH + B_WIDTH, :], preferred_element_type=F32)
    y = y + jnp.dot(yc.astype(BF16), w_ref[A_WIDTH + B_WIDTH:, :], preferred_element_type=F32)
    xn = x_ref[0] + gate_ref[0] * y
    xo_ref[0] = xn
    ms = jnp.mean(xn * xn, axis=-1, keepdims=True)
    h2 = xn * lax.rsqrt(ms + EPS) * mul_ref[0] + add_ref[0]
    _pack_planes(h2, h2_ref, (0,))
    h_hi = h2.astype(BF16)
    h_lo = (h2 - h_hi.astype(F32)).astype(BF16)
    both = jnp.dot(h_hi, wr_ref[...], preferred_element_type=F32)
    lg = (both[:, :ROUTER_LANES] + both[:, ROUTER_LANES:] + br_ref[...]
          + jnp.dot(h_lo, wr_ref[:, 0:ROUTER_LANES], preferred_element_type=F32))
    sel, counts = _select_experts(lg)
    sel_ref[0] = sel

    @pl.when((pl.program_id(0) == 0) & (pl.program_id(1) == 0))
    def _():
        cnt_ref[...] = jnp.zeros_like(cnt_ref)
    cnt_ref[...] += counts


def _outproj(x, ya, yb, o, g, w_bf, gate, mul, add, hw, wr, br):
    b, t, d = x.shape
    tm = min(PROJ_ROWS, t)
    row = lambda bb, i: (bb, i, 0)
    vec = lambda bb, i: (bb, 0, 0)
    const = lambda bb, i: (0, 0)
    return pl.pallas_call(
        _outproj_kernel,
        out_shape=[jax.ShapeDtypeStruct((b, t, d), F32),
                   jax.ShapeDtypeStruct((PLANES, b, t, SC_ROW), F32),
                   jax.ShapeDtypeStruct((b, t, ROUTER_LANES), F32),
                   jax.ShapeDtypeStruct((8, ROUTER_LANES), F32)],
        grid=(b, t // tm),
        in_specs=[pl.BlockSpec((1, tm, d), row),
                  pl.BlockSpec((1, tm, A_WIDTH), row),
                  pl.BlockSpec((1, tm, B_WIDTH), row),
                  pl.BlockSpec((1, tm, C_WIDTH), row),
                  pl.BlockSpec((1, tm, C_WIDTH), row),
                  pl.BlockSpec((d, d), const),
                  pl.BlockSpec((1, 1, d), vec),
                  pl.BlockSpec((1, 1, d), vec),
                  pl.BlockSpec((1, 1, d), vec),
                  pl.BlockSpec((1, C_WIDTH), const),
                  pl.BlockSpec((d, 2 * ROUTER_LANES), const),
                  pl.BlockSpec((1, ROUTER_LANES), const)],
        out_specs=[pl.BlockSpec((1, tm, d), row),
                   pl.BlockSpec((PLANES, 1, tm, SC_ROW), lambda bb, i: (0, bb, i, 0)),
                   pl.BlockSpec((1, tm, ROUTER_LANES), row),
                   pl.BlockSpec((8, ROUTER_LANES), const)],
        compiler_params=_cparams("arbitrary", "arbitrary"),
        name="outproj",
    )(x, ya, yb, o, g, w_bf, gate, mul, add, hw, wr, br)


def _route_kernel(sel_ref, cnt_ref, info_ref, meta_ref, base_ref):
    i = pl.program_id(0)
    tm = sel_ref.shape[0]
    lane = lax.broadcasted_iota(jnp.int32, (tm, ROUTER_LANES), 1)
    lane_f = lane.astype(F32)
    sel = sel_ref[...]
    hit1 = lane_f == sel[:, 0:1]
    hit2 = lane_f == sel[:, 1:2]
    onehot = (hit1 | hit2).astype(F32)

    @pl.when(i == 0)
    def _():
        counts = cnt_ref[...]
        padded = jnp.floor((counts + (MOE_ROWS - 1.0)) * (1.0 / MOE_ROWS)) * MOE_ROWS
        r = lax.broadcasted_iota(jnp.int32, (ROUTER_LANES, ROUTER_LANES), 0)
        c = lax.broadcasted_iota(jnp.int32, (ROUTER_LANES, ROUTER_LANES), 1)
        ends = jnp.dot(padded, (r <= c).astype(F32), preferred_element_type=F32,
                       precision=lax.Precision.HIGHEST)
        base_ref[...] = (ends - padded)[0:1]
        row = lax.broadcasted_iota(jnp.int32, (8, ROUTER_LANES), 0)
        meta_ref[...] = jnp.where(row == 0, counts, jnp.where(row == 1, ends - padded, ends))

    tr = lax.broadcasted_iota(jnp.int32, (tm, tm), 0)
    tc = lax.broadcasted_iota(jnp.int32, (tm, tm), 1)
    before = jnp.dot((tc < tr).astype(BF16), onehot.astype(BF16), preferred_element_type=F32)
    pos = base_ref[...] + before
    d1 = jnp.sum(jnp.where(hit1, pos, 0.0), axis=1, keepdims=True)
    d2 = jnp.sum(jnp.where(hit2, pos, 0.0), axis=1, keepdims=True)
    base_ref[...] += jnp.sum(onehot, axis=0, keepdims=True)
    info_ref[...] = jnp.where(lane == 0, d1, jnp.where(lane == 1, d2, sel))


def _route(sel, counts):
    n = sel.shape[0]
    tm = ROUTE_ROWS if n % ROUTE_ROWS == 0 else ROUTE_ROWS // 2
    return pl.pallas_call(
        _route_kernel,
        out_shape=[jax.ShapeDtypeStruct((n, ROUTER_LANES), F32),
                   jax.ShapeDtypeStruct((8, ROUTER_LANES), F32)],
        grid=(n // tm,),
        in_specs=[pl.BlockSpec((tm, ROUTER_LANES), lambda i: (i, 0)),
                  pl.BlockSpec((8, ROUTER_LANES), lambda i: (0, 0))],
        out_specs=[pl.BlockSpec((tm, ROUTER_LANES), lambda i: (i, 0)),
                   pl.BlockSpec((8, ROUTER_LANES), lambda i: (0, 0))],
        scratch_shapes=[pltpu.VMEM((1, ROUTER_LANES), F32)],
        compiler_params=_cparams("arbitrary"),
        name="route",
    )(sel, counts)


def _sc_mesh():
    return plsc.VectorSubcoreMesh(core_axis_name="c", subcore_axis_name="s")


def _sc_gather(table, idx):
    n = idx.shape[0]
    d = table.shape[1]

    @functools.partial(pl.kernel, out_type=jax.ShapeDtypeStruct((n, d), table.dtype), mesh=_sc_mesh())
    def gather(x_hbm, i_hbm, o_hbm):
        def body(i_vmem, o_vmem):
            pltpu.sync_copy(x_hbm.at[i_vmem.at[0]], o_vmem)

        pltpu.emit_pipeline(
            body,
            grid=(n // SC_WINDOW,),
            in_specs=[pl.BlockSpec((1, SC_WINDOW), lambda i: (0, i))],
            out_specs=[pl.BlockSpec((SC_WINDOW, d), lambda i: (i, 0))],
            core_axis_name=("c", "s"),
            dimension_semantics=(pltpu.PARALLEL,),
        )(i_hbm, o_hbm)

    return gather(table, idx.reshape(1, n))


def _sc_scatter2(rows, idx0, idx1, n_out):
    m, d = rows.shape

    @functools.partial(pl.kernel, out_type=jax.ShapeDtypeStruct((n_out, d), rows.dtype), mesh=_sc_mesh())
    def scatter(x_hbm, i0_hbm, i1_hbm, o_hbm):
        def body(x_vmem, i0_vmem, i1_vmem):
            pltpu.sync_copy(x_vmem, o_hbm.at[i0_vmem.at[0]])
            pltpu.sync_copy(x_vmem, o_hbm.at[i1_vmem.at[0]])

        pltpu.emit_pipeline(
            body,
            grid=(m // SC_WINDOW,),
            in_specs=[pl.BlockSpec((SC_WINDOW, d), lambda i: (i, 0)),
                      pl.BlockSpec((1, SC_WINDOW), lambda i: (0, i)),
                      pl.BlockSpec((1, SC_WINDOW), lambda i: (0, i))],
            out_specs=[],
            core_axis_name=("c", "s"),
            dimension_semantics=(pltpu.PARALLEL,),
        )(x_hbm, i0_hbm, i1_hbm)

    return scatter(rows, idx0.reshape(1, m), idx1.reshape(1, m))


def _moe_kernel(be_ref, nu_ref, first_ref, slot_ref, nxt_ref, x_ref, w1_hbm, w3_hbm, w2_hbm, o_ref,
                w1f, w3f, w2f, w1b, w3b, w2b, sem, *, layer):
    i = pl.program_id(0)

    def fetch(e, slot):
        return [pltpu.make_async_copy(src.at[layer, e], dst.at[slot], sem.at[n, slot])
                for n, (src, dst) in enumerate(((w1_hbm, w1f), (w3_hbm, w3f), (w2_hbm, w2f)))]

    @pl.when((i == 0) & (nu_ref[0] > 0))
    def _():
        for cp in fetch(be_ref[0], 0):
            cp.start()

    @pl.when((first_ref[i] > 0) & (i < nu_ref[0]))
    def _():
        slot = slot_ref[i]
        for cp in fetch(be_ref[i], slot):
            cp.wait()
        w1b[...] = w1f[slot].astype(BF16)
        w3b[...] = w3f[slot].astype(BF16)
        w2b[...] = w2f[slot].astype(BF16)

        @pl.when(nxt_ref[i] >= 0)
        def _():
            for cp in fetch(nxt_ref[i], 1 - slot):
                cp.start()

    @pl.when(i < nu_ref[0])
    def _():
        parts = [h.astype(BF16) for p in range(PLANES) for h in _unpack_rows(x_ref[p])]
        a = sum(jnp.dot(h, w1b[q * SC_ROW:(q + 1) * SC_ROW, :], preferred_element_type=F32)
                for q, h in enumerate(parts))
        b = sum(jnp.dot(h, w3b[q * SC_ROW:(q + 1) * SC_ROW, :], preferred_element_type=F32)
                for q, h in enumerate(parts))
        hmid = (jax.nn.silu(a) * b).astype(BF16)
        _pack_planes(jnp.dot(hmid, w2b[...], preferred_element_type=F32), o_ref)

    @pl.when(i >= nu_ref[0])
    def _():
        o_ref[...] = jnp.zeros_like(o_ref)


def _moe_mlp(blk_expert, n_used, xs, w1, w3, w2, layer):
    n_rows = xs.shape[1]
    d, f = w1.shape[2:]
    nblk = n_rows // MOE_ROWS
    rows = lambda i, *_: (0, i, 0)
    idx = jnp.arange(nblk, dtype=jnp.int32)
    first = (idx < n_used[0]) & ((idx == 0) | (blk_expert != jnp.roll(blk_expert, 1)))
    slot = (jnp.cumsum(first.astype(jnp.int32)) - 1) % 2
    nxt_first = lax.cummin(jnp.where(first, idx, nblk)[::-1])[::-1]
    nxt_first = jnp.concatenate([nxt_first[1:], jnp.full((1,), nblk, jnp.int32)])
    nxt = jnp.where(nxt_first < nblk, blk_expert[jnp.minimum(nxt_first, nblk - 1)], -1)
    hbm = pl.BlockSpec(memory_space=pl.ANY)
    return pl.pallas_call(
        functools.partial(_moe_kernel, layer=layer),
        out_shape=jax.ShapeDtypeStruct((PLANES, n_rows, SC_ROW), F32),
        grid_spec=pltpu.PrefetchScalarGridSpec(
            num_scalar_prefetch=5,
            grid=(nblk,),
            in_specs=[pl.BlockSpec((PLANES, MOE_ROWS, SC_ROW), rows), hbm, hbm, hbm],
            out_specs=pl.BlockSpec((PLANES, MOE_ROWS, SC_ROW), rows),
            scratch_shapes=[pltpu.VMEM((2, d, f), F32), pltpu.VMEM((2, d, f), F32), pltpu.VMEM((2, f, d), F32),
                            pltpu.VMEM((d, f), BF16), pltpu.VMEM((d, f), BF16), pltpu.VMEM((f, d), BF16),
                            pltpu.SemaphoreType.DMA((3, 2))]),
        compiler_params=_cparams("arbitrary"),
        name="moe",
    )(blk_expert, n_used, first.astype(jnp.int32), slot.astype(jnp.int32), nxt.astype(jnp.int32), xs, w1, w3, w2)


def _combine_kernel(x_ref, pk_ref, info_ref, g_ref, o_ref):
    o_ref[0] = _combined(x_ref, pk_ref, info_ref, g_ref)


def _combine(x, picked, info, gate, row0):
    b, t, d = x.shape
    tm = min(256, t)
    off = row0 // tm
    tok = lambda bb, i: (off + bb * (t // tm) + i, 0)
    tok3 = lambda bb, i: (0, off + bb * (t // tm) + i, 0)
    return pl.pallas_call(
        _combine_kernel,
        out_shape=jax.ShapeDtypeStruct((b, t, d), F32),
        grid=(b, t // tm),
        in_specs=[pl.BlockSpec((1, tm, d), lambda bb, i: (bb, i, 0)),
                  pl.BlockSpec((PLANES * TOP_K, tm, SC_ROW), tok3),
                  pl.BlockSpec((tm, ROUTER_LANES), tok),
                  pl.BlockSpec((1, 1, d), lambda bb, i: (bb, 0, 0))],
        out_specs=pl.BlockSpec((1, tm, d), lambda bb, i: (bb, i, 0)),
        compiler_params=_cparams("arbitrary", "arbitrary"),
        name="combine",
    )(x, picked, info, gate)


def _hier_moe(h2p, sel, counts, w1, w3, w2, layer):
    n_tok = h2p.shape[1]
    info, meta = _route(sel, counts)
    dest = info[:, 0:TOP_K].astype(jnp.int32)
    pad_ends = meta[2, N_GROUPS:N_GROUPS + N_EXPERTS].astype(jnp.int32)
    nblk = -(-(n_tok * TOP_K) // MOE_ROWS) + N_EXPERTS
    n_rows = nblk * MOE_ROWS
    blk_start = jnp.arange(nblk, dtype=jnp.int32) * MOE_ROWS
    blk_expert = jnp.minimum(jnp.sum((pad_ends[None, :] <= blk_start[:, None]).astype(jnp.int32), axis=1),
                             N_EXPERTS - 1)
    n_used = pad_ends[-1:] // MOE_ROWS
    slot = [jnp.concatenate([p * n_rows + dest[:, s] for p in range(PLANES)]) for s in range(TOP_K)]
    xs = _sc_scatter2(h2p.reshape(PLANES * n_tok, SC_ROW), slot[0], slot[1], PLANES * n_rows)
    out = _moe_mlp(blk_expert, n_used, xs.reshape(PLANES, n_rows, SC_ROW), w1, w3, w2, layer)
    idx_all = jnp.concatenate([p * n_rows + dest[:, s] for p in range(PLANES) for s in range(TOP_K)])
    picked = _sc_gather(out.reshape(PLANES * n_rows, SC_ROW), idx_all)
    return picked.reshape(PLANES * TOP_K, n_tok, SC_ROW), info


def _layer(layer, x, xc, pend, c, c_ctx, lb, w_mod, b_mod, norm1_w, w_in, w_s, b_s, q_norm_w, k_norm_w, hgrn_norm_w, w_out,
           norm2_w, w_grp, b_grp, w_exp, b_exp, w1, w3, w2, ctx_out):
    b, t, d = x.shape
    lc = xc.shape[1]
    cc = jnp.zeros((MOD_ROWS, d), F32).at[:b].set(c).at[b].set(c_ctx)
    mod = _mod(cc, w_mod, b_mod, layer)
    sh1, sc1, g1, sh2, sc2, g2 = [m[:, None, :] for m in jnp.split(mod[:b], 6, axis=-1)]
    mod_c = [jnp.broadcast_to(m[None, None, :], (b, 1, d)) for m in jnp.split(mod[b], 6)]

    w_in_bf = w_in.astype(BF16)
    scale = LOG2E * HEAD_DIM ** -0.5
    q_tabs = _rope_tables(t, q_norm_w, scale, B_KV_WIDTH, True)
    k_tabs = _rope_tables(t, k_norm_w, 1.0, B_KV_WIDTH, True)
    qc_tabs = _rope_tables(lc, q_norm_w, scale, B_KV_WIDTH, False)
    kc_tabs = _rope_tables(lc, k_norm_w, 1.0, B_KV_WIDTH, False)
    ws_bf = w_s.astype(BF16)
    bias2d = jnp.repeat(b_s.T, HEAD_DIM, axis=1)
    res = _inproj(x, norm1_w * (1.0 + sc1), sh1, w_in_bf, q_tabs, k_tabs, ws_bf, bias2d,
                  None if pend is None else pend[:2] + (pend[2], 0))
    res_c = _inproj(xc, norm1_w * (1.0 + mod_c[1]), mod_c[0], w_in_bf, qc_tabs, kc_tabs, ws_bf, bias2d,
                    None if pend is None else pend[:2] + (pend[3], b * t))
    if pend is not None:
        x, xc, res, res_c = res[0], res_c[0], res[1:], res_c[1:]
    ya, q, k, v, zc, g = res
    ya_c, q_c, k_c, v_c, zc_c, g_c = res_c

    bound = LOG2E * HEAD_DIM ** 0.5 * jnp.max(jnp.abs(q_norm_w)) * jnp.max(jnp.abs(k_norm_w)) * 1.02
    shift = jnp.ceil(bound)
    attn_flag = jnp.stack([(shift <= SAFE_SHIFT).astype(jnp.int32), shift.astype(jnp.int32)])
    yb = _attn(attn_flag, q, [(k, v), (k_c, v_c)])

    pos = lb > 0.0
    log_lb = jnp.log(jnp.where(pos, lb, 1.0))
    lbc = jnp.stack([1.0 - lb, jnp.log1p(-lb), log_lb, pos.astype(F32), jnp.where(pos, -log_lb, 1e30)], axis=1)
    lbc = jnp.concatenate([lbc, jnp.zeros((2, 3, C_WIDTH), F32)], axis=1)
    o, o_c = _hgrn(zc, zc_c, lbc, ctx_out)

    w_out_bf = w_out.astype(BF16)
    hw = jnp.tile(hgrn_norm_w, C_HEADS)[None, :]
    wr = jnp.zeros((d, ROUTER_LANES), F32).at[:, :N_GROUPS].set(w_grp).at[
        :, N_GROUPS:N_GROUPS + N_EXPERTS].set(w_exp)
    wr_hi = wr.astype(BF16)
    wr = jnp.concatenate([wr_hi, (wr - wr_hi.astype(F32)).astype(BF16)], axis=1)
    br = jnp.zeros((1, ROUTER_LANES), F32).at[0, :N_GROUPS].set(b_grp).at[
        0, N_GROUPS:N_GROUPS + N_EXPERTS].set(b_exp)
    x, h2, sel, cnt = _outproj(x, ya, yb, o, g, w_out_bf, g1, norm2_w * (1.0 + sc2), sh2, hw, wr, br)
    if ctx_out:
        yb_c = _attn(attn_flag, q_c, [(k_c, v_c)])
        xc, h2c, sel_c, cnt_c = _outproj(xc, ya_c, yb_c, o_c, g_c, w_out_bf, mod_c[2],
                                norm2_w * (1.0 + mod_c[4]), mod_c[3], hw, wr, br)
        tokens = jnp.concatenate([h2.reshape(PLANES, -1, SC_ROW), h2c.reshape(PLANES, -1, SC_ROW)], axis=1)
        sel_all = jnp.concatenate([sel.reshape(-1, ROUTER_LANES), sel_c.reshape(-1, ROUTER_LANES)], axis=0)
        picked, info = _hier_moe(tokens, sel_all, cnt + cnt_c, w1, w3, w2, layer)
        return x, xc, (picked, info, g2, mod_c[5])
    picked, info = _hier_moe(h2.reshape(PLANES, -1, SC_ROW), sel.reshape(-1, ROUTER_LANES), cnt, w1, w3, w2, layer)
    return x, xc, (picked, info, g2, None)


def kernel(x, c, ctx, c_ctx, w_mod, b_mod, norm1_w, w_in, w_s, b_s, q_norm_w, k_norm_w, hgrn_lb_logits,
           hgrn_norm_w, w_out, norm2_w, w_grp, b_grp, w_exp, b_exp, w1, w3, w2):
    depth = w_mod.shape[0]
    lb_sm = jax.nn.softmax(hgrn_lb_logits.astype(F32), axis=0)
    lb = jnp.cumsum(lb_sm, axis=0) - lb_sm[0]
    xc = ctx
    pend = None
    for l in range(depth):
        x, xc, pend = _layer(l, x, xc, pend, c, c_ctx, lb[l], w_mod, b_mod[l], norm1_w[l], w_in[l], w_s[l], b_s[l],
                             q_norm_w[l], k_norm_w[l], hgrn_norm_w[l], w_out[l], norm2_w[l], w_grp[l], b_grp[l],
                             w_exp[l], b_exp[l], w1, w3, w2, ctx_out=(l < depth - 1))
    return _combine(x, pend[0], pend[1], pend[2], 0)
```

```python
import functools

import jax
import jax.numpy as jnp
from jax import lax
from jax.experimental import pallas as pl
from jax.experimental.pallas import tpu as pltpu
from jax.experimental.pallas import tpu_sc as plsc

F32 = jnp.float32
BF16 = jnp.bfloat16

D_MODEL = 1024
HEAD_DIM = 64
GRID_W = 64
EPS = 1e-6
ROPE_BASE = 10000.0
A_WIDTH = D_MODEL // 4
A_HEADS = A_WIDTH // HEAD_DIM
A_CHUNK = 128
B_WIDTH = D_MODEL // 2
B_HEADS = B_WIDTH // HEAD_DIM
B_KV_HEADS = 2
B_GROUP = B_HEADS // B_KV_HEADS
B_KV_WIDTH = B_KV_HEADS * HEAD_DIM
C_WIDTH = D_MODEL // 4
C_HEADS = C_WIDTH // HEAD_DIM
OFF_B = 2 * A_WIDTH
OFF_KV = OFF_B + B_WIDTH
OFF_V = OFF_KV + B_KV_WIDTH
OFF_C = OFF_KV + 2 * B_KV_WIDTH
OFF_G = OFF_C + 4 * C_WIDTH
IN_WIDTH = OFF_G + C_WIDTH
N_GROUPS = 4
EXPERTS_PER_GROUP = 8
N_EXPERTS = N_GROUPS * EXPERTS_PER_GROUP
TOP_K = 2
D_FF_EXPERT = D_MODEL // 2

HEAD_SHIFT = HEAD_DIM.bit_length() - 1
GROUP_SHIFT = EXPERTS_PER_GROUP.bit_length() - 1
ROPE_AXIS = HEAD_DIM // 2
ROPE_PAIR = ROPE_AXIS // 2
MOD_ROWS = 16
MOD_COLS = 1536
ROUTER_LANES = 128
PROJ_ROWS = 512
HGRN_BLOCK = 32
HGRN_GROUP = 16
LOG2E = 1.4426950408889634
ATTN_ROWS = 256
ATTN_KEYS = 512
SAFE_SHIFT = 60
SAFE_DECAY = 80.0
MOE_ROWS = 512
ROUTE_ROWS = 512
SC_WINDOW = 128
SC_ROW = 256
PLANES = D_MODEL // (2 * SC_ROW)
VMEM_LIMIT = 48 * 1024 * 1024


def _cparams(*sem):
    return pltpu.CompilerParams(dimension_semantics=sem, vmem_limit_bytes=VMEM_LIMIT)


def _head_ones(n, dtype):
    r = lax.broadcasted_iota(jnp.int32, (n, n), 0) >> HEAD_SHIFT
    c = lax.broadcasted_iota(jnp.int32, (n, n), 1) >> HEAD_SHIFT
    return (r == c).astype(dtype)


def _head_sum(x, ones_bd):
    return jnp.dot(x.astype(BF16), ones_bd, preferred_element_type=F32)


def _head_rms(x, ones_bd):
    w = ones_bd.shape[0]
    if x.shape[1] > w:
        return jnp.concatenate([_head_rms(x[:, c:c + w], ones_bd) for c in range(0, x.shape[1], w)], axis=1)
    return x * lax.rsqrt(_head_sum(x * x, ones_bd) * (1.0 / HEAD_DIM) + EPS)


def _pack_rows(y):
    bits = lax.bitcast_convert_type(y.astype(BF16).astype(F32), jnp.uint32)
    half = y.shape[1] // 2
    return lax.bitcast_convert_type(bits[:, :half] | (bits[:, half:] >> 16), F32)


def _unpack_rows(w):
    bits = lax.bitcast_convert_type(w, jnp.uint32)
    hi = lax.bitcast_convert_type(bits & jnp.uint32(0xFFFF0000), F32)
    lo = lax.bitcast_convert_type(bits << 16, F32)
    return hi, lo


def _pack_planes(y, ref, lead=()):
    for p in range(PLANES):
        ref[(p,) + lead] = _pack_rows(y[:, 2 * p * SC_ROW:(2 * p + 2) * SC_ROW])


def _mod_kernel(c_ref, w_ref, b_ref, o_ref):
    a = jax.nn.silu(c_ref[...])
    w = w_ref[0]
    a_hi, w_hi = a.astype(BF16), w.astype(BF16)
    a_lo, w_lo = (a - a_hi.astype(F32)).astype(BF16), (w - w_hi.astype(F32)).astype(BF16)
    o_ref[0] = (jnp.dot(a_hi, w_hi, preferred_element_type=F32) + jnp.dot(a_hi, w_lo, preferred_element_type=F32)
                + jnp.dot(a_lo, w_hi, preferred_element_type=F32) + b_ref[0])


def _mod(cc, w_mod, b_mod):
    depth, _, n = w_mod.shape
    tn = MOD_COLS
    return pl.pallas_call(
        _mod_kernel,
        out_shape=jax.ShapeDtypeStruct((depth, MOD_ROWS, n), F32),
        grid=(depth, n // tn),
        in_specs=[pl.BlockSpec((MOD_ROWS, D_MODEL), lambda l, j: (0, 0)),
                  pl.BlockSpec((1, D_MODEL, tn), lambda l, j: (l, 0, j)),
                  pl.BlockSpec((1, 1, tn), lambda l, j: (l, 0, j))],
        out_specs=pl.BlockSpec((1, MOD_ROWS, tn), lambda l, j: (l, 0, j)),
        compiler_params=_cparams("arbitrary", "arbitrary"),
        name="mod",
    )(cc, w_mod, b_mod.reshape(depth, 1, n))


def _rope(xn, c_ref, sp_ref, sm_ref):
    w = xn.shape[-1]
    rep = lambda ref: jnp.tile(ref[...], (1, w // ref.shape[-1]))
    return xn * rep(c_ref) + pltpu.roll(xn, ROPE_PAIR, 1) * rep(sp_ref) + pltpu.roll(xn, w - ROPE_PAIR, 1) * rep(sm_ref)


def _combined(x_ref, pk_ref, info_ref, g_ref):
    info = info_ref[...]
    g1 = info[:, 2:3]
    g2 = info[:, 3:4]
    parts = []
    for p in range(PLANES):
        hi1, lo1 = _unpack_rows(pk_ref[TOP_K * p])
        hi2, lo2 = _unpack_rows(pk_ref[TOP_K * p + 1])
        parts += [g1 * hi1 + g2 * hi2, g1 * lo1 + g2 * lo2]
    return x_ref[0] + g_ref[0] * jnp.concatenate(parts, axis=1)


def _gmlp(z, ws_ref, bias_ref):
    gz = jax.nn.gelu(z)
    u = gz[:, :A_WIDTH]
    vn = _head_rms(gz[:, A_WIDTH:], _head_ones(A_WIDTH, BF16))
    lane_head = lax.broadcasted_iota(jnp.int32, vn.shape, 1) >> HEAD_SHIFT
    acc = bias_ref[...]
    for hh in range(A_HEADS):
        vh = jnp.where(lane_head == hh, vn, 0.0).astype(BF16)
        acc = acc + jnp.dot(ws_ref[hh], vh, preferred_element_type=F32)
    return u * acc


def _inproj_kernel(*refs, pending):
    if pending:
        (x_ref, pk_ref, info_ref, g2_ref, mul_ref, add_ref, w_ref, qc_ref, qsp_ref, qsm_ref, kc_ref, ksp_ref, ksm_ref,
         ws_ref, bias_ref, xo_ref, ya_ref, q_ref, k_ref, v_ref, zc_ref, g_ref) = refs
        x = _combined(x_ref, pk_ref, info_ref, g2_ref)
        xo_ref[0] = x
    else:
        (x_ref, mul_ref, add_ref, w_ref, qc_ref, qsp_ref, qsm_ref, kc_ref, ksp_ref, ksm_ref,
         ws_ref, bias_ref, ya_ref, q_ref, k_ref, v_ref, zc_ref, g_ref) = refs
        x = x_ref[0]
    ms = jnp.mean(x * x, axis=-1, keepdims=True)
    h = x * lax.rsqrt(ms + EPS) * mul_ref[0] + add_ref[0]
    y = jnp.dot(h.astype(BF16), w_ref[...], preferred_element_type=F32)
    for c0 in range(0, x.shape[0], A_CHUNK):
        ya_ref[0, c0:c0 + A_CHUNK, :] = _gmlp(y[c0:c0 + A_CHUNK, :OFF_B], ws_ref, bias_ref).astype(BF16)
    qn = _head_rms(y[:, OFF_B:OFF_KV], _head_ones(B_WIDTH // 2, BF16))
    q_ref[0] = _rope(qn, qc_ref, qsp_ref, qsm_ref).astype(BF16)
    kn = _head_rms(y[:, OFF_KV:OFF_V], _head_ones(B_KV_WIDTH, BF16))
    k_ref[0] = _rope(kn, kc_ref, ksp_ref, ksm_ref).astype(BF16)
    v_ref[0] = y[:, OFF_V:OFF_C].astype(BF16)
    zc_ref[0] = y[:, OFF_C:OFF_G]
    g_ref[0] = y[:, OFF_G:].astype(BF16)


def _inproj(x, mul, add, w_bf, q_tabs, k_tabs, ws_bf, bias2d, pending=None):
    b, t, d = x.shape
    tm = min(PROJ_ROWS, t)
    row = lambda i, bb: (bb, i, 0)
    vec = lambda i, bb: (bb, 0, 0)
    tab = lambda i, bb: (i, 0)
    widths = (A_WIDTH, B_WIDTH, B_KV_WIDTH, B_KV_WIDTH, 4 * C_WIDTH, C_WIDTH)
    dtypes = (BF16, BF16, BF16, BF16, F32, BF16)
    pre_specs, pre_args = [], []
    if pending is not None:
        picked, info, gate, row0 = pending
        off = row0 // tm
        widths, dtypes = (d,) + widths, (F32,) + dtypes
        pre_specs = [pl.BlockSpec((PLANES * TOP_K, tm, SC_ROW), lambda i, bb: (0, off + bb * (t // tm) + i, 0)),
                     pl.BlockSpec((tm, ROUTER_LANES), lambda i, bb: (off + bb * (t // tm) + i, 0)),
                     pl.BlockSpec((1, 1, d), vec)]
        pre_args = [picked, info, gate]
    return pl.pallas_call(
        functools.partial(_inproj_kernel, pending=pending is not None),
        out_shape=[jax.ShapeDtypeStruct((b, t, w), dt) for w, dt in zip(widths, dtypes)],
        grid=(t // tm, b),
        in_specs=[pl.BlockSpec((1, tm, d), row)] + pre_specs
                 + [pl.BlockSpec((1, 1, d), vec),
                    pl.BlockSpec((1, 1, d), vec),
                    pl.BlockSpec((d, IN_WIDTH), lambda i, bb: (0, 0))]
                 + [pl.BlockSpec((tm, B_KV_WIDTH), tab)] * 6
                 + [pl.BlockSpec((A_HEADS, A_CHUNK, A_CHUNK), lambda i, bb: (0, 0, 0)),
                    pl.BlockSpec((A_CHUNK, A_WIDTH), lambda i, bb: (0, 0))],
        out_specs=[pl.BlockSpec((1, tm, w), row) for w in widths],
        compiler_params=_cparams("arbitrary", "arbitrary"),
        name="inproj",
    )(x, *pre_args, mul, add, w_bf, *q_tabs, *k_tabs, ws_bf, bias2d)


def _rope_tables(t, w, scale, width, rotate):
    ws = w.astype(F32) * scale
    if not rotate:
        c = jnp.broadcast_to(jnp.tile(ws, width // HEAD_DIM)[None, :], (t, width))
        z = jnp.zeros((t, width), F32)
        return c, z, z
    pos = jnp.arange(t)
    row = (pos // GRID_W).astype(F32)
    col = (pos % GRID_W).astype(F32)
    inv_freq = 1.0 / (ROPE_BASE ** (jnp.arange(0, HEAD_DIM // 2, 2, dtype=F32) / (HEAD_DIM // 2)))
    dd = jnp.arange(HEAD_DIM)
    axis = dd // ROPE_AXIS
    half = (dd % ROPE_AXIS) // ROPE_PAIR
    ang = jnp.where(axis[None, :] == 0, row[:, None], col[:, None]) * inv_freq[dd % ROPE_PAIR][None, :]
    cos, sin = jnp.cos(ang), jnp.sin(ang)
    c = cos * ws[None, :]
    sm = jnp.where(half[None, :] == 0, -sin * jnp.roll(ws, -ROPE_PAIR)[None, :], 0.0)
    sp = jnp.where(half[None, :] == 1, sin * jnp.roll(ws, ROPE_PAIR)[None, :], 0.0)
    rep = width // HEAD_DIM
    return jnp.tile(c, (1, rep)), jnp.tile(sp, (1, rep)), jnp.tile(sm, (1, rep))


def _attn_kernel(flag_ref, q_ref, *refs, n_seg):
    kv_refs, o_ref = refs[:2 * n_seg], refs[2 * n_seg]
    tq = q_ref.shape[1]
    dh = HEAD_DIM

    def heads(j):
        q4 = jnp.concatenate([q_ref[0, :, (B_GROUP * j + gg) * dh:(B_GROUP * j + gg + 1) * dh]
                              for gg in range(B_GROUP)], axis=0)
        ks, vs = [], []
        for sg in range(n_seg):
            s_len = kv_refs[2 * sg].shape[1]
            for c0 in range(0, s_len, ATTN_KEYS):
                c1 = min(c0 + ATTN_KEYS, s_len)
                ks.append(kv_refs[2 * sg][0, c0:c1, j * dh:(j + 1) * dh])
                vs.append(kv_refs[2 * sg + 1][0, c0:c1, j * dh:(j + 1) * dh])
        return q4, ks, vs

    def scores(q4, ks):
        return [lax.dot_general(kk, q4, (((1,), (1,)), ((), ())), preferred_element_type=F32) for kk in ks]

    def finish(j, ps, vs):
        l = functools.reduce(jnp.add, [jnp.sum(p, axis=0, keepdims=True) for p in ps])
        acc = sum(lax.dot_general(vv, p.astype(BF16), (((0,), (0,)), ((), ())), preferred_element_type=F32)
                  for p, vv in zip(ps, vs))
        o = (acc / l).T
        for gg in range(B_GROUP):
            hh = B_GROUP * j + gg
            o_ref[0, :, hh * dh:(hh + 1) * dh] = o[gg * tq:(gg + 1) * tq].astype(BF16)

    @pl.when(flag_ref[0] > 0)
    def _():
        shift = flag_ref[1].astype(F32)
        for j in range(B_KV_HEADS):
            q4, ks, vs = heads(j)
            finish(j, [jnp.exp2(s - shift) for s in scores(q4, ks)], vs)

    @pl.when(flag_ref[0] <= 0)
    def _():
        for j in range(B_KV_HEADS):
            q4, ks, vs = heads(j)
            ss = scores(q4, ks)
            m = functools.reduce(jnp.maximum, [jnp.max(s, axis=0, keepdims=True) for s in ss])
            finish(j, [jnp.exp2(s - m) for s in ss], vs)


def _attn(flag, q, kv_segs):
    b, t, w = q.shape
    tq = min(ATTN_ROWS, t)
    n_seg = len(kv_segs)
    kv_flat, kv_specs = [], []
    for kk, vv in kv_segs:
        s_len, kw = kk.shape[1:]
        kv_flat += [kk, vv]
        kv_specs += [pl.BlockSpec((1, s_len, kw), lambda bb, i, fl: (bb, 0, 0))] * 2
    return pl.pallas_call(
        functools.partial(_attn_kernel, n_seg=n_seg),
        out_shape=jax.ShapeDtypeStruct((b, t, w), BF16),
        grid_spec=pltpu.PrefetchScalarGridSpec(
            num_scalar_prefetch=1,
            grid=(b, t // tq),
            in_specs=[pl.BlockSpec((1, tq, w), lambda bb, i, fl: (bb, i, 0))] + kv_specs,
            out_specs=pl.BlockSpec((1, tq, w), lambda bb, i, fl: (bb, i, 0))),
        compiler_params=_cparams("arbitrary", "arbitrary"),
        name="attn",
    )(flag, q, *kv_flat)


def _scan_rows(x, reverse):
    n = x.shape[0]
    rows = lax.broadcasted_iota(jnp.int32, x.shape, 0)
    sh = 1
    while sh < n:
        if reverse:
            x = x + jnp.where(rows < n - sh, pltpu.roll(x, n - sh, 0), 0.0)
        else:
            x = x + jnp.where(rows >= sh, pltpu.roll(x, sh, 0), 0.0)
        sh *= 2
    return x


def _stack_heads(x, lane_head):
    return jnp.concatenate([jnp.where(lane_head == hh, x, 0.0) for hh in range(C_HEADS)], axis=0)


def _hgrn_kernel(z_ref, zc_ref, lbc_ref, *refs, ctx_out):
    if ctx_out:
        o_ref, oc_ref, st_ref, kx_ref, bx_ref, vx_ref, flag_ref = refs
    else:
        o_ref, st_ref, kx_ref, bx_ref, vx_ref, flag_ref = refs
        oc_ref = None
    n = C_WIDTH
    nb = HGRN_BLOCK
    nblk_c = zc_ref.shape[1] // nb
    nblk_l = z_ref.shape[1] // nb
    ones_bd = _head_ones(n, BF16)
    rows = lax.broadcasted_iota(jnp.int32, (nb, n), 0)
    lane_head = lax.broadcasted_iota(jnp.int32, (nb, n), 1) >> HEAD_SHIFT
    lane_head64 = lax.broadcasted_iota(jnp.int32, (HEAD_DIM, n), 1) >> HEAD_SHIFT
    low_half = (lax.broadcasted_iota(jnp.int32, (HEAD_DIM, 2 * HEAD_DIM), 1) < HEAD_DIM)
    sc_t = lax.broadcasted_iota(jnp.int32, (nb, C_HEADS * nb), 0)
    sc_s = lax.broadcasted_iota(jnp.int32, (nb, C_HEADS * nb), 1) & (nb - 1)

    def gates(z, d):
        one_m_lb = lbc_ref[d, 0:1, :]
        log1m_lb = lbc_ref[d, 1:2, :]
        log_lb = lbc_ref[d, 2:3, :]
        lb_pos = lbc_ref[d, 3:4, :] > 0.5
        soft = jnp.log(1.0 + jnp.exp(-jnp.abs(z)))
        log_rest = log1m_lb + (jnp.minimum(z, 0.0) - soft)
        lse = jnp.maximum(log_lb, log_rest) + jnp.log(1.0 + jnp.exp(-jnp.abs(log_lb - log_rest)))
        return jnp.where(lb_pos, lse, log_rest), one_m_lb * jnp.exp(jnp.minimum(-z, 0.0) - soft)

    def group_blocks(nblk, i):
        grp = HGRN_GROUP
        while nblk % grp:
            grp //= 2
        fwd = [i * grp + gg for gg in range(grp)]
        return grp, fwd, [nblk - 1 - blk for blk in fwd]

    def block_bound(src_ref, blk, d):
        zz = src_ref[0, pl.ds(pl.multiple_of(blk * nb, nb), nb), (1 + d) * n:(2 + d) * n]
        step_bound = jnp.minimum(lbc_ref[d, 4:5, :],
                                 jnp.maximum(-zz, 0.0) + (jnp.log(2.0) - lbc_ref[d, 1:2, :]))
        return jnp.sum(step_bound, axis=0, keepdims=True)

    def flag_groups(src_ref, nblk, base):
        def body(i, carry):
            _, fwd, bwd = group_blocks(nblk, i)
            worst = functools.reduce(jnp.maximum, [block_bound(src_ref, blk, d)
                                                   for d, blks in ((0, fwd), (1, bwd)) for blk in blks])
            flag_ref[base + i] = (jnp.max(worst) <= SAFE_DECAY).astype(jnp.int32)
            return carry
        lax.fori_loop(0, nblk // group_blocks(nblk, 0)[0], body, 0)

    def worst_bound(src_ref, nblk):
        def body(blk, worst):
            return jnp.maximum(worst, jnp.maximum(block_bound(src_ref, blk, 0), block_bound(src_ref, blk, 1)))
        return lax.fori_loop(0, nblk, body, jnp.zeros((1, n), F32))

    n_grp_c = nblk_c // group_blocks(nblk_c, 0)[0]
    all_safe = jnp.max(jnp.maximum(worst_bound(zc_ref, nblk_c), worst_bound(z_ref, nblk_l))) <= SAFE_DECAY
    st_ref[...] = jnp.zeros_like(st_ref)
    o_ref[...] = jnp.zeros_like(o_ref)
    if ctx_out:
        oc_ref[...] = jnp.zeros_like(oc_ref)

    def step(src_ref, dst_ref, blk, d, fast):
        reverse = d == 1
        r0 = pl.multiple_of(blk * nb, nb)
        v = src_ref[0, pl.ds(r0, nb), 3 * n:4 * n]
        log_f, k = gates(src_ref[0, pl.ds(r0, nb), (1 + d) * n:(2 + d) * n], d)
        bc = _scan_rows(log_f, reverse)
        edge = 0 if reverse else nb - 1
        b_edge = bc[edge:edge + 1, :]
        st = st_ref[d]
        v_bf = v.astype(BF16)

        if dst_ref is not None:
            q = jax.nn.silu(src_ref[0, pl.ds(r0, nb), 0:n])
            qt = (q * jnp.exp(bc)).astype(BF16)
            o = lax.dot_general(qt, _stack_heads(st, lane_head64).astype(BF16), (((1,), (1,)), ((), ())),
                                preferred_element_type=F32)

            def intra_fast():
                kt = _stack_heads(k * jnp.exp(-bc), lane_head).astype(BF16)
                sc = lax.dot_general(qt, kt, (((1,), (1,)), ((), ())), preferred_element_type=F32)
                keep = (sc_s >= sc_t) if reverse else (sc_s <= sc_t)
                sc = jnp.where(keep, sc, 0.0).astype(BF16)
                return jnp.dot(sc, _stack_heads(v, lane_head).astype(BF16), preferred_element_type=F32)

            def intra_exact():
                kx_ref[d] = k
                bx_ref[d] = bc
                vx_ref[d] = v

                def sbody(s, acc):
                    keep = (rows <= s) if reverse else (rows >= s)
                    e = jnp.exp(jnp.where(keep, bc - bx_ref[d, pl.ds(s, 1), :], 0.0))
                    p = jnp.where(keep, q * e * kx_ref[d, pl.ds(s, 1), :], 0.0)
                    sc = jnp.dot(p.astype(BF16), ones_bd, preferred_element_type=F32)
                    return acc + sc * vx_ref[d, pl.ds(s, 1), :]

                return lax.fori_loop(0, nb, sbody, jnp.zeros((nb, n), F32))

            o = o + (intra_fast() if fast else intra_exact())
            dst_ref[0, pl.ds(r0, nb), :] += o

        kd = (k * jnp.exp(b_edge - bc)).astype(BF16)
        full = lax.dot_general(v_bf, kd, (((0,), (0,)), ((), ())), preferred_element_type=F32)
        upd = jnp.concatenate(
            [jnp.where(low_half,
                       full[(2 * c) * HEAD_DIM:(2 * c + 1) * HEAD_DIM, 2 * c * HEAD_DIM:(2 * c + 2) * HEAD_DIM],
                       full[(2 * c + 1) * HEAD_DIM:(2 * c + 2) * HEAD_DIM, 2 * c * HEAD_DIM:(2 * c + 2) * HEAD_DIM])
             for c in range(C_HEADS // 2)], axis=1)
        st_ref[d] = st * jnp.exp(b_edge) + upd

    def run(src_ref, dst_ref, nblk, base, per_group):
        def body(i, carry):
            _, fwd, bwd = group_blocks(nblk, i)

            def group(fast):
                for bf, bb in zip(fwd, bwd):
                    step(src_ref, dst_ref, bf, 0, fast)
                    step(src_ref, dst_ref, bb, 1, fast)

            if per_group:
                safe = flag_ref[base + i]
                pl.when(safe > 0)(functools.partial(group, True))
                pl.when(safe <= 0)(functools.partial(group, False))
            else:
                group(True)
            return carry
        lax.fori_loop(0, nblk // group_blocks(nblk, 0)[0], body, 0)

    @pl.when(all_safe)
    def _():
        run(zc_ref, oc_ref, nblk_c, 0, False)
        run(z_ref, o_ref, nblk_l, n_grp_c, False)

    @pl.when(jnp.logical_not(all_safe))
    def _():
        flag_groups(zc_ref, nblk_c, 0)
        flag_groups(z_ref, nblk_l, n_grp_c)
        run(zc_ref, oc_ref, nblk_c, 0, True)
        run(z_ref, o_ref, nblk_l, n_grp_c, True)


def _hgrn(zc, zc_c, lbc, ctx_out):
    b, t, w = zc.shape
    lc = zc_c.shape[1]
    n = C_WIDTH
    row = lambda bb: (bb, 0, 0)
    out_shape = [jax.ShapeDtypeStruct((b, t, n), F32)]
    out_specs = [pl.BlockSpec((1, t, n), row)]
    if ctx_out:
        out_shape.append(jax.ShapeDtypeStruct((b, lc, n), F32))
        out_specs.append(pl.BlockSpec((1, lc, n), row))
    res = pl.pallas_call(
        functools.partial(_hgrn_kernel, ctx_out=ctx_out),
        out_shape=out_shape,
        grid=(b,),
        in_specs=[pl.BlockSpec((1, t, w), row),
                  pl.BlockSpec((1, lc, w), row),
                  pl.BlockSpec((2, 8, n), lambda bb: (0, 0, 0))],
        out_specs=out_specs,
        scratch_shapes=[pltpu.VMEM((2, HEAD_DIM, n), F32)]
                       + [pltpu.VMEM((2, HGRN_BLOCK, n), F32)] * 3
                       + [pltpu.SMEM(((t + lc) // HGRN_BLOCK,), jnp.int32)],
        compiler_params=_cparams("arbitrary"),
        name="hgrn",
    )(zc, zc_c, lbc)
    return (res[0], res[1]) if ctx_out else (res[0], None)


def _select_experts(lg):
    lane = lax.broadcasted_iota(jnp.int32, lg.shape, 1)
    lane_f = lane.astype(F32)
    neg = -jnp.inf
    gl = jnp.where(lane < N_GROUPS, lg, neg)
    gmax = jnp.max(gl, axis=1, keepdims=True)
    grp = jnp.min(jnp.where(gl == gmax, lane_f, float(ROUTER_LANES)), axis=1, keepdims=True).astype(jnp.int32)
    p_grp = 1.0 / jnp.sum(jnp.exp(gl - gmax), axis=1, keepdims=True)
    in_grp = (lane >= N_GROUPS) & (lane < N_GROUPS + N_EXPERTS) & (((lane - N_GROUPS) >> GROUP_SHIFT) == grp)
    el = jnp.where(in_grp, lg, neg)
    v1 = jnp.max(el, axis=1, keepdims=True)
    i1 = jnp.min(jnp.where(el == v1, lane_f, float(ROUTER_LANES)), axis=1, keepdims=True)
    el2 = jnp.where(lane_f == i1, neg, el)
    v2 = jnp.max(el2, axis=1, keepdims=True)
    i2 = jnp.min(jnp.where(el2 == v2, lane_f, float(ROUTER_LANES)), axis=1, keepdims=True)
    rr = jnp.exp(v2 - v1)
    g1 = p_grp / (1.0 + rr)
    g2 = p_grp * rr / (1.0 + rr)
    sel = jnp.where(lane == 0, i1, jnp.where(lane == 1, i2, jnp.where(lane == 2, g1, jnp.where(lane == 3, g2, 0.0))))
    counts = jnp.sum(((lane_f == i1) | (lane_f == i2)).astype(F32), axis=0, keepdims=True)
    return sel, counts


def _outproj_kernel(x_ref, ya_ref, yb_ref, o_ref, g_ref, w_ref, gate_ref, mul_ref, add_ref, hw_ref,
                    wr_ref, br_ref, xo_ref, h2_ref, sel_ref, cnt_ref):
    yc = _head_rms(o_ref[0], _head_ones(C_WIDTH, BF16)) * hw_ref[...] * jax.nn.silu(g_ref[0].astype(F32))
    y = jnp.dot(ya_ref[0], w_ref[0:A_WIDTH, :], preferred_element_type=F32)
    y = y + jnp.dot(yb_ref[0], w_ref[A_WIDTH:A_WIDTH + B_WIDTH, :], preferred_element_type=F32)
    y = y + jnp.dot(yc.astype(BF16), w_ref[A_WIDTH + B_WIDTH:, :], preferred_element_type=F32)
    xn = x_ref[0] + gate_ref[0] * y
    xo_ref[0] = xn
    ms = jnp.mean(xn * xn, axis=-1, keepdims=True)
    h2 = xn * lax.rsqrt(ms + EPS) * mul_ref[0] + add_ref[0]
    _pack_planes(h2, h2_ref, (0,))
    h_hi = h2.astype(BF16)
    h_lo = (h2 - h_hi.astype(F32)).astype(BF16)
    both = jnp.dot(h_hi, wr_ref[...], preferred_element_type=F32)
    lg = (both[:, :ROUTER_LANES] + both[:, ROUTER_LANES:] + br_ref[...]
          + jnp.dot(h_lo, wr_ref[:, 0:ROUTER_LANES], preferred_element_type=F32))
    sel, counts = _select_experts(lg)
    sel_ref[0] = sel

    @pl.when((pl.program_id(0) == 0) & (pl.program_id(1) == 0))
    def _():
        cnt_ref[...] = jnp.zeros_like(cnt_ref)
    cnt_ref[...] += counts


def _outproj(x, ya, yb, o, g, w_bf, gate, mul, add, hw, wr, br):
    b, t, d = x.shape
    tm = min(PROJ_ROWS, t)
    row = lambda bb, i: (bb, i, 0)
    vec = lambda bb, i: (bb, 0, 0)
    const = lambda bb, i: (0, 0)
    return pl.pallas_call(
        _outproj_kernel,
        out_shape=[jax.ShapeDtypeStruct((b, t, d), F32),
                   jax.ShapeDtypeStruct((PLANES, b, t, SC_ROW), F32),
                   jax.ShapeDtypeStruct((b, t, ROUTER_LANES), F32),
                   jax.ShapeDtypeStruct((8, ROUTER_LANES), F32)],
        grid=(b, t // tm),
        in_specs=[pl.BlockSpec((1, tm, d), row),
                  pl.BlockSpec((1, tm, A_WIDTH), row),
                  pl.BlockSpec((1, tm, B_WIDTH), row),
                  pl.BlockSpec((1, tm, C_WIDTH), row),
                  pl.BlockSpec((1, tm, C_WIDTH), row),
                  pl.BlockSpec((d, d), const),
                  pl.BlockSpec((1, 1, d), vec),
                  pl.BlockSpec((1, 1, d), vec),
                  pl.BlockSpec((1, 1, d), vec),
                  pl.BlockSpec((1, C_WIDTH), const),
                  pl.BlockSpec((d, 2 * ROUTER_LANES), const),
                  pl.BlockSpec((1, ROUTER_LANES), const)],
        out_specs=[pl.BlockSpec((1, tm, d), row),
                   pl.BlockSpec((PLANES, 1, tm, SC_ROW), lambda bb, i: (0, bb, i, 0)),
                   pl.BlockSpec((1, tm, ROUTER_LANES), row),
                   pl.BlockSpec((8, ROUTER_LANES), const)],
        compiler_params=_cparams("arbitrary", "arbitrary"),
        name="outproj",
    )(x, ya, yb, o, g, w_bf, gate, mul, add, hw, wr, br)


def _route_kernel(sel_ref, cnt_ref, info_ref, meta_ref, base_ref):
    i = pl.program_id(0)
    tm = sel_ref.shape[0]
    lane = lax.broadcasted_iota(jnp.int32, (tm, ROUTER_LANES), 1)
    lane_f = lane.astype(F32)
    sel = sel_ref[...]
    hit1 = lane_f == sel[:, 0:1]
    hit2 = lane_f == sel[:, 1:2]
    onehot = (hit1 | hit2).astype(F32)

    @pl.when(i == 0)
    def _():
        counts = cnt_ref[...]
        padded = jnp.floor((counts + (MOE_ROWS - 1.0)) * (1.0 / MOE_ROWS)) * MOE_ROWS
        r = lax.broadcasted_iota(jnp.int32, (ROUTER_LANES, ROUTER_LANES), 0)
        c = lax.broadcasted_iota(jnp.int32, (ROUTER_LANES, ROUTER_LANES), 1)
        ends = jnp.dot(padded, (r <= c).astype(F32), preferred_element_type=F32,
                       precision=lax.Precision.HIGHEST)
        base_ref[...] = (ends - padded)[0:1]
        row = lax.broadcasted_iota(jnp.int32, (8, ROUTER_LANES), 0)
        meta_ref[...] = jnp.where(row == 0, counts, jnp.where(row == 1, ends - padded, ends))

    tr = lax.broadcasted_iota(jnp.int32, (tm, tm), 0)
    tc = lax.broadcasted_iota(jnp.int32, (tm, tm), 1)
    before = jnp.dot((tc < tr).astype(BF16), onehot.astype(BF16), preferred_element_type=F32)
    pos = base_ref[...] + before
    d1 = jnp.sum(jnp.where(hit1, pos, 0.0), axis=1, keepdims=True)
    d2 = jnp.sum(jnp.where(hit2, pos, 0.0), axis=1, keepdims=True)
    base_ref[...] += jnp.sum(onehot, axis=0, keepdims=True)
    info_ref[...] = jnp.where(lane == 0, d1, jnp.where(lane == 1, d2, sel))


def _route(sel, counts):
    n = sel.shape[0]
    tm = ROUTE_ROWS if n % ROUTE_ROWS == 0 else ROUTE_ROWS // 2
    return pl.pallas_call(
        _route_kernel,
        out_shape=[jax.ShapeDtypeStruct((n, ROUTER_LANES), F32),
                   jax.ShapeDtypeStruct((8, ROUTER_LANES), F32)],
        grid=(n // tm,),
        in_specs=[pl.BlockSpec((tm, ROUTER_LANES), lambda i: (i, 0)),
                  pl.BlockSpec((8, ROUTER_LANES), lambda i: (0, 0))],
        out_specs=[pl.BlockSpec((tm, ROUTER_LANES), lambda i: (i, 0)),
                   pl.BlockSpec((8, ROUTER_LANES), lambda i: (0, 0))],
        scratch_shapes=[pltpu.VMEM((1, ROUTER_LANES), F32)],
        compiler_params=_cparams("arbitrary"),
        name="route",
    )(sel, counts)


def _sc_mesh():
    return plsc.VectorSubcoreMesh(core_axis_name="c", subcore_axis_name="s")


def _sc_gather(table, idx):
    n = idx.shape[0]
    d = table.shape[1]

    @functools.partial(pl.kernel, out_type=jax.ShapeDtypeStruct((n, d), table.dtype), mesh=_sc_mesh())
    def gather(x_hbm, i_hbm, o_hbm):
        def body(i_vmem, o_vmem):
            pltpu.sync_copy(x_hbm.at[i_vmem.at[0]], o_vmem)

        pltpu.emit_pipeline(
            body,
            grid=(n // SC_WINDOW,),
            in_specs=[pl.BlockSpec((1, SC_WINDOW), lambda i: (0, i))],
            out_specs=[pl.BlockSpec((SC_WINDOW, d), lambda i: (i, 0))],
            core_axis_name=("c", "s"),
            dimension_semantics=(pltpu.PARALLEL,),
        )(i_hbm, o_hbm)

    return gather(table, idx.reshape(1, n))


def _sc_scatter2(rows, idx0, idx1, n_out):
    m, d = rows.shape

    @functools.partial(pl.kernel, out_type=jax.ShapeDtypeStruct((n_out, d), rows.dtype), mesh=_sc_mesh())
    def scatter(x_hbm, i0_hbm, i1_hbm, o_hbm):
        def body(x_vmem, i0_vmem, i1_vmem):
            pltpu.sync_copy(x_vmem, o_hbm.at[i0_vmem.at[0]])
            pltpu.sync_copy(x_vmem, o_hbm.at[i1_vmem.at[0]])

        pltpu.emit_pipeline(
            body,
            grid=(m // SC_WINDOW,),
            in_specs=[pl.BlockSpec((SC_WINDOW, d), lambda i: (i, 0)),
                      pl.BlockSpec((1, SC_WINDOW), lambda i: (0, i)),
                      pl.BlockSpec((1, SC_WINDOW), lambda i: (0, i))],
            out_specs=[],
            core_axis_name=("c", "s"),
            dimension_semantics=(pltpu.PARALLEL,),
        )(x_hbm, i0_hbm, i1_hbm)

    return scatter(rows, idx0.reshape(1, m), idx1.reshape(1, m))


def _moe_kernel(be_ref, nu_ref, first_ref, slot_ref, nxt_ref, x_ref, w1_hbm, w3_hbm, w2_hbm, o_ref,
                w1f, w3f, w2f, w1b, w3b, w2b, sem, *, layer):
    i = pl.program_id(0)

    def fetch(e, slot):
        return [pltpu.make_async_copy(src.at[layer, e], dst.at[slot], sem.at[n, slot])
                for n, (src, dst) in enumerate(((w1_hbm, w1f), (w3_hbm, w3f), (w2_hbm, w2f)))]

    @pl.when((i == 0) & (nu_ref[0] > 0))
    def _():
        for cp in fetch(be_ref[0], 0):
            cp.start()

    @pl.when((first_ref[i] > 0) & (i < nu_ref[0]))
    def _():
        slot = slot_ref[i]
        for cp in fetch(be_ref[i], slot):
            cp.wait()
        w1b[...] = w1f[slot].astype(BF16)
        w3b[...] = w3f[slot].astype(BF16)
        w2b[...] = w2f[slot].astype(BF16)

        @pl.when(nxt_ref[i] >= 0)
        def _():
            for cp in fetch(nxt_ref[i], 1 - slot):
                cp.start()

    @pl.when(i < nu_ref[0])
    def _():
        parts = [h.astype(BF16) for p in range(PLANES) for h in _unpack_rows(x_ref[p])]
        a = sum(jnp.dot(h, w1b[q * SC_ROW:(q + 1) * SC_ROW, :], preferred_element_type=F32)
                for q, h in enumerate(parts))
        b = sum(jnp.dot(h, w3b[q * SC_ROW:(q + 1) * SC_ROW, :], preferred_element_type=F32)
                for q, h in enumerate(parts))
        hmid = (jax.nn.silu(a) * b).astype(BF16)
        _pack_planes(jnp.dot(hmid, w2b[...], preferred_element_type=F32), o_ref)

    @pl.when(i >= nu_ref[0])
    def _():
        o_ref[...] = jnp.zeros_like(o_ref)


def _moe_mlp(blk_expert, n_used, xs, w1, w3, w2, layer):
    n_rows = xs.shape[1]
    d, f = w1.shape[2:]
    nblk = n_rows // MOE_ROWS
    rows = lambda i, *_: (0, i, 0)
    idx = jnp.arange(nblk, dtype=jnp.int32)
    first = (idx < n_used[0]) & ((idx == 0) | (blk_expert != jnp.roll(blk_expert, 1)))
    slot = (jnp.cumsum(first.astype(jnp.int32)) - 1) % 2
    nxt_first = lax.cummin(jnp.where(first, idx, nblk)[::-1])[::-1]
    nxt_first = jnp.concatenate([nxt_first[1:], jnp.full((1,), nblk, jnp.int32)])
    nxt = jnp.where(nxt_first < nblk, blk_expert[jnp.minimum(nxt_first, nblk - 1)], -1)
    hbm = pl.BlockSpec(memory_space=pl.ANY)
    return pl.pallas_call(
        functools.partial(_moe_kernel, layer=layer),
        out_shape=jax.ShapeDtypeStruct((PLANES, n_rows, SC_ROW), F32),
        grid_spec=pltpu.PrefetchScalarGridSpec(
            num_scalar_prefetch=5,
            grid=(nblk,),
            in_specs=[pl.BlockSpec((PLANES, MOE_ROWS, SC_ROW), rows), hbm, hbm, hbm],
            out_specs=pl.BlockSpec((PLANES, MOE_ROWS, SC_ROW), rows),
            scratch_shapes=[pltpu.VMEM((2, d, f), F32), pltpu.VMEM((2, d, f), F32), pltpu.VMEM((2, f, d), F32),
                            pltpu.VMEM((d, f), BF16), pltpu.VMEM((d, f), BF16), pltpu.VMEM((f, d), BF16),
                            pltpu.SemaphoreType.DMA((3, 2))]),
        compiler_params=_cparams("arbitrary"),
        name="moe",
    )(blk_expert, n_used, first.astype(jnp.int32), slot.astype(jnp.int32), nxt.astype(jnp.int32), xs, w1, w3, w2)


def _combine_kernel(x_ref, pk_ref, info_ref, g_ref, o_ref):
    o_ref[0] = _combined(x_ref, pk_ref, info_ref, g_ref)


def _combine(x, picked, info, gate, row0):
    b, t, d = x.shape
    tm = min(256, t)
    off = row0 // tm
    tok = lambda bb, i: (off + bb * (t // tm) + i, 0)
    tok3 = lambda bb, i: (0, off + bb * (t // tm) + i, 0)
    return pl.pallas_call(
        _combine_kernel,
        out_shape=jax.ShapeDtypeStruct((b, t, d), F32),
        grid=(b, t // tm),
        in_specs=[pl.BlockSpec((1, tm, d), lambda bb, i: (bb, i, 0)),
                  pl.BlockSpec((PLANES * TOP_K, tm, SC_ROW), tok3),
                  pl.BlockSpec((tm, ROUTER_LANES), tok),
                  pl.BlockSpec((1, 1, d), lambda bb, i: (bb, 0, 0))],
        out_specs=pl.BlockSpec((1, tm, d), lambda bb, i: (bb, i, 0)),
        compiler_params=_cparams("arbitrary", "arbitrary"),
        name="combine",
    )(x, picked, info, gate)


def _hier_moe(h2p, sel, counts, w1, w3, w2, layer):
    n_tok = h2p.shape[1]
    info, meta = _route(sel, counts)
    dest = info[:, 0:TOP_K].astype(jnp.int32)
    pad_ends = meta[2, N_GROUPS:N_GROUPS + N_EXPERTS].astype(jnp.int32)
    nblk = -(-(n_tok * TOP_K) // MOE_ROWS) + N_EXPERTS
    n_rows = nblk * MOE_ROWS
    blk_start = jnp.arange(nblk, dtype=jnp.int32) * MOE_ROWS
    blk_expert = jnp.minimum(jnp.sum((pad_ends[None, :] <= blk_start[:, None]).astype(jnp.int32), axis=1),
                             N_EXPERTS - 1)
    n_used = pad_ends[-1:] // MOE_ROWS
    slot = [jnp.concatenate([p * n_rows + dest[:, s] for p in range(PLANES)]) for s in range(TOP_K)]
    xs = _sc_scatter2(h2p.reshape(PLANES * n_tok, SC_ROW), slot[0], slot[1], PLANES * n_rows)
    out = _moe_mlp(blk_expert, n_used, xs.reshape(PLANES, n_rows, SC_ROW), w1, w3, w2, layer)
    idx_all = jnp.concatenate([p * n_rows + dest[:, s] for p in range(PLANES) for s in range(TOP_K)])
    picked = _sc_gather(out.reshape(PLANES * n_rows, SC_ROW), idx_all)
    return picked.reshape(PLANES * TOP_K, n_tok, SC_ROW), info


def _layer(layer, x, xc, pend, mod, lb, norm1_w, w_in, w_s, b_s, q_norm_w, k_norm_w, hgrn_norm_w, w_out,
           norm2_w, w_grp, b_grp, w_exp, b_exp, w1, w3, w2, ctx_out):
    b, t, d = x.shape
    lc = xc.shape[1]
    sh1, sc1, g1, sh2, sc2, g2 = [m[:, None, :] for m in jnp.split(mod[:b], 6, axis=-1)]
    mod_c = [jnp.broadcast_to(m[None, None, :], (b, 1, d)) for m in jnp.split(mod[b], 6)]

    w_in_bf = w_in.astype(BF16)
    scale = LOG2E * HEAD_DIM ** -0.5
    q_tabs = _rope_tables(t, q_norm_w, scale, B_KV_WIDTH, True)
    k_tabs = _rope_tables(t, k_norm_w, 1.0, B_KV_WIDTH, True)
    qc_tabs = _rope_tables(lc, q_norm_w, scale, B_KV_WIDTH, False)
    kc_tabs = _rope_tables(lc, k_norm_w, 1.0, B_KV_WIDTH, False)
    ws_bf = w_s.astype(BF16)
    bias2d = jnp.repeat(b_s.T, HEAD_DIM, axis=1)
    res = _inproj(x, norm1_w * (1.0 + sc1), sh1, w_in_bf, q_tabs, k_tabs, ws_bf, bias2d,
                  None if pend is None else pend[:2] + (pend[2], 0))
    res_c = _inproj(xc, norm1_w * (1.0 + mod_c[1]), mod_c[0], w_in_bf, qc_tabs, kc_tabs, ws_bf, bias2d,
                    None if pend is None else pend[:2] + (pend[3], b * t))
    if pend is not None:
        x, xc, res, res_c = res[0], res_c[0], res[1:], res_c[1:]
    ya, q, k, v, zc, g = res
    ya_c, q_c, k_c, v_c, zc_c, g_c = res_c

    bound = LOG2E * HEAD_DIM ** 0.5 * jnp.max(jnp.abs(q_norm_w)) * jnp.max(jnp.abs(k_norm_w)) * 1.02
    shift = jnp.ceil(bound)
    attn_flag = jnp.stack([(shift <= SAFE_SHIFT).astype(jnp.int32), shift.astype(jnp.int32)])
    yb = _attn(attn_flag, q, [(k, v), (k_c, v_c)])

    pos = lb > 0.0
    log_lb = jnp.log(jnp.where(pos, lb, 1.0))
    lbc = jnp.stack([1.0 - lb, jnp.log1p(-lb), log_lb, pos.astype(F32), jnp.where(pos, -log_lb, 1e30)], axis=1)
    lbc = jnp.concatenate([lbc, jnp.zeros((2, 3, C_WIDTH), F32)], axis=1)
    o, o_c = _hgrn(zc, zc_c, lbc, ctx_out)

    w_out_bf = w_out.astype(BF16)
    hw = jnp.tile(hgrn_norm_w, C_HEADS)[None, :]
    wr = jnp.zeros((d, ROUTER_LANES), F32).at[:, :N_GROUPS].set(w_grp).at[
        :, N_GROUPS:N_GROUPS + N_EXPERTS].set(w_exp)
    wr_hi = wr.astype(BF16)
    wr = jnp.concatenate([wr_hi, (wr - wr_hi.astype(F32)).astype(BF16)], axis=1)
    br = jnp.zeros((1, ROUTER_LANES), F32).at[0, :N_GROUPS].set(b_grp).at[
        0, N_GROUPS:N_GROUPS + N_EXPERTS].set(b_exp)
    x, h2, sel, cnt = _outproj(x, ya, yb, o, g, w_out_bf, g1, norm2_w * (1.0 + sc2), sh2, hw, wr, br)
    if ctx_out:
        yb_c = _attn(attn_flag, q_c, [(k_c, v_c)])
        xc, h2c, sel_c, cnt_c = _outproj(xc, ya_c, yb_c, o_c, g_c, w_out_bf, mod_c[2],
                                norm2_w * (1.0 + mod_c[4]), mod_c[3], hw, wr, br)
        tokens = jnp.concatenate([h2.reshape(PLANES, -1, SC_ROW), h2c.reshape(PLANES, -1, SC_ROW)], axis=1)
        sel_all = jnp.concatenate([sel.reshape(-1, ROUTER_LANES), sel_c.reshape(-1, ROUTER_LANES)], axis=0)
        picked, info = _hier_moe(tokens, sel_all, cnt + cnt_c, w1, w3, w2, layer)
        return x, xc, (picked, info, g2, mod_c[5])
    picked, info = _hier_moe(h2.reshape(PLANES, -1, SC_ROW), sel.reshape(-1, ROUTER_LANES), cnt, w1, w3, w2, layer)
    return x, xc, (picked, info, g2, None)


def kernel(x, c, ctx, c_ctx, w_mod, b_mod, norm1_w, w_in, w_s, b_s, q_norm_w, k_norm_w, hgrn_lb_logits,
           hgrn_norm_w, w_out, norm2_w, w_grp, b_grp, w_exp, b_exp, w1, w3, w2):
    depth = w_mod.shape[0]
    lb_sm = jax.nn.softmax(hgrn_lb_logits.astype(F32), axis=0)
    lb = jnp.cumsum(lb_sm, axis=0) - lb_sm[0]
    xc = ctx
    pend = None
    b = x.shape[0]
    mods = _mod(jnp.zeros((MOD_ROWS, x.shape[2]), F32).at[:b].set(c).at[b].set(c_ctx), w_mod, b_mod)
    for l in range(depth):
        x, xc, pend = _layer(l, x, xc, pend, mods[l], lb[l], norm1_w[l], w_in[l], w_s[l], b_s[l],
                             q_norm_w[l], k_norm_w[l], hgrn_norm_w[l], w_out[l], norm2_w[l], w_grp[l], b_grp[l],
                             w_exp[l], b_exp[l], w1, w3, w2, ctx_out=(l < depth - 1))
    return _combine(x, pend[0], pend[1], pend[2], 0)
```

```python
import functools

import jax
import jax.numpy as jnp
from jax import lax
from jax.experimental import pallas as pl
from jax.experimental.pallas import tpu as pltpu
from jax.experimental.pallas import tpu_sc as plsc

F32 = jnp.float32
BF16 = jnp.bfloat16

D_MODEL = 1024
HEAD_DIM = 64
GRID_W = 64
EPS = 1e-6
ROPE_BASE = 10000.0
A_WIDTH = D_MODEL // 4
A_HEADS = A_WIDTH // HEAD_DIM
A_CHUNK = 128
B_WIDTH = D_MODEL // 2
B_HEADS = B_WIDTH // HEAD_DIM
B_KV_HEADS = 2
B_GROUP = B_HEADS // B_KV_HEADS
B_KV_WIDTH = B_KV_HEADS * HEAD_DIM
C_WIDTH = D_MODEL // 4
C_HEADS = C_WIDTH // HEAD_DIM
OFF_B = 2 * A_WIDTH
OFF_KV = OFF_B + B_WIDTH
OFF_V = OFF_KV + B_KV_WIDTH
OFF_C = OFF_KV + 2 * B_KV_WIDTH
OFF_G = OFF_C + 4 * C_WIDTH
IN_WIDTH = OFF_G + C_WIDTH
N_GROUPS = 4
EXPERTS_PER_GROUP = 8
N_EXPERTS = N_GROUPS * EXPERTS_PER_GROUP
TOP_K = 2
D_FF_EXPERT = D_MODEL // 2

HEAD_SHIFT = HEAD_DIM.bit_length() - 1
GROUP_SHIFT = EXPERTS_PER_GROUP.bit_length() - 1
ROPE_AXIS = HEAD_DIM // 2
ROPE_PAIR = ROPE_AXIS // 2
MOD_ROWS = 16
MOD_COLS = 1536
ROUTER_LANES = 128
PROJ_ROWS = 512
HGRN_BLOCK = 32
HGRN_GROUP = 16
LOG2E = 1.4426950408889634
ATTN_ROWS = 256
ATTN_KEYS = 512
SAFE_SHIFT = 60
SAFE_DECAY = 80.0
MOE_ROWS = 512
ROUTE_ROWS = 512
SC_WINDOW = 128
SC_ROW = 256
PLANES = D_MODEL // (2 * SC_ROW)
VMEM_LIMIT = 48 * 1024 * 1024


def _cparams(*sem):
    return pltpu.CompilerParams(dimension_semantics=sem, vmem_limit_bytes=VMEM_LIMIT)


def _head_ones(n, dtype):
    r = lax.broadcasted_iota(jnp.int32, (n, n), 0) >> HEAD_SHIFT
    c = lax.broadcasted_iota(jnp.int32, (n, n), 1) >> HEAD_SHIFT
    return (r == c).astype(dtype)


def _head_sum(x, ones_bd):
    return jnp.dot(x.astype(BF16), ones_bd, preferred_element_type=F32)


def _head_rms(x, ones_bd):
    w = ones_bd.shape[0]
    if x.shape[1] > w:
        return jnp.concatenate([_head_rms(x[:, c:c + w], ones_bd) for c in range(0, x.shape[1], w)], axis=1)
    return x * lax.rsqrt(_head_sum(x * x, ones_bd) * (1.0 / HEAD_DIM) + EPS)


def _pack_rows(y):
    bits = lax.bitcast_convert_type(y.astype(BF16).astype(F32), jnp.uint32)
    half = y.shape[1] // 2
    return lax.bitcast_convert_type(bits[:, :half] | (bits[:, half:] >> 16), F32)


def _unpack_rows(w):
    bits = lax.bitcast_convert_type(w, jnp.uint32)
    hi = lax.bitcast_convert_type(bits & jnp.uint32(0xFFFF0000), F32)
    lo = lax.bitcast_convert_type(bits << 16, F32)
    return hi, lo


def _pack_planes(y, ref, lead=()):
    for p in range(PLANES):
        ref[(p,) + lead] = _pack_rows(y[:, 2 * p * SC_ROW:(2 * p + 2) * SC_ROW])


def _mod_kernel(c_ref, w_ref, b_ref, o_ref):
    a = jax.nn.silu(c_ref[...])
    w = w_ref[0]
    a_hi, w_hi = a.astype(BF16), w.astype(BF16)
    a_lo, w_lo = (a - a_hi.astype(F32)).astype(BF16), (w - w_hi.astype(F32)).astype(BF16)
    o_ref[0] = (jnp.dot(a_hi, w_hi, preferred_element_type=F32) + jnp.dot(a_hi, w_lo, preferred_element_type=F32)
                + jnp.dot(a_lo, w_hi, preferred_element_type=F32) + b_ref[0])


def _mod(cc, w_mod, b_mod):
    depth, _, n = w_mod.shape
    tn = MOD_COLS
    return pl.pallas_call(
        _mod_kernel,
        out_shape=jax.ShapeDtypeStruct((depth, MOD_ROWS, n), F32),
        grid=(depth, n // tn),
        in_specs=[pl.BlockSpec((MOD_ROWS, D_MODEL), lambda l, j: (0, 0)),
                  pl.BlockSpec((1, D_MODEL, tn), lambda l, j: (l, 0, j)),
                  pl.BlockSpec((1, 1, tn), lambda l, j: (l, 0, j))],
        out_specs=pl.BlockSpec((1, MOD_ROWS, tn), lambda l, j: (l, 0, j)),
        compiler_params=_cparams("arbitrary", "arbitrary"),
        name="mod",
    )(cc, w_mod, b_mod.reshape(depth, 1, n))


def _rope(xn, c_ref, sp_ref, sm_ref):
    w = xn.shape[-1]
    rep = lambda ref: jnp.tile(ref[...], (1, w // ref.shape[-1]))
    return xn * rep(c_ref) + pltpu.roll(xn, ROPE_PAIR, 1) * rep(sp_ref) + pltpu.roll(xn, w - ROPE_PAIR, 1) * rep(sm_ref)


def _combined(x_ref, pk_ref, info_ref, g_ref):
    info = info_ref[...]
    g1 = info[:, 2:3]
    g2 = info[:, 3:4]
    parts = []
    for p in range(PLANES):
        hi1, lo1 = _unpack_rows(pk_ref[TOP_K * p])
        hi2, lo2 = _unpack_rows(pk_ref[TOP_K * p + 1])
        parts += [g1 * hi1 + g2 * hi2, g1 * lo1 + g2 * lo2]
    return x_ref[0] + g_ref[0] * jnp.concatenate(parts, axis=1)


def _gmlp(z, ws_ref, bias_ref):
    gz = jax.nn.gelu(z)
    u = gz[:, :A_WIDTH]
    vn = _head_rms(gz[:, A_WIDTH:], _head_ones(A_WIDTH, BF16))
    lane_head = lax.broadcasted_iota(jnp.int32, vn.shape, 1) >> HEAD_SHIFT
    acc = bias_ref[...]
    for hh in range(A_HEADS):
        vh = jnp.where(lane_head == hh, vn, 0.0).astype(BF16)
        acc = acc + jnp.dot(ws_ref[hh], vh, preferred_element_type=F32)
    return u * acc


def _inproj_kernel(*refs, pending):
    if pending:
        (x_ref, pk_ref, info_ref, g2_ref, mul_ref, add_ref, w_ref, qc_ref, qsp_ref, qsm_ref, kc_ref, ksp_ref, ksm_ref,
         ws_ref, bias_ref, xo_ref, ya_ref, q_ref, k_ref, v_ref, zc_ref, g_ref) = refs
        x = _combined(x_ref, pk_ref, info_ref, g2_ref)
        xo_ref[0] = x
    else:
        (x_ref, mul_ref, add_ref, w_ref, qc_ref, qsp_ref, qsm_ref, kc_ref, ksp_ref, ksm_ref,
         ws_ref, bias_ref, ya_ref, q_ref, k_ref, v_ref, zc_ref, g_ref) = refs
        x = x_ref[0]
    ms = jnp.mean(x * x, axis=-1, keepdims=True)
    h = x * lax.rsqrt(ms + EPS) * mul_ref[0] + add_ref[0]
    y = jnp.dot(h.astype(BF16), w_ref[...], preferred_element_type=F32)
    for c0 in range(0, x.shape[0], A_CHUNK):
        ya_ref[0, c0:c0 + A_CHUNK, :] = _gmlp(y[c0:c0 + A_CHUNK, :OFF_B], ws_ref, bias_ref).astype(BF16)
    qn = _head_rms(y[:, OFF_B:OFF_KV], _head_ones(B_WIDTH // 2, BF16))
    q_ref[0] = _rope(qn, qc_ref, qsp_ref, qsm_ref).astype(BF16)
    kn = _head_rms(y[:, OFF_KV:OFF_V], _head_ones(B_KV_WIDTH, BF16))
    k_ref[0] = _rope(kn, kc_ref, ksp_ref, ksm_ref).astype(BF16)
    v_ref[0] = y[:, OFF_V:OFF_C].astype(BF16)
    zc_ref[0] = y[:, OFF_C:OFF_G]
    g_ref[0] = y[:, OFF_G:].astype(BF16)


def _inproj(x, mul, add, w_bf, q_tabs, k_tabs, ws_bf, bias2d, pending=None):
    b, t, d = x.shape
    tm = min(PROJ_ROWS, t)
    row = lambda i, bb: (bb, i, 0)
    vec = lambda i, bb: (bb, 0, 0)
    tab = lambda i, bb: (i, 0)
    widths = (A_WIDTH, B_WIDTH, B_KV_WIDTH, B_KV_WIDTH, 4 * C_WIDTH, C_WIDTH)
    dtypes = (BF16, BF16, BF16, BF16, F32, BF16)
    pre_specs, pre_args = [], []
    if pending is not None:
        picked, info, gate, row0 = pending
        off = row0 // tm
        widths, dtypes = (d,) + widths, (F32,) + dtypes
        pre_specs = [pl.BlockSpec((PLANES * TOP_K, tm, SC_ROW), lambda i, bb: (0, off + bb * (t // tm) + i, 0)),
                     pl.BlockSpec((tm, ROUTER_LANES), lambda i, bb: (off + bb * (t // tm) + i, 0)),
                     pl.BlockSpec((1, 1, d), vec)]
        pre_args = [picked, info, gate]
    return pl.pallas_call(
        functools.partial(_inproj_kernel, pending=pending is not None),
        out_shape=[jax.ShapeDtypeStruct((b, t, w), dt) for w, dt in zip(widths, dtypes)],
        grid=(t // tm, b),
        in_specs=[pl.BlockSpec((1, tm, d), row)] + pre_specs
                 + [pl.BlockSpec((1, 1, d), vec),
                    pl.BlockSpec((1, 1, d), vec),
                    pl.BlockSpec((d, IN_WIDTH), lambda i, bb: (0, 0))]
                 + [pl.BlockSpec((tm, B_KV_WIDTH), tab)] * 6
                 + [pl.BlockSpec((A_HEADS, A_CHUNK, A_CHUNK), lambda i, bb: (0, 0, 0)),
                    pl.BlockSpec((A_CHUNK, A_WIDTH), lambda i, bb: (0, 0))],
        out_specs=[pl.BlockSpec((1, tm, w), row) for w in widths],
        compiler_params=_cparams("arbitrary", "arbitrary"),
        name="inproj",
    )(x, *pre_args, mul, add, w_bf, *q_tabs, *k_tabs, ws_bf, bias2d)


def _rope_tables(t, w, scale, width, rotate):
    ws = w.astype(F32) * scale
    if not rotate:
        c = jnp.broadcast_to(jnp.tile(ws, width // HEAD_DIM)[None, :], (t, width))
        z = jnp.zeros((t, width), F32)
        return c, z, z
    pos = jnp.arange(t)
    row = (pos // GRID_W).astype(F32)
    col = (pos % GRID_W).astype(F32)
    inv_freq = 1.0 / (ROPE_BASE ** (jnp.arange(0, HEAD_DIM // 2, 2, dtype=F32) / (HEAD_DIM // 2)))
    dd = jnp.arange(HEAD_DIM)
    axis = dd // ROPE_AXIS
    half = (dd % ROPE_AXIS) // ROPE_PAIR
    ang = jnp.where(axis[None, :] == 0, row[:, None], col[:, None]) * inv_freq[dd % ROPE_PAIR][None, :]
    cos, sin = jnp.cos(ang), jnp.sin(ang)
    c = cos * ws[None, :]
    sm = jnp.where(half[None, :] == 0, -sin * jnp.roll(ws, -ROPE_PAIR)[None, :], 0.0)
    sp = jnp.where(half[None, :] == 1, sin * jnp.roll(ws, ROPE_PAIR)[None, :], 0.0)
    rep = width // HEAD_DIM
    return jnp.tile(c, (1, rep)), jnp.tile(sp, (1, rep)), jnp.tile(sm, (1, rep))


def _attn_kernel(flag_ref, q_ref, *refs, n_seg):
    kv_refs, o_ref = refs[:2 * n_seg], refs[2 * n_seg]
    tq = q_ref.shape[1]
    dh = HEAD_DIM

    def heads(j):
        q4 = jnp.concatenate([q_ref[0, :, (B_GROUP * j + gg) * dh:(B_GROUP * j + gg + 1) * dh]
                              for gg in range(B_GROUP)], axis=0)
        ks, vs = [], []
        for sg in range(n_seg):
            s_len = kv_refs[2 * sg].shape[1]
            for c0 in range(0, s_len, ATTN_KEYS):
                c1 = min(c0 + ATTN_KEYS, s_len)
                ks.append(kv_refs[2 * sg][0, c0:c1, j * dh:(j + 1) * dh])
                vs.append(kv_refs[2 * sg + 1][0, c0:c1, j * dh:(j + 1) * dh])
        return q4, ks, vs

    def scores(q4, ks):
        return [lax.dot_general(kk, q4, (((1,), (1,)), ((), ())), preferred_element_type=F32) for kk in ks]

    def finish(j, ps, vs):
        l = functools.reduce(jnp.add, [jnp.sum(p, axis=0, keepdims=True) for p in ps])
        acc = sum(lax.dot_general(vv, p.astype(BF16), (((0,), (0,)), ((), ())), preferred_element_type=F32)
                  for p, vv in zip(ps, vs))
        o = (acc / l).T
        for gg in range(B_GROUP):
            hh = B_GROUP * j + gg
            o_ref[0, :, hh * dh:(hh + 1) * dh] = o[gg * tq:(gg + 1) * tq].astype(BF16)

    @pl.when(flag_ref[0] > 0)
    def _():
        shift = flag_ref[1].astype(F32)
        for j in range(B_KV_HEADS):
            q4, ks, vs = heads(j)
            finish(j, [jnp.exp2(s - shift) for s in scores(q4, ks)], vs)

    @pl.when(flag_ref[0] <= 0)
    def _():
        for j in range(B_KV_HEADS):
            q4, ks, vs = heads(j)
            ss = scores(q4, ks)
            m = functools.reduce(jnp.maximum, [jnp.max(s, axis=0, keepdims=True) for s in ss])
            finish(j, [jnp.exp2(s - m) for s in ss], vs)


def _attn(flag, q, kv_segs):
    b, t, w = q.shape
    tq = min(ATTN_ROWS, t)
    n_seg = len(kv_segs)
    kv_flat, kv_specs = [], []
    for kk, vv in kv_segs:
        s_len, kw = kk.shape[1:]
        kv_flat += [kk, vv]
        kv_specs += [pl.BlockSpec((1, s_len, kw), lambda bb, i, fl: (bb, 0, 0))] * 2
    return pl.pallas_call(
        functools.partial(_attn_kernel, n_seg=n_seg),
        out_shape=jax.ShapeDtypeStruct((b, t, w), BF16),
        grid_spec=pltpu.PrefetchScalarGridSpec(
            num_scalar_prefetch=1,
            grid=(b, t // tq),
            in_specs=[pl.BlockSpec((1, tq, w), lambda bb, i, fl: (bb, i, 0))] + kv_specs,
            out_specs=pl.BlockSpec((1, tq, w), lambda bb, i, fl: (bb, i, 0))),
        compiler_params=_cparams("arbitrary", "arbitrary"),
        name="attn",
    )(flag, q, *kv_flat)


def _scan_rows(x, reverse):
    n = x.shape[0]
    rows = lax.broadcasted_iota(jnp.int32, x.shape, 0)
    sh = 1
    while sh < n:
        if reverse:
            x = x + jnp.where(rows < n - sh, pltpu.roll(x, n - sh, 0), 0.0)
        else:
            x = x + jnp.where(rows >= sh, pltpu.roll(x, sh, 0), 0.0)
        sh *= 2
    return x


def _stack_heads(x, lane_head):
    return jnp.concatenate([jnp.where(lane_head == hh, x, 0.0) for hh in range(C_HEADS)], axis=0)


def _hgrn_kernel(z_ref, zc_ref, lbc_ref, *refs, ctx_out):
    if ctx_out:
        o_ref, oc_ref, st_ref, kx_ref, bx_ref, vx_ref, flag_ref = refs
    else:
        o_ref, st_ref, kx_ref, bx_ref, vx_ref, flag_ref = refs
        oc_ref = None
    n = C_WIDTH
    nb = HGRN_BLOCK
    nblk_c = zc_ref.shape[1] // nb
    nblk_l = z_ref.shape[1] // nb
    ones_bd = _head_ones(n, BF16)
    rows = lax.broadcasted_iota(jnp.int32, (nb, n), 0)
    lane_head = lax.broadcasted_iota(jnp.int32, (nb, n), 1) >> HEAD_SHIFT
    lane_head64 = lax.broadcasted_iota(jnp.int32, (HEAD_DIM, n), 1) >> HEAD_SHIFT
    low_half = (lax.broadcasted_iota(jnp.int32, (HEAD_DIM, 2 * HEAD_DIM), 1) < HEAD_DIM)
    sc_t = lax.broadcasted_iota(jnp.int32, (nb, C_HEADS * nb), 0)
    sc_s = lax.broadcasted_iota(jnp.int32, (nb, C_HEADS * nb), 1) & (nb - 1)

    def gates(z, d):
        one_m_lb = lbc_ref[d, 0:1, :]
        log1m_lb = lbc_ref[d, 1:2, :]
        log_lb = lbc_ref[d, 2:3, :]
        lb_pos = lbc_ref[d, 3:4, :] > 0.5
        soft = jnp.log(1.0 + jnp.exp(-jnp.abs(z)))
        log_rest = log1m_lb + (jnp.minimum(z, 0.0) - soft)
        lse = jnp.maximum(log_lb, log_rest) + jnp.log(1.0 + jnp.exp(-jnp.abs(log_lb - log_rest)))
        return jnp.where(lb_pos, lse, log_rest), one_m_lb * jnp.exp(jnp.minimum(-z, 0.0) - soft)

    def group_blocks(nblk, i):
        grp = HGRN_GROUP
        while nblk % grp:
            grp //= 2
        fwd = [i * grp + gg for gg in range(grp)]
        return grp, fwd, [nblk - 1 - blk for blk in fwd]

    def block_bound(src_ref, blk, d):
        zz = src_ref[0, pl.ds(pl.multiple_of(blk * nb, nb), nb), (1 + d) * n:(2 + d) * n]
        step_bound = jnp.minimum(lbc_ref[d, 4:5, :],
                                 jnp.maximum(-zz, 0.0) + (jnp.log(2.0) - lbc_ref[d, 1:2, :]))
        return jnp.sum(step_bound, axis=0, keepdims=True)

    def flag_groups(src_ref, nblk, base):
        def body(i, carry):
            _, fwd, bwd = group_blocks(nblk, i)
            worst = functools.reduce(jnp.maximum, [block_bound(src_ref, blk, d)
                                                   for d, blks in ((0, fwd), (1, bwd)) for blk in blks])
            flag_ref[base + i] = (jnp.max(worst) <= SAFE_DECAY).astype(jnp.int32)
            return carry
        lax.fori_loop(0, nblk // group_blocks(nblk, 0)[0], body, 0)

    def worst_bound(src_ref, nblk):
        def body(blk, worst):
            return jnp.maximum(worst, jnp.maximum(block_bound(src_ref, blk, 0), block_bound(src_ref, blk, 1)))
        return lax.fori_loop(0, nblk, body, jnp.zeros((1, n), F32))

    n_grp_c = nblk_c // group_blocks(nblk_c, 0)[0]
    all_safe = jnp.max(jnp.maximum(worst_bound(zc_ref, nblk_c), worst_bound(z_ref, nblk_l))) <= SAFE_DECAY
    st_ref[...] = jnp.zeros_like(st_ref)
    o_ref[...] = jnp.zeros_like(o_ref)
    if ctx_out:
        oc_ref[...] = jnp.zeros_like(oc_ref)

    def step(src_ref, dst_ref, blk, d, fast):
        reverse = d == 1
        r0 = pl.multiple_of(blk * nb, nb)
        v = src_ref[0, pl.ds(r0, nb), 3 * n:4 * n]
        log_f, k = gates(src_ref[0, pl.ds(r0, nb), (1 + d) * n:(2 + d) * n], d)
        bc = _scan_rows(log_f, reverse)
        edge = 0 if reverse else nb - 1
        b_edge = bc[edge:edge + 1, :]
        st = st_ref[d]
        v_bf = v.astype(BF16)

        if dst_ref is not None:
            q = jax.nn.silu(src_ref[0, pl.ds(r0, nb), 0:n])
            qt = (q * jnp.exp(bc)).astype(BF16)
            o = lax.dot_general(qt, _stack_heads(st, lane_head64).astype(BF16), (((1,), (1,)), ((), ())),
                                preferred_element_type=F32)

            def intra_fast():
                kt = _stack_heads(k * jnp.exp(-bc), lane_head).astype(BF16)
                sc = lax.dot_general(qt, kt, (((1,), (1,)), ((), ())), preferred_element_type=F32)
                keep = (sc_s >= sc_t) if reverse else (sc_s <= sc_t)
                sc = jnp.where(keep, sc, 0.0).astype(BF16)
                return jnp.dot(sc, _stack_heads(v, lane_head).astype(BF16), preferred_element_type=F32)

            def intra_exact():
                kx_ref[d] = k
                bx_ref[d] = bc
                vx_ref[d] = v

                def sbody(s, acc):
                    keep = (rows <= s) if reverse else (rows >= s)
                    e = jnp.exp(jnp.where(keep, bc - bx_ref[d, pl.ds(s, 1), :], 0.0))
                    p = jnp.where(keep, q * e * kx_ref[d, pl.ds(s, 1), :], 0.0)
                    sc = jnp.dot(p.astype(BF16), ones_bd, preferred_element_type=F32)
                    return acc + sc * vx_ref[d, pl.ds(s, 1), :]

                return lax.fori_loop(0, nb, sbody, jnp.zeros((nb, n), F32))

            o = o + (intra_fast() if fast else intra_exact())
            rows_o = dst_ref[0, pl.ds(r0, nb), :].astype(F32) + o
            dst_ref[0, pl.ds(r0, nb), :] = rows_o.astype(dst_ref.dtype)

        kd = (k * jnp.exp(b_edge - bc)).astype(BF16)
        full = lax.dot_general(v_bf, kd, (((0,), (0,)), ((), ())), preferred_element_type=F32)
        upd = jnp.concatenate(
            [jnp.where(low_half,
                       full[(2 * c) * HEAD_DIM:(2 * c + 1) * HEAD_DIM, 2 * c * HEAD_DIM:(2 * c + 2) * HEAD_DIM],
                       full[(2 * c + 1) * HEAD_DIM:(2 * c + 2) * HEAD_DIM, 2 * c * HEAD_DIM:(2 * c + 2) * HEAD_DIM])
             for c in range(C_HEADS // 2)], axis=1)
        st_ref[d] = st * jnp.exp(b_edge) + upd

    def run(src_ref, dst_ref, nblk, base, per_group):
        def body(i, carry):
            _, fwd, bwd = group_blocks(nblk, i)

            def group(fast):
                for bf, bb in zip(fwd, bwd):
                    step(src_ref, dst_ref, bf, 0, fast)
                    step(src_ref, dst_ref, bb, 1, fast)

            if per_group:
                safe = flag_ref[base + i]
                pl.when(safe > 0)(functools.partial(group, True))
                pl.when(safe <= 0)(functools.partial(group, False))
            else:
                group(True)
            return carry
        lax.fori_loop(0, nblk // group_blocks(nblk, 0)[0], body, 0)

    @pl.when(all_safe)
    def _():
        run(zc_ref, oc_ref, nblk_c, 0, False)
        run(z_ref, o_ref, nblk_l, n_grp_c, False)

    @pl.when(jnp.logical_not(all_safe))
    def _():
        flag_groups(zc_ref, nblk_c, 0)
        flag_groups(z_ref, nblk_l, n_grp_c)
        run(zc_ref, oc_ref, nblk_c, 0, True)
        run(z_ref, o_ref, nblk_l, n_grp_c, True)


def _hgrn(zc, zc_c, lbc, ctx_out):
    b, t, w = zc.shape
    lc = zc_c.shape[1]
    n = C_WIDTH
    row = lambda bb: (bb, 0, 0)
    out_shape = [jax.ShapeDtypeStruct((b, t, n), BF16)]
    out_specs = [pl.BlockSpec((1, t, n), row)]
    if ctx_out:
        out_shape.append(jax.ShapeDtypeStruct((b, lc, n), BF16))
        out_specs.append(pl.BlockSpec((1, lc, n), row))
    res = pl.pallas_call(
        functools.partial(_hgrn_kernel, ctx_out=ctx_out),
        out_shape=out_shape,
        grid=(b,),
        in_specs=[pl.BlockSpec((1, t, w), row),
                  pl.BlockSpec((1, lc, w), row),
                  pl.BlockSpec((2, 8, n), lambda bb: (0, 0, 0))],
        out_specs=out_specs,
        scratch_shapes=[pltpu.VMEM((2, HEAD_DIM, n), F32)]
                       + [pltpu.VMEM((2, HGRN_BLOCK, n), F32)] * 3
                       + [pltpu.SMEM(((t + lc) // HGRN_BLOCK,), jnp.int32)],
        compiler_params=_cparams("arbitrary"),
        name="hgrn",
    )(zc, zc_c, lbc)
    return (res[0], res[1]) if ctx_out else (res[0], None)


def _select_experts(lg):
    lane = lax.broadcasted_iota(jnp.int32, lg.shape, 1)
    lane_f = lane.astype(F32)
    neg = -jnp.inf
    gl = jnp.where(lane < N_GROUPS, lg, neg)
    gmax = jnp.max(gl, axis=1, keepdims=True)
    grp = jnp.min(jnp.where(gl == gmax, lane_f, float(ROUTER_LANES)), axis=1, keepdims=True).astype(jnp.int32)
    p_grp = 1.0 / jnp.sum(jnp.exp(gl - gmax), axis=1, keepdims=True)
    in_grp = (lane >= N_GROUPS) & (lane < N_GROUPS + N_EXPERTS) & (((lane - N_GROUPS) >> GROUP_SHIFT) == grp)
    el = jnp.where(in_grp, lg, neg)
    v1 = jnp.max(el, axis=1, keepdims=True)
    i1 = jnp.min(jnp.where(el == v1, lane_f, float(ROUTER_LANES)), axis=1, keepdims=True)
    el2 = jnp.where(lane_f == i1, neg, el)
    v2 = jnp.max(el2, axis=1, keepdims=True)
    i2 = jnp.min(jnp.where(el2 == v2, lane_f, float(ROUTER_LANES)), axis=1, keepdims=True)
    rr = jnp.exp(v2 - v1)
    g1 = p_grp / (1.0 + rr)
    g2 = p_grp * rr / (1.0 + rr)
    sel = jnp.where(lane == 0, i1, jnp.where(lane == 1, i2, jnp.where(lane == 2, g1, jnp.where(lane == 3, g2, 0.0))))
    counts = jnp.sum(((lane_f == i1) | (lane_f == i2)).astype(F32), axis=0, keepdims=True)
    return sel, counts


def _outproj_kernel(x_ref, ya_ref, yb_ref, o_ref, g_ref, w_ref, gate_ref, mul_ref, add_ref, hw_ref,
                    wr_ref, br_ref, xo_ref, h2_ref, sel_ref, cnt_ref):
    yc = _head_rms(o_ref[0].astype(F32), _head_ones(C_WIDTH, BF16)) * hw_ref[...] * jax.nn.silu(g_ref[0].astype(F32))
    y = jnp.dot(ya_ref[0], w_ref[0:A_WIDTH, :], preferred_element_type=F32)
    y = y + jnp.dot(yb_ref[0], w_ref[A_WIDTH:A_WIDTH + B_WIDTH, :], preferred_element_type=F32)
    y = y + jnp.dot(yc.astype(BF16), w_ref[A_WIDTH + B_WIDTH:, :], preferred_element_type=F32)
    xn = x_ref[0] + gate_ref[0] * y
    xo_ref[0] = xn
    ms = jnp.mean(xn * xn, axis=-1, keepdims=True)
    h2 = xn * lax.rsqrt(ms + EPS) * mul_ref[0] + add_ref[0]
    _pack_planes(h2, h2_ref, (0,))
    h_hi = h2.astype(BF16)
    h_lo = (h2 - h_hi.astype(F32)).astype(BF16)
    both = jnp.dot(h_hi, wr_ref[...], preferred_element_type=F32)
    lg = (both[:, :ROUTER_LANES] + both[:, ROUTER_LANES:] + br_ref[...]
          + jnp.dot(h_lo, wr_ref[:, 0:ROUTER_LANES], preferred_element_type=F32))
    sel, counts = _select_experts(lg)
    sel_ref[0] = sel

    @pl.when((pl.program_id(0) == 0) & (pl.program_id(1) == 0))
    def _():
        cnt_ref[...] = jnp.zeros_like(cnt_ref)
    cnt_ref[...] += counts


def _outproj(x, ya, yb, o, g, w_bf, gate, mul, add, hw, wr, br):
    b, t, d = x.shape
    tm = min(PROJ_ROWS, t)
    row = lambda bb, i: (bb, i, 0)
    vec = lambda bb, i: (bb, 0, 0)
    const = lambda bb, i: (0, 0)
    return pl.pallas_call(
        _outproj_kernel,
        out_shape=[jax.ShapeDtypeStruct((b, t, d), F32),
                   jax.ShapeDtypeStruct((PLANES, b, t, SC_ROW), F32),
                   jax.ShapeDtypeStruct((b, t, ROUTER_LANES), F32),
                   jax.ShapeDtypeStruct((8, ROUTER_LANES), F32)],
        grid=(b, t // tm),
        in_specs=[pl.BlockSpec((1, tm, d), row),
                  pl.BlockSpec((1, tm, A_WIDTH), row),
                  pl.BlockSpec((1, tm, B_WIDTH), row),
                  pl.BlockSpec((1, tm, C_WIDTH), row),
                  pl.BlockSpec((1, tm, C_WIDTH), row),
                  pl.BlockSpec((d, d), const),
                  pl.BlockSpec((1, 1, d), vec),
                  pl.BlockSpec((1, 1, d), vec),
                  pl.BlockSpec((1, 1, d), vec),
                  pl.BlockSpec((1, C_WIDTH), const),
                  pl.BlockSpec((d, 2 * ROUTER_LANES), const),
                  pl.BlockSpec((1, ROUTER_LANES), const)],
        out_specs=[pl.BlockSpec((1, tm, d), row),
                   pl.BlockSpec((PLANES, 1, tm, SC_ROW), lambda bb, i: (0, bb, i, 0)),
                   pl.BlockSpec((1, tm, ROUTER_LANES), row),
                   pl.BlockSpec((8, ROUTER_LANES), const)],
        compiler_params=_cparams("arbitrary", "arbitrary"),
        name="outproj",
    )(x, ya, yb, o, g, w_bf, gate, mul, add, hw, wr, br)


def _route_kernel(sel_ref, cnt_ref, info_ref, meta_ref, base_ref):
    i = pl.program_id(0)
    tm = sel_ref.shape[0]
    lane = lax.broadcasted_iota(jnp.int32, (tm, ROUTER_LANES), 1)
    lane_f = lane.astype(F32)
    sel = sel_ref[...]
    hit1 = lane_f == sel[:, 0:1]
    hit2 = lane_f == sel[:, 1:2]
    onehot = (hit1 | hit2).astype(F32)

    @pl.when(i == 0)
    def _():
        counts = cnt_ref[...]
        padded = jnp.floor((counts + (MOE_ROWS - 1.0)) * (1.0 / MOE_ROWS)) * MOE_ROWS
        r = lax.broadcasted_iota(jnp.int32, (ROUTER_LANES, ROUTER_LANES), 0)
        c = lax.broadcasted_iota(jnp.int32, (ROUTER_LANES, ROUTER_LANES), 1)
        ends = jnp.dot(padded, (r <= c).astype(F32), preferred_element_type=F32,
                       precision=lax.Precision.HIGHEST)
        base_ref[...] = (ends - padded)[0:1]
        row = lax.broadcasted_iota(jnp.int32, (8, ROUTER_LANES), 0)
        meta_ref[...] = jnp.where(row == 0, counts, jnp.where(row == 1, ends - padded, ends))

    tr = lax.broadcasted_iota(jnp.int32, (tm, tm), 0)
    tc = lax.broadcasted_iota(jnp.int32, (tm, tm), 1)
    before = jnp.dot((tc < tr).astype(BF16), onehot.astype(BF16), preferred_element_type=F32)
    pos = base_ref[...] + before
    d1 = jnp.sum(jnp.where(hit1, pos, 0.0), axis=1, keepdims=True)
    d2 = jnp.sum(jnp.where(hit2, pos, 0.0), axis=1, keepdims=True)
    base_ref[...] += jnp.sum(onehot, axis=0, keepdims=True)
    info_ref[...] = jnp.where(lane == 0, d1, jnp.where(lane == 1, d2, sel))


def _route(sel, counts):
    n = sel.shape[0]
    tm = ROUTE_ROWS if n % ROUTE_ROWS == 0 else ROUTE_ROWS // 2
    return pl.pallas_call(
        _route_kernel,
        out_shape=[jax.ShapeDtypeStruct((n, ROUTER_LANES), F32),
                   jax.ShapeDtypeStruct((8, ROUTER_LANES), F32)],
        grid=(n // tm,),
        in_specs=[pl.BlockSpec((tm, ROUTER_LANES), lambda i: (i, 0)),
                  pl.BlockSpec((8, ROUTER_LANES), lambda i: (0, 0))],
        out_specs=[pl.BlockSpec((tm, ROUTER_LANES), lambda i: (i, 0)),
                   pl.BlockSpec((8, ROUTER_LANES), lambda i: (0, 0))],
        scratch_shapes=[pltpu.VMEM((1, ROUTER_LANES), F32)],
        compiler_params=_cparams("arbitrary"),
        name="route",
    )(sel, counts)


def _sc_mesh():
    return plsc.VectorSubcoreMesh(core_axis_name="c", subcore_axis_name="s")


def _sc_gather(table, idx):
    n = idx.shape[0]
    d = table.shape[1]

    @functools.partial(pl.kernel, out_type=jax.ShapeDtypeStruct((n, d), table.dtype), mesh=_sc_mesh())
    def gather(x_hbm, i_hbm, o_hbm):
        def body(i_vmem, o_vmem):
            pltpu.sync_copy(x_hbm.at[i_vmem.at[0]], o_vmem)

        pltpu.emit_pipeline(
            body,
            grid=(n // SC_WINDOW,),
            in_specs=[pl.BlockSpec((1, SC_WINDOW), lambda i: (0, i))],
            out_specs=[pl.BlockSpec((SC_WINDOW, d), lambda i: (i, 0))],
            core_axis_name=("c", "s"),
            dimension_semantics=(pltpu.PARALLEL,),
        )(i_hbm, o_hbm)

    return gather(table, idx.reshape(1, n))


def _sc_scatter2(rows, idx0, idx1, n_out):
    m, d = rows.shape

    @functools.partial(pl.kernel, out_type=jax.ShapeDtypeStruct((n_out, d), rows.dtype), mesh=_sc_mesh())
    def scatter(x_hbm, i0_hbm, i1_hbm, o_hbm):
        def body(x_vmem, i0_vmem, i1_vmem):
            pltpu.sync_copy(x_vmem, o_hbm.at[i0_vmem.at[0]])
            pltpu.sync_copy(x_vmem, o_hbm.at[i1_vmem.at[0]])

        pltpu.emit_pipeline(
            body,
            grid=(m // SC_WINDOW,),
            in_specs=[pl.BlockSpec((SC_WINDOW, d), lambda i: (i, 0)),
                      pl.BlockSpec((1, SC_WINDOW), lambda i: (0, i)),
                      pl.BlockSpec((1, SC_WINDOW), lambda i: (0, i))],
            out_specs=[],
            core_axis_name=("c", "s"),
            dimension_semantics=(pltpu.PARALLEL,),
        )(x_hbm, i0_hbm, i1_hbm)

    return scatter(rows, idx0.reshape(1, m), idx1.reshape(1, m))


def _moe_kernel(be_ref, nu_ref, first_ref, slot_ref, nxt_ref, x_ref, w1_hbm, w3_hbm, w2_hbm, o_ref,
                w1f, w3f, w2f, w1b, w3b, w2b, sem, *, layer):
    i = pl.program_id(0)

    def fetch(e, slot):
        return [pltpu.make_async_copy(src.at[layer, e], dst.at[slot], sem.at[n, slot])
                for n, (src, dst) in enumerate(((w1_hbm, w1f), (w3_hbm, w3f), (w2_hbm, w2f)))]

    @pl.when((i == 0) & (nu_ref[0] > 0))
    def _():
        for cp in fetch(be_ref[0], 0):
            cp.start()

    @pl.when((first_ref[i] > 0) & (i < nu_ref[0]))
    def _():
        slot = slot_ref[i]
        for cp in fetch(be_ref[i], slot):
            cp.wait()
        w1b[...] = w1f[slot].astype(BF16)
        w3b[...] = w3f[slot].astype(BF16)
        w2b[...] = w2f[slot].astype(BF16)

        @pl.when(nxt_ref[i] >= 0)
        def _():
            for cp in fetch(nxt_ref[i], 1 - slot):
                cp.start()

    @pl.when(i < nu_ref[0])
    def _():
        parts = [h.astype(BF16) for p in range(PLANES) for h in _unpack_rows(x_ref[p])]
        a = sum(jnp.dot(h, w1b[q * SC_ROW:(q + 1) * SC_ROW, :], preferred_element_type=F32)
                for q, h in enumerate(parts))
        b = sum(jnp.dot(h, w3b[q * SC_ROW:(q + 1) * SC_ROW, :], preferred_element_type=F32)
                for q, h in enumerate(parts))
        hmid = (jax.nn.silu(a) * b).astype(BF16)
        _pack_planes(jnp.dot(hmid, w2b[...], preferred_element_type=F32), o_ref)

    @pl.when(i >= nu_ref[0])
    def _():
        o_ref[...] = jnp.zeros_like(o_ref)


def _moe_mlp(blk_expert, n_used, xs, w1, w3, w2, layer):
    n_rows = xs.shape[1]
    d, f = w1.shape[2:]
    nblk = n_rows // MOE_ROWS
    rows = lambda i, *_: (0, i, 0)
    idx = jnp.arange(nblk, dtype=jnp.int32)
    first = (idx < n_used[0]) & ((idx == 0) | (blk_expert != jnp.roll(blk_expert, 1)))
    slot = (jnp.cumsum(first.astype(jnp.int32)) - 1) % 2
    nxt_first = lax.cummin(jnp.where(first, idx, nblk)[::-1])[::-1]
    nxt_first = jnp.concatenate([nxt_first[1:], jnp.full((1,), nblk, jnp.int32)])
    nxt = jnp.where(nxt_first < nblk, blk_expert[jnp.minimum(nxt_first, nblk - 1)], -1)
    hbm = pl.BlockSpec(memory_space=pl.ANY)
    return pl.pallas_call(
        functools.partial(_moe_kernel, layer=layer),
        out_shape=jax.ShapeDtypeStruct((PLANES, n_rows, SC_ROW), F32),
        grid_spec=pltpu.PrefetchScalarGridSpec(
            num_scalar_prefetch=5,
            grid=(nblk,),
            in_specs=[pl.BlockSpec((PLANES, MOE_ROWS, SC_ROW), rows), hbm, hbm, hbm],
            out_specs=pl.BlockSpec((PLANES, MOE_ROWS, SC_ROW), rows),
            scratch_shapes=[pltpu.VMEM((2, d, f), F32), pltpu.VMEM((2, d, f), F32), pltpu.VMEM((2, f, d), F32),
                            pltpu.VMEM((d, f), BF16), pltpu.VMEM((d, f), BF16), pltpu.VMEM((f, d), BF16),
                            pltpu.SemaphoreType.DMA((3, 2))]),
        compiler_params=_cparams("arbitrary"),
        name="moe",
    )(blk_expert, n_used, first.astype(jnp.int32), slot.astype(jnp.int32), nxt.astype(jnp.int32), xs, w1, w3, w2)


def _combine_kernel(x_ref, pk_ref, info_ref, g_ref, o_ref):
    o_ref[0] = _combined(x_ref, pk_ref, info_ref, g_ref)


def _combine(x, picked, info, gate, row0):
    b, t, d = x.shape
    tm = min(256, t)
    off = row0 // tm
    tok = lambda bb, i: (off + bb * (t // tm) + i, 0)
    tok3 = lambda bb, i: (0, off + bb * (t // tm) + i, 0)
    return pl.pallas_call(
        _combine_kernel,
        out_shape=jax.ShapeDtypeStruct((b, t, d), F32),
        grid=(b, t // tm),
        in_specs=[pl.BlockSpec((1, tm, d), lambda bb, i: (bb, i, 0)),
                  pl.BlockSpec((PLANES * TOP_K, tm, SC_ROW), tok3),
                  pl.BlockSpec((tm, ROUTER_LANES), tok),
                  pl.BlockSpec((1, 1, d), lambda bb, i: (bb, 0, 0))],
        out_specs=pl.BlockSpec((1, tm, d), lambda bb, i: (bb, i, 0)),
        compiler_params=_cparams("arbitrary", "arbitrary"),
        name="combine",
    )(x, picked, info, gate)


def _hier_moe(h2p, sel, counts, w1, w3, w2, layer):
    n_tok = h2p.shape[1]
    info, meta = _route(sel, counts)
    dest = info[:, 0:TOP_K].astype(jnp.int32)
    pad_ends = meta[2, N_GROUPS:N_GROUPS + N_EXPERTS].astype(jnp.int32)
    nblk = -(-(n_tok * TOP_K) // MOE_ROWS) + N_EXPERTS
    n_rows = nblk * MOE_ROWS
    blk_start = jnp.arange(nblk, dtype=jnp.int32) * MOE_ROWS
    blk_expert = jnp.minimum(jnp.sum((pad_ends[None, :] <= blk_start[:, None]).astype(jnp.int32), axis=1),
                             N_EXPERTS - 1)
    n_used = pad_ends[-1:] // MOE_ROWS
    slot = [jnp.concatenate([p * n_rows + dest[:, s] for p in range(PLANES)]) for s in range(TOP_K)]
    xs = _sc_scatter2(h2p.reshape(PLANES * n_tok, SC_ROW), slot[0], slot[1], PLANES * n_rows)
    out = _moe_mlp(blk_expert, n_used, xs.reshape(PLANES, n_rows, SC_ROW), w1, w3, w2, layer)
    idx_all = jnp.concatenate([p * n_rows + dest[:, s] for p in range(PLANES) for s in range(TOP_K)])
    picked = _sc_gather(out.reshape(PLANES * n_rows, SC_ROW), idx_all)
    return picked.reshape(PLANES * TOP_K, n_tok, SC_ROW), info


def _layer(layer, x, xc, pend, mod, lb, norm1_w, w_in, w_s, b_s, q_norm_w, k_norm_w, hgrn_norm_w, w_out,
           norm2_w, w_grp, b_grp, w_exp, b_exp, w1, w3, w2, ctx_out):
    b, t, d = x.shape
    lc = xc.shape[1]
    sh1, sc1, g1, sh2, sc2, g2 = [m[:, None, :] for m in jnp.split(mod[:b], 6, axis=-1)]
    mod_c = [jnp.broadcast_to(m[None, None, :], (b, 1, d)) for m in jnp.split(mod[b], 6)]

    w_in_bf = w_in.astype(BF16)
    scale = LOG2E * HEAD_DIM ** -0.5
    q_tabs = _rope_tables(t, q_norm_w, scale, B_KV_WIDTH, True)
    k_tabs = _rope_tables(t, k_norm_w, 1.0, B_KV_WIDTH, True)
    qc_tabs = _rope_tables(lc, q_norm_w, scale, B_KV_WIDTH, False)
    kc_tabs = _rope_tables(lc, k_norm_w, 1.0, B_KV_WIDTH, False)
    ws_bf = w_s.astype(BF16)
    bias2d = jnp.repeat(b_s.T, HEAD_DIM, axis=1)
    res = _inproj(x, norm1_w * (1.0 + sc1), sh1, w_in_bf, q_tabs, k_tabs, ws_bf, bias2d,
                  None if pend is None else pend[:2] + (pend[2], 0))
    res_c = _inproj(xc, norm1_w * (1.0 + mod_c[1]), mod_c[0], w_in_bf, qc_tabs, kc_tabs, ws_bf, bias2d,
                    None if pend is None else pend[:2] + (pend[3], b * t))
    if pend is not None:
        x, xc, res, res_c = res[0], res_c[0], res[1:], res_c[1:]
    ya, q, k, v, zc, g = res
    ya_c, q_c, k_c, v_c, zc_c, g_c = res_c

    bound = LOG2E * HEAD_DIM ** 0.5 * jnp.max(jnp.abs(q_norm_w)) * jnp.max(jnp.abs(k_norm_w)) * 1.02
    shift = jnp.ceil(bound)
    attn_flag = jnp.stack([(shift <= SAFE_SHIFT).astype(jnp.int32), shift.astype(jnp.int32)])
    yb = _attn(attn_flag, q, [(k, v), (k_c, v_c)])

    pos = lb > 0.0
    log_lb = jnp.log(jnp.where(pos, lb, 1.0))
    lbc = jnp.stack([1.0 - lb, jnp.log1p(-lb), log_lb, pos.astype(F32), jnp.where(pos, -log_lb, 1e30)], axis=1)
    lbc = jnp.concatenate([lbc, jnp.zeros((2, 3, C_WIDTH), F32)], axis=1)
    o, o_c = _hgrn(zc, zc_c, lbc, ctx_out)

    w_out_bf = w_out.astype(BF16)
    hw = jnp.tile(hgrn_norm_w, C_HEADS)[None, :]
    wr = jnp.zeros((d, ROUTER_LANES), F32).at[:, :N_GROUPS].set(w_grp).at[
        :, N_GROUPS:N_GROUPS + N_EXPERTS].set(w_exp)
    wr_hi = wr.astype(BF16)
    wr = jnp.concatenate([wr_hi, (wr - wr_hi.astype(F32)).astype(BF16)], axis=1)
    br = jnp.zeros((1, ROUTER_LANES), F32).at[0, :N_GROUPS].set(b_grp).at[
        0, N_GROUPS:N_GROUPS + N_EXPERTS].set(b_exp)
    x, h2, sel, cnt = _outproj(x, ya, yb, o, g, w_out_bf, g1, norm2_w * (1.0 + sc2), sh2, hw, wr, br)
    if ctx_out:
        yb_c = _attn(attn_flag, q_c, [(k_c, v_c)])
        xc, h2c, sel_c, cnt_c = _outproj(xc, ya_c, yb_c, o_c, g_c, w_out_bf, mod_c[2],
                                norm2_w * (1.0 + mod_c[4]), mod_c[3], hw, wr, br)
        tokens = jnp.concatenate([h2.reshape(PLANES, -1, SC_ROW), h2c.reshape(PLANES, -1, SC_ROW)], axis=1)
        sel_all = jnp.concatenate([sel.reshape(-1, ROUTER_LANES), sel_c.reshape(-1, ROUTER_LANES)], axis=0)
        picked, info = _hier_moe(tokens, sel_all, cnt + cnt_c, w1, w3, w2, layer)
        return x, xc, (picked, info, g2, mod_c[5])
    picked, info = _hier_moe(h2.reshape(PLANES, -1, SC_ROW), sel.reshape(-1, ROUTER_LANES), cnt, w1, w3, w2, layer)
    return x, xc, (picked, info, g2, None)


def kernel(x, c, ctx, c_ctx, w_mod, b_mod, norm1_w, w_in, w_s, b_s, q_norm_w, k_norm_w, hgrn_lb_logits,
           hgrn_norm_w, w_out, norm2_w, w_grp, b_grp, w_exp, b_exp, w1, w3, w2):
    depth = w_mod.shape[0]
    lb_sm = jax.nn.softmax(hgrn_lb_logits.astype(F32), axis=0)
    lb = jnp.cumsum(lb_sm, axis=0) - lb_sm[0]
    xc = ctx
    pend = None
    b = x.shape[0]
    mods = _mod(jnp.zeros((MOD_ROWS, x.shape[2]), F32).at[:b].set(c).at[b].set(c_ctx), w_mod, b_mod)
    for l in range(depth):
        x, xc, pend = _layer(l, x, xc, pend, mods[l], lb[l], norm1_w[l], w_in[l], w_s[l], b_s[l],
                             q_norm_w[l], k_norm_w[l], hgrn_norm_w[l], w_out[l], norm2_w[l], w_grp[l], b_grp[l],
                             w_exp[l], b_exp[l], w1, w3, w2, ctx_out=(l < depth - 1))
    return _combine(x, pend[0], pend[1], pend[2], 0)
```

```python
import functools

import jax
import jax.numpy as jnp
from jax import lax
from jax.experimental import pallas as pl
from jax.experimental.pallas import tpu as pltpu
from jax.experimental.pallas import tpu_sc as plsc

F32 = jnp.float32
BF16 = jnp.bfloat16

D_MODEL = 1024
HEAD_DIM = 64
GRID_W = 64
EPS = 1e-6
ROPE_BASE = 10000.0
A_WIDTH = D_MODEL // 4
A_HEADS = A_WIDTH // HEAD_DIM
A_CHUNK = 128
B_WIDTH = D_MODEL // 2
B_HEADS = B_WIDTH // HEAD_DIM
B_KV_HEADS = 2
B_GROUP = B_HEADS // B_KV_HEADS
B_KV_WIDTH = B_KV_HEADS * HEAD_DIM
C_WIDTH = D_MODEL // 4
C_HEADS = C_WIDTH // HEAD_DIM
OFF_B = 2 * A_WIDTH
OFF_KV = OFF_B + B_WIDTH
OFF_V = OFF_KV + B_KV_WIDTH
OFF_C = OFF_KV + 2 * B_KV_WIDTH
OFF_G = OFF_C + 4 * C_WIDTH
IN_WIDTH = OFF_G + C_WIDTH
N_GROUPS = 4
EXPERTS_PER_GROUP = 8
N_EXPERTS = N_GROUPS * EXPERTS_PER_GROUP
TOP_K = 2
D_FF_EXPERT = D_MODEL // 2

HEAD_SHIFT = HEAD_DIM.bit_length() - 1
GROUP_SHIFT = EXPERTS_PER_GROUP.bit_length() - 1
ROPE_AXIS = HEAD_DIM // 2
ROPE_PAIR = ROPE_AXIS // 2
MOD_ROWS = 16
MOD_COLS = 1536
ROUTER_LANES = 128
PROJ_ROWS = 512
HGRN_BLOCK = 32
HGRN_GROUP = 16
LOG2E = 1.4426950408889634
ATTN_ROWS = 256
ATTN_KEYS = 512
SAFE_SHIFT = 60
SAFE_DECAY = 80.0
MOE_ROWS = 512
ROUTE_ROWS = 512
SC_WINDOW = 128
SC_ROW = 256
PLANES = D_MODEL // (2 * SC_ROW)
VMEM_LIMIT = 48 * 1024 * 1024


def _cparams(*sem):
    return pltpu.CompilerParams(dimension_semantics=sem, vmem_limit_bytes=VMEM_LIMIT)


def _head_ones(n, dtype):
    r = lax.broadcasted_iota(jnp.int32, (n, n), 0) >> HEAD_SHIFT
    c = lax.broadcasted_iota(jnp.int32, (n, n), 1) >> HEAD_SHIFT
    return (r == c).astype(dtype)


def _head_sum(x, ones_bd):
    return jnp.dot(x.astype(BF16), ones_bd, preferred_element_type=F32)


def _head_rms(x, ones_bd):
    w = ones_bd.shape[0]
    if x.shape[1] > w:
        return jnp.concatenate([_head_rms(x[:, c:c + w], ones_bd) for c in range(0, x.shape[1], w)], axis=1)
    return x * lax.rsqrt(_head_sum(x * x, ones_bd) * (1.0 / HEAD_DIM) + EPS)


def _pack_rows(y):
    bits = lax.bitcast_convert_type(y.astype(BF16).astype(F32), jnp.uint32)
    half = y.shape[1] // 2
    return lax.bitcast_convert_type(bits[:, :half] | (bits[:, half:] >> 16), F32)


def _unpack_rows(w):
    bits = lax.bitcast_convert_type(w, jnp.uint32)
    hi = lax.bitcast_convert_type(bits & jnp.uint32(0xFFFF0000), F32)
    lo = lax.bitcast_convert_type(bits << 16, F32)
    return hi, lo


def _pack_planes(y, ref, lead=()):
    for p in range(PLANES):
        ref[(p,) + lead] = _pack_rows(y[:, 2 * p * SC_ROW:(2 * p + 2) * SC_ROW])


def _mod_kernel(c_ref, w_ref, b_ref, o_ref):
    a = jax.nn.silu(c_ref[...])
    w = w_ref[0]
    a_hi, w_hi = a.astype(BF16), w.astype(BF16)
    a_lo, w_lo = (a - a_hi.astype(F32)).astype(BF16), (w - w_hi.astype(F32)).astype(BF16)
    o_ref[0] = (jnp.dot(a_hi, w_hi, preferred_element_type=F32) + jnp.dot(a_hi, w_lo, preferred_element_type=F32)
                + jnp.dot(a_lo, w_hi, preferred_element_type=F32) + b_ref[0])


def _mod(cc, w_mod, b_mod):
    depth, _, n = w_mod.shape
    tn = MOD_COLS
    return pl.pallas_call(
        _mod_kernel,
        out_shape=jax.ShapeDtypeStruct((depth, MOD_ROWS, n), F32),
        grid=(depth, n // tn),
        in_specs=[pl.BlockSpec((MOD_ROWS, D_MODEL), lambda l, j: (0, 0)),
                  pl.BlockSpec((1, D_MODEL, tn), lambda l, j: (l, 0, j)),
                  pl.BlockSpec((1, 1, tn), lambda l, j: (l, 0, j))],
        out_specs=pl.BlockSpec((1, MOD_ROWS, tn), lambda l, j: (l, 0, j)),
        compiler_params=_cparams("arbitrary", "arbitrary"),
        name="mod",
    )(cc, w_mod, b_mod.reshape(depth, 1, n))


def _rope(xn, c_ref, sp_ref, sm_ref):
    w = xn.shape[-1]
    rep = lambda ref: jnp.tile(ref[...], (1, w // ref.shape[-1]))
    return xn * rep(c_ref) + pltpu.roll(xn, ROPE_PAIR, 1) * rep(sp_ref) + pltpu.roll(xn, w - ROPE_PAIR, 1) * rep(sm_ref)


def _combined(x_ref, pk_ref, info_ref, g_ref):
    info = info_ref[...]
    g1 = info[:, 2:3]
    g2 = info[:, 3:4]
    parts = []
    for p in range(PLANES):
        hi1, lo1 = _unpack_rows(pk_ref[TOP_K * p])
        hi2, lo2 = _unpack_rows(pk_ref[TOP_K * p + 1])
        parts += [g1 * hi1 + g2 * hi2, g1 * lo1 + g2 * lo2]
    return x_ref[0] + g_ref[0] * jnp.concatenate(parts, axis=1)


def _gmlp(z, ws_ref, bias_ref):
    gz = jax.nn.gelu(z)
    u = gz[:, :A_WIDTH]
    vn = _head_rms(gz[:, A_WIDTH:], _head_ones(A_WIDTH, BF16))
    lane_head = lax.broadcasted_iota(jnp.int32, vn.shape, 1) >> HEAD_SHIFT
    acc = bias_ref[...]
    for hh in range(A_HEADS):
        vh = jnp.where(lane_head == hh, vn, 0.0).astype(BF16)
        acc = acc + jnp.dot(ws_ref[hh], vh, preferred_element_type=F32)
    return u * acc


def _inproj_kernel(*refs, pending):
    if pending:
        (x_ref, pk_ref, info_ref, g2_ref, mul_ref, add_ref, w_ref, qc_ref, qsp_ref, qsm_ref, kc_ref, ksp_ref, ksm_ref,
         ws_ref, bias_ref, xo_ref, ya_ref, q_ref, k_ref, v_ref, zc_ref, g_ref) = refs
        x = _combined(x_ref, pk_ref, info_ref, g2_ref)
        xo_ref[0] = x
    else:
        (x_ref, mul_ref, add_ref, w_ref, qc_ref, qsp_ref, qsm_ref, kc_ref, ksp_ref, ksm_ref,
         ws_ref, bias_ref, ya_ref, q_ref, k_ref, v_ref, zc_ref, g_ref) = refs
        x = x_ref[0]
    ms = jnp.mean(x * x, axis=-1, keepdims=True)
    h = x * lax.rsqrt(ms + EPS) * mul_ref[0] + add_ref[0]
    y = jnp.dot(h.astype(BF16), w_ref[...], preferred_element_type=F32)
    for c0 in range(0, x.shape[0], A_CHUNK):
        ya_ref[0, c0:c0 + A_CHUNK, :] = _gmlp(y[c0:c0 + A_CHUNK, :OFF_B], ws_ref, bias_ref).astype(BF16)
    qn = _head_rms(y[:, OFF_B:OFF_KV], _head_ones(B_WIDTH // 2, BF16))
    q_ref[0] = _rope(qn, qc_ref, qsp_ref, qsm_ref).astype(BF16)
    kn = _head_rms(y[:, OFF_KV:OFF_V], _head_ones(B_KV_WIDTH, BF16))
    k_ref[0] = _rope(kn, kc_ref, ksp_ref, ksm_ref).astype(BF16)
    v_ref[0] = y[:, OFF_V:OFF_C].astype(BF16)
    zc_ref[0] = y[:, OFF_C:OFF_G]
    g_ref[0] = y[:, OFF_G:].astype(BF16)


def _inproj(x, mul, add, w_bf, q_tabs, k_tabs, ws_bf, bias2d, pending=None):
    b, t, d = x.shape
    tm = min(PROJ_ROWS, t)
    row = lambda i, bb: (bb, i, 0)
    vec = lambda i, bb: (bb, 0, 0)
    tab = lambda i, bb: (i, 0)
    widths = (A_WIDTH, B_WIDTH, B_KV_WIDTH, B_KV_WIDTH, 4 * C_WIDTH, C_WIDTH)
    dtypes = (BF16, BF16, BF16, BF16, F32, BF16)
    pre_specs, pre_args = [], []
    if pending is not None:
        picked, info, gate, row0 = pending
        off = row0 // tm
        widths, dtypes = (d,) + widths, (F32,) + dtypes
        pre_specs = [pl.BlockSpec((PLANES * TOP_K, tm, SC_ROW), lambda i, bb: (0, off + bb * (t // tm) + i, 0)),
                     pl.BlockSpec((tm, ROUTER_LANES), lambda i, bb: (off + bb * (t // tm) + i, 0)),
                     pl.BlockSpec((1, 1, d), vec)]
        pre_args = [picked, info, gate]
    return pl.pallas_call(
        functools.partial(_inproj_kernel, pending=pending is not None),
        out_shape=[jax.ShapeDtypeStruct((b, t, w), dt) for w, dt in zip(widths, dtypes)],
        grid=(t // tm, b),
        in_specs=[pl.BlockSpec((1, tm, d), row)] + pre_specs
                 + [pl.BlockSpec((1, 1, d), vec),
                    pl.BlockSpec((1, 1, d), vec),
                    pl.BlockSpec((d, IN_WIDTH), lambda i, bb: (0, 0))]
                 + [pl.BlockSpec((tm, B_KV_WIDTH), tab)] * 6
                 + [pl.BlockSpec((A_HEADS, A_CHUNK, A_CHUNK), lambda i, bb: (0, 0, 0)),
                    pl.BlockSpec((A_CHUNK, A_WIDTH), lambda i, bb: (0, 0))],
        out_specs=[pl.BlockSpec((1, tm, w), row) for w in widths],
        compiler_params=_cparams("arbitrary", "arbitrary"),
        name="inproj",
    )(x, *pre_args, mul, add, w_bf, *q_tabs, *k_tabs, ws_bf, bias2d)


def _rope_tables(t, w, scale, width, rotate):
    ws = w.astype(F32) * scale
    if not rotate:
        c = jnp.broadcast_to(jnp.tile(ws, width // HEAD_DIM)[None, :], (t, width))
        z = jnp.zeros((t, width), F32)
        return c, z, z
    pos = jnp.arange(t)
    row = (pos // GRID_W).astype(F32)
    col = (pos % GRID_W).astype(F32)
    inv_freq = 1.0 / (ROPE_BASE ** (jnp.arange(0, HEAD_DIM // 2, 2, dtype=F32) / (HEAD_DIM // 2)))
    dd = jnp.arange(HEAD_DIM)
    axis = dd // ROPE_AXIS
    half = (dd % ROPE_AXIS) // ROPE_PAIR
    ang = jnp.where(axis[None, :] == 0, row[:, None], col[:, None]) * inv_freq[dd % ROPE_PAIR][None, :]
    cos, sin = jnp.cos(ang), jnp.sin(ang)
    c = cos * ws[None, :]
    sm = jnp.where(half[None, :] == 0, -sin * jnp.roll(ws, -ROPE_PAIR)[None, :], 0.0)
    sp = jnp.where(half[None, :] == 1, sin * jnp.roll(ws, ROPE_PAIR)[None, :], 0.0)
    rep = width // HEAD_DIM
    return jnp.tile(c, (1, rep)), jnp.tile(sp, (1, rep)), jnp.tile(sm, (1, rep))


def _attn_kernel(flag_ref, q_ref, *refs, n_seg):
    kv_refs, o_ref = refs[:2 * n_seg], refs[2 * n_seg]
    tq = q_ref.shape[1]
    dh = HEAD_DIM

    def heads(j):
        q4 = jnp.concatenate([q_ref[0, :, (B_GROUP * j + gg) * dh:(B_GROUP * j + gg + 1) * dh]
                              for gg in range(B_GROUP)], axis=0)
        ks, vs = [], []
        for sg in range(n_seg):
            s_len = kv_refs[2 * sg].shape[1]
            for c0 in range(0, s_len, ATTN_KEYS):
                c1 = min(c0 + ATTN_KEYS, s_len)
                ks.append(kv_refs[2 * sg][0, c0:c1, j * dh:(j + 1) * dh])
                vs.append(kv_refs[2 * sg + 1][0, c0:c1, j * dh:(j + 1) * dh])
        return q4, ks, vs

    def scores(q4, ks):
        return [lax.dot_general(kk, q4, (((1,), (1,)), ((), ())), preferred_element_type=F32) for kk in ks]

    def finish(j, ps, vs):
        l = functools.reduce(jnp.add, [jnp.sum(p, axis=0, keepdims=True) for p in ps])
        acc = sum(lax.dot_general(vv, p.astype(BF16), (((0,), (0,)), ((), ())), preferred_element_type=F32)
                  for p, vv in zip(ps, vs))
        o = (acc / l).T
        for gg in range(B_GROUP):
            hh = B_GROUP * j + gg
            o_ref[0, :, hh * dh:(hh + 1) * dh] = o[gg * tq:(gg + 1) * tq].astype(BF16)

    @pl.when(flag_ref[0] > 0)
    def _():
        shift = flag_ref[1].astype(F32)
        for j in range(B_KV_HEADS):
            q4, ks, vs = heads(j)
            finish(j, [jnp.exp2(s - shift) for s in scores(q4, ks)], vs)

    @pl.when(flag_ref[0] <= 0)
    def _():
        for j in range(B_KV_HEADS):
            q4, ks, vs = heads(j)
            ss = scores(q4, ks)
            m = functools.reduce(jnp.maximum, [jnp.max(s, axis=0, keepdims=True) for s in ss])
            finish(j, [jnp.exp2(s - m) for s in ss], vs)


def _attn(flag, q, kv_segs):
    b, t, w = q.shape
    tq = min(ATTN_ROWS, t)
    n_seg = len(kv_segs)
    kv_flat, kv_specs = [], []
    for kk, vv in kv_segs:
        s_len, kw = kk.shape[1:]
        kv_flat += [kk, vv]
        kv_specs += [pl.BlockSpec((1, s_len, kw), lambda bb, i, fl: (bb, 0, 0))] * 2
    return pl.pallas_call(
        functools.partial(_attn_kernel, n_seg=n_seg),
        out_shape=jax.ShapeDtypeStruct((b, t, w), BF16),
        grid_spec=pltpu.PrefetchScalarGridSpec(
            num_scalar_prefetch=1,
            grid=(b, t // tq),
            in_specs=[pl.BlockSpec((1, tq, w), lambda bb, i, fl: (bb, i, 0))] + kv_specs,
            out_specs=pl.BlockSpec((1, tq, w), lambda bb, i, fl: (bb, i, 0))),
        compiler_params=_cparams("arbitrary", "arbitrary"),
        name="attn",
    )(flag, q, *kv_flat)


def _scan_rows(x, reverse):
    n = x.shape[0]
    rows = lax.broadcasted_iota(jnp.int32, x.shape, 0)
    sh = 1
    while sh < n:
        if reverse:
            x = x + jnp.where(rows < n - sh, pltpu.roll(x, n - sh, 0), 0.0)
        else:
            x = x + jnp.where(rows >= sh, pltpu.roll(x, sh, 0), 0.0)
        sh *= 2
    return x


def _stack_heads(x, lane_head):
    return jnp.concatenate([jnp.where(lane_head == hh, x, 0.0) for hh in range(C_HEADS)], axis=0)


def _hgrn_kernel(z_ref, zc_ref, lbc_ref, *refs, ctx_out):
    if ctx_out:
        o_ref, oc_ref, st_ref, kx_ref, bx_ref, vx_ref, flag_ref = refs
    else:
        o_ref, st_ref, kx_ref, bx_ref, vx_ref, flag_ref = refs
        oc_ref = None
    n = C_WIDTH
    nb = HGRN_BLOCK
    nblk_c = zc_ref.shape[1] // nb
    nblk_l = z_ref.shape[1] // nb
    ones_bd = _head_ones(n, BF16)
    rows = lax.broadcasted_iota(jnp.int32, (nb, n), 0)
    lane_head = lax.broadcasted_iota(jnp.int32, (nb, n), 1) >> HEAD_SHIFT
    lane_head64 = lax.broadcasted_iota(jnp.int32, (HEAD_DIM, n), 1) >> HEAD_SHIFT
    low_half = (lax.broadcasted_iota(jnp.int32, (HEAD_DIM, 2 * HEAD_DIM), 1) < HEAD_DIM)
    sc_t = lax.broadcasted_iota(jnp.int32, (nb, C_HEADS * nb), 0)
    sc_s = lax.broadcasted_iota(jnp.int32, (nb, C_HEADS * nb), 1) & (nb - 1)

    def gates(z, d):
        one_m_lb = lbc_ref[d, 0:1, :]
        log1m_lb = lbc_ref[d, 1:2, :]
        log_lb = lbc_ref[d, 2:3, :]
        lb_pos = lbc_ref[d, 3:4, :] > 0.5
        soft = jnp.log(1.0 + jnp.exp(-jnp.abs(z)))
        log_rest = log1m_lb + (jnp.minimum(z, 0.0) - soft)
        lse = jnp.maximum(log_lb, log_rest) + jnp.log(1.0 + jnp.exp(-jnp.abs(log_lb - log_rest)))
        return jnp.where(lb_pos, lse, log_rest), one_m_lb * jnp.exp(jnp.minimum(-z, 0.0) - soft)

    def group_blocks(nblk, i):
        grp = HGRN_GROUP
        while nblk % grp:
            grp //= 2
        fwd = [i * grp + gg for gg in range(grp)]
        return grp, fwd, [nblk - 1 - blk for blk in fwd]

    def block_bound(src_ref, blk, d):
        zz = src_ref[0, pl.ds(pl.multiple_of(blk * nb, nb), nb), (1 + d) * n:(2 + d) * n]
        step_bound = jnp.minimum(lbc_ref[d, 4:5, :],
                                 jnp.maximum(-zz, 0.0) + (jnp.log(2.0) - lbc_ref[d, 1:2, :]))
        return jnp.sum(step_bound, axis=0, keepdims=True)

    def flag_groups(src_ref, nblk, base):
        def body(i, carry):
            _, fwd, bwd = group_blocks(nblk, i)
            worst = functools.reduce(jnp.maximum, [block_bound(src_ref, blk, d)
                                                   for d, blks in ((0, fwd), (1, bwd)) for blk in blks])
            flag_ref[base + i] = (jnp.max(worst) <= SAFE_DECAY).astype(jnp.int32)
            return carry
        lax.fori_loop(0, nblk // group_blocks(nblk, 0)[0], body, 0)

    def worst_bound(src_ref, nblk):
        def body(blk, worst):
            return jnp.maximum(worst, jnp.maximum(block_bound(src_ref, blk, 0), block_bound(src_ref, blk, 1)))
        return lax.fori_loop(0, nblk, body, jnp.zeros((1, n), F32))

    n_grp_c = nblk_c // group_blocks(nblk_c, 0)[0]
    all_safe = jnp.max(jnp.maximum(worst_bound(zc_ref, nblk_c), worst_bound(z_ref, nblk_l))) <= SAFE_DECAY
    st_ref[...] = jnp.zeros_like(st_ref)
    o_ref[...] = jnp.zeros_like(o_ref)
    if ctx_out:
        oc_ref[...] = jnp.zeros_like(oc_ref)

    def step(src_ref, dst_ref, blk, d, fast):
        reverse = d == 1
        r0 = pl.multiple_of(blk * nb, nb)
        v = src_ref[0, pl.ds(r0, nb), 3 * n:4 * n]
        log_f, k = gates(src_ref[0, pl.ds(r0, nb), (1 + d) * n:(2 + d) * n], d)
        bc = _scan_rows(log_f, reverse)
        edge = 0 if reverse else nb - 1
        b_edge = bc[edge:edge + 1, :]
        st = st_ref[d]
        v_bf = v.astype(BF16)

        if dst_ref is not None:
            q = jax.nn.silu(src_ref[0, pl.ds(r0, nb), 0:n])
            qt = (q * jnp.exp(bc)).astype(BF16)
            o = lax.dot_general(qt, _stack_heads(st, lane_head64).astype(BF16), (((1,), (1,)), ((), ())),
                                preferred_element_type=F32)

            def intra_fast():
                kt = _stack_heads(k * jnp.exp(-bc), lane_head).astype(BF16)
                sc = lax.dot_general(qt, kt, (((1,), (1,)), ((), ())), preferred_element_type=F32)
                keep = (sc_s >= sc_t) if reverse else (sc_s <= sc_t)
                sc = jnp.where(keep, sc, 0.0).astype(BF16)
                return jnp.dot(sc, _stack_heads(v, lane_head).astype(BF16), preferred_element_type=F32)

            def intra_exact():
                kx_ref[d] = k
                bx_ref[d] = bc
                vx_ref[d] = v

                def sbody(s, acc):
                    keep = (rows <= s) if reverse else (rows >= s)
                    e = jnp.exp(jnp.where(keep, bc - bx_ref[d, pl.ds(s, 1), :], 0.0))
                    p = jnp.where(keep, q * e * kx_ref[d, pl.ds(s, 1), :], 0.0)
                    sc = jnp.dot(p.astype(BF16), ones_bd, preferred_element_type=F32)
                    return acc + sc * vx_ref[d, pl.ds(s, 1), :]

                return lax.fori_loop(0, nb, sbody, jnp.zeros((nb, n), F32))

            o = o + (intra_fast() if fast else intra_exact())
            rows_o = dst_ref[0, pl.ds(r0, nb), :].astype(F32) + o
            dst_ref[0, pl.ds(r0, nb), :] = rows_o.astype(dst_ref.dtype)

        kd = (k * jnp.exp(b_edge - bc)).astype(BF16)
        full = lax.dot_general(v_bf, kd, (((0,), (0,)), ((), ())), preferred_element_type=F32)
        upd = jnp.concatenate(
            [jnp.where(low_half,
                       full[(2 * c) * HEAD_DIM:(2 * c + 1) * HEAD_DIM, 2 * c * HEAD_DIM:(2 * c + 2) * HEAD_DIM],
                       full[(2 * c + 1) * HEAD_DIM:(2 * c + 2) * HEAD_DIM, 2 * c * HEAD_DIM:(2 * c + 2) * HEAD_DIM])
             for c in range(C_HEADS // 2)], axis=1)
        st_ref[d] = st * jnp.exp(b_edge) + upd

    def run(src_ref, dst_ref, nblk, base, per_group):
        def body(i, carry):
            _, fwd, bwd = group_blocks(nblk, i)

            def group(fast):
                for bf, bb in zip(fwd, bwd):
                    step(src_ref, dst_ref, bf, 0, fast)
                    step(src_ref, dst_ref, bb, 1, fast)

            if per_group:
                safe = flag_ref[base + i]
                pl.when(safe > 0)(functools.partial(group, True))
                pl.when(safe <= 0)(functools.partial(group, False))
            else:
                group(True)
            return carry
        lax.fori_loop(0, nblk // group_blocks(nblk, 0)[0], body, 0)

    @pl.when(all_safe)
    def _():
        run(zc_ref, oc_ref, nblk_c, 0, False)
        run(z_ref, o_ref, nblk_l, n_grp_c, False)

    @pl.when(jnp.logical_not(all_safe))
    def _():
        flag_groups(zc_ref, nblk_c, 0)
        flag_groups(z_ref, nblk_l, n_grp_c)
        run(zc_ref, oc_ref, nblk_c, 0, True)
        run(z_ref, o_ref, nblk_l, n_grp_c, True)


def _hgrn(zc, zc_c, lbc, ctx_out):
    b, t, w = zc.shape
    lc = zc_c.shape[1]
    n = C_WIDTH
    row = lambda bb: (bb, 0, 0)
    out_shape = [jax.ShapeDtypeStruct((b, t, n), BF16)]
    out_specs = [pl.BlockSpec((1, t, n), row)]
    if ctx_out:
        out_shape.append(jax.ShapeDtypeStruct((b, lc, n), BF16))
        out_specs.append(pl.BlockSpec((1, lc, n), row))
    res = pl.pallas_call(
        functools.partial(_hgrn_kernel, ctx_out=ctx_out),
        out_shape=out_shape,
        grid=(b,),
        in_specs=[pl.BlockSpec((1, t, w), row),
                  pl.BlockSpec((1, lc, w), row),
                  pl.BlockSpec((2, 8, n), lambda bb: (0, 0, 0))],
        out_specs=out_specs,
        scratch_shapes=[pltpu.VMEM((2, HEAD_DIM, n), F32)]
                       + [pltpu.VMEM((2, HGRN_BLOCK, n), F32)] * 3
                       + [pltpu.SMEM(((t + lc) // HGRN_BLOCK,), jnp.int32)],
        compiler_params=_cparams("arbitrary"),
        name="hgrn",
    )(zc, zc_c, lbc)
    return (res[0], res[1]) if ctx_out else (res[0], None)


def _select_experts(lg):
    lane = lax.broadcasted_iota(jnp.int32, lg.shape, 1)
    lane_f = lane.astype(F32)
    neg = -jnp.inf
    gl = jnp.where(lane < N_GROUPS, lg, neg)
    gmax = jnp.max(gl, axis=1, keepdims=True)
    grp = jnp.min(jnp.where(gl == gmax, lane_f, float(ROUTER_LANES)), axis=1, keepdims=True).astype(jnp.int32)
    p_grp = 1.0 / jnp.sum(jnp.exp(gl - gmax), axis=1, keepdims=True)
    in_grp = (lane >= N_GROUPS) & (lane < N_GROUPS + N_EXPERTS) & (((lane - N_GROUPS) >> GROUP_SHIFT) == grp)
    el = jnp.where(in_grp, lg, neg)
    v1 = jnp.max(el, axis=1, keepdims=True)
    i1 = jnp.min(jnp.where(el == v1, lane_f, float(ROUTER_LANES)), axis=1, keepdims=True)
    el2 = jnp.where(lane_f == i1, neg, el)
    v2 = jnp.max(el2, axis=1, keepdims=True)
    i2 = jnp.min(jnp.where(el2 == v2, lane_f, float(ROUTER_LANES)), axis=1, keepdims=True)
    rr = jnp.exp(v2 - v1)
    g1 = p_grp / (1.0 + rr)
    g2 = p_grp * rr / (1.0 + rr)
    sel = jnp.where(lane == 0, i1, jnp.where(lane == 1, i2, jnp.where(lane == 2, g1, jnp.where(lane == 3, g2, 0.0))))
    counts = jnp.sum(((lane_f == i1) | (lane_f == i2)).astype(F32), axis=0, keepdims=True)
    return sel, counts


def _outproj_kernel(x_ref, ya_ref, yb_ref, o_ref, g_ref, w_ref, gate_ref, mul_ref, add_ref, hw_ref,
                    wr_ref, br_ref, xo_ref, h2_ref, sel_ref, cnt_ref):
    yc = _head_rms(o_ref[0].astype(F32), _head_ones(C_WIDTH, BF16)) * hw_ref[...] * jax.nn.silu(g_ref[0].astype(F32))
    y = jnp.dot(ya_ref[0], w_ref[0:A_WIDTH, :], preferred_element_type=F32)
    y = y + jnp.dot(yb_ref[0], w_ref[A_WIDTH:A_WIDTH + B_WIDTH, :], preferred_element_type=F32)
    y = y + jnp.dot(yc.astype(BF16), w_ref[A_WIDTH + B_WIDTH:, :], preferred_element_type=F32)
    xn = x_ref[0] + gate_ref[0] * y
    xo_ref[0] = xn
    ms = jnp.mean(xn * xn, axis=-1, keepdims=True)
    h2 = xn * lax.rsqrt(ms + EPS) * mul_ref[0] + add_ref[0]
    _pack_planes(h2, h2_ref, (0,))
    h_hi = h2.astype(BF16)
    h_lo = (h2 - h_hi.astype(F32)).astype(BF16)
    both = jnp.dot(h_hi, wr_ref[...], preferred_element_type=F32)
    lg = (both[:, :ROUTER_LANES] + both[:, ROUTER_LANES:] + br_ref[...]
          + jnp.dot(h_lo, wr_ref[:, 0:ROUTER_LANES], preferred_element_type=F32))
    sel, counts = _select_experts(lg)
    sel_ref[0] = sel

    @pl.when((pl.program_id(0) == 0) & (pl.program_id(1) == 0))
    def _():
        cnt_ref[...] = jnp.zeros_like(cnt_ref)
    cnt_ref[...] += counts


def _outproj(x, ya, yb, o, g, w_bf, gate, mul, add, hw, wr, br):
    b, t, d = x.shape
    tm = min(PROJ_ROWS, t)
    row = lambda bb, i: (bb, i, 0)
    vec = lambda bb, i: (bb, 0, 0)
    const = lambda bb, i: (0, 0)
    return pl.pallas_call(
        _outproj_kernel,
        out_shape=[jax.ShapeDtypeStruct((b, t, d), F32),
                   jax.ShapeDtypeStruct((PLANES, b, t, SC_ROW), F32),
                   jax.ShapeDtypeStruct((b, t, ROUTER_LANES), F32),
                   jax.ShapeDtypeStruct((8, ROUTER_LANES), F32)],
        grid=(b, t // tm),
        in_specs=[pl.BlockSpec((1, tm, d), row),
                  pl.BlockSpec((1, tm, A_WIDTH), row),
                  pl.BlockSpec((1, tm, B_WIDTH), row),
                  pl.BlockSpec((1, tm, C_WIDTH), row),
                  pl.BlockSpec((1, tm, C_WIDTH), row),
                  pl.BlockSpec((d, d), const),
                  pl.BlockSpec((1, 1, d), vec),
                  pl.BlockSpec((1, 1, d), vec),
                  pl.BlockSpec((1, 1, d), vec),
                  pl.BlockSpec((1, C_WIDTH), const),
                  pl.BlockSpec((d, 2 * ROUTER_LANES), const),
                  pl.BlockSpec((1, ROUTER_LANES), const)],
        out_specs=[pl.BlockSpec((1, tm, d), row),
                   pl.BlockSpec((PLANES, 1, tm, SC_ROW), lambda bb, i: (0, bb, i, 0)),
                   pl.BlockSpec((1, tm, ROUTER_LANES), row),
                   pl.BlockSpec((8, ROUTER_LANES), const)],
        compiler_params=_cparams("arbitrary", "arbitrary"),
        name="outproj",
    )(x, ya, yb, o, g, w_bf, gate, mul, add, hw, wr, br)


def _route_kernel(sel_ref, cnt_ref, info_ref, meta_ref, base_ref):
    i = pl.program_id(0)
    tm = sel_ref.shape[0]
    lane = lax.broadcasted_iota(jnp.int32, (tm, ROUTER_LANES), 1)
    lane_f = lane.astype(F32)
    sel = sel_ref[...]
    hit1 = lane_f == sel[:, 0:1]
    hit2 = lane_f == sel[:, 1:2]
    onehot = (hit1 | hit2).astype(F32)

    @pl.when(i == 0)
    def _():
        counts = cnt_ref[...]
        padded = jnp.floor((counts + (MOE_ROWS - 1.0)) * (1.0 / MOE_ROWS)) * MOE_ROWS
        r = lax.broadcasted_iota(jnp.int32, (ROUTER_LANES, ROUTER_LANES), 0)
        c = lax.broadcasted_iota(jnp.int32, (ROUTER_LANES, ROUTER_LANES), 1)
        ends = jnp.dot(padded, (r <= c).astype(F32), preferred_element_type=F32,
                       precision=lax.Precision.HIGHEST)
        base_ref[...] = (ends - padded)[0:1]
        row = lax.broadcasted_iota(jnp.int32, (8, ROUTER_LANES), 0)
        meta_ref[...] = jnp.where(row == 0, counts, jnp.where(row == 1, ends - padded, ends))

    tr = lax.broadcasted_iota(jnp.int32, (tm, tm), 0)
    tc = lax.broadcasted_iota(jnp.int32, (tm, tm), 1)
    before = jnp.dot((tc < tr).astype(BF16), onehot.astype(BF16), preferred_element_type=F32)
    pos = base_ref[...] + before
    d1 = jnp.sum(jnp.where(hit1, pos, 0.0), axis=1, keepdims=True)
    d2 = jnp.sum(jnp.where(hit2, pos, 0.0), axis=1, keepdims=True)
    base_ref[...] += jnp.sum(onehot, axis=0, keepdims=True)
    info_ref[...] = jnp.where(lane == 0, d1, jnp.where(lane == 1, d2, sel))


def _route(sel, counts):
    n = sel.shape[0]
    tm = ROUTE_ROWS if n % ROUTE_ROWS == 0 else ROUTE_ROWS // 2
    return pl.pallas_call(
        _route_kernel,
        out_shape=[jax.ShapeDtypeStruct((n, ROUTER_LANES), F32),
                   jax.ShapeDtypeStruct((8, ROUTER_LANES), F32)],
        grid=(n // tm,),
        in_specs=[pl.BlockSpec((tm, ROUTER_LANES), lambda i: (i, 0)),
                  pl.BlockSpec((8, ROUTER_LANES), lambda i: (0, 0))],
        out_specs=[pl.BlockSpec((tm, ROUTER_LANES), lambda i: (i, 0)),
                   pl.BlockSpec((8, ROUTER_LANES), lambda i: (0, 0))],
        scratch_shapes=[pltpu.VMEM((1, ROUTER_LANES), F32)],
        compiler_params=_cparams("arbitrary"),
        name="route",
    )(sel, counts)


def _sc_mesh():
    return plsc.VectorSubcoreMesh(core_axis_name="c", subcore_axis_name="s")


def _sc_gather(table, idx):
    n = idx.shape[0]
    d = table.shape[1]

    @functools.partial(pl.kernel, out_type=jax.ShapeDtypeStruct((n, d), table.dtype), mesh=_sc_mesh())
    def gather(x_hbm, i_hbm, o_hbm):
        def body(i_vmem, o_vmem):
            pltpu.sync_copy(x_hbm.at[i_vmem.at[0]], o_vmem)

        pltpu.emit_pipeline(
            body,
            grid=(n // SC_WINDOW,),
            in_specs=[pl.BlockSpec((1, SC_WINDOW), lambda i: (0, i))],
            out_specs=[pl.BlockSpec((SC_WINDOW, d), lambda i: (i, 0))],
            core_axis_name=("c", "s"),
            dimension_semantics=(pltpu.PARALLEL,),
        )(i_hbm, o_hbm)

    return gather(table, idx.reshape(1, n))


def _sc_scatter2(rows, idx0, idx1, n_out):
    m, d = rows.shape

    @functools.partial(pl.kernel, out_type=jax.ShapeDtypeStruct((n_out, d), rows.dtype), mesh=_sc_mesh())
    def scatter(x_hbm, i0_hbm, i1_hbm, o_hbm):
        def body(x_vmem, i0_vmem, i1_vmem):
            pltpu.sync_copy(x_vmem, o_hbm.at[i0_vmem.at[0]])
            pltpu.sync_copy(x_vmem, o_hbm.at[i1_vmem.at[0]])

        pltpu.emit_pipeline(
            body,
            grid=(m // SC_WINDOW,),
            in_specs=[pl.BlockSpec((SC_WINDOW, d), lambda i: (i, 0)),
                      pl.BlockSpec((1, SC_WINDOW), lambda i: (0, i)),
                      pl.BlockSpec((1, SC_WINDOW), lambda i: (0, i))],
            out_specs=[],
            core_axis_name=("c", "s"),
            dimension_semantics=(pltpu.PARALLEL,),
        )(x_hbm, i0_hbm, i1_hbm)

    return scatter(rows, idx0.reshape(1, m), idx1.reshape(1, m))


def _moe_kernel(be_ref, nu_ref, first_ref, slot_ref, nxt_ref, x_ref, w1_hbm, w3_hbm, w2_hbm, o_ref,
                w1f, w3f, w2f, w1b, w3b, w2b, sem, *, layer):
    i = pl.program_id(0)

    def fetch(e, slot):
        return [pltpu.make_async_copy(src.at[layer, e], dst.at[slot], sem.at[n, slot])
                for n, (src, dst) in enumerate(((w1_hbm, w1f), (w3_hbm, w3f), (w2_hbm, w2f)))]

    @pl.when((i == 0) & (nu_ref[0] > 0))
    def _():
        for cp in fetch(be_ref[0], 0):
            cp.start()

    @pl.when((first_ref[i] > 0) & (i < nu_ref[0]))
    def _():
        slot = slot_ref[i]

        @pl.when(nxt_ref[i] >= 0)
        def _():
            for cp in fetch(nxt_ref[i], 1 - slot):
                cp.start()

        for cp in fetch(be_ref[i], slot):
            cp.wait()
        w1b[...] = w1f[slot].astype(BF16)
        w3b[...] = w3f[slot].astype(BF16)
        w2b[...] = w2f[slot].astype(BF16)

    @pl.when(i < nu_ref[0])
    def _():
        parts = [h.astype(BF16) for p in range(PLANES) for h in _unpack_rows(x_ref[p])]
        a = sum(jnp.dot(h, w1b[q * SC_ROW:(q + 1) * SC_ROW, :], preferred_element_type=F32)
                for q, h in enumerate(parts))
        b = sum(jnp.dot(h, w3b[q * SC_ROW:(q + 1) * SC_ROW, :], preferred_element_type=F32)
                for q, h in enumerate(parts))
        hmid = (jax.nn.silu(a) * b).astype(BF16)
        _pack_planes(jnp.dot(hmid, w2b[...], preferred_element_type=F32), o_ref)

    @pl.when(i >= nu_ref[0])
    def _():
        o_ref[...] = jnp.zeros_like(o_ref)


def _moe_mlp(blk_expert, n_used, xs, w1, w3, w2, layer):
    n_rows = xs.shape[1]
    d, f = w1.shape[2:]
    nblk = n_rows // MOE_ROWS
    rows = lambda i, *_: (0, i, 0)
    idx = jnp.arange(nblk, dtype=jnp.int32)
    first = (idx < n_used[0]) & ((idx == 0) | (blk_expert != jnp.roll(blk_expert, 1)))
    slot = (jnp.cumsum(first.astype(jnp.int32)) - 1) % 2
    nxt_first = lax.cummin(jnp.where(first, idx, nblk)[::-1])[::-1]
    nxt_first = jnp.concatenate([nxt_first[1:], jnp.full((1,), nblk, jnp.int32)])
    nxt = jnp.where(nxt_first < nblk, blk_expert[jnp.minimum(nxt_first, nblk - 1)], -1)
    hbm = pl.BlockSpec(memory_space=pl.ANY)
    return pl.pallas_call(
        functools.partial(_moe_kernel, layer=layer),
        out_shape=jax.ShapeDtypeStruct((PLANES, n_rows, SC_ROW), F32),
        grid_spec=pltpu.PrefetchScalarGridSpec(
            num_scalar_prefetch=5,
            grid=(nblk,),
            in_specs=[pl.BlockSpec((PLANES, MOE_ROWS, SC_ROW), rows), hbm, hbm, hbm],
            out_specs=pl.BlockSpec((PLANES, MOE_ROWS, SC_ROW), rows),
            scratch_shapes=[pltpu.VMEM((2, d, f), F32), pltpu.VMEM((2, d, f), F32), pltpu.VMEM((2, f, d), F32),
                            pltpu.VMEM((d, f), BF16), pltpu.VMEM((d, f), BF16), pltpu.VMEM((f, d), BF16),
                            pltpu.SemaphoreType.DMA((3, 2))]),
        compiler_params=_cparams("arbitrary"),
        name="moe",
    )(blk_expert, n_used, first.astype(jnp.int32), slot.astype(jnp.int32), nxt.astype(jnp.int32), xs, w1, w3, w2)


def _combine_kernel(x_ref, pk_ref, info_ref, g_ref, o_ref):
    o_ref[0] = _combined(x_ref, pk_ref, info_ref, g_ref)


def _combine(x, picked, info, gate, row0):
    b, t, d = x.shape
    tm = min(256, t)
    off = row0 // tm
    tok = lambda bb, i: (off + bb * (t // tm) + i, 0)
    tok3 = lambda bb, i: (0, off + bb * (t // tm) + i, 0)
    return pl.pallas_call(
        _combine_kernel,
        out_shape=jax.ShapeDtypeStruct((b, t, d), F32),
        grid=(b, t // tm),
        in_specs=[pl.BlockSpec((1, tm, d), lambda bb, i: (bb, i, 0)),
                  pl.BlockSpec((PLANES * TOP_K, tm, SC_ROW), tok3),
                  pl.BlockSpec((tm, ROUTER_LANES), tok),
                  pl.BlockSpec((1, 1, d), lambda bb, i: (bb, 0, 0))],
        out_specs=pl.BlockSpec((1, tm, d), lambda bb, i: (bb, i, 0)),
        compiler_params=_cparams("arbitrary", "arbitrary"),
        name="combine",
    )(x, picked, info, gate)


def _hier_moe(h2p, sel, counts, w1, w3, w2, layer):
    n_tok = h2p.shape[1]
    info, meta = _route(sel, counts)
    dest = info[:, 0:TOP_K].astype(jnp.int32)
    pad_ends = meta[2, N_GROUPS:N_GROUPS + N_EXPERTS].astype(jnp.int32)
    nblk = -(-(n_tok * TOP_K) // MOE_ROWS) + N_EXPERTS
    n_rows = nblk * MOE_ROWS
    blk_start = jnp.arange(nblk, dtype=jnp.int32) * MOE_ROWS
    blk_expert = jnp.minimum(jnp.sum((pad_ends[None, :] <= blk_start[:, None]).astype(jnp.int32), axis=1),
                             N_EXPERTS - 1)
    n_used = pad_ends[-1:] // MOE_ROWS
    slot = [jnp.concatenate([p * n_rows + dest[:, s] for p in range(PLANES)]) for s in range(TOP_K)]
    xs = _sc_scatter2(h2p.reshape(PLANES * n_tok, SC_ROW), slot[0], slot[1], PLANES * n_rows)
    out = _moe_mlp(blk_expert, n_used, xs.reshape(PLANES, n_rows, SC_ROW), w1, w3, w2, layer)
    idx_all = jnp.concatenate([p * n_rows + dest[:, s] for p in range(PLANES) for s in range(TOP_K)])
    picked = _sc_gather(out.reshape(PLANES * n_rows, SC_ROW), idx_all)
    return picked.reshape(PLANES * TOP_K, n_tok, SC_ROW), info


def _layer(layer, x, xc, pend, mod, lb, norm1_w, w_in, w_s, b_s, q_norm_w, k_norm_w, hgrn_norm_w, w_out,
           norm2_w, w_grp, b_grp, w_exp, b_exp, w1, w3, w2, ctx_out):
    b, t, d = x.shape
    lc = xc.shape[1]
    sh1, sc1, g1, sh2, sc2, g2 = [m[:, None, :] for m in jnp.split(mod[:b], 6, axis=-1)]
    mod_c = [jnp.broadcast_to(m[None, None, :], (b, 1, d)) for m in jnp.split(mod[b], 6)]

    w_in_bf = w_in.astype(BF16)
    scale = LOG2E * HEAD_DIM ** -0.5
    q_tabs = _rope_tables(t, q_norm_w, scale, B_KV_WIDTH, True)
    k_tabs = _rope_tables(t, k_norm_w, 1.0, B_KV_WIDTH, True)
    qc_tabs = _rope_tables(lc, q_norm_w, scale, B_KV_WIDTH, False)
    kc_tabs = _rope_tables(lc, k_norm_w, 1.0, B_KV_WIDTH, False)
    ws_bf = w_s.astype(BF16)
    bias2d = jnp.repeat(b_s.T, HEAD_DIM, axis=1)
    res = _inproj(x, norm1_w * (1.0 + sc1), sh1, w_in_bf, q_tabs, k_tabs, ws_bf, bias2d,
                  None if pend is None else pend[:2] + (pend[2], 0))
    res_c = _inproj(xc, norm1_w * (1.0 + mod_c[1]), mod_c[0], w_in_bf, qc_tabs, kc_tabs, ws_bf, bias2d,
                    None if pend is None else pend[:2] + (pend[3], b * t))
    if pend is not None:
        x, xc, res, res_c = res[0], res_c[0], res[1:], res_c[1:]
    ya, q, k, v, zc, g = res
    ya_c, q_c, k_c, v_c, zc_c, g_c = res_c

    bound = LOG2E * HEAD_DIM ** 0.5 * jnp.max(jnp.abs(q_norm_w)) * jnp.max(jnp.abs(k_norm_w)) * 1.02
    shift = jnp.ceil(bound)
    attn_flag = jnp.stack([(shift <= SAFE_SHIFT).astype(jnp.int32), shift.astype(jnp.int32)])
    yb = _attn(attn_flag, q, [(k, v), (k_c, v_c)])

    pos = lb > 0.0
    log_lb = jnp.log(jnp.where(pos, lb, 1.0))
    lbc = jnp.stack([1.0 - lb, jnp.log1p(-lb), log_lb, pos.astype(F32), jnp.where(pos, -log_lb, 1e30)], axis=1)
    lbc = jnp.concatenate([lbc, jnp.zeros((2, 3, C_WIDTH), F32)], axis=1)
    o, o_c = _hgrn(zc, zc_c, lbc, ctx_out)

    w_out_bf = w_out.astype(BF16)
    hw = jnp.tile(hgrn_norm_w, C_HEADS)[None, :]
    wr = jnp.zeros((d, ROUTER_LANES), F32).at[:, :N_GROUPS].set(w_grp).at[
        :, N_GROUPS:N_GROUPS + N_EXPERTS].set(w_exp)
    wr_hi = wr.astype(BF16)
    wr = jnp.concatenate([wr_hi, (wr - wr_hi.astype(F32)).astype(BF16)], axis=1)
    br = jnp.zeros((1, ROUTER_LANES), F32).at[0, :N_GROUPS].set(b_grp).at[
        0, N_GROUPS:N_GROUPS + N_EXPERTS].set(b_exp)
    x, h2, sel, cnt = _outproj(x, ya, yb, o, g, w_out_bf, g1, norm2_w * (1.0 + sc2), sh2, hw, wr, br)
    if ctx_out:
        yb_c = _attn(attn_flag, q_c, [(k_c, v_c)])
        xc, h2c, sel_c, cnt_c = _outproj(xc, ya_c, yb_c, o_c, g_c, w_out_bf, mod_c[2],
                                norm2_w * (1.0 + mod_c[4]), mod_c[3], hw, wr, br)
        tokens = jnp.concatenate([h2.reshape(PLANES, -1, SC_ROW), h2c.reshape(PLANES, -1, SC_ROW)], axis=1)
        sel_all = jnp.concatenate([sel.reshape(-1, ROUTER_LANES), sel_c.reshape(-1, ROUTER_LANES)], axis=0)
        picked, info = _hier_moe(tokens, sel_all, cnt + cnt_c, w1, w3, w2, layer)
        return x, xc, (picked, info, g2, mod_c[5])
    picked, info = _hier_moe(h2.reshape(PLANES, -1, SC_ROW), sel.reshape(-1, ROUTER_LANES), cnt, w1, w3, w2, layer)
    return x, xc, (picked, info, g2, None)


def kernel(x, c, ctx, c_ctx, w_mod, b_mod, norm1_w, w_in, w_s, b_s, q_norm_w, k_norm_w, hgrn_lb_logits,
           hgrn_norm_w, w_out, norm2_w, w_grp, b_grp, w_exp, b_exp, w1, w3, w2):
    depth = w_mod.shape[0]
    lb_sm = jax.nn.softmax(hgrn_lb_logits.astype(F32), axis=0)
    lb = jnp.cumsum(lb_sm, axis=0) - lb_sm[0]
    xc = ctx
    pend = None
    b = x.shape[0]
    mods = _mod(jnp.zeros((MOD_ROWS, x.shape[2]), F32).at[:b].set(c).at[b].set(c_ctx), w_mod, b_mod)
    for l in range(depth):
        x, xc, pend = _layer(l, x, xc, pend, mods[l], lb[l], norm1_w[l], w_in[l], w_s[l], b_s[l],
                             q_norm_w[l], k_norm_w[l], hgrn_norm_w[l], w_out[l], norm2_w[l], w_grp[l], b_grp[l],
                             w_exp[l], b_exp[l], w1, w3, w2, ctx_out=(l < depth - 1))
    return _combine(x, pend[0], pend[1], pend[2], 0)
```
